```python
import math
import jax
import jax.numpy as jnp
from jax import lax
import numpy as np

D_MODEL = 2048
BATCH = 4
SEQ = 2048
DEPTH = 4

GRID_W = 64
CTX_LEN = 256
N_MIXERS = 4
REPEATS = DEPTH // N_MIXERS
DN_ALPHA = (2 * DEPTH) ** 0.25
DN_BETA = (8 * DEPTH) ** -0.25
LN_EPS = 1e-5
N_MOD = 6

D_FF = 5632
FFN_CONV = 3

RW_HEAD = 64
RW_HEADS = D_MODEL // RW_HEAD
RW_DECAY_LORA = 96
RW_ICLR_LORA = 64
RW_GATE_LORA = 256
RW_DECAY_SCALE = 0.606531
RW_GN_EPS = 64e-5

DA_HEAD = 128
DA_HEADS = D_MODEL // (2 * DA_HEAD)
DA_QBLOCK = 128
ROPE_BASE = 10000.0
ROPE_FREQS = DA_HEAD // 4

HG_EXPAND = 128
HG_HEADS = D_MODEL // HG_EXPAND
HG_HEAD_V = D_MODEL // HG_HEADS
HG_CHUNK = 64

LR_WIDTH = D_MODEL
LR_BLOCKS = 8
LR_BS = LR_WIDTH // LR_BLOCKS
LR_CONV = 4
LR_C = 8.0

kernel_name = 'hybrid_interleaved_diffusion_backbone'


def layer_norm(x, g, b):
    xf = x.astype(jnp.float32)
    mu = jnp.mean(xf, -1, keepdims=True)
    var = jnp.mean(jnp.square(xf - mu), -1, keepdims=True)
    return ((xf - mu) * lax.rsqrt(var + LN_EPS)).astype(x.dtype) * g + b


def rms_norm(x, g, eps):
    xf = x.astype(jnp.float32)
    return (xf * lax.rsqrt(jnp.mean(jnp.square(xf), -1, keepdims=True) + eps)).astype(x.dtype) * g


def split_apply(fn_ctx, fn_lat, z, n_ctx):
    if n_ctx == 0:
        return fn_lat(z)
    return jnp.concatenate([fn_ctx(z[:, :n_ctx]), fn_lat(z[:, n_ctx:])], axis=1)


def seg_flip(t, n_ctx):
    return jnp.concatenate([jnp.flip(t[:, :n_ctx], 1), jnp.flip(t[:, n_ctx:], 1)], axis=1)


def dwconv_centred(x, w, b):
    k_w, n = w.shape[0], x.shape[1]
    xp = jnp.pad(x, ((0, 0), ((k_w - 1) // 2, k_w // 2), (0, 0)))
    return b + sum(xp[:, j:j + n] * w[j] for j in range(k_w))


def centred_shift_delta(x):
    xp = jnp.pad(x, ((0, 0), (1, 1), (0, 0)))
    return 0.5 * (xp[:, :-2] + xp[:, 2:]) - x


def ada_modulate(z, n_ctx, m_ctx, m_lat, j):
    mod = lambda m: (lambda s: s * (1 + m[:, None, j + 1]) + m[:, None, j])
    return split_apply(mod(m_ctx), mod(m_lat), z, n_ctx)


def ada_gate(y, n_ctx, m_ctx, m_lat, j):
    gate = lambda m: (lambda s: s * m[:, None, j])
    return split_apply(gate(m_ctx), gate(m_lat), y, n_ctx)


def prefix_axial_rope(n_ctx, n_lat):
    n_rows = n_lat // GRID_W
    row = jnp.repeat(jnp.arange(n_rows, dtype=jnp.float32), GRID_W)
    col = jnp.tile(jnp.arange(GRID_W, dtype=jnp.float32), n_rows)
    inv_freq = ROPE_BASE ** (-jnp.arange(ROPE_FREQS, dtype=jnp.float32) / ROPE_FREQS)
    ang_r, ang_c = row[:, None] * inv_freq, col[:, None] * inv_freq
    ang = jnp.concatenate([ang_r, ang_r, ang_c, ang_c], axis=-1)
    ang = jnp.concatenate([jnp.zeros((n_ctx, DA_HEAD), jnp.float32), ang], axis=0)
    return jnp.cos(ang), jnp.sin(ang)


def rotate_quarters(t):
    qa, qb, qc, qd = jnp.split(t, 4, axis=-1)
    return jnp.concatenate([-qb, qa, -qd, qc], axis=-1)


def rwkv7_scan(r, w, k, v, kk, a):
    def step(S, inp):
        r_t, w_t, k_t, v_t, kk_t, a_t = inp
        sa = jnp.einsum('ghij,ghj->ghi', S, kk_t)
        S = (S * w_t[:, :, None, :] - sa[..., None] * (kk_t * a_t)[:, :, None, :]
             + v_t[..., None] * k_t[:, :, None, :])
        return S, jnp.einsum('ghij,ghj->ghi', S, r_t)
    S0 = jnp.zeros(r.shape[1:] + (r.shape[-1],), r.dtype)
    return lax.scan(step, S0, (r, w, k, v, kk, a))[1]


def gla_chunked(q, k, v, logf):
    g_, n, h_, dk = q.shape
    dv = v.shape[-1]
    nc = n // HG_CHUNK
    blk = lambda t: jnp.moveaxis(t.reshape(g_, nc, HG_CHUNK, h_, t.shape[-1]), 1, 0)
    q, k, v = blk(q), blk(k), blk(v)
    b = jnp.cumsum(blk(logf).astype(jnp.float32), axis=2)
    b_end = b[:, :, -1:]
    qd = q * jnp.exp(b)
    kd = k * jnp.exp(-b)
    ke = k * jnp.exp(b_end - b)
    causal = jnp.tril(jnp.ones((HG_CHUNK, HG_CHUNK), bool))
    att = jnp.where(causal, jnp.einsum('ngthk,ngshk->nghts', qd, kd), 0.0)
    o_intra = jnp.einsum('nghts,ngshv->ngthv', att, v.astype(att.dtype))

    def step(S, inp):
        qd_n, ke_n, v_n, dec_n = inp
        o_n = jnp.einsum('gthk,ghkv->gthv', qd_n, S)
        S = dec_n[..., None] * S + jnp.einsum('gshk,gshv->ghkv', ke_n, v_n)
        return S, o_n
    S0 = jnp.zeros((g_, h_, dk, dv), qd.dtype)
    _, o_inter = lax.scan(step, S0, (qd, ke, v.astype(qd.dtype), jnp.exp(b_end[:, :, 0])))
    o = o_intra + o_inter
    return jnp.moveaxis(o, 0, 1).reshape(g_, n, h_, dv)


def rwkv7_mixer(h, n_ctx, drop_ctx, mu, w_rkv, w0, w1, w2, a0, a1, a2, g1, g2, k_k, k_a, r_k,
                gn_g, gn_b, w_o):
    bsz, n, _ = h.shape
    dt = h.dtype
    dx = split_apply(centred_shift_delta, centred_shift_delta, h, n_ctx)
    xs = h[None] + dx[None] * mu[:, None, None, :]
    r, k, v = jnp.einsum('nbld,nde->nble', xs[:3], w_rkv)
    w_log = w0[:, None, None] + jnp.einsum('nblr,nrd->nbld', jnp.tanh(jnp.einsum('bld,ndr->nblr', xs[3], w1)), w2)
    decay = jnp.exp(-RW_DECAY_SCALE * jax.nn.sigmoid(w_log))
    iclr = jax.nn.sigmoid(a0[:, None, None] + jnp.einsum('nblr,nrd->nbld', jnp.einsum('bld,ndr->nblr', xs[4], a1), a2))
    g = jax.nn.sigmoid(xs[5] @ g1) @ g2
    kk = (k * k_k).reshape(bsz, n, RW_HEADS, RW_HEAD)
    kk = kk * lax.rsqrt(jnp.sum(jnp.square(kk.astype(jnp.float32)), -1, keepdims=True) + 1e-12).astype(kk.dtype)
    kk = kk.reshape(bsz, n, D_MODEL)
    k_dir = k[None] * (1 + (iclr - 1) * k_a)

    def dirs(t_f, t_b):
        t = jnp.concatenate([t_f, seg_flip(t_b, n_ctx)], axis=0).astype(dt)
        return jnp.moveaxis(t.reshape(t.shape[:2] + (RW_HEADS, RW_HEAD)), 1, 0)
    y = rwkv7_scan(dirs(r, r), dirs(decay[0], decay[1]), dirs(k_dir[0], k_dir[1]),
                   dirs(v, v), dirs(kk, kk), dirs(iclr[0], iclr[1]))
    y = jnp.moveaxis(y, 0, 1)
    y = y[:bsz] + seg_flip(y[bsz:], n_ctx)
    heads = lambda t: t.reshape(t.shape[:2] + (RW_HEADS, RW_HEAD))
    rh, vh = heads(r), heads(v)
    bonus = jnp.sum(rh * heads(k_dir[0] + k_dir[1]) * r_k, -1, keepdims=True) * vh
    if drop_ctx:
        y, bonus, g = y[:, n_ctx:], bonus[:, n_ctx:], g[:, n_ctx:]
    yf = y.astype(jnp.float32)
    mean = jnp.mean(yf, -1, keepdims=True)
    var = jnp.mean(jnp.square(yf - mean), -1, keepdims=True)
    yn = ((yf - mean) * lax.rsqrt(var + RW_GN_EPS)).astype(dt)
    yn = yn.reshape(yn.shape[:2] + (D_MODEL,)) * gn_g + gn_b
    out = (yn + bonus.reshape(yn.shape)) * g
    return out @ w_o


def diff_attention_mixer(h, n_ctx, drop_ctx, layer_idx, rope_cos, rope_sin, w_qkv, lam_vec, sub_g, w_o):
    bsz, n, _ = h.shape
    q, k, v = jnp.split(h @ w_qkv, 3, axis=-1)
    cos, sin = rope_cos[:, None].astype(h.dtype), rope_sin[:, None].astype(h.dtype)
    rope = lambda t: t * cos + rotate_quarters(t) * sin
    q = rope(q.reshape(bsz, n, 2 * DA_HEADS, DA_HEAD)).reshape(bsz, n, DA_HEADS, 2, DA_HEAD)
    k = rope(k.reshape(bsz, n, 2 * DA_HEADS, DA_HEAD)).reshape(bsz, n, DA_HEADS, 2, DA_HEAD)
    v = v.reshape(bsz, n, DA_HEADS, 2 * DA_HEAD)
    lam_init = 0.8 - 0.6 * math.exp(-0.3 * layer_idx)
    lv = lam_vec.astype(jnp.float32)
    lam = jnp.exp(jnp.sum(lv[0] * lv[1])) - jnp.exp(jnp.sum(lv[2] * lv[3])) + lam_init

    def attend(qb, kb, vb):
        s = jnp.einsum('bqhmd,bkhmd->bhmqk', qb, kb).astype(jnp.float32) * (DA_HEAD ** -0.5)
        p = jax.nn.softmax(s, axis=-1)
        p = p[:, :, 0] - lam * p[:, :, 1]
        return jnp.einsum('bhqk,bkhe->bqhe', p.astype(vb.dtype), vb)

    q_lat = q[:, n_ctx:]
    n_blk = q_lat.shape[1] // DA_QBLOCK
    qb = jnp.moveaxis(q_lat.reshape(bsz, n_blk, DA_QBLOCK, DA_HEADS, 2, DA_HEAD), 1, 0)
    o = lax.map(lambda blk: attend(blk, k, v), qb)
    o = jnp.moveaxis(o, 0, 1).reshape(bsz, n - n_ctx, DA_HEADS, 2 * DA_HEAD)
    if not drop_ctx:
        o = jnp.concatenate([attend(q[:, :n_ctx], k[:, :n_ctx], v[:, :n_ctx]), o], axis=1)
    o = rms_norm(o, sub_g, 1e-5) * (1 - lam_init)
    return o.reshape(o.shape[:2] + (D_MODEL,)) @ w_o


def hgrn2_mixer(h, n_ctx, drop_ctx, layer_idx, w_in, lower, norm_g, w_o):
    bsz, n, _ = h.shape
    q, i_in, g_out, f_fwd, f_bwd = jnp.split(h @ w_in, 5, axis=-1)
    lb = jnp.cumsum(jax.nn.softmax(lower.astype(jnp.float32), axis=1), axis=1)
    lb = (lb - lb[:, :1])[:, layer_idx]
    f = lb[:, None, None] + (1.0 - lb[:, None, None]) * jax.nn.sigmoid(jnp.stack([f_fwd, f_bwd]).astype(jnp.float32))
    heads_k = lambda t: t.reshape(t.shape[:-1] + (HG_HEADS, HG_EXPAND))
    heads_v = lambda t: t.reshape(t.shape[:-1] + (HG_HEADS, HG_HEAD_V))
    both = lambda t_f, t_b: jnp.concatenate([t_f, seg_flip(t_b, n_ctx)], axis=0)
    qh, vh, fh = heads_k(jax.nn.silu(q)), heads_v(i_in), heads_k(f)
    o = gla_chunked(both(qh, qh), both(1.0 - fh[0], 1.0 - fh[1]), both(vh, vh),
                    both(jnp.log(fh[0]), jnp.log(fh[1])))
    o = o[:bsz] + seg_flip(o[bsz:], n_ctx)
    gh = heads_v(g_out)
    if drop_ctx:
        o, gh = o[:, n_ctx:], gh[:, n_ctx:]
    o = rms_norm(o, norm_g, 1e-5).astype(h.dtype) * jax.nn.silu(gh)
    return o.reshape(o.shape[:2] + (D_MODEL,)) @ w_o


def rglru_mixer(h, n_ctx, drop_ctx, w_in, conv_w, conv_b, w_gate, b_gate, lam, w_o):
    bsz, n, _ = h.shape
    gate_branch, xb = jnp.split(h @ w_in, 2, axis=-1)
    conv = lambda s: dwconv_centred(s, conv_w, conv_b)
    xb = split_apply(conv, conv, xb, n_ctx)
    gates = jnp.einsum('blni,dgnij->dgblnj', xb.reshape(bsz, n, LR_BLOCKS, LR_BS), w_gate)
    gates = jax.nn.sigmoid(gates.reshape(2, 2, bsz, n, LR_WIDTH) + b_gate[:, :, None, None])
    rec_gate, in_gate = gates[:, 0], gates[:, 1]
    log_a = -LR_C * rec_gate * jax.nn.softplus(-lam)[:, None, None]
    a = jnp.exp(log_a)
    u = jnp.sqrt(-jnp.expm1(2.0 * log_a)) * in_gate * xb[None]
    both = lambda t: jnp.concatenate([t[0], seg_flip(t[1], n_ctx)], axis=0)
    combine = lambda p, q: (p[0] * q[0], q[0] * p[1] + q[1])
    _, hs = lax.associative_scan(combine, (both(a), both(u)), axis=1)
    y = hs[:bsz] + seg_flip(hs[bsz:], n_ctx)
    if drop_ctx:
        y, gate_branch = y[:, n_ctx:], gate_branch[:, n_ctx:]
    return (y * jax.nn.gelu(gate_branch)) @ w_o


def conv_ffn(h, n_ctx, w_up, conv_w, conv_b, w_down):
    conv = lambda s: dwconv_centred(s, conv_w, conv_b)
    u = split_apply(conv, conv, h @ w_up, n_ctx)
    gate, val = jnp.split(u, 2, axis=-1)
    return (jax.nn.silu(gate) * val) @ w_down


def setup_inputs(seed: int = 0) -> dict:
    key = jax.random.key(seed)
    ks = iter(jax.random.split(key, 48))
    f32 = jnp.float32
    D, R, W = D_MODEL, REPEATS, LR_WIDTH

    def nrm(shape, scale):
        return scale * jax.random.normal(next(ks), shape, f32)

    inp = {}
    inp['x'] = nrm((BATCH, SEQ, D), 1.0)
    inp['c'] = nrm((BATCH, D), 1.0)
    inp['ctx'] = nrm((BATCH, CTX_LEN, D), 1.0)
    inp['c_ctx'] = nrm((D,), 1.0)
    inp['ada_w'] = nrm((DEPTH, D, N_MOD * D), 0.5 * D ** -0.5)
    inp['ada_b'] = nrm((DEPTH, N_MOD * D), 0.02)
    inp['ln_g'] = 1.0 + nrm((DEPTH, 2, D), 0.02)
    inp['ln_b'] = nrm((DEPTH, 2, D), 0.02)
    inp['ffn_w_up'] = nrm((DEPTH, D, 2 * D_FF), D ** -0.5)
    inp['ffn_conv_w'] = nrm((DEPTH, FFN_CONV, 2 * D_FF), FFN_CONV ** -0.5)
    inp['ffn_conv_b'] = nrm((DEPTH, 2 * D_FF), 0.02)
    inp['ffn_w_down'] = nrm((DEPTH, D_FF, D), DN_BETA * D_FF ** -0.5)
    inp['rw_mu'] = jax.random.uniform(next(ks), (R, 6, D), f32)
    inp['rw_w_rkv'] = nrm((R, 3, D, D), D ** -0.5)
    inp['rw_w0'] = nrm((R, 2, D), 0.5)
    inp['rw_w1'] = nrm((R, 2, D, RW_DECAY_LORA), D ** -0.5)
    inp['rw_w2'] = nrm((R, 2, RW_DECAY_LORA, D), RW_DECAY_LORA ** -0.5)
    inp['rw_a0'] = nrm((R, 2, D), 0.5)
    inp['rw_a1'] = nrm((R, 2, D, RW_ICLR_LORA), D ** -0.5)
    inp['rw_a2'] = nrm((R, 2, RW_ICLR_LORA, D), RW_ICLR_LORA ** -0.5)
    inp['rw_g1'] = nrm((R, D, RW_GATE_LORA), D ** -0.5)
    inp['rw_g2'] = nrm((R, RW_GATE_LORA, D), RW_GATE_LORA ** -0.5)
    inp['rw_k_k'] = 1.0 + nrm((R, D), 0.1)
    inp['rw_k_a'] = 1.0 + nrm((R, D), 0.1)
    inp['rw_r_k'] = nrm((R, RW_HEADS, RW_HEAD), 0.1)
    inp['rw_gn_g'] = 1.0 + nrm((R, D), 0.02)
    inp['rw_gn_b'] = nrm((R, D), 0.02)
    inp['rw_w_o'] = nrm((R, D, D), DN_BETA * D ** -0.5)
    inp['da_w_qkv'] = nrm((R, D, 3 * D), D ** -0.5)
    inp['da_lambda'] = nrm((R, 4, DA_HEAD), 0.1)
    inp['da_sub_g'] = 1.0 + nrm((R, 2 * DA_HEAD), 0.02)
    inp['da_w_o'] = nrm((R, D, D), DN_BETA * D ** -0.5)
    inp['hg_w_in'] = nrm((R, D, 5 * D), D ** -0.5)
    inp['hg_lower'] = nrm((2, DEPTH, D), 0.1)
    inp['hg_norm_g'] = 1.0 + nrm((R, HG_HEAD_V), 0.02)
    inp['hg_w_o'] = nrm((R, D, D), DN_BETA * D ** -0.5)
    inp['lr_w_in'] = nrm((R, D, 2 * W), D ** -0.5)
    inp['lr_conv_w'] = nrm((R, LR_CONV, W), LR_CONV ** -0.5)
    inp['lr_conv_b'] = nrm((R, W), 0.02)
    inp['lr_w_gate'] = nrm((R, 2, 2, LR_BLOCKS, LR_BS, LR_BS), LR_BS ** -0.5)
    inp['lr_b_gate'] = nrm((R, 2, 2, W), 0.02)
    a_init = jax.random.uniform(next(ks), (R, 2, W), f32, 0.9, 0.999)
    inp['lr_lambda'] = jnp.log(a_init) - jnp.log1p(-a_init)
    inp['lr_w_o'] = nrm((R, W, D), DN_BETA * W ** -0.5)
    return inp


def reference(x, c, ctx, c_ctx, ada_w, ada_b, ln_g, ln_b, ffn_w_up, ffn_conv_w, ffn_conv_b, ffn_w_down,
              rw_mu, rw_w_rkv, rw_w0, rw_w1, rw_w2, rw_a0, rw_a1, rw_a2, rw_g1, rw_g2, rw_k_k, rw_k_a,
              rw_r_k, rw_gn_g, rw_gn_b, rw_w_o, da_w_qkv, da_lambda, da_sub_g, da_w_o,
              hg_w_in, hg_lower, hg_norm_g, hg_w_o, lr_w_in, lr_conv_w, lr_conv_b, lr_w_gate, lr_b_gate,
              lr_lambda, lr_w_o):
    n_ctx = ctx.shape[1]
    rope_cos, rope_sin = prefix_axial_rope(n_ctx, x.shape[1])
    z = jnp.concatenate([ctx, x], axis=1)
    silu_c = jax.nn.silu(c)
    silu_cc = jax.nn.silu(c_ctx)[None]
    for i in range(DEPTH):
        rep, kind, last = i // N_MIXERS, i % N_MIXERS, i == DEPTH - 1
        m_lat = (silu_c @ ada_w[i] + ada_b[i]).reshape(-1, N_MOD, D_MODEL)
        m_ctx = (silu_cc @ ada_w[i] + ada_b[i]).reshape(1, N_MOD, D_MODEL)
        h = ada_modulate(z, n_ctx, m_ctx, m_lat, 0)
        if kind == 0:
            y = rwkv7_mixer(h, n_ctx, last, rw_mu[rep], rw_w_rkv[rep], rw_w0[rep], rw_w1[rep], rw_w2[rep],
                            rw_a0[rep], rw_a1[rep], rw_a2[rep], rw_g1[rep], rw_g2[rep], rw_k_k[rep],
                            rw_k_a[rep], rw_r_k[rep], rw_gn_g[rep], rw_gn_b[rep], rw_w_o[rep])
        elif kind == 1:
            y = diff_attention_mixer(h, n_ctx, last, i, rope_cos, rope_sin, da_w_qkv[rep], da_lambda[rep],
                                     da_sub_g[rep], da_w_o[rep])
        elif kind == 2:
            y = hgrn2_mixer(h, n_ctx, last, i, hg_w_in[rep], hg_lower, hg_norm_g[rep], hg_w_o[rep])
        else:
            y = rglru_mixer(h, n_ctx, last, lr_w_in[rep], lr_conv_w[rep], lr_conv_b[rep], lr_w_gate[rep],
                            lr_b_gate[rep], lr_lambda[rep], lr_w_o[rep])
        if last:
            z, n_ctx = z[:, n_ctx:], 0
        z = layer_norm(DN_ALPHA * z + ada_gate(y, n_ctx, m_ctx, m_lat, 2), ln_g[i, 0], ln_b[i, 0])
        h = ada_modulate(z, n_ctx, m_ctx, m_lat, 3)
        y = conv_ffn(h, n_ctx, ffn_w_up[i], ffn_conv_w[i], ffn_conv_b[i], ffn_w_down[i])
        z = layer_norm(DN_ALPHA * z + ada_gate(y, n_ctx, m_ctx, m_lat, 5), ln_g[i, 1], ln_b[i, 1])
    return z[:, n_ctx:]
```

```python
import functools
import math

import jax
import jax.numpy as jnp
from jax import lax
from jax.experimental import pallas as pl
from jax.experimental.pallas import tpu as pltpu

F32, BF16 = jnp.float32, jnp.bfloat16
HIGHEST = lax.Precision.HIGHEST

LANES = 128
CHUNK = 64
LN_EPS = 1e-5
GRID_W = 64
ROPE_BASE = 10000.0
RW_HEAD = 64
RW_DECAY_SCALE = 0.606531
RW_GN_EPS = 64e-5
DA_HEAD = 128
HG_EXPAND = 128
LR_BS = 256
LR_C = 8.0
MIB = 1024 * 1024


def _pick(n, cands):
    for c in cands:
        if n % c == 0:
            return c
    return n


def _cparams(sem, vmem_mib):
    return pltpu.CompilerParams(dimension_semantics=sem, vmem_limit_bytes=vmem_mib * MIB)


def _sigmoid(x):
    return jax.nn.sigmoid(x)


def _dotf(a, b):
    return jnp.dot(a, b, preferred_element_type=F32, precision=HIGHEST)


def _dot_nt(a, b):
    return lax.dot_general(a, b, (((1,), (1,)), ((), ())), preferred_element_type=F32, precision=HIGHEST)


def _dot_tn(a, b):
    return lax.dot_general(a, b, (((0,), (0,)), ((), ())), preferred_element_type=F32, precision=HIGHEST)


def _mm_kernel(a_ref, w_ref, o_ref, acc_ref, *, nk):
    prod = jnp.dot(a_ref[...], w_ref[...], preferred_element_type=F32)
    if nk == 1:
        o_ref[...] = prod.astype(o_ref.dtype)
    else:
        k = pl.program_id(3)

        @pl.when(k == 0)
        def _():
            acc_ref[...] = prod

        @pl.when(k > 0)
        def _():
            acc_ref[...] += prod

        @pl.when(k == nk - 1)
        def _():
            o_ref[...] = acc_ref[...].astype(o_ref.dtype)


def _matmul(a, w, *, out_dtype=F32, a_off=0):
    g_n, k_n, n_n = w.shape
    m_n = a.shape[1]
    tm = _pick(m_n, (1024, 512, 256, 128, 64))
    tn = _pick(n_n, (1024, 512, 256, 128))
    tk = k_n if k_n <= 2048 else _pick(k_n, (2816, 2048, 1024, 512))
    nk = k_n // tk
    return pl.pallas_call(
        functools.partial(_mm_kernel, nk=nk),
        grid=(g_n, m_n // tm, n_n // tn, nk),
        in_specs=[pl.BlockSpec((None, tm, tk), lambda g, i, j, k: (g + a_off, i, k)),
                  pl.BlockSpec((None, tk, tn), lambda g, i, j, k: (g, k, j))],
        out_specs=pl.BlockSpec((None, tm, tn), lambda g, i, j, k: (g, i, j)),
        out_shape=jax.ShapeDtypeStruct((g_n, m_n, n_n), out_dtype),
        scratch_shapes=[pltpu.VMEM((tm, tn), F32)],
        compiler_params=_cparams(("parallel", "parallel", "parallel", "arbitrary"), 48),
        name="matmul",
    )(a, w)


def _mm2(a, w, **kw):
    return _matmul(a[None], w[None], **kw)[0]


def _ada_kernel(c_ref, w_ref, b_ref, o_ref):
    c = c_ref[...]
    s = (c * _sigmoid(c)).astype(BF16)
    o_ref[...] = jnp.dot(s, w_ref[...].astype(BF16), preferred_element_type=F32) + b_ref[...]


def _ada(c8, ada_w, ada_b):
    depth, d, n = ada_w.shape
    tn = _pick(n, (1024, 512, 256, 128))
    return pl.pallas_call(
        _ada_kernel,
        grid=(depth, n // tn),
        in_specs=[pl.BlockSpec((8, d), lambda l, j: (0, 0)),
                  pl.BlockSpec((None, d, tn), lambda l, j: (l, 0, j)),
                  pl.BlockSpec((None, 1, tn), lambda l, j: (l, 0, j))],
        out_specs=pl.BlockSpec((None, 8, tn), lambda l, j: (l, 0, j)),
        out_shape=jax.ShapeDtypeStruct((depth, 8, n), F32),
        compiler_params=_cparams(("parallel", "parallel"), 40),
        name="ada",
    )(c8, ada_w, ada_b.reshape(depth, 1, n))


def _ln_mod_kernel(z_ref, y_ref, mod_ref, g_ref, b_ref, mod2_ref, *out_refs, gate_j, mod_j, alpha):
    m = mod_ref[...]
    zz = alpha * z_ref[...] + y_ref[...] * m[gate_j:gate_j + 1]
    mu = jnp.mean(zz, axis=-1, keepdims=True)
    zc = zz - mu
    var = jnp.mean(zc * zc, axis=-1, keepdims=True)
    zn = zc * lax.rsqrt(var + LN_EPS) * g_ref[...] + b_ref[...]
    out_refs[0][...] = zn
    if mod_j is not None:
        m2 = mod2_ref[...]
        out_refs[1][...] = (zn * (1 + m2[mod_j + 1:mod_j + 2]) + m2[mod_j:mod_j + 1]).astype(BF16)


def _ln_mod(z, y, mod, ln_g, ln_b, mod2, *, gate_j, mod_j, tr, n_ctx, alpha):
    b_n, l_z, d = z.shape
    l_y = y.shape[1]
    z_off = (l_z - l_y) // tr
    ncb = (n_ctx - (l_z - l_y)) // tr
    seg = lambda b, t: (b, jnp.where(t < ncb, 0, 1), 0, 0)
    row = pl.BlockSpec((None, tr, d), lambda b, t: (b, t, 0))
    out_shape = [jax.ShapeDtypeStruct((b_n, l_y, d), F32)]
    out_specs = [row]
    if mod_j is not None:
        out_shape.append(jax.ShapeDtypeStruct((b_n, l_y, d), BF16))
        out_specs.append(row)
    res = pl.pallas_call(
        functools.partial(_ln_mod_kernel, gate_j=gate_j, mod_j=mod_j, alpha=alpha),
        grid=(b_n, l_y // tr),
        in_specs=[pl.BlockSpec((None, tr, d), lambda b, t: (b, t + z_off, 0)),
                  row,
                  pl.BlockSpec((None, None, 6, d), seg),
                  pl.BlockSpec((1, d), lambda b, t: (0, 0)),
                  pl.BlockSpec((1, d), lambda b, t: (0, 0)),
                  pl.BlockSpec((None, None, 6, d), seg)],
        out_specs=out_specs,
        out_shape=out_shape,
        compiler_params=_cparams(("parallel", "parallel"), 40),
        name="ln_mod",
    )(z, y, mod, ln_g.reshape(1, d), ln_b.reshape(1, d), mod2)
    return res if mod_j is not None else (res[0], None)


def _seg_shift(x, row, shift, n_ctx):
    l_n = x.shape[0]
    rolled = pltpu.roll(x, (-shift) % l_n, 0)
    src = row + shift
    same_seg = (src >= 0) & (src < l_n) & ((src < n_ctx) == (row < n_ctx))
    return jnp.where(same_seg, rolled, 0.0)


def _rw_mix_kernel(z_ref, mod_ref, mu_ref, o_ref, *, n_ctx):
    z = z_ref[...]
    row = lax.broadcasted_iota(jnp.int32, z.shape, 0)
    is_ctx = row < n_ctx
    shift = jnp.where(is_ctx, mod_ref[0, 0:1, :], mod_ref[1, 0:1, :])
    scale = jnp.where(is_ctx, mod_ref[0, 1:2, :], mod_ref[1, 1:2, :])
    h = z * (1 + scale) + shift
    dx = 0.5 * (_seg_shift(h, row, -1, n_ctx) + _seg_shift(h, row, 1, n_ctx)) - h
    for n in range(6):
        o_ref[n] = (h + dx * mu_ref[n:n + 1, :]).astype(BF16)


def _rw_mix(z, mod, mu, *, n_ctx):
    b_n, l_n, d = z.shape
    tc = _pick(d, (256, 128))
    return pl.pallas_call(
        functools.partial(_rw_mix_kernel, n_ctx=n_ctx),
        grid=(b_n, d // tc),
        in_specs=[pl.BlockSpec((None, l_n, tc), lambda b, j: (b, 0, j)),
                  pl.BlockSpec((None, 2, 6, tc), lambda b, j: (b, 0, 0, j)),
                  pl.BlockSpec((6, tc), lambda b, j: (0, j))],
        out_specs=pl.BlockSpec((6, None, l_n, tc), lambda b, j: (0, b, 0, j)),
        out_shape=jax.ShapeDtypeStruct((6, b_n, l_n, d), BF16),
        compiler_params=_cparams(("parallel", "parallel"), 48),
        name="rw_mix",
    )(z, mod, mu)


def _lora_kernel(x_ref, a_ref, b_ref, o_ref, *, act):
    t = jnp.dot(x_ref[...], a_ref[...], preferred_element_type=F32)
    if act == "tanh":
        t = jnp.tanh(t)
    elif act == "sigmoid":
        t = _sigmoid(t)
    o_ref[...] = jnp.dot(t.astype(BF16), b_ref[...], preferred_element_type=F32)


def _lora(xs, x_idx, a, b, act):
    g_n, d, r = a.shape
    m_n = xs.shape[1]
    tm = _pick(m_n, (512, 256, 128, 64))
    return pl.pallas_call(
        functools.partial(_lora_kernel, act=act),
        grid=(g_n, m_n // tm),
        in_specs=[pl.BlockSpec((None, tm, d), lambda g, i: (x_idx, i, 0)),
                  pl.BlockSpec((None, d, r), lambda g, i: (g, 0, 0)),
                  pl.BlockSpec((None, r, d), lambda g, i: (g, 0, 0))],
        out_specs=pl.BlockSpec((None, tm, d), lambda g, i: (g, i, 0)),
        out_shape=jax.ShapeDtypeStruct((g_n, m_n, d), F32),
        compiler_params=_cparams(("parallel", "parallel"), 40),
        name="lora",
    )(xs, a, b)


def _chunk_of(q, ncc, nc, rev):
    if not rev:
        return q
    return jnp.where(q < ncc, ncc - 1 - q, nc - 1 - (q - ncc))


def _rwkv_kernel(r_ref, k_ref, v_ref, lw_ref, la_ref, g_ref, w0_ref, a0_ref, kk_ref, ka_ref, rk_ref,
                 gng_ref, gnb_ref, o_ref,
                 y_scr, u0_s, wk_s, y0_s, arb_s, rt_s, bh_s, g0_s, pt_s, *, n_ctx):
    t_n = CHUNK
    l_n = r_ref.shape[0]
    nc, ncc = l_n // t_n, n_ctx // t_n
    lane = lax.broadcasted_iota(jnp.int32, (1, LANES), 1)
    m1 = jnp.where(lane < RW_HEAD, 1.0, 0.0)
    m2 = 1.0 - m1
    ri = lax.broadcasted_iota(jnp.int32, (LANES, LANES), 0)
    ci = lax.broadcasted_iota(jnp.int32, (LANES, LANES), 1)
    gsum = jnp.where((ri // RW_HEAD) == (ci // RW_HEAD), 1.0, 0.0)
    gavg = gsum * (1.0 / RW_HEAD)
    eye = jnp.where(ri == ci, 1.0, 0.0)
    tr_i, tc_i = ri % t_n, ci % t_n
    r64 = lax.broadcasted_iota(jnp.int32, (t_n, t_n), 0)
    c64 = lax.broadcasted_iota(jnp.int32, (t_n, t_n), 1)
    k_k, k_a = kk_ref[...], ka_ref[...]

    def stack(x):
        return jnp.concatenate([x * m1, x * m2], axis=0)

    def rows_of(c):
        return pl.ds(pl.multiple_of(c * t_n, t_n), t_n)

    for d in (0, 1):
        rev = d == 1
        ltri = jnp.where((c64 >= r64) if rev else (c64 <= r64), 1.0, 0.0)
        strict = (tc_i > tr_i) if rev else (tc_i < tr_i)
        incl = (tc_i >= tr_i) if rev else (tc_i <= tr_i)
        w0, a0 = w0_ref[d:d + 1, :], a0_ref[d:d + 1, :]

        def local(c, carry, d=d, rev=rev, ltri=ltri, strict=strict, incl=incl, w0=w0, a0=a0):
            rows = rows_of(c)
            k, r, v = k_ref[rows, :], r_ref[rows, :], v_ref[rows, :]
            lw = -RW_DECAY_SCALE * _sigmoid(w0 + lw_ref[d, rows, :])
            a = _sigmoid(a0 + la_ref[d, rows, :])
            kkr = k * k_k
            kk = kkr * lax.rsqrt(_dotf(kkr * kkr, gsum) + 1e-12)
            kd = k * (1 + (a - 1) * k_a)
            bv = kk * a
            cum = _dotf(ltri, lw)
            p_end = cum[0:1, :] if rev else cum[t_n - 1:t_n, :]
            e_m = jnp.exp(-cum)
            e_h = jnp.exp(p_end - cum)
            rt = stack(r * jnp.exp(cum))
            kt = stack(kk * jnp.exp(cum - lw))
            q2 = jnp.concatenate([kt, rt], axis=0)
            k2 = jnp.concatenate([stack(bv * e_m), stack(kd * e_m)], axis=0)
            amat = _dot_nt(q2, k2)
            h2 = 2 * t_n
            lm = jnp.where(strict, amat[:h2, :h2], 0.0)
            l2 = _dotf(lm, lm)
            l4 = _dotf(l2, l2)
            l8 = _dotf(l4, l4)
            l16 = _dotf(l8, l8)
            l32 = _dotf(l16, l16)
            p = eye - lm
            for lp in (l2, l4, l8, l16, l32):
                p = p + _dotf(p, lp)
            vst = stack(v)
            akv = _dotf(jnp.where(strict, amat[:h2, h2:], 0.0), vst)
            u0_s[c] = -_dotf(p, akv)
            wk_s[c] = _dotf(p, kt)
            y0_s[c] = _dotf(jnp.where(incl, amat[h2:, h2:], 0.0), vst)
            arb_s[c] = jnp.where(incl, amat[h2:, :h2], 0.0)
            rt_s[c] = rt
            bh_s[c] = stack(bv * e_h)
            g0_s[c] = _dot_tn(vst, stack(kd * e_h))
            pt_s[c] = jnp.broadcast_to(jnp.exp(p_end), (8, LANES))
            return carry

        lax.fori_loop(0, nc, local, 0)

        def seq(q, s, d=d, rev=rev):
            c = _chunk_of(q, ncc, nc, rev)
            u = u0_s[c] - _dot_nt(wk_s[c], s)
            y = y0_s[c] + _dot_nt(rt_s[c], s) + _dotf(arb_s[c], u)
            yp = y[:t_n] + y[t_n:]
            rows = rows_of(c)
            if d == 0:
                y_scr[rows, :] = yp
            else:
                y_scr[rows, :] += yp
            return s * pt_s[c][0:1, :] + _dot_tn(u, bh_s[c]) + g0_s[c]

        lax.fori_loop(0, nc, seq, jnp.zeros((LANES, LANES), F32))

    def post(c, carry):
        rows = rows_of(c)
        y = y_scr[rows, :]
        yc = y - _dotf(y, gavg)
        var = _dotf(yc * yc, gavg)
        yn = yc * lax.rsqrt(var + RW_GN_EPS) * gng_ref[...] + gnb_ref[...]
        k, r, v = k_ref[rows, :], r_ref[rows, :], v_ref[rows, :]
        kd_f = k * (1 + (_sigmoid(a0_ref[0:1, :] + la_ref[0, rows, :]) - 1) * k_a)
        kd_b = k * (1 + (_sigmoid(a0_ref[1:2, :] + la_ref[1, rows, :]) - 1) * k_a)
        bonus = _dotf(r * (kd_f + kd_b) * rk_ref[...], gsum) * v
        o_ref[rows, :] = ((yn + bonus) * g_ref[rows, :]).astype(BF16)
        return carry

    lax.fori_loop(0, nc, post, 0)


def _rwkv_scan(rkv, lw, la, g, w0, a0, k_k, k_a, r_k, gn_g, gn_b, *, n_ctx):
    _, b_n, l_n, d = rkv.shape
    nc = l_n // CHUNK
    col = lambda n: pl.BlockSpec((None, None, l_n, LANES), lambda b, p, n=n: (n, b, 0, p))
    two = pl.BlockSpec((2, None, l_n, LANES), lambda b, p: (0, b, 0, p))
    par = lambda rows: pl.BlockSpec((rows, LANES), lambda b, p: (0, p))
    big = pltpu.VMEM((nc, LANES, LANES), F32)
    return pl.pallas_call(
        functools.partial(_rwkv_kernel, n_ctx=n_ctx),
        grid=(b_n, d // LANES),
        in_specs=[col(0), col(1), col(2), two, two,
                  pl.BlockSpec((None, l_n, LANES), lambda b, p: (b, 0, p)),
                  par(2), par(2), par(1), par(1), par(1), par(1), par(1)],
        out_specs=pl.BlockSpec((None, l_n, LANES), lambda b, p: (b, 0, p)),
        out_shape=jax.ShapeDtypeStruct((b_n, l_n, d), BF16),
        scratch_shapes=[pltpu.VMEM((l_n, LANES), F32), big, big, big, big, big, big, big,
                        pltpu.VMEM((nc, 8, LANES), F32)],
        compiler_params=_cparams(("parallel", "parallel"), 56),
        name="rwkv_scan",
    )(rkv, rkv, rkv, lw, la, g, w0, a0, k_k.reshape(1, d), k_a.reshape(1, d), r_k.reshape(1, d),
      gn_g.reshape(1, d), gn_b.reshape(1, d))


def _pad_axis(w, axis, to):
    pad = [(0, 0)] * w.ndim
    pad[axis] = (0, to - w.shape[axis])
    return jnp.pad(w, pad)


def _rwkv7_layer(z, mod, n_ctx, mu, w_rkv, w0, w1, w2, a0, a1, a2, g1, g2, k_k, k_a, r_k, gn_g, gn_b, w_o):
    b_n, l_n, d = z.shape
    m_n = b_n * l_n
    xs = _rw_mix(z, mod, mu, n_ctx=n_ctx).reshape(6, m_n, d)
    rkv = _matmul(xs, w_rkv.astype(BF16))
    r_w = -(-w1.shape[-1] // LANES) * LANES
    r_a = -(-a1.shape[-1] // LANES) * LANES
    lw = _lora(xs, 3, _pad_axis(w1, 2, r_w).astype(BF16), _pad_axis(w2, 1, r_w).astype(BF16), "tanh")
    la = _lora(xs, 4, _pad_axis(a1, 2, r_a).astype(BF16), _pad_axis(a2, 1, r_a).astype(BF16), None)
    gate = _lora(xs, 5, g1[None].astype(BF16), g2[None].astype(BF16), "sigmoid")
    o = _rwkv_scan(rkv.reshape(3, b_n, l_n, d), lw.reshape(2, b_n, l_n, d), la.reshape(2, b_n, l_n, d),
                   gate.reshape(b_n, l_n, d), w0, a0, k_k, k_a, r_k, gn_g, gn_b, n_ctx=n_ctx)
    return _mm2(o.reshape(m_n, d), w_o.astype(BF16)).reshape(b_n, l_n, d)


def _rope_kernel(x_ref, cos_ref, sa_ref, sb_ref, o_ref):
    j = pl.program_id(2)
    x = x_ref[...]
    d = x.shape[1]

    @pl.when(j < 2)
    def _():
        rep = d // DA_HEAD
        cos = jnp.tile(cos_ref[...], (1, rep))
        s_a = jnp.tile(sa_ref[...], (1, rep))
        s_b = jnp.tile(sb_ref[...], (1, rep))
        q = DA_HEAD // 4
        o_ref[...] = (x * cos + pltpu.roll(x, d - q, 1) * s_a + pltpu.roll(x, q, 1) * s_b).astype(BF16)

    @pl.when(j == 2)
    def _():
        o_ref[...] = x.astype(BF16)


def _rope(qkv, cos, s_a, s_b, *, tr):
    b_n, l_n, d3 = qkv.shape
    d = d3 // 3
    tab = pl.BlockSpec((tr, DA_HEAD), lambda b, t, j: (t, 0))
    return pl.pallas_call(
        _rope_kernel,
        grid=(b_n, l_n // tr, 3),
        in_specs=[pl.BlockSpec((None, tr, d), lambda b, t, j: (b, t, j)), tab, tab, tab],
        out_specs=pl.BlockSpec((None, tr, d), lambda b, t, j: (b, t, j)),
        out_shape=jax.ShapeDtypeStruct((b_n, l_n, d3), BF16),
        compiler_params=_cparams(("parallel", "parallel", "parallel"), 40),
        name="rope",
    )(qkv, cos, s_a, s_b)


def _attn_kernel(q_ref, k_ref, v_ref, lam_ref, sg_ref, o_ref, *, ncb, n_ctx, lam_init):
    qi = pl.program_id(2)
    lv = lam_ref[...]
    lam = (jnp.exp(jnp.sum(lv[0:1] * lv[1:2], axis=-1, keepdims=True))
           - jnp.exp(jnp.sum(lv[2:3] * lv[3:4], axis=-1, keepdims=True)) + lam_init)
    scale = DA_HEAD ** -0.5

    def attend(nk):
        def probs(m):
            q = q_ref[:, m * DA_HEAD:(m + 1) * DA_HEAD]
            k = k_ref[0:nk, m * DA_HEAD:(m + 1) * DA_HEAD]
            s = lax.dot_general(q, k, (((1,), (1,)), ((), ())), preferred_element_type=F32) * scale
            e = jnp.exp(s - jnp.max(s, axis=-1, keepdims=True))
            return e, 1.0 / jnp.sum(e, axis=-1, keepdims=True)
        e0, i0 = probs(0)
        e1, i1 = probs(1)
        p = e0 * i0 - e1 * (lam * i1)
        o = jnp.dot(p.astype(BF16), v_ref[0:nk, :], preferred_element_type=F32)
        o = o * lax.rsqrt(jnp.mean(o * o, axis=-1, keepdims=True) + 1e-5) * sg_ref[...] * (1 - lam_init)
        o_ref[...] = o.astype(BF16)

    if ncb > 0:
        @pl.when(qi < ncb)
        def _():
            attend(n_ctx)

    @pl.when(qi >= ncb)
    def _():
        attend(k_ref.shape[0])


def _attention(qkv, lam_vec, sub_g, *, tq, n_ctx, lam_init):
    b_n, l_n, d3 = qkv.shape
    d = d3 // 3
    hw = 2 * DA_HEAD
    nh = d // hw
    return pl.pallas_call(
        functools.partial(_attn_kernel, ncb=n_ctx // tq, n_ctx=n_ctx, lam_init=lam_init),
        grid=(b_n, nh, l_n // tq),
        in_specs=[pl.BlockSpec((None, tq, hw), lambda b, h, t: (b, t, h)),
                  pl.BlockSpec((None, l_n, hw), lambda b, h, t: (b, 0, nh + h)),
                  pl.BlockSpec((None, l_n, hw), lambda b, h, t: (b, 0, 2 * nh + h)),
                  pl.BlockSpec((4, DA_HEAD), lambda b, h, t: (0, 0)),
                  pl.BlockSpec((1, hw), lambda b, h, t: (0, 0))],
        out_specs=pl.BlockSpec((None, tq, hw), lambda b, h, t: (b, t, h)),
        out_shape=jax.ShapeDtypeStruct((b_n, l_n, d), BF16),
        compiler_params=_cparams(("parallel", "parallel", "arbitrary"), 48),
        name="diff_attn",
    )(qkv, qkv, qkv, lam_vec, sub_g.reshape(1, hw))


def _rope_tables(n_ctx, n_lat):
    n_rows = n_lat // GRID_W
    row = jnp.repeat(jnp.arange(n_rows, dtype=F32), GRID_W)
    col = jnp.tile(jnp.arange(GRID_W, dtype=F32), n_rows)
    nf = DA_HEAD // 4
    inv_freq = ROPE_BASE ** (-jnp.arange(nf, dtype=F32) / nf)
    ang_r, ang_c = row[:, None] * inv_freq, col[:, None] * inv_freq
    ang = jnp.concatenate([ang_r, ang_r, ang_c, ang_c], axis=-1)
    ang = jnp.concatenate([jnp.zeros((n_ctx, DA_HEAD), F32), ang], axis=0)
    cos, sin = jnp.cos(ang), jnp.sin(ang)
    even_q = (jnp.arange(DA_HEAD) // nf) % 2 == 0
    return cos, jnp.where(even_q, -sin, 0.0), jnp.where(even_q, 0.0, sin)


def _diff_attention_layer(h, n_ctx, layer_idx, w_qkv, lam_vec, sub_g, w_o, *, tr):
    b_n, l_n, d = h.shape
    m_n = b_n * l_n
    qkv = _mm2(h.reshape(m_n, d), w_qkv.astype(BF16)).reshape(b_n, l_n, 3 * d)
    cos, s_a, s_b = _rope_tables(n_ctx, l_n - n_ctx)
    qkv = _rope(qkv, cos, s_a, s_b, tr=tr)
    lam_init = 0.8 - 0.6 * math.exp(-0.3 * layer_idx)
    o = _attention(qkv, lam_vec, sub_g, tq=tr, n_ctx=n_ctx, lam_init=lam_init)
    return _mm2(o.reshape(m_n, d), w_o.astype(BF16)).reshape(b_n, l_n, d)


def _hgrn_kernel(q_ref, i_ref, g_ref, ff_ref, fb_ref, low_ref, ng_ref, o_ref, o_scr, *, n_ctx, layer_idx):
    t_n = CHUNK
    l_n = q_ref.shape[0]
    nc, ncc = l_n // t_n, n_ctx // t_n
    r64 = lax.broadcasted_iota(jnp.int32, (t_n, t_n), 0)
    c64 = lax.broadcasted_iota(jnp.int32, (t_n, t_n), 1)

    def rows_of(c):
        return pl.ds(pl.multiple_of(c * t_n, t_n), t_n)

    for d, f_ref in enumerate((ff_ref, fb_ref)):
        rev = d == 1
        low = low_ref[d]
        e = jnp.exp(low - jnp.max(low, axis=0, keepdims=True))
        sm = e / jnp.sum(e, axis=0, keepdims=True)
        cs = sm[0:1]
        for rr in range(1, layer_idx + 1):
            cs = cs + sm[rr:rr + 1]
        lb = cs - sm[0:1]
        ltri = jnp.where((c64 >= r64) if rev else (c64 <= r64), 1.0, 0.0)
        incl = (c64 >= r64) if rev else (c64 <= r64)

        def body(q, s, d=d, rev=rev, f_ref=f_ref, lb=lb, ltri=ltri, incl=incl):
            c = _chunk_of(q, ncc, nc, rev)
            rows = rows_of(c)
            f = lb + (1.0 - lb) * _sigmoid(f_ref[rows, :])
            cum = _dotf(ltri, jnp.log(f))
            b_end = cum[0:1, :] if rev else cum[t_n - 1:t_n, :]
            qv = q_ref[rows, :]
            qd = qv * _sigmoid(qv) * jnp.exp(cum)
            kk = 1.0 - f
            v = i_ref[rows, :]
            att = jnp.where(incl, _dot_nt(qd, kk * jnp.exp(-cum)), 0.0)
            o = _dotf(att, v) + _dot_nt(qd, s)
            if d == 0:
                o_scr[rows, :] = o
            else:
                o_scr[rows, :] += o
            return s * jnp.exp(b_end) + _dot_tn(v, kk * jnp.exp(b_end - cum))

        lax.fori_loop(0, nc, body, jnp.zeros((LANES, LANES), F32))

    def post(c, carry):
        rows = rows_of(c)
        o = o_scr[rows, :]
        o = o * lax.rsqrt(jnp.mean(o * o, axis=-1, keepdims=True) + 1e-5) * ng_ref[...]
        gv = g_ref[rows, :]
        o_ref[rows, :] = (o * (gv * _sigmoid(gv))).astype(BF16)
        return carry

    lax.fori_loop(0, nc, post, 0)


def _hgrn2_layer(h, n_ctx, layer_idx, w_in, lower, norm_g, w_o):
    b_n, l_n, d = h.shape
    m_n = b_n * l_n
    nh = d // HG_EXPAND
    proj = _mm2(h.reshape(m_n, d), w_in.astype(BF16)).reshape(b_n, l_n, 5 * d)
    col = lambda n: pl.BlockSpec((None, l_n, LANES), lambda b, p, n=n: (b, 0, n * nh + p))
    o = pl.pallas_call(
        functools.partial(_hgrn_kernel, n_ctx=n_ctx, layer_idx=layer_idx),
        grid=(b_n, nh),
        in_specs=[col(0), col(1), col(2), col(3), col(4),
                  pl.BlockSpec((2, lower.shape[1], LANES), lambda b, p: (0, 0, p)),
                  pl.BlockSpec((1, LANES), lambda b, p: (0, 0))],
        out_specs=pl.BlockSpec((None, l_n, LANES), lambda b, p: (b, 0, p)),
        out_shape=jax.ShapeDtypeStruct((b_n, l_n, d), BF16),
        scratch_shapes=[pltpu.VMEM((l_n, LANES), F32)],
        compiler_params=_cparams(("parallel", "parallel"), 40),
        name="hgrn_scan",
    )(proj, proj, proj, proj, proj, lower, norm_g.reshape(1, LANES))
    return _mm2(o.reshape(m_n, d), w_o.astype(BF16)).reshape(b_n, l_n, d)


def _gelu_tanh(x):
    return 0.5 * x * (1.0 + jnp.tanh(math.sqrt(2.0 / math.pi) * (x + 0.044715 * (x * x * x))))


def _softplus(x):
    return jnp.maximum(x, 0.0) + jnp.log1p(jnp.exp(-jnp.abs(x)))


def _rglru_kernel(gb_ref, xb_ref, cw_ref, cb_ref, wg_ref, bg_ref, lam_ref, o_ref, a_s, u_s, h_s, *, n_ctx):
    l_n = xb_ref.shape[0]
    n_lat = l_n - n_ctx
    x = xb_ref[...]
    row = lax.broadcasted_iota(jnp.int32, x.shape, 0)
    k_w = cw_ref.shape[0]
    xc = cb_ref[...] + sum(_seg_shift(x, row, j - (k_w - 1) // 2, n_ctx) * cw_ref[j:j + 1, :]
                           for j in range(k_w))
    xcb = xc.astype(BF16)
    for d in (0, 1):
        gate = lambda g: _sigmoid(jnp.dot(xcb, wg_ref[d, g].astype(BF16), preferred_element_type=F32)
                                  + bg_ref[d, g:g + 1, :])
        log_a = -LR_C * gate(0) * _softplus(-lam_ref[d:d + 1, :])
        a_s[d] = jnp.exp(log_a)
        u_s[d] = jnp.sqrt(jnp.tanh(-log_a) * (jnp.exp(2.0 * log_a) + 1.0)) * gate(1) * xc

    def fwd(t, h):
        h = a_s[0, pl.ds(t, 1), :] * h + u_s[0, pl.ds(t, 1), :]
        h_s[pl.ds(t, 1), :] = h
        return h

    lax.fori_loop(0, l_n, fwd, jnp.zeros((1, x.shape[1]), F32))

    def bwd(lo):
        def step(i, h):
            t = lo - i
            h = a_s[1, pl.ds(t, 1), :] * h + u_s[1, pl.ds(t, 1), :]
            h_s[pl.ds(t, 1), :] += h
            return h
        return step

    h = lax.fori_loop(0, n_ctx, bwd(n_ctx - 1), jnp.zeros((1, x.shape[1]), F32))
    lax.fori_loop(0, n_lat, bwd(l_n - 1), h)
    o_ref[...] = (h_s[n_ctx:, :] * _gelu_tanh(gb_ref[n_ctx:, :])).astype(BF16)


def _rglru_layer(h, n_ctx, w_in, conv_w, conv_b, w_gate, b_gate, lam, w_o):
    b_n, l_n, d = h.shape
    n_lat = l_n - n_ctx
    nb = d // LR_BS
    proj = _mm2(h.reshape(b_n * l_n, d), w_in.astype(BF16)).reshape(b_n, l_n, 2 * d)
    k_w = conv_w.shape[0]
    o = pl.pallas_call(
        functools.partial(_rglru_kernel, n_ctx=n_ctx),
        grid=(b_n, nb),
        in_specs=[pl.BlockSpec((None, l_n, LR_BS), lambda b, j: (b, 0, j)),
                  pl.BlockSpec((None, l_n, LR_BS), lambda b, j: (b, 0, nb + j)),
                  pl.BlockSpec((k_w, LR_BS), lambda b, j: (0, j)),
                  pl.BlockSpec((1, LR_BS), lambda b, j: (0, j)),
                  pl.BlockSpec((2, 2, None, LR_BS, LR_BS), lambda b, j: (0, 0, j, 0, 0)),
                  pl.BlockSpec((2, 2, LR_BS), lambda b, j: (0, 0, j)),
                  pl.BlockSpec((2, LR_BS), lambda b, j: (0, j))],
        out_specs=pl.BlockSpec((None, n_lat, LR_BS), lambda b, j: (b, 0, j)),
        out_shape=jax.ShapeDtypeStruct((b_n, n_lat, d), BF16),
        scratch_shapes=[pltpu.VMEM((2, l_n, LR_BS), F32), pltpu.VMEM((2, l_n, LR_BS), F32),
                        pltpu.VMEM((l_n, LR_BS), F32)],
        compiler_params=_cparams(("parallel", "parallel"), 56),
        name="rglru",
    )(proj, proj, conv_w, conv_b.reshape(1, d), w_gate, b_gate, lam)
    return _mm2(o.reshape(b_n * n_lat, d), w_o.astype(BF16)).reshape(b_n, n_lat, d)


def _ffn_act_kernel(ug_ref, uv_ref, wg_ref, wv_ref, bg_ref, bv_ref, o_ref, *, n_ctx):
    row = lax.broadcasted_iota(jnp.int32, ug_ref.shape, 0)

    def conv(u_ref, w_ref, b_ref):
        u = u_ref[...]
        k_w = w_ref.shape[0]
        return b_ref[...] + sum(_seg_shift(u, row, j - (k_w - 1) // 2, n_ctx) * w_ref[j:j + 1, :]
                                for j in range(k_w))
    gate = conv(ug_ref, wg_ref, bg_ref)
    o_ref[...] = (gate * _sigmoid(gate) * conv(uv_ref, wv_ref, bv_ref)).astype(BF16)


def _conv_ffn(h, n_ctx, w_up, conv_w, conv_b, w_down):
    b_n, l_n, d = h.shape
    m_n = b_n * l_n
    f = w_down.shape[0]
    u = _mm2(h.reshape(m_n, d), w_up.astype(BF16)).reshape(b_n, l_n, 2 * f)
    tc = _pick(f, (256, 128))
    nf = f // tc
    k_w = conv_w.shape[0]
    cb = conv_b.reshape(1, 2 * f)
    act = pl.pallas_call(
        functools.partial(_ffn_act_kernel, n_ctx=n_ctx),
        grid=(b_n, nf),
        in_specs=[pl.BlockSpec((None, l_n, tc), lambda b, j: (b, 0, j)),
                  pl.BlockSpec((None, l_n, tc), lambda b, j: (b, 0, nf + j)),
                  pl.BlockSpec((k_w, tc), lambda b, j: (0, j)),
                  pl.BlockSpec((k_w, tc), lambda b, j: (0, nf + j)),
                  pl.BlockSpec((1, tc), lambda b, j: (0, j)),
                  pl.BlockSpec((1, tc), lambda b, j: (0, nf + j))],
        out_specs=pl.BlockSpec((None, l_n, tc), lambda b, j: (b, 0, j)),
        out_shape=jax.ShapeDtypeStruct((b_n, l_n, f), BF16),
        compiler_params=_cparams(("parallel", "parallel"), 48),
        name="ffn_act",
    )(u, u, conv_w, conv_w, cb, cb)
    return _mm2(act.reshape(m_n, f), w_down.astype(BF16)).reshape(b_n, l_n, d)


def kernel(x, c, ctx, c_ctx, ada_w, ada_b, ln_g, ln_b, ffn_w_up, ffn_conv_w, ffn_conv_b, ffn_w_down, rw_mu, rw_w_rkv, rw_w0, rw_w1, rw_w2, rw_a0, rw_a1, rw_a2, rw_g1, rw_g2, rw_k_k, rw_k_a, rw_r_k, rw_gn_g, rw_gn_b, rw_w_o, da_w_qkv, da_lambda, da_sub_g, da_w_o, hg_w_in, hg_lower, hg_norm_g, hg_w_o, lr_w_in, lr_conv_w, lr_conv_b, lr_w_gate, lr_b_gate, lr_lambda, lr_w_o):
    b_n, n_lat, d = x.shape
    n_ctx = ctx.shape[1]
    depth = ada_w.shape[0]
    assert depth == 4 and rw_mu.shape[0] == 1, "one occurrence of each of the four mixers"
    assert b_n + 1 <= 8 and n_ctx % CHUNK == 0 and n_lat % CHUNK == 0
    tr = math.gcd(math.gcd(n_ctx, n_lat), 256)
    alpha = (2 * depth) ** 0.25

    c8 = jnp.concatenate([c, c_ctx[None], jnp.zeros((8 - b_n - 1, d), F32)], axis=0)
    m = _ada(c8, ada_w, ada_b)
    m_lat = m[:, :b_n].reshape(depth, b_n, 1, 6, d)
    m_ctx = jnp.broadcast_to(m[:, b_n].reshape(depth, 1, 1, 6, d), (depth, b_n, 1, 6, d))
    mod = jnp.concatenate([m_ctx, m_lat], axis=2)

    z = jnp.concatenate([ctx, x], axis=1)
    h = None
    for i in range(depth):
        last = i == depth - 1
        if i == 0:
            y = _rwkv7_layer(z, mod[0], n_ctx, rw_mu[0], rw_w_rkv[0], rw_w0[0], rw_w1[0], rw_w2[0], rw_a0[0],
                             rw_a1[0], rw_a2[0], rw_g1[0], rw_g2[0], rw_k_k[0], rw_k_a[0], rw_r_k[0],
                             rw_gn_g[0], rw_gn_b[0], rw_w_o[0])
        elif i == 1:
            y = _diff_attention_layer(h, n_ctx, i, da_w_qkv[0], da_lambda[0], da_sub_g[0], da_w_o[0], tr=tr)
        elif i == 2:
            y = _hgrn2_layer(h, n_ctx, i, hg_w_in[0], hg_lower, hg_norm_g[0], hg_w_o[0])
        else:
            y = _rglru_layer(h, n_ctx, lr_w_in[0], lr_conv_w[0], lr_conv_b[0], lr_w_gate[0], lr_b_gate[0],
                             lr_lambda[0], lr_w_o[0])
        z, h = _ln_mod(z, y, mod[i], ln_g[i, 0], ln_b[i, 0], mod[i], gate_j=2, mod_j=3, tr=tr,
                       n_ctx=n_ctx, alpha=alpha)
        if last:
            n_ctx = 0
        y = _conv_ffn(h, n_ctx, ffn_w_up[i], ffn_conv_w[i], ffn_conv_b[i], ffn_w_down[i])
        z, h = _ln_mod(z, y, mod[i], ln_g[i, 1], ln_b[i, 1], mod[min(i + 1, depth - 1)], gate_j=5,
                       mod_j=None if last else 0, tr=tr, n_ctx=n_ctx, alpha=alpha)
    return z
```

```python
import functools
import math

import jax
import jax.numpy as jnp
from jax import lax
from jax.experimental import pallas as pl
from jax.experimental.pallas import tpu as pltpu

F32, BF16 = jnp.float32, jnp.bfloat16

LANES = 128
CHUNK = 64
LN_EPS = 1e-5
GRID_W = 64
ROPE_BASE = 10000.0
RW_HEAD = 64
RW_DECAY_SCALE = 0.606531
RW_GN_EPS = 64e-5
DA_HEAD = 128
HG_EXPAND = 128
LR_BS = 256
LR_C = 8.0
MIB = 1024 * 1024


def _pick(n, cands):
    for c in cands:
        if n % c == 0:
            return c
    return n


def _cparams(sem, vmem_mib):
    return pltpu.CompilerParams(dimension_semantics=sem, vmem_limit_bytes=vmem_mib * MIB)


def _sigmoid(x):
    return jax.nn.sigmoid(x)


NN = (((1,), (0,)), ((), ()))
NT = (((1,), (1,)), ((), ()))


def _parts(x, n):
    out = []
    for i in range(n):
        p = x.astype(BF16)
        out.append(p)
        if i + 1 < n:
            x = x - p.astype(F32)
    return out


def _mdot(ap, bp, dims=NN, order=2):
    acc = None
    for i, a in enumerate(ap):
        for j, b in enumerate(bp):
            if i + j < order:
                t = lax.dot_general(a, b, dims, preferred_element_type=F32)
                acc = t if acc is None else acc + t
    return acc


def _cat_parts(xs, axis):
    return [jnp.concatenate(ps, axis=axis) for ps in zip(*xs)]


def _mm_kernel(a_ref, w_ref, o_ref, acc_ref, *, nk):
    prod = jnp.dot(a_ref[...], w_ref[...], preferred_element_type=F32)
    if nk == 1:
        o_ref[...] = prod.astype(o_ref.dtype)
    else:
        k = pl.program_id(3)

        @pl.when(k == 0)
        def _():
            acc_ref[...] = prod

        @pl.when(k > 0)
        def _():
            acc_ref[...] += prod

        @pl.when(k == nk - 1)
        def _():
            o_ref[...] = acc_ref[...].astype(o_ref.dtype)


def _matmul(a, w, *, out_dtype=F32, a_off=0):
    g_n, k_n, n_n = w.shape
    m_n = a.shape[1]
    tm = _pick(m_n, (1024, 512, 256, 128, 64))
    tn = _pick(n_n, (1024, 512, 256, 128))
    tk = k_n if k_n <= 2048 else _pick(k_n, (2816, 2048, 1024, 512))
    nk = k_n // tk
    return pl.pallas_call(
        functools.partial(_mm_kernel, nk=nk),
        grid=(g_n, m_n // tm, n_n // tn, nk),
        in_specs=[pl.BlockSpec((None, tm, tk), lambda g, i, j, k: (g + a_off, i, k)),
                  pl.BlockSpec((None, tk, tn), lambda g, i, j, k: (g, k, j))],
        out_specs=pl.BlockSpec((None, tm, tn), lambda g, i, j, k: (g, i, j)),
        out_shape=jax.ShapeDtypeStruct((g_n, m_n, n_n), out_dtype),
        scratch_shapes=[pltpu.VMEM((tm, tn), F32)],
        compiler_params=_cparams(("parallel", "parallel", "parallel", "arbitrary"), 48),
        name="matmul",
    )(a, w)


def _mm2(a, w, **kw):
    return _matmul(a[None], w[None], **kw)[0]


def _ada_kernel(c_ref, w_ref, b_ref, o_ref):
    c = c_ref[...]
    s = (c * _sigmoid(c)).astype(BF16)
    o_ref[...] = jnp.dot(s, w_ref[...].astype(BF16), preferred_element_type=F32) + b_ref[...]


def _ada(c8, ada_w, ada_b):
    depth, d, n = ada_w.shape
    tn = _pick(n, (1024, 512, 256, 128))
    return pl.pallas_call(
        _ada_kernel,
        grid=(depth, n // tn),
        in_specs=[pl.BlockSpec((8, d), lambda l, j: (0, 0)),
                  pl.BlockSpec((None, d, tn), lambda l, j: (l, 0, j)),
                  pl.BlockSpec((None, 1, tn), lambda l, j: (l, 0, j))],
        out_specs=pl.BlockSpec((None, 8, tn), lambda l, j: (l, 0, j)),
        out_shape=jax.ShapeDtypeStruct((depth, 8, n), F32),
        compiler_params=_cparams(("parallel", "parallel"), 40),
        name="ada",
    )(c8, ada_w, ada_b.reshape(depth, 1, n))


def _ln_mod_kernel(z_ref, y_ref, mod_ref, g_ref, b_ref, mod2_ref, *out_refs, gate_j, mod_j, alpha):
    m = mod_ref[...]
    zz = alpha * z_ref[...] + y_ref[...] * m[gate_j:gate_j + 1]
    mu = jnp.mean(zz, axis=-1, keepdims=True)
    zc = zz - mu
    var = jnp.mean(zc * zc, axis=-1, keepdims=True)
    zn = zc * lax.rsqrt(var + LN_EPS) * g_ref[...] + b_ref[...]
    out_refs[0][...] = zn
    if mod_j is not None:
        m2 = mod2_ref[...]
        out_refs[1][...] = (zn * (1 + m2[mod_j + 1:mod_j + 2]) + m2[mod_j:mod_j + 1]).astype(BF16)


def _ln_mod(z, y, mod, ln_g, ln_b, mod2, *, gate_j, mod_j, tr, n_ctx, alpha):
    b_n, l_z, d = z.shape
    l_y = y.shape[1]
    z_off = (l_z - l_y) // tr
    ncb = (n_ctx - (l_z - l_y)) // tr
    seg = lambda b, t: (b, jnp.where(t < ncb, 0, 1), 0, 0)
    row = pl.BlockSpec((None, tr, d), lambda b, t: (b, t, 0))
    out_shape = [jax.ShapeDtypeStruct((b_n, l_y, d), F32)]
    out_specs = [row]
    if mod_j is not None:
        out_shape.append(jax.ShapeDtypeStruct((b_n, l_y, d), BF16))
        out_specs.append(row)
    res = pl.pallas_call(
        functools.partial(_ln_mod_kernel, gate_j=gate_j, mod_j=mod_j, alpha=alpha),
        grid=(b_n, l_y // tr),
        in_specs=[pl.BlockSpec((None, tr, d), lambda b, t: (b, t + z_off, 0)),
                  row,
                  pl.BlockSpec((None, None, 6, d), seg),
                  pl.BlockSpec((1, d), lambda b, t: (0, 0)),
                  pl.BlockSpec((1, d), lambda b, t: (0, 0)),
                  pl.BlockSpec((None, None, 6, d), seg)],
        out_specs=out_specs,
        out_shape=out_shape,
        compiler_params=_cparams(("parallel", "parallel"), 40),
        name="ln_mod",
    )(z, y, mod, ln_g.reshape(1, d), ln_b.reshape(1, d), mod2)
    return res if mod_j is not None else (res[0], None)


def _seg_shift(x, row, shift, n_ctx):
    l_n = x.shape[0]
    rolled = pltpu.roll(x, (-shift) % l_n, 0)
    src = row + shift
    same_seg = (src >= 0) & (src < l_n) & ((src < n_ctx) == (row < n_ctx))
    return jnp.where(same_seg, rolled, 0.0)


def _rw_mix_kernel(z_ref, mod_ref, mu_ref, o_ref, *, n_ctx):
    z = z_ref[...]
    row = lax.broadcasted_iota(jnp.int32, z.shape, 0)
    is_ctx = row < n_ctx
    shift = jnp.where(is_ctx, mod_ref[0, 0:1, :], mod_ref[1, 0:1, :])
    scale = jnp.where(is_ctx, mod_ref[0, 1:2, :], mod_ref[1, 1:2, :])
    h = z * (1 + scale) + shift
    dx = 0.5 * (_seg_shift(h, row, -1, n_ctx) + _seg_shift(h, row, 1, n_ctx)) - h
    for n in range(6):
        o_ref[n] = (h + dx * mu_ref[n:n + 1, :]).astype(BF16)


def _rw_mix(z, mod, mu, *, n_ctx):
    b_n, l_n, d = z.shape
    tc = _pick(d, (256, 128))
    return pl.pallas_call(
        functools.partial(_rw_mix_kernel, n_ctx=n_ctx),
        grid=(b_n, d // tc),
        in_specs=[pl.BlockSpec((None, l_n, tc), lambda b, j: (b, 0, j)),
                  pl.BlockSpec((None, 2, 6, tc), lambda b, j: (b, 0, 0, j)),
                  pl.BlockSpec((6, tc), lambda b, j: (0, j))],
        out_specs=pl.BlockSpec((6, None, l_n, tc), lambda b, j: (0, b, 0, j)),
        out_shape=jax.ShapeDtypeStruct((6, b_n, l_n, d), BF16),
        compiler_params=_cparams(("parallel", "parallel"), 48),
        name="rw_mix",
    )(z, mod, mu)


def _lora_kernel(x_ref, a_ref, b_ref, o_ref, *, act):
    t = jnp.dot(x_ref[...], a_ref[...], preferred_element_type=F32)
    if act == "tanh":
        t = jnp.tanh(t)
    elif act == "sigmoid":
        t = _sigmoid(t)
    o_ref[...] = jnp.dot(t.astype(BF16), b_ref[...], preferred_element_type=F32)


def _lora(xs, x_idx, a, b, act):
    g_n, d, r = a.shape
    m_n = xs.shape[1]
    tm = _pick(m_n, (512, 256, 128, 64))
    return pl.pallas_call(
        functools.partial(_lora_kernel, act=act),
        grid=(g_n, m_n // tm),
        in_specs=[pl.BlockSpec((None, tm, d), lambda g, i: (x_idx, i, 0)),
                  pl.BlockSpec((None, d, r), lambda g, i: (g, 0, 0)),
                  pl.BlockSpec((None, r, d), lambda g, i: (g, 0, 0))],
        out_specs=pl.BlockSpec((None, tm, d), lambda g, i: (g, i, 0)),
        out_shape=jax.ShapeDtypeStruct((g_n, m_n, d), F32),
        compiler_params=_cparams(("parallel", "parallel"), 40),
        name="lora",
    )(xs, a, b)


def _chunk_of(q, ncc, nc, rev):
    if not rev:
        return q
    return jnp.where(q < ncc, ncc - 1 - q, nc - 1 - (q - ncc))


def _rwkv_kernel(r_ref, k_ref, v_ref, lw_ref, la_ref, g_ref, w0_ref, a0_ref, kk_ref, ka_ref, rk_ref,
                 gng_ref, gnb_ref, o_ref,
                 y_scr, mr_s, n_s, *, n_ctx):
    t_n = CHUNK
    h2 = 2 * t_n
    l_n = r_ref.shape[0]
    nc, ncc = l_n // t_n, n_ctx // t_n
    group = 2 if nc % 2 == 0 else 1
    lane = lax.broadcasted_iota(jnp.int32, (1, LANES), 1)
    m1 = jnp.where(lane < RW_HEAD, 1.0, 0.0)
    m2 = 1.0 - m1
    ri = lax.broadcasted_iota(jnp.int32, (LANES, LANES), 0)
    ci = lax.broadcasted_iota(jnp.int32, (LANES, LANES), 1)
    same_head = (ri // RW_HEAD) == (ci // RW_HEAD)
    gsum_b = jnp.where(same_head, 1.0, 0.0).astype(BF16)
    gavg_b = jnp.where(same_head, 1.0 / RW_HEAD, 0.0).astype(BF16)
    eye = jnp.where(ri == ci, 1.0, 0.0)
    tr_i, tc_i = ri % t_n, ci % t_n
    r64 = lax.broadcasted_iota(jnp.int32, (t_n, t_n), 0)
    c64 = lax.broadcasted_iota(jnp.int32, (t_n, t_n), 1)
    k_k, k_a = kk_ref[...], ka_ref[...]

    def stack(x):
        return jnp.concatenate([x * m1, x * m2], axis=0)

    def rows_of(c):
        return pl.ds(pl.multiple_of(c * t_n, t_n), t_n)

    def head_sum(x, w_b):
        return _mdot(_parts(x, 3), [w_b], order=3)

    ltri_b = [jnp.where((c64 >= r64) if rev else (c64 <= r64), 1.0, 0.0).astype(BF16) for rev in (False, True)]
    strict = [(tc_i > tr_i) if rev else (tc_i < tr_i) for rev in (False, True)]
    incl = [(tc_i >= tr_i) if rev else (tc_i <= tr_i) for rev in (False, True)]

    def stage_prep(d, c):
        rows = rows_of(c)
        k, r, v = k_ref[rows, :], r_ref[rows, :], v_ref[rows, :]
        lw = -RW_DECAY_SCALE * _sigmoid(w0_ref[d:d + 1, :] + lw_ref[d, rows, :])
        a = _sigmoid(a0_ref[d:d + 1, :] + la_ref[d, rows, :])
        kkr = k * k_k
        return dict(d=d, c=c, k=k, r=r, v=v, lw=lw, a=a, kkr=kkr,
                    ss=head_sum(kkr * kkr, gsum_b), cum=_mdot([ltri_b[d]], _parts(lw, 3), order=3))

    def stage_amat(s):
        d, cum, lw, a = s["d"], s["cum"], s["lw"], s["a"]
        kk = s["kkr"] * lax.rsqrt(s["ss"] + 1e-12)
        kd = s["k"] * (1 + (a - 1) * k_a)
        bv = kk * a
        p_end = cum[0:1, :] if d == 1 else cum[t_n - 1:t_n, :]
        e_m = jnp.exp(-cum)
        e_h = jnp.exp(p_end - cum)
        ktp = _parts(stack(kk * jnp.exp(cum - lw)), 2)
        rt = stack(s["r"] * jnp.exp(cum))
        k2p = _cat_parts([_parts(stack(bv * e_m), 2), _parts(stack(kd * e_m), 2)], 0)
        return dict(d=d, c=s["c"], ktp=ktp, rt=rt, vp=_parts(stack(s["v"]), 2), p_end=p_end,
                    bh=stack(bv * e_h), kh=stack(kd * e_h),
                    amat=_mdot(_cat_parts([ktp, _parts(rt, 2)], 0), k2p, NT))

    def stage_square(s):
        d, amat = s["d"], s["amat"]
        lm = jnp.where(strict[d], amat[:h2, :h2], 0.0)
        lmp = _parts(lm, 2)
        msk = jnp.concatenate([jnp.where(strict[d], amat[:h2, h2:], 0.0),
                               jnp.where(incl[d], amat[h2:, h2:], 0.0)], axis=0)
        s = dict(s, p=eye - lm, x=_mdot(lmp, lmp),
                 av=_mdot(_parts(msk, 2), s["vp"]),
                 arbp=_parts(jnp.where(incl[d], amat[h2:, :h2], 0.0), 2))
        del s["amat"]
        return s

    def stage_double(s, final):
        xp = _parts(s["x"], 2)
        if final:
            return dict(s, p=s["p"] + _mdot(_parts(s["p"], 2), xp[:1]))
        both = _mdot(_cat_parts([_parts(s["p"], 2), xp], 0), xp[:1])
        return dict(s, p=s["p"] + both[:h2], x=both[h2:])

    def stage_solve(s):
        uw = _mdot(_parts(s["p"], 2), _cat_parts([s["ktp"], _parts(-s["av"][:h2], 2)], 1))
        return dict(s, wup=_parts(uw, 2))

    def stage_fold(s):
        d, c, wup = s["d"], s["c"], s["wup"]
        aw = _mdot(s["arbp"], wup)
        zp = [jnp.zeros((h2, LANES), BF16)] * 2
        lhs = _cat_parts([_parts(s["bh"].T, 2), _parts(s["kh"].T, 2)], 1)
        mn = _mdot(lhs, _cat_parts([wup, _cat_parts([zp, s["vp"]], 1)], 0))
        dg = jnp.where(ri == ci, jnp.broadcast_to(jnp.exp(s["p_end"]), (LANES, LANES)), 0.0)
        rp = s["rt"] - aw[:, :LANES]
        mrp = _parts(jnp.concatenate([dg - mn[:, :LANES], rp[:t_n] + rp[t_n:]], axis=0), 2)
        for i in range(2):
            mr_s[d, c, i] = mrp[i]
        n_s[d, c] = mn[:, LANES:]
        y0 = s["av"][h2:] + aw[:, LANES:]
        return y0[:t_n] + y0[t_n:]

    def local(i, carry):
        sts = [stage_prep(d, i * group + g) for g in range(group) for d in (0, 1)]
        sts = [stage_amat(s) for s in sts]
        sts = [stage_square(s) for s in sts]
        for step in range(5):
            sts = [stage_double(s, step == 4) for s in sts]
        sts = [stage_solve(s) for s in sts]
        y0 = [stage_fold(s) for s in sts]
        for g in range(group):
            y_scr[rows_of(i * group + g), :] = y0[2 * g] + y0[2 * g + 1]
        return carry

    lax.fori_loop(0, nc // group, local, 0)

    def seq(q, hs):
        cs = (q, _chunk_of(q, ncc, nc, True))
        mh = [_mdot([mr_s[d, cs[d], 0], mr_s[d, cs[d], 1]], _parts(hs[d], 2)) for d in (0, 1)]
        for d in (0, 1):
            y_scr[rows_of(cs[d]), :] += mh[d][h2:]
        return tuple(mh[d][:h2] + n_s[d, cs[d]] for d in (0, 1))

    zero = jnp.zeros((LANES, LANES), F32)
    lax.fori_loop(0, nc, seq, (zero, zero))

    n_post = _pick(nc, (4, 3, 2, 1))

    def post(i, carry):
        rows = [rows_of(i * n_post + g) for g in range(n_post)]

        def bonus_sum(rw):
            k, r = k_ref[rw, :], r_ref[rw, :]
            kd_f = k * (1 + (_sigmoid(a0_ref[0:1, :] + la_ref[0, rw, :]) - 1) * k_a)
            kd_b = k * (1 + (_sigmoid(a0_ref[1:2, :] + la_ref[1, rw, :]) - 1) * k_a)
            return head_sum(r * (kd_f + kd_b) * rk_ref[...], gsum_b)
        bsum = [bonus_sum(rw) for rw in rows]
        ys = [y_scr[rw, :] for rw in rows]
        ycs = [y - m for y, m in zip(ys, [head_sum(y, gavg_b) for y in ys])]
        var = [head_sum(yc * yc, gavg_b) for yc in ycs]
        for rw, yc, vr, bs in zip(rows, ycs, var, bsum):
            yn = yc * lax.rsqrt(vr + RW_GN_EPS) * gng_ref[...] + gnb_ref[...]
            o_ref[rw, :] = ((yn + bs * v_ref[rw, :]) * g_ref[rw, :]).astype(BF16)
        return carry

    lax.fori_loop(0, nc // n_post, post, 0)


def _rwkv_scan(rkv, lw, la, g, w0, a0, k_k, k_a, r_k, gn_g, gn_b, *, n_ctx):
    _, b_n, l_n, d = rkv.shape
    nc = l_n // CHUNK
    col = lambda n: pl.BlockSpec((None, None, l_n, LANES), lambda b, p, n=n: (n, b, 0, p))
    two = pl.BlockSpec((2, None, l_n, LANES), lambda b, p: (0, b, 0, p))
    par = lambda rows: pl.BlockSpec((rows, LANES), lambda b, p: (0, p))
    return pl.pallas_call(
        functools.partial(_rwkv_kernel, n_ctx=n_ctx),
        grid=(b_n, d // LANES),
        in_specs=[col(0), col(1), col(2), two, two,
                  pl.BlockSpec((None, l_n, LANES), lambda b, p: (b, 0, p)),
                  par(2), par(2), par(1), par(1), par(1), par(1), par(1)],
        out_specs=pl.BlockSpec((None, l_n, LANES), lambda b, p: (b, 0, p)),
        out_shape=jax.ShapeDtypeStruct((b_n, l_n, d), BF16),
        scratch_shapes=[pltpu.VMEM((l_n, LANES), F32),
                        pltpu.VMEM((2, nc, 2, LANES + CHUNK, LANES), BF16),
                        pltpu.VMEM((2, nc, LANES, LANES), F32)],
        compiler_params=_cparams(("parallel", "parallel"), 56),
        name="rwkv_scan",
    )(rkv, rkv, rkv, lw, la, g, w0, a0, k_k.reshape(1, d), k_a.reshape(1, d), r_k.reshape(1, d),
      gn_g.reshape(1, d), gn_b.reshape(1, d))


def _pad_axis(w, axis, to):
    pad = [(0, 0)] * w.ndim
    pad[axis] = (0, to - w.shape[axis])
    return jnp.pad(w, pad)


def _rwkv7_layer(z, mod, n_ctx, mu, w_rkv, w0, w1, w2, a0, a1, a2, g1, g2, k_k, k_a, r_k, gn_g, gn_b, w_o):
    b_n, l_n, d = z.shape
    m_n = b_n * l_n
    xs = _rw_mix(z, mod, mu, n_ctx=n_ctx).reshape(6, m_n, d)
    rkv = _matmul(xs, w_rkv.astype(BF16))
    r_w = -(-w1.shape[-1] // LANES) * LANES
    r_a = -(-a1.shape[-1] // LANES) * LANES
    lw = _lora(xs, 3, _pad_axis(w1, 2, r_w).astype(BF16), _pad_axis(w2, 1, r_w).astype(BF16), "tanh")
    la = _lora(xs, 4, _pad_axis(a1, 2, r_a).astype(BF16), _pad_axis(a2, 1, r_a).astype(BF16), None)
    gate = _lora(xs, 5, g1[None].astype(BF16), g2[None].astype(BF16), "sigmoid")
    o = _rwkv_scan(rkv.reshape(3, b_n, l_n, d), lw.reshape(2, b_n, l_n, d), la.reshape(2, b_n, l_n, d),
                   gate.reshape(b_n, l_n, d), w0, a0, k_k, k_a, r_k, gn_g, gn_b, n_ctx=n_ctx)
    return _mm2(o.reshape(m_n, d), w_o.astype(BF16)).reshape(b_n, l_n, d)


def _rope_kernel(x_ref, cos_ref, sa_ref, sb_ref, o_ref):
    j = pl.program_id(2)
    x = x_ref[...]
    d = x.shape[1]

    @pl.when(j < 2)
    def _():
        rep = d // DA_HEAD
        cos = jnp.tile(cos_ref[...], (1, rep))
        s_a = jnp.tile(sa_ref[...], (1, rep))
        s_b = jnp.tile(sb_ref[...], (1, rep))
        q = DA_HEAD // 4
        o_ref[...] = (x * cos + pltpu.roll(x, d - q, 1) * s_a + pltpu.roll(x, q, 1) * s_b).astype(BF16)

    @pl.when(j == 2)
    def _():
        o_ref[...] = x.astype(BF16)


def _rope(qkv, cos, s_a, s_b, *, tr):
    b_n, l_n, d3 = qkv.shape
    d = d3 // 3
    tab = pl.BlockSpec((tr, DA_HEAD), lambda b, t, j: (t, 0))
    return pl.pallas_call(
        _rope_kernel,
        grid=(b_n, l_n // tr, 3),
        in_specs=[pl.BlockSpec((None, tr, d), lambda b, t, j: (b, t, j)), tab, tab, tab],
        out_specs=pl.BlockSpec((None, tr, d), lambda b, t, j: (b, t, j)),
        out_shape=jax.ShapeDtypeStruct((b_n, l_n, d3), BF16),
        compiler_params=_cparams(("parallel", "parallel", "parallel"), 40),
        name="rope",
    )(qkv, cos, s_a, s_b)


def _attn_kernel(q_ref, k_ref, v_ref, lam_ref, sg_ref, o_ref, *, ncb, n_ctx, lam_init):
    qi = pl.program_id(2)
    lv = lam_ref[...]
    lam = (jnp.exp(jnp.sum(lv[0:1] * lv[1:2], axis=-1, keepdims=True))
           - jnp.exp(jnp.sum(lv[2:3] * lv[3:4], axis=-1, keepdims=True)) + lam_init)
    scale = DA_HEAD ** -0.5

    def attend(nk):
        def probs(m):
            q = q_ref[:, m * DA_HEAD:(m + 1) * DA_HEAD]
            k = k_ref[0:nk, m * DA_HEAD:(m + 1) * DA_HEAD]
            s = lax.dot_general(q, k, NT, preferred_element_type=F32) * scale
            e = jnp.exp(s - jnp.max(s, axis=-1, keepdims=True))
            return e, 1.0 / jnp.sum(e, axis=-1, keepdims=True)
        e0, i0 = probs(0)
        e1, i1 = probs(1)
        p = e0 * i0 - e1 * (lam * i1)
        o = jnp.dot(p.astype(BF16), v_ref[0:nk, :], preferred_element_type=F32)
        o = o * lax.rsqrt(jnp.mean(o * o, axis=-1, keepdims=True) + 1e-5) * sg_ref[...] * (1 - lam_init)
        o_ref[...] = o.astype(BF16)

    if ncb > 0:
        @pl.when(qi < ncb)
        def _():
            attend(n_ctx)

    @pl.when(qi >= ncb)
    def _():
        attend(k_ref.shape[0])


def _attention(qkv, lam_vec, sub_g, *, tq, n_ctx, lam_init):
    b_n, l_n, d3 = qkv.shape
    d = d3 // 3
    hw = 2 * DA_HEAD
    nh = d // hw
    return pl.pallas_call(
        functools.partial(_attn_kernel, ncb=n_ctx // tq, n_ctx=n_ctx, lam_init=lam_init),
        grid=(b_n, nh, l_n // tq),
        in_specs=[pl.BlockSpec((None, tq, hw), lambda b, h, t: (b, t, h)),
                  pl.BlockSpec((None, l_n, hw), lambda b, h, t: (b, 0, nh + h)),
                  pl.BlockSpec((None, l_n, hw), lambda b, h, t: (b, 0, 2 * nh + h)),
                  pl.BlockSpec((4, DA_HEAD), lambda b, h, t: (0, 0)),
                  pl.BlockSpec((1, hw), lambda b, h, t: (0, 0))],
        out_specs=pl.BlockSpec((None, tq, hw), lambda b, h, t: (b, t, h)),
        out_shape=jax.ShapeDtypeStruct((b_n, l_n, d), BF16),
        compiler_params=_cparams(("parallel", "parallel", "arbitrary"), 48),
        name="diff_attn",
    )(qkv, qkv, qkv, lam_vec, sub_g.reshape(1, hw))


def _rope_tables(n_ctx, n_lat):
    n_rows = n_lat // GRID_W
    row = jnp.repeat(jnp.arange(n_rows, dtype=F32), GRID_W)
    col = jnp.tile(jnp.arange(GRID_W, dtype=F32), n_rows)
    nf = DA_HEAD // 4
    inv_freq = ROPE_BASE ** (-jnp.arange(nf, dtype=F32) / nf)
    ang_r, ang_c = row[:, None] * inv_freq, col[:, None] * inv_freq
    ang = jnp.concatenate([ang_r, ang_r, ang_c, ang_c], axis=-1)
    ang = jnp.concatenate([jnp.zeros((n_ctx, DA_HEAD), F32), ang], axis=0)
    cos, sin = jnp.cos(ang), jnp.sin(ang)
    even_q = (jnp.arange(DA_HEAD) // nf) % 2 == 0
    return cos, jnp.where(even_q, -sin, 0.0), jnp.where(even_q, 0.0, sin)


def _diff_attention_layer(h, n_ctx, layer_idx, w_qkv, lam_vec, sub_g, w_o, *, tr):
    b_n, l_n, d = h.shape
    m_n = b_n * l_n
    qkv = _mm2(h.reshape(m_n, d), w_qkv.astype(BF16)).reshape(b_n, l_n, 3 * d)
    cos, s_a, s_b = _rope_tables(n_ctx, l_n - n_ctx)
    qkv = _rope(qkv, cos, s_a, s_b, tr=tr)
    lam_init = 0.8 - 0.6 * math.exp(-0.3 * layer_idx)
    o = _attention(qkv, lam_vec, sub_g, tq=tr, n_ctx=n_ctx, lam_init=lam_init)
    return _mm2(o.reshape(m_n, d), w_o.astype(BF16)).reshape(b_n, l_n, d)


def _hgrn_kernel(q_ref, i_ref, g_ref, ff_ref, fb_ref, low_ref, ng_ref, o_ref, o_scr, *, n_ctx, layer_idx):
    t_n = CHUNK
    l_n = q_ref.shape[0]
    nc, ncc = l_n // t_n, n_ctx // t_n
    r64 = lax.broadcasted_iota(jnp.int32, (t_n, t_n), 0)
    c64 = lax.broadcasted_iota(jnp.int32, (t_n, t_n), 1)

    def rows_of(c):
        return pl.ds(pl.multiple_of(c * t_n, t_n), t_n)

    f_refs = (ff_ref, fb_ref)
    lbs, incl, ltri_b = [], [], []
    for d in (0, 1):
        low = low_ref[d]
        e = jnp.exp(low - jnp.max(low, axis=0, keepdims=True))
        sm = e / jnp.sum(e, axis=0, keepdims=True)
        cs = sm[0:1]
        for rr in range(1, layer_idx + 1):
            cs = cs + sm[rr:rr + 1]
        lbs.append(cs - sm[0:1])
        incl.append((c64 >= r64) if d == 1 else (c64 <= r64))
        ltri_b.append(jnp.where(incl[d], 1.0, 0.0).astype(BF16))
    group = 2 if (nc % 2 == 0 and ncc % 2 == 0) else 1

    def stage_cum(d, c):
        rows = rows_of(c)
        f = lbs[d] + (1.0 - lbs[d]) * _sigmoid(f_refs[d][rows, :])
        return dict(d=d, rows=rows, f=f, cum=_mdot([ltri_b[d]], _parts(jnp.log(f), 3), order=3))

    def stage_att(s):
        d, cum, rows = s["d"], s["cum"], s["rows"]
        b_end = cum[0:1, :] if d == 1 else cum[t_n - 1:t_n, :]
        qv = q_ref[rows, :]
        qd = (qv * _sigmoid(qv) * jnp.exp(cum)).astype(BF16)
        kk = 1.0 - s["f"]
        v = i_ref[rows, :]
        kd = (kk * jnp.exp(-cum)).astype(BF16)
        ke = (kk * jnp.exp(b_end - cum)).astype(BF16)
        return dict(d=d, rows=rows, qd=qd, vb=v.astype(BF16), dec=jnp.exp(b_end),
                    att=lax.dot_general(qd, kd, NT, preferred_element_type=F32),
                    upd=jnp.dot(v.T.astype(BF16), ke, preferred_element_type=F32))

    def stage_intra(s):
        att = jnp.where(incl[s["d"]], s["att"], 0.0).astype(BF16)
        return dict(s, o=jnp.dot(att, s["vb"], preferred_element_type=F32))

    def body(i, states):
        items = [(d, _chunk_of(i * group + g, ncc, nc, d == 1)) for g in range(group) for d in (0, 1)]
        sts = [stage_cum(d, c) for d, c in items]
        sts = [stage_att(s) for s in sts]
        sts = [stage_intra(s) for s in sts]
        states = list(states)
        for s in sts:
            d = s["d"]
            o = s["o"] + lax.dot_general(s["qd"], states[d].astype(BF16), NT, preferred_element_type=F32)
            o_scr[d, s["rows"], :] = o
            states[d] = states[d] * s["dec"] + s["upd"]
        return tuple(states)

    zero = jnp.zeros((LANES, LANES), F32)
    lax.fori_loop(0, nc // group, body, (zero, zero))

    def post(c, carry):
        rows = rows_of(c)
        o = o_scr[0, rows, :] + o_scr[1, rows, :]
        o = o * lax.rsqrt(jnp.mean(o * o, axis=-1, keepdims=True) + 1e-5) * ng_ref[...]
        gv = g_ref[rows, :]
        o_ref[rows, :] = (o * (gv * _sigmoid(gv))).astype(BF16)
        return carry

    lax.fori_loop(0, nc, post, 0)


def _hgrn2_layer(h, n_ctx, layer_idx, w_in, lower, norm_g, w_o):
    b_n, l_n, d = h.shape
    m_n = b_n * l_n
    nh = d // HG_EXPAND
    proj = _mm2(h.reshape(m_n, d), w_in.astype(BF16)).reshape(b_n, l_n, 5 * d)
    col = lambda n: pl.BlockSpec((None, l_n, LANES), lambda b, p, n=n: (b, 0, n * nh + p))
    o = pl.pallas_call(
        functools.partial(_hgrn_kernel, n_ctx=n_ctx, layer_idx=layer_idx),
        grid=(b_n, nh),
        in_specs=[col(0), col(1), col(2), col(3), col(4),
                  pl.BlockSpec((2, lower.shape[1], LANES), lambda b, p: (0, 0, p)),
                  pl.BlockSpec((1, LANES), lambda b, p: (0, 0))],
        out_specs=pl.BlockSpec((None, l_n, LANES), lambda b, p: (b, 0, p)),
        out_shape=jax.ShapeDtypeStruct((b_n, l_n, d), BF16),
        scratch_shapes=[pltpu.VMEM((2, l_n, LANES), F32)],
        compiler_params=_cparams(("parallel", "parallel"), 40),
        name="hgrn_scan",
    )(proj, proj, proj, proj, proj, lower, norm_g.reshape(1, LANES))
    return _mm2(o.reshape(m_n, d), w_o.astype(BF16)).reshape(b_n, l_n, d)


def _gelu_tanh(x):
    return 0.5 * x * (1.0 + jnp.tanh(math.sqrt(2.0 / math.pi) * (x + 0.044715 * (x * x * x))))


def _softplus(x):
    return jnp.maximum(x, 0.0) + jnp.log1p(jnp.exp(-jnp.abs(x)))


def _lin_scan(a_ref, u_ref, hl_s, cp_s, h_s, row0, n, rev, h_in, accumulate):
    seg = n // 8
    n_p = a_ref.shape[0]

    def step(i, carry):
        t = (seg - 1 - i) if rev else i
        idx = pl.ds(row0 + t, 8, stride=seg)
        out = []
        for j in range(n_p):
            hl, cp = carry[j]
            a = a_ref[j, idx, :]
            hl = a * hl + u_ref[j, idx, :]
            cp = a * cp
            hl_s[j, idx, :] = hl
            cp_s[j, idx, :] = cp
            out.append((hl, cp))
        return tuple(out)

    init = tuple((jnp.zeros((8, LANES), F32), jnp.ones((8, LANES), F32)) for _ in range(n_p))
    ends = lax.fori_loop(0, seg, step, init)
    order = range(7, -1, -1) if rev else range(8)
    h_out = []
    for j in range(n_p):
        hl_e, cp_e = ends[j]
        carry = h_in[j]
        for s in order:
            r0 = row0 + s * seg
            blk = hl_s[j, r0:r0 + seg, :] + cp_s[j, r0:r0 + seg, :] * carry
            if accumulate:
                h_s[j, r0:r0 + seg, :] += blk
            else:
                h_s[j, r0:r0 + seg, :] = blk
            carry = hl_e[s:s + 1, :] + cp_e[s:s + 1, :] * carry
        h_out.append(carry)
    return h_out


def _rglru_kernel(gb_ref, xb_ref, cw_ref, cb_ref, wg_ref, bg_ref, lam_ref, o_ref, a_s, u_s, h_s, hl_s, cp_s, *,
                  n_ctx):
    l_n = xb_ref.shape[0]
    n_lat = l_n - n_ctx
    x = xb_ref[...]
    row = lax.broadcasted_iota(jnp.int32, x.shape, 0)
    k_w = cw_ref.shape[0]
    xc = cb_ref[...] + sum(_seg_shift(x, row, j - (k_w - 1) // 2, n_ctx) * cw_ref[j:j + 1, :]
                           for j in range(k_w))
    xcb = xc.astype(BF16)
    n_p = x.shape[1] // LANES
    for d in (0, 1):
        gate = lambda g: _sigmoid(jnp.dot(xcb, wg_ref[d, g].astype(BF16), preferred_element_type=F32)
                                  + bg_ref[d, g:g + 1, :])
        log_a = -LR_C * gate(0) * _softplus(-lam_ref[d:d + 1, :])
        a = jnp.exp(log_a)
        u = jnp.sqrt(jnp.tanh(-log_a) * (jnp.exp(2.0 * log_a) + 1.0)) * gate(1) * xc
        for j in range(n_p):
            a_s[j] = a[:, j * LANES:(j + 1) * LANES]
            u_s[j] = u[:, j * LANES:(j + 1) * LANES]
        h = [jnp.zeros((1, LANES), F32)] * n_p
        for row0, n in ((0, n_ctx), (n_ctx, n_lat)):
            if n:
                h = _lin_scan(a_s, u_s, hl_s, cp_s, h_s, row0, n, d == 1, h, d == 1)
    for j in range(n_p):
        cols = slice(j * LANES, (j + 1) * LANES)
        o_ref[:, cols] = (h_s[j, n_ctx:, :] * _gelu_tanh(gb_ref[n_ctx:, cols])).astype(BF16)


def _rglru_layer(h, n_ctx, w_in, conv_w, conv_b, w_gate, b_gate, lam, w_o):
    b_n, l_n, d = h.shape
    n_lat = l_n - n_ctx
    nb = d // LR_BS
    proj = _mm2(h.reshape(b_n * l_n, d), w_in.astype(BF16)).reshape(b_n, l_n, 2 * d)
    k_w = conv_w.shape[0]
    o = pl.pallas_call(
        functools.partial(_rglru_kernel, n_ctx=n_ctx),
        grid=(b_n, nb),
        in_specs=[pl.BlockSpec((None, l_n, LR_BS), lambda b, j: (b, 0, j)),
                  pl.BlockSpec((None, l_n, LR_BS), lambda b, j: (b, 0, nb + j)),
                  pl.BlockSpec((k_w, LR_BS), lambda b, j: (0, j)),
                  pl.BlockSpec((1, LR_BS), lambda b, j: (0, j)),
                  pl.BlockSpec((2, 2, None, LR_BS, LR_BS), lambda b, j: (0, 0, j, 0, 0)),
                  pl.BlockSpec((2, 2, LR_BS), lambda b, j: (0, 0, j)),
                  pl.BlockSpec((2, LR_BS), lambda b, j: (0, j))],
        out_specs=pl.BlockSpec((None, n_lat, LR_BS), lambda b, j: (b, 0, j)),
        out_shape=jax.ShapeDtypeStruct((b_n, n_lat, d), BF16),
        scratch_shapes=[pltpu.VMEM((LR_BS // LANES, l_n, LANES), F32)] * 5,
        compiler_params=_cparams(("parallel", "parallel"), 56),
        name="rglru",
    )(proj, proj, conv_w, conv_b.reshape(1, d), w_gate, b_gate, lam)
    return _mm2(o.reshape(b_n * n_lat, d), w_o.astype(BF16)).reshape(b_n, n_lat, d)


def _ffn_act_kernel(ug_ref, uv_ref, wg_ref, wv_ref, bg_ref, bv_ref, o_ref, *, n_ctx):
    row = lax.broadcasted_iota(jnp.int32, ug_ref.shape, 0)

    def conv(u_ref, w_ref, b_ref):
        u = u_ref[...]
        k_w = w_ref.shape[0]
        return b_ref[...] + sum(_seg_shift(u, row, j - (k_w - 1) // 2, n_ctx) * w_ref[j:j + 1, :]
                                for j in range(k_w))
    gate = conv(ug_ref, wg_ref, bg_ref)
    o_ref[...] = (gate * _sigmoid(gate) * conv(uv_ref, wv_ref, bv_ref)).astype(BF16)


def _conv_ffn(h, n_ctx, w_up, conv_w, conv_b, w_down):
    b_n, l_n, d = h.shape
    m_n = b_n * l_n
    f = w_down.shape[0]
    u = _mm2(h.reshape(m_n, d), w_up.astype(BF16)).reshape(b_n, l_n, 2 * f)
    tc = _pick(f, (256, 128))
    nf = f // tc
    k_w = conv_w.shape[0]
    cb = conv_b.reshape(1, 2 * f)
    act = pl.pallas_call(
        functools.partial(_ffn_act_kernel, n_ctx=n_ctx),
        grid=(b_n, nf),
        in_specs=[pl.BlockSpec((None, l_n, tc), lambda b, j: (b, 0, j)),
                  pl.BlockSpec((None, l_n, tc), lambda b, j: (b, 0, nf + j)),
                  pl.BlockSpec((k_w, tc), lambda b, j: (0, j)),
                  pl.BlockSpec((k_w, tc), lambda b, j: (0, nf + j)),
                  pl.BlockSpec((1, tc), lambda b, j: (0, j)),
                  pl.BlockSpec((1, tc), lambda b, j: (0, nf + j))],
        out_specs=pl.BlockSpec((None, l_n, tc), lambda b, j: (b, 0, j)),
        out_shape=jax.ShapeDtypeStruct((b_n, l_n, f), BF16),
        compiler_params=_cparams(("parallel", "parallel"), 48),
        name="ffn_act",
    )(u, u, conv_w, conv_w, cb, cb)
    return _mm2(act.reshape(m_n, f), w_down.astype(BF16)).reshape(b_n, l_n, d)


def kernel(x, c, ctx, c_ctx, ada_w, ada_b, ln_g, ln_b, ffn_w_up, ffn_conv_w, ffn_conv_b, ffn_w_down, rw_mu, rw_w_rkv, rw_w0, rw_w1, rw_w2, rw_a0, rw_a1, rw_a2, rw_g1, rw_g2, rw_k_k, rw_k_a, rw_r_k, rw_gn_g, rw_gn_b, rw_w_o, da_w_qkv, da_lambda, da_sub_g, da_w_o, hg_w_in, hg_lower, hg_norm_g, hg_w_o, lr_w_in, lr_conv_w, lr_conv_b, lr_w_gate, lr_b_gate, lr_lambda, lr_w_o):
    b_n, n_lat, d = x.shape
    n_ctx = ctx.shape[1]
    depth = ada_w.shape[0]
    assert depth == 4 and rw_mu.shape[0] == 1, "one occurrence of each of the four mixers"
    assert b_n + 1 <= 8 and n_ctx % CHUNK == 0 and n_lat % CHUNK == 0
    tr = math.gcd(math.gcd(n_ctx, n_lat), 256)
    alpha = (2 * depth) ** 0.25

    c8 = jnp.concatenate([c, c_ctx[None], jnp.zeros((8 - b_n - 1, d), F32)], axis=0)
    m = _ada(c8, ada_w, ada_b)
    m_lat = m[:, :b_n].reshape(depth, b_n, 1, 6, d)
    m_ctx = jnp.broadcast_to(m[:, b_n].reshape(depth, 1, 1, 6, d), (depth, b_n, 1, 6, d))
    mod = jnp.concatenate([m_ctx, m_lat], axis=2)

    z = jnp.concatenate([ctx, x], axis=1)
    h = None
    for i in range(depth):
        last = i == depth - 1
        if i == 0:
            y = _rwkv7_layer(z, mod[0], n_ctx, rw_mu[0], rw_w_rkv[0], rw_w0[0], rw_w1[0], rw_w2[0], rw_a0[0],
                             rw_a1[0], rw_a2[0], rw_g1[0], rw_g2[0], rw_k_k[0], rw_k_a[0], rw_r_k[0],
                             rw_gn_g[0], rw_gn_b[0], rw_w_o[0])
        elif i == 1:
            y = _diff_attention_layer(h, n_ctx, i, da_w_qkv[0], da_lambda[0], da_sub_g[0], da_w_o[0], tr=tr)
        elif i == 2:
            y = _hgrn2_layer(h, n_ctx, i, hg_w_in[0], hg_lower, hg_norm_g[0], hg_w_o[0])
        else:
            y = _rglru_layer(h, n_ctx, lr_w_in[0], lr_conv_w[0], lr_conv_b[0], lr_w_gate[0], lr_b_gate[0],
                             lr_lambda[0], lr_w_o[0])
        z, h = _ln_mod(z, y, mod[i], ln_g[i, 0], ln_b[i, 0], mod[i], gate_j=2, mod_j=3, tr=tr,
                       n_ctx=n_ctx, alpha=alpha)
        if last:
            n_ctx = 0
        y = _conv_ffn(h, n_ctx, ffn_w_up[i], ffn_conv_w[i], ffn_conv_b[i], ffn_w_down[i])
        z, h = _ln_mod(z, y, mod[i], ln_g[i, 1], ln_b[i, 1], mod[min(i + 1, depth - 1)], gate_j=5,
                       mod_j=None if last else 0, tr=tr, n_ctx=n_ctx, alpha=alpha)
    return z
```

```python
import functools
import math

import jax
import jax.numpy as jnp
from jax import lax
from jax.experimental import pallas as pl
from jax.experimental.pallas import tpu as pltpu

F32, BF16 = jnp.float32, jnp.bfloat16

LANES = 128
CHUNK = 64
LN_EPS = 1e-5
GRID_W = 64
ROPE_BASE = 10000.0
RW_HEAD = 64
RW_DECAY_SCALE = 0.606531
RW_GN_EPS = 64e-5
DA_HEAD = 128
HG_EXPAND = 128
LR_BS = 256
LR_C = 8.0
MIB = 1024 * 1024


def _pick(n, cands):
    for c in cands:
        if n % c == 0:
            return c
    return n


def _cparams(sem, vmem_mib):
    return pltpu.CompilerParams(dimension_semantics=sem, vmem_limit_bytes=vmem_mib * MIB)


def _sigmoid(x):
    return jax.nn.sigmoid(x)


NN = (((1,), (0,)), ((), ()))
NT = (((1,), (1,)), ((), ()))


def _parts(x, n):
    out = []
    for i in range(n):
        p = x.astype(BF16)
        out.append(p)
        if i + 1 < n:
            x = x - p.astype(F32)
    return out


def _mdot(ap, bp, dims=NN, order=2):
    pairs = [(a, b) for i, a in enumerate(ap) for j, b in enumerate(bp) if i + j < order]
    (ca,), (cb,) = dims[0]
    lhs = jnp.concatenate([a for a, _ in pairs], axis=ca) if len(pairs) > 1 else pairs[0][0]
    rhs = jnp.concatenate([b for _, b in pairs], axis=cb) if len(pairs) > 1 else pairs[0][1]
    return lax.dot_general(lhs, rhs, dims, preferred_element_type=F32)


def _cumsum_matrix(t_n, rev):
    r = lax.broadcasted_iota(jnp.int32, (t_n, 3 * t_n), 0)
    c = lax.broadcasted_iota(jnp.int32, (t_n, 3 * t_n), 1) % t_n
    return jnp.where((c >= r) if rev else (c <= r), 1.0, 0.0).astype(BF16)


def _cumsum(tri3_b, x):
    return jnp.dot(tri3_b, jnp.concatenate(_parts(x, 3), axis=0), preferred_element_type=F32)


def _cat_parts(xs, axis):
    return [jnp.concatenate(ps, axis=axis) for ps in zip(*xs)]


def _mm_kernel(a_ref, w_ref, o_ref, acc_ref, *, nk):
    prod = jnp.dot(a_ref[...], w_ref[...], preferred_element_type=F32)
    if nk == 1:
        o_ref[...] = prod.astype(o_ref.dtype)
    else:
        k = pl.program_id(3)

        @pl.when(k == 0)
        def _():
            acc_ref[...] = prod

        @pl.when(k > 0)
        def _():
            acc_ref[...] += prod

        @pl.when(k == nk - 1)
        def _():
            o_ref[...] = acc_ref[...].astype(o_ref.dtype)


def _mm_wres_kernel(a_ref, w_ref, o_ref, wb_ref):
    @pl.when(pl.program_id(2) == 0)
    def _():
        wb_ref[...] = w_ref[...].astype(BF16)

    o_ref[...] = jnp.dot(a_ref[...], wb_ref[...], preferred_element_type=F32).astype(o_ref.dtype)


def _matmul(a, w, *, out_dtype=F32, a_off=0):
    g_n, k_n, n_n = w.shape
    m_n = a.shape[1]
    tm = _pick(m_n, (1024, 512, 256, 128, 64))
    tn = _pick(n_n, (1024, 512, 256, 128))
    tk = k_n if k_n <= 2048 else _pick(k_n, (2816, 2048, 1024, 512))
    nk = k_n // tk
    if w.dtype == F32:
        assert nk == 1
        return pl.pallas_call(
            _mm_wres_kernel,
            grid=(g_n, n_n // tn, m_n // tm),
            in_specs=[pl.BlockSpec((None, tm, k_n), lambda g, j, i: (g + a_off, i, 0)),
                      pl.BlockSpec((None, k_n, tn), lambda g, j, i: (g, 0, j))],
            out_specs=pl.BlockSpec((None, tm, tn), lambda g, j, i: (g, i, j)),
            out_shape=jax.ShapeDtypeStruct((g_n, m_n, n_n), out_dtype),
            scratch_shapes=[pltpu.VMEM((k_n, tn), BF16)],
            compiler_params=_cparams(("parallel", "parallel", "arbitrary"), 48),
            name="matmul_wres",
        )(a, w)
    return pl.pallas_call(
        functools.partial(_mm_kernel, nk=nk),
        grid=(g_n, m_n // tm, n_n // tn, nk),
        in_specs=[pl.BlockSpec((None, tm, tk), lambda g, i, j, k: (g + a_off, i, k)),
                  pl.BlockSpec((None, tk, tn), lambda g, i, j, k: (g, k, j))],
        out_specs=pl.BlockSpec((None, tm, tn), lambda g, i, j, k: (g, i, j)),
        out_shape=jax.ShapeDtypeStruct((g_n, m_n, n_n), out_dtype),
        scratch_shapes=[pltpu.VMEM((tm, tn), F32)],
        compiler_params=_cparams(("parallel", "parallel", "parallel", "arbitrary"), 48),
        name="matmul",
    )(a, w)


def _mm2(a, w, **kw):
    return _matmul(a[None], w[None], **kw)[0]


def _ada_kernel(c_ref, w_ref, b_ref, o_ref):
    c = c_ref[...]
    s = (c * _sigmoid(c)).astype(BF16)
    o_ref[...] = jnp.dot(s, w_ref[...].astype(BF16), preferred_element_type=F32) + b_ref[...]


def _ada(c8, ada_w, ada_b):
    depth, d, n = ada_w.shape
    tn = _pick(n, (1024, 512, 256, 128))
    return pl.pallas_call(
        _ada_kernel,
        grid=(depth, n // tn),
        in_specs=[pl.BlockSpec((8, d), lambda l, j: (0, 0)),
                  pl.BlockSpec((None, d, tn), lambda l, j: (l, 0, j)),
                  pl.BlockSpec((None, 1, tn), lambda l, j: (l, 0, j))],
        out_specs=pl.BlockSpec((None, 8, tn), lambda l, j: (l, 0, j)),
        out_shape=jax.ShapeDtypeStruct((depth, 8, n), F32),
        compiler_params=_cparams(("parallel", "parallel"), 40),
        name="ada",
    )(c8, ada_w, ada_b.reshape(depth, 1, n))


def _ln_mod_kernel(z_ref, y_ref, mod_ref, g_ref, b_ref, mod2_ref, *out_refs, gate_j, mod_j, alpha):
    m = mod_ref[...]
    zz = alpha * z_ref[...] + y_ref[...] * m[gate_j:gate_j + 1]
    mu = jnp.mean(zz, axis=-1, keepdims=True)
    zc = zz - mu
    var = jnp.mean(zc * zc, axis=-1, keepdims=True)
    zn = zc * lax.rsqrt(var + LN_EPS) * g_ref[...] + b_ref[...]
    out_refs[0][...] = zn
    if mod_j is not None:
        m2 = mod2_ref[...]
        out_refs[1][...] = (zn * (1 + m2[mod_j + 1:mod_j + 2]) + m2[mod_j:mod_j + 1]).astype(BF16)


def _ln_mod(z, y, mod, ln_g, ln_b, mod2, *, gate_j, mod_j, tr, n_ctx, alpha):
    b_n, l_z, d = z.shape
    l_y = y.shape[1]
    z_off = (l_z - l_y) // tr
    ncb = (n_ctx - (l_z - l_y)) // tr
    seg = lambda b, t: (b, jnp.where(t < ncb, 0, 1), 0, 0)
    row = pl.BlockSpec((None, tr, d), lambda b, t: (b, t, 0))
    out_shape = [jax.ShapeDtypeStruct((b_n, l_y, d), F32)]
    out_specs = [row]
    if mod_j is not None:
        out_shape.append(jax.ShapeDtypeStruct((b_n, l_y, d), BF16))
        out_specs.append(row)
    res = pl.pallas_call(
        functools.partial(_ln_mod_kernel, gate_j=gate_j, mod_j=mod_j, alpha=alpha),
        grid=(b_n, l_y // tr),
        in_specs=[pl.BlockSpec((None, tr, d), lambda b, t: (b, t + z_off, 0)),
                  row,
                  pl.BlockSpec((None, None, 6, d), seg),
                  pl.BlockSpec((1, d), lambda b, t: (0, 0)),
                  pl.BlockSpec((1, d), lambda b, t: (0, 0)),
                  pl.BlockSpec((None, None, 6, d), seg)],
        out_specs=out_specs,
        out_shape=out_shape,
        compiler_params=_cparams(("parallel", "parallel"), 40),
        name="ln_mod",
    )(z, y, mod, ln_g.reshape(1, d), ln_b.reshape(1, d), mod2)
    return res if mod_j is not None else (res[0], None)


def _seg_shift(x, row, shift, n_ctx):
    l_n = x.shape[0]
    rolled = pltpu.roll(x, (-shift) % l_n, 0)
    src = row + shift
    same_seg = (src >= 0) & (src < l_n) & ((src < n_ctx) == (row < n_ctx))
    return jnp.where(same_seg, rolled, 0.0)


def _rw_mix_kernel(z_ref, mod_ref, mu_ref, o_ref, *, n_ctx):
    z = z_ref[...]
    row = lax.broadcasted_iota(jnp.int32, z.shape, 0)
    is_ctx = row < n_ctx
    shift = jnp.where(is_ctx, mod_ref[0, 0:1, :], mod_ref[1, 0:1, :])
    scale = jnp.where(is_ctx, mod_ref[0, 1:2, :], mod_ref[1, 1:2, :])
    h = z * (1 + scale) + shift
    dx = 0.5 * (_seg_shift(h, row, -1, n_ctx) + _seg_shift(h, row, 1, n_ctx)) - h
    for n in range(6):
        o_ref[n] = (h + dx * mu_ref[n:n + 1, :]).astype(BF16)


def _rw_mix(z, mod, mu, *, n_ctx):
    b_n, l_n, d = z.shape
    tc = _pick(d, (256, 128))
    return pl.pallas_call(
        functools.partial(_rw_mix_kernel, n_ctx=n_ctx),
        grid=(b_n, d // tc),
        in_specs=[pl.BlockSpec((None, l_n, tc), lambda b, j: (b, 0, j)),
                  pl.BlockSpec((None, 2, 6, tc), lambda b, j: (b, 0, 0, j)),
                  pl.BlockSpec((6, tc), lambda b, j: (0, j))],
        out_specs=pl.BlockSpec((6, None, l_n, tc), lambda b, j: (0, b, 0, j)),
        out_shape=jax.ShapeDtypeStruct((6, b_n, l_n, d), BF16),
        compiler_params=_cparams(("parallel", "parallel"), 48),
        name="rw_mix",
    )(z, mod, mu)


def _lora_kernel(x_ref, a_ref, b_ref, o_ref, *, act):
    t = jnp.dot(x_ref[...], a_ref[...], preferred_element_type=F32)
    if act == "tanh":
        t = jnp.tanh(t)
    elif act == "sigmoid":
        t = _sigmoid(t)
    o_ref[...] = jnp.dot(t.astype(BF16), b_ref[...], preferred_element_type=F32)


def _lora(xs, x_idx, a, b, act):
    g_n, d, r = a.shape
    m_n = xs.shape[1]
    tm = _pick(m_n, (512, 256, 128, 64))
    return pl.pallas_call(
        functools.partial(_lora_kernel, act=act),
        grid=(g_n, m_n // tm),
        in_specs=[pl.BlockSpec((None, tm, d), lambda g, i: (x_idx, i, 0)),
                  pl.BlockSpec((None, d, r), lambda g, i: (g, 0, 0)),
                  pl.BlockSpec((None, r, d), lambda g, i: (g, 0, 0))],
        out_specs=pl.BlockSpec((None, tm, d), lambda g, i: (g, i, 0)),
        out_shape=jax.ShapeDtypeStruct((g_n, m_n, d), F32),
        compiler_params=_cparams(("parallel", "parallel"), 40),
        name="lora",
    )(xs, a, b)


def _chunk_of(q, ncc, nc, rev):
    if not rev:
        return q
    return jnp.where(q < ncc, ncc - 1 - q, nc - 1 - (q - ncc))


def _rwkv_kernel(r_ref, k_ref, v_ref, lw_ref, la_ref, g_ref, w0_ref, a0_ref, kk_ref, ka_ref, rk_ref,
                 gng_ref, gnb_ref, o_ref,
                 y_scr, mr_s, n_s, *, n_ctx):
    t_n = CHUNK
    h2 = 2 * t_n
    l_n = r_ref.shape[0]
    nc, ncc = l_n // t_n, n_ctx // t_n
    group = _pick(nc, (3, 2, 1))
    lane = lax.broadcasted_iota(jnp.int32, (1, LANES), 1)
    m1 = jnp.where(lane < RW_HEAD, 1.0, 0.0)
    m2 = 1.0 - m1
    ri = lax.broadcasted_iota(jnp.int32, (LANES, LANES), 0)
    ci = lax.broadcasted_iota(jnp.int32, (LANES, LANES), 1)
    same_head = (ri // RW_HEAD) == (ci // RW_HEAD)
    gsum_b = jnp.where(same_head, 1.0, 0.0).astype(BF16)
    gavg_b = jnp.where(same_head, 1.0 / RW_HEAD, 0.0).astype(BF16)
    eye = jnp.where(ri == ci, 1.0, 0.0)
    tr_i, tc_i = ri % t_n, ci % t_n
    k_k, k_a = kk_ref[...], ka_ref[...]

    def stack(x):
        return jnp.concatenate([x * m1, x * m2], axis=0)

    def rows_of(c):
        return pl.ds(pl.multiple_of(c * t_n, t_n), t_n)

    def head_sum(x, w_b):
        return _mdot(_parts(x, 3), [w_b], order=3)

    tri3_b = [_cumsum_matrix(t_n, rev) for rev in (False, True)]
    strict = [(tc_i > tr_i) if rev else (tc_i < tr_i) for rev in (False, True)]
    incl = [(tc_i >= tr_i) if rev else (tc_i <= tr_i) for rev in (False, True)]

    def stage_prep(d, c):
        rows = rows_of(c)
        k, r, v = k_ref[rows, :], r_ref[rows, :], v_ref[rows, :]
        lw = -RW_DECAY_SCALE * _sigmoid(w0_ref[d:d + 1, :] + lw_ref[d, rows, :])
        a = _sigmoid(a0_ref[d:d + 1, :] + la_ref[d, rows, :])
        kkr = k * k_k
        return dict(d=d, c=c, k=k, r=r, v=v, lw=lw, a=a, kkr=kkr,
                    ss=head_sum(kkr * kkr, gsum_b), cum=_cumsum(tri3_b[d], lw))

    def stage_amat(s):
        d, cum, lw, a = s["d"], s["cum"], s["lw"], s["a"]
        kk = s["kkr"] * lax.rsqrt(s["ss"] + 1e-12)
        kd = s["k"] * (1 + (a - 1) * k_a)
        bv = kk * a
        p_end = cum[0:1, :] if d == 1 else cum[t_n - 1:t_n, :]
        e_m = jnp.exp(-cum)
        e_h = jnp.exp(p_end - cum)
        ktp = _parts(stack(kk * jnp.exp(cum - lw)), 2)
        rt = stack(s["r"] * jnp.exp(cum))
        k2p = _cat_parts([_parts(stack(bv * e_m), 2), _parts(stack(kd * e_m), 2)], 0)
        return dict(d=d, c=s["c"], ktp=ktp, rt=rt, vp=_parts(stack(s["v"]), 1), p_end=p_end,
                    bh=stack(bv * e_h), kh=stack(kd * e_h),
                    amat=_mdot(_cat_parts([ktp, _parts(rt, 2)], 0), k2p, NT))

    def stage_square(s):
        d, amat = s["d"], s["amat"]
        lt = jnp.where(strict[d], amat[:h2, :h2], 0.0).T
        ltp = _parts(lt, 2)
        msk = jnp.concatenate([jnp.where(strict[d], amat[:h2, h2:], 0.0),
                               jnp.where(incl[d], amat[h2:, h2:], 0.0)], axis=0)
        s = dict(s, pt=eye - lt, xt=_mdot(ltp, ltp),
                 av=_mdot(_parts(msk, 2), s["vp"][:1]),
                 arbp=_parts(jnp.where(incl[d], amat[h2:, :h2], 0.0), 2))
        del s["amat"]
        return s

    def stage_double(s, final):
        xh = _parts(s["xt"], 1)
        if final:
            return dict(s, pt=s["pt"] + _mdot(xh, _parts(s["pt"], 2)))
        both = _mdot(xh, _parts(jnp.concatenate([s["pt"], s["xt"]], axis=1), 2))
        return dict(s, pt=s["pt"] + both[:, :LANES], xt=both[:, LANES:])

    def stage_solve(s):
        rhs = jnp.concatenate([s["ktp"][0], (-s["av"][:h2]).astype(BF16)], axis=1)
        return dict(s, wub=_mdot(_parts(s["pt"].T, 2), [rhs]).astype(BF16))

    def stage_fold(s):
        d, c, wub = s["d"], s["c"], s["wub"]
        aw = _mdot(s["arbp"], [wub])
        zb = jnp.zeros((h2, LANES), BF16)
        lhs = _cat_parts([_parts(s["bh"].T, 2), _parts(s["kh"].T, 2)], 1)
        rhs = jnp.concatenate([wub, jnp.concatenate([zb, s["vp"][0]], axis=1)], axis=0)
        mn = _mdot(lhs, [rhs])
        dg = jnp.where(ri == ci, jnp.broadcast_to(jnp.exp(s["p_end"]), (LANES, LANES)), 0.0)
        rp = s["rt"] - aw[:, :LANES]
        mrp = _parts(jnp.concatenate([dg - mn[:, :LANES], rp[:t_n] + rp[t_n:]], axis=0), 2)
        for i in range(2):
            mr_s[d, c, i] = mrp[i]
        n_s[d, c] = mn[:, LANES:]
        y0 = s["av"][h2:] + aw[:, LANES:]
        return y0[:t_n] + y0[t_n:]

    def local(i, carry):
        sts = [stage_prep(d, i * group + g) for g in range(group) for d in (0, 1)]
        sts = [stage_amat(s) for s in sts]
        sts = [stage_square(s) for s in sts]
        for step in range(5):
            sts = [stage_double(s, step == 4) for s in sts]
        sts = [stage_solve(s) for s in sts]
        y0 = [stage_fold(s) for s in sts]
        for g in range(group):
            y_scr[rows_of(i * group + g), :] = y0[2 * g] + y0[2 * g + 1]
        return carry

    lax.fori_loop(0, nc // group, local, 0)

    def seq(q, hs):
        cs = (q, _chunk_of(q, ncc, nc, True))
        mh = [_mdot([mr_s[d, cs[d], 0], mr_s[d, cs[d], 1]], _parts(hs[d], 2)) for d in (0, 1)]
        for d in (0, 1):
            y_scr[rows_of(cs[d]), :] += mh[d][h2:]
        return tuple(mh[d][:h2] + n_s[d, cs[d]] for d in (0, 1))

    zero = jnp.zeros((LANES, LANES), F32)
    lax.fori_loop(0, nc, seq, (zero, zero))

    n_post = _pick(nc, (4, 3, 2, 1))

    def post(i, carry):
        rows = [rows_of(i * n_post + g) for g in range(n_post)]

        def bonus_sum(rw):
            k, r = k_ref[rw, :], r_ref[rw, :]
            kd_f = k * (1 + (_sigmoid(a0_ref[0:1, :] + la_ref[0, rw, :]) - 1) * k_a)
            kd_b = k * (1 + (_sigmoid(a0_ref[1:2, :] + la_ref[1, rw, :]) - 1) * k_a)
            return head_sum(r * (kd_f + kd_b) * rk_ref[...], gsum_b)
        bsum = [bonus_sum(rw) for rw in rows]
        ys = [y_scr[rw, :] for rw in rows]
        ycs = [y - m for y, m in zip(ys, [head_sum(y, gavg_b) for y in ys])]
        var = [head_sum(yc * yc, gavg_b) for yc in ycs]
        for rw, yc, vr, bs in zip(rows, ycs, var, bsum):
            yn = yc * lax.rsqrt(vr + RW_GN_EPS) * gng_ref[...] + gnb_ref[...]
            o_ref[rw, :] = ((yn + bs * v_ref[rw, :]) * g_ref[rw, :]).astype(BF16)
        return carry

    lax.fori_loop(0, nc // n_post, post, 0)


def _rwkv_scan(rkv, lw, la, g, w0, a0, k_k, k_a, r_k, gn_g, gn_b, *, n_ctx):
    _, b_n, l_n, d = rkv.shape
    nc = l_n // CHUNK
    col = lambda n: pl.BlockSpec((None, None, l_n, LANES), lambda b, p, n=n: (n, b, 0, p))
    two = pl.BlockSpec((2, None, l_n, LANES), lambda b, p: (0, b, 0, p))
    par = lambda rows: pl.BlockSpec((rows, LANES), lambda b, p: (0, p))
    return pl.pallas_call(
        functools.partial(_rwkv_kernel, n_ctx=n_ctx),
        grid=(b_n, d // LANES),
        in_specs=[col(0), col(1), col(2), two, two,
                  pl.BlockSpec((None, l_n, LANES), lambda b, p: (b, 0, p)),
                  par(2), par(2), par(1), par(1), par(1), par(1), par(1)],
        out_specs=pl.BlockSpec((None, l_n, LANES), lambda b, p: (b, 0, p)),
        out_shape=jax.ShapeDtypeStruct((b_n, l_n, d), BF16),
        scratch_shapes=[pltpu.VMEM((l_n, LANES), F32),
                        pltpu.VMEM((2, nc, 2, LANES + CHUNK, LANES), BF16),
                        pltpu.VMEM((2, nc, LANES, LANES), F32)],
        compiler_params=_cparams(("parallel", "parallel"), 56),
        name="rwkv_scan",
    )(rkv, rkv, rkv, lw, la, g, w0, a0, k_k.reshape(1, d), k_a.reshape(1, d), r_k.reshape(1, d),
      gn_g.reshape(1, d), gn_b.reshape(1, d))


def _pad_axis(w, axis, to):
    pad = [(0, 0)] * w.ndim
    pad[axis] = (0, to - w.shape[axis])
    return jnp.pad(w, pad)


def _rwkv7_layer(z, mod, n_ctx, mu, w_rkv, w0, w1, w2, a0, a1, a2, g1, g2, k_k, k_a, r_k, gn_g, gn_b, w_o):
    b_n, l_n, d = z.shape
    m_n = b_n * l_n
    xs = _rw_mix(z, mod, mu, n_ctx=n_ctx).reshape(6, m_n, d)
    rkv = _matmul(xs, w_rkv)
    r_w = -(-w1.shape[-1] // LANES) * LANES
    r_a = -(-a1.shape[-1] // LANES) * LANES
    lw = _lora(xs, 3, _pad_axis(w1, 2, r_w).astype(BF16), _pad_axis(w2, 1, r_w).astype(BF16), "tanh")
    la = _lora(xs, 4, _pad_axis(a1, 2, r_a).astype(BF16), _pad_axis(a2, 1, r_a).astype(BF16), None)
    gate = _lora(xs, 5, g1[None].astype(BF16), g2[None].astype(BF16), "sigmoid")
    o = _rwkv_scan(rkv.reshape(3, b_n, l_n, d), lw.reshape(2, b_n, l_n, d), la.reshape(2, b_n, l_n, d),
                   gate.reshape(b_n, l_n, d), w0, a0, k_k, k_a, r_k, gn_g, gn_b, n_ctx=n_ctx)
    return _mm2(o.reshape(m_n, d), w_o).reshape(b_n, l_n, d)


def _rope_kernel(x_ref, cos_ref, sa_ref, sb_ref, o_ref):
    j = pl.program_id(2)
    x = x_ref[...]
    d = x.shape[1]

    @pl.when(j < 2)
    def _():
        rep = d // DA_HEAD
        cos = jnp.tile(cos_ref[...], (1, rep))
        s_a = jnp.tile(sa_ref[...], (1, rep))
        s_b = jnp.tile(sb_ref[...], (1, rep))
        q = DA_HEAD // 4
        q_scale = jnp.where(j == 0, DA_HEAD ** -0.5 * math.log2(math.e), 1.0)
        rot = x * cos + pltpu.roll(x, d - q, 1) * s_a + pltpu.roll(x, q, 1) * s_b
        o_ref[...] = (rot * q_scale).astype(BF16)

    @pl.when(j == 2)
    def _():
        o_ref[...] = x.astype(BF16)


def _rope(qkv, cos, s_a, s_b, *, tr):
    b_n, l_n, d3 = qkv.shape
    d = d3 // 3
    tab = pl.BlockSpec((tr, DA_HEAD), lambda b, t, j: (t, 0))
    return pl.pallas_call(
        _rope_kernel,
        grid=(b_n, l_n // tr, 3),
        in_specs=[pl.BlockSpec((None, tr, d), lambda b, t, j: (b, t, j)), tab, tab, tab],
        out_specs=pl.BlockSpec((None, tr, d), lambda b, t, j: (b, t, j)),
        out_shape=jax.ShapeDtypeStruct((b_n, l_n, d3), BF16),
        compiler_params=_cparams(("parallel", "parallel", "parallel"), 40),
        name="rope",
    )(qkv, cos, s_a, s_b)


def _attn_kernel(q_ref, k_ref, v_ref, lam_ref, sg_ref, o_ref, *, ncb, n_ctx, lam_init):
    qi = pl.program_id(2)
    lv = lam_ref[...]
    lam = (jnp.exp(jnp.sum(lv[0:1] * lv[1:2], axis=-1, keepdims=True))
           - jnp.exp(jnp.sum(lv[2:3] * lv[3:4], axis=-1, keepdims=True)) + lam_init)

    def attend(nk):
        def probs(m):
            q = q_ref[:, m * DA_HEAD:(m + 1) * DA_HEAD]
            k = k_ref[0:nk, m * DA_HEAD:(m + 1) * DA_HEAD]
            s = lax.dot_general(q, k, NT, preferred_element_type=F32)
            e = jnp.exp2(s - jnp.max(s, axis=-1, keepdims=True))
            return e, 1.0 / jnp.sum(e, axis=-1, keepdims=True)
        e0, i0 = probs(0)
        e1, i1 = probs(1)
        v = v_ref[0:nk, :]
        o = (jnp.dot(e0.astype(BF16), v, preferred_element_type=F32) * i0
             - jnp.dot(e1.astype(BF16), v, preferred_element_type=F32) * (lam * i1))
        o = o * lax.rsqrt(jnp.mean(o * o, axis=-1, keepdims=True) + 1e-5) * sg_ref[...] * (1 - lam_init)
        o_ref[...] = o.astype(BF16)

    if ncb > 0:
        @pl.when(qi < ncb)
        def _():
            attend(n_ctx)

    @pl.when(qi >= ncb)
    def _():
        attend(k_ref.shape[0])


def _attention(qkv, lam_vec, sub_g, *, tq, n_ctx, lam_init):
    b_n, l_n, d3 = qkv.shape
    d = d3 // 3
    hw = 2 * DA_HEAD
    nh = d // hw
    return pl.pallas_call(
        functools.partial(_attn_kernel, ncb=n_ctx // tq, n_ctx=n_ctx, lam_init=lam_init),
        grid=(b_n, nh, l_n // tq),
        in_specs=[pl.BlockSpec((None, tq, hw), lambda b, h, t: (b, t, h)),
                  pl.BlockSpec((None, l_n, hw), lambda b, h, t: (b, 0, nh + h)),
                  pl.BlockSpec((None, l_n, hw), lambda b, h, t: (b, 0, 2 * nh + h)),
                  pl.BlockSpec((4, DA_HEAD), lambda b, h, t: (0, 0)),
                  pl.BlockSpec((1, hw), lambda b, h, t: (0, 0))],
        out_specs=pl.BlockSpec((None, tq, hw), lambda b, h, t: (b, t, h)),
        out_shape=jax.ShapeDtypeStruct((b_n, l_n, d), BF16),
        compiler_params=_cparams(("parallel", "parallel", "arbitrary"), 48),
        name="diff_attn",
    )(qkv, qkv, qkv, lam_vec, sub_g.reshape(1, hw))


def _rope_tables(n_ctx, n_lat):
    n_rows = n_lat // GRID_W
    row = jnp.repeat(jnp.arange(n_rows, dtype=F32), GRID_W)
    col = jnp.tile(jnp.arange(GRID_W, dtype=F32), n_rows)
    nf = DA_HEAD // 4
    inv_freq = ROPE_BASE ** (-jnp.arange(nf, dtype=F32) / nf)
    ang_r, ang_c = row[:, None] * inv_freq, col[:, None] * inv_freq
    ang = jnp.concatenate([ang_r, ang_r, ang_c, ang_c], axis=-1)
    ang = jnp.concatenate([jnp.zeros((n_ctx, DA_HEAD), F32), ang], axis=0)
    cos, sin = jnp.cos(ang), jnp.sin(ang)
    even_q = (jnp.arange(DA_HEAD) // nf) % 2 == 0
    return cos, jnp.where(even_q, -sin, 0.0), jnp.where(even_q, 0.0, sin)


def _diff_attention_layer(h, n_ctx, layer_idx, w_qkv, lam_vec, sub_g, w_o, *, tr):
    b_n, l_n, d = h.shape
    m_n = b_n * l_n
    qkv = _mm2(h.reshape(m_n, d), w_qkv).reshape(b_n, l_n, 3 * d)
    cos, s_a, s_b = _rope_tables(n_ctx, l_n - n_ctx)
    qkv = _rope(qkv, cos, s_a, s_b, tr=tr)
    lam_init = 0.8 - 0.6 * math.exp(-0.3 * layer_idx)
    o = _attention(qkv, lam_vec, sub_g, tq=tr, n_ctx=n_ctx, lam_init=lam_init)
    return _mm2(o.reshape(m_n, d), w_o).reshape(b_n, l_n, d)


def _hgrn_kernel(q_ref, i_ref, g_ref, ff_ref, fb_ref, low_ref, ng_ref, o_ref, o_scr, *, n_ctx, layer_idx):
    t_n = CHUNK
    l_n = q_ref.shape[0]
    nc, ncc = l_n // t_n, n_ctx // t_n
    r64 = lax.broadcasted_iota(jnp.int32, (t_n, t_n), 0)
    c64 = lax.broadcasted_iota(jnp.int32, (t_n, t_n), 1)

    def rows_of(c):
        return pl.ds(pl.multiple_of(c * t_n, t_n), t_n)

    f_refs = (ff_ref, fb_ref)
    lbs, incl = [], []
    tri3_b = [_cumsum_matrix(t_n, rev) for rev in (False, True)]
    for d in (0, 1):
        low = low_ref[d]
        e = jnp.exp(low - jnp.max(low, axis=0, keepdims=True))
        sm = e / jnp.sum(e, axis=0, keepdims=True)
        cs = sm[0:1]
        for rr in range(1, layer_idx + 1):
            cs = cs + sm[rr:rr + 1]
        lbs.append(cs - sm[0:1])
        incl.append((c64 >= r64) if d == 1 else (c64 <= r64))
    group = _pick(nc, (4, 2, 1))

    def stage_cum(d, c):
        rows = rows_of(c)
        f = lbs[d] + (1.0 - lbs[d]) * _sigmoid(f_refs[d][rows, :])
        return dict(d=d, rows=rows, f=f, cum=_cumsum(tri3_b[d], jnp.log(f)))

    def stage_att(s):
        d, cum, rows = s["d"], s["cum"], s["rows"]
        b_end = cum[0:1, :] if d == 1 else cum[t_n - 1:t_n, :]
        qv = q_ref[rows, :]
        qd = (qv * _sigmoid(qv) * jnp.exp(cum)).astype(BF16)
        kk = 1.0 - s["f"]
        v = i_ref[rows, :]
        kd = (kk * jnp.exp(-cum)).astype(BF16)
        ke = (kk * jnp.exp(b_end - cum)).astype(BF16)
        return dict(d=d, rows=rows, qd=qd, vb=v.astype(BF16), dec=jnp.exp(b_end),
                    att=lax.dot_general(qd, kd, NT, preferred_element_type=F32),
                    upd=jnp.dot(v.T.astype(BF16), ke, preferred_element_type=F32))

    def stage_intra(s):
        att = jnp.where(incl[s["d"]], s["att"], 0.0).astype(BF16)
        return dict(s, o=jnp.dot(att, s["vb"], preferred_element_type=F32))

    def body(i, states):
        items = [(d, _chunk_of(i * group + g, ncc, nc, d == 1)) for g in range(group) for d in (0, 1)]
        sts = [stage_cum(d, c) for d, c in items]
        sts = [stage_att(s) for s in sts]
        sts = [stage_intra(s) for s in sts]
        states = list(states)
        for s in sts:
            d = s["d"]
            o = s["o"] + lax.dot_general(s["qd"], states[d].astype(BF16), NT, preferred_element_type=F32)
            o_scr[d, s["rows"], :] = o
            states[d] = states[d] * s["dec"] + s["upd"]
        return tuple(states)

    zero = jnp.zeros((LANES, LANES), F32)
    lax.fori_loop(0, nc // group, body, (zero, zero))

    def post(c, carry):
        rows = rows_of(c)
        o = o_scr[0, rows, :] + o_scr[1, rows, :]
        o = o * lax.rsqrt(jnp.mean(o * o, axis=-1, keepdims=True) + 1e-5) * ng_ref[...]
        gv = g_ref[rows, :]
        o_ref[rows, :] = (o * (gv * _sigmoid(gv))).astype(BF16)
        return carry

    lax.fori_loop(0, nc, post, 0)


def _hgrn2_layer(h, n_ctx, layer_idx, w_in, lower, norm_g, w_o):
    b_n, l_n, d = h.shape
    m_n = b_n * l_n
    nh = d // HG_EXPAND
    proj = _mm2(h.reshape(m_n, d), w_in).reshape(b_n, l_n, 5 * d)
    col = lambda n: pl.BlockSpec((None, l_n, LANES), lambda b, p, n=n: (b, 0, n * nh + p))
    o = pl.pallas_call(
        functools.partial(_hgrn_kernel, n_ctx=n_ctx, layer_idx=layer_idx),
        grid=(b_n, nh),
        in_specs=[col(0), col(1), col(2), col(3), col(4),
                  pl.BlockSpec((2, lower.shape[1], LANES), lambda b, p: (0, 0, p)),
                  pl.BlockSpec((1, LANES), lambda b, p: (0, 0))],
        out_specs=pl.BlockSpec((None, l_n, LANES), lambda b, p: (b, 0, p)),
        out_shape=jax.ShapeDtypeStruct((b_n, l_n, d), BF16),
        scratch_shapes=[pltpu.VMEM((2, l_n, LANES), F32)],
        compiler_params=_cparams(("parallel", "parallel"), 40),
        name="hgrn_scan",
    )(proj, proj, proj, proj, proj, lower, norm_g.reshape(1, LANES))
    return _mm2(o.reshape(m_n, d), w_o).reshape(b_n, l_n, d)


def _gelu_tanh(x):
    return 0.5 * x * (1.0 + jnp.tanh(math.sqrt(2.0 / math.pi) * (x + 0.044715 * (x * x * x))))


def _softplus(x):
    return jnp.maximum(x, 0.0) + jnp.log1p(jnp.exp(-jnp.abs(x)))


SEG_PAD = 8


def _lin_scan(a_ref, u_ref, hl_s, cp_s, h_s, base, row0, n, rev, h_in, accumulate):
    seg = n // 8
    stride = seg + SEG_PAD
    n_p = a_ref.shape[0]

    def step(i, carry):
        t = (seg - 1 - i) if rev else i
        idx = pl.ds(base + t, 8, stride=stride)
        out = []
        for j in range(n_p):
            hl, cp = carry[j]
            a = a_ref[j, idx, :]
            hl = a * hl + u_ref[j, idx, :]
            cp = a * cp
            hl_s[j, idx, :] = hl
            cp_s[j, idx, :] = cp
            out.append((hl, cp))
        return tuple(out)

    init = tuple((jnp.zeros((8, LANES), F32), jnp.ones((8, LANES), F32)) for _ in range(n_p))
    ends = lax.fori_loop(0, seg, step, init)
    order = range(7, -1, -1) if rev else range(8)
    h_out = []
    for j in range(n_p):
        hl_e, cp_e = ends[j]
        carry = h_in[j]
        for s in order:
            r0, p0 = row0 + s * seg, base + s * stride
            blk = hl_s[j, p0:p0 + seg, :] + cp_s[j, p0:p0 + seg, :] * carry
            if accumulate:
                h_s[j, r0:r0 + seg, :] += blk
            else:
                h_s[j, r0:r0 + seg, :] = blk
            carry = hl_e[s:s + 1, :] + cp_e[s:s + 1, :] * carry
        h_out.append(carry)
    return h_out


def _rglru_kernel(gb_ref, xb_ref, cw_ref, cb_ref, wg_ref, bg_ref, lam_ref, o_ref, a_s, u_s, h_s, hl_s, cp_s, *,
                  n_ctx):
    l_n = xb_ref.shape[0]
    n_lat = l_n - n_ctx
    x = xb_ref[...]
    row = lax.broadcasted_iota(jnp.int32, x.shape, 0)
    k_w = cw_ref.shape[0]
    xc = cb_ref[...] + sum(_seg_shift(x, row, j - (k_w - 1) // 2, n_ctx) * cw_ref[j:j + 1, :]
                           for j in range(k_w))
    xcb = xc.astype(BF16)
    n_p = x.shape[1] // LANES
    for d in (0, 1):
        gate = lambda g: _sigmoid(jnp.dot(xcb, wg_ref[d, g].astype(BF16), preferred_element_type=F32)
                                  + bg_ref[d, g:g + 1, :])
        log_a = -LR_C * gate(0) * _softplus(-lam_ref[d:d + 1, :])
        a = jnp.exp(log_a)
        u = jnp.sqrt(jnp.tanh(-log_a) * (jnp.exp(2.0 * log_a) + 1.0)) * gate(1) * xc
        h = [jnp.zeros((1, LANES), F32)] * n_p
        base = 0
        for row0, n in ((0, n_ctx), (n_ctx, n_lat)):
            if n:
                seg = n // 8
                for j in range(n_p):
                    for s in range(8):
                        src = slice(row0 + s * seg, row0 + (s + 1) * seg)
                        dst = slice(base + s * (seg + SEG_PAD), base + s * (seg + SEG_PAD) + seg)
                        a_s[j, dst, :] = a[src, j * LANES:(j + 1) * LANES]
                        u_s[j, dst, :] = u[src, j * LANES:(j + 1) * LANES]
                h = _lin_scan(a_s, u_s, hl_s, cp_s, h_s, base, row0, n, d == 1, h, d == 1)
                base += 8 * (seg + SEG_PAD)
    for j in range(n_p):
        cols = slice(j * LANES, (j + 1) * LANES)
        o_ref[:, cols] = (h_s[j, n_ctx:l_n, :] * _gelu_tanh(gb_ref[n_ctx:, cols])).astype(BF16)


def _rglru_layer(h, n_ctx, w_in, conv_w, conv_b, w_gate, b_gate, lam, w_o):
    b_n, l_n, d = h.shape
    n_lat = l_n - n_ctx
    nb = d // LR_BS
    proj = _mm2(h.reshape(b_n * l_n, d), w_in).reshape(b_n, l_n, 2 * d)
    k_w = conv_w.shape[0]
    o = pl.pallas_call(
        functools.partial(_rglru_kernel, n_ctx=n_ctx),
        grid=(b_n, nb),
        in_specs=[pl.BlockSpec((None, l_n, LR_BS), lambda b, j: (b, 0, j)),
                  pl.BlockSpec((None, l_n, LR_BS), lambda b, j: (b, 0, nb + j)),
                  pl.BlockSpec((k_w, LR_BS), lambda b, j: (0, j)),
                  pl.BlockSpec((1, LR_BS), lambda b, j: (0, j)),
                  pl.BlockSpec((2, 2, None, LR_BS, LR_BS), lambda b, j: (0, 0, j, 0, 0)),
                  pl.BlockSpec((2, 2, LR_BS), lambda b, j: (0, 0, j)),
                  pl.BlockSpec((2, LR_BS), lambda b, j: (0, j))],
        out_specs=pl.BlockSpec((None, n_lat, LR_BS), lambda b, j: (b, 0, j)),
        out_shape=jax.ShapeDtypeStruct((b_n, n_lat, d), BF16),
        scratch_shapes=[pltpu.VMEM((LR_BS // LANES, l_n + 16 * SEG_PAD, LANES), F32)] * 5,
        compiler_params=_cparams(("parallel", "parallel"), 56),
        name="rglru",
    )(proj, proj, conv_w, conv_b.reshape(1, d), w_gate, b_gate, lam)
    return _mm2(o.reshape(b_n * n_lat, d), w_o).reshape(b_n, n_lat, d)


def _ffn_act_kernel(ug_ref, uv_ref, wg_ref, wv_ref, bg_ref, bv_ref, o_ref, *, n_ctx):
    row = lax.broadcasted_iota(jnp.int32, ug_ref.shape, 0)

    def conv(u_ref, w_ref, b_ref):
        u = u_ref[...]
        k_w = w_ref.shape[0]
        return b_ref[...] + sum(_seg_shift(u, row, j - (k_w - 1) // 2, n_ctx) * w_ref[j:j + 1, :]
                                for j in range(k_w))
    gate = conv(ug_ref, wg_ref, bg_ref)
    o_ref[...] = (gate * _sigmoid(gate) * conv(uv_ref, wv_ref, bv_ref)).astype(BF16)


def _conv_ffn(h, n_ctx, w_up, conv_w, conv_b, w_down):
    b_n, l_n, d = h.shape
    m_n = b_n * l_n
    f = w_down.shape[0]
    u = _mm2(h.reshape(m_n, d), w_up).reshape(b_n, l_n, 2 * f)
    tc = _pick(f, (256, 128))
    nf = f // tc
    k_w = conv_w.shape[0]
    cb = conv_b.reshape(1, 2 * f)
    act = pl.pallas_call(
        functools.partial(_ffn_act_kernel, n_ctx=n_ctx),
        grid=(b_n, nf),
        in_specs=[pl.BlockSpec((None, l_n, tc), lambda b, j: (b, 0, j)),
                  pl.BlockSpec((None, l_n, tc), lambda b, j: (b, 0, nf + j)),
                  pl.BlockSpec((k_w, tc), lambda b, j: (0, j)),
                  pl.BlockSpec((k_w, tc), lambda b, j: (0, nf + j)),
                  pl.BlockSpec((1, tc), lambda b, j: (0, j)),
                  pl.BlockSpec((1, tc), lambda b, j: (0, nf + j))],
        out_specs=pl.BlockSpec((None, l_n, tc), lambda b, j: (b, 0, j)),
        out_shape=jax.ShapeDtypeStruct((b_n, l_n, f), BF16),
        compiler_params=_cparams(("parallel", "parallel"), 48),
        name="ffn_act",
    )(u, u, conv_w, conv_w, cb, cb)
    return _mm2(act.reshape(m_n, f), w_down.astype(BF16)).reshape(b_n, l_n, d)


def kernel(x, c, ctx, c_ctx, ada_w, ada_b, ln_g, ln_b, ffn_w_up, ffn_conv_w, ffn_conv_b, ffn_w_down, rw_mu, rw_w_rkv, rw_w0, rw_w1, rw_w2, rw_a0, rw_a1, rw_a2, rw_g1, rw_g2, rw_k_k, rw_k_a, rw_r_k, rw_gn_g, rw_gn_b, rw_w_o, da_w_qkv, da_lambda, da_sub_g, da_w_o, hg_w_in, hg_lower, hg_norm_g, hg_w_o, lr_w_in, lr_conv_w, lr_conv_b, lr_w_gate, lr_b_gate, lr_lambda, lr_w_o):
    b_n, n_lat, d = x.shape
    n_ctx = ctx.shape[1]
    depth = ada_w.shape[0]
    assert depth == 4 and rw_mu.shape[0] == 1, "one occurrence of each of the four mixers"
    assert b_n + 1 <= 8 and n_ctx % CHUNK == 0 and n_lat % CHUNK == 0
    tr = math.gcd(math.gcd(n_ctx, n_lat), 256)
    alpha = (2 * depth) ** 0.25

    c8 = jnp.concatenate([c, c_ctx[None], jnp.zeros((8 - b_n - 1, d), F32)], axis=0)
    m = _ada(c8, ada_w, ada_b)
    m_lat = m[:, :b_n].reshape(depth, b_n, 1, 6, d)
    m_ctx = jnp.broadcast_to(m[:, b_n].reshape(depth, 1, 1, 6, d), (depth, b_n, 1, 6, d))
    mod = jnp.concatenate([m_ctx, m_lat], axis=2)

    z = jnp.concatenate([ctx, x], axis=1)
    h = None
    for i in range(depth):
        last = i == depth - 1
        if i == 0:
            y = _rwkv7_layer(z, mod[0], n_ctx, rw_mu[0], rw_w_rkv[0], rw_w0[0], rw_w1[0], rw_w2[0], rw_a0[0],
                             rw_a1[0], rw_a2[0], rw_g1[0], rw_g2[0], rw_k_k[0], rw_k_a[0], rw_r_k[0],
                             rw_gn_g[0], rw_gn_b[0], rw_w_o[0])
        elif i == 1:
            y = _diff_attention_layer(h, n_ctx, i, da_w_qkv[0], da_lambda[0], da_sub_g[0], da_w_o[0], tr=tr)
        elif i == 2:
            y = _hgrn2_layer(h, n_ctx, i, hg_w_in[0], hg_lower, hg_norm_g[0], hg_w_o[0])
        else:
            y = _rglru_layer(h, n_ctx, lr_w_in[0], lr_conv_w[0], lr_conv_b[0], lr_w_gate[0], lr_b_gate[0],
                             lr_lambda[0], lr_w_o[0])
        z, h = _ln_mod(z, y, mod[i], ln_g[i, 0], ln_b[i, 0], mod[i], gate_j=2, mod_j=3, tr=tr,
                       n_ctx=n_ctx, alpha=alpha)
        if last:
            n_ctx = 0
        y = _conv_ffn(h, n_ctx, ffn_w_up[i], ffn_conv_w[i], ffn_conv_b[i], ffn_w_down[i])
        z, h = _ln_mod(z, y, mod[i], ln_g[i, 1], ln_b[i, 1], mod[min(i + 1, depth - 1)], gate_j=5,
                       mod_j=None if last else 0, tr=tr, n_ctx=n_ctx, alpha=alpha)
    return z
```

```python
import functools
import math

import jax
import jax.numpy as jnp
from jax import lax
from jax.experimental import pallas as pl
from jax.experimental.pallas import tpu as pltpu

F32, BF16 = jnp.float32, jnp.bfloat16

LANES = 128
CHUNK = 64
LN_EPS = 1e-5
GRID_W = 64
ROPE_BASE = 10000.0
RW_HEAD = 64
RW_DECAY_SCALE = 0.606531
RW_GN_EPS = 64e-5
DA_HEAD = 128
HG_EXPAND = 128
LR_BS = 256
LR_C = 8.0
MIB = 1024 * 1024


def _pick(n, cands):
    for c in cands:
        if n % c == 0:
            return c
    return n


def _cparams(sem, vmem_mib):
    return pltpu.CompilerParams(dimension_semantics=sem, vmem_limit_bytes=vmem_mib * MIB)


def _sigmoid(x):
    return jax.nn.sigmoid(x)


NN = (((1,), (0,)), ((), ()))
NT = (((1,), (1,)), ((), ()))


def _parts(x, n):
    out = []
    for i in range(n):
        p = x.astype(BF16)
        out.append(p)
        if i + 1 < n:
            x = x - p.astype(F32)
    return out


def _mdot(ap, bp, dims=NN, order=2):
    pairs = [(a, b) for i, a in enumerate(ap) for j, b in enumerate(bp) if i + j < order]
    (ca,), (cb,) = dims[0]
    lhs = jnp.concatenate([a for a, _ in pairs], axis=ca) if len(pairs) > 1 else pairs[0][0]
    rhs = jnp.concatenate([b for _, b in pairs], axis=cb) if len(pairs) > 1 else pairs[0][1]
    return lax.dot_general(lhs, rhs, dims, preferred_element_type=F32)


def _cumsum_matrix(t_n, rev):
    r = lax.broadcasted_iota(jnp.int32, (t_n, 3 * t_n), 0)
    c = lax.broadcasted_iota(jnp.int32, (t_n, 3 * t_n), 1) % t_n
    return jnp.where((c >= r) if rev else (c <= r), 1.0, 0.0).astype(BF16)


def _cumsum(tri3_b, x):
    return jnp.dot(tri3_b, jnp.concatenate(_parts(x, 3), axis=0), preferred_element_type=F32)


def _cat_parts(xs, axis):
    return [jnp.concatenate(ps, axis=axis) for ps in zip(*xs)]


def _mm_kernel(a_ref, w_ref, o_ref, acc_ref, *, nk):
    prod = jnp.dot(a_ref[...], w_ref[...], preferred_element_type=F32)
    if nk == 1:
        o_ref[...] = prod.astype(o_ref.dtype)
    else:
        k = pl.program_id(3)

        @pl.when(k == 0)
        def _():
            acc_ref[...] = prod

        @pl.when(k > 0)
        def _():
            acc_ref[...] += prod

        @pl.when(k == nk - 1)
        def _():
            o_ref[...] = acc_ref[...].astype(o_ref.dtype)


def _mm_wres_kernel(a_ref, w_ref, o_ref, wb_ref):
    i = pl.program_id(2)
    k_n = w_ref.shape[0]
    n_kc = 4 if k_n % (4 * LANES) == 0 else 1

    @pl.when(i == 0)
    def _():
        kc = k_n // n_kc
        acc = None
        for c in range(n_kc):
            ks = slice(c * kc, (c + 1) * kc)
            wb = w_ref[ks, :].astype(BF16)
            wb_ref[ks, :] = wb
            t = jnp.dot(a_ref[:, ks], wb, preferred_element_type=F32)
            acc = t if acc is None else acc + t
        o_ref[...] = acc.astype(o_ref.dtype)

    @pl.when(i > 0)
    def _():
        o_ref[...] = jnp.dot(a_ref[...], wb_ref[...], preferred_element_type=F32).astype(o_ref.dtype)


def _matmul(a, w, *, out_dtype=F32, a_off=0):
    g_n, k_n, n_n = w.shape
    m_n = a.shape[1]
    tm = _pick(m_n, (1024, 512, 256, 128, 64))
    tn = _pick(n_n, (1024, 512, 256, 128))
    tk = k_n if k_n <= 2048 else _pick(k_n, (2816, 2048, 1024, 512))
    nk = k_n // tk
    if w.dtype == F32:
        assert nk == 1
        return pl.pallas_call(
            _mm_wres_kernel,
            grid=(g_n, n_n // tn, m_n // tm),
            in_specs=[pl.BlockSpec((None, tm, k_n), lambda g, j, i: (g + a_off, i, 0)),
                      pl.BlockSpec((None, k_n, tn), lambda g, j, i: (g, 0, j))],
            out_specs=pl.BlockSpec((None, tm, tn), lambda g, j, i: (g, i, j)),
            out_shape=jax.ShapeDtypeStruct((g_n, m_n, n_n), out_dtype),
            scratch_shapes=[pltpu.VMEM((k_n, tn), BF16)],
            compiler_params=_cparams(("parallel", "parallel", "arbitrary"), 48),
            name="matmul_wres",
        )(a, w)
    return pl.pallas_call(
        functools.partial(_mm_kernel, nk=nk),
        grid=(g_n, m_n // tm, n_n // tn, nk),
        in_specs=[pl.BlockSpec((None, tm, tk), lambda g, i, j, k: (g + a_off, i, k)),
                  pl.BlockSpec((None, tk, tn), lambda g, i, j, k: (g, k, j))],
        out_specs=pl.BlockSpec((None, tm, tn), lambda g, i, j, k: (g, i, j)),
        out_shape=jax.ShapeDtypeStruct((g_n, m_n, n_n), out_dtype),
        scratch_shapes=[pltpu.VMEM((tm, tn), F32)],
        compiler_params=_cparams(("parallel", "parallel", "parallel", "arbitrary"), 48),
        name="matmul",
    )(a, w)


def _mm2(a, w, **kw):
    return _matmul(a[None], w[None], **kw)[0]


def _ada_kernel(c_ref, w_ref, b_ref, o_ref):
    c = c_ref[...]
    s = (c * _sigmoid(c)).astype(BF16)
    o_ref[...] = jnp.dot(s, w_ref[...].astype(BF16), preferred_element_type=F32) + b_ref[...]


def _ada(c8, ada_w, ada_b):
    depth, d, n = ada_w.shape
    tn = _pick(n, (1024, 512, 256, 128))
    return pl.pallas_call(
        _ada_kernel,
        grid=(depth, n // tn),
        in_specs=[pl.BlockSpec((8, d), lambda l, j: (0, 0)),
                  pl.BlockSpec((None, d, tn), lambda l, j: (l, 0, j)),
                  pl.BlockSpec((None, 1, tn), lambda l, j: (l, 0, j))],
        out_specs=pl.BlockSpec((None, 8, tn), lambda l, j: (l, 0, j)),
        out_shape=jax.ShapeDtypeStruct((depth, 8, n), F32),
        compiler_params=_cparams(("parallel", "parallel"), 40),
        name="ada",
    )(c8, ada_w, ada_b.reshape(depth, 1, n))


def _ln_mod_kernel(z_ref, y_ref, mod_ref, g_ref, b_ref, mod2_ref, *out_refs, gate_j, mod_j, alpha):
    m = mod_ref[...]
    zz = alpha * z_ref[...] + y_ref[...] * m[gate_j:gate_j + 1]
    mu = jnp.mean(zz, axis=-1, keepdims=True)
    zc = zz - mu
    var = jnp.mean(zc * zc, axis=-1, keepdims=True)
    zn = zc * lax.rsqrt(var + LN_EPS) * g_ref[...] + b_ref[...]
    out_refs[0][...] = zn
    if mod_j is not None:
        m2 = mod2_ref[...]
        out_refs[1][...] = (zn * (1 + m2[mod_j + 1:mod_j + 2]) + m2[mod_j:mod_j + 1]).astype(BF16)


def _ln_mod(z, y, mod, ln_g, ln_b, mod2, *, gate_j, mod_j, tr, n_ctx, alpha):
    b_n, l_z, d = z.shape
    l_y = y.shape[1]
    z_off = (l_z - l_y) // tr
    ncb = (n_ctx - (l_z - l_y)) // tr
    seg = lambda b, t: (b, jnp.where(t < ncb, 0, 1), 0, 0)
    row = pl.BlockSpec((None, tr, d), lambda b, t: (b, t, 0))
    out_shape = [jax.ShapeDtypeStruct((b_n, l_y, d), F32)]
    out_specs = [row]
    if mod_j is not None:
        out_shape.append(jax.ShapeDtypeStruct((b_n, l_y, d), BF16))
        out_specs.append(row)
    res = pl.pallas_call(
        functools.partial(_ln_mod_kernel, gate_j=gate_j, mod_j=mod_j, alpha=alpha),
        grid=(b_n, l_y // tr),
        in_specs=[pl.BlockSpec((None, tr, d), lambda b, t: (b, t + z_off, 0)),
                  row,
                  pl.BlockSpec((None, None, 6, d), seg),
                  pl.BlockSpec((1, d), lambda b, t: (0, 0)),
                  pl.BlockSpec((1, d), lambda b, t: (0, 0)),
                  pl.BlockSpec((None, None, 6, d), seg)],
        out_specs=out_specs,
        out_shape=out_shape,
        compiler_params=_cparams(("parallel", "parallel"), 40),
        name="ln_mod",
    )(z, y, mod, ln_g.reshape(1, d), ln_b.reshape(1, d), mod2)
    return res if mod_j is not None else (res[0], None)


def _seg_shift(x, row, shift, n_ctx):
    l_n = x.shape[0]
    rolled = pltpu.roll(x, (-shift) % l_n, 0)
    src = row + shift
    same_seg = (src >= 0) & (src < l_n) & ((src < n_ctx) == (row < n_ctx))
    return jnp.where(same_seg, rolled, 0.0)


def _rw_mix_kernel(z_ref, mod_ref, mu_ref, o_ref, *, n_ctx):
    z = z_ref[...]
    row = lax.broadcasted_iota(jnp.int32, z.shape, 0)
    is_ctx = row < n_ctx
    shift = jnp.where(is_ctx, mod_ref[0, 0:1, :], mod_ref[1, 0:1, :])
    scale = jnp.where(is_ctx, mod_ref[0, 1:2, :], mod_ref[1, 1:2, :])
    h = z * (1 + scale) + shift
    dx = 0.5 * (_seg_shift(h, row, -1, n_ctx) + _seg_shift(h, row, 1, n_ctx)) - h
    for n in range(6):
        o_ref[n] = (h + dx * mu_ref[n:n + 1, :]).astype(BF16)


def _rw_mix(z, mod, mu, *, n_ctx):
    b_n, l_n, d = z.shape
    tc = _pick(d, (256, 128))
    return pl.pallas_call(
        functools.partial(_rw_mix_kernel, n_ctx=n_ctx),
        grid=(b_n, d // tc),
        in_specs=[pl.BlockSpec((None, l_n, tc), lambda b, j: (b, 0, j)),
                  pl.BlockSpec((None, 2, 6, tc), lambda b, j: (b, 0, 0, j)),
                  pl.BlockSpec((6, tc), lambda b, j: (0, j))],
        out_specs=pl.BlockSpec((6, None, l_n, tc), lambda b, j: (0, b, 0, j)),
        out_shape=jax.ShapeDtypeStruct((6, b_n, l_n, d), BF16),
        compiler_params=_cparams(("parallel", "parallel"), 48),
        name="rw_mix",
    )(z, mod, mu)


def _lora_kernel(x_ref, a_ref, b_ref, o_ref, *, act):
    t = jnp.dot(x_ref[...], a_ref[...], preferred_element_type=F32)
    if act == "tanh":
        t = jnp.tanh(t)
    elif act == "sigmoid":
        t = _sigmoid(t)
    o_ref[...] = jnp.dot(t.astype(BF16), b_ref[...], preferred_element_type=F32)


def _lora(xs, x_idx, a, b, act):
    g_n, d, r = a.shape
    m_n = xs.shape[1]
    tm = _pick(m_n, (512, 256, 128, 64))
    return pl.pallas_call(
        functools.partial(_lora_kernel, act=act),
        grid=(g_n, m_n // tm),
        in_specs=[pl.BlockSpec((None, tm, d), lambda g, i: (x_idx, i, 0)),
                  pl.BlockSpec((None, d, r), lambda g, i: (g, 0, 0)),
                  pl.BlockSpec((None, r, d), lambda g, i: (g, 0, 0))],
        out_specs=pl.BlockSpec((None, tm, d), lambda g, i: (g, i, 0)),
        out_shape=jax.ShapeDtypeStruct((g_n, m_n, d), F32),
        compiler_params=_cparams(("parallel", "parallel"), 40),
        name="lora",
    )(xs, a, b)


def _chunk_of(q, ncc, nc, rev):
    if not rev:
        return q
    return jnp.where(q < ncc, ncc - 1 - q, nc - 1 - (q - ncc))


def _rwkv_kernel(r_ref, k_ref, v_ref, lw_ref, la_ref, g_ref, w0_ref, a0_ref, kk_ref, ka_ref, rk_ref,
                 gng_ref, gnb_ref, o_ref,
                 y_scr, mr_s, n_s, *, n_ctx):
    t_n = CHUNK
    h2 = 2 * t_n
    l_n = r_ref.shape[0]
    nc, ncc = l_n // t_n, n_ctx // t_n
    group = _pick(nc, (3, 2, 1))
    lane = lax.broadcasted_iota(jnp.int32, (1, LANES), 1)
    m1 = jnp.where(lane < RW_HEAD, 1.0, 0.0)
    m2 = 1.0 - m1
    ri = lax.broadcasted_iota(jnp.int32, (LANES, LANES), 0)
    ci = lax.broadcasted_iota(jnp.int32, (LANES, LANES), 1)
    same_head = (ri // RW_HEAD) == (ci // RW_HEAD)
    gsum_b = jnp.where(same_head, 1.0, 0.0).astype(BF16)
    gavg_b = jnp.where(same_head, 1.0 / RW_HEAD, 0.0).astype(BF16)
    eye = jnp.where(ri == ci, 1.0, 0.0)
    tr_i, tc_i = ri % t_n, ci % t_n
    k_k, k_a = kk_ref[...], ka_ref[...]

    def stack(x):
        return jnp.concatenate([x * m1, x * m2], axis=0)

    def rows_of(c):
        return pl.ds(pl.multiple_of(c * t_n, t_n), t_n)

    def head_sum(x, w_b):
        return _mdot(_parts(x, 3), [w_b], order=3)

    tri3_b = [_cumsum_matrix(t_n, rev) for rev in (False, True)]
    strict = [(tc_i > tr_i) if rev else (tc_i < tr_i) for rev in (False, True)]
    incl = [(tc_i >= tr_i) if rev else (tc_i <= tr_i) for rev in (False, True)]

    def stage_prep(c):
        rows = rows_of(c)
        k, r, v = k_ref[rows, :], r_ref[rows, :], v_ref[rows, :]
        kkr = k * k_k
        both = dict(kkr=kkr, ss=head_sum(kkr * kkr, gsum_b))
        items = []
        for d in (0, 1):
            lw = -RW_DECAY_SCALE * _sigmoid(w0_ref[d:d + 1, :] + lw_ref[d, rows, :])
            a = _sigmoid(a0_ref[d:d + 1, :] + la_ref[d, rows, :])
            items.append(dict(d=d, c=c, k=k, r=r, v=v, lw=lw, a=a, both=both, cum=_cumsum(tri3_b[d], lw)))
        return items

    def stage_amat(s):
        d, cum, lw, a, both = s["d"], s["cum"], s["lw"], s["a"], s["both"]
        if "kk" not in both:
            both["kk"] = both["kkr"] * lax.rsqrt(both["ss"] + 1e-12)
            both["vp"] = _parts(stack(s["v"]), 1)
        kk = both["kk"]
        kd = s["k"] * (1 + (a - 1) * k_a)
        bv = kk * a
        p_end = cum[0:1, :] if d == 1 else cum[t_n - 1:t_n, :]
        e_m = jnp.exp(-cum)
        e_h = jnp.exp(p_end - cum)
        ktp = _parts(stack(kk * jnp.exp(cum - lw)), 2)
        rt = stack(s["r"] * jnp.exp(cum))
        k2p = _cat_parts([_parts(stack(bv * e_m), 2), _parts(stack(kd * e_m), 2)], 0)
        return dict(d=d, c=s["c"], ktp=ktp, rt=rt, vp=both["vp"], p_end=p_end,
                    bh=stack(bv * e_h), kh=stack(kd * e_h),
                    amat=_mdot(_cat_parts([ktp, _parts(rt, 2)], 0), k2p, NT))

    def stage_square(s):
        d, amat = s["d"], s["amat"]
        lt = jnp.where(strict[d], amat[:h2, :h2], 0.0).T
        ltp = _parts(lt, 2)
        msk = jnp.concatenate([jnp.where(strict[d], amat[:h2, h2:], 0.0),
                               jnp.where(incl[d], amat[h2:, h2:], 0.0)], axis=0)
        s = dict(s, pt=eye - lt, xt=_mdot(ltp, ltp),
                 av=_mdot(_parts(msk, 2), s["vp"][:1]),
                 arbp=_parts(jnp.where(incl[d], amat[h2:, :h2], 0.0), 2))
        del s["amat"]
        return s

    def stage_double(s, final):
        xh = _parts(s["xt"], 1)
        ptp = _parts(s["pt"], 2)
        if final:
            return dict(s, pt=s["pt"] + _mdot(xh, ptp))
        rhs = [jnp.concatenate([ptp[0], xh[0]], axis=1), jnp.concatenate([ptp[1], jnp.zeros_like(xh[0])], axis=1)]
        both = _mdot(xh, rhs)
        return dict(s, pt=s["pt"] + both[:, :LANES], xt=both[:, LANES:])

    def stage_solve(s):
        rhs = jnp.concatenate([s["ktp"][0], (-s["av"][:h2]).astype(BF16)], axis=1)
        return dict(s, wub=_mdot(_parts(s["pt"].T, 2), [rhs]).astype(BF16))

    def stage_fold(s):
        d, c, wub = s["d"], s["c"], s["wub"]
        aw = _mdot(s["arbp"], [wub])
        zb = jnp.zeros((h2, LANES), BF16)
        lhs = _cat_parts([_parts(s["bh"].T, 2), _parts(s["kh"].T, 2)], 1)
        rhs = jnp.concatenate([wub, jnp.concatenate([zb, s["vp"][0]], axis=1)], axis=0)
        mn = _mdot(lhs, [rhs])
        dg = jnp.where(ri == ci, jnp.broadcast_to(jnp.exp(s["p_end"]), (LANES, LANES)), 0.0)
        rp = s["rt"] - aw[:, :LANES]
        mrp = _parts(jnp.concatenate([dg - mn[:, :LANES], rp[:t_n] + rp[t_n:]], axis=0), 2)
        for i in range(2):
            mr_s[d, c, i] = mrp[i]
        n_s[d, c] = mn[:, LANES:]
        y0 = s["av"][h2:] + aw[:, LANES:]
        return y0[:t_n] + y0[t_n:]

    def local(i, carry):
        sts = [s for g in range(group) for s in stage_prep(i * group + g)]
        sts = [stage_amat(s) for s in sts]
        sts = [stage_square(s) for s in sts]
        for step in range(5):
            sts = [stage_double(s, step == 4) for s in sts]
        sts = [stage_solve(s) for s in sts]
        y0 = [stage_fold(s) for s in sts]
        for g in range(group):
            y_scr[rows_of(i * group + g), :] = y0[2 * g] + y0[2 * g + 1]
        return carry

    lax.fori_loop(0, nc // group, local, 0)

    def seq(q, hs):
        cs = (q, _chunk_of(q, ncc, nc, True))
        mh = [_mdot([mr_s[d, cs[d], 0], mr_s[d, cs[d], 1]], _parts(hs[d], 2)) for d in (0, 1)]
        for d in (0, 1):
            y_scr[rows_of(cs[d]), :] += mh[d][h2:]
        return tuple(mh[d][:h2] + n_s[d, cs[d]] for d in (0, 1))

    zero = jnp.zeros((LANES, LANES), F32)
    lax.fori_loop(0, nc, seq, (zero, zero))

    n_post = _pick(nc, (4, 3, 2, 1))

    def post(i, carry):
        rows = [rows_of(i * n_post + g) for g in range(n_post)]

        def bonus_sum(rw):
            k, r = k_ref[rw, :], r_ref[rw, :]
            kd_f = k * (1 + (_sigmoid(a0_ref[0:1, :] + la_ref[0, rw, :]) - 1) * k_a)
            kd_b = k * (1 + (_sigmoid(a0_ref[1:2, :] + la_ref[1, rw, :]) - 1) * k_a)
            return head_sum(r * (kd_f + kd_b) * rk_ref[...], gsum_b)
        bsum = [bonus_sum(rw) for rw in rows]
        ys = [y_scr[rw, :] for rw in rows]
        ycs = [y - m for y, m in zip(ys, [head_sum(y, gavg_b) for y in ys])]
        var = [head_sum(yc * yc, gavg_b) for yc in ycs]
        for rw, yc, vr, bs in zip(rows, ycs, var, bsum):
            yn = yc * lax.rsqrt(vr + RW_GN_EPS) * gng_ref[...] + gnb_ref[...]
            o_ref[rw, :] = ((yn + bs * v_ref[rw, :]) * g_ref[rw, :]).astype(BF16)
        return carry

    lax.fori_loop(0, nc // n_post, post, 0)


def _rwkv_scan(rkv, lw, la, g, w0, a0, k_k, k_a, r_k, gn_g, gn_b, *, n_ctx):
    _, b_n, l_n, d = rkv.shape
    nc = l_n // CHUNK
    col = lambda n: pl.BlockSpec((None, None, l_n, LANES), lambda b, p, n=n: (n, b, 0, p))
    two = pl.BlockSpec((2, None, l_n, LANES), lambda b, p: (0, b, 0, p))
    par = lambda rows: pl.BlockSpec((rows, LANES), lambda b, p: (0, p))
    return pl.pallas_call(
        functools.partial(_rwkv_kernel, n_ctx=n_ctx),
        grid=(b_n, d // LANES),
        in_specs=[col(0), col(1), col(2), two, two,
                  pl.BlockSpec((None, l_n, LANES), lambda b, p: (b, 0, p)),
                  par(2), par(2), par(1), par(1), par(1), par(1), par(1)],
        out_specs=pl.BlockSpec((None, l_n, LANES), lambda b, p: (b, 0, p)),
        out_shape=jax.ShapeDtypeStruct((b_n, l_n, d), BF16),
        scratch_shapes=[pltpu.VMEM((l_n, LANES), F32),
                        pltpu.VMEM((2, nc, 2, LANES + CHUNK, LANES), BF16),
                        pltpu.VMEM((2, nc, LANES, LANES), F32)],
        compiler_params=_cparams(("parallel", "parallel"), 56),
        name="rwkv_scan",
    )(rkv, rkv, rkv, lw, la, g, w0, a0, k_k.reshape(1, d), k_a.reshape(1, d), r_k.reshape(1, d),
      gn_g.reshape(1, d), gn_b.reshape(1, d))


def _pad_axis(w, axis, to):
    pad = [(0, 0)] * w.ndim
    pad[axis] = (0, to - w.shape[axis])
    return jnp.pad(w, pad)


def _rwkv7_layer(z, mod, n_ctx, mu, w_rkv, w0, w1, w2, a0, a1, a2, g1, g2, k_k, k_a, r_k, gn_g, gn_b, w_o):
    b_n, l_n, d = z.shape
    m_n = b_n * l_n
    xs = _rw_mix(z, mod, mu, n_ctx=n_ctx).reshape(6, m_n, d)
    rkv = _matmul(xs, w_rkv)
    r_w = -(-w1.shape[-1] // LANES) * LANES
    r_a = -(-a1.shape[-1] // LANES) * LANES
    lw = _lora(xs, 3, _pad_axis(w1, 2, r_w).astype(BF16), _pad_axis(w2, 1, r_w).astype(BF16), "tanh")
    la = _lora(xs, 4, _pad_axis(a1, 2, r_a).astype(BF16), _pad_axis(a2, 1, r_a).astype(BF16), None)
    gate = _lora(xs, 5, g1[None].astype(BF16), g2[None].astype(BF16), "sigmoid")
    o = _rwkv_scan(rkv.reshape(3, b_n, l_n, d), lw.reshape(2, b_n, l_n, d), la.reshape(2, b_n, l_n, d),
                   gate.reshape(b_n, l_n, d), w0, a0, k_k, k_a, r_k, gn_g, gn_b, n_ctx=n_ctx)
    return _mm2(o.reshape(m_n, d), w_o).reshape(b_n, l_n, d)


def _rope_kernel(x_ref, cos_ref, sa_ref, sb_ref, o_ref):
    j = pl.program_id(2)
    x = x_ref[...]
    d = x.shape[1]

    @pl.when(j < 2)
    def _():
        rep = d // DA_HEAD
        cos = jnp.tile(cos_ref[...], (1, rep))
        s_a = jnp.tile(sa_ref[...], (1, rep))
        s_b = jnp.tile(sb_ref[...], (1, rep))
        q = DA_HEAD // 4
        q_scale = jnp.where(j == 0, DA_HEAD ** -0.5 * math.log2(math.e), 1.0)
        rot = x * cos + pltpu.roll(x, d - q, 1) * s_a + pltpu.roll(x, q, 1) * s_b
        o_ref[...] = (rot * q_scale).astype(BF16)

    @pl.when(j == 2)
    def _():
        o_ref[...] = x.astype(BF16)


def _rope(qkv, cos, s_a, s_b, *, tr):
    b_n, l_n, d3 = qkv.shape
    d = d3 // 3
    tab = pl.BlockSpec((tr, DA_HEAD), lambda b, t, j: (t, 0))
    return pl.pallas_call(
        _rope_kernel,
        grid=(b_n, l_n // tr, 3),
        in_specs=[pl.BlockSpec((None, tr, d), lambda b, t, j: (b, t, j)), tab, tab, tab],
        out_specs=pl.BlockSpec((None, tr, d), lambda b, t, j: (b, t, j)),
        out_shape=jax.ShapeDtypeStruct((b_n, l_n, d3), BF16),
        compiler_params=_cparams(("parallel", "parallel", "parallel"), 40),
        name="rope",
    )(qkv, cos, s_a, s_b)


def _attn_kernel(q_ref, k_ref, v_ref, lam_ref, sg_ref, o_ref, *, ncb, n_ctx, lam_init):
    qi = pl.program_id(2)
    lv = lam_ref[...]
    lam = (jnp.exp(jnp.sum(lv[0:1] * lv[1:2], axis=-1, keepdims=True))
           - jnp.exp(jnp.sum(lv[2:3] * lv[3:4], axis=-1, keepdims=True)) + lam_init)

    def attend(nk):
        def probs(m):
            q = q_ref[:, m * DA_HEAD:(m + 1) * DA_HEAD]
            k = k_ref[0:nk, m * DA_HEAD:(m + 1) * DA_HEAD]
            s = lax.dot_general(q, k, NT, preferred_element_type=F32)
            e = jnp.exp2(s - jnp.max(s, axis=-1, keepdims=True))
            return e, 1.0 / jnp.sum(e, axis=-1, keepdims=True)
        e0, i0 = probs(0)
        e1, i1 = probs(1)
        v = v_ref[0:nk, :]
        o = (jnp.dot(e0.astype(BF16), v, preferred_element_type=F32) * i0
             - jnp.dot(e1.astype(BF16), v, preferred_element_type=F32) * (lam * i1))
        o = o * lax.rsqrt(jnp.mean(o * o, axis=-1, keepdims=True) + 1e-5) * sg_ref[...] * (1 - lam_init)
        o_ref[...] = o.astype(BF16)

    if ncb > 0:
        @pl.when(qi < ncb)
        def _():
            attend(n_ctx)

    @pl.when(qi >= ncb)
    def _():
        attend(k_ref.shape[0])


def _attention(qkv, lam_vec, sub_g, *, tq, n_ctx, lam_init):
    b_n, l_n, d3 = qkv.shape
    d = d3 // 3
    hw = 2 * DA_HEAD
    nh = d // hw
    return pl.pallas_call(
        functools.partial(_attn_kernel, ncb=n_ctx // tq, n_ctx=n_ctx, lam_init=lam_init),
        grid=(b_n, nh, l_n // tq),
        in_specs=[pl.BlockSpec((None, tq, hw), lambda b, h, t: (b, t, h)),
                  pl.BlockSpec((None, l_n, hw), lambda b, h, t: (b, 0, nh + h)),
                  pl.BlockSpec((None, l_n, hw), lambda b, h, t: (b, 0, 2 * nh + h)),
                  pl.BlockSpec((4, DA_HEAD), lambda b, h, t: (0, 0)),
                  pl.BlockSpec((1, hw), lambda b, h, t: (0, 0))],
        out_specs=pl.BlockSpec((None, tq, hw), lambda b, h, t: (b, t, h)),
        out_shape=jax.ShapeDtypeStruct((b_n, l_n, d), BF16),
        compiler_params=_cparams(("parallel", "parallel", "arbitrary"), 48),
        name="diff_attn",
    )(qkv, qkv, qkv, lam_vec, sub_g.reshape(1, hw))


def _rope_tables(n_ctx, n_lat):
    n_rows = n_lat // GRID_W
    row = jnp.repeat(jnp.arange(n_rows, dtype=F32), GRID_W)
    col = jnp.tile(jnp.arange(GRID_W, dtype=F32), n_rows)
    nf = DA_HEAD // 4
    inv_freq = ROPE_BASE ** (-jnp.arange(nf, dtype=F32) / nf)
    ang_r, ang_c = row[:, None] * inv_freq, col[:, None] * inv_freq
    ang = jnp.concatenate([ang_r, ang_r, ang_c, ang_c], axis=-1)
    ang = jnp.concatenate([jnp.zeros((n_ctx, DA_HEAD), F32), ang], axis=0)
    cos, sin = jnp.cos(ang), jnp.sin(ang)
    even_q = (jnp.arange(DA_HEAD) // nf) % 2 == 0
    return cos, jnp.where(even_q, -sin, 0.0), jnp.where(even_q, 0.0, sin)


def _diff_attention_layer(h, n_ctx, layer_idx, w_qkv, lam_vec, sub_g, w_o, *, tr):
    b_n, l_n, d = h.shape
    m_n = b_n * l_n
    qkv = _mm2(h.reshape(m_n, d), w_qkv).reshape(b_n, l_n, 3 * d)
    cos, s_a, s_b = _rope_tables(n_ctx, l_n - n_ctx)
    qkv = _rope(qkv, cos, s_a, s_b, tr=tr)
    lam_init = 0.8 - 0.6 * math.exp(-0.3 * layer_idx)
    o = _attention(qkv, lam_vec, sub_g, tq=tr, n_ctx=n_ctx, lam_init=lam_init)
    return _mm2(o.reshape(m_n, d), w_o).reshape(b_n, l_n, d)


def _hgrn_kernel(q_ref, i_ref, g_ref, ff_ref, fb_ref, low_ref, ng_ref, o_ref, o_scr, *, n_ctx, layer_idx):
    t_n = CHUNK
    l_n = q_ref.shape[0]
    nc, ncc = l_n // t_n, n_ctx // t_n
    r64 = lax.broadcasted_iota(jnp.int32, (t_n, t_n), 0)
    c64 = lax.broadcasted_iota(jnp.int32, (t_n, t_n), 1)

    def rows_of(c):
        return pl.ds(pl.multiple_of(c * t_n, t_n), t_n)

    f_refs = (ff_ref, fb_ref)
    lbs, incl = [], []
    tri3_b = [_cumsum_matrix(t_n, rev) for rev in (False, True)]
    for d in (0, 1):
        low = low_ref[d]
        e = jnp.exp(low - jnp.max(low, axis=0, keepdims=True))
        sm = e / jnp.sum(e, axis=0, keepdims=True)
        cs = sm[0:1]
        for rr in range(1, layer_idx + 1):
            cs = cs + sm[rr:rr + 1]
        lbs.append(cs - sm[0:1])
        incl.append((c64 >= r64) if d == 1 else (c64 <= r64))
    group = _pick(nc, (4, 2, 1))

    def stage_cum(d, c):
        rows = rows_of(c)
        f = lbs[d] + (1.0 - lbs[d]) * _sigmoid(f_refs[d][rows, :])
        return dict(d=d, rows=rows, f=f, cum=_cumsum(tri3_b[d], jnp.log(f)))

    def stage_att(s):
        d, cum, rows = s["d"], s["cum"], s["rows"]
        b_end = cum[0:1, :] if d == 1 else cum[t_n - 1:t_n, :]
        qv = q_ref[rows, :]
        qd = (qv * _sigmoid(qv) * jnp.exp(cum)).astype(BF16)
        kk = 1.0 - s["f"]
        v = i_ref[rows, :]
        kd = (kk * jnp.exp(-cum)).astype(BF16)
        ke = (kk * jnp.exp(b_end - cum)).astype(BF16)
        return dict(d=d, rows=rows, qd=qd, vb=v.astype(BF16), dec=jnp.exp(b_end),
                    att=lax.dot_general(qd, kd, NT, preferred_element_type=F32),
                    upd=jnp.dot(v.T.astype(BF16), ke, preferred_element_type=F32))

    def stage_intra(s):
        att = jnp.where(incl[s["d"]], s["att"], 0.0).astype(BF16)
        return dict(s, o=jnp.dot(att, s["vb"], preferred_element_type=F32))

    def body(i, states):
        items = [(d, _chunk_of(i * group + g, ncc, nc, d == 1)) for g in range(group) for d in (0, 1)]
        sts = [stage_cum(d, c) for d, c in items]
        sts = [stage_att(s) for s in sts]
        sts = [stage_intra(s) for s in sts]
        states = list(states)
        for s in sts:
            d = s["d"]
            o = s["o"] + lax.dot_general(s["qd"], states[d].astype(BF16), NT, preferred_element_type=F32)
            o_scr[d, s["rows"], :] = o
            states[d] = states[d] * s["dec"] + s["upd"]
        return tuple(states)

    zero = jnp.zeros((LANES, LANES), F32)
    lax.fori_loop(0, nc // group, body, (zero, zero))

    def post(c, carry):
        rows = rows_of(c)
        o = o_scr[0, rows, :] + o_scr[1, rows, :]
        o = o * lax.rsqrt(jnp.mean(o * o, axis=-1, keepdims=True) + 1e-5) * ng_ref[...]
        gv = g_ref[rows, :]
        o_ref[rows, :] = (o * (gv * _sigmoid(gv))).astype(BF16)
        return carry

    lax.fori_loop(0, nc, post, 0)


def _hgrn2_layer(h, n_ctx, layer_idx, w_in, lower, norm_g, w_o):
    b_n, l_n, d = h.shape
    m_n = b_n * l_n
    nh = d // HG_EXPAND
    proj = _mm2(h.reshape(m_n, d), w_in).reshape(b_n, l_n, 5 * d)
    col = lambda n: pl.BlockSpec((None, l_n, LANES), lambda b, p, n=n: (b, 0, n * nh + p))
    o = pl.pallas_call(
        functools.partial(_hgrn_kernel, n_ctx=n_ctx, layer_idx=layer_idx),
        grid=(b_n, nh),
        in_specs=[col(0), col(1), col(2), col(3), col(4),
                  pl.BlockSpec((2, lower.shape[1], LANES), lambda b, p: (0, 0, p)),
                  pl.BlockSpec((1, LANES), lambda b, p: (0, 0))],
        out_specs=pl.BlockSpec((None, l_n, LANES), lambda b, p: (b, 0, p)),
        out_shape=jax.ShapeDtypeStruct((b_n, l_n, d), BF16),
        scratch_shapes=[pltpu.VMEM((2, l_n, LANES), F32)],
        compiler_params=_cparams(("parallel", "parallel"), 40),
        name="hgrn_scan",
    )(proj, proj, proj, proj, proj, lower, norm_g.reshape(1, LANES))
    return _mm2(o.reshape(m_n, d), w_o).reshape(b_n, l_n, d)


def _gelu_tanh(x):
    return 0.5 * x * (1.0 + jnp.tanh(math.sqrt(2.0 / math.pi) * (x + 0.044715 * (x * x * x))))


def _softplus(x):
    return jnp.maximum(x, 0.0) + jnp.log1p(jnp.exp(-jnp.abs(x)))


SEG_PAD = 8


def _lin_scan(a_ref, u_ref, hl_s, cp_s, h_s, base, row0, n, rev, h_in, accumulate):
    seg = n // 8
    stride = seg + SEG_PAD
    n_p = a_ref.shape[0]

    def step(i, carry):
        t = (seg - 1 - i) if rev else i
        idx = pl.ds(base + t, 8, stride=stride)
        out = []
        for j in range(n_p):
            hl, cp = carry[j]
            a = a_ref[j, idx, :]
            hl = a * hl + u_ref[j, idx, :]
            cp = a * cp
            hl_s[j, idx, :] = hl
            cp_s[j, idx, :] = cp
            out.append((hl, cp))
        return tuple(out)

    init = tuple((jnp.zeros((8, LANES), F32), jnp.ones((8, LANES), F32)) for _ in range(n_p))
    ends = lax.fori_loop(0, seg, step, init)
    order = range(7, -1, -1) if rev else range(8)
    h_out = []
    for j in range(n_p):
        hl_e, cp_e = ends[j]
        carry = h_in[j]
        for s in order:
            r0, p0 = row0 + s * seg, base + s * stride
            blk = hl_s[j, p0:p0 + seg, :] + cp_s[j, p0:p0 + seg, :] * carry
            if accumulate:
                h_s[j, r0:r0 + seg, :] += blk
            else:
                h_s[j, r0:r0 + seg, :] = blk
            carry = hl_e[s:s + 1, :] + cp_e[s:s + 1, :] * carry
        h_out.append(carry)
    return h_out


def _rglru_kernel(gb_ref, xb_ref, cw_ref, cb_ref, wg_ref, bg_ref, lam_ref, o_ref, a_s, u_s, h_s, hl_s, cp_s, *,
                  n_ctx):
    l_n = xb_ref.shape[0]
    n_lat = l_n - n_ctx
    x = xb_ref[...]
    row = lax.broadcasted_iota(jnp.int32, x.shape, 0)
    k_w = cw_ref.shape[0]
    xc = cb_ref[...] + sum(_seg_shift(x, row, j - (k_w - 1) // 2, n_ctx) * cw_ref[j:j + 1, :]
                           for j in range(k_w))
    xcb = xc.astype(BF16)
    n_p = x.shape[1] // LANES
    for d in (0, 1):
        gate = lambda g: _sigmoid(jnp.dot(xcb, wg_ref[d, g].astype(BF16), preferred_element_type=F32)
                                  + bg_ref[d, g:g + 1, :])
        log_a = -LR_C * gate(0) * _softplus(-lam_ref[d:d + 1, :])
        a = jnp.exp(log_a)
        u = jnp.sqrt(jnp.tanh(-log_a) * (jnp.exp(2.0 * log_a) + 1.0)) * gate(1) * xc
        h = [jnp.zeros((1, LANES), F32)] * n_p
        base = 0
        for row0, n in ((0, n_ctx), (n_ctx, n_lat)):
            if n:
                seg = n // 8
                for j in range(n_p):
                    for s in range(8):
                        src = slice(row0 + s * seg, row0 + (s + 1) * seg)
                        dst = slice(base + s * (seg + SEG_PAD), base + s * (seg + SEG_PAD) + seg)
                        a_s[j, dst, :] = a[src, j * LANES:(j + 1) * LANES]
                        u_s[j, dst, :] = u[src, j * LANES:(j + 1) * LANES]
                h = _lin_scan(a_s, u_s, hl_s, cp_s, h_s, base, row0, n, d == 1, h, d == 1)
                base += 8 * (seg + SEG_PAD)
    for j in range(n_p):
        cols = slice(j * LANES, (j + 1) * LANES)
        o_ref[:, cols] = (h_s[j, n_ctx:l_n, :] * _gelu_tanh(gb_ref[n_ctx:, cols])).astype(BF16)


def _rglru_layer(h, n_ctx, w_in, conv_w, conv_b, w_gate, b_gate, lam, w_o):
    b_n, l_n, d = h.shape
    n_lat = l_n - n_ctx
    nb = d // LR_BS
    proj = _mm2(h.reshape(b_n * l_n, d), w_in).reshape(b_n, l_n, 2 * d)
    k_w = conv_w.shape[0]
    o = pl.pallas_call(
        functools.partial(_rglru_kernel, n_ctx=n_ctx),
        grid=(b_n, nb),
        in_specs=[pl.BlockSpec((None, l_n, LR_BS), lambda b, j: (b, 0, j)),
                  pl.BlockSpec((None, l_n, LR_BS), lambda b, j: (b, 0, nb + j)),
                  pl.BlockSpec((k_w, LR_BS), lambda b, j: (0, j)),
                  pl.BlockSpec((1, LR_BS), lambda b, j: (0, j)),
                  pl.BlockSpec((2, 2, None, LR_BS, LR_BS), lambda b, j: (0, 0, j, 0, 0)),
                  pl.BlockSpec((2, 2, LR_BS), lambda b, j: (0, 0, j)),
                  pl.BlockSpec((2, LR_BS), lambda b, j: (0, j))],
        out_specs=pl.BlockSpec((None, n_lat, LR_BS), lambda b, j: (b, 0, j)),
        out_shape=jax.ShapeDtypeStruct((b_n, n_lat, d), BF16),
        scratch_shapes=[pltpu.VMEM((LR_BS // LANES, l_n + 16 * SEG_PAD, LANES), F32)] * 5,
        compiler_params=_cparams(("parallel", "parallel"), 56),
        name="rglru",
    )(proj, proj, conv_w, conv_b.reshape(1, d), w_gate, b_gate, lam)
    return _mm2(o.reshape(b_n * n_lat, d), w_o).reshape(b_n, n_lat, d)


def _ffn_up_kernel(h_ref, wg_ref, wv_ref, cg_ref, cv_ref, bg_ref, bv_ref, o_ref, u_scr, *, n_ctx, tr):
    l_n, tf = o_ref.shape
    k_w = cg_ref.shape[0]
    half = (k_w - 1) // 2
    n_buf, pad = u_scr.shape[0], (u_scr.shape[1] - l_n) // 2
    row = lax.broadcasted_iota(jnp.int32, (tr, LANES), 0)
    for p in range(n_buf):
        u_scr[p, 0:pad, :] = jnp.zeros((pad, 2 * LANES), F32)
        u_scr[p, pad + l_n:, :] = jnp.zeros((pad, 2 * LANES), F32)

    def product(s):
        cols = slice(s * LANES, (s + 1) * LANES)
        w = jnp.concatenate([wg_ref[:, cols], wv_ref[:, cols]], axis=1).astype(BF16)
        u_scr[s % n_buf, pad:pad + l_n, :] = jnp.dot(h_ref[...], w, preferred_element_type=F32)

    def finish(s):
        p, cols = s % n_buf, slice(s * LANES, (s + 1) * LANES)
        for r0 in range(0, l_n, tr):
            def conv(lane0, w_ref, b_ref):
                acc = None
                for j in range(k_w):
                    sh = j - half
                    x = u_scr[p, pad + r0 + sh:pad + r0 + sh + tr, lane0:lane0 + LANES]
                    if sh < 0 and r0 in (0, n_ctx):
                        x = jnp.where(row < -sh, 0.0, x)
                    if sh > 0 and r0 + tr in (n_ctx, l_n):
                        x = jnp.where(row >= tr - sh, 0.0, x)
                    t = x * w_ref[j:j + 1, cols]
                    acc = t if acc is None else acc + t
                return b_ref[:, cols] + acc
            gate = conv(0, cg_ref, bg_ref)
            val = conv(LANES, cv_ref, bv_ref)
            o_ref[r0:r0 + tr, cols] = (gate * _sigmoid(gate) * val).astype(BF16)

    n_s = tf // LANES
    product(0)
    for s in range(1, n_s):
        product(s)
        finish(s - 1)
    finish(n_s - 1)


def _conv_ffn(h, n_ctx, w_up, conv_w, conv_b, w_down, *, tr):
    b_n, l_n, d = h.shape
    f = w_down.shape[0]
    tf = _pick(f, (512, 256, 128))
    nf = f // tf
    k_w = conv_w.shape[0]
    cb = conv_b.reshape(1, 2 * f)
    act = pl.pallas_call(
        functools.partial(_ffn_up_kernel, n_ctx=n_ctx, tr=tr),
        grid=(b_n, nf),
        in_specs=[pl.BlockSpec((None, l_n, d), lambda b, j: (b, 0, 0), pipeline_mode=pl.Buffered(1)),
                  pl.BlockSpec((d, tf), lambda b, j: (0, j)),
                  pl.BlockSpec((d, tf), lambda b, j: (0, nf + j)),
                  pl.BlockSpec((k_w, tf), lambda b, j: (0, j)),
                  pl.BlockSpec((k_w, tf), lambda b, j: (0, nf + j)),
                  pl.BlockSpec((1, tf), lambda b, j: (0, j)),
                  pl.BlockSpec((1, tf), lambda b, j: (0, nf + j))],
        out_specs=pl.BlockSpec((None, l_n, tf), lambda b, j: (b, 0, j)),
        out_shape=jax.ShapeDtypeStruct((b_n, l_n, f), BF16),
        scratch_shapes=[pltpu.VMEM((3, l_n + 16, 2 * LANES), F32)],
        compiler_params=_cparams(("parallel", "arbitrary"), 56),
        name="ffn_up",
    )(h, w_up, w_up, conv_w, conv_w, cb, cb)
    return _mm2(act.reshape(b_n * l_n, f), w_down.astype(BF16)).reshape(b_n, l_n, d)


def kernel(x, c, ctx, c_ctx, ada_w, ada_b, ln_g, ln_b, ffn_w_up, ffn_conv_w, ffn_conv_b, ffn_w_down, rw_mu, rw_w_rkv, rw_w0, rw_w1, rw_w2, rw_a0, rw_a1, rw_a2, rw_g1, rw_g2, rw_k_k, rw_k_a, rw_r_k, rw_gn_g, rw_gn_b, rw_w_o, da_w_qkv, da_lambda, da_sub_g, da_w_o, hg_w_in, hg_lower, hg_norm_g, hg_w_o, lr_w_in, lr_conv_w, lr_conv_b, lr_w_gate, lr_b_gate, lr_lambda, lr_w_o):
    b_n, n_lat, d = x.shape
    n_ctx = ctx.shape[1]
    depth = ada_w.shape[0]
    assert depth == 4 and rw_mu.shape[0] == 1, "one occurrence of each of the four mixers"
    assert b_n + 1 <= 8 and n_ctx % CHUNK == 0 and n_lat % CHUNK == 0
    tr = math.gcd(math.gcd(n_ctx, n_lat), 256)
    alpha = (2 * depth) ** 0.25

    c8 = jnp.concatenate([c, c_ctx[None], jnp.zeros((8 - b_n - 1, d), F32)], axis=0)
    m = _ada(c8, ada_w, ada_b)
    m_lat = m[:, :b_n].reshape(depth, b_n, 1, 6, d)
    m_ctx = jnp.broadcast_to(m[:, b_n].reshape(depth, 1, 1, 6, d), (depth, b_n, 1, 6, d))
    mod = jnp.concatenate([m_ctx, m_lat], axis=2)

    z = jnp.concatenate([ctx, x], axis=1)
    h = None
    for i in range(depth):
        last = i == depth - 1
        if i == 0:
            y = _rwkv7_layer(z, mod[0], n_ctx, rw_mu[0], rw_w_rkv[0], rw_w0[0], rw_w1[0], rw_w2[0], rw_a0[0],
                             rw_a1[0], rw_a2[0], rw_g1[0], rw_g2[0], rw_k_k[0], rw_k_a[0], rw_r_k[0],
                             rw_gn_g[0], rw_gn_b[0], rw_w_o[0])
        elif i == 1:
            y = _diff_attention_layer(h, n_ctx, i, da_w_qkv[0], da_lambda[0], da_sub_g[0], da_w_o[0], tr=tr)
        elif i == 2:
            y = _hgrn2_layer(h, n_ctx, i, hg_w_in[0], hg_lower, hg_norm_g[0], hg_w_o[0])
        else:
            y = _rglru_layer(h, n_ctx, lr_w_in[0], lr_conv_w[0], lr_conv_b[0], lr_w_gate[0], lr_b_gate[0],
                             lr_lambda[0], lr_w_o[0])
        z, h = _ln_mod(z, y, mod[i], ln_g[i, 0], ln_b[i, 0], mod[i], gate_j=2, mod_j=3, tr=tr,
                       n_ctx=n_ctx, alpha=alpha)
        if last:
            n_ctx = 0
        y = _conv_ffn(h, n_ctx, ffn_w_up[i], ffn_conv_w[i], ffn_conv_b[i], ffn_w_down[i], tr=tr)
        z, h = _ln_mod(z, y, mod[i], ln_g[i, 1], ln_b[i, 1], mod[min(i + 1, depth - 1)], gate_j=5,
                       mod_j=None if last else 0, tr=tr, n_ctx=n_ctx, alpha=alpha)
    return z
```

```python
import functools
import math

import jax
import jax.numpy as jnp
from jax import lax
from jax.experimental import pallas as pl
from jax.experimental.pallas import tpu as pltpu

F32, BF16 = jnp.float32, jnp.bfloat16

LANES = 128
CHUNK = 64
LN_EPS = 1e-5
GRID_W = 64
ROPE_BASE = 10000.0
RW_HEAD = 64
RW_DECAY_SCALE = 0.606531
RW_GN_EPS = 64e-5
DA_HEAD = 128
HG_EXPAND = 128
LR_BS = 256
LR_C = 8.0
MIB = 1024 * 1024


def _pick(n, cands):
    for c in cands:
        if n % c == 0:
            return c
    return n


def _cparams(sem, vmem_mib):
    return pltpu.CompilerParams(dimension_semantics=sem, vmem_limit_bytes=vmem_mib * MIB)


def _sigmoid(x):
    return jax.nn.sigmoid(x)


NN = (((1,), (0,)), ((), ()))
NT = (((1,), (1,)), ((), ()))


def _parts(x, n):
    out = []
    for i in range(n):
        p = x.astype(BF16)
        out.append(p)
        if i + 1 < n:
            x = x - p.astype(F32)
    return out


def _mdot(ap, bp, dims=NN, order=2):
    pairs = [(a, b) for i, a in enumerate(ap) for j, b in enumerate(bp) if i + j < order]
    (ca,), (cb,) = dims[0]
    lhs = jnp.concatenate([a for a, _ in pairs], axis=ca) if len(pairs) > 1 else pairs[0][0]
    rhs = jnp.concatenate([b for _, b in pairs], axis=cb) if len(pairs) > 1 else pairs[0][1]
    return lax.dot_general(lhs, rhs, dims, preferred_element_type=F32)


def _cumsum_matrix(t_n, rev):
    r = lax.broadcasted_iota(jnp.int32, (t_n, 3 * t_n), 0)
    c = lax.broadcasted_iota(jnp.int32, (t_n, 3 * t_n), 1) % t_n
    return jnp.where((c >= r) if rev else (c <= r), 1.0, 0.0).astype(BF16)


def _cumsum(tri3_b, x):
    return jnp.dot(tri3_b, jnp.concatenate(_parts(x, 3), axis=0), preferred_element_type=F32)


def _cat_parts(xs, axis):
    return [jnp.concatenate(ps, axis=axis) for ps in zip(*xs)]


def _mm_kernel(a_ref, w_ref, o_ref, acc_ref, *, nk):
    if w_ref.dtype == BF16:
        prod = jnp.dot(a_ref[...], w_ref[...], preferred_element_type=F32)
    else:
        tk = w_ref.shape[0]
        n_kc = 2 if tk % (2 * LANES) == 0 else 1
        prod = None
        for c in range(n_kc):
            ks = slice(c * (tk // n_kc), (c + 1) * (tk // n_kc))
            t = jnp.dot(a_ref[:, ks], w_ref[ks, :].astype(BF16), preferred_element_type=F32)
            prod = t if prod is None else prod + t
    if nk == 1:
        o_ref[...] = prod.astype(o_ref.dtype)
    else:
        k = pl.program_id(3)

        @pl.when(k == 0)
        def _():
            acc_ref[...] = prod

        @pl.when(k > 0)
        def _():
            acc_ref[...] += prod

        @pl.when(k == nk - 1)
        def _():
            o_ref[...] = acc_ref[...].astype(o_ref.dtype)


def _mm_wres_kernel(a_ref, w_ref, o_ref, wb_ref):
    i = pl.program_id(2)
    k_n = w_ref.shape[0]
    n_kc = 4 if k_n % (4 * LANES) == 0 else 1

    @pl.when(i == 0)
    def _():
        kc = k_n // n_kc
        acc = None
        for c in range(n_kc):
            ks = slice(c * kc, (c + 1) * kc)
            wb = w_ref[ks, :].astype(BF16)
            wb_ref[ks, :] = wb
            t = jnp.dot(a_ref[:, ks], wb, preferred_element_type=F32)
            acc = t if acc is None else acc + t
        o_ref[...] = acc.astype(o_ref.dtype)

    @pl.when(i > 0)
    def _():
        o_ref[...] = jnp.dot(a_ref[...], wb_ref[...], preferred_element_type=F32).astype(o_ref.dtype)


def _matmul(a, w, *, out_dtype=F32, a_off=0):
    g_n, k_n, n_n = w.shape
    m_n = a.shape[1]
    tm = _pick(m_n, (1024, 512, 256, 128, 64))
    tn = _pick(n_n, (1024, 512, 256, 128))
    tk = k_n if k_n <= 2048 else _pick(k_n, (2816, 2048, 1024, 512))
    nk = k_n // tk
    if w.dtype == F32 and nk > 1:
        tn = _pick(n_n, (512, 256, 128))
    if w.dtype == F32 and nk == 1:
        return pl.pallas_call(
            _mm_wres_kernel,
            grid=(g_n, n_n // tn, m_n // tm),
            in_specs=[pl.BlockSpec((None, tm, k_n), lambda g, j, i: (g + a_off, i, 0)),
                      pl.BlockSpec((None, k_n, tn), lambda g, j, i: (g, 0, j))],
            out_specs=pl.BlockSpec((None, tm, tn), lambda g, j, i: (g, i, j)),
            out_shape=jax.ShapeDtypeStruct((g_n, m_n, n_n), out_dtype),
            scratch_shapes=[pltpu.VMEM((k_n, tn), BF16)],
            compiler_params=_cparams(("parallel", "parallel", "arbitrary"), 48),
            name="matmul_wres",
        )(a, w)
    return pl.pallas_call(
        functools.partial(_mm_kernel, nk=nk),
        grid=(g_n, m_n // tm, n_n // tn, nk),
        in_specs=[pl.BlockSpec((None, tm, tk), lambda g, i, j, k: (g + a_off, i, k)),
                  pl.BlockSpec((None, tk, tn), lambda g, i, j, k: (g, k, j))],
        out_specs=pl.BlockSpec((None, tm, tn), lambda g, i, j, k: (g, i, j)),
        out_shape=jax.ShapeDtypeStruct((g_n, m_n, n_n), out_dtype),
        scratch_shapes=[pltpu.VMEM((tm, tn), F32)],
        compiler_params=_cparams(("parallel", "parallel", "parallel", "arbitrary"), 48),
        name="matmul",
    )(a, w)


def _mm2(a, w, **kw):
    return _matmul(a[None], w[None], **kw)[0]


def _ada_kernel(c_ref, w_ref, b_ref, o_ref):
    c = c_ref[...]
    s = (c * _sigmoid(c)).astype(BF16)
    o_ref[...] = jnp.dot(s, w_ref[...].astype(BF16), preferred_element_type=F32) + b_ref[...]


def _ada(c8, ada_w, ada_b):
    depth, d, n = ada_w.shape
    tn = _pick(n, (1024, 512, 256, 128))
    return pl.pallas_call(
        _ada_kernel,
        grid=(depth, n // tn),
        in_specs=[pl.BlockSpec((8, d), lambda l, j: (0, 0)),
                  pl.BlockSpec((None, d, tn), lambda l, j: (l, 0, j)),
                  pl.BlockSpec((None, 1, tn), lambda l, j: (l, 0, j))],
        out_specs=pl.BlockSpec((None, 8, tn), lambda l, j: (l, 0, j)),
        out_shape=jax.ShapeDtypeStruct((depth, 8, n), F32),
        compiler_params=_cparams(("parallel", "parallel"), 40),
        name="ada",
    )(c8, ada_w, ada_b.reshape(depth, 1, n))


def _ln_mod_kernel(z_ref, y_ref, mod_ref, g_ref, b_ref, mod2_ref, *out_refs, gate_j, mod_j, alpha):
    m = mod_ref[...]
    zz = alpha * z_ref[...] + y_ref[...] * m[gate_j:gate_j + 1]
    mu = jnp.mean(zz, axis=-1, keepdims=True)
    zc = zz - mu
    var = jnp.mean(zc * zc, axis=-1, keepdims=True)
    zn = zc * lax.rsqrt(var + LN_EPS) * g_ref[...] + b_ref[...]
    out_refs[0][...] = zn
    if mod_j is not None:
        m2 = mod2_ref[...]
        out_refs[1][...] = (zn * (1 + m2[mod_j + 1:mod_j + 2]) + m2[mod_j:mod_j + 1]).astype(BF16)


def _ln_mod(z, y, mod, ln_g, ln_b, mod2, *, gate_j, mod_j, tr, n_ctx, alpha):
    b_n, l_z, d = z.shape
    l_y = y.shape[1]
    z_off = (l_z - l_y) // tr
    ncb = (n_ctx - (l_z - l_y)) // tr
    seg = lambda b, t: (b, jnp.where(t < ncb, 0, 1), 0, 0)
    row = pl.BlockSpec((None, tr, d), lambda b, t: (b, t, 0))
    out_shape = [jax.ShapeDtypeStruct((b_n, l_y, d), F32)]
    out_specs = [row]
    if mod_j is not None:
        out_shape.append(jax.ShapeDtypeStruct((b_n, l_y, d), BF16))
        out_specs.append(row)
    res = pl.pallas_call(
        functools.partial(_ln_mod_kernel, gate_j=gate_j, mod_j=mod_j, alpha=alpha),
        grid=(b_n, l_y // tr),
        in_specs=[pl.BlockSpec((None, tr, d), lambda b, t: (b, t + z_off, 0)),
                  row,
                  pl.BlockSpec((None, None, 6, d), seg),
                  pl.BlockSpec((1, d), lambda b, t: (0, 0)),
                  pl.BlockSpec((1, d), lambda b, t: (0, 0)),
                  pl.BlockSpec((None, None, 6, d), seg)],
        out_specs=out_specs,
        out_shape=out_shape,
        compiler_params=_cparams(("parallel", "parallel"), 40),
        name="ln_mod",
    )(z, y, mod, ln_g.reshape(1, d), ln_b.reshape(1, d), mod2)
    return res if mod_j is not None else (res[0], None)


def _seg_shift(x, row, shift, n_ctx):
    l_n = x.shape[0]
    rolled = pltpu.roll(x, (-shift) % l_n, 0)
    src = row + shift
    same_seg = (src >= 0) & (src < l_n) & ((src < n_ctx) == (row < n_ctx))
    return jnp.where(same_seg, rolled, 0.0)


def _rw_mix_kernel(z_ref, mod_ref, mu_ref, o_ref, *, n_ctx):
    z = z_ref[...]
    row = lax.broadcasted_iota(jnp.int32, z.shape, 0)
    is_ctx = row < n_ctx
    shift = jnp.where(is_ctx, mod_ref[0, 0:1, :], mod_ref[1, 0:1, :])
    scale = jnp.where(is_ctx, mod_ref[0, 1:2, :], mod_ref[1, 1:2, :])
    h = z * (1 + scale) + shift
    dx = 0.5 * (_seg_shift(h, row, -1, n_ctx) + _seg_shift(h, row, 1, n_ctx)) - h
    for n in range(6):
        o_ref[n] = (h + dx * mu_ref[n:n + 1, :]).astype(BF16)


def _rw_mix(z, mod, mu, *, n_ctx):
    b_n, l_n, d = z.shape
    tc = _pick(d, (256, 128))
    return pl.pallas_call(
        functools.partial(_rw_mix_kernel, n_ctx=n_ctx),
        grid=(b_n, d // tc),
        in_specs=[pl.BlockSpec((None, l_n, tc), lambda b, j: (b, 0, j)),
                  pl.BlockSpec((None, 2, 6, tc), lambda b, j: (b, 0, 0, j)),
                  pl.BlockSpec((6, tc), lambda b, j: (0, j))],
        out_specs=pl.BlockSpec((6, None, l_n, tc), lambda b, j: (0, b, 0, j)),
        out_shape=jax.ShapeDtypeStruct((6, b_n, l_n, d), BF16),
        compiler_params=_cparams(("parallel", "parallel"), 48),
        name="rw_mix",
    )(z, mod, mu)


def _lora_kernel(x_ref, a_ref, b_ref, o_ref, *, act):
    t = jnp.dot(x_ref[...], a_ref[...], preferred_element_type=F32)
    if act == "tanh":
        t = jnp.tanh(t)
    elif act == "sigmoid":
        t = _sigmoid(t)
    o_ref[...] = jnp.dot(t.astype(BF16), b_ref[...], preferred_element_type=F32)


def _lora(xs, x_idx, a, b, act):
    g_n, d, r = a.shape
    m_n = xs.shape[1]
    tm = _pick(m_n, (512, 256, 128, 64))
    return pl.pallas_call(
        functools.partial(_lora_kernel, act=act),
        grid=(g_n, m_n // tm),
        in_specs=[pl.BlockSpec((None, tm, d), lambda g, i: (x_idx, i, 0)),
                  pl.BlockSpec((None, d, r), lambda g, i: (g, 0, 0)),
                  pl.BlockSpec((None, r, d), lambda g, i: (g, 0, 0))],
        out_specs=pl.BlockSpec((None, tm, d), lambda g, i: (g, i, 0)),
        out_shape=jax.ShapeDtypeStruct((g_n, m_n, d), F32),
        compiler_params=_cparams(("parallel", "parallel"), 40),
        name="lora",
    )(xs, a, b)


def _chunk_of(q, ncc, nc, rev):
    if not rev:
        return q
    return jnp.where(q < ncc, ncc - 1 - q, nc - 1 - (q - ncc))


def _rwkv_kernel(r_ref, k_ref, v_ref, lw_ref, la_ref, g_ref, w0_ref, a0_ref, kk_ref, ka_ref, rk_ref,
                 gng_ref, gnb_ref, o_ref,
                 y_scr, mr_s, n_s, *, n_ctx):
    t_n = CHUNK
    h2 = 2 * t_n
    l_n = r_ref.shape[0]
    nc, ncc = l_n // t_n, n_ctx // t_n
    group = _pick(nc, (4, 3, 2, 1))
    lane = lax.broadcasted_iota(jnp.int32, (1, LANES), 1)
    m1 = jnp.where(lane < RW_HEAD, 1.0, 0.0)
    m2 = 1.0 - m1
    ri = lax.broadcasted_iota(jnp.int32, (LANES, LANES), 0)
    ci = lax.broadcasted_iota(jnp.int32, (LANES, LANES), 1)
    same_head = (ri // RW_HEAD) == (ci // RW_HEAD)
    gsum_b = jnp.where(same_head, 1.0, 0.0).astype(BF16)
    gavg_b = jnp.where(same_head, 1.0 / RW_HEAD, 0.0).astype(BF16)
    eye = jnp.where(ri == ci, 1.0, 0.0)
    tr_i, tc_i = ri % t_n, ci % t_n
    k_k, k_a = kk_ref[...], ka_ref[...]

    def stack(x):
        return jnp.concatenate([x * m1, x * m2], axis=0)

    def rows_of(c):
        return pl.ds(pl.multiple_of(c * t_n, t_n), t_n)

    def head_sum(x, w_b):
        return _mdot(_parts(x, 3), [w_b], order=3)

    tri3_b = [_cumsum_matrix(t_n, rev) for rev in (False, True)]
    strict = [(tc_i > tr_i) if rev else (tc_i < tr_i) for rev in (False, True)]
    incl = [(tc_i >= tr_i) if rev else (tc_i <= tr_i) for rev in (False, True)]

    def stage_prep(c):
        rows = rows_of(c)
        k, r, v = k_ref[rows, :], r_ref[rows, :], v_ref[rows, :]
        kkr = k * k_k
        both = dict(kkr=kkr, ss=head_sum(kkr * kkr, gsum_b))
        items = []
        for d in (0, 1):
            lw = -RW_DECAY_SCALE * _sigmoid(w0_ref[d:d + 1, :] + lw_ref[d, rows, :])
            a = _sigmoid(a0_ref[d:d + 1, :] + la_ref[d, rows, :])
            items.append(dict(d=d, c=c, k=k, r=r, v=v, lw=lw, a=a, both=both, cum=_cumsum(tri3_b[d], lw)))
        return items

    def stage_amat(s):
        d, cum, lw, a, both = s["d"], s["cum"], s["lw"], s["a"], s["both"]
        if "kk" not in both:
            both["kk"] = both["kkr"] * lax.rsqrt(both["ss"] + 1e-12)
            both["vp"] = _parts(stack(s["v"]), 1)
        kk = both["kk"]
        kd = s["k"] * (1 + (a - 1) * k_a)
        bv = kk * a
        p_end = cum[0:1, :] if d == 1 else cum[t_n - 1:t_n, :]
        e_m = jnp.exp(-cum)
        e_h = jnp.exp(p_end - cum)
        ktp = _parts(stack(kk * jnp.exp(cum - lw)), 2)
        rt = stack(s["r"] * jnp.exp(cum))
        k2p = _cat_parts([_parts(stack(bv * e_m), 2), _parts(stack(kd * e_m), 2)], 0)
        return dict(d=d, c=s["c"], ktp=ktp, rt=rt, vp=both["vp"], p_end=p_end,
                    bh=stack(bv * e_h), kh=stack(kd * e_h),
                    amat=_mdot(_cat_parts([ktp, _parts(rt, 2)], 0), k2p, NT))

    def stage_square(s):
        d, amat = s["d"], s["amat"]
        lt = jnp.where(strict[d], amat[:h2, :h2], 0.0).T
        ltp = _parts(lt, 2)
        msk = jnp.concatenate([jnp.where(strict[d], amat[:h2, h2:], 0.0),
                               jnp.where(incl[d], amat[h2:, h2:], 0.0)], axis=0)
        s = dict(s, pt=eye - lt, xt=_mdot(ltp, ltp),
                 av=_mdot(_parts(msk, 2), s["vp"][:1]),
                 arbp=_parts(jnp.where(incl[d], amat[h2:, :h2], 0.0), 2))
        del s["amat"]
        return s

    def stage_double(s, final):
        xh = _parts(s["xt"], 1)
        ptp = _parts(s["pt"], 2)
        if final:
            return dict(s, pt=s["pt"] + _mdot(xh, ptp))
        rhs = [jnp.concatenate([ptp[0], xh[0]], axis=1), jnp.concatenate([ptp[1], jnp.zeros_like(xh[0])], axis=1)]
        both = _mdot(xh, rhs)
        return dict(s, pt=s["pt"] + both[:, :LANES], xt=both[:, LANES:])

    def stage_solve(s):
        rhs = jnp.concatenate([s["ktp"][0], (-s["av"][:h2]).astype(BF16)], axis=1)
        return dict(s, wub=_mdot(_parts(s["pt"].T, 2), [rhs]).astype(BF16))

    def stage_fold(s):
        d, c, wub = s["d"], s["c"], s["wub"]
        aw = _mdot(s["arbp"], [wub])
        zb = jnp.zeros((h2, LANES), BF16)
        lhs = _cat_parts([_parts(s["bh"].T, 2), _parts(s["kh"].T, 2)], 1)
        rhs = jnp.concatenate([wub, jnp.concatenate([zb, s["vp"][0]], axis=1)], axis=0)
        mn = _mdot(lhs, [rhs])
        dg = jnp.where(ri == ci, jnp.broadcast_to(jnp.exp(s["p_end"]), (LANES, LANES)), 0.0)
        rp = s["rt"] - aw[:, :LANES]
        mrp = _parts(jnp.concatenate([dg - mn[:, :LANES], rp[:t_n] + rp[t_n:]], axis=0), 2)
        for i in range(2):
            mr_s[d, c, i] = mrp[i]
        n_s[d, c] = mn[:, LANES:]
        y0 = s["av"][h2:] + aw[:, LANES:]
        return y0[:t_n] + y0[t_n:]

    def local(i, carry):
        sts = [s for g in range(group) for s in stage_prep(i * group + g)]
        sts = [stage_amat(s) for s in sts]
        sts = [stage_square(s) for s in sts]
        for step in range(5):
            sts = [stage_double(s, step == 4) for s in sts]
        sts = [stage_solve(s) for s in sts]
        y0 = [stage_fold(s) for s in sts]
        for g in range(group):
            y_scr[rows_of(i * group + g), :] = y0[2 * g] + y0[2 * g + 1]
        return carry

    lax.fori_loop(0, nc // group, local, 0)

    def seq(q, hs):
        cs = (q, _chunk_of(q, ncc, nc, True))
        mh = [_mdot([mr_s[d, cs[d], 0], mr_s[d, cs[d], 1]], _parts(hs[d], 2)) for d in (0, 1)]
        for d in (0, 1):
            y_scr[rows_of(cs[d]), :] += mh[d][h2:]
        return tuple(mh[d][:h2] + n_s[d, cs[d]] for d in (0, 1))

    zero = jnp.zeros((LANES, LANES), F32)
    lax.fori_loop(0, nc, seq, (zero, zero))

    n_post = _pick(nc, (4, 3, 2, 1))

    def post(i, carry):
        rows = [rows_of(i * n_post + g) for g in range(n_post)]

        def bonus_sum(rw):
            k, r = k_ref[rw, :], r_ref[rw, :]
            kd_f = k * (1 + (_sigmoid(a0_ref[0:1, :] + la_ref[0, rw, :]) - 1) * k_a)
            kd_b = k * (1 + (_sigmoid(a0_ref[1:2, :] + la_ref[1, rw, :]) - 1) * k_a)
            return head_sum(r * (kd_f + kd_b) * rk_ref[...], gsum_b)
        bsum = [bonus_sum(rw) for rw in rows]
        ys = [y_scr[rw, :] for rw in rows]
        ycs = [y - m for y, m in zip(ys, [head_sum(y, gavg_b) for y in ys])]
        var = [head_sum(yc * yc, gavg_b) for yc in ycs]
        for rw, yc, vr, bs in zip(rows, ycs, var, bsum):
            yn = yc * lax.rsqrt(vr + RW_GN_EPS) * gng_ref[...] + gnb_ref[...]
            o_ref[rw, :] = ((yn + bs * v_ref[rw, :]) * g_ref[rw, :]).astype(BF16)
        return carry

    lax.fori_loop(0, nc // n_post, post, 0)


def _rwkv_scan(rkv, lw, la, g, w0, a0, k_k, k_a, r_k, gn_g, gn_b, *, n_ctx):
    _, b_n, l_n, d = rkv.shape
    nc = l_n // CHUNK
    col = lambda n: pl.BlockSpec((None, None, l_n, LANES), lambda b, p, n=n: (n, b, 0, p))
    two = pl.BlockSpec((2, None, l_n, LANES), lambda b, p: (0, b, 0, p))
    par = lambda rows: pl.BlockSpec((rows, LANES), lambda b, p: (0, p))
    return pl.pallas_call(
        functools.partial(_rwkv_kernel, n_ctx=n_ctx),
        grid=(b_n, d // LANES),
        in_specs=[col(0), col(1), col(2), two, two,
                  pl.BlockSpec((None, l_n, LANES), lambda b, p: (b, 0, p)),
                  par(2), par(2), par(1), par(1), par(1), par(1), par(1)],
        out_specs=pl.BlockSpec((None, l_n, LANES), lambda b, p: (b, 0, p)),
        out_shape=jax.ShapeDtypeStruct((b_n, l_n, d), BF16),
        scratch_shapes=[pltpu.VMEM((l_n, LANES), F32),
                        pltpu.VMEM((2, nc, 2, LANES + CHUNK, LANES), BF16),
                        pltpu.VMEM((2, nc, LANES, LANES), F32)],
        compiler_params=_cparams(("parallel", "parallel"), 56),
        name="rwkv_scan",
    )(rkv, rkv, rkv, lw, la, g, w0, a0, k_k.reshape(1, d), k_a.reshape(1, d), r_k.reshape(1, d),
      gn_g.reshape(1, d), gn_b.reshape(1, d))


def _pad_axis(w, axis, to):
    pad = [(0, 0)] * w.ndim
    pad[axis] = (0, to - w.shape[axis])
    return jnp.pad(w, pad)


def _rwkv7_layer(z, mod, n_ctx, mu, w_rkv, w0, w1, w2, a0, a1, a2, g1, g2, k_k, k_a, r_k, gn_g, gn_b, w_o):
    b_n, l_n, d = z.shape
    m_n = b_n * l_n
    xs = _rw_mix(z, mod, mu, n_ctx=n_ctx).reshape(6, m_n, d)
    rkv = _matmul(xs, w_rkv)
    r_w = -(-w1.shape[-1] // LANES) * LANES
    r_a = -(-a1.shape[-1] // LANES) * LANES
    lw = _lora(xs, 3, _pad_axis(w1, 2, r_w).astype(BF16), _pad_axis(w2, 1, r_w).astype(BF16), "tanh")
    la = _lora(xs, 4, _pad_axis(a1, 2, r_a).astype(BF16), _pad_axis(a2, 1, r_a).astype(BF16), None)
    gate = _lora(xs, 5, g1[None].astype(BF16), g2[None].astype(BF16), "sigmoid")
    o = _rwkv_scan(rkv.reshape(3, b_n, l_n, d), lw.reshape(2, b_n, l_n, d), la.reshape(2, b_n, l_n, d),
                   gate.reshape(b_n, l_n, d), w0, a0, k_k, k_a, r_k, gn_g, gn_b, n_ctx=n_ctx)
    return _mm2(o.reshape(m_n, d), w_o).reshape(b_n, l_n, d)


def _rope_kernel(x_ref, cos_ref, sa_ref, sb_ref, o_ref):
    j = pl.program_id(2)
    x = x_ref[...]
    d = x.shape[1]

    @pl.when(j < 2)
    def _():
        rep = d // DA_HEAD
        cos = jnp.tile(cos_ref[...], (1, rep))
        s_a = jnp.tile(sa_ref[...], (1, rep))
        s_b = jnp.tile(sb_ref[...], (1, rep))
        q = DA_HEAD // 4
        q_scale = jnp.where(j == 0, DA_HEAD ** -0.5 * math.log2(math.e), 1.0)
        rot = x * cos + pltpu.roll(x, d - q, 1) * s_a + pltpu.roll(x, q, 1) * s_b
        o_ref[...] = (rot * q_scale).astype(BF16)

    @pl.when(j == 2)
    def _():
        o_ref[...] = x.astype(BF16)


def _rope(qkv, cos, s_a, s_b, *, tr):
    b_n, l_n, d3 = qkv.shape
    d = d3 // 3
    tab = pl.BlockSpec((tr, DA_HEAD), lambda b, t, j: (t, 0))
    return pl.pallas_call(
        _rope_kernel,
        grid=(b_n, l_n // tr, 3),
        in_specs=[pl.BlockSpec((None, tr, d), lambda b, t, j: (b, t, j)), tab, tab, tab],
        out_specs=pl.BlockSpec((None, tr, d), lambda b, t, j: (b, t, j)),
        out_shape=jax.ShapeDtypeStruct((b_n, l_n, d3), BF16),
        compiler_params=_cparams(("parallel", "parallel", "parallel"), 40),
        name="rope",
    )(qkv, cos, s_a, s_b)


def _attn_kernel(q_ref, k_ref, v_ref, lam_ref, sg_ref, o_ref, *, ncb, n_ctx, lam_init):
    qi = pl.program_id(2)
    lv = lam_ref[...]
    lam = (jnp.exp(jnp.sum(lv[0:1] * lv[1:2], axis=-1, keepdims=True))
           - jnp.exp(jnp.sum(lv[2:3] * lv[3:4], axis=-1, keepdims=True)) + lam_init)

    def attend(nk):
        def probs(m):
            q = q_ref[:, m * DA_HEAD:(m + 1) * DA_HEAD]
            k = k_ref[0:nk, m * DA_HEAD:(m + 1) * DA_HEAD]
            s = lax.dot_general(q, k, NT, preferred_element_type=F32)
            e = jnp.exp2(s - jnp.max(s, axis=-1, keepdims=True))
            return e, 1.0 / jnp.sum(e, axis=-1, keepdims=True)
        e0, i0 = probs(0)
        e1, i1 = probs(1)
        v = v_ref[0:nk, :]
        o = (jnp.dot(e0.astype(BF16), v, preferred_element_type=F32) * i0
             - jnp.dot(e1.astype(BF16), v, preferred_element_type=F32) * (lam * i1))
        o = o * lax.rsqrt(jnp.mean(o * o, axis=-1, keepdims=True) + 1e-5) * sg_ref[...] * (1 - lam_init)
        o_ref[...] = o.astype(BF16)

    if ncb > 0:
        @pl.when(qi < ncb)
        def _():
            attend(n_ctx)

    @pl.when(qi >= ncb)
    def _():
        attend(k_ref.shape[0])


def _attention(qkv, lam_vec, sub_g, *, tq, n_ctx, lam_init):
    b_n, l_n, d3 = qkv.shape
    d = d3 // 3
    hw = 2 * DA_HEAD
    nh = d // hw
    return pl.pallas_call(
        functools.partial(_attn_kernel, ncb=n_ctx // tq, n_ctx=n_ctx, lam_init=lam_init),
        grid=(b_n, nh, l_n // tq),
        in_specs=[pl.BlockSpec((None, tq, hw), lambda b, h, t: (b, t, h)),
                  pl.BlockSpec((None, l_n, hw), lambda b, h, t: (b, 0, nh + h)),
                  pl.BlockSpec((None, l_n, hw), lambda b, h, t: (b, 0, 2 * nh + h)),
                  pl.BlockSpec((4, DA_HEAD), lambda b, h, t: (0, 0)),
                  pl.BlockSpec((1, hw), lambda b, h, t: (0, 0))],
        out_specs=pl.BlockSpec((None, tq, hw), lambda b, h, t: (b, t, h)),
        out_shape=jax.ShapeDtypeStruct((b_n, l_n, d), BF16),
        compiler_params=_cparams(("parallel", "parallel", "arbitrary"), 48),
        name="diff_attn",
    )(qkv, qkv, qkv, lam_vec, sub_g.reshape(1, hw))


def _rope_tables(n_ctx, n_lat):
    n_rows = n_lat // GRID_W
    row = jnp.repeat(jnp.arange(n_rows, dtype=F32), GRID_W)
    col = jnp.tile(jnp.arange(GRID_W, dtype=F32), n_rows)
    nf = DA_HEAD // 4
    inv_freq = ROPE_BASE ** (-jnp.arange(nf, dtype=F32) / nf)
    ang_r, ang_c = row[:, None] * inv_freq, col[:, None] * inv_freq
    ang = jnp.concatenate([ang_r, ang_r, ang_c, ang_c], axis=-1)
    ang = jnp.concatenate([jnp.zeros((n_ctx, DA_HEAD), F32), ang], axis=0)
    cos, sin = jnp.cos(ang), jnp.sin(ang)
    even_q = (jnp.arange(DA_HEAD) // nf) % 2 == 0
    return cos, jnp.where(even_q, -sin, 0.0), jnp.where(even_q, 0.0, sin)


def _diff_attention_layer(h, n_ctx, layer_idx, w_qkv, lam_vec, sub_g, w_o, *, tr):
    b_n, l_n, d = h.shape
    m_n = b_n * l_n
    qkv = _mm2(h.reshape(m_n, d), w_qkv).reshape(b_n, l_n, 3 * d)
    cos, s_a, s_b = _rope_tables(n_ctx, l_n - n_ctx)
    qkv = _rope(qkv, cos, s_a, s_b, tr=tr)
    lam_init = 0.8 - 0.6 * math.exp(-0.3 * layer_idx)
    o = _attention(qkv, lam_vec, sub_g, tq=tr, n_ctx=n_ctx, lam_init=lam_init)
    return _mm2(o.reshape(m_n, d), w_o).reshape(b_n, l_n, d)


def _hgrn_kernel(q_ref, i_ref, g_ref, ff_ref, fb_ref, low_ref, ng_ref, o_ref, o_scr, *, n_ctx, layer_idx):
    t_n = CHUNK
    l_n = q_ref.shape[0]
    nc, ncc = l_n // t_n, n_ctx // t_n
    r64 = lax.broadcasted_iota(jnp.int32, (t_n, t_n), 0)
    c64 = lax.broadcasted_iota(jnp.int32, (t_n, t_n), 1)

    def rows_of(c):
        return pl.ds(pl.multiple_of(c * t_n, t_n), t_n)

    f_refs = (ff_ref, fb_ref)
    lbs, incl = [], []
    tri3_b = [_cumsum_matrix(t_n, rev) for rev in (False, True)]
    for d in (0, 1):
        low = low_ref[d]
        e = jnp.exp(low - jnp.max(low, axis=0, keepdims=True))
        sm = e / jnp.sum(e, axis=0, keepdims=True)
        cs = sm[0:1]
        for rr in range(1, layer_idx + 1):
            cs = cs + sm[rr:rr + 1]
        lbs.append(cs - sm[0:1])
        incl.append((c64 >= r64) if d == 1 else (c64 <= r64))
    group = _pick(nc, (4, 2, 1))

    def stage_cum(d, c):
        rows = rows_of(c)
        f = lbs[d] + (1.0 - lbs[d]) * _sigmoid(f_refs[d][rows, :])
        return dict(d=d, rows=rows, f=f, cum=_cumsum(tri3_b[d], jnp.log(f)))

    def stage_att(s):
        d, cum, rows = s["d"], s["cum"], s["rows"]
        b_end = cum[0:1, :] if d == 1 else cum[t_n - 1:t_n, :]
        qv = q_ref[rows, :]
        qd = (qv * _sigmoid(qv) * jnp.exp(cum)).astype(BF16)
        kk = 1.0 - s["f"]
        v = i_ref[rows, :]
        kd = (kk * jnp.exp(-cum)).astype(BF16)
        ke = (kk * jnp.exp(b_end - cum)).astype(BF16)
        return dict(d=d, rows=rows, qd=qd, vb=v.astype(BF16), dec=jnp.exp(b_end),
                    att=lax.dot_general(qd, kd, NT, preferred_element_type=F32),
                    upd=jnp.dot(v.T.astype(BF16), ke, preferred_element_type=F32))

    def stage_intra(s):
        att = jnp.where(incl[s["d"]], s["att"], 0.0).astype(BF16)
        return dict(s, o=jnp.dot(att, s["vb"], preferred_element_type=F32))

    def body(i, states):
        items = [(d, _chunk_of(i * group + g, ncc, nc, d == 1)) for g in range(group) for d in (0, 1)]
        sts = [stage_cum(d, c) for d, c in items]
        sts = [stage_att(s) for s in sts]
        sts = [stage_intra(s) for s in sts]
        states = list(states)
        for s in sts:
            d = s["d"]
            o = s["o"] + lax.dot_general(s["qd"], states[d].astype(BF16), NT, preferred_element_type=F32)
            o_scr[d, s["rows"], :] = o
            states[d] = states[d] * s["dec"] + s["upd"]
        return tuple(states)

    zero = jnp.zeros((LANES, LANES), F32)
    lax.fori_loop(0, nc // group, body, (zero, zero))

    def post(c, carry):
        rows = rows_of(c)
        o = o_scr[0, rows, :] + o_scr[1, rows, :]
        o = o * lax.rsqrt(jnp.mean(o * o, axis=-1, keepdims=True) + 1e-5) * ng_ref[...]
        gv = g_ref[rows, :]
        o_ref[rows, :] = (o * (gv * _sigmoid(gv))).astype(BF16)
        return carry

    lax.fori_loop(0, nc, post, 0)


def _hgrn2_layer(h, n_ctx, layer_idx, w_in, lower, norm_g, w_o):
    b_n, l_n, d = h.shape
    m_n = b_n * l_n
    nh = d // HG_EXPAND
    proj = _mm2(h.reshape(m_n, d), w_in).reshape(b_n, l_n, 5 * d)
    col = lambda n: pl.BlockSpec((None, l_n, LANES), lambda b, p, n=n: (b, 0, n * nh + p))
    o = pl.pallas_call(
        functools.partial(_hgrn_kernel, n_ctx=n_ctx, layer_idx=layer_idx),
        grid=(b_n, nh),
        in_specs=[col(0), col(1), col(2), col(3), col(4),
                  pl.BlockSpec((2, lower.shape[1], LANES), lambda b, p: (0, 0, p)),
                  pl.BlockSpec((1, LANES), lambda b, p: (0, 0))],
        out_specs=pl.BlockSpec((None, l_n, LANES), lambda b, p: (b, 0, p)),
        out_shape=jax.ShapeDtypeStruct((b_n, l_n, d), BF16),
        scratch_shapes=[pltpu.VMEM((2, l_n, LANES), F32)],
        compiler_params=_cparams(("parallel", "parallel"), 40),
        name="hgrn_scan",
    )(proj, proj, proj, proj, proj, lower, norm_g.reshape(1, LANES))
    return _mm2(o.reshape(m_n, d), w_o).reshape(b_n, l_n, d)


def _gelu_tanh(x):
    return 0.5 * x * (1.0 + jnp.tanh(math.sqrt(2.0 / math.pi) * (x + 0.044715 * (x * x * x))))


def _softplus(x):
    return jnp.maximum(x, 0.0) + jnp.log1p(jnp.exp(-jnp.abs(x)))


SEG_PAD = 8


def _lin_scan(a_ref, u_ref, hl_s, cp_s, h_s, base, row0, n, rev, h_in, accumulate):
    seg = n // 8
    stride = seg + SEG_PAD
    n_p = a_ref.shape[0]

    def step(i, carry):
        t = (seg - 1 - i) if rev else i
        idx = pl.ds(base + t, 8, stride=stride)
        out = []
        for j in range(n_p):
            hl, cp = carry[j]
            a = a_ref[j, idx, :]
            hl = a * hl + u_ref[j, idx, :]
            cp = a * cp
            hl_s[j, idx, :] = hl
            cp_s[j, idx, :] = cp
            out.append((hl, cp))
        return tuple(out)

    init = tuple((jnp.zeros((8, LANES), F32), jnp.ones((8, LANES), F32)) for _ in range(n_p))
    ends = lax.fori_loop(0, seg, step, init)
    order = range(7, -1, -1) if rev else range(8)
    h_out = []
    for j in range(n_p):
        hl_e, cp_e = ends[j]
        carry = h_in[j]
        for s in order:
            r0, p0 = row0 + s * seg, base + s * stride
            blk = hl_s[j, p0:p0 + seg, :] + cp_s[j, p0:p0 + seg, :] * carry
            if accumulate:
                h_s[j, r0:r0 + seg, :] += blk
            else:
                h_s[j, r0:r0 + seg, :] = blk
            carry = hl_e[s:s + 1, :] + cp_e[s:s + 1, :] * carry
        h_out.append(carry)
    return h_out


def _rglru_kernel(gb_ref, xb_ref, cw_ref, cb_ref, wg_ref, bg_ref, lam_ref, o_ref, a_s, u_s, h_s, hl_s, cp_s, *,
                  n_ctx):
    l_n = xb_ref.shape[0]
    n_lat = l_n - n_ctx
    x = xb_ref[...]
    row = lax.broadcasted_iota(jnp.int32, x.shape, 0)
    k_w = cw_ref.shape[0]
    xc = cb_ref[...] + sum(_seg_shift(x, row, j - (k_w - 1) // 2, n_ctx) * cw_ref[j:j + 1, :]
                           for j in range(k_w))
    xcb = xc.astype(BF16)
    n_p = x.shape[1] // LANES
    for d in (0, 1):
        gate = lambda g: _sigmoid(jnp.dot(xcb, wg_ref[d, g].astype(BF16), preferred_element_type=F32)
                                  + bg_ref[d, g:g + 1, :])
        log_a = -LR_C * gate(0) * _softplus(-lam_ref[d:d + 1, :])
        a = jnp.exp(log_a)
        u = jnp.sqrt(jnp.tanh(-log_a) * (jnp.exp(2.0 * log_a) + 1.0)) * gate(1) * xc
        h = [jnp.zeros((1, LANES), F32)] * n_p
        base = 0
        for row0, n in ((0, n_ctx), (n_ctx, n_lat)):
            if n:
                seg = n // 8
                for j in range(n_p):
                    for s in range(8):
                        src = slice(row0 + s * seg, row0 + (s + 1) * seg)
                        dst = slice(base + s * (seg + SEG_PAD), base + s * (seg + SEG_PAD) + seg)
                        a_s[j, dst, :] = a[src, j * LANES:(j + 1) * LANES]
                        u_s[j, dst, :] = u[src, j * LANES:(j + 1) * LANES]
                h = _lin_scan(a_s, u_s, hl_s, cp_s, h_s, base, row0, n, d == 1, h, d == 1)
                base += 8 * (seg + SEG_PAD)
    for j in range(n_p):
        cols = slice(j * LANES, (j + 1) * LANES)
        o_ref[:, cols] = (h_s[j, n_ctx:l_n, :] * _gelu_tanh(gb_ref[n_ctx:, cols])).astype(BF16)


def _rglru_layer(h, n_ctx, w_in, conv_w, conv_b, w_gate, b_gate, lam, w_o):
    b_n, l_n, d = h.shape
    n_lat = l_n - n_ctx
    nb = d // LR_BS
    proj = _mm2(h.reshape(b_n * l_n, d), w_in).reshape(b_n, l_n, 2 * d)
    k_w = conv_w.shape[0]
    o = pl.pallas_call(
        functools.partial(_rglru_kernel, n_ctx=n_ctx),
        grid=(b_n, nb),
        in_specs=[pl.BlockSpec((None, l_n, LR_BS), lambda b, j: (b, 0, j)),
                  pl.BlockSpec((None, l_n, LR_BS), lambda b, j: (b, 0, nb + j)),
                  pl.BlockSpec((k_w, LR_BS), lambda b, j: (0, j)),
                  pl.BlockSpec((1, LR_BS), lambda b, j: (0, j)),
                  pl.BlockSpec((2, 2, None, LR_BS, LR_BS), lambda b, j: (0, 0, j, 0, 0)),
                  pl.BlockSpec((2, 2, LR_BS), lambda b, j: (0, 0, j)),
                  pl.BlockSpec((2, LR_BS), lambda b, j: (0, j))],
        out_specs=pl.BlockSpec((None, n_lat, LR_BS), lambda b, j: (b, 0, j)),
        out_shape=jax.ShapeDtypeStruct((b_n, n_lat, d), BF16),
        scratch_shapes=[pltpu.VMEM((LR_BS // LANES, l_n + 16 * SEG_PAD, LANES), F32)] * 5,
        compiler_params=_cparams(("parallel", "parallel"), 56),
        name="rglru",
    )(proj, proj, conv_w, conv_b.reshape(1, d), w_gate, b_gate, lam)
    return _mm2(o.reshape(b_n * n_lat, d), w_o).reshape(b_n, n_lat, d)


def _ffn_up_kernel(h_ref, wg_ref, wv_ref, cg_ref, cv_ref, bg_ref, bv_ref, o_ref, u_scr, *, n_ctx, tr):
    l_n, tf = o_ref.shape
    k_w = cg_ref.shape[0]
    half = (k_w - 1) // 2
    n_buf, pad = u_scr.shape[0], (u_scr.shape[1] - l_n) // 2
    rblk = _pick(l_n, (768, 1024, 512, 256))
    row = lax.broadcasted_iota(jnp.int32, (tr, LANES), 0)
    for p in range(n_buf):
        u_scr[p, 0:pad, :] = jnp.zeros((pad, 2 * LANES), F32)
        u_scr[p, pad + l_n:, :] = jnp.zeros((pad, 2 * LANES), F32)

    def weights(s):
        cols = slice(s * LANES, (s + 1) * LANES)
        return jnp.concatenate([wg_ref[:, cols], wv_ref[:, cols]], axis=1).astype(BF16)

    def product(s, w, r0):
        u_scr[s % n_buf, pad + r0:pad + r0 + rblk, :] = jnp.dot(h_ref[r0:r0 + rblk, :], w,
                                                                preferred_element_type=F32)

    def finish(s, r0):
        p, cols = s % n_buf, slice(s * LANES, (s + 1) * LANES)

        def conv(lane0, w_ref, b_ref):
            acc = None
            for j in range(k_w):
                sh = j - half
                x = u_scr[p, pad + r0 + sh:pad + r0 + sh + tr, lane0:lane0 + LANES]
                if sh < 0 and r0 in (0, n_ctx):
                    x = jnp.where(row < -sh, 0.0, x)
                if sh > 0 and r0 + tr in (n_ctx, l_n):
                    x = jnp.where(row >= tr - sh, 0.0, x)
                t = x * w_ref[j:j + 1, cols]
                acc = t if acc is None else acc + t
            return b_ref[:, cols] + acc
        gate = conv(0, cg_ref, bg_ref)
        val = conv(LANES, cv_ref, bv_ref)
        o_ref[r0:r0 + tr, cols] = (gate * _sigmoid(gate) * val).astype(BF16)

    n_s = tf // LANES
    for s in range(n_s + 1):
        w = weights(s) if s < n_s else None
        for r0 in range(0, l_n, rblk):
            if s < n_s:
                product(s, w, r0)
            if s > 0:
                for r1 in range(r0, r0 + rblk, tr):
                    finish(s - 1, r1)


def _conv_ffn(h, n_ctx, w_up, conv_w, conv_b, w_down, *, tr):
    b_n, l_n, d = h.shape
    f = w_down.shape[0]
    tf = _pick(f, (512, 256, 128))
    nf = f // tf
    k_w = conv_w.shape[0]
    cb = conv_b.reshape(1, 2 * f)
    act = pl.pallas_call(
        functools.partial(_ffn_up_kernel, n_ctx=n_ctx, tr=tr),
        grid=(b_n, nf),
        in_specs=[pl.BlockSpec((None, l_n, d), lambda b, j: (b, 0, 0), pipeline_mode=pl.Buffered(1)),
                  pl.BlockSpec((d, tf), lambda b, j: (0, j)),
                  pl.BlockSpec((d, tf), lambda b, j: (0, nf + j)),
                  pl.BlockSpec((k_w, tf), lambda b, j: (0, j)),
                  pl.BlockSpec((k_w, tf), lambda b, j: (0, nf + j)),
                  pl.BlockSpec((1, tf), lambda b, j: (0, j)),
                  pl.BlockSpec((1, tf), lambda b, j: (0, nf + j))],
        out_specs=pl.BlockSpec((None, l_n, tf), lambda b, j: (b, 0, j)),
        out_shape=jax.ShapeDtypeStruct((b_n, l_n, f), BF16),
        scratch_shapes=[pltpu.VMEM((3, l_n + 16, 2 * LANES), F32)],
        compiler_params=_cparams(("parallel", "arbitrary"), 56),
        name="ffn_up",
    )(h, w_up, w_up, conv_w, conv_w, cb, cb)
    return _mm2(act.reshape(b_n * l_n, f), w_down).reshape(b_n, l_n, d)


def kernel(x, c, ctx, c_ctx, ada_w, ada_b, ln_g, ln_b, ffn_w_up, ffn_conv_w, ffn_conv_b, ffn_w_down, rw_mu, rw_w_rkv, rw_w0, rw_w1, rw_w2, rw_a0, rw_a1, rw_a2, rw_g1, rw_g2, rw_k_k, rw_k_a, rw_r_k, rw_gn_g, rw_gn_b, rw_w_o, da_w_qkv, da_lambda, da_sub_g, da_w_o, hg_w_in, hg_lower, hg_norm_g, hg_w_o, lr_w_in, lr_conv_w, lr_conv_b, lr_w_gate, lr_b_gate, lr_lambda, lr_w_o):
    b_n, n_lat, d = x.shape
    n_ctx = ctx.shape[1]
    depth = ada_w.shape[0]
    assert depth == 4 and rw_mu.shape[0] == 1, "one occurrence of each of the four mixers"
    assert b_n + 1 <= 8 and n_ctx % CHUNK == 0 and n_lat % CHUNK == 0
    tr = math.gcd(math.gcd(n_ctx, n_lat), 256)
    alpha = (2 * depth) ** 0.25

    c8 = jnp.concatenate([c, c_ctx[None], jnp.zeros((8 - b_n - 1, d), F32)], axis=0)
    m = _ada(c8, ada_w, ada_b)
    m_lat = m[:, :b_n].reshape(depth, b_n, 1, 6, d)
    m_ctx = jnp.broadcast_to(m[:, b_n].reshape(depth, 1, 1, 6, d), (depth, b_n, 1, 6, d))
    mod = jnp.concatenate([m_ctx, m_lat], axis=2)

    z = jnp.concatenate([ctx, x], axis=1)
    h = None
    for i in range(depth):
        last = i == depth - 1
        if i == 0:
            y = _rwkv7_layer(z, mod[0], n_ctx, rw_mu[0], rw_w_rkv[0], rw_w0[0], rw_w1[0], rw_w2[0], rw_a0[0],
                             rw_a1[0], rw_a2[0], rw_g1[0], rw_g2[0], rw_k_k[0], rw_k_a[0], rw_r_k[0],
                             rw_gn_g[0], rw_gn_b[0], rw_w_o[0])
        elif i == 1:
            y = _diff_attention_layer(h, n_ctx, i, da_w_qkv[0], da_lambda[0], da_sub_g[0], da_w_o[0], tr=tr)
        elif i == 2:
            y = _hgrn2_layer(h, n_ctx, i, hg_w_in[0], hg_lower, hg_norm_g[0], hg_w_o[0])
        else:
            y = _rglru_layer(h, n_ctx, lr_w_in[0], lr_conv_w[0], lr_conv_b[0], lr_w_gate[0], lr_b_gate[0],
                             lr_lambda[0], lr_w_o[0])
        z, h = _ln_mod(z, y, mod[i], ln_g[i, 0], ln_b[i, 0], mod[i], gate_j=2, mod_j=3, tr=tr,
                       n_ctx=n_ctx, alpha=alpha)
        if last:
            n_ctx = 0
        y = _conv_ffn(h, n_ctx, ffn_w_up[i], ffn_conv_w[i], ffn_conv_b[i], ffn_w_down[i], tr=tr)
        z, h = _ln_mod(z, y, mod[i], ln_g[i, 1], ln_b[i, 1], mod[min(i + 1, depth - 1)], gate_j=5,
                       mod_j=None if last else 0, tr=tr, n_ctx=n_ctx, alpha=alpha)
    return z
```

```python
import functools
import math

import jax
import jax.numpy as jnp
from jax import lax
from jax.experimental import pallas as pl
from jax.experimental.pallas import tpu as pltpu

F32, BF16 = jnp.float32, jnp.bfloat16

LANES = 128
CHUNK = 64
LN_EPS = 1e-5
GRID_W = 64
ROPE_BASE = 10000.0
RW_HEAD = 64
RW_DECAY_SCALE = 0.606531
RW_GN_EPS = 64e-5
DA_HEAD = 128
HG_EXPAND = 128
LR_BS = 256
LR_C = 8.0
MIB = 1024 * 1024


def _pick(n, cands):
    for c in cands:
        if n % c == 0:
            return c
    return n


def _cparams(sem, vmem_mib):
    return pltpu.CompilerParams(dimension_semantics=sem, vmem_limit_bytes=vmem_mib * MIB)


def _sigmoid(x):
    return jax.nn.sigmoid(x)


NN = (((1,), (0,)), ((), ()))
NT = (((1,), (1,)), ((), ()))


def _parts(x, n):
    out = []
    for i in range(n):
        p = x.astype(BF16)
        out.append(p)
        if i + 1 < n:
            x = x - p.astype(F32)
    return out


def _mdot(ap, bp, dims=NN, order=2):
    pairs = [(a, b) for i, a in enumerate(ap) for j, b in enumerate(bp) if i + j < order]
    (ca,), (cb,) = dims[0]
    lhs = jnp.concatenate([a for a, _ in pairs], axis=ca) if len(pairs) > 1 else pairs[0][0]
    rhs = jnp.concatenate([b for _, b in pairs], axis=cb) if len(pairs) > 1 else pairs[0][1]
    return lax.dot_general(lhs, rhs, dims, preferred_element_type=F32)


def _cumsum_matrix(t_n, rev):
    r = lax.broadcasted_iota(jnp.int32, (t_n, 3 * t_n), 0)
    c = lax.broadcasted_iota(jnp.int32, (t_n, 3 * t_n), 1) % t_n
    return jnp.where((c >= r) if rev else (c <= r), 1.0, 0.0).astype(BF16)


def _cumsum(tri3_b, x):
    return jnp.dot(tri3_b, jnp.concatenate(_parts(x, 3), axis=0), preferred_element_type=F32)


def _cat_parts(xs, axis):
    return [jnp.concatenate(ps, axis=axis) for ps in zip(*xs)]


def _mm_kernel(a_ref, w_ref, o_ref, acc_ref, *, nk):
    if w_ref.dtype == BF16:
        prod = jnp.dot(a_ref[...], w_ref[...], preferred_element_type=F32)
    else:
        tk = w_ref.shape[0]
        n_kc = 2 if tk % (2 * LANES) == 0 else 1
        prod = None
        for c in range(n_kc):
            ks = slice(c * (tk // n_kc), (c + 1) * (tk // n_kc))
            t = jnp.dot(a_ref[:, ks], w_ref[ks, :].astype(BF16), preferred_element_type=F32)
            prod = t if prod is None else prod + t
    if nk == 1:
        o_ref[...] = prod.astype(o_ref.dtype)
    else:
        k = pl.program_id(3)

        @pl.when(k == 0)
        def _():
            acc_ref[...] = prod

        @pl.when(k > 0)
        def _():
            acc_ref[...] += prod

        @pl.when(k == nk - 1)
        def _():
            o_ref[...] = acc_ref[...].astype(o_ref.dtype)


def _mm_wres_kernel(a_ref, w_ref, o_ref, wb_ref):
    i = pl.program_id(2)
    k_n = w_ref.shape[0]
    n_kc = 4 if k_n % (4 * LANES) == 0 else 1

    @pl.when(i == 0)
    def _():
        kc = k_n // n_kc
        acc = None
        for c in range(n_kc):
            ks = slice(c * kc, (c + 1) * kc)
            wb = w_ref[ks, :].astype(BF16)
            wb_ref[ks, :] = wb
            t = jnp.dot(a_ref[:, ks], wb, preferred_element_type=F32)
            acc = t if acc is None else acc + t
        o_ref[...] = acc.astype(o_ref.dtype)

    @pl.when(i > 0)
    def _():
        o_ref[...] = jnp.dot(a_ref[...], wb_ref[...], preferred_element_type=F32).astype(o_ref.dtype)


def _matmul(a, w, *, out_dtype=F32, a_off=0):
    g_n, k_n, n_n = w.shape
    m_n = a.shape[1]
    tm = _pick(m_n, (1024, 512, 256, 128, 64))
    tn = _pick(n_n, (1024, 512, 256, 128))
    tk = k_n if k_n <= 2048 else _pick(k_n, (2816, 2048, 1024, 512))
    nk = k_n // tk
    if w.dtype == F32 and nk > 1:
        tn = _pick(n_n, (512, 256, 128))
    if w.dtype == F32 and nk == 1:
        return pl.pallas_call(
            _mm_wres_kernel,
            grid=(g_n, n_n // tn, m_n // tm),
            in_specs=[pl.BlockSpec((None, tm, k_n), lambda g, j, i: (g + a_off, i, 0)),
                      pl.BlockSpec((None, k_n, tn), lambda g, j, i: (g, 0, j))],
            out_specs=pl.BlockSpec((None, tm, tn), lambda g, j, i: (g, i, j)),
            out_shape=jax.ShapeDtypeStruct((g_n, m_n, n_n), out_dtype),
            scratch_shapes=[pltpu.VMEM((k_n, tn), BF16)],
            compiler_params=_cparams(("parallel", "parallel", "arbitrary"), 48),
            name="matmul_wres",
        )(a, w)
    return pl.pallas_call(
        functools.partial(_mm_kernel, nk=nk),
        grid=(g_n, m_n // tm, n_n // tn, nk),
        in_specs=[pl.BlockSpec((None, tm, tk), lambda g, i, j, k: (g + a_off, i, k)),
                  pl.BlockSpec((None, tk, tn), lambda g, i, j, k: (g, k, j))],
        out_specs=pl.BlockSpec((None, tm, tn), lambda g, i, j, k: (g, i, j)),
        out_shape=jax.ShapeDtypeStruct((g_n, m_n, n_n), out_dtype),
        scratch_shapes=[pltpu.VMEM((tm, tn), F32)],
        compiler_params=_cparams(("parallel", "parallel", "parallel", "arbitrary"), 48),
        name="matmul",
    )(a, w)


def _mm2(a, w, **kw):
    return _matmul(a[None], w[None], **kw)[0]


def _ada_kernel(c_ref, w_ref, b_ref, o_ref):
    c = c_ref[...]
    s = (c * _sigmoid(c)).astype(BF16)
    o_ref[...] = jnp.dot(s, w_ref[...].astype(BF16), preferred_element_type=F32) + b_ref[...]


def _ada(c8, ada_w, ada_b):
    depth, d, n = ada_w.shape
    tn = _pick(n, (1024, 512, 256, 128))
    return pl.pallas_call(
        _ada_kernel,
        grid=(depth, n // tn),
        in_specs=[pl.BlockSpec((8, d), lambda l, j: (0, 0)),
                  pl.BlockSpec((None, d, tn), lambda l, j: (l, 0, j)),
                  pl.BlockSpec((None, 1, tn), lambda l, j: (l, 0, j))],
        out_specs=pl.BlockSpec((None, 8, tn), lambda l, j: (l, 0, j)),
        out_shape=jax.ShapeDtypeStruct((depth, 8, n), F32),
        compiler_params=_cparams(("parallel", "parallel"), 40),
        name="ada",
    )(c8, ada_w, ada_b.reshape(depth, 1, n))


def _ln_mod_kernel(z_ref, y_ref, mod_ref, g_ref, b_ref, mod2_ref, *out_refs, gate_j, mod_j, alpha):
    m = mod_ref[...]
    zz = alpha * z_ref[...] + y_ref[...] * m[gate_j:gate_j + 1]
    mu = jnp.mean(zz, axis=-1, keepdims=True)
    zc = zz - mu
    var = jnp.mean(zc * zc, axis=-1, keepdims=True)
    zn = zc * lax.rsqrt(var + LN_EPS) * g_ref[...] + b_ref[...]
    out_refs[0][...] = zn
    if mod_j is not None:
        m2 = mod2_ref[...]
        out_refs[1][...] = (zn * (1 + m2[mod_j + 1:mod_j + 2]) + m2[mod_j:mod_j + 1]).astype(BF16)


def _ln_mod(z, y, mod, ln_g, ln_b, mod2, *, gate_j, mod_j, tr, n_ctx, alpha):
    b_n, l_z, d = z.shape
    l_y = y.shape[1]
    z_off = (l_z - l_y) // tr
    ncb = (n_ctx - (l_z - l_y)) // tr
    seg = lambda b, t: (b, jnp.where(t < ncb, 0, 1), 0, 0)
    row = pl.BlockSpec((None, tr, d), lambda b, t: (b, t, 0))
    out_shape = [jax.ShapeDtypeStruct((b_n, l_y, d), F32)]
    out_specs = [row]
    if mod_j is not None:
        out_shape.append(jax.ShapeDtypeStruct((b_n, l_y, d), BF16))
        out_specs.append(row)
    res = pl.pallas_call(
        functools.partial(_ln_mod_kernel, gate_j=gate_j, mod_j=mod_j, alpha=alpha),
        grid=(b_n, l_y // tr),
        in_specs=[pl.BlockSpec((None, tr, d), lambda b, t: (b, t + z_off, 0)),
                  row,
                  pl.BlockSpec((None, None, 6, d), seg),
                  pl.BlockSpec((1, d), lambda b, t: (0, 0)),
                  pl.BlockSpec((1, d), lambda b, t: (0, 0)),
                  pl.BlockSpec((None, None, 6, d), seg)],
        out_specs=out_specs,
        out_shape=out_shape,
        compiler_params=_cparams(("parallel", "parallel"), 40),
        name="ln_mod",
    )(z, y, mod, ln_g.reshape(1, d), ln_b.reshape(1, d), mod2)
    return res if mod_j is not None else (res[0], None)


def _proj_ln_kernel(a_ref, w_ref, z_ref, mod_ref, g_ref, b_ref, mod2_ref, zo_ref, *ho_refs,
                    nk, gate_j, mod_j, alpha, tr, ncb):
    t, k = pl.program_id(1), pl.program_id(2)
    tm = zo_ref.shape[0]
    rb = tm // 2 if tm % 16 == 0 else tm

    def accumulate(first):
        for r0 in range(0, tm, rb):
            p = jnp.dot(a_ref[r0:r0 + rb, :], w_ref[...], preferred_element_type=F32)
            if first:
                zo_ref[r0:r0 + rb, :] = p
            else:
                zo_ref[r0:r0 + rb, :] += p

    rs = math.gcd(tr, 64)

    def finish():
        for sb in range(tm // rs):
            rows = slice(sb * rs, (sb + 1) * rs)
            is_ctx = t * (tm // tr) + (sb * rs) // tr < ncb
            m = jnp.where(is_ctx, mod_ref[0], mod_ref[1])
            zz = alpha * z_ref[rows, :] + zo_ref[rows, :] * m[gate_j:gate_j + 1]
            mu = jnp.mean(zz, axis=-1, keepdims=True)
            zc = zz - mu
            var = jnp.mean(zc * zc, axis=-1, keepdims=True)
            zn = zc * lax.rsqrt(var + LN_EPS) * g_ref[...] + b_ref[...]
            zo_ref[rows, :] = zn
            if mod_j is not None:
                m2 = jnp.where(is_ctx, mod2_ref[0], mod2_ref[1])
                ho_refs[0][rows, :] = (zn * (1 + m2[mod_j + 1:mod_j + 2]) + m2[mod_j:mod_j + 1]).astype(BF16)

    if nk == 1:
        accumulate(True)
        finish()
    else:
        @pl.when(k == 0)
        def _():
            accumulate(True)

        @pl.when(k > 0)
        def _():
            accumulate(False)

        @pl.when(k == nk - 1)
        def _():
            finish()


def _proj_ln(a, w, z, mod, ln_g, ln_b, mod2, *, gate_j, mod_j, tr, n_ctx, alpha):
    b_n, l_a, k_n = a.shape
    d = w.shape[1]
    if z.shape[1] != l_a:
        y = _mm2(a.reshape(b_n * l_a, k_n), w).reshape(b_n, l_a, d)
        return _ln_mod(z, y, mod, ln_g, ln_b, mod2, gate_j=gate_j, mod_j=mod_j, tr=tr, n_ctx=n_ctx, alpha=alpha)
    tm = tr * _pick(l_a // tr, (3, 2, 1))
    tk = k_n if k_n <= 2048 else _pick(k_n, (1408, 1024, 512, 256, 128))
    nk = k_n // tk
    row = pl.BlockSpec((None, tm, d), lambda b, t, k: (b, t, 0))
    seg = pl.BlockSpec((None, 2, 6, d), lambda b, t, k: (b, 0, 0, 0))
    vec = pl.BlockSpec((1, d), lambda b, t, k: (0, 0))
    out_shape = [jax.ShapeDtypeStruct((b_n, l_a, d), F32)]
    out_specs = [row]
    if mod_j is not None:
        out_shape.append(jax.ShapeDtypeStruct((b_n, l_a, d), BF16))
        out_specs.append(row)
    res = pl.pallas_call(
        functools.partial(_proj_ln_kernel, nk=nk, gate_j=gate_j, mod_j=mod_j, alpha=alpha, tr=tr,
                          ncb=n_ctx // tr),
        grid=(b_n, l_a // tm, nk),
        in_specs=[pl.BlockSpec((None, tm, tk), lambda b, t, k: (b, t, k)),
                  pl.BlockSpec((tk, d), lambda b, t, k: (k, 0),
                               pipeline_mode=pl.Buffered(1) if nk == 1 else None),
                  row, seg, vec, vec, seg],
        out_specs=out_specs,
        out_shape=out_shape,
        compiler_params=_cparams(("parallel", "parallel", "arbitrary"), 56),
        name="proj_ln",
    )(a, w.astype(BF16), z, mod, ln_g.reshape(1, d), ln_b.reshape(1, d), mod2)
    return res if mod_j is not None else (res[0], None)


def _seg_shift(x, row, shift, n_ctx):
    l_n = x.shape[0]
    rolled = pltpu.roll(x, (-shift) % l_n, 0)
    src = row + shift
    same_seg = (src >= 0) & (src < l_n) & ((src < n_ctx) == (row < n_ctx))
    return jnp.where(same_seg, rolled, 0.0)


def _rw_mix_kernel(z_ref, mod_ref, mu_ref, o_ref, *, n_ctx):
    z = z_ref[...]
    row = lax.broadcasted_iota(jnp.int32, z.shape, 0)
    is_ctx = row < n_ctx
    shift = jnp.where(is_ctx, mod_ref[0, 0:1, :], mod_ref[1, 0:1, :])
    scale = jnp.where(is_ctx, mod_ref[0, 1:2, :], mod_ref[1, 1:2, :])
    h = z * (1 + scale) + shift
    dx = 0.5 * (_seg_shift(h, row, -1, n_ctx) + _seg_shift(h, row, 1, n_ctx)) - h
    for n in range(6):
        o_ref[n] = (h + dx * mu_ref[n:n + 1, :]).astype(BF16)


def _rw_mix(z, mod, mu, *, n_ctx):
    b_n, l_n, d = z.shape
    tc = _pick(d, (256, 128))
    return pl.pallas_call(
        functools.partial(_rw_mix_kernel, n_ctx=n_ctx),
        grid=(b_n, d // tc),
        in_specs=[pl.BlockSpec((None, l_n, tc), lambda b, j: (b, 0, j)),
                  pl.BlockSpec((None, 2, 6, tc), lambda b, j: (b, 0, 0, j)),
                  pl.BlockSpec((6, tc), lambda b, j: (0, j))],
        out_specs=pl.BlockSpec((6, None, l_n, tc), lambda b, j: (0, b, 0, j)),
        out_shape=jax.ShapeDtypeStruct((6, b_n, l_n, d), BF16),
        compiler_params=_cparams(("parallel", "parallel"), 48),
        name="rw_mix",
    )(z, mod, mu)


def _lora_kernel(x_ref, a_ref, b_ref, o_ref, *, act):
    t = jnp.dot(x_ref[...], a_ref[...], preferred_element_type=F32)
    if act == "tanh":
        t = jnp.tanh(t)
    elif act == "sigmoid":
        t = _sigmoid(t)
    o_ref[...] = jnp.dot(t.astype(BF16), b_ref[...], preferred_element_type=F32)


def _lora(xs, x_idx, a, b, act):
    g_n, d, r = a.shape
    m_n = xs.shape[1]
    tm = _pick(m_n, (512, 256, 128, 64))
    return pl.pallas_call(
        functools.partial(_lora_kernel, act=act),
        grid=(g_n, m_n // tm),
        in_specs=[pl.BlockSpec((None, tm, d), lambda g, i: (x_idx, i, 0)),
                  pl.BlockSpec((None, d, r), lambda g, i: (g, 0, 0)),
                  pl.BlockSpec((None, r, d), lambda g, i: (g, 0, 0))],
        out_specs=pl.BlockSpec((None, tm, d), lambda g, i: (g, i, 0)),
        out_shape=jax.ShapeDtypeStruct((g_n, m_n, d), F32),
        compiler_params=_cparams(("parallel", "parallel"), 40),
        name="lora",
    )(xs, a, b)


def _chunk_of(q, ncc, nc, rev):
    if not rev:
        return q
    return jnp.where(q < ncc, ncc - 1 - q, nc - 1 - (q - ncc))


def _rwkv_kernel(r_ref, k_ref, v_ref, lw_ref, la_ref, g_ref, w0_ref, a0_ref, kk_ref, ka_ref, rk_ref,
                 gng_ref, gnb_ref, o_ref,
                 y_scr, mr_s, n_s, *, n_ctx):
    t_n = CHUNK
    h2 = 2 * t_n
    l_n = r_ref.shape[0]
    nc, ncc = l_n // t_n, n_ctx // t_n
    group = _pick(nc, (4, 3, 2, 1))
    lane = lax.broadcasted_iota(jnp.int32, (1, LANES), 1)
    m1 = jnp.where(lane < RW_HEAD, 1.0, 0.0)
    m2 = 1.0 - m1
    ri = lax.broadcasted_iota(jnp.int32, (LANES, LANES), 0)
    ci = lax.broadcasted_iota(jnp.int32, (LANES, LANES), 1)
    same_head = (ri // RW_HEAD) == (ci // RW_HEAD)
    gsum_b = jnp.where(same_head, 1.0, 0.0).astype(BF16)
    gavg_b = jnp.where(same_head, 1.0 / RW_HEAD, 0.0).astype(BF16)
    eye = jnp.where(ri == ci, 1.0, 0.0)
    tr_i, tc_i = ri % t_n, ci % t_n
    k_k, k_a = kk_ref[...], ka_ref[...]

    def stack(x):
        return jnp.concatenate([x * m1, x * m2], axis=0)

    def rows_of(c):
        return pl.ds(pl.multiple_of(c * t_n, t_n), t_n)

    def head_sum(x, w_b):
        return _mdot(_parts(x, 3), [w_b], order=3)

    tri3_b = [_cumsum_matrix(t_n, rev) for rev in (False, True)]
    strict = [(tc_i > tr_i) if rev else (tc_i < tr_i) for rev in (False, True)]
    incl = [(tc_i >= tr_i) if rev else (tc_i <= tr_i) for rev in (False, True)]

    def stage_prep(c):
        rows = rows_of(c)
        k, r, v = k_ref[rows, :], r_ref[rows, :], v_ref[rows, :]
        kkr = k * k_k
        both = dict(kkr=kkr, ss=head_sum(kkr * kkr, gsum_b))
        items = []
        for d in (0, 1):
            lw = -RW_DECAY_SCALE * _sigmoid(w0_ref[d:d + 1, :] + lw_ref[d, rows, :])
            a = _sigmoid(a0_ref[d:d + 1, :] + la_ref[d, rows, :])
            items.append(dict(d=d, c=c, k=k, r=r, v=v, lw=lw, a=a, both=both, cum=_cumsum(tri3_b[d], lw)))
        return items

    def stage_amat(s):
        d, cum, lw, a, both = s["d"], s["cum"], s["lw"], s["a"], s["both"]
        if "kk" not in both:
            both["kk"] = both["kkr"] * lax.rsqrt(both["ss"] + 1e-12)
            both["vp"] = _parts(stack(s["v"]), 1)
        kk = both["kk"]
        kd = s["k"] * (1 + (a - 1) * k_a)
        bv = kk * a
        p_end = cum[0:1, :] if d == 1 else cum[t_n - 1:t_n, :]
        e_m = jnp.exp(-cum)
        e_h = jnp.exp(p_end - cum)
        ktp = _parts(stack(kk * jnp.exp(cum - lw)), 2)
        rt = stack(s["r"] * jnp.exp(cum))
        k2p = _cat_parts([_parts(stack(bv * e_m), 2), _parts(stack(kd * e_m), 2)], 0)
        return dict(d=d, c=s["c"], ktp=ktp, rt=rt, vp=both["vp"], p_end=p_end,
                    bh=stack(bv * e_h), kh=stack(kd * e_h),
                    amat=_mdot(_cat_parts([ktp, _parts(rt, 2)], 0), k2p, NT))

    def stage_square(s):
        d, amat = s["d"], s["amat"]
        lt = jnp.where(strict[d], amat[:h2, :h2], 0.0).T
        ltp = _parts(lt, 2)
        msk = jnp.concatenate([jnp.where(strict[d], amat[:h2, h2:], 0.0),
                               jnp.where(incl[d], amat[h2:, h2:], 0.0)], axis=0)
        s = dict(s, pt=eye - lt, xt=_mdot(ltp, ltp),
                 av=_mdot(_parts(msk, 2), s["vp"][:1]),
                 arbp=_parts(jnp.where(incl[d], amat[h2:, :h2], 0.0), 2))
        del s["amat"]
        return s

    def stage_double(s, final):
        xh = _parts(s["xt"], 1)
        ptp = _parts(s["pt"], 2)
        if final:
            return dict(s, pt=s["pt"] + _mdot(xh, ptp))
        rhs = [jnp.concatenate([ptp[0], xh[0]], axis=1), jnp.concatenate([ptp[1], jnp.zeros_like(xh[0])], axis=1)]
        both = _mdot(xh, rhs)
        return dict(s, pt=s["pt"] + both[:, :LANES], xt=both[:, LANES:])

    def stage_solve(s):
        rhs = jnp.concatenate([s["ktp"][0], (-s["av"][:h2]).astype(BF16)], axis=1)
        return dict(s, wub=_mdot(_parts(s["pt"].T, 2), [rhs]).astype(BF16))

    def stage_fold(s):
        d, c, wub = s["d"], s["c"], s["wub"]
        aw = _mdot(s["arbp"], [wub])
        zb = jnp.zeros((h2, LANES), BF16)
        lhs = _cat_parts([_parts(s["bh"].T, 2), _parts(s["kh"].T, 2)], 1)
        rhs = jnp.concatenate([wub, jnp.concatenate([zb, s["vp"][0]], axis=1)], axis=0)
        mn = _mdot(lhs, [rhs])
        dg = jnp.where(ri == ci, jnp.broadcast_to(jnp.exp(s["p_end"]), (LANES, LANES)), 0.0)
        rp = s["rt"] - aw[:, :LANES]
        mrp = _parts(jnp.concatenate([dg - mn[:, :LANES], rp[:t_n] + rp[t_n:]], axis=0), 2)
        for i in range(2):
            mr_s[d, c, i] = mrp[i]
        n_s[d, c] = mn[:, LANES:]
        y0 = s["av"][h2:] + aw[:, LANES:]
        return y0[:t_n] + y0[t_n:]

    def local(i, carry):
        sts = [s for g in range(group) for s in stage_prep(i * group + g)]
        sts = [stage_amat(s) for s in sts]
        sts = [stage_square(s) for s in sts]
        for step in range(5):
            sts = [stage_double(s, step == 4) for s in sts]
        sts = [stage_solve(s) for s in sts]
        y0 = [stage_fold(s) for s in sts]
        for g in range(group):
            y_scr[rows_of(i * group + g), :] = y0[2 * g] + y0[2 * g + 1]
        return carry

    lax.fori_loop(0, nc // group, local, 0)

    def seq(q, hs):
        cs = (q, _chunk_of(q, ncc, nc, True))
        mh = [_mdot([mr_s[d, cs[d], 0], mr_s[d, cs[d], 1]], _parts(hs[d], 2)) for d in (0, 1)]
        for d in (0, 1):
            y_scr[rows_of(cs[d]), :] += mh[d][h2:]
        return tuple(mh[d][:h2] + n_s[d, cs[d]] for d in (0, 1))

    zero = jnp.zeros((LANES, LANES), F32)
    lax.fori_loop(0, nc, seq, (zero, zero))

    n_post = _pick(nc, (4, 3, 2, 1))

    def post(i, carry):
        rows = [rows_of(i * n_post + g) for g in range(n_post)]

        def bonus_sum(rw):
            k, r = k_ref[rw, :], r_ref[rw, :]
            kd_f = k * (1 + (_sigmoid(a0_ref[0:1, :] + la_ref[0, rw, :]) - 1) * k_a)
            kd_b = k * (1 + (_sigmoid(a0_ref[1:2, :] + la_ref[1, rw, :]) - 1) * k_a)
            return head_sum(r * (kd_f + kd_b) * rk_ref[...], gsum_b)
        bsum = [bonus_sum(rw) for rw in rows]
        ys = [y_scr[rw, :] for rw in rows]
        ycs = [y - m for y, m in zip(ys, [head_sum(y, gavg_b) for y in ys])]
        var = [head_sum(yc * yc, gavg_b) for yc in ycs]
        for rw, yc, vr, bs in zip(rows, ycs, var, bsum):
            yn = yc * lax.rsqrt(vr + RW_GN_EPS) * gng_ref[...] + gnb_ref[...]
            o_ref[rw, :] = ((yn + bs * v_ref[rw, :]) * g_ref[rw, :]).astype(BF16)
        return carry

    lax.fori_loop(0, nc // n_post, post, 0)


def _rwkv_scan(rkv, lw, la, g, w0, a0, k_k, k_a, r_k, gn_g, gn_b, *, n_ctx):
    _, b_n, l_n, d = rkv.shape
    nc = l_n // CHUNK
    col = lambda n: pl.BlockSpec((None, None, l_n, LANES), lambda b, p, n=n: (n, b, 0, p))
    two = pl.BlockSpec((2, None, l_n, LANES), lambda b, p: (0, b, 0, p))
    par = lambda rows: pl.BlockSpec((rows, LANES), lambda b, p: (0, p))
    return pl.pallas_call(
        functools.partial(_rwkv_kernel, n_ctx=n_ctx),
        grid=(b_n, d // LANES),
        in_specs=[col(0), col(1), col(2), two, two,
                  pl.BlockSpec((None, l_n, LANES), lambda b, p: (b, 0, p)),
                  par(2), par(2), par(1), par(1), par(1), par(1), par(1)],
        out_specs=pl.BlockSpec((None, l_n, LANES), lambda b, p: (b, 0, p)),
        out_shape=jax.ShapeDtypeStruct((b_n, l_n, d), BF16),
        scratch_shapes=[pltpu.VMEM((l_n, LANES), F32),
                        pltpu.VMEM((2, nc, 2, LANES + CHUNK, LANES), BF16),
                        pltpu.VMEM((2, nc, LANES, LANES), F32)],
        compiler_params=_cparams(("parallel", "parallel"), 56),
        name="rwkv_scan",
    )(rkv, rkv, rkv, lw, la, g, w0, a0, k_k.reshape(1, d), k_a.reshape(1, d), r_k.reshape(1, d),
      gn_g.reshape(1, d), gn_b.reshape(1, d))


def _pad_axis(w, axis, to):
    pad = [(0, 0)] * w.ndim
    pad[axis] = (0, to - w.shape[axis])
    return jnp.pad(w, pad)


def _rwkv7_layer(z, mod, n_ctx, mu, w_rkv, w0, w1, w2, a0, a1, a2, g1, g2, k_k, k_a, r_k, gn_g, gn_b):
    b_n, l_n, d = z.shape
    m_n = b_n * l_n
    xs = _rw_mix(z, mod, mu, n_ctx=n_ctx).reshape(6, m_n, d)
    rkv = _matmul(xs, w_rkv)
    r_w = -(-w1.shape[-1] // LANES) * LANES
    r_a = -(-a1.shape[-1] // LANES) * LANES
    lw = _lora(xs, 3, _pad_axis(w1, 2, r_w).astype(BF16), _pad_axis(w2, 1, r_w).astype(BF16), "tanh")
    la = _lora(xs, 4, _pad_axis(a1, 2, r_a).astype(BF16), _pad_axis(a2, 1, r_a).astype(BF16), None)
    gate = _lora(xs, 5, g1[None].astype(BF16), g2[None].astype(BF16), "sigmoid")
    o = _rwkv_scan(rkv.reshape(3, b_n, l_n, d), lw.reshape(2, b_n, l_n, d), la.reshape(2, b_n, l_n, d),
                   gate.reshape(b_n, l_n, d), w0, a0, k_k, k_a, r_k, gn_g, gn_b, n_ctx=n_ctx)
    return o


def _rope_kernel(x_ref, cos_ref, sa_ref, sb_ref, o_ref):
    j = pl.program_id(2)
    x = x_ref[...]
    d = x.shape[1]

    @pl.when(j < 2)
    def _():
        rep = d // DA_HEAD
        cos = jnp.tile(cos_ref[...], (1, rep))
        s_a = jnp.tile(sa_ref[...], (1, rep))
        s_b = jnp.tile(sb_ref[...], (1, rep))
        q = DA_HEAD // 4
        q_scale = jnp.where(j == 0, DA_HEAD ** -0.5 * math.log2(math.e), 1.0)
        rot = x * cos + pltpu.roll(x, d - q, 1) * s_a + pltpu.roll(x, q, 1) * s_b
        o_ref[...] = (rot * q_scale).astype(BF16)

    @pl.when(j == 2)
    def _():
        o_ref[...] = x.astype(BF16)


def _rope(qkv, cos, s_a, s_b, *, tr):
    b_n, l_n, d3 = qkv.shape
    d = d3 // 3
    tab = pl.BlockSpec((tr, DA_HEAD), lambda b, t, j: (t, 0))
    return pl.pallas_call(
        _rope_kernel,
        grid=(b_n, l_n // tr, 3),
        in_specs=[pl.BlockSpec((None, tr, d), lambda b, t, j: (b, t, j)), tab, tab, tab],
        out_specs=pl.BlockSpec((None, tr, d), lambda b, t, j: (b, t, j)),
        out_shape=jax.ShapeDtypeStruct((b_n, l_n, d3), BF16),
        compiler_params=_cparams(("parallel", "parallel", "parallel"), 40),
        name="rope",
    )(qkv, cos, s_a, s_b)


def _attn_kernel(q_ref, k_ref, v_ref, lam_ref, sg_ref, o_ref, *, ncb, n_ctx, lam_init):
    qi = pl.program_id(2)
    lv = lam_ref[...]
    lam = (jnp.exp(jnp.sum(lv[0:1] * lv[1:2], axis=-1, keepdims=True))
           - jnp.exp(jnp.sum(lv[2:3] * lv[3:4], axis=-1, keepdims=True)) + lam_init)

    def attend(nk):
        def probs(m):
            q = q_ref[:, m * DA_HEAD:(m + 1) * DA_HEAD]
            k = k_ref[0:nk, m * DA_HEAD:(m + 1) * DA_HEAD]
            s = lax.dot_general(q, k, NT, preferred_element_type=F32)
            e = jnp.exp2(s - jnp.max(s, axis=-1, keepdims=True))
            return e, 1.0 / jnp.sum(e, axis=-1, keepdims=True)
        e0, i0 = probs(0)
        e1, i1 = probs(1)
        v = v_ref[0:nk, :]
        o = (jnp.dot(e0.astype(BF16), v, preferred_element_type=F32) * i0
             - jnp.dot(e1.astype(BF16), v, preferred_element_type=F32) * (lam * i1))
        o = o * lax.rsqrt(jnp.mean(o * o, axis=-1, keepdims=True) + 1e-5) * sg_ref[...] * (1 - lam_init)
        o_ref[...] = o.astype(BF16)

    if ncb > 0:
        @pl.when(qi < ncb)
        def _():
            attend(n_ctx)

    @pl.when(qi >= ncb)
    def _():
        attend(k_ref.shape[0])


def _attention(qkv, lam_vec, sub_g, *, tq, n_ctx, lam_init):
    b_n, l_n, d3 = qkv.shape
    d = d3 // 3
    hw = 2 * DA_HEAD
    nh = d // hw
    return pl.pallas_call(
        functools.partial(_attn_kernel, ncb=n_ctx // tq, n_ctx=n_ctx, lam_init=lam_init),
        grid=(b_n, nh, l_n // tq),
        in_specs=[pl.BlockSpec((None, tq, hw), lambda b, h, t: (b, t, h)),
                  pl.BlockSpec((None, l_n, hw), lambda b, h, t: (b, 0, nh + h)),
                  pl.BlockSpec((None, l_n, hw), lambda b, h, t: (b, 0, 2 * nh + h)),
                  pl.BlockSpec((4, DA_HEAD), lambda b, h, t: (0, 0)),
                  pl.BlockSpec((1, hw), lambda b, h, t: (0, 0))],
        out_specs=pl.BlockSpec((None, tq, hw), lambda b, h, t: (b, t, h)),
        out_shape=jax.ShapeDtypeStruct((b_n, l_n, d), BF16),
        compiler_params=_cparams(("parallel", "parallel", "arbitrary"), 48),
        name="diff_attn",
    )(qkv, qkv, qkv, lam_vec, sub_g.reshape(1, hw))


def _rope_tables(n_ctx, n_lat):
    n_rows = n_lat // GRID_W
    row = jnp.repeat(jnp.arange(n_rows, dtype=F32), GRID_W)
    col = jnp.tile(jnp.arange(GRID_W, dtype=F32), n_rows)
    nf = DA_HEAD // 4
    inv_freq = ROPE_BASE ** (-jnp.arange(nf, dtype=F32) / nf)
    ang_r, ang_c = row[:, None] * inv_freq, col[:, None] * inv_freq
    ang = jnp.concatenate([ang_r, ang_r, ang_c, ang_c], axis=-1)
    ang = jnp.concatenate([jnp.zeros((n_ctx, DA_HEAD), F32), ang], axis=0)
    cos, sin = jnp.cos(ang), jnp.sin(ang)
    even_q = (jnp.arange(DA_HEAD) // nf) % 2 == 0
    return cos, jnp.where(even_q, -sin, 0.0), jnp.where(even_q, 0.0, sin)


def _diff_attention_layer(h, n_ctx, layer_idx, w_qkv, lam_vec, sub_g, *, tr):
    b_n, l_n, d = h.shape
    m_n = b_n * l_n
    qkv = _mm2(h.reshape(m_n, d), w_qkv).reshape(b_n, l_n, 3 * d)
    cos, s_a, s_b = _rope_tables(n_ctx, l_n - n_ctx)
    qkv = _rope(qkv, cos, s_a, s_b, tr=tr)
    lam_init = 0.8 - 0.6 * math.exp(-0.3 * layer_idx)
    o = _attention(qkv, lam_vec, sub_g, tq=tr, n_ctx=n_ctx, lam_init=lam_init)
    return o


def _hgrn_kernel(q_ref, i_ref, g_ref, ff_ref, fb_ref, low_ref, ng_ref, o_ref, o_scr, *, n_ctx, layer_idx):
    t_n = CHUNK
    l_n = q_ref.shape[0]
    nc, ncc = l_n // t_n, n_ctx // t_n
    r64 = lax.broadcasted_iota(jnp.int32, (t_n, t_n), 0)
    c64 = lax.broadcasted_iota(jnp.int32, (t_n, t_n), 1)

    def rows_of(c):
        return pl.ds(pl.multiple_of(c * t_n, t_n), t_n)

    f_refs = (ff_ref, fb_ref)
    lbs, incl = [], []
    tri3_b = [_cumsum_matrix(t_n, rev) for rev in (False, True)]
    for d in (0, 1):
        low = low_ref[d]
        e = jnp.exp(low - jnp.max(low, axis=0, keepdims=True))
        sm = e / jnp.sum(e, axis=0, keepdims=True)
        cs = sm[0:1]
        for rr in range(1, layer_idx + 1):
            cs = cs + sm[rr:rr + 1]
        lbs.append(cs - sm[0:1])
        incl.append((c64 >= r64) if d == 1 else (c64 <= r64))
    group = _pick(nc, (4, 2, 1))

    def stage_cum(d, c):
        rows = rows_of(c)
        f = lbs[d] + (1.0 - lbs[d]) * _sigmoid(f_refs[d][rows, :])
        return dict(d=d, rows=rows, f=f, cum=_cumsum(tri3_b[d], jnp.log(f)))

    def stage_att(s):
        d, cum, rows = s["d"], s["cum"], s["rows"]
        b_end = cum[0:1, :] if d == 1 else cum[t_n - 1:t_n, :]
        qv = q_ref[rows, :]
        qd = (qv * _sigmoid(qv) * jnp.exp(cum)).astype(BF16)
        kk = 1.0 - s["f"]
        v = i_ref[rows, :]
        kd = (kk * jnp.exp(-cum)).astype(BF16)
        ke = (kk * jnp.exp(b_end - cum)).astype(BF16)
        return dict(d=d, rows=rows, qd=qd, vb=v.astype(BF16), dec=jnp.exp(b_end),
                    att=lax.dot_general(qd, kd, NT, preferred_element_type=F32),
                    upd=jnp.dot(v.T.astype(BF16), ke, preferred_element_type=F32))

    def stage_intra(s):
        att = jnp.where(incl[s["d"]], s["att"], 0.0).astype(BF16)
        return dict(s, o=jnp.dot(att, s["vb"], preferred_element_type=F32))

    def body(i, states):
        items = [(d, _chunk_of(i * group + g, ncc, nc, d == 1)) for g in range(group) for d in (0, 1)]
        sts = [stage_cum(d, c) for d, c in items]
        sts = [stage_att(s) for s in sts]
        sts = [stage_intra(s) for s in sts]
        states = list(states)
        for s in sts:
            d = s["d"]
            o = s["o"] + lax.dot_general(s["qd"], states[d].astype(BF16), NT, preferred_element_type=F32)
            o_scr[d, s["rows"], :] = o
            states[d] = states[d] * s["dec"] + s["upd"]
        return tuple(states)

    zero = jnp.zeros((LANES, LANES), F32)
    lax.fori_loop(0, nc // group, body, (zero, zero))

    def post(c, carry):
        rows = rows_of(c)
        o = o_scr[0, rows, :] + o_scr[1, rows, :]
        o = o * lax.rsqrt(jnp.mean(o * o, axis=-1, keepdims=True) + 1e-5) * ng_ref[...]
        gv = g_ref[rows, :]
        o_ref[rows, :] = (o * (gv * _sigmoid(gv))).astype(BF16)
        return carry

    lax.fori_loop(0, nc, post, 0)


def _hgrn2_layer(h, n_ctx, layer_idx, w_in, lower, norm_g):
    b_n, l_n, d = h.shape
    m_n = b_n * l_n
    nh = d // HG_EXPAND
    proj = _mm2(h.reshape(m_n, d), w_in).reshape(b_n, l_n, 5 * d)
    col = lambda n: pl.BlockSpec((None, l_n, LANES), lambda b, p, n=n: (b, 0, n * nh + p))
    o = pl.pallas_call(
        functools.partial(_hgrn_kernel, n_ctx=n_ctx, layer_idx=layer_idx),
        grid=(b_n, nh),
        in_specs=[col(0), col(1), col(2), col(3), col(4),
                  pl.BlockSpec((2, lower.shape[1], LANES), lambda b, p: (0, 0, p)),
                  pl.BlockSpec((1, LANES), lambda b, p: (0, 0))],
        out_specs=pl.BlockSpec((None, l_n, LANES), lambda b, p: (b, 0, p)),
        out_shape=jax.ShapeDtypeStruct((b_n, l_n, d), BF16),
        scratch_shapes=[pltpu.VMEM((2, l_n, LANES), F32)],
        compiler_params=_cparams(("parallel", "parallel"), 40),
        name="hgrn_scan",
    )(proj, proj, proj, proj, proj, lower, norm_g.reshape(1, LANES))
    return o


def _gelu_tanh(x):
    return 0.5 * x * (1.0 + jnp.tanh(math.sqrt(2.0 / math.pi) * (x + 0.044715 * (x * x * x))))


def _softplus(x):
    return jnp.maximum(x, 0.0) + jnp.log1p(jnp.exp(-jnp.abs(x)))


SEG_PAD = 8


def _lin_scan(a_ref, u_ref, hl_s, cp_s, h_s, base, row0, n, rev, h_in, accumulate):
    seg = n // 8
    stride = seg + SEG_PAD
    n_p = a_ref.shape[0]

    def step(i, carry):
        t = (seg - 1 - i) if rev else i
        idx = pl.ds(base + t, 8, stride=stride)
        out = []
        for j in range(n_p):
            hl, cp = carry[j]
            a = a_ref[j, idx, :]
            hl = a * hl + u_ref[j, idx, :]
            cp = a * cp
            hl_s[j, idx, :] = hl
            cp_s[j, idx, :] = cp
            out.append((hl, cp))
        return tuple(out)

    init = tuple((jnp.zeros((8, LANES), F32), jnp.ones((8, LANES), F32)) for _ in range(n_p))
    ends = lax.fori_loop(0, seg, step, init)
    order = range(7, -1, -1) if rev else range(8)
    h_out = []
    for j in range(n_p):
        hl_e, cp_e = ends[j]
        carry = h_in[j]
        for s in order:
            r0, p0 = row0 + s * seg, base + s * stride
            blk = hl_s[j, p0:p0 + seg, :] + cp_s[j, p0:p0 + seg, :] * carry
            if accumulate:
                h_s[j, r0:r0 + seg, :] += blk
            else:
                h_s[j, r0:r0 + seg, :] = blk
            carry = hl_e[s:s + 1, :] + cp_e[s:s + 1, :] * carry
        h_out.append(carry)
    return h_out


def _rglru_kernel(gb_ref, xb_ref, cw_ref, cb_ref, wg_ref, bg_ref, lam_ref, o_ref, a_s, u_s, h_s, hl_s, cp_s, *,
                  n_ctx):
    l_n = xb_ref.shape[0]
    n_lat = l_n - n_ctx
    x = xb_ref[...]
    row = lax.broadcasted_iota(jnp.int32, x.shape, 0)
    k_w = cw_ref.shape[0]
    xc = cb_ref[...] + sum(_seg_shift(x, row, j - (k_w - 1) // 2, n_ctx) * cw_ref[j:j + 1, :]
                           for j in range(k_w))
    xcb = xc.astype(BF16)
    n_p = x.shape[1] // LANES
    for d in (0, 1):
        gate = lambda g: _sigmoid(jnp.dot(xcb, wg_ref[d, g].astype(BF16), preferred_element_type=F32)
                                  + bg_ref[d, g:g + 1, :])
        log_a = -LR_C * gate(0) * _softplus(-lam_ref[d:d + 1, :])
        a = jnp.exp(log_a)
        u = jnp.sqrt(jnp.tanh(-log_a) * (jnp.exp(2.0 * log_a) + 1.0)) * gate(1) * xc
        h = [jnp.zeros((1, LANES), F32)] * n_p
        base = 0
        for row0, n in ((0, n_ctx), (n_ctx, n_lat)):
            if n:
                seg = n // 8
                for j in range(n_p):
                    for s in range(8):
                        src = slice(row0 + s * seg, row0 + (s + 1) * seg)
                        dst = slice(base + s * (seg + SEG_PAD), base + s * (seg + SEG_PAD) + seg)
                        a_s[j, dst, :] = a[src, j * LANES:(j + 1) * LANES]
                        u_s[j, dst, :] = u[src, j * LANES:(j + 1) * LANES]
                h = _lin_scan(a_s, u_s, hl_s, cp_s, h_s, base, row0, n, d == 1, h, d == 1)
                base += 8 * (seg + SEG_PAD)
    for j in range(n_p):
        cols = slice(j * LANES, (j + 1) * LANES)
        o_ref[:, cols] = (h_s[j, n_ctx:l_n, :] * _gelu_tanh(gb_ref[n_ctx:, cols])).astype(BF16)


def _rglru_layer(h, n_ctx, w_in, conv_w, conv_b, w_gate, b_gate, lam):
    b_n, l_n, d = h.shape
    n_lat = l_n - n_ctx
    nb = d // LR_BS
    proj = _mm2(h.reshape(b_n * l_n, d), w_in).reshape(b_n, l_n, 2 * d)
    k_w = conv_w.shape[0]
    o = pl.pallas_call(
        functools.partial(_rglru_kernel, n_ctx=n_ctx),
        grid=(b_n, nb),
        in_specs=[pl.BlockSpec((None, l_n, LR_BS), lambda b, j: (b, 0, j)),
                  pl.BlockSpec((None, l_n, LR_BS), lambda b, j: (b, 0, nb + j)),
                  pl.BlockSpec((k_w, LR_BS), lambda b, j: (0, j)),
                  pl.BlockSpec((1, LR_BS), lambda b, j: (0, j)),
                  pl.BlockSpec((2, 2, None, LR_BS, LR_BS), lambda b, j: (0, 0, j, 0, 0)),
                  pl.BlockSpec((2, 2, LR_BS), lambda b, j: (0, 0, j)),
                  pl.BlockSpec((2, LR_BS), lambda b, j: (0, j))],
        out_specs=pl.BlockSpec((None, n_lat, LR_BS), lambda b, j: (b, 0, j)),
        out_shape=jax.ShapeDtypeStruct((b_n, n_lat, d), BF16),
        scratch_shapes=[pltpu.VMEM((LR_BS // LANES, l_n + 16 * SEG_PAD, LANES), F32)] * 5,
        compiler_params=_cparams(("parallel", "parallel"), 56),
        name="rglru",
    )(proj, proj, conv_w, conv_b.reshape(1, d), w_gate, b_gate, lam)
    return o


def _ffn_up_kernel(h_ref, wg_ref, wv_ref, cg_ref, cv_ref, bg_ref, bv_ref, o_ref, u_scr, *, n_ctx, tr):
    l_n, tf = o_ref.shape
    k_w = cg_ref.shape[0]
    half = (k_w - 1) // 2
    n_buf, pad = u_scr.shape[0], (u_scr.shape[1] - l_n) // 2
    rblk = _pick(l_n, (768, 1024, 512, 256))
    row = lax.broadcasted_iota(jnp.int32, (tr, LANES), 0)
    for p in range(n_buf):
        u_scr[p, 0:pad, :] = jnp.zeros((pad, 2 * LANES), F32)
        u_scr[p, pad + l_n:, :] = jnp.zeros((pad, 2 * LANES), F32)

    def weights(s):
        cols = slice(s * LANES, (s + 1) * LANES)
        return jnp.concatenate([wg_ref[:, cols], wv_ref[:, cols]], axis=1).astype(BF16)

    def product(s, w, r0):
        u_scr[s % n_buf, pad + r0:pad + r0 + rblk, :] = jnp.dot(h_ref[r0:r0 + rblk, :], w,
                                                                preferred_element_type=F32)

    def finish(s, r0):
        p, cols = s % n_buf, slice(s * LANES, (s + 1) * LANES)

        def conv(lane0, w_ref, b_ref):
            acc = None
            for j in range(k_w):
                sh = j - half
                x = u_scr[p, pad + r0 + sh:pad + r0 + sh + tr, lane0:lane0 + LANES]
                if sh < 0 and r0 in (0, n_ctx):
                    x = jnp.where(row < -sh, 0.0, x)
                if sh > 0 and r0 + tr in (n_ctx, l_n):
                    x = jnp.where(row >= tr - sh, 0.0, x)
                t = x * w_ref[j:j + 1, cols]
                acc = t if acc is None else acc + t
            return b_ref[:, cols] + acc
        gate = conv(0, cg_ref, bg_ref)
        val = conv(LANES, cv_ref, bv_ref)
        o_ref[r0:r0 + tr, cols] = (gate * _sigmoid(gate) * val).astype(BF16)

    n_s = tf // LANES
    for s in range(n_s + 1):
        w = weights(s) if s < n_s else None
        for r0 in range(0, l_n, rblk):
            if s < n_s:
                product(s, w, r0)
            if s > 0:
                for r1 in range(r0, r0 + rblk, tr):
                    finish(s - 1, r1)


def _conv_ffn(h, n_ctx, w_up, conv_w, conv_b, *, tr):
    b_n, l_n, d = h.shape
    f = w_up.shape[1] // 2
    tf = _pick(f, (512, 256, 128))
    nf = f // tf
    k_w = conv_w.shape[0]
    cb = conv_b.reshape(1, 2 * f)
    act = pl.pallas_call(
        functools.partial(_ffn_up_kernel, n_ctx=n_ctx, tr=tr),
        grid=(b_n, nf),
        in_specs=[pl.BlockSpec((None, l_n, d), lambda b, j: (b, 0, 0), pipeline_mode=pl.Buffered(1)),
                  pl.BlockSpec((d, tf), lambda b, j: (0, j)),
                  pl.BlockSpec((d, tf), lambda b, j: (0, nf + j)),
                  pl.BlockSpec((k_w, tf), lambda b, j: (0, j)),
                  pl.BlockSpec((k_w, tf), lambda b, j: (0, nf + j)),
                  pl.BlockSpec((1, tf), lambda b, j: (0, j)),
                  pl.BlockSpec((1, tf), lambda b, j: (0, nf + j))],
        out_specs=pl.BlockSpec((None, l_n, tf), lambda b, j: (b, 0, j)),
        out_shape=jax.ShapeDtypeStruct((b_n, l_n, f), BF16),
        scratch_shapes=[pltpu.VMEM((3, l_n + 16, 2 * LANES), F32)],
        compiler_params=_cparams(("parallel", "arbitrary"), 56),
        name="ffn_up",
    )(h, w_up, w_up, conv_w, conv_w, cb, cb)
    return act


def kernel(x, c, ctx, c_ctx, ada_w, ada_b, ln_g, ln_b, ffn_w_up, ffn_conv_w, ffn_conv_b, ffn_w_down, rw_mu, rw_w_rkv, rw_w0, rw_w1, rw_w2, rw_a0, rw_a1, rw_a2, rw_g1, rw_g2, rw_k_k, rw_k_a, rw_r_k, rw_gn_g, rw_gn_b, rw_w_o, da_w_qkv, da_lambda, da_sub_g, da_w_o, hg_w_in, hg_lower, hg_norm_g, hg_w_o, lr_w_in, lr_conv_w, lr_conv_b, lr_w_gate, lr_b_gate, lr_lambda, lr_w_o):
    b_n, n_lat, d = x.shape
    n_ctx = ctx.shape[1]
    depth = ada_w.shape[0]
    assert depth == 4 and rw_mu.shape[0] == 1, "one occurrence of each of the four mixers"
    assert b_n + 1 <= 8 and n_ctx % CHUNK == 0 and n_lat % CHUNK == 0
    tr = math.gcd(math.gcd(n_ctx, n_lat), 256)
    alpha = (2 * depth) ** 0.25

    c8 = jnp.concatenate([c, c_ctx[None], jnp.zeros((8 - b_n - 1, d), F32)], axis=0)
    m = _ada(c8, ada_w, ada_b)
    m_lat = m[:, :b_n].reshape(depth, b_n, 1, 6, d)
    m_ctx = jnp.broadcast_to(m[:, b_n].reshape(depth, 1, 1, 6, d), (depth, b_n, 1, 6, d))
    mod = jnp.concatenate([m_ctx, m_lat], axis=2)

    z = jnp.concatenate([ctx, x], axis=1)
    h = None
    for i in range(depth):
        last = i == depth - 1
        if i == 0:
            o = _rwkv7_layer(z, mod[0], n_ctx, rw_mu[0], rw_w_rkv[0], rw_w0[0], rw_w1[0], rw_w2[0], rw_a0[0],
                             rw_a1[0], rw_a2[0], rw_g1[0], rw_g2[0], rw_k_k[0], rw_k_a[0], rw_r_k[0],
                             rw_gn_g[0], rw_gn_b[0])
            w_o = rw_w_o[0]
        elif i == 1:
            o = _diff_attention_layer(h, n_ctx, i, da_w_qkv[0], da_lambda[0], da_sub_g[0], tr=tr)
            w_o = da_w_o[0]
        elif i == 2:
            o = _hgrn2_layer(h, n_ctx, i, hg_w_in[0], hg_lower, hg_norm_g[0])
            w_o = hg_w_o[0]
        else:
            o = _rglru_layer(h, n_ctx, lr_w_in[0], lr_conv_w[0], lr_conv_b[0], lr_w_gate[0], lr_b_gate[0],
                             lr_lambda[0])
            w_o = lr_w_o[0]
        z, h = _proj_ln(o, w_o, z, mod[i], ln_g[i, 0], ln_b[i, 0], mod[i], gate_j=2, mod_j=3, tr=tr,
                        n_ctx=n_ctx, alpha=alpha)
        if last:
            n_ctx = 0
        act = _conv_ffn(h, n_ctx, ffn_w_up[i], ffn_conv_w[i], ffn_conv_b[i], tr=tr)
        z, h = _proj_ln(act, ffn_w_down[i], z, mod[i], ln_g[i, 1], ln_b[i, 1], mod[min(i + 1, depth - 1)],
                        gate_j=5, mod_j=None if last else 0, tr=tr, n_ctx=n_ctx, alpha=alpha)
    return z
```

```python
import functools
import math

import jax
import jax.numpy as jnp
from jax import lax
from jax.experimental import pallas as pl
from jax.experimental.pallas import tpu as pltpu

F32, BF16 = jnp.float32, jnp.bfloat16

LANES = 128
CHUNK = 64
LN_EPS = 1e-5
GRID_W = 64
ROPE_BASE = 10000.0
RW_HEAD = 64
RW_DECAY_SCALE = 0.606531
RW_GN_EPS = 64e-5
DA_HEAD = 128
HG_EXPAND = 128
LR_BS = 256
LR_C = 8.0
MIB = 1024 * 1024


def _pick(n, cands):
    for c in cands:
        if n % c == 0:
            return c
    return n


def _cparams(sem, vmem_mib):
    return pltpu.CompilerParams(dimension_semantics=sem, vmem_limit_bytes=vmem_mib * MIB)


def _sigmoid(x):
    return jax.nn.sigmoid(x)


NN = (((1,), (0,)), ((), ()))
NT = (((1,), (1,)), ((), ()))


def _parts(x, n):
    out = []
    for i in range(n):
        p = x.astype(BF16)
        out.append(p)
        if i + 1 < n:
            x = x - p.astype(F32)
    return out


def _mdot(ap, bp, dims=NN, order=2):
    pairs = [(a, b) for i, a in enumerate(ap) for j, b in enumerate(bp) if i + j < order]
    (ca,), (cb,) = dims[0]
    lhs = jnp.concatenate([a for a, _ in pairs], axis=ca) if len(pairs) > 1 else pairs[0][0]
    rhs = jnp.concatenate([b for _, b in pairs], axis=cb) if len(pairs) > 1 else pairs[0][1]
    return lax.dot_general(lhs, rhs, dims, preferred_element_type=F32)


def _cumsum_matrix(t_n, rev):
    r = lax.broadcasted_iota(jnp.int32, (t_n, 3 * t_n), 0)
    c = lax.broadcasted_iota(jnp.int32, (t_n, 3 * t_n), 1) % t_n
    return jnp.where((c >= r) if rev else (c <= r), 1.0, 0.0).astype(BF16)


def _cumsum(tri3_b, x):
    return jnp.dot(tri3_b, jnp.concatenate(_parts(x, 3), axis=0), preferred_element_type=F32)


def _cat_parts(xs, axis):
    return [jnp.concatenate(ps, axis=axis) for ps in zip(*xs)]


def _mm_kernel(a_ref, w_ref, o_ref, acc_ref, *, nk):
    if w_ref.dtype == BF16:
        prod = jnp.dot(a_ref[...], w_ref[...], preferred_element_type=F32)
    else:
        tk = w_ref.shape[0]
        n_kc = 2 if tk % (2 * LANES) == 0 else 1
        prod = None
        for c in range(n_kc):
            ks = slice(c * (tk // n_kc), (c + 1) * (tk // n_kc))
            t = jnp.dot(a_ref[:, ks], w_ref[ks, :].astype(BF16), preferred_element_type=F32)
            prod = t if prod is None else prod + t
    if nk == 1:
        o_ref[...] = prod.astype(o_ref.dtype)
    else:
        k = pl.program_id(3)

        @pl.when(k == 0)
        def _():
            acc_ref[...] = prod

        @pl.when(k > 0)
        def _():
            acc_ref[...] += prod

        @pl.when(k == nk - 1)
        def _():
            o_ref[...] = acc_ref[...].astype(o_ref.dtype)


def _mm_wres_kernel(a_ref, w_ref, o_ref, wb_ref):
    i = pl.program_id(2)
    k_n = w_ref.shape[0]
    n_kc = 4 if k_n % (4 * LANES) == 0 else 1

    @pl.when(i == 0)
    def _():
        kc = k_n // n_kc
        acc = None
        for c in range(n_kc):
            ks = slice(c * kc, (c + 1) * kc)
            wb = w_ref[ks, :].astype(BF16)
            wb_ref[ks, :] = wb
            t = jnp.dot(a_ref[:, ks], wb, preferred_element_type=F32)
            acc = t if acc is None else acc + t
        o_ref[...] = acc.astype(o_ref.dtype)

    @pl.when(i > 0)
    def _():
        o_ref[...] = jnp.dot(a_ref[...], wb_ref[...], preferred_element_type=F32).astype(o_ref.dtype)


def _matmul(a, w, *, out_dtype=F32, a_off=0):
    g_n, k_n, n_n = w.shape
    m_n = a.shape[1]
    tm = _pick(m_n, (1024, 512, 256, 128, 64))
    tn = _pick(n_n, (1024, 512, 256, 128))
    tk = k_n if k_n <= 2048 else _pick(k_n, (2816, 2048, 1024, 512))
    nk = k_n // tk
    if w.dtype == F32 and nk > 1:
        tn = _pick(n_n, (512, 256, 128))
    if w.dtype == F32 and nk == 1:
        return pl.pallas_call(
            _mm_wres_kernel,
            grid=(g_n, n_n // tn, m_n // tm),
            in_specs=[pl.BlockSpec((None, tm, k_n), lambda g, j, i: (g + a_off, i, 0)),
                      pl.BlockSpec((None, k_n, tn), lambda g, j, i: (g, 0, j))],
            out_specs=pl.BlockSpec((None, tm, tn), lambda g, j, i: (g, i, j)),
            out_shape=jax.ShapeDtypeStruct((g_n, m_n, n_n), out_dtype),
            scratch_shapes=[pltpu.VMEM((k_n, tn), BF16)],
            compiler_params=_cparams(("parallel", "parallel", "arbitrary"), 48),
            name="matmul_wres",
        )(a, w)
    return pl.pallas_call(
        functools.partial(_mm_kernel, nk=nk),
        grid=(g_n, m_n // tm, n_n // tn, nk),
        in_specs=[pl.BlockSpec((None, tm, tk), lambda g, i, j, k: (g + a_off, i, k)),
                  pl.BlockSpec((None, tk, tn), lambda g, i, j, k: (g, k, j))],
        out_specs=pl.BlockSpec((None, tm, tn), lambda g, i, j, k: (g, i, j)),
        out_shape=jax.ShapeDtypeStruct((g_n, m_n, n_n), out_dtype),
        scratch_shapes=[pltpu.VMEM((tm, tn), F32)],
        compiler_params=_cparams(("parallel", "parallel", "parallel", "arbitrary"), 48),
        name="matmul",
    )(a, w)


def _mm2(a, w, **kw):
    return _matmul(a[None], w[None], **kw)[0]


def _ada_kernel(c_ref, w_ref, b_ref, o_ref):
    c = c_ref[...]
    s = (c * _sigmoid(c)).astype(BF16)
    o_ref[...] = jnp.dot(s, w_ref[...].astype(BF16), preferred_element_type=F32) + b_ref[...]


def _ada(c8, ada_w, ada_b):
    depth, d, n = ada_w.shape
    tn = _pick(n, (1024, 512, 256, 128))
    return pl.pallas_call(
        _ada_kernel,
        grid=(depth, n // tn),
        in_specs=[pl.BlockSpec((8, d), lambda l, j: (0, 0)),
                  pl.BlockSpec((None, d, tn), lambda l, j: (l, 0, j)),
                  pl.BlockSpec((None, 1, tn), lambda l, j: (l, 0, j))],
        out_specs=pl.BlockSpec((None, 8, tn), lambda l, j: (l, 0, j)),
        out_shape=jax.ShapeDtypeStruct((depth, 8, n), F32),
        compiler_params=_cparams(("parallel", "parallel"), 40),
        name="ada",
    )(c8, ada_w, ada_b.reshape(depth, 1, n))


def _ln_mod_kernel(z_ref, y_ref, mod_ref, g_ref, b_ref, mod2_ref, *out_refs, gate_j, mod_j, alpha):
    m = mod_ref[...]
    zz = alpha * z_ref[...] + y_ref[...] * m[gate_j:gate_j + 1]
    mu = jnp.mean(zz, axis=-1, keepdims=True)
    zc = zz - mu
    var = jnp.mean(zc * zc, axis=-1, keepdims=True)
    zn = zc * lax.rsqrt(var + LN_EPS) * g_ref[...] + b_ref[...]
    out_refs[0][...] = zn
    if mod_j is not None:
        m2 = mod2_ref[...]
        out_refs[1][...] = (zn * (1 + m2[mod_j + 1:mod_j + 2]) + m2[mod_j:mod_j + 1]).astype(BF16)


def _ln_mod(z, y, mod, ln_g, ln_b, mod2, *, gate_j, mod_j, tr, n_ctx, alpha):
    b_n, l_z, d = z.shape
    l_y = y.shape[1]
    z_off = (l_z - l_y) // tr
    ncb = (n_ctx - (l_z - l_y)) // tr
    seg = lambda b, t: (b, jnp.where(t < ncb, 0, 1), 0, 0)
    row = pl.BlockSpec((None, tr, d), lambda b, t: (b, t, 0))
    out_shape = [jax.ShapeDtypeStruct((b_n, l_y, d), F32)]
    out_specs = [row]
    if mod_j is not None:
        out_shape.append(jax.ShapeDtypeStruct((b_n, l_y, d), BF16))
        out_specs.append(row)
    res = pl.pallas_call(
        functools.partial(_ln_mod_kernel, gate_j=gate_j, mod_j=mod_j, alpha=alpha),
        grid=(b_n, l_y // tr),
        in_specs=[pl.BlockSpec((None, tr, d), lambda b, t: (b, t + z_off, 0)),
                  row,
                  pl.BlockSpec((None, None, 6, d), seg),
                  pl.BlockSpec((1, d), lambda b, t: (0, 0)),
                  pl.BlockSpec((1, d), lambda b, t: (0, 0)),
                  pl.BlockSpec((None, None, 6, d), seg)],
        out_specs=out_specs,
        out_shape=out_shape,
        compiler_params=_cparams(("parallel", "parallel"), 40),
        name="ln_mod",
    )(z, y, mod, ln_g.reshape(1, d), ln_b.reshape(1, d), mod2)
    return res if mod_j is not None else (res[0], None)


def _cast_kernel(x_ref, o_ref):
    o_ref[...] = x_ref[...].astype(o_ref.dtype)


def _to_bf16(w):
    g_n, k_n, n_n = w.shape
    tk = _pick(k_n, (512, 256, 128))
    spec = pl.BlockSpec((None, tk, n_n), lambda g, i: (g, i, 0))
    return pl.pallas_call(
        _cast_kernel, grid=(g_n, k_n // tk), in_specs=[spec], out_specs=spec,
        out_shape=jax.ShapeDtypeStruct(w.shape, BF16),
        compiler_params=_cparams(("parallel", "parallel"), 32),
        name="to_bf16",
    )(w)


def _proj_ln_kernel(a_ref, w_ref, z_ref, mod_ref, g_ref, b_ref, mod2_ref, zo_ref, *ho_refs,
                    nk, gate_j, mod_j, alpha, tr, ncb):
    t, k = pl.program_id(1), pl.program_id(2)
    tm = zo_ref.shape[0]
    rb = tm // 2 if tm % 16 == 0 else tm

    def accumulate(first):
        for r0 in range(0, tm, rb):
            p = jnp.dot(a_ref[r0:r0 + rb, :], w_ref[...], preferred_element_type=F32)
            if first:
                zo_ref[r0:r0 + rb, :] = p
            else:
                zo_ref[r0:r0 + rb, :] += p

    rs = math.gcd(tr, 64)

    def finish():
        for sb in range(tm // rs):
            rows = slice(sb * rs, (sb + 1) * rs)
            is_ctx = t * (tm // tr) + (sb * rs) // tr < ncb
            m = jnp.where(is_ctx, mod_ref[0], mod_ref[1])
            zz = alpha * z_ref[rows, :] + zo_ref[rows, :] * m[gate_j:gate_j + 1]
            mu = jnp.mean(zz, axis=-1, keepdims=True)
            zc = zz - mu
            var = jnp.mean(zc * zc, axis=-1, keepdims=True)
            zn = zc * lax.rsqrt(var + LN_EPS) * g_ref[...] + b_ref[...]
            zo_ref[rows, :] = zn
            if mod_j is not None:
                m2 = jnp.where(is_ctx, mod2_ref[0], mod2_ref[1])
                ho_refs[0][rows, :] = (zn * (1 + m2[mod_j + 1:mod_j + 2]) + m2[mod_j:mod_j + 1]).astype(BF16)

    if nk == 1:
        accumulate(True)
        finish()
    else:
        @pl.when(k == 0)
        def _():
            accumulate(True)

        @pl.when(k > 0)
        def _():
            accumulate(False)

        @pl.when(k == nk - 1)
        def _():
            finish()


def _proj_ln(a, w, z, mod, ln_g, ln_b, mod2, *, gate_j, mod_j, tr, n_ctx, alpha):
    b_n, l_a, k_n = a.shape
    d = w.shape[1]
    if z.shape[1] != l_a:
        y = _mm2(a.reshape(b_n * l_a, k_n), w).reshape(b_n, l_a, d)
        return _ln_mod(z, y, mod, ln_g, ln_b, mod2, gate_j=gate_j, mod_j=mod_j, tr=tr, n_ctx=n_ctx, alpha=alpha)
    tm = tr * _pick(l_a // tr, (3, 2, 1))
    tk = k_n if k_n <= 2048 else _pick(k_n, (1408, 1024, 512, 256, 128))
    nk = k_n // tk
    row = pl.BlockSpec((None, tm, d), lambda b, t, k: (b, t, 0))
    seg = pl.BlockSpec((None, 2, 6, d), lambda b, t, k: (b, 0, 0, 0))
    vec = pl.BlockSpec((1, d), lambda b, t, k: (0, 0))
    out_shape = [jax.ShapeDtypeStruct((b_n, l_a, d), F32)]
    out_specs = [row]
    if mod_j is not None:
        out_shape.append(jax.ShapeDtypeStruct((b_n, l_a, d), BF16))
        out_specs.append(row)
    res = pl.pallas_call(
        functools.partial(_proj_ln_kernel, nk=nk, gate_j=gate_j, mod_j=mod_j, alpha=alpha, tr=tr,
                          ncb=n_ctx // tr),
        grid=(b_n, l_a // tm, nk),
        in_specs=[pl.BlockSpec((None, tm, tk), lambda b, t, k: (b, t, k)),
                  pl.BlockSpec((tk, d), lambda b, t, k: (k, 0),
                               pipeline_mode=pl.Buffered(1) if nk == 1 else None),
                  row, seg, vec, vec, seg],
        out_specs=out_specs,
        out_shape=out_shape,
        compiler_params=_cparams(("parallel", "parallel", "arbitrary"), 56),
        name="proj_ln",
    )(a, w if w.dtype == BF16 else _to_bf16(w[None])[0], z, mod, ln_g.reshape(1, d), ln_b.reshape(1, d), mod2)
    return res if mod_j is not None else (res[0], None)


def _seg_shift(x, row, shift, n_ctx):
    l_n = x.shape[0]
    rolled = pltpu.roll(x, (-shift) % l_n, 0)
    src = row + shift
    same_seg = (src >= 0) & (src < l_n) & ((src < n_ctx) == (row < n_ctx))
    return jnp.where(same_seg, rolled, 0.0)


def _rw_mix_kernel(z_ref, mod_ref, mu_ref, o_ref, *, n_ctx):
    z = z_ref[...]
    row = lax.broadcasted_iota(jnp.int32, z.shape, 0)
    is_ctx = row < n_ctx
    shift = jnp.where(is_ctx, mod_ref[0, 0:1, :], mod_ref[1, 0:1, :])
    scale = jnp.where(is_ctx, mod_ref[0, 1:2, :], mod_ref[1, 1:2, :])
    h = z * (1 + scale) + shift
    dx = 0.5 * (_seg_shift(h, row, -1, n_ctx) + _seg_shift(h, row, 1, n_ctx)) - h
    for n in range(6):
        o_ref[n] = (h + dx * mu_ref[n:n + 1, :]).astype(BF16)


def _rw_mix(z, mod, mu, *, n_ctx):
    b_n, l_n, d = z.shape
    tc = _pick(d, (256, 128))
    return pl.pallas_call(
        functools.partial(_rw_mix_kernel, n_ctx=n_ctx),
        grid=(b_n, d // tc),
        in_specs=[pl.BlockSpec((None, l_n, tc), lambda b, j: (b, 0, j)),
                  pl.BlockSpec((None, 2, 6, tc), lambda b, j: (b, 0, 0, j)),
                  pl.BlockSpec((6, tc), lambda b, j: (0, j))],
        out_specs=pl.BlockSpec((6, None, l_n, tc), lambda b, j: (0, b, 0, j)),
        out_shape=jax.ShapeDtypeStruct((6, b_n, l_n, d), BF16),
        compiler_params=_cparams(("parallel", "parallel"), 48),
        name="rw_mix",
    )(z, mod, mu)


def _lora_kernel(x_ref, a_ref, b_ref, o_ref, *, act):
    t = jnp.dot(x_ref[...], a_ref[...], preferred_element_type=F32)
    if act == "tanh":
        t = jnp.tanh(t)
    elif act == "sigmoid":
        t = _sigmoid(t)
    o_ref[...] = jnp.dot(t.astype(BF16), b_ref[...], preferred_element_type=F32)


def _lora(xs, x_idx, a, b, act):
    g_n, d, r = a.shape
    m_n = xs.shape[1]
    tm = _pick(m_n, (512, 256, 128, 64))
    return pl.pallas_call(
        functools.partial(_lora_kernel, act=act),
        grid=(g_n, m_n // tm),
        in_specs=[pl.BlockSpec((None, tm, d), lambda g, i: (x_idx, i, 0)),
                  pl.BlockSpec((None, d, r), lambda g, i: (g, 0, 0)),
                  pl.BlockSpec((None, r, d), lambda g, i: (g, 0, 0))],
        out_specs=pl.BlockSpec((None, tm, d), lambda g, i: (g, i, 0)),
        out_shape=jax.ShapeDtypeStruct((g_n, m_n, d), F32),
        compiler_params=_cparams(("parallel", "parallel"), 40),
        name="lora",
    )(xs, a, b)


def _chunk_of(q, ncc, nc, rev):
    if not rev:
        return q
    return jnp.where(q < ncc, ncc - 1 - q, nc - 1 - (q - ncc))


def _rwkv_kernel(r_ref, k_ref, v_ref, lw_ref, la_ref, g_ref, w0_ref, a0_ref, kk_ref, ka_ref, rk_ref,
                 gng_ref, gnb_ref, o_ref,
                 y_scr, mr_s, n_s, *, n_ctx):
    t_n = CHUNK
    h2 = 2 * t_n
    l_n = r_ref.shape[0]
    nc, ncc = l_n // t_n, n_ctx // t_n
    group = _pick(nc, (6, 4, 3, 2, 1))
    lane = lax.broadcasted_iota(jnp.int32, (1, LANES), 1)
    m1 = jnp.where(lane < RW_HEAD, 1.0, 0.0)
    m2 = 1.0 - m1
    ri = lax.broadcasted_iota(jnp.int32, (LANES, LANES), 0)
    ci = lax.broadcasted_iota(jnp.int32, (LANES, LANES), 1)
    same_head = (ri // RW_HEAD) == (ci // RW_HEAD)
    gsum_b = jnp.where(same_head, 1.0, 0.0).astype(BF16)
    gavg_b = jnp.where(same_head, 1.0 / RW_HEAD, 0.0).astype(BF16)
    eye = jnp.where(ri == ci, 1.0, 0.0)
    tr_i, tc_i = ri % t_n, ci % t_n
    k_k, k_a = kk_ref[...], ka_ref[...]

    def stack(x):
        return jnp.concatenate([x * m1, x * m2], axis=0)

    def rows_of(c):
        return pl.ds(pl.multiple_of(c * t_n, t_n), t_n)

    def head_sum(x, w_b):
        return _mdot(_parts(x, 3), [w_b], order=3)

    tri3_b = [_cumsum_matrix(t_n, rev) for rev in (False, True)]
    strict = [(tc_i > tr_i) if rev else (tc_i < tr_i) for rev in (False, True)]
    incl = [(tc_i >= tr_i) if rev else (tc_i <= tr_i) for rev in (False, True)]

    def stage_prep(c):
        rows = rows_of(c)
        k, r, v = k_ref[rows, :], r_ref[rows, :], v_ref[rows, :]
        kkr = k * k_k
        both = dict(kkr=kkr, ss=head_sum(kkr * kkr, gsum_b))
        items = []
        for d in (0, 1):
            lw = -RW_DECAY_SCALE * _sigmoid(w0_ref[d:d + 1, :] + lw_ref[d, rows, :])
            a = _sigmoid(a0_ref[d:d + 1, :] + la_ref[d, rows, :])
            items.append(dict(d=d, c=c, k=k, r=r, v=v, lw=lw, a=a, both=both, cum=_cumsum(tri3_b[d], lw)))
        return items

    def stage_amat(s):
        d, cum, lw, a, both = s["d"], s["cum"], s["lw"], s["a"], s["both"]
        if "kk" not in both:
            both["kk"] = both["kkr"] * lax.rsqrt(both["ss"] + 1e-12)
            both["vp"] = _parts(stack(s["v"]), 1)
        kk = both["kk"]
        kd = s["k"] * (1 + (a - 1) * k_a)
        bv = kk * a
        p_end = cum[0:1, :] if d == 1 else cum[t_n - 1:t_n, :]
        e_m = jnp.exp(-cum)
        e_h = jnp.exp(p_end - cum)
        ktp = _parts(stack(kk * jnp.exp(cum - lw)), 2)
        rt = stack(s["r"] * jnp.exp(cum))
        k2p = _cat_parts([_parts(stack(bv * e_m), 2), _parts(stack(kd * e_m), 2)], 0)
        return dict(d=d, c=s["c"], ktp=ktp, rt=rt, vp=both["vp"], p_end=p_end,
                    bh=stack(bv * e_h), kh=stack(kd * e_h),
                    amat=_mdot(_cat_parts([ktp, _parts(rt, 2)], 0), k2p, NT))

    def stage_square(s):
        d, amat = s["d"], s["amat"]
        lt = jnp.where(strict[d], amat[:h2, :h2], 0.0).T
        ltp = _parts(lt, 2)
        msk = jnp.concatenate([jnp.where(strict[d], amat[:h2, h2:], 0.0),
                               jnp.where(incl[d], amat[h2:, h2:], 0.0)], axis=0)
        s = dict(s, pt=eye - lt, xt=_mdot(ltp, ltp),
                 av=_mdot(_parts(msk, 2), s["vp"][:1]),
                 arbp=_parts(jnp.where(incl[d], amat[h2:, :h2], 0.0), 2))
        del s["amat"]
        return s

    def stage_double(s, final):
        xh = _parts(s["xt"], 1)
        ptp = _parts(s["pt"], 2)
        if final:
            return dict(s, pt=s["pt"] + _mdot(xh, ptp))
        rhs = [jnp.concatenate([ptp[0], xh[0]], axis=1), jnp.concatenate([ptp[1], jnp.zeros_like(xh[0])], axis=1)]
        both = _mdot(xh, rhs)
        return dict(s, pt=s["pt"] + both[:, :LANES], xt=both[:, LANES:])

    def stage_solve(s):
        rhs = jnp.concatenate([s["ktp"][0], (-s["av"][:h2]).astype(BF16)], axis=1)
        return dict(s, wub=_mdot(_parts(s["pt"].T, 2), [rhs]).astype(BF16))

    def stage_fold(s):
        d, c, wub = s["d"], s["c"], s["wub"]
        aw = _mdot(s["arbp"], [wub])
        zb = jnp.zeros((h2, LANES), BF16)
        lhs = _cat_parts([_parts(s["bh"].T, 2), _parts(s["kh"].T, 2)], 1)
        rhs = jnp.concatenate([wub, jnp.concatenate([zb, s["vp"][0]], axis=1)], axis=0)
        mn = _mdot(lhs, [rhs])
        dg = jnp.where(ri == ci, jnp.broadcast_to(jnp.exp(s["p_end"]), (LANES, LANES)), 0.0)
        rp = s["rt"] - aw[:, :LANES]
        mrp = _parts(jnp.concatenate([dg - mn[:, :LANES], rp[:t_n] + rp[t_n:]], axis=0), 2)
        for i in range(2):
            mr_s[d, c, i] = mrp[i]
        n_s[d, c] = mn[:, LANES:]
        y0 = s["av"][h2:] + aw[:, LANES:]
        return y0[:t_n] + y0[t_n:]

    def local(i, carry):
        sts = [s for g in range(group) for s in stage_prep(i * group + g)]
        sts = [stage_amat(s) for s in sts]
        sts = [stage_square(s) for s in sts]
        for step in range(5):
            sts = [stage_double(s, step == 4) for s in sts]
        sts = [stage_solve(s) for s in sts]
        y0 = [stage_fold(s) for s in sts]
        for g in range(group):
            y_scr[rows_of(i * group + g), :] = y0[2 * g] + y0[2 * g + 1]
        return carry

    lax.fori_loop(0, nc // group, local, 0)

    def seq(q, hs):
        cs = (q, _chunk_of(q, ncc, nc, True))
        mh = [_mdot([mr_s[d, cs[d], 0], mr_s[d, cs[d], 1]], _parts(hs[d], 2)) for d in (0, 1)]
        for d in (0, 1):
            y_scr[rows_of(cs[d]), :] += mh[d][h2:]
        return tuple(mh[d][:h2] + n_s[d, cs[d]] for d in (0, 1))

    zero = jnp.zeros((LANES, LANES), F32)
    lax.fori_loop(0, nc, seq, (zero, zero))

    n_post = _pick(nc, (4, 3, 2, 1))

    def post(i, carry):
        rows = [rows_of(i * n_post + g) for g in range(n_post)]

        def bonus_sum(rw):
            k, r = k_ref[rw, :], r_ref[rw, :]
            kd_f = k * (1 + (_sigmoid(a0_ref[0:1, :] + la_ref[0, rw, :]) - 1) * k_a)
            kd_b = k * (1 + (_sigmoid(a0_ref[1:2, :] + la_ref[1, rw, :]) - 1) * k_a)
            return head_sum(r * (kd_f + kd_b) * rk_ref[...], gsum_b)
        bsum = [bonus_sum(rw) for rw in rows]
        ys = [y_scr[rw, :] for rw in rows]
        ycs = [y - m for y, m in zip(ys, [head_sum(y, gavg_b) for y in ys])]
        var = [head_sum(yc * yc, gavg_b) for yc in ycs]
        for rw, yc, vr, bs in zip(rows, ycs, var, bsum):
            yn = yc * lax.rsqrt(vr + RW_GN_EPS) * gng_ref[...] + gnb_ref[...]
            o_ref[rw, :] = ((yn + bs * v_ref[rw, :]) * g_ref[rw, :]).astype(BF16)
        return carry

    lax.fori_loop(0, nc // n_post, post, 0)


def _rwkv_scan(rkv, lw, la, g, w0, a0, k_k, k_a, r_k, gn_g, gn_b, *, n_ctx):
    _, b_n, l_n, d = rkv.shape
    nc = l_n // CHUNK
    col = lambda n: pl.BlockSpec((None, None, l_n, LANES), lambda b, p, n=n: (n, b, 0, p))
    two = pl.BlockSpec((2, None, l_n, LANES), lambda b, p: (0, b, 0, p))
    par = lambda rows: pl.BlockSpec((rows, LANES), lambda b, p: (0, p))
    return pl.pallas_call(
        functools.partial(_rwkv_kernel, n_ctx=n_ctx),
        grid=(b_n, d // LANES),
        in_specs=[col(0), col(1), col(2), two, two,
                  pl.BlockSpec((None, l_n, LANES), lambda b, p: (b, 0, p)),
                  par(2), par(2), par(1), par(1), par(1), par(1), par(1)],
        out_specs=pl.BlockSpec((None, l_n, LANES), lambda b, p: (b, 0, p)),
        out_shape=jax.ShapeDtypeStruct((b_n, l_n, d), BF16),
        scratch_shapes=[pltpu.VMEM((l_n, LANES), F32),
                        pltpu.VMEM((2, nc, 2, LANES + CHUNK, LANES), BF16),
                        pltpu.VMEM((2, nc, LANES, LANES), F32)],
        compiler_params=_cparams(("parallel", "parallel"), 56),
        name="rwkv_scan",
    )(rkv, rkv, rkv, lw, la, g, w0, a0, k_k.reshape(1, d), k_a.reshape(1, d), r_k.reshape(1, d),
      gn_g.reshape(1, d), gn_b.reshape(1, d))


def _pad_axis(w, axis, to):
    pad = [(0, 0)] * w.ndim
    pad[axis] = (0, to - w.shape[axis])
    return jnp.pad(w, pad)


def _rwkv7_layer(z, mod, n_ctx, mu, w_rkv, w0, w1, w2, a0, a1, a2, g1, g2, k_k, k_a, r_k, gn_g, gn_b):
    b_n, l_n, d = z.shape
    m_n = b_n * l_n
    xs = _rw_mix(z, mod, mu, n_ctx=n_ctx).reshape(6, m_n, d)
    rkv = _matmul(xs, w_rkv)
    r_w = -(-w1.shape[-1] // LANES) * LANES
    r_a = -(-a1.shape[-1] // LANES) * LANES
    lw = _lora(xs, 3, _pad_axis(w1, 2, r_w).astype(BF16), _pad_axis(w2, 1, r_w).astype(BF16), "tanh")
    la = _lora(xs, 4, _pad_axis(a1, 2, r_a).astype(BF16), _pad_axis(a2, 1, r_a).astype(BF16), None)
    gate = _lora(xs, 5, g1[None].astype(BF16), g2[None].astype(BF16), "sigmoid")
    o = _rwkv_scan(rkv.reshape(3, b_n, l_n, d), lw.reshape(2, b_n, l_n, d), la.reshape(2, b_n, l_n, d),
                   gate.reshape(b_n, l_n, d), w0, a0, k_k, k_a, r_k, gn_g, gn_b, n_ctx=n_ctx)
    return o


def _qkv_rope_kernel(a_ref, w_ref, cos_ref, sa_ref, sb_ref, o_ref, wb_ref, *, n_q, n_qk):
    j, i = pl.program_id(0), pl.program_id(1)

    @pl.when(i == 0)
    def _():
        wb_ref[...] = w_ref[...].astype(BF16)

    x = jnp.dot(a_ref[...], wb_ref[...], preferred_element_type=F32)

    @pl.when(j < n_qk)
    def _():
        q = DA_HEAD // 4
        q_scale = jnp.where(j < n_q, DA_HEAD ** -0.5 * math.log2(math.e), 1.0)
        cos, s_a, s_b = cos_ref[...] * q_scale, sa_ref[...] * q_scale, sb_ref[...] * q_scale
        for s in range(x.shape[1] // DA_HEAD):
            cols = slice(s * DA_HEAD, (s + 1) * DA_HEAD)
            xs = x[:, cols]
            rot = xs * cos + pltpu.roll(xs, DA_HEAD - q, 1) * s_a + pltpu.roll(xs, q, 1) * s_b
            o_ref[:, cols] = rot.astype(BF16)

    @pl.when(j >= n_qk)
    def _():
        o_ref[...] = x.astype(BF16)


def _qkv_rope(h, w_qkv, cos, s_a, s_b):
    b_n, l_n, d = h.shape
    d3 = w_qkv.shape[1]
    tm = _pick(l_n, (768, 1024, 512, 256, 128, 64))
    tn = _pick(d, (1024, 512, 256, 128))
    n_t = l_n // tm
    tab = pl.BlockSpec((tm, DA_HEAD), lambda j, i: (i % n_t, 0))
    out = pl.pallas_call(
        functools.partial(_qkv_rope_kernel, n_q=d // tn, n_qk=2 * d // tn),
        grid=(d3 // tn, b_n * n_t),
        in_specs=[pl.BlockSpec((tm, d), lambda j, i: (i, 0)),
                  pl.BlockSpec((d, tn), lambda j, i: (0, j)), tab, tab, tab],
        out_specs=pl.BlockSpec((tm, tn), lambda j, i: (i, j)),
        out_shape=jax.ShapeDtypeStruct((b_n * l_n, d3), BF16),
        scratch_shapes=[pltpu.VMEM((d, tn), BF16)],
        compiler_params=_cparams(("parallel", "arbitrary"), 48),
        name="qkv_rope",
    )(h.reshape(b_n * l_n, d), w_qkv, cos, s_a, s_b)
    return out.reshape(b_n, l_n, d3)


def _attn_kernel(q_ref, k_ref, v_ref, lam_ref, sg_ref, o_ref, *, ncb, n_ctx, lam_init):
    qi = pl.program_id(2)
    lv = lam_ref[...]
    lam = (jnp.exp(jnp.sum(lv[0:1] * lv[1:2], axis=-1, keepdims=True))
           - jnp.exp(jnp.sum(lv[2:3] * lv[3:4], axis=-1, keepdims=True)) + lam_init)

    def attend(nk):
        def probs(m):
            q = q_ref[:, m * DA_HEAD:(m + 1) * DA_HEAD]
            k = k_ref[0:nk, m * DA_HEAD:(m + 1) * DA_HEAD]
            s = lax.dot_general(q, k, NT, preferred_element_type=F32)
            e = jnp.exp2(s - jnp.max(s, axis=-1, keepdims=True))
            return e, 1.0 / jnp.sum(e, axis=-1, keepdims=True)
        e0, i0 = probs(0)
        e1, i1 = probs(1)
        v = v_ref[0:nk, :]
        o = (jnp.dot(e0.astype(BF16), v, preferred_element_type=F32) * i0
             - jnp.dot(e1.astype(BF16), v, preferred_element_type=F32) * (lam * i1))
        o = o * lax.rsqrt(jnp.mean(o * o, axis=-1, keepdims=True) + 1e-5) * sg_ref[...] * (1 - lam_init)
        o_ref[...] = o.astype(BF16)

    if ncb > 0:
        @pl.when(qi < ncb)
        def _():
            attend(n_ctx)

    @pl.when(qi >= ncb)
    def _():
        attend(k_ref.shape[0])


def _attention(qkv, lam_vec, sub_g, *, tq, n_ctx, lam_init):
    b_n, l_n, d3 = qkv.shape
    d = d3 // 3
    hw = 2 * DA_HEAD
    nh = d // hw
    return pl.pallas_call(
        functools.partial(_attn_kernel, ncb=n_ctx // tq, n_ctx=n_ctx, lam_init=lam_init),
        grid=(b_n, nh, l_n // tq),
        in_specs=[pl.BlockSpec((None, tq, hw), lambda b, h, t: (b, t, h)),
                  pl.BlockSpec((None, l_n, hw), lambda b, h, t: (b, 0, nh + h)),
                  pl.BlockSpec((None, l_n, hw), lambda b, h, t: (b, 0, 2 * nh + h)),
                  pl.BlockSpec((4, DA_HEAD), lambda b, h, t: (0, 0)),
                  pl.BlockSpec((1, hw), lambda b, h, t: (0, 0))],
        out_specs=pl.BlockSpec((None, tq, hw), lambda b, h, t: (b, t, h)),
        out_shape=jax.ShapeDtypeStruct((b_n, l_n, d), BF16),
        compiler_params=_cparams(("parallel", "parallel", "arbitrary"), 48),
        name="diff_attn",
    )(qkv, qkv, qkv, lam_vec, sub_g.reshape(1, hw))


def _rope_tables(n_ctx, n_lat):
    n_rows = n_lat // GRID_W
    row = jnp.repeat(jnp.arange(n_rows, dtype=F32), GRID_W)
    col = jnp.tile(jnp.arange(GRID_W, dtype=F32), n_rows)
    nf = DA_HEAD // 4
    inv_freq = ROPE_BASE ** (-jnp.arange(nf, dtype=F32) / nf)
    ang_r, ang_c = row[:, None] * inv_freq, col[:, None] * inv_freq
    ang = jnp.concatenate([ang_r, ang_r, ang_c, ang_c], axis=-1)
    ang = jnp.concatenate([jnp.zeros((n_ctx, DA_HEAD), F32), ang], axis=0)
    cos, sin = jnp.cos(ang), jnp.sin(ang)
    even_q = (jnp.arange(DA_HEAD) // nf) % 2 == 0
    return cos, jnp.where(even_q, -sin, 0.0), jnp.where(even_q, 0.0, sin)


def _diff_attention_layer(h, n_ctx, layer_idx, w_qkv, lam_vec, sub_g, *, tr):
    b_n, l_n, d = h.shape
    cos, s_a, s_b = _rope_tables(n_ctx, l_n - n_ctx)
    qkv = _qkv_rope(h, w_qkv, cos, s_a, s_b)
    lam_init = 0.8 - 0.6 * math.exp(-0.3 * layer_idx)
    o = _attention(qkv, lam_vec, sub_g, tq=tr, n_ctx=n_ctx, lam_init=lam_init)
    return o


def _hgrn_kernel(q_ref, i_ref, g_ref, ff_ref, fb_ref, low_ref, ng_ref, o_ref, o_scr, *, n_ctx, layer_idx):
    t_n = CHUNK
    l_n = q_ref.shape[0]
    nc, ncc = l_n // t_n, n_ctx // t_n
    r64 = lax.broadcasted_iota(jnp.int32, (t_n, t_n), 0)
    c64 = lax.broadcasted_iota(jnp.int32, (t_n, t_n), 1)

    def rows_of(c):
        return pl.ds(pl.multiple_of(c * t_n, t_n), t_n)

    f_refs = (ff_ref, fb_ref)
    lbs, incl = [], []
    tri3_b = [_cumsum_matrix(t_n, rev) for rev in (False, True)]
    for d in (0, 1):
        low = low_ref[d]
        e = jnp.exp(low - jnp.max(low, axis=0, keepdims=True))
        sm = e / jnp.sum(e, axis=0, keepdims=True)
        cs = sm[0:1]
        for rr in range(1, layer_idx + 1):
            cs = cs + sm[rr:rr + 1]
        lbs.append(cs - sm[0:1])
        incl.append((c64 >= r64) if d == 1 else (c64 <= r64))
    group = _pick(nc, (4, 2, 1))

    def stage_cum(d, c):
        rows = rows_of(c)
        f = lbs[d] + (1.0 - lbs[d]) * _sigmoid(f_refs[d][rows, :])
        return dict(d=d, rows=rows, f=f, cum=_cumsum(tri3_b[d], jnp.log(f)))

    def stage_att(s):
        d, cum, rows = s["d"], s["cum"], s["rows"]
        b_end = cum[0:1, :] if d == 1 else cum[t_n - 1:t_n, :]
        qv = q_ref[rows, :]
        qd = (qv * _sigmoid(qv) * jnp.exp(cum)).astype(BF16)
        kk = 1.0 - s["f"]
        v = i_ref[rows, :]
        kd = (kk * jnp.exp(-cum)).astype(BF16)
        ke = (kk * jnp.exp(b_end - cum)).astype(BF16)
        return dict(d=d, rows=rows, qd=qd, vb=v.astype(BF16), dec=jnp.exp(b_end),
                    att=lax.dot_general(qd, kd, NT, preferred_element_type=F32),
                    upd=jnp.dot(v.T.astype(BF16), ke, preferred_element_type=F32))

    def stage_intra(s):
        att = jnp.where(incl[s["d"]], s["att"], 0.0).astype(BF16)
        return dict(s, o=jnp.dot(att, s["vb"], preferred_element_type=F32))

    def body(i, states):
        items = [(d, _chunk_of(i * group + g, ncc, nc, d == 1)) for g in range(group) for d in (0, 1)]
        sts = [stage_cum(d, c) for d, c in items]
        sts = [stage_att(s) for s in sts]
        sts = [stage_intra(s) for s in sts]
        states = list(states)
        for s in sts:
            d = s["d"]
            o = s["o"] + lax.dot_general(s["qd"], states[d].astype(BF16), NT, preferred_element_type=F32)
            o_scr[d, s["rows"], :] = o
            states[d] = states[d] * s["dec"] + s["upd"]
        return tuple(states)

    zero = jnp.zeros((LANES, LANES), F32)
    lax.fori_loop(0, nc // group, body, (zero, zero))

    def post(c, carry):
        rows = rows_of(c)
        o = o_scr[0, rows, :] + o_scr[1, rows, :]
        o = o * lax.rsqrt(jnp.mean(o * o, axis=-1, keepdims=True) + 1e-5) * ng_ref[...]
        gv = g_ref[rows, :]
        o_ref[rows, :] = (o * (gv * _sigmoid(gv))).astype(BF16)
        return carry

    lax.fori_loop(0, nc, post, 0)


def _hgrn2_layer(h, n_ctx, layer_idx, w_in, lower, norm_g):
    b_n, l_n, d = h.shape
    m_n = b_n * l_n
    nh = d // HG_EXPAND
    proj = _mm2(h.reshape(m_n, d), w_in).reshape(b_n, l_n, 5 * d)
    col = lambda n: pl.BlockSpec((None, l_n, LANES), lambda b, p, n=n: (b, 0, n * nh + p))
    o = pl.pallas_call(
        functools.partial(_hgrn_kernel, n_ctx=n_ctx, layer_idx=layer_idx),
        grid=(b_n, nh),
        in_specs=[col(0), col(1), col(2), col(3), col(4),
                  pl.BlockSpec((2, lower.shape[1], LANES), lambda b, p: (0, 0, p)),
                  pl.BlockSpec((1, LANES), lambda b, p: (0, 0))],
        out_specs=pl.BlockSpec((None, l_n, LANES), lambda b, p: (b, 0, p)),
        out_shape=jax.ShapeDtypeStruct((b_n, l_n, d), BF16),
        scratch_shapes=[pltpu.VMEM((2, l_n, LANES), F32)],
        compiler_params=_cparams(("parallel", "parallel"), 40),
        name="hgrn_scan",
    )(proj, proj, proj, proj, proj, lower, norm_g.reshape(1, LANES))
    return o


def _gelu_tanh(x):
    return 0.5 * x * (1.0 + jnp.tanh(math.sqrt(2.0 / math.pi) * (x + 0.044715 * (x * x * x))))


def _softplus(x):
    return jnp.maximum(x, 0.0) + jnp.log1p(jnp.exp(-jnp.abs(x)))


SEG_PAD = 8


def _lin_scan(a_ref, u_ref, hl_s, cp_s, h_s, base, row0, n, rev, h_in, accumulate):
    seg = n // 8
    stride = seg + SEG_PAD
    n_p = a_ref.shape[0]

    def step(i, carry):
        t = (seg - 1 - i) if rev else i
        idx = pl.ds(base + t, 8, stride=stride)
        out = []
        for j in range(n_p):
            hl, cp = carry[j]
            a = a_ref[j, idx, :]
            hl = a * hl + u_ref[j, idx, :]
            cp = a * cp
            hl_s[j, idx, :] = hl
            cp_s[j, idx, :] = cp
            out.append((hl, cp))
        return tuple(out)

    init = tuple((jnp.zeros((8, LANES), F32), jnp.ones((8, LANES), F32)) for _ in range(n_p))
    ends = lax.fori_loop(0, seg, step, init)
    order = range(7, -1, -1) if rev else range(8)
    h_out = []
    for j in range(n_p):
        hl_e, cp_e = ends[j]
        carry = h_in[j]
        for s in order:
            r0, p0 = row0 + s * seg, base + s * stride
            blk = hl_s[j, p0:p0 + seg, :] + cp_s[j, p0:p0 + seg, :] * carry
            if accumulate:
                h_s[j, r0:r0 + seg, :] += blk
            else:
                h_s[j, r0:r0 + seg, :] = blk
            carry = hl_e[s:s + 1, :] + cp_e[s:s + 1, :] * carry
        h_out.append(carry)
    return h_out


def _rglru_kernel(gb_ref, xb_ref, cw_ref, cb_ref, wg_ref, bg_ref, lam_ref, o_ref, a_s, u_s, h_s, hl_s, cp_s, *,
                  n_ctx):
    l_n = xb_ref.shape[0]
    n_lat = l_n - n_ctx
    x = xb_ref[...]
    row = lax.broadcasted_iota(jnp.int32, x.shape, 0)
    k_w = cw_ref.shape[0]
    xc = cb_ref[...] + sum(_seg_shift(x, row, j - (k_w - 1) // 2, n_ctx) * cw_ref[j:j + 1, :]
                           for j in range(k_w))
    xcb = xc.astype(BF16)
    n_p = x.shape[1] // LANES
    for d in (0, 1):
        gate = lambda g: _sigmoid(jnp.dot(xcb, wg_ref[d, g].astype(BF16), preferred_element_type=F32)
                                  + bg_ref[d, g:g + 1, :])
        log_a = -LR_C * gate(0) * _softplus(-lam_ref[d:d + 1, :])
        a = jnp.exp(log_a)
        u = jnp.sqrt(jnp.tanh(-log_a) * (jnp.exp(2.0 * log_a) + 1.0)) * gate(1) * xc
        h = [jnp.zeros((1, LANES), F32)] * n_p
        base = 0
        for row0, n in ((0, n_ctx), (n_ctx, n_lat)):
            if n:
                seg = n // 8
                for j in range(n_p):
                    for s in range(8):
                        src = slice(row0 + s * seg, row0 + (s + 1) * seg)
                        dst = slice(base + s * (seg + SEG_PAD), base + s * (seg + SEG_PAD) + seg)
                        a_s[j, dst, :] = a[src, j * LANES:(j + 1) * LANES]
                        u_s[j, dst, :] = u[src, j * LANES:(j + 1) * LANES]
                h = _lin_scan(a_s, u_s, hl_s, cp_s, h_s, base, row0, n, d == 1, h, d == 1)
                base += 8 * (seg + SEG_PAD)
    for j in range(n_p):
        cols = slice(j * LANES, (j + 1) * LANES)
        o_ref[:, cols] = (h_s[j, n_ctx:l_n, :] * _gelu_tanh(gb_ref[n_ctx:, cols])).astype(BF16)


def _rglru_layer(h, n_ctx, w_in, conv_w, conv_b, w_gate, b_gate, lam):
    b_n, l_n, d = h.shape
    n_lat = l_n - n_ctx
    nb = d // LR_BS
    proj = _mm2(h.reshape(b_n * l_n, d), w_in).reshape(b_n, l_n, 2 * d)
    k_w = conv_w.shape[0]
    o = pl.pallas_call(
        functools.partial(_rglru_kernel, n_ctx=n_ctx),
        grid=(b_n, nb),
        in_specs=[pl.BlockSpec((None, l_n, LR_BS), lambda b, j: (b, 0, j)),
                  pl.BlockSpec((None, l_n, LR_BS), lambda b, j: (b, 0, nb + j)),
                  pl.BlockSpec((k_w, LR_BS), lambda b, j: (0, j)),
                  pl.BlockSpec((1, LR_BS), lambda b, j: (0, j)),
                  pl.BlockSpec((2, 2, None, LR_BS, LR_BS), lambda b, j: (0, 0, j, 0, 0)),
                  pl.BlockSpec((2, 2, LR_BS), lambda b, j: (0, 0, j)),
                  pl.BlockSpec((2, LR_BS), lambda b, j: (0, j))],
        out_specs=pl.BlockSpec((None, n_lat, LR_BS), lambda b, j: (b, 0, j)),
        out_shape=jax.ShapeDtypeStruct((b_n, n_lat, d), BF16),
        scratch_shapes=[pltpu.VMEM((LR_BS // LANES, l_n + 16 * SEG_PAD, LANES), F32)] * 5,
        compiler_params=_cparams(("parallel", "parallel"), 56),
        name="rglru",
    )(proj, proj, conv_w, conv_b.reshape(1, d), w_gate, b_gate, lam)
    return o


def _ffn_up_kernel(h_ref, wg_ref, wv_ref, cg_ref, cv_ref, bg_ref, bv_ref, o_ref, u_scr, *, n_ctx, tr):
    l_n, tf = o_ref.shape
    k_w = cg_ref.shape[0]
    half = (k_w - 1) // 2
    n_buf, pad = u_scr.shape[0], (u_scr.shape[1] - l_n) // 2
    rblk = tr * _pick(l_n // tr, (3, 4, 2, 1))
    row = lax.broadcasted_iota(jnp.int32, (tr, LANES), 0)
    for p in range(n_buf):
        u_scr[p, 0:pad, :] = jnp.zeros((pad, 2 * LANES), F32)
        u_scr[p, pad + l_n:, :] = jnp.zeros((pad, 2 * LANES), F32)

    def weights(s):
        cols = slice(s * LANES, (s + 1) * LANES)
        return jnp.concatenate([wg_ref[:, cols], wv_ref[:, cols]], axis=1).astype(BF16)

    def product(s, w, r0):
        u_scr[s % n_buf, pad + r0:pad + r0 + rblk, :] = jnp.dot(h_ref[r0:r0 + rblk, :], w,
                                                                preferred_element_type=F32)

    def finish(s, r0):
        p, cols = s % n_buf, slice(s * LANES, (s + 1) * LANES)

        def conv(lane0, w_ref, b_ref):
            acc = None
            for j in range(k_w):
                sh = j - half
                x = u_scr[p, pad + r0 + sh:pad + r0 + sh + tr, lane0:lane0 + LANES]
                if sh < 0 and r0 in (0, n_ctx):
                    x = jnp.where(row < -sh, 0.0, x)
                if sh > 0 and r0 + tr in (n_ctx, l_n):
                    x = jnp.where(row >= tr - sh, 0.0, x)
                t = x * w_ref[j:j + 1, cols]
                acc = t if acc is None else acc + t
            return b_ref[:, cols] + acc
        gate = conv(0, cg_ref, bg_ref)
        val = conv(LANES, cv_ref, bv_ref)
        o_ref[r0:r0 + tr, cols] = (gate * _sigmoid(gate) * val).astype(BF16)

    n_s = tf // LANES
    for s in range(n_s + 1):
        w = weights(s) if s < n_s else None
        for r0 in range(0, l_n, rblk):
            if s < n_s:
                product(s, w, r0)
            if s > 0:
                for r1 in range(r0, r0 + rblk, tr):
                    finish(s - 1, r1)


def _conv_ffn(h, n_ctx, w_up, conv_w, conv_b, *, tr):
    b_n, l_n, d = h.shape
    f = w_up.shape[1] // 2
    tf = _pick(f, (512, 256, 128))
    nf = f // tf
    k_w = conv_w.shape[0]
    cb = conv_b.reshape(1, 2 * f)
    act = pl.pallas_call(
        functools.partial(_ffn_up_kernel, n_ctx=n_ctx, tr=tr),
        grid=(b_n, nf),
        in_specs=[pl.BlockSpec((None, l_n, d), lambda b, j: (b, 0, 0), pipeline_mode=pl.Buffered(1)),
                  pl.BlockSpec((d, tf), lambda b, j: (0, j)),
                  pl.BlockSpec((d, tf), lambda b, j: (0, nf + j)),
                  pl.BlockSpec((k_w, tf), lambda b, j: (0, j)),
                  pl.BlockSpec((k_w, tf), lambda b, j: (0, nf + j)),
                  pl.BlockSpec((1, tf), lambda b, j: (0, j)),
                  pl.BlockSpec((1, tf), lambda b, j: (0, nf + j))],
        out_specs=pl.BlockSpec((None, l_n, tf), lambda b, j: (b, 0, j)),
        out_shape=jax.ShapeDtypeStruct((b_n, l_n, f), BF16),
        scratch_shapes=[pltpu.VMEM((3, l_n + 16, 2 * LANES), F32)],
        compiler_params=_cparams(("parallel", "arbitrary"), 56),
        name="ffn_up",
    )(h, w_up, w_up, conv_w, conv_w, cb, cb)
    return act


def kernel(x, c, ctx, c_ctx, ada_w, ada_b, ln_g, ln_b, ffn_w_up, ffn_conv_w, ffn_conv_b, ffn_w_down, rw_mu, rw_w_rkv, rw_w0, rw_w1, rw_w2, rw_a0, rw_a1, rw_a2, rw_g1, rw_g2, rw_k_k, rw_k_a, rw_r_k, rw_gn_g, rw_gn_b, rw_w_o, da_w_qkv, da_lambda, da_sub_g, da_w_o, hg_w_in, hg_lower, hg_norm_g, hg_w_o, lr_w_in, lr_conv_w, lr_conv_b, lr_w_gate, lr_b_gate, lr_lambda, lr_w_o):
    b_n, n_lat, d = x.shape
    n_ctx = ctx.shape[1]
    depth = ada_w.shape[0]
    assert depth == 4 and rw_mu.shape[0] == 1, "one occurrence of each of the four mixers"
    assert b_n + 1 <= 8 and n_ctx % CHUNK == 0 and n_lat % CHUNK == 0
    tr = math.gcd(math.gcd(n_ctx, n_lat), 256)
    alpha = (2 * depth) ** 0.25

    c8 = jnp.concatenate([c, c_ctx[None], jnp.zeros((8 - b_n - 1, d), F32)], axis=0)
    m = _ada(c8, ada_w, ada_b)
    m_lat = m[:, :b_n].reshape(depth, b_n, 1, 6, d)
    m_ctx = jnp.broadcast_to(m[:, b_n].reshape(depth, 1, 1, 6, d), (depth, b_n, 1, 6, d))
    mod = jnp.concatenate([m_ctx, m_lat], axis=2)

    z = jnp.concatenate([ctx, x], axis=1)
    w_down_b = _to_bf16(ffn_w_down)
    h = None
    for i in range(depth):
        last = i == depth - 1
        if i == 0:
            o = _rwkv7_layer(z, mod[0], n_ctx, rw_mu[0], rw_w_rkv[0], rw_w0[0], rw_w1[0], rw_w2[0], rw_a0[0],
                             rw_a1[0], rw_a2[0], rw_g1[0], rw_g2[0], rw_k_k[0], rw_k_a[0], rw_r_k[0],
                             rw_gn_g[0], rw_gn_b[0])
            w_o = rw_w_o[0]
        elif i == 1:
            o = _diff_attention_layer(h, n_ctx, i, da_w_qkv[0], da_lambda[0], da_sub_g[0], tr=tr)
            w_o = da_w_o[0]
        elif i == 2:
            o = _hgrn2_layer(h, n_ctx, i, hg_w_in[0], hg_lower, hg_norm_g[0])
            w_o = hg_w_o[0]
        else:
            o = _rglru_layer(h, n_ctx, lr_w_in[0], lr_conv_w[0], lr_conv_b[0], lr_w_gate[0], lr_b_gate[0],
                             lr_lambda[0])
            w_o = lr_w_o[0]
        z, h = _proj_ln(o, w_o, z, mod[i], ln_g[i, 0], ln_b[i, 0], mod[i], gate_j=2, mod_j=3, tr=tr,
                        n_ctx=n_ctx, alpha=alpha)
        if last:
            n_ctx = 0
        act = _conv_ffn(h, n_ctx, ffn_w_up[i], ffn_conv_w[i], ffn_conv_b[i], tr=tr)
        z, h = _proj_ln(act, w_down_b[i], z, mod[i], ln_g[i, 1], ln_b[i, 1], mod[min(i + 1, depth - 1)],
                        gate_j=5, mod_j=None if last else 0, tr=tr, n_ctx=n_ctx, alpha=alpha)
    return z
```

```python
import functools
import math

import jax
import jax.numpy as jnp
from jax import lax
from jax.experimental import pallas as pl
from jax.experimental.pallas import tpu as pltpu

F32, BF16 = jnp.float32, jnp.bfloat16

LANES = 128
CHUNK = 64
LN_EPS = 1e-5
GRID_W = 64
ROPE_BASE = 10000.0
RW_HEAD = 64
RW_DECAY_SCALE = 0.606531
RW_GN_EPS = 64e-5
DA_HEAD = 128
HG_EXPAND = 128
LR_BS = 256
LR_C = 8.0
MIB = 1024 * 1024


def _pick(n, cands):
    for c in cands:
        if n % c == 0:
            return c
    return n


def _cparams(sem, vmem_mib):
    return pltpu.CompilerParams(dimension_semantics=sem, vmem_limit_bytes=vmem_mib * MIB)


def _sigmoid(x):
    return jax.nn.sigmoid(x)


NN = (((1,), (0,)), ((), ()))
NT = (((1,), (1,)), ((), ()))


def _parts(x, n):
    out = []
    for i in range(n):
        p = x.astype(BF16)
        out.append(p)
        if i + 1 < n:
            x = x - p.astype(F32)
    return out


def _mdot(ap, bp, dims=NN, order=2):
    pairs = [(a, b) for i, a in enumerate(ap) for j, b in enumerate(bp) if i + j < order]
    (ca,), (cb,) = dims[0]
    lhs = jnp.concatenate([a for a, _ in pairs], axis=ca) if len(pairs) > 1 else pairs[0][0]
    rhs = jnp.concatenate([b for _, b in pairs], axis=cb) if len(pairs) > 1 else pairs[0][1]
    return lax.dot_general(lhs, rhs, dims, preferred_element_type=F32)


def _cumsum_matrix(t_n, rev):
    r = lax.broadcasted_iota(jnp.int32, (t_n, 3 * t_n), 0)
    c = lax.broadcasted_iota(jnp.int32, (t_n, 3 * t_n), 1) % t_n
    return jnp.where((c >= r) if rev else (c <= r), 1.0, 0.0).astype(BF16)


def _cumsum(tri3_b, x):
    return jnp.dot(tri3_b, jnp.concatenate(_parts(x, 3), axis=0), preferred_element_type=F32)


def _cat_parts(xs, axis):
    return [jnp.concatenate(ps, axis=axis) for ps in zip(*xs)]


def _mm_kernel(a_ref, w_ref, o_ref, acc_ref, *, nk):
    if w_ref.dtype == BF16:
        prod = jnp.dot(a_ref[...], w_ref[...], preferred_element_type=F32)
    else:
        tk = w_ref.shape[0]
        n_kc = 2 if tk % (2 * LANES) == 0 else 1
        prod = None
        for c in range(n_kc):
            ks = slice(c * (tk // n_kc), (c + 1) * (tk // n_kc))
            t = jnp.dot(a_ref[:, ks], w_ref[ks, :].astype(BF16), preferred_element_type=F32)
            prod = t if prod is None else prod + t
    if nk == 1:
        o_ref[...] = prod.astype(o_ref.dtype)
    else:
        k = pl.program_id(3)

        @pl.when(k == 0)
        def _():
            acc_ref[...] = prod

        @pl.when(k > 0)
        def _():
            acc_ref[...] += prod

        @pl.when(k == nk - 1)
        def _():
            o_ref[...] = acc_ref[...].astype(o_ref.dtype)


def _mm_wres_kernel(a_ref, w_ref, o_ref, wb_ref):
    i = pl.program_id(2)
    k_n = w_ref.shape[0]
    n_kc = 4 if k_n % (4 * LANES) == 0 else 1

    @pl.when(i == 0)
    def _():
        kc = k_n // n_kc
        acc = None
        for c in range(n_kc):
            ks = slice(c * kc, (c + 1) * kc)
            wb = w_ref[ks, :].astype(BF16)
            wb_ref[ks, :] = wb
            t = jnp.dot(a_ref[:, ks], wb, preferred_element_type=F32)
            acc = t if acc is None else acc + t
        o_ref[...] = acc.astype(o_ref.dtype)

    @pl.when(i > 0)
    def _():
        o_ref[...] = jnp.dot(a_ref[...], wb_ref[...], preferred_element_type=F32).astype(o_ref.dtype)


def _matmul(a, w, *, out_dtype=F32, a_off=0):
    g_n, k_n, n_n = w.shape
    m_n = a.shape[1]
    tm = _pick(m_n, (1024, 512, 256, 128, 64))
    tn = _pick(n_n, (1024, 512, 256, 128))
    tk = k_n if k_n <= 2048 else _pick(k_n, (2816, 2048, 1024, 512))
    nk = k_n // tk
    if w.dtype == F32 and nk > 1:
        tn = _pick(n_n, (512, 256, 128))
    if w.dtype == F32 and nk == 1:
        return pl.pallas_call(
            _mm_wres_kernel,
            grid=(g_n, n_n // tn, m_n // tm),
            in_specs=[pl.BlockSpec((None, tm, k_n), lambda g, j, i: (g + a_off, i, 0)),
                      pl.BlockSpec((None, k_n, tn), lambda g, j, i: (g, 0, j))],
            out_specs=pl.BlockSpec((None, tm, tn), lambda g, j, i: (g, i, j)),
            out_shape=jax.ShapeDtypeStruct((g_n, m_n, n_n), out_dtype),
            scratch_shapes=[pltpu.VMEM((k_n, tn), BF16)],
            compiler_params=_cparams(("parallel", "parallel", "arbitrary"), 48),
            name="matmul_wres",
        )(a, w)
    return pl.pallas_call(
        functools.partial(_mm_kernel, nk=nk),
        grid=(g_n, m_n // tm, n_n // tn, nk),
        in_specs=[pl.BlockSpec((None, tm, tk), lambda g, i, j, k: (g + a_off, i, k)),
                  pl.BlockSpec((None, tk, tn), lambda g, i, j, k: (g, k, j))],
        out_specs=pl.BlockSpec((None, tm, tn), lambda g, i, j, k: (g, i, j)),
        out_shape=jax.ShapeDtypeStruct((g_n, m_n, n_n), out_dtype),
        scratch_shapes=[pltpu.VMEM((tm, tn), F32)],
        compiler_params=_cparams(("parallel", "parallel", "parallel", "arbitrary"), 48),
        name="matmul",
    )(a, w)


def _mm2(a, w, **kw):
    return _matmul(a[None], w[None], **kw)[0]


def _ada_kernel(c_ref, w_ref, b_ref, o_ref):
    c = c_ref[...]
    s = (c * _sigmoid(c)).astype(BF16)
    o_ref[...] = jnp.dot(s, w_ref[...].astype(BF16), preferred_element_type=F32) + b_ref[...]


def _ada(c8, ada_w, ada_b):
    depth, d, n = ada_w.shape
    tn = _pick(n, (1024, 512, 256, 128))
    return pl.pallas_call(
        _ada_kernel,
        grid=(depth, n // tn),
        in_specs=[pl.BlockSpec((8, d), lambda l, j: (0, 0)),
                  pl.BlockSpec((None, d, tn), lambda l, j: (l, 0, j)),
                  pl.BlockSpec((None, 1, tn), lambda l, j: (l, 0, j))],
        out_specs=pl.BlockSpec((None, 8, tn), lambda l, j: (l, 0, j)),
        out_shape=jax.ShapeDtypeStruct((depth, 8, n), F32),
        compiler_params=_cparams(("parallel", "parallel"), 40),
        name="ada",
    )(c8, ada_w, ada_b.reshape(depth, 1, n))


def _ln_mod_kernel(z_ref, y_ref, mod_ref, g_ref, b_ref, mod2_ref, *out_refs, gate_j, mod_j, alpha):
    m = mod_ref[...]
    zz = alpha * z_ref[...] + y_ref[...] * m[gate_j:gate_j + 1]
    mu = jnp.mean(zz, axis=-1, keepdims=True)
    zc = zz - mu
    var = jnp.mean(zc * zc, axis=-1, keepdims=True)
    zn = zc * lax.rsqrt(var + LN_EPS) * g_ref[...] + b_ref[...]
    out_refs[0][...] = zn
    if mod_j is not None:
        m2 = mod2_ref[...]
        out_refs[1][...] = (zn * (1 + m2[mod_j + 1:mod_j + 2]) + m2[mod_j:mod_j + 1]).astype(BF16)


def _ln_mod(z, y, mod, ln_g, ln_b, mod2, *, gate_j, mod_j, tr, n_ctx, alpha):
    b_n, l_z, d = z.shape
    l_y = y.shape[1]
    z_off = (l_z - l_y) // tr
    ncb = (n_ctx - (l_z - l_y)) // tr
    seg = lambda b, t: (b, jnp.where(t < ncb, 0, 1), 0, 0)
    row = pl.BlockSpec((None, tr, d), lambda b, t: (b, t, 0))
    out_shape = [jax.ShapeDtypeStruct((b_n, l_y, d), F32)]
    out_specs = [row]
    if mod_j is not None:
        out_shape.append(jax.ShapeDtypeStruct((b_n, l_y, d), BF16))
        out_specs.append(row)
    res = pl.pallas_call(
        functools.partial(_ln_mod_kernel, gate_j=gate_j, mod_j=mod_j, alpha=alpha),
        grid=(b_n, l_y // tr),
        in_specs=[pl.BlockSpec((None, tr, d), lambda b, t: (b, t + z_off, 0)),
                  row,
                  pl.BlockSpec((None, None, 6, d), seg),
                  pl.BlockSpec((1, d), lambda b, t: (0, 0)),
                  pl.BlockSpec((1, d), lambda b, t: (0, 0)),
                  pl.BlockSpec((None, None, 6, d), seg)],
        out_specs=out_specs,
        out_shape=out_shape,
        compiler_params=_cparams(("parallel", "parallel"), 40),
        name="ln_mod",
    )(z, y, mod, ln_g.reshape(1, d), ln_b.reshape(1, d), mod2)
    return res if mod_j is not None else (res[0], None)


def _cast_kernel(x_ref, o_ref):
    o_ref[...] = x_ref[...].astype(o_ref.dtype)


def _to_bf16(w):
    g_n, k_n, n_n = w.shape
    tk = _pick(k_n, (512, 256, 128))
    spec = pl.BlockSpec((None, tk, n_n), lambda g, i: (g, i, 0))
    return pl.pallas_call(
        _cast_kernel, grid=(g_n, k_n // tk), in_specs=[spec], out_specs=spec,
        out_shape=jax.ShapeDtypeStruct(w.shape, BF16),
        compiler_params=_cparams(("parallel", "parallel"), 32),
        name="to_bf16",
    )(w)


def _proj_ln_kernel(a_ref, w_ref, z_ref, mod_ref, g_ref, b_ref, mod2_ref, zo_ref, *ho_refs,
                    nk, gate_j, mod_j, alpha, tr, ncb):
    t, k = pl.program_id(1), pl.program_id(2)
    tm = zo_ref.shape[0]
    rb = tm // 2 if tm % 16 == 0 else tm

    def accumulate(first):
        for r0 in range(0, tm, rb):
            p = jnp.dot(a_ref[r0:r0 + rb, :], w_ref[...], preferred_element_type=F32)
            if first:
                zo_ref[r0:r0 + rb, :] = p
            else:
                zo_ref[r0:r0 + rb, :] += p

    rs = math.gcd(tr, 64)

    def finish():
        for sb in range(tm // rs):
            rows = slice(sb * rs, (sb + 1) * rs)
            is_ctx = t * (tm // tr) + (sb * rs) // tr < ncb
            m = jnp.where(is_ctx, mod_ref[0], mod_ref[1])
            zz = alpha * z_ref[rows, :] + zo_ref[rows, :] * m[gate_j:gate_j + 1]
            mu = jnp.mean(zz, axis=-1, keepdims=True)
            zc = zz - mu
            var = jnp.mean(zc * zc, axis=-1, keepdims=True)
            zn = zc * lax.rsqrt(var + LN_EPS) * g_ref[...] + b_ref[...]
            zo_ref[rows, :] = zn
            if mod_j is not None:
                m2 = jnp.where(is_ctx, mod2_ref[0], mod2_ref[1])
                ho_refs[0][rows, :] = (zn * (1 + m2[mod_j + 1:mod_j + 2]) + m2[mod_j:mod_j + 1]).astype(BF16)

    if nk == 1:
        accumulate(True)
        finish()
    else:
        @pl.when(k == 0)
        def _():
            accumulate(True)

        @pl.when(k > 0)
        def _():
            accumulate(False)

        @pl.when(k == nk - 1)
        def _():
            finish()


def _proj_ln(a, w, g, z, mod, ln_g, ln_b, mod2, *, gate_j, mod_j, tr, n_ctx, alpha):
    b_n, l_a, k_n = a.shape
    d = w.shape[2]
    if z.shape[1] != l_a:
        y = _matmul(a.reshape(1, b_n * l_a, k_n), w[g:g + 1])[0].reshape(b_n, l_a, d)
        return _ln_mod(z, y, mod, ln_g, ln_b, mod2, gate_j=gate_j, mod_j=mod_j, tr=tr, n_ctx=n_ctx, alpha=alpha)
    if w.dtype != BF16:
        w = _to_bf16(w)
    tm = tr * _pick(l_a // tr, (3, 2, 1))
    tk = k_n if k_n <= 2048 else _pick(k_n, (1408, 1024, 512, 256, 128))
    nk = k_n // tk
    row = pl.BlockSpec((None, tm, d), lambda b, t, k: (b, t, 0))
    seg = pl.BlockSpec((None, 2, 6, d), lambda b, t, k: (b, 0, 0, 0))
    vec = pl.BlockSpec((1, d), lambda b, t, k: (0, 0))
    out_shape = [jax.ShapeDtypeStruct((b_n, l_a, d), F32)]
    out_specs = [row]
    if mod_j is not None:
        out_shape.append(jax.ShapeDtypeStruct((b_n, l_a, d), BF16))
        out_specs.append(row)
    res = pl.pallas_call(
        functools.partial(_proj_ln_kernel, nk=nk, gate_j=gate_j, mod_j=mod_j, alpha=alpha, tr=tr,
                          ncb=n_ctx // tr),
        grid=(b_n, l_a // tm, nk),
        in_specs=[pl.BlockSpec((None, tm, tk), lambda b, t, k: (b, t, k)),
                  pl.BlockSpec((None, tk, d), lambda b, t, k: (g, k, 0),
                               pipeline_mode=pl.Buffered(1) if nk == 1 else None),
                  row, seg, vec, vec, seg],
        out_specs=out_specs,
        out_shape=out_shape,
        compiler_params=_cparams(("parallel", "parallel", "arbitrary"), 56),
        name="proj_ln",
    )(a, w, z, mod, ln_g.reshape(1, d), ln_b.reshape(1, d), mod2)
    return res if mod_j is not None else (res[0], None)


def _seg_shift(x, row, shift, n_ctx):
    l_n = x.shape[0]
    rolled = pltpu.roll(x, (-shift) % l_n, 0)
    src = row + shift
    same_seg = (src >= 0) & (src < l_n) & ((src < n_ctx) == (row < n_ctx))
    return jnp.where(same_seg, rolled, 0.0)


def _rw_mix_kernel(z_ref, mod_ref, mu_ref, o_ref, *, n_ctx):
    z = z_ref[...]
    row = lax.broadcasted_iota(jnp.int32, z.shape, 0)
    is_ctx = row < n_ctx
    shift = jnp.where(is_ctx, mod_ref[0, 0:1, :], mod_ref[1, 0:1, :])
    scale = jnp.where(is_ctx, mod_ref[0, 1:2, :], mod_ref[1, 1:2, :])
    h = z * (1 + scale) + shift
    dx = 0.5 * (_seg_shift(h, row, -1, n_ctx) + _seg_shift(h, row, 1, n_ctx)) - h
    for n in range(6):
        o_ref[n] = (h + dx * mu_ref[n:n + 1, :]).astype(BF16)


def _rw_mix(z, mod, mu, *, n_ctx):
    b_n, l_n, d = z.shape
    tc = _pick(d, (256, 128))
    return pl.pallas_call(
        functools.partial(_rw_mix_kernel, n_ctx=n_ctx),
        grid=(b_n, d // tc),
        in_specs=[pl.BlockSpec((None, l_n, tc), lambda b, j: (b, 0, j)),
                  pl.BlockSpec((None, 2, 6, tc), lambda b, j: (b, 0, 0, j)),
                  pl.BlockSpec((6, tc), lambda b, j: (0, j))],
        out_specs=pl.BlockSpec((6, None, l_n, tc), lambda b, j: (0, b, 0, j)),
        out_shape=jax.ShapeDtypeStruct((6, b_n, l_n, d), BF16),
        compiler_params=_cparams(("parallel", "parallel"), 48),
        name="rw_mix",
    )(z, mod, mu)


def _lora_kernel(x_ref, a_ref, b_ref, o_ref, *, act):
    t = jnp.dot(x_ref[...], a_ref[...], preferred_element_type=F32)
    if act == "tanh":
        t = jnp.tanh(t)
    elif act == "sigmoid":
        t = _sigmoid(t)
    o_ref[...] = jnp.dot(t.astype(BF16), b_ref[...], preferred_element_type=F32)


def _lora(xs, x_idx, a, b, act):
    g_n, d, r = a.shape
    m_n = xs.shape[1]
    tm = _pick(m_n, (512, 256, 128, 64))
    return pl.pallas_call(
        functools.partial(_lora_kernel, act=act),
        grid=(g_n, m_n // tm),
        in_specs=[pl.BlockSpec((None, tm, d), lambda g, i: (x_idx, i, 0)),
                  pl.BlockSpec((None, d, r), lambda g, i: (g, 0, 0)),
                  pl.BlockSpec((None, r, d), lambda g, i: (g, 0, 0))],
        out_specs=pl.BlockSpec((None, tm, d), lambda g, i: (g, i, 0)),
        out_shape=jax.ShapeDtypeStruct((g_n, m_n, d), F32),
        compiler_params=_cparams(("parallel", "parallel"), 40),
        name="lora",
    )(xs, a, b)


def _chunk_of(q, ncc, nc, rev):
    if not rev:
        return q
    return jnp.where(q < ncc, ncc - 1 - q, nc - 1 - (q - ncc))


def _rwkv_kernel(r_ref, k_ref, v_ref, lw_ref, la_ref, g_ref, w0_ref, a0_ref, kk_ref, ka_ref, rk_ref,
                 gng_ref, gnb_ref, o_ref,
                 y_scr, mr_s, n_s, *, n_ctx):
    t_n = CHUNK
    h2 = 2 * t_n
    l_n = r_ref.shape[0]
    nc, ncc = l_n // t_n, n_ctx // t_n
    group = _pick(nc, (6, 4, 3, 2, 1))
    lane = lax.broadcasted_iota(jnp.int32, (1, LANES), 1)
    m1 = jnp.where(lane < RW_HEAD, 1.0, 0.0)
    m2 = 1.0 - m1
    ri = lax.broadcasted_iota(jnp.int32, (LANES, LANES), 0)
    ci = lax.broadcasted_iota(jnp.int32, (LANES, LANES), 1)
    same_head = (ri // RW_HEAD) == (ci // RW_HEAD)
    gsum_b = jnp.where(same_head, 1.0, 0.0).astype(BF16)
    gavg_b = jnp.where(same_head, 1.0 / RW_HEAD, 0.0).astype(BF16)
    eye = jnp.where(ri == ci, 1.0, 0.0)
    tr_i, tc_i = ri % t_n, ci % t_n
    k_k, k_a = kk_ref[...], ka_ref[...]

    def stack(x):
        return jnp.concatenate([x * m1, x * m2], axis=0)

    def rows_of(c):
        return pl.ds(pl.multiple_of(c * t_n, t_n), t_n)

    def head_sum(x, w_b):
        return _mdot(_parts(x, 3), [w_b], order=3)

    tri3_b = [_cumsum_matrix(t_n, rev) for rev in (False, True)]
    strict = [(tc_i > tr_i) if rev else (tc_i < tr_i) for rev in (False, True)]
    incl = [(tc_i >= tr_i) if rev else (tc_i <= tr_i) for rev in (False, True)]

    def stage_prep(c):
        rows = rows_of(c)
        k, r, v = k_ref[rows, :], r_ref[rows, :], v_ref[rows, :]
        kkr = k * k_k
        both = dict(kkr=kkr, ss=head_sum(kkr * kkr, gsum_b))
        items = []
        for d in (0, 1):
            lw = -RW_DECAY_SCALE * _sigmoid(w0_ref[d:d + 1, :] + lw_ref[d, rows, :])
            a = _sigmoid(a0_ref[d:d + 1, :] + la_ref[d, rows, :])
            items.append(dict(d=d, c=c, k=k, r=r, v=v, lw=lw, a=a, both=both, cum=_cumsum(tri3_b[d], lw)))
        return items

    def stage_amat(s):
        d, cum, lw, a, both = s["d"], s["cum"], s["lw"], s["a"], s["both"]
        if "kk" not in both:
            both["kk"] = both["kkr"] * lax.rsqrt(both["ss"] + 1e-12)
            both["vp"] = _parts(stack(s["v"]), 1)
        kk = both["kk"]
        kd = s["k"] * (1 + (a - 1) * k_a)
        bv = kk * a
        p_end = cum[0:1, :] if d == 1 else cum[t_n - 1:t_n, :]
        e_m = jnp.exp(-cum)
        e_h = jnp.exp(p_end - cum)
        ktp = _parts(stack(kk * jnp.exp(cum - lw)), 2)
        rt = stack(s["r"] * jnp.exp(cum))
        k2p = _cat_parts([_parts(stack(bv * e_m), 2), _parts(stack(kd * e_m), 2)], 0)
        return dict(d=d, c=s["c"], ktp=ktp, rt=rt, vp=both["vp"], p_end=p_end,
                    bh=stack(bv * e_h), kh=stack(kd * e_h),
                    amat=_mdot(_cat_parts([ktp, _parts(rt, 2)], 0), k2p, NT))

    def stage_square(s):
        d, amat = s["d"], s["amat"]
        lt = jnp.where(strict[d], amat[:h2, :h2], 0.0).T
        ltp = _parts(lt, 2)
        msk = jnp.concatenate([jnp.where(strict[d], amat[:h2, h2:], 0.0),
                               jnp.where(incl[d], amat[h2:, h2:], 0.0)], axis=0)
        s = dict(s, pt=eye - lt, xt=_mdot(ltp, ltp),
                 av=_mdot(_parts(msk, 2), s["vp"][:1]),
                 arbp=_parts(jnp.where(incl[d], amat[h2:, :h2], 0.0), 2))
        del s["amat"]
        return s

    def stage_double(s, final):
        xh = _parts(s["xt"], 1)
        ptp = _parts(s["pt"], 2)
        if final:
            return dict(s, pt=s["pt"] + _mdot(xh, ptp))
        rhs = [jnp.concatenate([ptp[0], xh[0]], axis=1), jnp.concatenate([ptp[1], jnp.zeros_like(xh[0])], axis=1)]
        both = _mdot(xh, rhs)
        return dict(s, pt=s["pt"] + both[:, :LANES], xt=both[:, LANES:])

    def stage_solve(s):
        rhs = jnp.concatenate([s["ktp"][0], (-s["av"][:h2]).astype(BF16)], axis=1)
        return dict(s, wub=_mdot(_parts(s["pt"].T, 2), [rhs]).astype(BF16))

    def stage_fold(s):
        d, c, wub = s["d"], s["c"], s["wub"]
        aw = _mdot(s["arbp"], [wub])
        zb = jnp.zeros((h2, LANES), BF16)
        lhs = _cat_parts([_parts(s["bh"].T, 2), _parts(s["kh"].T, 2)], 1)
        rhs = jnp.concatenate([wub, jnp.concatenate([zb, s["vp"][0]], axis=1)], axis=0)
        mn = _mdot(lhs, [rhs])
        dg = jnp.where(ri == ci, jnp.broadcast_to(jnp.exp(s["p_end"]), (LANES, LANES)), 0.0)
        rp = s["rt"] - aw[:, :LANES]
        mrp = _parts(jnp.concatenate([dg - mn[:, :LANES], rp[:t_n] + rp[t_n:]], axis=0), 2)
        for i in range(2):
            mr_s[d, c, i] = mrp[i]
        n_s[d, c] = mn[:, LANES:]
        y0 = s["av"][h2:] + aw[:, LANES:]
        return y0[:t_n] + y0[t_n:]

    def local(i, carry):
        sts = [s for g in range(group) for s in stage_prep(i * group + g)]
        sts = [stage_amat(s) for s in sts]
        sts = [stage_square(s) for s in sts]
        for step in range(5):
            sts = [stage_double(s, step == 4) for s in sts]
        sts = [stage_solve(s) for s in sts]
        y0 = [stage_fold(s) for s in sts]
        for g in range(group):
            y_scr[rows_of(i * group + g), :] = y0[2 * g] + y0[2 * g + 1]
        return carry

    lax.fori_loop(0, nc // group, local, 0)

    def seq(q, hs):
        cs = (q, _chunk_of(q, ncc, nc, True))
        mh = [_mdot([mr_s[d, cs[d], 0], mr_s[d, cs[d], 1]], _parts(hs[d], 2)) for d in (0, 1)]
        for d in (0, 1):
            y_scr[rows_of(cs[d]), :] += mh[d][h2:]
        return tuple(mh[d][:h2] + n_s[d, cs[d]] for d in (0, 1))

    zero = jnp.zeros((LANES, LANES), F32)
    lax.fori_loop(0, nc, seq, (zero, zero))

    n_post = _pick(nc, (4, 3, 2, 1))

    def post(i, carry):
        rows = [rows_of(i * n_post + g) for g in range(n_post)]

        def bonus_sum(rw):
            k, r = k_ref[rw, :], r_ref[rw, :]
            kd_f = k * (1 + (_sigmoid(a0_ref[0:1, :] + la_ref[0, rw, :]) - 1) * k_a)
            kd_b = k * (1 + (_sigmoid(a0_ref[1:2, :] + la_ref[1, rw, :]) - 1) * k_a)
            return head_sum(r * (kd_f + kd_b) * rk_ref[...], gsum_b)
        bsum = [bonus_sum(rw) for rw in rows]
        ys = [y_scr[rw, :] for rw in rows]
        ycs = [y - m for y, m in zip(ys, [head_sum(y, gavg_b) for y in ys])]
        var = [head_sum(yc * yc, gavg_b) for yc in ycs]
        for rw, yc, vr, bs in zip(rows, ycs, var, bsum):
            yn = yc * lax.rsqrt(vr + RW_GN_EPS) * gng_ref[...] + gnb_ref[...]
            o_ref[rw, :] = ((yn + bs * v_ref[rw, :]) * g_ref[rw, :]).astype(BF16)
        return carry

    lax.fori_loop(0, nc // n_post, post, 0)


def _rwkv_scan(rkv, lw, la, g, w0, a0, k_k, k_a, r_k, gn_g, gn_b, *, n_ctx):
    _, b_n, l_n, d = rkv.shape
    nc = l_n // CHUNK
    col = lambda n: pl.BlockSpec((None, None, l_n, LANES), lambda b, p, n=n: (n, b, 0, p))
    two = pl.BlockSpec((2, None, l_n, LANES), lambda b, p: (0, b, 0, p))
    par = lambda rows: pl.BlockSpec((rows, LANES), lambda b, p: (0, p))
    return pl.pallas_call(
        functools.partial(_rwkv_kernel, n_ctx=n_ctx),
        grid=(b_n, d // LANES),
        in_specs=[col(0), col(1), col(2), two, two,
                  pl.BlockSpec((None, l_n, LANES), lambda b, p: (b, 0, p)),
                  par(2), par(2), par(1), par(1), par(1), par(1), par(1)],
        out_specs=pl.BlockSpec((None, l_n, LANES), lambda b, p: (b, 0, p)),
        out_shape=jax.ShapeDtypeStruct((b_n, l_n, d), BF16),
        scratch_shapes=[pltpu.VMEM((l_n, LANES), F32),
                        pltpu.VMEM((2, nc, 2, LANES + CHUNK, LANES), BF16),
                        pltpu.VMEM((2, nc, LANES, LANES), F32)],
        compiler_params=_cparams(("parallel", "parallel"), 56),
        name="rwkv_scan",
    )(rkv, rkv, rkv, lw, la, g, w0, a0, k_k.reshape(1, d), k_a.reshape(1, d), r_k.reshape(1, d),
      gn_g.reshape(1, d), gn_b.reshape(1, d))


def _pad_axis(w, axis, to):
    pad = [(0, 0)] * w.ndim
    pad[axis] = (0, to - w.shape[axis])
    return jnp.pad(w, pad)


def _rwkv7_layer(z, mod, n_ctx, mu, w_rkv, w0, w1, w2, a0, a1, a2, g1, g2, k_k, k_a, r_k, gn_g, gn_b):
    b_n, l_n, d = z.shape
    m_n = b_n * l_n
    xs = _rw_mix(z, mod, mu, n_ctx=n_ctx).reshape(6, m_n, d)
    rkv = _matmul(xs, w_rkv)
    r_w = -(-w1.shape[-1] // LANES) * LANES
    r_a = -(-a1.shape[-1] // LANES) * LANES
    lw = _lora(xs, 3, _pad_axis(w1, 2, r_w).astype(BF16), _pad_axis(w2, 1, r_w).astype(BF16), "tanh")
    la = _lora(xs, 4, _pad_axis(a1, 2, r_a).astype(BF16), _pad_axis(a2, 1, r_a).astype(BF16), None)
    gate = _lora(xs, 5, g1[None].astype(BF16), g2[None].astype(BF16), "sigmoid")
    o = _rwkv_scan(rkv.reshape(3, b_n, l_n, d), lw.reshape(2, b_n, l_n, d), la.reshape(2, b_n, l_n, d),
                   gate.reshape(b_n, l_n, d), w0, a0, k_k, k_a, r_k, gn_g, gn_b, n_ctx=n_ctx)
    return o


def _qkv_rope_kernel(a_ref, w_ref, cos_ref, sa_ref, sb_ref, o_ref, wb_ref, *, n_q, n_qk):
    j, i = pl.program_id(0), pl.program_id(1)

    @pl.when(i == 0)
    def _():
        wb_ref[...] = w_ref[...].astype(BF16)

    x = jnp.dot(a_ref[...], wb_ref[...], preferred_element_type=F32)

    @pl.when(j < n_qk)
    def _():
        q = DA_HEAD // 4
        q_scale = jnp.where(j < n_q, DA_HEAD ** -0.5 * math.log2(math.e), 1.0)
        cos, s_a, s_b = cos_ref[...] * q_scale, sa_ref[...] * q_scale, sb_ref[...] * q_scale
        for s in range(x.shape[1] // DA_HEAD):
            cols = slice(s * DA_HEAD, (s + 1) * DA_HEAD)
            xs = x[:, cols]
            rot = xs * cos + pltpu.roll(xs, DA_HEAD - q, 1) * s_a + pltpu.roll(xs, q, 1) * s_b
            o_ref[:, cols] = rot.astype(BF16)

    @pl.when(j >= n_qk)
    def _():
        o_ref[...] = x.astype(BF16)


def _qkv_rope(h, w_qkv, cos, s_a, s_b):
    b_n, l_n, d = h.shape
    d3 = w_qkv.shape[1]
    tm = _pick(l_n, (768, 1024, 512, 256, 128, 64))
    tn = _pick(d, (1024, 512, 256, 128))
    n_t = l_n // tm
    tab = pl.BlockSpec((tm, DA_HEAD), lambda j, i: (i % n_t, 0))
    out = pl.pallas_call(
        functools.partial(_qkv_rope_kernel, n_q=d // tn, n_qk=2 * d // tn),
        grid=(d3 // tn, b_n * n_t),
        in_specs=[pl.BlockSpec((tm, d), lambda j, i: (i, 0)),
                  pl.BlockSpec((d, tn), lambda j, i: (0, j)), tab, tab, tab],
        out_specs=pl.BlockSpec((tm, tn), lambda j, i: (i, j)),
        out_shape=jax.ShapeDtypeStruct((b_n * l_n, d3), BF16),
        scratch_shapes=[pltpu.VMEM((d, tn), BF16)],
        compiler_params=_cparams(("parallel", "arbitrary"), 48),
        name="qkv_rope",
    )(h.reshape(b_n * l_n, d), w_qkv, cos, s_a, s_b)
    return out.reshape(b_n, l_n, d3)


def _attn_kernel(q_ref, k_ref, v_ref, lam_ref, sg_ref, o_ref, *, ncb, n_ctx, lam_init):
    qi = pl.program_id(2)
    lv = lam_ref[...]
    lam = (jnp.exp(jnp.sum(lv[0:1] * lv[1:2], axis=-1, keepdims=True))
           - jnp.exp(jnp.sum(lv[2:3] * lv[3:4], axis=-1, keepdims=True)) + lam_init)

    def attend(nk):
        def probs(m):
            q = q_ref[:, m * DA_HEAD:(m + 1) * DA_HEAD]
            k = k_ref[0:nk, m * DA_HEAD:(m + 1) * DA_HEAD]
            s = lax.dot_general(q, k, NT, preferred_element_type=F32)
            e = jnp.exp2(s - jnp.max(s, axis=-1, keepdims=True))
            return e, 1.0 / jnp.sum(e, axis=-1, keepdims=True)
        e0, i0 = probs(0)
        e1, i1 = probs(1)
        v = v_ref[0:nk, :]
        o = (jnp.dot(e0.astype(BF16), v, preferred_element_type=F32) * i0
             - jnp.dot(e1.astype(BF16), v, preferred_element_type=F32) * (lam * i1))
        o = o * lax.rsqrt(jnp.mean(o * o, axis=-1, keepdims=True) + 1e-5) * sg_ref[...] * (1 - lam_init)
        o_ref[...] = o.astype(BF16)

    if ncb > 0:
        @pl.when(qi < ncb)
        def _():
            attend(n_ctx)

    @pl.when(qi >= ncb)
    def _():
        attend(k_ref.shape[0])


def _attention(qkv, lam_vec, sub_g, *, tq, n_ctx, lam_init):
    b_n, l_n, d3 = qkv.shape
    d = d3 // 3
    hw = 2 * DA_HEAD
    nh = d // hw
    return pl.pallas_call(
        functools.partial(_attn_kernel, ncb=n_ctx // tq, n_ctx=n_ctx, lam_init=lam_init),
        grid=(b_n, nh, l_n // tq),
        in_specs=[pl.BlockSpec((None, tq, hw), lambda b, h, t: (b, t, h)),
                  pl.BlockSpec((None, l_n, hw), lambda b, h, t: (b, 0, nh + h)),
                  pl.BlockSpec((None, l_n, hw), lambda b, h, t: (b, 0, 2 * nh + h)),
                  pl.BlockSpec((4, DA_HEAD), lambda b, h, t: (0, 0)),
                  pl.BlockSpec((1, hw), lambda b, h, t: (0, 0))],
        out_specs=pl.BlockSpec((None, tq, hw), lambda b, h, t: (b, t, h)),
        out_shape=jax.ShapeDtypeStruct((b_n, l_n, d), BF16),
        compiler_params=_cparams(("parallel", "parallel", "arbitrary"), 48),
        name="diff_attn",
    )(qkv, qkv, qkv, lam_vec, sub_g.reshape(1, hw))


def _rope_tables(n_ctx, n_lat):
    n_rows = n_lat // GRID_W
    row = jnp.repeat(jnp.arange(n_rows, dtype=F32), GRID_W)
    col = jnp.tile(jnp.arange(GRID_W, dtype=F32), n_rows)
    nf = DA_HEAD // 4
    inv_freq = ROPE_BASE ** (-jnp.arange(nf, dtype=F32) / nf)
    ang_r, ang_c = row[:, None] * inv_freq, col[:, None] * inv_freq
    ang = jnp.concatenate([ang_r, ang_r, ang_c, ang_c], axis=-1)
    ang = jnp.concatenate([jnp.zeros((n_ctx, DA_HEAD), F32), ang], axis=0)
    cos, sin = jnp.cos(ang), jnp.sin(ang)
    even_q = (jnp.arange(DA_HEAD) // nf) % 2 == 0
    return cos, jnp.where(even_q, -sin, 0.0), jnp.where(even_q, 0.0, sin)


def _diff_attention_layer(h, n_ctx, layer_idx, w_qkv, lam_vec, sub_g, *, tr):
    b_n, l_n, d = h.shape
    cos, s_a, s_b = _rope_tables(n_ctx, l_n - n_ctx)
    qkv = _qkv_rope(h, w_qkv, cos, s_a, s_b)
    lam_init = 0.8 - 0.6 * math.exp(-0.3 * layer_idx)
    o = _attention(qkv, lam_vec, sub_g, tq=tr, n_ctx=n_ctx, lam_init=lam_init)
    return o


def _hgrn_kernel(q_ref, i_ref, g_ref, ff_ref, fb_ref, low_ref, ng_ref, o_ref, o_scr, *, n_ctx, layer_idx):
    t_n = CHUNK
    l_n = q_ref.shape[0]
    nc, ncc = l_n // t_n, n_ctx // t_n
    r64 = lax.broadcasted_iota(jnp.int32, (t_n, t_n), 0)
    c64 = lax.broadcasted_iota(jnp.int32, (t_n, t_n), 1)

    def rows_of(c):
        return pl.ds(pl.multiple_of(c * t_n, t_n), t_n)

    f_refs = (ff_ref, fb_ref)
    lbs, incl = [], []
    tri3_b = [_cumsum_matrix(t_n, rev) for rev in (False, True)]
    for d in (0, 1):
        low = low_ref[d]
        e = jnp.exp(low - jnp.max(low, axis=0, keepdims=True))
        sm = e / jnp.sum(e, axis=0, keepdims=True)
        cs = sm[0:1]
        for rr in range(1, layer_idx + 1):
            cs = cs + sm[rr:rr + 1]
        lbs.append(cs - sm[0:1])
        incl.append((c64 >= r64) if d == 1 else (c64 <= r64))
    group = _pick(nc, (4, 2, 1))

    def stage_cum(d, c):
        rows = rows_of(c)
        f = lbs[d] + (1.0 - lbs[d]) * _sigmoid(f_refs[d][rows, :])
        return dict(d=d, rows=rows, f=f, cum=_cumsum(tri3_b[d], jnp.log(f)))

    def stage_att(s):
        d, cum, rows = s["d"], s["cum"], s["rows"]
        b_end = cum[0:1, :] if d == 1 else cum[t_n - 1:t_n, :]
        qv = q_ref[rows, :]
        qd = (qv * _sigmoid(qv) * jnp.exp(cum)).astype(BF16)
        kk = 1.0 - s["f"]
        v = i_ref[rows, :]
        kd = (kk * jnp.exp(-cum)).astype(BF16)
        ke = (kk * jnp.exp(b_end - cum)).astype(BF16)
        return dict(d=d, rows=rows, qd=qd, vb=v.astype(BF16), dec=jnp.exp(b_end),
                    att=lax.dot_general(qd, kd, NT, preferred_element_type=F32),
                    upd=jnp.dot(v.T.astype(BF16), ke, preferred_element_type=F32))

    def stage_intra(s):
        att = jnp.where(incl[s["d"]], s["att"], 0.0).astype(BF16)
        return dict(s, o=jnp.dot(att, s["vb"], preferred_element_type=F32))

    def body(i, states):
        items = [(d, _chunk_of(i * group + g, ncc, nc, d == 1)) for g in range(group) for d in (0, 1)]
        sts = [stage_cum(d, c) for d, c in items]
        sts = [stage_att(s) for s in sts]
        sts = [stage_intra(s) for s in sts]
        states = list(states)
        for s in sts:
            d = s["d"]
            o = s["o"] + lax.dot_general(s["qd"], states[d].astype(BF16), NT, preferred_element_type=F32)
            o_scr[d, s["rows"], :] = o
            states[d] = states[d] * s["dec"] + s["upd"]
        return tuple(states)

    zero = jnp.zeros((LANES, LANES), F32)
    lax.fori_loop(0, nc // group, body, (zero, zero))

    p_n = t_n * _pick(nc, (4, 3, 2, 1))

    def post(c, carry):
        rows = pl.ds(pl.multiple_of(c * p_n, p_n), p_n)
        o = o_scr[0, rows, :] + o_scr[1, rows, :]
        o = o * lax.rsqrt(jnp.mean(o * o, axis=-1, keepdims=True) + 1e-5) * ng_ref[...]
        gv = g_ref[rows, :]
        o_ref[rows, :] = (o * (gv * _sigmoid(gv))).astype(BF16)
        return carry

    lax.fori_loop(0, l_n // p_n, post, 0)


def _hgrn2_layer(h, n_ctx, layer_idx, w_in, lower, norm_g):
    b_n, l_n, d = h.shape
    m_n = b_n * l_n
    nh = d // HG_EXPAND
    proj = _mm2(h.reshape(m_n, d), w_in).reshape(b_n, l_n, 5 * d)
    col = lambda n: pl.BlockSpec((None, l_n, LANES), lambda b, p, n=n: (b, 0, n * nh + p))
    o = pl.pallas_call(
        functools.partial(_hgrn_kernel, n_ctx=n_ctx, layer_idx=layer_idx),
        grid=(b_n, nh),
        in_specs=[col(0), col(1), col(2), col(3), col(4),
                  pl.BlockSpec((2, lower.shape[1], LANES), lambda b, p: (0, 0, p)),
                  pl.BlockSpec((1, LANES), lambda b, p: (0, 0))],
        out_specs=pl.BlockSpec((None, l_n, LANES), lambda b, p: (b, 0, p)),
        out_shape=jax.ShapeDtypeStruct((b_n, l_n, d), BF16),
        scratch_shapes=[pltpu.VMEM((2, l_n, LANES), F32)],
        compiler_params=_cparams(("parallel", "parallel"), 40),
        name="hgrn_scan",
    )(proj, proj, proj, proj, proj, lower, norm_g.reshape(1, LANES))
    return o


def _gelu_tanh(x):
    return 0.5 * x * (1.0 + jnp.tanh(math.sqrt(2.0 / math.pi) * (x + 0.044715 * (x * x * x))))


def _softplus(x):
    return jnp.maximum(x, 0.0) + jnp.log1p(jnp.exp(-jnp.abs(x)))


SEG_PAD = 8


def _lin_scan(a_ref, u_ref, hl_s, cp_s, h_s, base, row0, n, rev, h_in, accumulate):
    seg = n // 8
    stride = seg + SEG_PAD
    n_p = a_ref.shape[0]

    def step(i, carry):
        t = (seg - 1 - i) if rev else i
        idx = pl.ds(base + t, 8, stride=stride)
        out = []
        for j in range(n_p):
            hl, cp = carry[j]
            a = a_ref[j, idx, :]
            hl = a * hl + u_ref[j, idx, :]
            cp = a * cp
            hl_s[j, idx, :] = hl
            cp_s[j, idx, :] = cp
            out.append((hl, cp))
        return tuple(out)

    init = tuple((jnp.zeros((8, LANES), F32), jnp.ones((8, LANES), F32)) for _ in range(n_p))
    ends = lax.fori_loop(0, seg, step, init)
    order = range(7, -1, -1) if rev else range(8)
    h_out = []
    for j in range(n_p):
        hl_e, cp_e = ends[j]
        carry = h_in[j]
        for s in order:
            r0, p0 = row0 + s * seg, base + s * stride
            blk = hl_s[j, p0:p0 + seg, :] + cp_s[j, p0:p0 + seg, :] * carry
            if accumulate:
                h_s[j, r0:r0 + seg, :] += blk
            else:
                h_s[j, r0:r0 + seg, :] = blk
            carry = hl_e[s:s + 1, :] + cp_e[s:s + 1, :] * carry
        h_out.append(carry)
    return h_out


def _rglru_kernel(gb_ref, xb_ref, cw_ref, cb_ref, wg_ref, bg_ref, lam_ref, o_ref, a_s, u_s, h_s, hl_s, cp_s, *,
                  n_ctx):
    l_n = xb_ref.shape[0]
    n_lat = l_n - n_ctx
    x = xb_ref[...]
    row = lax.broadcasted_iota(jnp.int32, x.shape, 0)
    k_w = cw_ref.shape[0]
    xc = cb_ref[...] + sum(_seg_shift(x, row, j - (k_w - 1) // 2, n_ctx) * cw_ref[j:j + 1, :]
                           for j in range(k_w))
    xcb = xc.astype(BF16)
    n_p = x.shape[1] // LANES
    for d in (0, 1):
        gate = lambda g: _sigmoid(jnp.dot(xcb, wg_ref[d, g].astype(BF16), preferred_element_type=F32)
                                  + bg_ref[d, g:g + 1, :])
        log_a = -LR_C * gate(0) * _softplus(-lam_ref[d:d + 1, :])
        a = jnp.exp(log_a)
        u = jnp.sqrt(jnp.tanh(-log_a) * (jnp.exp(2.0 * log_a) + 1.0)) * gate(1) * xc
        h = [jnp.zeros((1, LANES), F32)] * n_p
        base = 0
        for row0, n in ((0, n_ctx), (n_ctx, n_lat)):
            if n:
                seg = n // 8
                for j in range(n_p):
                    for s in range(8):
                        src = slice(row0 + s * seg, row0 + (s + 1) * seg)
                        dst = slice(base + s * (seg + SEG_PAD), base + s * (seg + SEG_PAD) + seg)
                        a_s[j, dst, :] = a[src, j * LANES:(j + 1) * LANES]
                        u_s[j, dst, :] = u[src, j * LANES:(j + 1) * LANES]
                h = _lin_scan(a_s, u_s, hl_s, cp_s, h_s, base, row0, n, d == 1, h, d == 1)
                base += 8 * (seg + SEG_PAD)
    for j in range(n_p):
        cols = slice(j * LANES, (j + 1) * LANES)
        o_ref[:, cols] = (h_s[j, n_ctx:l_n, :] * _gelu_tanh(gb_ref[n_ctx:, cols])).astype(BF16)


def _rglru_layer(h, n_ctx, w_in, conv_w, conv_b, w_gate, b_gate, lam):
    b_n, l_n, d = h.shape
    n_lat = l_n - n_ctx
    nb = d // LR_BS
    proj = _mm2(h.reshape(b_n * l_n, d), w_in).reshape(b_n, l_n, 2 * d)
    k_w = conv_w.shape[0]
    o = pl.pallas_call(
        functools.partial(_rglru_kernel, n_ctx=n_ctx),
        grid=(b_n, nb),
        in_specs=[pl.BlockSpec((None, l_n, LR_BS), lambda b, j: (b, 0, j)),
                  pl.BlockSpec((None, l_n, LR_BS), lambda b, j: (b, 0, nb + j)),
                  pl.BlockSpec((k_w, LR_BS), lambda b, j: (0, j)),
                  pl.BlockSpec((1, LR_BS), lambda b, j: (0, j)),
                  pl.BlockSpec((2, 2, None, LR_BS, LR_BS), lambda b, j: (0, 0, j, 0, 0)),
                  pl.BlockSpec((2, 2, LR_BS), lambda b, j: (0, 0, j)),
                  pl.BlockSpec((2, LR_BS), lambda b, j: (0, j))],
        out_specs=pl.BlockSpec((None, n_lat, LR_BS), lambda b, j: (b, 0, j)),
        out_shape=jax.ShapeDtypeStruct((b_n, n_lat, d), BF16),
        scratch_shapes=[pltpu.VMEM((LR_BS // LANES, l_n + 16 * SEG_PAD, LANES), F32)] * 5,
        compiler_params=_cparams(("parallel", "parallel"), 56),
        name="rglru",
    )(proj, proj, conv_w, conv_b.reshape(1, d), w_gate, b_gate, lam)
    return o


def _ffn_up_kernel(h_ref, wg_ref, wv_ref, cg_ref, cv_ref, bg_ref, bv_ref, o_ref, u_scr, *, n_ctx, tr):
    l_n, tf = o_ref.shape
    k_w = cg_ref.shape[0]
    half = (k_w - 1) // 2
    n_buf, pad = u_scr.shape[0], (u_scr.shape[1] - l_n) // 2
    rblk = tr * _pick(l_n // tr, (3, 4, 2, 1))
    row = lax.broadcasted_iota(jnp.int32, (tr, LANES), 0)
    for p in range(n_buf):
        u_scr[p, 0:pad, :] = jnp.zeros((pad, 2 * LANES), F32)
        u_scr[p, pad + l_n:, :] = jnp.zeros((pad, 2 * LANES), F32)

    def weights(s):
        cols = slice(s * LANES, (s + 1) * LANES)
        return jnp.concatenate([wg_ref[:, cols], wv_ref[:, cols]], axis=1).astype(BF16)

    def product(s, w, r0):
        u_scr[s % n_buf, pad + r0:pad + r0 + rblk, :] = jnp.dot(h_ref[r0:r0 + rblk, :], w,
                                                                preferred_element_type=F32)

    def finish(s, r0):
        p, cols = s % n_buf, slice(s * LANES, (s + 1) * LANES)

        def conv(lane0, w_ref, b_ref):
            acc = None
            for j in range(k_w):
                sh = j - half
                x = u_scr[p, pad + r0 + sh:pad + r0 + sh + tr, lane0:lane0 + LANES]
                if sh < 0 and r0 in (0, n_ctx):
                    x = jnp.where(row < -sh, 0.0, x)
                if sh > 0 and r0 + tr in (n_ctx, l_n):
                    x = jnp.where(row >= tr - sh, 0.0, x)
                t = x * w_ref[j:j + 1, cols]
                acc = t if acc is None else acc + t
            return b_ref[:, cols] + acc
        gate = conv(0, cg_ref, bg_ref)
        val = conv(LANES, cv_ref, bv_ref)
        o_ref[r0:r0 + tr, cols] = (gate * _sigmoid(gate) * val).astype(BF16)

    n_s = tf // LANES
    for s in range(n_s + 1):
        w = weights(s) if s < n_s else None
        for r0 in range(0, l_n, rblk):
            if s < n_s:
                product(s, w, r0)
            if s > 0:
                for r1 in range(r0, r0 + rblk, tr):
                    finish(s - 1, r1)


def _conv_ffn(h, n_ctx, w_up, layer, conv_w, conv_b, *, tr):
    b_n, l_n, d = h.shape
    f = w_up.shape[2] // 2
    tf = _pick(f, (512, 256, 128))
    nf = f // tf
    k_w = conv_w.shape[0]
    cb = conv_b.reshape(1, 2 * f)
    act = pl.pallas_call(
        functools.partial(_ffn_up_kernel, n_ctx=n_ctx, tr=tr),
        grid=(b_n, nf),
        in_specs=[pl.BlockSpec((None, l_n, d), lambda b, j: (b, 0, 0), pipeline_mode=pl.Buffered(1)),
                  pl.BlockSpec((None, d, tf), lambda b, j: (layer, 0, j)),
                  pl.BlockSpec((None, d, tf), lambda b, j: (layer, 0, nf + j)),
                  pl.BlockSpec((k_w, tf), lambda b, j: (0, j)),
                  pl.BlockSpec((k_w, tf), lambda b, j: (0, nf + j)),
                  pl.BlockSpec((1, tf), lambda b, j: (0, j)),
                  pl.BlockSpec((1, tf), lambda b, j: (0, nf + j))],
        out_specs=pl.BlockSpec((None, l_n, tf), lambda b, j: (b, 0, j)),
        out_shape=jax.ShapeDtypeStruct((b_n, l_n, f), BF16),
        scratch_shapes=[pltpu.VMEM((3, l_n + 16, 2 * LANES), F32)],
        compiler_params=_cparams(("parallel", "arbitrary"), 56),
        name="ffn_up",
    )(h, w_up, w_up, conv_w, conv_w, cb, cb)
    return act


def kernel(x, c, ctx, c_ctx, ada_w, ada_b, ln_g, ln_b, ffn_w_up, ffn_conv_w, ffn_conv_b, ffn_w_down, rw_mu, rw_w_rkv, rw_w0, rw_w1, rw_w2, rw_a0, rw_a1, rw_a2, rw_g1, rw_g2, rw_k_k, rw_k_a, rw_r_k, rw_gn_g, rw_gn_b, rw_w_o, da_w_qkv, da_lambda, da_sub_g, da_w_o, hg_w_in, hg_lower, hg_norm_g, hg_w_o, lr_w_in, lr_conv_w, lr_conv_b, lr_w_gate, lr_b_gate, lr_lambda, lr_w_o):
    b_n, n_lat, d = x.shape
    n_ctx = ctx.shape[1]
    depth = ada_w.shape[0]
    assert depth == 4 and rw_mu.shape[0] == 1, "one occurrence of each of the four mixers"
    assert b_n + 1 <= 8 and n_ctx % CHUNK == 0 and n_lat % CHUNK == 0
    tr = math.gcd(math.gcd(n_ctx, n_lat), 256)
    alpha = (2 * depth) ** 0.25

    c8 = jnp.concatenate([c, c_ctx[None], jnp.zeros((8 - b_n - 1, d), F32)], axis=0)
    m = _ada(c8, ada_w, ada_b)
    m_lat = m[:, :b_n].reshape(depth, b_n, 1, 6, d)
    m_ctx = jnp.broadcast_to(m[:, b_n].reshape(depth, 1, 1, 6, d), (depth, b_n, 1, 6, d))
    mod = jnp.concatenate([m_ctx, m_lat], axis=2)

    z = jnp.concatenate([ctx, x], axis=1)
    w_down_b = _to_bf16(ffn_w_down)
    h = None
    for i in range(depth):
        last = i == depth - 1
        if i == 0:
            o = _rwkv7_layer(z, mod[0], n_ctx, rw_mu[0], rw_w_rkv[0], rw_w0[0], rw_w1[0], rw_w2[0], rw_a0[0],
                             rw_a1[0], rw_a2[0], rw_g1[0], rw_g2[0], rw_k_k[0], rw_k_a[0], rw_r_k[0],
                             rw_gn_g[0], rw_gn_b[0])
            w_o = rw_w_o
        elif i == 1:
            o = _diff_attention_layer(h, n_ctx, i, da_w_qkv[0], da_lambda[0], da_sub_g[0], tr=tr)
            w_o = da_w_o
        elif i == 2:
            o = _hgrn2_layer(h, n_ctx, i, hg_w_in[0], hg_lower, hg_norm_g[0])
            w_o = hg_w_o
        else:
            o = _rglru_layer(h, n_ctx, lr_w_in[0], lr_conv_w[0], lr_conv_b[0], lr_w_gate[0], lr_b_gate[0],
                             lr_lambda[0])
            w_o = lr_w_o
        z, h = _proj_ln(o, w_o, 0, z, mod[i], ln_g[i, 0], ln_b[i, 0], mod[i], gate_j=2, mod_j=3, tr=tr,
                        n_ctx=n_ctx, alpha=alpha)
        if last:
            n_ctx = 0
        act = _conv_ffn(h, n_ctx, ffn_w_up, i, ffn_conv_w[i], ffn_conv_b[i], tr=tr)
        z, h = _proj_ln(act, w_down_b, i, z, mod[i], ln_g[i, 1], ln_b[i, 1], mod[min(i + 1, depth - 1)],
                        gate_j=5, mod_j=None if last else 0, tr=tr, n_ctx=n_ctx, alpha=alpha)
    return z
```

```python
import functools
import math

import jax
import jax.numpy as jnp
from jax import lax
from jax.experimental import pallas as pl
from jax.experimental.pallas import tpu as pltpu

F32, BF16 = jnp.float32, jnp.bfloat16

LANES = 128
CHUNK = 64
LN_EPS = 1e-5
GRID_W = 64
ROPE_BASE = 10000.0
RW_HEAD = 64
RW_DECAY_SCALE = 0.606531
RW_GN_EPS = 64e-5
DA_HEAD = 128
HG_EXPAND = 128
LR_BS = 256
LR_C = 8.0
MIB = 1024 * 1024


def _pick(n, cands):
    for c in cands:
        if n % c == 0:
            return c
    return n


def _cparams(sem, vmem_mib):
    return pltpu.CompilerParams(dimension_semantics=sem, vmem_limit_bytes=vmem_mib * MIB)


def _sigmoid(x):
    return jax.nn.sigmoid(x)


NN = (((1,), (0,)), ((), ()))
NT = (((1,), (1,)), ((), ()))


def _parts(x, n):
    out = []
    for i in range(n):
        p = x.astype(BF16)
        out.append(p)
        if i + 1 < n:
            x = x - p.astype(F32)
    return out


def _mdot(ap, bp, dims=NN, order=2):
    pairs = [(a, b) for i, a in enumerate(ap) for j, b in enumerate(bp) if i + j < order]
    (ca,), (cb,) = dims[0]
    lhs = jnp.concatenate([a for a, _ in pairs], axis=ca) if len(pairs) > 1 else pairs[0][0]
    rhs = jnp.concatenate([b for _, b in pairs], axis=cb) if len(pairs) > 1 else pairs[0][1]
    return lax.dot_general(lhs, rhs, dims, preferred_element_type=F32)


def _cumsum_matrix(t_n, rev):
    r = lax.broadcasted_iota(jnp.int32, (t_n, 3 * t_n), 0)
    c = lax.broadcasted_iota(jnp.int32, (t_n, 3 * t_n), 1) % t_n
    return jnp.where((c >= r) if rev else (c <= r), 1.0, 0.0).astype(BF16)


def _cumsum(tri3_b, x):
    return jnp.dot(tri3_b, jnp.concatenate(_parts(x, 3), axis=0), preferred_element_type=F32)


def _cat_parts(xs, axis):
    return [jnp.concatenate(ps, axis=axis) for ps in zip(*xs)]


def _mm_kernel(a_ref, w_ref, o_ref, acc_ref, *, nk):
    if w_ref.dtype == BF16:
        prod = jnp.dot(a_ref[...], w_ref[...], preferred_element_type=F32)
    else:
        tk = w_ref.shape[0]
        n_kc = 2 if tk % (2 * LANES) == 0 else 1
        prod = None
        for c in range(n_kc):
            ks = slice(c * (tk // n_kc), (c + 1) * (tk // n_kc))
            t = jnp.dot(a_ref[:, ks], w_ref[ks, :].astype(BF16), preferred_element_type=F32)
            prod = t if prod is None else prod + t
    if nk == 1:
        o_ref[...] = prod.astype(o_ref.dtype)
    else:
        k = pl.program_id(3)

        @pl.when(k == 0)
        def _():
            acc_ref[...] = prod

        @pl.when(k > 0)
        def _():
            acc_ref[...] += prod

        @pl.when(k == nk - 1)
        def _():
            o_ref[...] = acc_ref[...].astype(o_ref.dtype)


def _mm_wres_kernel(a_ref, w_ref, o_ref, wb_ref):
    i = pl.program_id(2)
    k_n = w_ref.shape[0]
    n_kc = 4 if k_n % (4 * LANES) == 0 else 1

    @pl.when(i == 0)
    def _():
        kc = k_n // n_kc
        acc = None
        for c in range(n_kc):
            ks = slice(c * kc, (c + 1) * kc)
            wb = w_ref[ks, :].astype(BF16)
            wb_ref[ks, :] = wb
            t = jnp.dot(a_ref[:, ks], wb, preferred_element_type=F32)
            acc = t if acc is None else acc + t
        o_ref[...] = acc.astype(o_ref.dtype)

    @pl.when(i > 0)
    def _():
        o_ref[...] = jnp.dot(a_ref[...], wb_ref[...], preferred_element_type=F32).astype(o_ref.dtype)


def _matmul(a, w, *, out_dtype=F32, a_off=0):
    g_n, k_n, n_n = w.shape
    m_n = a.shape[1]
    tm = _pick(m_n, (1024, 512, 256, 128, 64))
    tn = _pick(n_n, (1024, 512, 256, 128))
    tk = k_n if k_n <= 2048 else _pick(k_n, (2816, 2048, 1024, 512))
    nk = k_n // tk
    if w.dtype == F32 and nk > 1:
        tn = _pick(n_n, (512, 256, 128))
    if w.dtype == F32 and nk == 1:
        return pl.pallas_call(
            _mm_wres_kernel,
            grid=(g_n, n_n // tn, m_n // tm),
            in_specs=[pl.BlockSpec((None, tm, k_n), lambda g, j, i: (g + a_off, i, 0)),
                      pl.BlockSpec((None, k_n, tn), lambda g, j, i: (g, 0, j))],
            out_specs=pl.BlockSpec((None, tm, tn), lambda g, j, i: (g, i, j)),
            out_shape=jax.ShapeDtypeStruct((g_n, m_n, n_n), out_dtype),
            scratch_shapes=[pltpu.VMEM((k_n, tn), BF16)],
            compiler_params=_cparams(("parallel", "parallel", "arbitrary"), 48),
            name="matmul_wres",
        )(a, w)
    return pl.pallas_call(
        functools.partial(_mm_kernel, nk=nk),
        grid=(g_n, m_n // tm, n_n // tn, nk),
        in_specs=[pl.BlockSpec((None, tm, tk), lambda g, i, j, k: (g + a_off, i, k)),
                  pl.BlockSpec((None, tk, tn), lambda g, i, j, k: (g, k, j))],
        out_specs=pl.BlockSpec((None, tm, tn), lambda g, i, j, k: (g, i, j)),
        out_shape=jax.ShapeDtypeStruct((g_n, m_n, n_n), out_dtype),
        scratch_shapes=[pltpu.VMEM((tm, tn), F32)],
        compiler_params=_cparams(("parallel", "parallel", "parallel", "arbitrary"), 48),
        name="matmul",
    )(a, w)


def _mm2(a, w, **kw):
    return _matmul(a[None], w[None], **kw)[0]


def _ada_kernel(c_ref, w_ref, b_ref, o_ref):
    c = c_ref[...]
    s = (c * _sigmoid(c)).astype(BF16)
    o_ref[...] = jnp.dot(s, w_ref[...].astype(BF16), preferred_element_type=F32) + b_ref[...]


def _ada(c8, ada_w, ada_b):
    depth, d, n = ada_w.shape
    tn = _pick(n, (1024, 512, 256, 128))
    return pl.pallas_call(
        _ada_kernel,
        grid=(depth, n // tn),
        in_specs=[pl.BlockSpec((8, d), lambda l, j: (0, 0)),
                  pl.BlockSpec((None, d, tn), lambda l, j: (l, 0, j)),
                  pl.BlockSpec((None, 1, tn), lambda l, j: (l, 0, j))],
        out_specs=pl.BlockSpec((None, 8, tn), lambda l, j: (l, 0, j)),
        out_shape=jax.ShapeDtypeStruct((depth, 8, n), F32),
        compiler_params=_cparams(("parallel", "parallel"), 40),
        name="ada",
    )(c8, ada_w, ada_b.reshape(depth, 1, n))


def _ln_mod_kernel(z_ref, y_ref, mod_ref, g_ref, b_ref, mod2_ref, *out_refs, gate_j, mod_j, alpha):
    m = mod_ref[...]
    zz = alpha * z_ref[...] + y_ref[...] * m[gate_j:gate_j + 1]
    mu = jnp.mean(zz, axis=-1, keepdims=True)
    zc = zz - mu
    var = jnp.mean(zc * zc, axis=-1, keepdims=True)
    zn = zc * lax.rsqrt(var + LN_EPS) * g_ref[...] + b_ref[...]
    out_refs[0][...] = zn
    if mod_j is not None:
        m2 = mod2_ref[...]
        out_refs[1][...] = (zn * (1 + m2[mod_j + 1:mod_j + 2]) + m2[mod_j:mod_j + 1]).astype(BF16)


def _ln_mod(z, y, mod, ln_g, ln_b, mod2, *, gate_j, mod_j, tr, n_ctx, alpha):
    b_n, l_z, d = z.shape
    l_y = y.shape[1]
    z_off = (l_z - l_y) // tr
    ncb = (n_ctx - (l_z - l_y)) // tr
    seg = lambda b, t: (b, jnp.where(t < ncb, 0, 1), 0, 0)
    row = pl.BlockSpec((None, tr, d), lambda b, t: (b, t, 0))
    out_shape = [jax.ShapeDtypeStruct((b_n, l_y, d), F32)]
    out_specs = [row]
    if mod_j is not None:
        out_shape.append(jax.ShapeDtypeStruct((b_n, l_y, d), BF16))
        out_specs.append(row)
    res = pl.pallas_call(
        functools.partial(_ln_mod_kernel, gate_j=gate_j, mod_j=mod_j, alpha=alpha),
        grid=(b_n, l_y // tr),
        in_specs=[pl.BlockSpec((None, tr, d), lambda b, t: (b, t + z_off, 0)),
                  row,
                  pl.BlockSpec((None, None, 6, d), seg),
                  pl.BlockSpec((1, d), lambda b, t: (0, 0)),
                  pl.BlockSpec((1, d), lambda b, t: (0, 0)),
                  pl.BlockSpec((None, None, 6, d), seg)],
        out_specs=out_specs,
        out_shape=out_shape,
        compiler_params=_cparams(("parallel", "parallel"), 40),
        name="ln_mod",
    )(z, y, mod, ln_g.reshape(1, d), ln_b.reshape(1, d), mod2)
    return res if mod_j is not None else (res[0], None)


def _cast_kernel(x_ref, o_ref):
    o_ref[...] = x_ref[...].astype(o_ref.dtype)


def _to_bf16(w):
    g_n, k_n, n_n = w.shape
    tk = _pick(k_n, (512, 256, 128))
    spec = pl.BlockSpec((None, tk, n_n), lambda g, i: (g, i, 0))
    return pl.pallas_call(
        _cast_kernel, grid=(g_n, k_n // tk), in_specs=[spec], out_specs=spec,
        out_shape=jax.ShapeDtypeStruct(w.shape, BF16),
        compiler_params=_cparams(("parallel", "parallel"), 32),
        name="to_bf16",
    )(w)


def _proj_ln_kernel(a_ref, w_ref, z_ref, mod_ref, g_ref, b_ref, mod2_ref, zo_ref, *ho_refs,
                    nk, gate_j, mod_j, alpha, tr, ncb):
    t, k = pl.program_id(1), pl.program_id(2)
    tm = zo_ref.shape[0]
    rb = tm // 2 if tm % 16 == 0 else tm

    def accumulate(first):
        for r0 in range(0, tm, rb):
            p = jnp.dot(a_ref[r0:r0 + rb, :], w_ref[...], preferred_element_type=F32)
            if first:
                zo_ref[r0:r0 + rb, :] = p
            else:
                zo_ref[r0:r0 + rb, :] += p

    rs = math.gcd(tr, 64)

    def finish():
        for sb in range(tm // rs):
            rows = slice(sb * rs, (sb + 1) * rs)
            is_ctx = t * (tm // tr) + (sb * rs) // tr < ncb
            m = jnp.where(is_ctx, mod_ref[0], mod_ref[1])
            zz = alpha * z_ref[rows, :] + zo_ref[rows, :] * m[gate_j:gate_j + 1]
            mu = jnp.mean(zz, axis=-1, keepdims=True)
            zc = zz - mu
            var = jnp.mean(zc * zc, axis=-1, keepdims=True)
            zn = zc * lax.rsqrt(var + LN_EPS) * g_ref[...] + b_ref[...]
            zo_ref[rows, :] = zn
            if mod_j is not None:
                m2 = jnp.where(is_ctx, mod2_ref[0], mod2_ref[1])
                ho_refs[0][rows, :] = (zn * (1 + m2[mod_j + 1:mod_j + 2]) + m2[mod_j:mod_j + 1]).astype(BF16)

    if nk == 1:
        accumulate(True)
        finish()
    else:
        @pl.when(k == 0)
        def _():
            accumulate(True)

        @pl.when(k > 0)
        def _():
            accumulate(False)

        @pl.when(k == nk - 1)
        def _():
            finish()


def _proj_ln(a, w, g, z, mod, ln_g, ln_b, mod2, *, gate_j, mod_j, tr, n_ctx, alpha):
    b_n, l_a, k_n = a.shape
    d = w.shape[2]
    if z.shape[1] != l_a:
        y = _matmul(a.reshape(1, b_n * l_a, k_n), w[g:g + 1])[0].reshape(b_n, l_a, d)
        return _ln_mod(z, y, mod, ln_g, ln_b, mod2, gate_j=gate_j, mod_j=mod_j, tr=tr, n_ctx=n_ctx, alpha=alpha)
    if w.dtype != BF16:
        w = _to_bf16(w)
    tm = tr * _pick(l_a // tr, (3, 2, 1))
    tk = k_n if k_n <= 2048 else _pick(k_n, (1408, 1024, 512, 256, 128))
    nk = k_n // tk
    row = pl.BlockSpec((None, tm, d), lambda b, t, k: (b, t, 0))
    seg = pl.BlockSpec((None, 2, 6, d), lambda b, t, k: (b, 0, 0, 0))
    vec = pl.BlockSpec((1, d), lambda b, t, k: (0, 0))
    out_shape = [jax.ShapeDtypeStruct((b_n, l_a, d), F32)]
    out_specs = [row]
    if mod_j is not None:
        out_shape.append(jax.ShapeDtypeStruct((b_n, l_a, d), BF16))
        out_specs.append(row)
    res = pl.pallas_call(
        functools.partial(_proj_ln_kernel, nk=nk, gate_j=gate_j, mod_j=mod_j, alpha=alpha, tr=tr,
                          ncb=n_ctx // tr),
        grid=(b_n, l_a // tm, nk),
        in_specs=[pl.BlockSpec((None, tm, tk), lambda b, t, k: (b, t, k)),
                  pl.BlockSpec((None, tk, d), lambda b, t, k: (g, k, 0),
                               pipeline_mode=pl.Buffered(1) if nk == 1 else None),
                  row, seg, vec, vec, seg],
        out_specs=out_specs,
        out_shape=out_shape,
        compiler_params=_cparams(("parallel", "parallel", "arbitrary"), 56),
        name="proj_ln",
    )(a, w, z, mod, ln_g.reshape(1, d), ln_b.reshape(1, d), mod2)
    return res if mod_j is not None else (res[0], None)


def _seg_shift(x, row, shift, n_ctx):
    l_n = x.shape[0]
    rolled = pltpu.roll(x, (-shift) % l_n, 0)
    src = row + shift
    same_seg = (src >= 0) & (src < l_n) & ((src < n_ctx) == (row < n_ctx))
    return jnp.where(same_seg, rolled, 0.0)


def _rw_mix_kernel(z_ref, mod_ref, mu_ref, o_ref, *, n_ctx):
    z = z_ref[...]
    row = lax.broadcasted_iota(jnp.int32, z.shape, 0)
    is_ctx = row < n_ctx
    shift = jnp.where(is_ctx, mod_ref[0, 0:1, :], mod_ref[1, 0:1, :])
    scale = jnp.where(is_ctx, mod_ref[0, 1:2, :], mod_ref[1, 1:2, :])
    h = z * (1 + scale) + shift
    dx = 0.5 * (_seg_shift(h, row, -1, n_ctx) + _seg_shift(h, row, 1, n_ctx)) - h
    for n in range(6):
        o_ref[n] = (h + dx * mu_ref[n:n + 1, :]).astype(BF16)


def _rw_mix(z, mod, mu, *, n_ctx):
    b_n, l_n, d = z.shape
    tc = _pick(d, (256, 128))
    return pl.pallas_call(
        functools.partial(_rw_mix_kernel, n_ctx=n_ctx),
        grid=(b_n, d // tc),
        in_specs=[pl.BlockSpec((None, l_n, tc), lambda b, j: (b, 0, j)),
                  pl.BlockSpec((None, 2, 6, tc), lambda b, j: (b, 0, 0, j)),
                  pl.BlockSpec((6, tc), lambda b, j: (0, j))],
        out_specs=pl.BlockSpec((6, None, l_n, tc), lambda b, j: (0, b, 0, j)),
        out_shape=jax.ShapeDtypeStruct((6, b_n, l_n, d), BF16),
        compiler_params=_cparams(("parallel", "parallel"), 48),
        name="rw_mix",
    )(z, mod, mu)


def _lora_kernel(x_ref, a_ref, b_ref, o_ref, *, act):
    t = jnp.dot(x_ref[...], a_ref[...], preferred_element_type=F32)
    if act == "tanh":
        t = jnp.tanh(t)
    elif act == "sigmoid":
        t = _sigmoid(t)
    o_ref[...] = jnp.dot(t.astype(BF16), b_ref[...], preferred_element_type=F32)


def _lora(xs, x_idx, a, b, act):
    g_n, d, r = a.shape
    m_n = xs.shape[1]
    tm = _pick(m_n, (512, 256, 128, 64))
    return pl.pallas_call(
        functools.partial(_lora_kernel, act=act),
        grid=(g_n, m_n // tm),
        in_specs=[pl.BlockSpec((None, tm, d), lambda g, i: (x_idx, i, 0)),
                  pl.BlockSpec((None, d, r), lambda g, i: (g, 0, 0)),
                  pl.BlockSpec((None, r, d), lambda g, i: (g, 0, 0))],
        out_specs=pl.BlockSpec((None, tm, d), lambda g, i: (g, i, 0)),
        out_shape=jax.ShapeDtypeStruct((g_n, m_n, d), F32),
        compiler_params=_cparams(("parallel", "parallel"), 40),
        name="lora",
    )(xs, a, b)


def _chunk_of(q, ncc, nc, rev):
    if not rev:
        return q
    return jnp.where(q < ncc, ncc - 1 - q, nc - 1 - (q - ncc))


def _rwkv_kernel(r_ref, k_ref, v_ref, lw_ref, la_ref, g_ref, w0_ref, a0_ref, kk_ref, ka_ref, rk_ref,
                 gng_ref, gnb_ref, o_ref,
                 y_scr, mr_s, n_s, *, n_ctx):
    t_n = CHUNK
    h2 = 2 * t_n
    l_n = r_ref.shape[0]
    nc, ncc = l_n // t_n, n_ctx // t_n
    group = _pick(nc, (6, 4, 3, 2, 1))
    lane = lax.broadcasted_iota(jnp.int32, (1, LANES), 1)
    m1 = jnp.where(lane < RW_HEAD, 1.0, 0.0)
    m2 = 1.0 - m1
    ri = lax.broadcasted_iota(jnp.int32, (LANES, LANES), 0)
    ci = lax.broadcasted_iota(jnp.int32, (LANES, LANES), 1)
    same_head = (ri // RW_HEAD) == (ci // RW_HEAD)
    gsum_b = jnp.where(same_head, 1.0, 0.0).astype(BF16)
    gavg_b = jnp.where(same_head, 1.0 / RW_HEAD, 0.0).astype(BF16)
    eye = jnp.where(ri == ci, 1.0, 0.0)
    tr_i, tc_i = ri % t_n, ci % t_n
    k_k, k_a = kk_ref[...], ka_ref[...]

    def stack(x):
        return jnp.concatenate([x * m1, x * m2], axis=0)

    def rows_of(c):
        return pl.ds(pl.multiple_of(c * t_n, t_n), t_n)

    def head_sum(x, w_b):
        return _mdot(_parts(x, 3), [w_b], order=3)

    tri3_b = [_cumsum_matrix(t_n, rev) for rev in (False, True)]
    strict = [(tc_i > tr_i) if rev else (tc_i < tr_i) for rev in (False, True)]
    incl = [(tc_i >= tr_i) if rev else (tc_i <= tr_i) for rev in (False, True)]

    def stage_prep(d, c):
        rows = rows_of(c)
        k, r, v = k_ref[rows, :], r_ref[rows, :], v_ref[rows, :]
        kkr = k * k_k
        both = dict(kkr=kkr, ss=head_sum(kkr * kkr, gsum_b))
        lw = -RW_DECAY_SCALE * _sigmoid(w0_ref[d:d + 1, :] + lw_ref[d, rows, :])
        a = _sigmoid(a0_ref[d:d + 1, :] + la_ref[d, rows, :])
        return dict(d=d, c=c, k=k, r=r, v=v, lw=lw, a=a, both=both, cum=_cumsum(tri3_b[d], lw))

    def stage_amat(s):
        d, cum, lw, a, both = s["d"], s["cum"], s["lw"], s["a"], s["both"]
        if "kk" not in both:
            both["kk"] = both["kkr"] * lax.rsqrt(both["ss"] + 1e-12)
            both["vp"] = _parts(stack(s["v"]), 1)
        kk = both["kk"]
        kd = s["k"] * (1 + (a - 1) * k_a)
        bv = kk * a
        p_end = cum[0:1, :] if d == 1 else cum[t_n - 1:t_n, :]
        e_m = jnp.exp(-cum)
        e_h = jnp.exp(p_end - cum)
        ktp = _parts(stack(kk * jnp.exp(cum - lw)), 2)
        rt = stack(s["r"] * jnp.exp(cum))
        k2p = _cat_parts([_parts(stack(bv * e_m), 2), _parts(stack(kd * e_m), 2)], 0)
        return dict(d=d, c=s["c"], ktp=ktp, rt=rt, vp=both["vp"], p_end=p_end,
                    bh=stack(bv * e_h), kh=stack(kd * e_h),
                    amat=_mdot(_cat_parts([ktp, _parts(rt, 2)], 0), k2p, NT))

    def stage_square(s):
        d, amat = s["d"], s["amat"]
        lt = jnp.where(strict[d], amat[:h2, :h2], 0.0).T
        ltp = _parts(lt, 2)
        msk = jnp.concatenate([jnp.where(strict[d], amat[:h2, h2:], 0.0),
                               jnp.where(incl[d], amat[h2:, h2:], 0.0)], axis=0)
        s = dict(s, pt=eye - lt, xt=_mdot(ltp, ltp),
                 av=_mdot(_parts(msk, 2), s["vp"][:1]),
                 arbp=_parts(jnp.where(incl[d], amat[h2:, :h2], 0.0), 2))
        del s["amat"]
        return s

    def stage_double(s, final):
        xh = _parts(s["xt"], 1)
        ptp = _parts(s["pt"], 2)
        if final:
            return dict(s, pt=s["pt"] + _mdot(xh, ptp))
        rhs = [jnp.concatenate([ptp[0], xh[0]], axis=1), jnp.concatenate([ptp[1], jnp.zeros_like(xh[0])], axis=1)]
        both = _mdot(xh, rhs)
        return dict(s, pt=s["pt"] + both[:, :LANES], xt=both[:, LANES:])

    def stage_solve(s):
        rhs = jnp.concatenate([s["ktp"][0], (-s["av"][:h2]).astype(BF16)], axis=1)
        return dict(s, wub=_mdot(_parts(s["pt"].T, 2), [rhs]).astype(BF16))

    def stage_fold(s):
        d, c, wub = s["d"], s["c"], s["wub"]
        aw = _mdot(s["arbp"], [wub])
        zb = jnp.zeros((h2, LANES), BF16)
        lhs = _cat_parts([_parts(s["bh"].T, 2), _parts(s["kh"].T, 2)], 1)
        rhs = jnp.concatenate([wub, jnp.concatenate([zb, s["vp"][0]], axis=1)], axis=0)
        mn = _mdot(lhs, [rhs])
        dg = jnp.where(ri == ci, jnp.broadcast_to(jnp.exp(s["p_end"]), (LANES, LANES)), 0.0)
        rp = s["rt"] - aw[:, :LANES]
        mrp = _parts(jnp.concatenate([dg - mn[:, :LANES], rp[:t_n] + rp[t_n:]], axis=0), 2)
        for i in range(2):
            mr_s[d, c, i] = mrp[i]
        n_s[d, c] = mn[:, LANES:]
        y0 = s["av"][h2:] + aw[:, LANES:]
        return y0[:t_n] + y0[t_n:]

    def seq(q, hs):
        cs = (q, _chunk_of(q, ncc, nc, True))
        mh = [_mdot([mr_s[d, cs[d], 0], mr_s[d, cs[d], 1]], _parts(hs[d], 2)) for d in (0, 1)]
        for d in (0, 1):
            y_scr[rows_of(cs[d]), :] += mh[d][h2:]
        return tuple(mh[d][:h2] + n_s[d, cs[d]] for d in (0, 1))

    def local(i, hs, with_seq):
        steps = [i * group + g for g in range(group)]
        pending = [q - group for q in steps] if with_seq else []
        sts = [stage_prep(d, q if d == 0 else _chunk_of(q, ncc, nc, True)) for q in steps for d in (0, 1)]
        stages = ([stage_amat, stage_square] + [functools.partial(stage_double, final=f) for f in (False,) * 4 + (True,)]
                  + [stage_solve])
        for stage in stages:
            sts = [stage(s) for s in sts]
            if pending:
                hs = seq(pending.pop(0), hs)
        for s in sts:
            y0 = stage_fold(s)
            y_scr[rows_of(s["c"]), :] += y0
        while pending:
            hs = seq(pending.pop(0), hs)
        return hs

    y_scr[...] = jnp.zeros(y_scr.shape, F32)
    zero = jnp.zeros((LANES, LANES), F32)
    n_trip = nc // group
    hs = local(0, (zero, zero), False)
    hs = lax.fori_loop(1, n_trip, functools.partial(local, with_seq=True), hs)
    for q in range((n_trip - 1) * group, nc):
        hs = seq(q, hs)

    n_post = _pick(nc, (4, 3, 2, 1))

    def post(i, carry):
        rows = [rows_of(i * n_post + g) for g in range(n_post)]

        def bonus_sum(rw):
            k, r = k_ref[rw, :], r_ref[rw, :]
            kd_f = k * (1 + (_sigmoid(a0_ref[0:1, :] + la_ref[0, rw, :]) - 1) * k_a)
            kd_b = k * (1 + (_sigmoid(a0_ref[1:2, :] + la_ref[1, rw, :]) - 1) * k_a)
            return head_sum(r * (kd_f + kd_b) * rk_ref[...], gsum_b)
        bsum = [bonus_sum(rw) for rw in rows]
        ys = [y_scr[rw, :] for rw in rows]
        ycs = [y - m for y, m in zip(ys, [head_sum(y, gavg_b) for y in ys])]
        var = [head_sum(yc * yc, gavg_b) for yc in ycs]
        for rw, yc, vr, bs in zip(rows, ycs, var, bsum):
            yn = yc * lax.rsqrt(vr + RW_GN_EPS) * gng_ref[...] + gnb_ref[...]
            o_ref[rw, :] = ((yn + bs * v_ref[rw, :]) * g_ref[rw, :]).astype(BF16)
        return carry

    lax.fori_loop(0, nc // n_post, post, 0)


def _rwkv_scan(rkv, lw, la, g, w0, a0, k_k, k_a, r_k, gn_g, gn_b, *, n_ctx):
    _, b_n, l_n, d = rkv.shape
    nc = l_n // CHUNK
    col = lambda n: pl.BlockSpec((None, None, l_n, LANES), lambda b, p, n=n: (n, b, 0, p))
    two = pl.BlockSpec((2, None, l_n, LANES), lambda b, p: (0, b, 0, p))
    par = lambda rows: pl.BlockSpec((rows, LANES), lambda b, p: (0, p))
    return pl.pallas_call(
        functools.partial(_rwkv_kernel, n_ctx=n_ctx),
        grid=(b_n, d // LANES),
        in_specs=[col(0), col(1), col(2), two, two,
                  pl.BlockSpec((None, l_n, LANES), lambda b, p: (b, 0, p)),
                  par(2), par(2), par(1), par(1), par(1), par(1), par(1)],
        out_specs=pl.BlockSpec((None, l_n, LANES), lambda b, p: (b, 0, p)),
        out_shape=jax.ShapeDtypeStruct((b_n, l_n, d), BF16),
        scratch_shapes=[pltpu.VMEM((l_n, LANES), F32),
                        pltpu.VMEM((2, nc, 2, LANES + CHUNK, LANES), BF16),
                        pltpu.VMEM((2, nc, LANES, LANES), F32)],
        compiler_params=_cparams(("parallel", "parallel"), 56),
        name="rwkv_scan",
    )(rkv, rkv, rkv, lw, la, g, w0, a0, k_k.reshape(1, d), k_a.reshape(1, d), r_k.reshape(1, d),
      gn_g.reshape(1, d), gn_b.reshape(1, d))


def _pad_axis(w, axis, to):
    pad = [(0, 0)] * w.ndim
    pad[axis] = (0, to - w.shape[axis])
    return jnp.pad(w, pad)


def _rwkv7_layer(z, mod, n_ctx, mu, w_rkv, w0, w1, w2, a0, a1, a2, g1, g2, k_k, k_a, r_k, gn_g, gn_b):
    b_n, l_n, d = z.shape
    m_n = b_n * l_n
    xs = _rw_mix(z, mod, mu, n_ctx=n_ctx).reshape(6, m_n, d)
    rkv = _matmul(xs, w_rkv)
    r_w = -(-w1.shape[-1] // LANES) * LANES
    r_a = -(-a1.shape[-1] // LANES) * LANES
    lw = _lora(xs, 3, _pad_axis(w1, 2, r_w).astype(BF16), _pad_axis(w2, 1, r_w).astype(BF16), "tanh")
    la = _lora(xs, 4, _pad_axis(a1, 2, r_a).astype(BF16), _pad_axis(a2, 1, r_a).astype(BF16), None)
    gate = _lora(xs, 5, g1[None].astype(BF16), g2[None].astype(BF16), "sigmoid")
    o = _rwkv_scan(rkv.reshape(3, b_n, l_n, d), lw.reshape(2, b_n, l_n, d), la.reshape(2, b_n, l_n, d),
                   gate.reshape(b_n, l_n, d), w0, a0, k_k, k_a, r_k, gn_g, gn_b, n_ctx=n_ctx)
    return o


def _qkv_rope_kernel(a_ref, w_ref, cos_ref, sa_ref, sb_ref, o_ref, wb_ref, *, n_q, n_qk):
    j, i = pl.program_id(0), pl.program_id(1)

    @pl.when(i == 0)
    def _():
        wb_ref[...] = w_ref[...].astype(BF16)

    x = jnp.dot(a_ref[...], wb_ref[...], preferred_element_type=F32)

    @pl.when(j < n_qk)
    def _():
        q = DA_HEAD // 4
        q_scale = jnp.where(j < n_q, DA_HEAD ** -0.5 * math.log2(math.e), 1.0)
        cos, s_a, s_b = cos_ref[...] * q_scale, sa_ref[...] * q_scale, sb_ref[...] * q_scale
        for s in range(x.shape[1] // DA_HEAD):
            cols = slice(s * DA_HEAD, (s + 1) * DA_HEAD)
            xs = x[:, cols]
            rot = xs * cos + pltpu.roll(xs, DA_HEAD - q, 1) * s_a + pltpu.roll(xs, q, 1) * s_b
            o_ref[:, cols] = rot.astype(BF16)

    @pl.when(j >= n_qk)
    def _():
        o_ref[...] = x.astype(BF16)


def _qkv_rope(h, w_qkv, cos, s_a, s_b):
    b_n, l_n, d = h.shape
    d3 = w_qkv.shape[1]
    tm = _pick(l_n, (768, 1024, 512, 256, 128, 64))
    tn = _pick(d, (1024, 512, 256, 128))
    n_t = l_n // tm
    tab = pl.BlockSpec((tm, DA_HEAD), lambda j, i: (i % n_t, 0))
    out = pl.pallas_call(
        functools.partial(_qkv_rope_kernel, n_q=d // tn, n_qk=2 * d // tn),
        grid=(d3 // tn, b_n * n_t),
        in_specs=[pl.BlockSpec((tm, d), lambda j, i: (i, 0)),
                  pl.BlockSpec((d, tn), lambda j, i: (0, j)), tab, tab, tab],
        out_specs=pl.BlockSpec((tm, tn), lambda j, i: (i, j)),
        out_shape=jax.ShapeDtypeStruct((b_n * l_n, d3), BF16),
        scratch_shapes=[pltpu.VMEM((d, tn), BF16)],
        compiler_params=_cparams(("parallel", "arbitrary"), 48),
        name="qkv_rope",
    )(h.reshape(b_n * l_n, d), w_qkv, cos, s_a, s_b)
    return out.reshape(b_n, l_n, d3)


def _attn_kernel(q_ref, k_ref, v_ref, lam_ref, sg_ref, o_ref, *, ncb, n_ctx, lam_init):
    qi = pl.program_id(2)
    lv = lam_ref[...]
    lam = (jnp.exp(jnp.sum(lv[0:1] * lv[1:2], axis=-1, keepdims=True))
           - jnp.exp(jnp.sum(lv[2:3] * lv[3:4], axis=-1, keepdims=True)) + lam_init)

    def attend(nk):
        def probs(m):
            q = q_ref[:, m * DA_HEAD:(m + 1) * DA_HEAD]
            k = k_ref[0:nk, m * DA_HEAD:(m + 1) * DA_HEAD]
            s = lax.dot_general(q, k, NT, preferred_element_type=F32)
            e = jnp.exp2(s - jnp.max(s, axis=-1, keepdims=True))
            return e, 1.0 / jnp.sum(e, axis=-1, keepdims=True)
        e0, i0 = probs(0)
        e1, i1 = probs(1)
        v = v_ref[0:nk, :]
        o = (jnp.dot(e0.astype(BF16), v, preferred_element_type=F32) * i0
             - jnp.dot(e1.astype(BF16), v, preferred_element_type=F32) * (lam * i1))
        o = o * lax.rsqrt(jnp.mean(o * o, axis=-1, keepdims=True) + 1e-5) * sg_ref[...] * (1 - lam_init)
        o_ref[...] = o.astype(BF16)

    if ncb > 0:
        @pl.when(qi < ncb)
        def _():
            attend(n_ctx)

    @pl.when(qi >= ncb)
    def _():
        attend(k_ref.shape[0])


def _attention(qkv, lam_vec, sub_g, *, tq, n_ctx, lam_init):
    b_n, l_n, d3 = qkv.shape
    d = d3 // 3
    hw = 2 * DA_HEAD
    nh = d // hw
    return pl.pallas_call(
        functools.partial(_attn_kernel, ncb=n_ctx // tq, n_ctx=n_ctx, lam_init=lam_init),
        grid=(b_n, nh, l_n // tq),
        in_specs=[pl.BlockSpec((None, tq, hw), lambda b, h, t: (b, t, h)),
                  pl.BlockSpec((None, l_n, hw), lambda b, h, t: (b, 0, nh + h)),
                  pl.BlockSpec((None, l_n, hw), lambda b, h, t: (b, 0, 2 * nh + h)),
                  pl.BlockSpec((4, DA_HEAD), lambda b, h, t: (0, 0)),
                  pl.BlockSpec((1, hw), lambda b, h, t: (0, 0))],
        out_specs=pl.BlockSpec((None, tq, hw), lambda b, h, t: (b, t, h)),
        out_shape=jax.ShapeDtypeStruct((b_n, l_n, d), BF16),
        compiler_params=_cparams(("parallel", "parallel", "arbitrary"), 48),
        name="diff_attn",
    )(qkv, qkv, qkv, lam_vec, sub_g.reshape(1, hw))


def _rope_tables(n_ctx, n_lat):
    n_rows = n_lat // GRID_W
    row = jnp.repeat(jnp.arange(n_rows, dtype=F32), GRID_W)
    col = jnp.tile(jnp.arange(GRID_W, dtype=F32), n_rows)
    nf = DA_HEAD // 4
    inv_freq = ROPE_BASE ** (-jnp.arange(nf, dtype=F32) / nf)
    ang_r, ang_c = row[:, None] * inv_freq, col[:, None] * inv_freq
    ang = jnp.concatenate([ang_r, ang_r, ang_c, ang_c], axis=-1)
    ang = jnp.concatenate([jnp.zeros((n_ctx, DA_HEAD), F32), ang], axis=0)
    cos, sin = jnp.cos(ang), jnp.sin(ang)
    even_q = (jnp.arange(DA_HEAD) // nf) % 2 == 0
    return cos, jnp.where(even_q, -sin, 0.0), jnp.where(even_q, 0.0, sin)


def _diff_attention_layer(h, n_ctx, layer_idx, w_qkv, lam_vec, sub_g, *, tr):
    b_n, l_n, d = h.shape
    cos, s_a, s_b = _rope_tables(n_ctx, l_n - n_ctx)
    qkv = _qkv_rope(h, w_qkv, cos, s_a, s_b)
    lam_init = 0.8 - 0.6 * math.exp(-0.3 * layer_idx)
    o = _attention(qkv, lam_vec, sub_g, tq=tr, n_ctx=n_ctx, lam_init=lam_init)
    return o


def _hgrn_kernel(q_ref, i_ref, g_ref, ff_ref, fb_ref, low_ref, ng_ref, o_ref, o_scr, *, n_ctx, layer_idx):
    t_n = CHUNK
    l_n = q_ref.shape[0]
    nc, ncc = l_n // t_n, n_ctx // t_n
    r64 = lax.broadcasted_iota(jnp.int32, (t_n, t_n), 0)
    c64 = lax.broadcasted_iota(jnp.int32, (t_n, t_n), 1)

    def rows_of(c):
        return pl.ds(pl.multiple_of(c * t_n, t_n), t_n)

    f_refs = (ff_ref, fb_ref)
    lbs, incl = [], []
    tri3_b = [_cumsum_matrix(t_n, rev) for rev in (False, True)]
    for d in (0, 1):
        low = low_ref[d]
        e = jnp.exp(low - jnp.max(low, axis=0, keepdims=True))
        sm = e / jnp.sum(e, axis=0, keepdims=True)
        cs = sm[0:1]
        for rr in range(1, layer_idx + 1):
            cs = cs + sm[rr:rr + 1]
        lbs.append(cs - sm[0:1])
        incl.append((c64 >= r64) if d == 1 else (c64 <= r64))
    group = _pick(nc, (4, 2, 1))

    def stage_cum(d, c):
        rows = rows_of(c)
        f = lbs[d] + (1.0 - lbs[d]) * _sigmoid(f_refs[d][rows, :])
        return dict(d=d, rows=rows, f=f, cum=_cumsum(tri3_b[d], jnp.log(f)))

    def stage_att(s):
        d, cum, rows = s["d"], s["cum"], s["rows"]
        b_end = cum[0:1, :] if d == 1 else cum[t_n - 1:t_n, :]
        qv = q_ref[rows, :]
        qd = (qv * _sigmoid(qv) * jnp.exp(cum)).astype(BF16)
        kk = 1.0 - s["f"]
        v = i_ref[rows, :]
        kd = (kk * jnp.exp(-cum)).astype(BF16)
        ke = (kk * jnp.exp(b_end - cum)).astype(BF16)
        return dict(d=d, rows=rows, qd=qd, vb=v.astype(BF16), dec=jnp.exp(b_end),
                    att=lax.dot_general(qd, kd, NT, preferred_element_type=F32),
                    upd=jnp.dot(v.T.astype(BF16), ke, preferred_element_type=F32))

    def stage_intra(s):
        att = jnp.where(incl[s["d"]], s["att"], 0.0).astype(BF16)
        return dict(s, o=jnp.dot(att, s["vb"], preferred_element_type=F32))

    def body(i, states):
        items = [(d, _chunk_of(i * group + g, ncc, nc, d == 1)) for g in range(group) for d in (0, 1)]
        sts = [stage_cum(d, c) for d, c in items]
        sts = [stage_att(s) for s in sts]
        sts = [stage_intra(s) for s in sts]
        states = list(states)
        for s in sts:
            d = s["d"]
            o = s["o"] + lax.dot_general(s["qd"], states[d].astype(BF16), NT, preferred_element_type=F32)
            o_scr[d, s["rows"], :] = o
            states[d] = states[d] * s["dec"] + s["upd"]
        return tuple(states)

    zero = jnp.zeros((LANES, LANES), F32)
    lax.fori_loop(0, nc // group, body, (zero, zero))

    p_n = t_n * _pick(nc, (4, 3, 2, 1))

    def post(c, carry):
        rows = pl.ds(pl.multiple_of(c * p_n, p_n), p_n)
        o = o_scr[0, rows, :] + o_scr[1, rows, :]
        o = o * lax.rsqrt(jnp.mean(o * o, axis=-1, keepdims=True) + 1e-5) * ng_ref[...]
        gv = g_ref[rows, :]
        o_ref[rows, :] = (o * (gv * _sigmoid(gv))).astype(BF16)
        return carry

    lax.fori_loop(0, l_n // p_n, post, 0)


def _hgrn2_layer(h, n_ctx, layer_idx, w_in, lower, norm_g):
    b_n, l_n, d = h.shape
    m_n = b_n * l_n
    nh = d // HG_EXPAND
    proj = _mm2(h.reshape(m_n, d), w_in).reshape(b_n, l_n, 5 * d)
    col = lambda n: pl.BlockSpec((None, l_n, LANES), lambda b, p, n=n: (b, 0, n * nh + p))
    o = pl.pallas_call(
        functools.partial(_hgrn_kernel, n_ctx=n_ctx, layer_idx=layer_idx),
        grid=(b_n, nh),
        in_specs=[col(0), col(1), col(2), col(3), col(4),
                  pl.BlockSpec((2, lower.shape[1], LANES), lambda b, p: (0, 0, p)),
                  pl.BlockSpec((1, LANES), lambda b, p: (0, 0))],
        out_specs=pl.BlockSpec((None, l_n, LANES), lambda b, p: (b, 0, p)),
        out_shape=jax.ShapeDtypeStruct((b_n, l_n, d), BF16),
        scratch_shapes=[pltpu.VMEM((2, l_n, LANES), F32)],
        compiler_params=_cparams(("parallel", "parallel"), 40),
        name="hgrn_scan",
    )(proj, proj, proj, proj, proj, lower, norm_g.reshape(1, LANES))
    return o


def _gelu_tanh(x):
    return 0.5 * x * (1.0 + jnp.tanh(math.sqrt(2.0 / math.pi) * (x + 0.044715 * (x * x * x))))


def _softplus(x):
    return jnp.maximum(x, 0.0) + jnp.log1p(jnp.exp(-jnp.abs(x)))


SEG_PAD = 8


def _lin_scan(a_ref, u_ref, hl_s, cp_s, h_s, base, row0, n, rev, h_in, accumulate):
    seg = n // 8
    stride = seg + SEG_PAD
    n_p = a_ref.shape[0]

    def step(i, carry):
        t = (seg - 1 - i) if rev else i
        idx = pl.ds(base + t, 8, stride=stride)
        out = []
        for j in range(n_p):
            hl, cp = carry[j]
            a = a_ref[j, idx, :]
            hl = a * hl + u_ref[j, idx, :]
            cp = a * cp
            hl_s[j, idx, :] = hl
            cp_s[j, idx, :] = cp
            out.append((hl, cp))
        return tuple(out)

    init = tuple((jnp.zeros((8, LANES), F32), jnp.ones((8, LANES), F32)) for _ in range(n_p))
    ends = lax.fori_loop(0, seg, step, init)
    order = range(7, -1, -1) if rev else range(8)
    h_out = []
    for j in range(n_p):
        hl_e, cp_e = ends[j]
        carry = h_in[j]
        for s in order:
            r0, p0 = row0 + s * seg, base + s * stride
            blk = hl_s[j, p0:p0 + seg, :] + cp_s[j, p0:p0 + seg, :] * carry
            if accumulate:
                h_s[j, r0:r0 + seg, :] += blk
            else:
                h_s[j, r0:r0 + seg, :] = blk
            carry = hl_e[s:s + 1, :] + cp_e[s:s + 1, :] * carry
        h_out.append(carry)
    return h_out


def _rglru_kernel(gb_ref, xb_ref, cw_ref, cb_ref, wg_ref, bg_ref, lam_ref, o_ref, a_s, u_s, h_s, hl_s, cp_s, *,
                  n_ctx):
    l_n = xb_ref.shape[0]
    n_lat = l_n - n_ctx
    x = xb_ref[...]
    row = lax.broadcasted_iota(jnp.int32, x.shape, 0)
    k_w = cw_ref.shape[0]
    xc = cb_ref[...] + sum(_seg_shift(x, row, j - (k_w - 1) // 2, n_ctx) * cw_ref[j:j + 1, :]
                           for j in range(k_w))
    xcb = xc.astype(BF16)
    n_p = x.shape[1] // LANES
    for d in (0, 1):
        gate = lambda g: _sigmoid(jnp.dot(xcb, wg_ref[d, g].astype(BF16), preferred_element_type=F32)
                                  + bg_ref[d, g:g + 1, :])
        log_a = -LR_C * gate(0) * _softplus(-lam_ref[d:d + 1, :])
        a = jnp.exp(log_a)
        u = jnp.sqrt(jnp.tanh(-log_a) * (jnp.exp(2.0 * log_a) + 1.0)) * gate(1) * xc
        h = [jnp.zeros((1, LANES), F32)] * n_p
        base = 0
        for row0, n in ((0, n_ctx), (n_ctx, n_lat)):
            if n:
                seg = n // 8
                for j in range(n_p):
                    for s in range(8):
                        src = slice(row0 + s * seg, row0 + (s + 1) * seg)
                        dst = slice(base + s * (seg + SEG_PAD), base + s * (seg + SEG_PAD) + seg)
                        a_s[j, dst, :] = a[src, j * LANES:(j + 1) * LANES]
                        u_s[j, dst, :] = u[src, j * LANES:(j + 1) * LANES]
                h = _lin_scan(a_s, u_s, hl_s, cp_s, h_s, base, row0, n, d == 1, h, d == 1)
                base += 8 * (seg + SEG_PAD)
    for j in range(n_p):
        cols = slice(j * LANES, (j + 1) * LANES)
        o_ref[:, cols] = (h_s[j, n_ctx:l_n, :] * _gelu_tanh(gb_ref[n_ctx:, cols])).astype(BF16)


def _rglru_layer(h, n_ctx, w_in, conv_w, conv_b, w_gate, b_gate, lam):
    b_n, l_n, d = h.shape
    n_lat = l_n - n_ctx
    nb = d // LR_BS
    proj = _mm2(h.reshape(b_n * l_n, d), w_in).reshape(b_n, l_n, 2 * d)
    k_w = conv_w.shape[0]
    o = pl.pallas_call(
        functools.partial(_rglru_kernel, n_ctx=n_ctx),
        grid=(b_n, nb),
        in_specs=[pl.BlockSpec((None, l_n, LR_BS), lambda b, j: (b, 0, j)),
                  pl.BlockSpec((None, l_n, LR_BS), lambda b, j: (b, 0, nb + j)),
                  pl.BlockSpec((k_w, LR_BS), lambda b, j: (0, j)),
                  pl.BlockSpec((1, LR_BS), lambda b, j: (0, j)),
                  pl.BlockSpec((2, 2, None, LR_BS, LR_BS), lambda b, j: (0, 0, j, 0, 0)),
                  pl.BlockSpec((2, 2, LR_BS), lambda b, j: (0, 0, j)),
                  pl.BlockSpec((2, LR_BS), lambda b, j: (0, j))],
        out_specs=pl.BlockSpec((None, n_lat, LR_BS), lambda b, j: (b, 0, j)),
        out_shape=jax.ShapeDtypeStruct((b_n, n_lat, d), BF16),
        scratch_shapes=[pltpu.VMEM((LR_BS // LANES, l_n + 16 * SEG_PAD, LANES), F32)] * 5,
        compiler_params=_cparams(("parallel", "parallel"), 56),
        name="rglru",
    )(proj, proj, conv_w, conv_b.reshape(1, d), w_gate, b_gate, lam)
    return o


def _ffn_up_kernel(h_ref, wg_ref, wv_ref, cg_ref, cv_ref, bg_ref, bv_ref, o_ref, u_scr, *, n_ctx, tr):
    l_n, tf = o_ref.shape
    k_w = cg_ref.shape[0]
    half = (k_w - 1) // 2
    n_buf, pad = u_scr.shape[0], (u_scr.shape[1] - l_n) // 2
    rblk = tr * _pick(l_n // tr, (3, 4, 2, 1))
    row = lax.broadcasted_iota(jnp.int32, (tr, LANES), 0)
    for p in range(n_buf):
        u_scr[p, 0:pad, :] = jnp.zeros((pad, 2 * LANES), F32)
        u_scr[p, pad + l_n:, :] = jnp.zeros((pad, 2 * LANES), F32)

    def weights(s):
        cols = slice(s * LANES, (s + 1) * LANES)
        return jnp.concatenate([wg_ref[:, cols], wv_ref[:, cols]], axis=1).astype(BF16)

    def product(s, w, r0):
        u_scr[s % n_buf, pad + r0:pad + r0 + rblk, :] = jnp.dot(h_ref[r0:r0 + rblk, :], w,
                                                                preferred_element_type=F32)

    def finish(s, r0):
        p, cols = s % n_buf, slice(s * LANES, (s + 1) * LANES)

        def conv(lane0, w_ref, b_ref):
            acc = None
            for j in range(k_w):
                sh = j - half
                x = u_scr[p, pad + r0 + sh:pad + r0 + sh + tr, lane0:lane0 + LANES]
                if sh < 0 and r0 in (0, n_ctx):
                    x = jnp.where(row < -sh, 0.0, x)
                if sh > 0 and r0 + tr in (n_ctx, l_n):
                    x = jnp.where(row >= tr - sh, 0.0, x)
                t = x * w_ref[j:j + 1, cols]
                acc = t if acc is None else acc + t
            return b_ref[:, cols] + acc
        gate = conv(0, cg_ref, bg_ref)
        val = conv(LANES, cv_ref, bv_ref)
        o_ref[r0:r0 + tr, cols] = (gate * _sigmoid(gate) * val).astype(BF16)

    n_s = tf // LANES
    for s in range(n_s + 1):
        w = weights(s) if s < n_s else None
        for r0 in range(0, l_n, rblk):
            if s < n_s:
                product(s, w, r0)
            if s > 0:
                for r1 in range(r0, r0 + rblk, tr):
                    finish(s - 1, r1)


def _conv_ffn(h, n_ctx, w_up, layer, conv_w, conv_b, *, tr):
    b_n, l_n, d = h.shape
    f = w_up.shape[2] // 2
    tf = _pick(f, (512, 256, 128))
    nf = f // tf
    k_w = conv_w.shape[0]
    cb = conv_b.reshape(1, 2 * f)
    act = pl.pallas_call(
        functools.partial(_ffn_up_kernel, n_ctx=n_ctx, tr=tr),
        grid=(b_n, nf),
        in_specs=[pl.BlockSpec((None, l_n, d), lambda b, j: (b, 0, 0), pipeline_mode=pl.Buffered(1)),
                  pl.BlockSpec((None, d, tf), lambda b, j: (layer, 0, j)),
                  pl.BlockSpec((None, d, tf), lambda b, j: (layer, 0, nf + j)),
                  pl.BlockSpec((k_w, tf), lambda b, j: (0, j)),
                  pl.BlockSpec((k_w, tf), lambda b, j: (0, nf + j)),
                  pl.BlockSpec((1, tf), lambda b, j: (0, j)),
                  pl.BlockSpec((1, tf), lambda b, j: (0, nf + j))],
        out_specs=pl.BlockSpec((None, l_n, tf), lambda b, j: (b, 0, j)),
        out_shape=jax.ShapeDtypeStruct((b_n, l_n, f), BF16),
        scratch_shapes=[pltpu.VMEM((3, l_n + 16, 2 * LANES), F32)],
        compiler_params=_cparams(("parallel", "arbitrary"), 56),
        name="ffn_up",
    )(h, w_up, w_up, conv_w, conv_w, cb, cb)
    return act


def kernel(x, c, ctx, c_ctx, ada_w, ada_b, ln_g, ln_b, ffn_w_up, ffn_conv_w, ffn_conv_b, ffn_w_down, rw_mu, rw_w_rkv, rw_w0, rw_w1, rw_w2, rw_a0, rw_a1, rw_a2, rw_g1, rw_g2, rw_k_k, rw_k_a, rw_r_k, rw_gn_g, rw_gn_b, rw_w_o, da_w_qkv, da_lambda, da_sub_g, da_w_o, hg_w_in, hg_lower, hg_norm_g, hg_w_o, lr_w_in, lr_conv_w, lr_conv_b, lr_w_gate, lr_b_gate, lr_lambda, lr_w_o):
    b_n, n_lat, d = x.shape
    n_ctx = ctx.shape[1]
    depth = ada_w.shape[0]
    assert depth == 4 and rw_mu.shape[0] == 1, "one occurrence of each of the four mixers"
    assert b_n + 1 <= 8 and n_ctx % CHUNK == 0 and n_lat % CHUNK == 0
    tr = math.gcd(math.gcd(n_ctx, n_lat), 256)
    alpha = (2 * depth) ** 0.25

    c8 = jnp.concatenate([c, c_ctx[None], jnp.zeros((8 - b_n - 1, d), F32)], axis=0)
    m = _ada(c8, ada_w, ada_b)
    m_lat = m[:, :b_n].reshape(depth, b_n, 1, 6, d)
    m_ctx = jnp.broadcast_to(m[:, b_n].reshape(depth, 1, 1, 6, d), (depth, b_n, 1, 6, d))
    mod = jnp.concatenate([m_ctx, m_lat], axis=2)

    z = jnp.concatenate([ctx, x], axis=1)
    w_down_b = _to_bf16(ffn_w_down)
    h = None
    for i in range(depth):
        last = i == depth - 1
        if i == 0:
            o = _rwkv7_layer(z, mod[0], n_ctx, rw_mu[0], rw_w_rkv[0], rw_w0[0], rw_w1[0], rw_w2[0], rw_a0[0],
                             rw_a1[0], rw_a2[0], rw_g1[0], rw_g2[0], rw_k_k[0], rw_k_a[0], rw_r_k[0],
                             rw_gn_g[0], rw_gn_b[0])
            w_o = rw_w_o
        elif i == 1:
            o = _diff_attention_layer(h, n_ctx, i, da_w_qkv[0], da_lambda[0], da_sub_g[0], tr=tr)
            w_o = da_w_o
        elif i == 2:
            o = _hgrn2_layer(h, n_ctx, i, hg_w_in[0], hg_lower, hg_norm_g[0])
            w_o = hg_w_o
        else:
            o = _rglru_layer(h, n_ctx, lr_w_in[0], lr_conv_w[0], lr_conv_b[0], lr_w_gate[0], lr_b_gate[0],
                             lr_lambda[0])
            w_o = lr_w_o
        z, h = _proj_ln(o, w_o, 0, z, mod[i], ln_g[i, 0], ln_b[i, 0], mod[i], gate_j=2, mod_j=3, tr=tr,
                        n_ctx=n_ctx, alpha=alpha)
        if last:
            n_ctx = 0
        act = _conv_ffn(h, n_ctx, ffn_w_up, i, ffn_conv_w[i], ffn_conv_b[i], tr=tr)
        z, h = _proj_ln(act, w_down_b, i, z, mod[i], ln_g[i, 1], ln_b[i, 1], mod[min(i + 1, depth - 1)],
                        gate_j=5, mod_j=None if last else 0, tr=tr, n_ctx=n_ctx, alpha=alpha)
    return z
```

```python
import functools
import math

import jax
import jax.numpy as jnp
from jax import lax
from jax.experimental import pallas as pl
from jax.experimental.pallas import tpu as pltpu

F32, BF16 = jnp.float32, jnp.bfloat16

LANES = 128
CHUNK = 64
LN_EPS = 1e-5
GRID_W = 64
ROPE_BASE = 10000.0
RW_HEAD = 64
RW_DECAY_SCALE = 0.606531
RW_GN_EPS = 64e-5
DA_HEAD = 128
HG_EXPAND = 128
LR_BS = 256
LR_C = 8.0
MIB = 1024 * 1024


def _pick(n, cands):
    for c in cands:
        if n % c == 0:
            return c
    return n


def _cparams(sem, vmem_mib):
    return pltpu.CompilerParams(dimension_semantics=sem, vmem_limit_bytes=vmem_mib * MIB)


def _sigmoid(x):
    return jax.nn.sigmoid(x)


NN = (((1,), (0,)), ((), ()))
NT = (((1,), (1,)), ((), ()))


def _parts(x, n):
    out = []
    for i in range(n):
        p = x.astype(BF16)
        out.append(p)
        if i + 1 < n:
            x = x - p.astype(F32)
    return out


def _mdot(ap, bp, dims=NN, order=2):
    pairs = [(a, b) for i, a in enumerate(ap) for j, b in enumerate(bp) if i + j < order]
    (ca,), (cb,) = dims[0]
    lhs = jnp.concatenate([a for a, _ in pairs], axis=ca) if len(pairs) > 1 else pairs[0][0]
    rhs = jnp.concatenate([b for _, b in pairs], axis=cb) if len(pairs) > 1 else pairs[0][1]
    return lax.dot_general(lhs, rhs, dims, preferred_element_type=F32)


def _cumsum_matrix(t_n, rev):
    r = lax.broadcasted_iota(jnp.int32, (t_n, 3 * t_n), 0)
    c = lax.broadcasted_iota(jnp.int32, (t_n, 3 * t_n), 1) % t_n
    return jnp.where((c >= r) if rev else (c <= r), 1.0, 0.0).astype(BF16)


def _cumsum(tri3_b, x):
    return jnp.dot(tri3_b, jnp.concatenate(_parts(x, 3), axis=0), preferred_element_type=F32)


def _cat_parts(xs, axis):
    return [jnp.concatenate(ps, axis=axis) for ps in zip(*xs)]


def _mm_kernel(a_ref, w_ref, o_ref, acc_ref, *, nk):
    if w_ref.dtype == BF16:
        prod = jnp.dot(a_ref[...], w_ref[...], preferred_element_type=F32)
    else:
        tk = w_ref.shape[0]
        n_kc = 2 if tk % (2 * LANES) == 0 else 1
        prod = None
        for c in range(n_kc):
            ks = slice(c * (tk // n_kc), (c + 1) * (tk // n_kc))
            t = jnp.dot(a_ref[:, ks], w_ref[ks, :].astype(BF16), preferred_element_type=F32)
            prod = t if prod is None else prod + t
    if nk == 1:
        o_ref[...] = prod.astype(o_ref.dtype)
    else:
        k = pl.program_id(3)

        @pl.when(k == 0)
        def _():
            acc_ref[...] = prod

        @pl.when(k > 0)
        def _():
            acc_ref[...] += prod

        @pl.when(k == nk - 1)
        def _():
            o_ref[...] = acc_ref[...].astype(o_ref.dtype)


def _mm_wres_kernel(a_ref, w_ref, o_ref, wb_ref):
    i = pl.program_id(2)
    k_n = w_ref.shape[0]
    n_kc = 4 if k_n % (4 * LANES) == 0 else 1

    @pl.when(i == 0)
    def _():
        kc = k_n // n_kc
        acc = None
        for c in range(n_kc):
            ks = slice(c * kc, (c + 1) * kc)
            wb = w_ref[ks, :].astype(BF16)
            wb_ref[ks, :] = wb
            t = jnp.dot(a_ref[:, ks], wb, preferred_element_type=F32)
            acc = t if acc is None else acc + t
        o_ref[...] = acc.astype(o_ref.dtype)

    @pl.when(i > 0)
    def _():
        o_ref[...] = jnp.dot(a_ref[...], wb_ref[...], preferred_element_type=F32).astype(o_ref.dtype)


def _matmul(a, w, *, out_dtype=F32, a_off=0):
    g_n, k_n, n_n = w.shape
    m_n = a.shape[1]
    tm = _pick(m_n, (1024, 512, 256, 128, 64))
    tn = _pick(n_n, (1024, 512, 256, 128))
    tk = k_n if k_n <= 2048 else _pick(k_n, (2816, 2048, 1024, 512))
    nk = k_n // tk
    if w.dtype == F32 and nk > 1:
        tn = _pick(n_n, (512, 256, 128))
    if w.dtype == F32 and nk == 1:
        return pl.pallas_call(
            _mm_wres_kernel,
            grid=(g_n, n_n // tn, m_n // tm),
            in_specs=[pl.BlockSpec((None, tm, k_n), lambda g, j, i: (g + a_off, i, 0)),
                      pl.BlockSpec((None, k_n, tn), lambda g, j, i: (g, 0, j))],
            out_specs=pl.BlockSpec((None, tm, tn), lambda g, j, i: (g, i, j)),
            out_shape=jax.ShapeDtypeStruct((g_n, m_n, n_n), out_dtype),
            scratch_shapes=[pltpu.VMEM((k_n, tn), BF16)],
            compiler_params=_cparams(("parallel", "parallel", "arbitrary"), 48),
            name="matmul_wres",
        )(a, w)
    return pl.pallas_call(
        functools.partial(_mm_kernel, nk=nk),
        grid=(g_n, m_n // tm, n_n // tn, nk),
        in_specs=[pl.BlockSpec((None, tm, tk), lambda g, i, j, k: (g + a_off, i, k)),
                  pl.BlockSpec((None, tk, tn), lambda g, i, j, k: (g, k, j))],
        out_specs=pl.BlockSpec((None, tm, tn), lambda g, i, j, k: (g, i, j)),
        out_shape=jax.ShapeDtypeStruct((g_n, m_n, n_n), out_dtype),
        scratch_shapes=[pltpu.VMEM((tm, tn), F32)],
        compiler_params=_cparams(("parallel", "parallel", "parallel", "arbitrary"), 48),
        name="matmul",
    )(a, w)


def _mm2(a, w, **kw):
    return _matmul(a[None], w[None], **kw)[0]


def _ada_kernel(c_ref, w_ref, b_ref, o_ref):
    c = c_ref[...]
    s = (c * _sigmoid(c)).astype(BF16)
    o_ref[...] = jnp.dot(s, w_ref[...].astype(BF16), preferred_element_type=F32) + b_ref[...]


def _ada(c8, ada_w, ada_b):
    depth, d, n = ada_w.shape
    tn = _pick(n, (1024, 512, 256, 128))
    return pl.pallas_call(
        _ada_kernel,
        grid=(depth, n // tn),
        in_specs=[pl.BlockSpec((8, d), lambda l, j: (0, 0)),
                  pl.BlockSpec((None, d, tn), lambda l, j: (l, 0, j)),
                  pl.BlockSpec((None, 1, tn), lambda l, j: (l, 0, j))],
        out_specs=pl.BlockSpec((None, 8, tn), lambda l, j: (l, 0, j)),
        out_shape=jax.ShapeDtypeStruct((depth, 8, n), F32),
        compiler_params=_cparams(("parallel", "parallel"), 40),
        name="ada",
    )(c8, ada_w, ada_b.reshape(depth, 1, n))


def _ln_mod_kernel(z_ref, y_ref, mod_ref, g_ref, b_ref, mod2_ref, *out_refs, gate_j, mod_j, alpha):
    m = mod_ref[...]
    zz = alpha * z_ref[...] + y_ref[...] * m[gate_j:gate_j + 1]
    mu = jnp.mean(zz, axis=-1, keepdims=True)
    zc = zz - mu
    var = jnp.mean(zc * zc, axis=-1, keepdims=True)
    zn = zc * lax.rsqrt(var + LN_EPS) * g_ref[...] + b_ref[...]
    out_refs[0][...] = zn
    if mod_j is not None:
        m2 = mod2_ref[...]
        out_refs[1][...] = (zn * (1 + m2[mod_j + 1:mod_j + 2]) + m2[mod_j:mod_j + 1]).astype(BF16)


def _ln_mod(z, y, mod, ln_g, ln_b, mod2, *, gate_j, mod_j, tr, n_ctx, alpha):
    b_n, l_z, d = z.shape
    l_y = y.shape[1]
    z_off = (l_z - l_y) // tr
    ncb = (n_ctx - (l_z - l_y)) // tr
    seg = lambda b, t: (b, jnp.where(t < ncb, 0, 1), 0, 0)
    row = pl.BlockSpec((None, tr, d), lambda b, t: (b, t, 0))
    out_shape = [jax.ShapeDtypeStruct((b_n, l_y, d), F32)]
    out_specs = [row]
    if mod_j is not None:
        out_shape.append(jax.ShapeDtypeStruct((b_n, l_y, d), BF16))
        out_specs.append(row)
    res = pl.pallas_call(
        functools.partial(_ln_mod_kernel, gate_j=gate_j, mod_j=mod_j, alpha=alpha),
        grid=(b_n, l_y // tr),
        in_specs=[pl.BlockSpec((None, tr, d), lambda b, t: (b, t + z_off, 0)),
                  row,
                  pl.BlockSpec((None, None, 6, d), seg),
                  pl.BlockSpec((1, d), lambda b, t: (0, 0)),
                  pl.BlockSpec((1, d), lambda b, t: (0, 0)),
                  pl.BlockSpec((None, None, 6, d), seg)],
        out_specs=out_specs,
        out_shape=out_shape,
        compiler_params=_cparams(("parallel", "parallel"), 40),
        name="ln_mod",
    )(z, y, mod, ln_g.reshape(1, d), ln_b.reshape(1, d), mod2)
    return res if mod_j is not None else (res[0], None)


def _cast_kernel(x_ref, o_ref):
    o_ref[...] = x_ref[...].astype(o_ref.dtype)


def _to_bf16(w):
    g_n, k_n, n_n = w.shape
    tk = _pick(k_n, (512, 256, 128))
    spec = pl.BlockSpec((None, tk, n_n), lambda g, i: (g, i, 0))
    return pl.pallas_call(
        _cast_kernel, grid=(g_n, k_n // tk), in_specs=[spec], out_specs=spec,
        out_shape=jax.ShapeDtypeStruct(w.shape, BF16),
        compiler_params=_cparams(("parallel", "parallel"), 32),
        name="to_bf16",
    )(w)


def _proj_ln_kernel(a_ref, w_ref, z_ref, mod_ref, g_ref, b_ref, mod2_ref, zo_ref, *ho_refs,
                    nk, gate_j, mod_j, alpha, tr, ncb):
    t, k = pl.program_id(1), pl.program_id(2)
    tm = zo_ref.shape[0]
    rb = tm // 2 if tm % 16 == 0 else tm

    def accumulate(first):
        for r0 in range(0, tm, rb):
            p = jnp.dot(a_ref[r0:r0 + rb, :], w_ref[...], preferred_element_type=F32)
            if first:
                zo_ref[r0:r0 + rb, :] = p
            else:
                zo_ref[r0:r0 + rb, :] += p

    rs = math.gcd(tr, 64)

    def finish():
        for sb in range(tm // rs):
            rows = slice(sb * rs, (sb + 1) * rs)
            is_ctx = t * (tm // tr) + (sb * rs) // tr < ncb
            m = jnp.where(is_ctx, mod_ref[0], mod_ref[1])
            zz = alpha * z_ref[rows, :] + zo_ref[rows, :] * m[gate_j:gate_j + 1]
            mu = jnp.mean(zz, axis=-1, keepdims=True)
            zc = zz - mu
            var = jnp.mean(zc * zc, axis=-1, keepdims=True)
            zn = zc * lax.rsqrt(var + LN_EPS) * g_ref[...] + b_ref[...]
            zo_ref[rows, :] = zn
            if mod_j is not None:
                m2 = jnp.where(is_ctx, mod2_ref[0], mod2_ref[1])
                ho_refs[0][rows, :] = (zn * (1 + m2[mod_j + 1:mod_j + 2]) + m2[mod_j:mod_j + 1]).astype(BF16)

    if nk == 1:
        accumulate(True)
        finish()
    else:
        @pl.when(k == 0)
        def _():
            accumulate(True)

        @pl.when(k > 0)
        def _():
            accumulate(False)

        @pl.when(k == nk - 1)
        def _():
            finish()


def _proj_ln(a, w, g, z, mod, ln_g, ln_b, mod2, *, gate_j, mod_j, tr, n_ctx, alpha):
    b_n, l_a, k_n = a.shape
    d = w.shape[2]
    if z.shape[1] != l_a:
        y = _matmul(a.reshape(1, b_n * l_a, k_n), w[g:g + 1])[0].reshape(b_n, l_a, d)
        return _ln_mod(z, y, mod, ln_g, ln_b, mod2, gate_j=gate_j, mod_j=mod_j, tr=tr, n_ctx=n_ctx, alpha=alpha)
    if w.dtype != BF16:
        w = _to_bf16(w)
    tm = tr * _pick(l_a // tr, (3, 2, 1))
    tk = k_n if k_n <= 2048 else _pick(k_n, (1408, 1024, 512, 256, 128))
    nk = k_n // tk
    row = pl.BlockSpec((None, tm, d), lambda b, t, k: (b, t, 0))
    seg = pl.BlockSpec((None, 2, 6, d), lambda b, t, k: (b, 0, 0, 0))
    vec = pl.BlockSpec((1, d), lambda b, t, k: (0, 0))
    out_shape = [jax.ShapeDtypeStruct((b_n, l_a, d), F32)]
    out_specs = [row]
    if mod_j is not None:
        out_shape.append(jax.ShapeDtypeStruct((b_n, l_a, d), BF16))
        out_specs.append(row)
    res = pl.pallas_call(
        functools.partial(_proj_ln_kernel, nk=nk, gate_j=gate_j, mod_j=mod_j, alpha=alpha, tr=tr,
                          ncb=n_ctx // tr),
        grid=(b_n, l_a // tm, nk),
        in_specs=[pl.BlockSpec((None, tm, tk), lambda b, t, k: (b, t, k)),
                  pl.BlockSpec((None, tk, d), lambda b, t, k: (g, k, 0),
                               pipeline_mode=pl.Buffered(1) if nk == 1 else None),
                  row, seg, vec, vec, seg],
        out_specs=out_specs,
        out_shape=out_shape,
        compiler_params=_cparams(("parallel", "parallel", "arbitrary"), 56),
        name="proj_ln",
    )(a, w, z, mod, ln_g.reshape(1, d), ln_b.reshape(1, d), mod2)
    return res if mod_j is not None else (res[0], None)


def _seg_shift(x, row, shift, n_ctx):
    l_n = x.shape[0]
    rolled = pltpu.roll(x, (-shift) % l_n, 0)
    src = row + shift
    same_seg = (src >= 0) & (src < l_n) & ((src < n_ctx) == (row < n_ctx))
    return jnp.where(same_seg, rolled, 0.0)


def _rw_mix_kernel(z_ref, mod_ref, mu_ref, o_ref, *, n_ctx):
    z = z_ref[...]
    row = lax.broadcasted_iota(jnp.int32, z.shape, 0)
    is_ctx = row < n_ctx
    shift = jnp.where(is_ctx, mod_ref[0, 0:1, :], mod_ref[1, 0:1, :])
    scale = jnp.where(is_ctx, mod_ref[0, 1:2, :], mod_ref[1, 1:2, :])
    h = z * (1 + scale) + shift
    dx = 0.5 * (_seg_shift(h, row, -1, n_ctx) + _seg_shift(h, row, 1, n_ctx)) - h
    for n in range(6):
        o_ref[n] = (h + dx * mu_ref[n:n + 1, :]).astype(BF16)


def _rw_mix(z, mod, mu, *, n_ctx):
    b_n, l_n, d = z.shape
    tc = _pick(d, (256, 128))
    return pl.pallas_call(
        functools.partial(_rw_mix_kernel, n_ctx=n_ctx),
        grid=(b_n, d // tc),
        in_specs=[pl.BlockSpec((None, l_n, tc), lambda b, j: (b, 0, j)),
                  pl.BlockSpec((None, 2, 6, tc), lambda b, j: (b, 0, 0, j)),
                  pl.BlockSpec((6, tc), lambda b, j: (0, j))],
        out_specs=pl.BlockSpec((6, None, l_n, tc), lambda b, j: (0, b, 0, j)),
        out_shape=jax.ShapeDtypeStruct((6, b_n, l_n, d), BF16),
        compiler_params=_cparams(("parallel", "parallel"), 48),
        name="rw_mix",
    )(z, mod, mu)


def _lora_kernel(x_ref, a_ref, b_ref, o_ref, *, act):
    t = jnp.dot(x_ref[...], a_ref[...], preferred_element_type=F32)
    if act == "tanh":
        t = jnp.tanh(t)
    elif act == "sigmoid":
        t = _sigmoid(t)
    o_ref[...] = jnp.dot(t.astype(BF16), b_ref[...], preferred_element_type=F32)


def _lora(xs, x_idx, a, b, act):
    g_n, d, r = a.shape
    m_n = xs.shape[1]
    tm = _pick(m_n, (512, 256, 128, 64))
    return pl.pallas_call(
        functools.partial(_lora_kernel, act=act),
        grid=(g_n, m_n // tm),
        in_specs=[pl.BlockSpec((None, tm, d), lambda g, i: (x_idx, i, 0)),
                  pl.BlockSpec((None, d, r), lambda g, i: (g, 0, 0)),
                  pl.BlockSpec((None, r, d), lambda g, i: (g, 0, 0))],
        out_specs=pl.BlockSpec((None, tm, d), lambda g, i: (g, i, 0)),
        out_shape=jax.ShapeDtypeStruct((g_n, m_n, d), F32),
        compiler_params=_cparams(("parallel", "parallel"), 40),
        name="lora",
    )(xs, a, b)


def _chunk_of(q, ncc, nc, rev):
    if not rev:
        return q
    return jnp.where(q < ncc, ncc - 1 - q, nc - 1 - (q - ncc))


def _rwkv_kernel(r_ref, k_ref, v_ref, lw_ref, la_ref, g_ref, w0_ref, a0_ref, kk_ref, ka_ref, rk_ref,
                 gng_ref, gnb_ref, o_ref,
                 y_scr, mr_s, n_s, *, n_ctx):
    t_n = CHUNK
    h2 = 2 * t_n
    l_n = r_ref.shape[0]
    nc, ncc = l_n // t_n, n_ctx // t_n
    group = _pick(nc, (6, 4, 3, 2, 1))
    lane = lax.broadcasted_iota(jnp.int32, (1, LANES), 1)
    m1 = jnp.where(lane < RW_HEAD, 1.0, 0.0)
    m2 = 1.0 - m1
    ri = lax.broadcasted_iota(jnp.int32, (LANES, LANES), 0)
    ci = lax.broadcasted_iota(jnp.int32, (LANES, LANES), 1)
    same_head = (ri // RW_HEAD) == (ci // RW_HEAD)
    gsum_b = jnp.where(same_head, 1.0, 0.0).astype(BF16)
    gavg_b = jnp.where(same_head, 1.0 / RW_HEAD, 0.0).astype(BF16)
    eye = jnp.where(ri == ci, 1.0, 0.0)
    tr_i, tc_i = ri % t_n, ci % t_n
    k_k, k_a = kk_ref[...], ka_ref[...]

    def stack(x):
        return jnp.concatenate([x * m1, x * m2], axis=0)

    def rows_of(c):
        return pl.ds(pl.multiple_of(c * t_n, t_n), t_n)

    def head_sum(x, w_b):
        return _mdot(_parts(x, 3), [w_b], order=3)

    tri3_b = [_cumsum_matrix(t_n, rev) for rev in (False, True)]
    strict = [(tc_i > tr_i) if rev else (tc_i < tr_i) for rev in (False, True)]
    incl = [(tc_i >= tr_i) if rev else (tc_i <= tr_i) for rev in (False, True)]

    def stage_prep(d, c):
        rows = rows_of(c)
        k, r, v = k_ref[rows, :], r_ref[rows, :], v_ref[rows, :]
        kkr = k * k_k
        both = dict(kkr=kkr, ss=head_sum(kkr * kkr, gsum_b))
        lw = -RW_DECAY_SCALE * _sigmoid(w0_ref[d:d + 1, :] + lw_ref[d, rows, :])
        a = _sigmoid(a0_ref[d:d + 1, :] + la_ref[d, rows, :])
        return dict(d=d, c=c, k=k, r=r, v=v, lw=lw, a=a, both=both, cum=_cumsum(tri3_b[d], lw))

    def stage_amat(s):
        d, cum, lw, a, both = s["d"], s["cum"], s["lw"], s["a"], s["both"]
        if "kk" not in both:
            both["kk"] = both["kkr"] * lax.rsqrt(both["ss"] + 1e-12)
            both["vp"] = _parts(stack(s["v"]), 1)
        kk = both["kk"]
        kd = s["k"] * (1 + (a - 1) * k_a)
        bv = kk * a
        p_end = cum[0:1, :] if d == 1 else cum[t_n - 1:t_n, :]
        e_m = jnp.exp(-cum)
        e_h = jnp.exp(p_end - cum)
        ktp = _parts(stack(kk * jnp.exp(cum - lw)), 2)
        rt = stack(s["r"] * jnp.exp(cum))
        k2p = _cat_parts([_parts(stack(bv * e_m), 2), _parts(stack(kd * e_m), 2)], 0)
        return dict(d=d, c=s["c"], ktp=ktp, rt=rt, vp=both["vp"], p_end=p_end,
                    bh=stack(bv * e_h), kh=stack(kd * e_h),
                    amat=_mdot(_cat_parts([ktp, _parts(rt, 2)], 0), k2p, NT))

    def stage_square(s):
        d, amat = s["d"], s["amat"]
        lt = jnp.where(strict[d], amat[:h2, :h2], 0.0).T
        ltp = _parts(lt, 2)
        msk = jnp.concatenate([jnp.where(strict[d], amat[:h2, h2:], 0.0),
                               jnp.where(incl[d], amat[h2:, h2:], 0.0)], axis=0)
        s = dict(s, pt=eye - lt, xt=_mdot(ltp, ltp),
                 av=_mdot(_parts(msk, 2), s["vp"][:1]),
                 arbp=_parts(jnp.where(incl[d], amat[h2:, :h2], 0.0), 2))
        del s["amat"]
        return s

    def stage_double(s, final):
        xh = _parts(s["xt"], 1)
        ptp = _parts(s["pt"], 2)
        if final:
            return dict(s, pt=s["pt"] + _mdot(xh, ptp))
        rhs = [jnp.concatenate([ptp[0], xh[0]], axis=1), jnp.concatenate([ptp[1], jnp.zeros_like(xh[0])], axis=1)]
        both = _mdot(xh, rhs)
        return dict(s, pt=s["pt"] + both[:, :LANES], xt=both[:, LANES:])

    def stage_solve(s):
        rhs = jnp.concatenate([s["ktp"][0], (-s["av"][:h2]).astype(BF16)], axis=1)
        return dict(s, wub=_mdot(_parts(s["pt"].T, 2), [rhs]).astype(BF16))

    def stage_fold(s):
        d, c, wub = s["d"], s["c"], s["wub"]
        aw = _mdot(s["arbp"], [wub])
        zb = jnp.zeros((h2, LANES), BF16)
        lhs = _cat_parts([_parts(s["bh"].T, 2), _parts(s["kh"].T, 2)], 1)
        rhs = jnp.concatenate([wub, jnp.concatenate([zb, s["vp"][0]], axis=1)], axis=0)
        mn = _mdot(lhs, [rhs])
        dg = jnp.where(ri == ci, jnp.broadcast_to(jnp.exp(s["p_end"]), (LANES, LANES)), 0.0)
        rp = s["rt"] - aw[:, :LANES]
        mrp = _parts(jnp.concatenate([dg - mn[:, :LANES], rp[:t_n] + rp[t_n:]], axis=0), 2)
        for i in range(2):
            mr_s[d, c, i] = mrp[i]
        n_s[d, c] = mn[:, LANES:]
        y0 = s["av"][h2:] + aw[:, LANES:]
        return y0[:t_n] + y0[t_n:]

    def seq(q, hs):
        cs = (q, _chunk_of(q, ncc, nc, True))
        mh = [_mdot([mr_s[d, cs[d], 0], mr_s[d, cs[d], 1]], _parts(hs[d], 2)) for d in (0, 1)]
        for d in (0, 1):
            y_scr[rows_of(cs[d]), :] += mh[d][h2:]
        return tuple(mh[d][:h2] + n_s[d, cs[d]] for d in (0, 1))

    def local(i, hs, with_seq):
        steps = [i * group + g for g in range(group)]
        pending = [q - group for q in steps] if with_seq else []
        sts = [stage_prep(d, q if d == 0 else _chunk_of(q, ncc, nc, True)) for q in steps for d in (0, 1)]
        stages = ([stage_amat, stage_square] + [functools.partial(stage_double, final=f) for f in (False,) * 4 + (True,)]
                  + [stage_solve])
        for stage in stages:
            sts = [stage(s) for s in sts]
            if pending:
                hs = seq(pending.pop(0), hs)
        for s in sts:
            y0 = stage_fold(s)
            y_scr[rows_of(s["c"]), :] += y0
        while pending:
            hs = seq(pending.pop(0), hs)
        return hs

    y_scr[...] = jnp.zeros(y_scr.shape, F32)
    zero = jnp.zeros((LANES, LANES), F32)
    n_trip = nc // group
    hs = local(0, (zero, zero), False)
    hs = lax.fori_loop(1, n_trip, functools.partial(local, with_seq=True), hs)
    for q in range((n_trip - 1) * group, nc):
        hs = seq(q, hs)

    n_post = _pick(nc, (4, 3, 2, 1))

    def post(i, carry):
        rows = [rows_of(i * n_post + g) for g in range(n_post)]

        def bonus_sum(rw):
            k, r = k_ref[rw, :], r_ref[rw, :]
            kd_f = k * (1 + (_sigmoid(a0_ref[0:1, :] + la_ref[0, rw, :]) - 1) * k_a)
            kd_b = k * (1 + (_sigmoid(a0_ref[1:2, :] + la_ref[1, rw, :]) - 1) * k_a)
            return head_sum(r * (kd_f + kd_b) * rk_ref[...], gsum_b)
        bsum = [bonus_sum(rw) for rw in rows]
        ys = [y_scr[rw, :] for rw in rows]
        ycs = [y - m for y, m in zip(ys, [head_sum(y, gavg_b) for y in ys])]
        var = [head_sum(yc * yc, gavg_b) for yc in ycs]
        for rw, yc, vr, bs in zip(rows, ycs, var, bsum):
            yn = yc * lax.rsqrt(vr + RW_GN_EPS) * gng_ref[...] + gnb_ref[...]
            o_ref[rw, :] = ((yn + bs * v_ref[rw, :]) * g_ref[rw, :]).astype(BF16)
        return carry

    lax.fori_loop(0, nc // n_post, post, 0)


def _rwkv_scan(rkv, lw, la, g, w0, a0, k_k, k_a, r_k, gn_g, gn_b, *, n_ctx):
    _, b_n, l_n, d = rkv.shape
    nc = l_n // CHUNK
    col = lambda n: pl.BlockSpec((None, None, l_n, LANES), lambda b, p, n=n: (n, b, 0, p))
    two = pl.BlockSpec((2, None, l_n, LANES), lambda b, p: (0, b, 0, p))
    par = lambda rows: pl.BlockSpec((rows, LANES), lambda b, p: (0, p))
    return pl.pallas_call(
        functools.partial(_rwkv_kernel, n_ctx=n_ctx),
        grid=(b_n, d // LANES),
        in_specs=[col(0), col(1), col(2), two, two,
                  pl.BlockSpec((None, l_n, LANES), lambda b, p: (b, 0, p)),
                  par(2), par(2), par(1), par(1), par(1), par(1), par(1)],
        out_specs=pl.BlockSpec((None, l_n, LANES), lambda b, p: (b, 0, p)),
        out_shape=jax.ShapeDtypeStruct((b_n, l_n, d), BF16),
        scratch_shapes=[pltpu.VMEM((l_n, LANES), F32),
                        pltpu.VMEM((2, nc, 2, LANES + CHUNK, LANES), BF16),
                        pltpu.VMEM((2, nc, LANES, LANES), F32)],
        compiler_params=_cparams(("parallel", "parallel"), 56),
        name="rwkv_scan",
    )(rkv, rkv, rkv, lw, la, g, w0, a0, k_k.reshape(1, d), k_a.reshape(1, d), r_k.reshape(1, d),
      gn_g.reshape(1, d), gn_b.reshape(1, d))


def _pad_axis(w, axis, to):
    pad = [(0, 0)] * w.ndim
    pad[axis] = (0, to - w.shape[axis])
    return jnp.pad(w, pad)


def _rwkv7_layer(z, mod, n_ctx, mu, w_rkv, w0, w1, w2, a0, a1, a2, g1, g2, k_k, k_a, r_k, gn_g, gn_b):
    b_n, l_n, d = z.shape
    m_n = b_n * l_n
    xs = _rw_mix(z, mod, mu, n_ctx=n_ctx).reshape(6, m_n, d)
    rkv = _matmul(xs, w_rkv)
    r_w = -(-w1.shape[-1] // LANES) * LANES
    r_a = -(-a1.shape[-1] // LANES) * LANES
    lw = _lora(xs, 3, _pad_axis(w1, 2, r_w).astype(BF16), _pad_axis(w2, 1, r_w).astype(BF16), "tanh")
    la = _lora(xs, 4, _pad_axis(a1, 2, r_a).astype(BF16), _pad_axis(a2, 1, r_a).astype(BF16), None)
    gate = _lora(xs, 5, g1[None].astype(BF16), g2[None].astype(BF16), "sigmoid")
    o = _rwkv_scan(rkv.reshape(3, b_n, l_n, d), lw.reshape(2, b_n, l_n, d), la.reshape(2, b_n, l_n, d),
                   gate.reshape(b_n, l_n, d), w0, a0, k_k, k_a, r_k, gn_g, gn_b, n_ctx=n_ctx)
    return o


def _qkv_rope_kernel(a_ref, w_ref, cos_ref, sa_ref, sb_ref, o_ref, wb_ref, *, n_q, n_qk):
    j, i = pl.program_id(0), pl.program_id(1)

    @pl.when(i == 0)
    def _():
        wb_ref[...] = w_ref[...].astype(BF16)

    x = jnp.dot(a_ref[...], wb_ref[...], preferred_element_type=F32)

    @pl.when(j < n_qk)
    def _():
        q = DA_HEAD // 4
        q_scale = jnp.where(j < n_q, DA_HEAD ** -0.5 * math.log2(math.e), 1.0)
        cos, s_a, s_b = cos_ref[...] * q_scale, sa_ref[...] * q_scale, sb_ref[...] * q_scale
        for s in range(x.shape[1] // DA_HEAD):
            cols = slice(s * DA_HEAD, (s + 1) * DA_HEAD)
            xs = x[:, cols]
            rot = xs * cos + pltpu.roll(xs, DA_HEAD - q, 1) * s_a + pltpu.roll(xs, q, 1) * s_b
            o_ref[:, cols] = rot.astype(BF16)

    @pl.when(j >= n_qk)
    def _():
        o_ref[...] = x.astype(BF16)


def _qkv_rope(h, w_qkv, cos, s_a, s_b):
    b_n, l_n, d = h.shape
    d3 = w_qkv.shape[1]
    tm = _pick(l_n, (768, 1024, 512, 256, 128, 64))
    tn = _pick(d, (1024, 512, 256, 128))
    n_t = l_n // tm
    tab = pl.BlockSpec((tm, DA_HEAD), lambda j, i: (i % n_t, 0))
    out = pl.pallas_call(
        functools.partial(_qkv_rope_kernel, n_q=d // tn, n_qk=2 * d // tn),
        grid=(d3 // tn, b_n * n_t),
        in_specs=[pl.BlockSpec((tm, d), lambda j, i: (i, 0)),
                  pl.BlockSpec((d, tn), lambda j, i: (0, j)), tab, tab, tab],
        out_specs=pl.BlockSpec((tm, tn), lambda j, i: (i, j)),
        out_shape=jax.ShapeDtypeStruct((b_n * l_n, d3), BF16),
        scratch_shapes=[pltpu.VMEM((d, tn), BF16)],
        compiler_params=_cparams(("parallel", "arbitrary"), 48),
        name="qkv_rope",
    )(h.reshape(b_n * l_n, d), w_qkv, cos, s_a, s_b)
    return out.reshape(b_n, l_n, d3)


def _attn_kernel(q_ref, k_ref, v_ref, lam_ref, sg_ref, o_ref, *, ncb, n_ctx, lam_init):
    qi = pl.program_id(2)
    lv = lam_ref[...]
    lam = (jnp.exp(jnp.sum(lv[0:1] * lv[1:2], axis=-1, keepdims=True))
           - jnp.exp(jnp.sum(lv[2:3] * lv[3:4], axis=-1, keepdims=True)) + lam_init)

    hw = 2 * DA_HEAD
    n_hh = o_ref.shape[1] // hw

    def attend(nk):
        def scores(hh, m):
            cols = slice(hh * hw + m * DA_HEAD, hh * hw + (m + 1) * DA_HEAD)
            return lax.dot_general(q_ref[:, cols], k_ref[0:nk, cols], NT, preferred_element_type=F32)
        s_all = [[scores(hh, m) for m in (0, 1)] for hh in range(n_hh)]
        for hh in range(n_hh):
            def probs(s):
                e = jnp.exp2(s - jnp.max(s, axis=-1, keepdims=True))
                return e, 1.0 / jnp.sum(e, axis=-1, keepdims=True)
            e0, i0 = probs(s_all[hh][0])
            e1, i1 = probs(s_all[hh][1])
            v = v_ref[0:nk, hh * hw:(hh + 1) * hw]
            o = (jnp.dot(e0.astype(BF16), v, preferred_element_type=F32) * i0
                 - jnp.dot(e1.astype(BF16), v, preferred_element_type=F32) * (lam * i1))
            o = o * lax.rsqrt(jnp.mean(o * o, axis=-1, keepdims=True) + 1e-5) * sg_ref[...] * (1 - lam_init)
            o_ref[:, hh * hw:(hh + 1) * hw] = o.astype(BF16)

    if ncb > 0:
        @pl.when(qi < ncb)
        def _():
            attend(n_ctx)

    @pl.when(qi >= ncb)
    def _():
        attend(k_ref.shape[0])


def _attention(qkv, lam_vec, sub_g, *, tq, n_ctx, lam_init):
    b_n, l_n, d3 = qkv.shape
    d = d3 // 3
    hw = 2 * DA_HEAD
    n_hh = 2 if (d // hw) % 2 == 0 else 1
    bw = n_hh * hw
    nh = d // bw
    return pl.pallas_call(
        functools.partial(_attn_kernel, ncb=n_ctx // tq, n_ctx=n_ctx, lam_init=lam_init),
        grid=(b_n, nh, l_n // tq),
        in_specs=[pl.BlockSpec((None, tq, bw), lambda b, h, t: (b, t, h)),
                  pl.BlockSpec((None, l_n, bw), lambda b, h, t: (b, 0, nh + h)),
                  pl.BlockSpec((None, l_n, bw), lambda b, h, t: (b, 0, 2 * nh + h)),
                  pl.BlockSpec((4, DA_HEAD), lambda b, h, t: (0, 0)),
                  pl.BlockSpec((1, hw), lambda b, h, t: (0, 0))],
        out_specs=pl.BlockSpec((None, tq, bw), lambda b, h, t: (b, t, h)),
        out_shape=jax.ShapeDtypeStruct((b_n, l_n, d), BF16),
        compiler_params=_cparams(("parallel", "parallel", "arbitrary"), 48),
        name="diff_attn",
    )(qkv, qkv, qkv, lam_vec, sub_g.reshape(1, hw))


def _rope_tables(n_ctx, n_lat):
    n_rows = n_lat // GRID_W
    row = jnp.repeat(jnp.arange(n_rows, dtype=F32), GRID_W)
    col = jnp.tile(jnp.arange(GRID_W, dtype=F32), n_rows)
    nf = DA_HEAD // 4
    inv_freq = ROPE_BASE ** (-jnp.arange(nf, dtype=F32) / nf)
    ang_r, ang_c = row[:, None] * inv_freq, col[:, None] * inv_freq
    ang = jnp.concatenate([ang_r, ang_r, ang_c, ang_c], axis=-1)
    ang = jnp.concatenate([jnp.zeros((n_ctx, DA_HEAD), F32), ang], axis=0)
    cos, sin = jnp.cos(ang), jnp.sin(ang)
    even_q = (jnp.arange(DA_HEAD) // nf) % 2 == 0
    return cos, jnp.where(even_q, -sin, 0.0), jnp.where(even_q, 0.0, sin)


def _diff_attention_layer(h, n_ctx, layer_idx, w_qkv, lam_vec, sub_g, *, tr):
    b_n, l_n, d = h.shape
    cos, s_a, s_b = _rope_tables(n_ctx, l_n - n_ctx)
    qkv = _qkv_rope(h, w_qkv, cos, s_a, s_b)
    lam_init = 0.8 - 0.6 * math.exp(-0.3 * layer_idx)
    o = _attention(qkv, lam_vec, sub_g, tq=tr, n_ctx=n_ctx, lam_init=lam_init)
    return o


def _hgrn_kernel(q_ref, i_ref, g_ref, ff_ref, fb_ref, low_ref, ng_ref, o_ref, o_scr, *, n_ctx, layer_idx):
    t_n = CHUNK
    l_n = q_ref.shape[0]
    nc, ncc = l_n // t_n, n_ctx // t_n
    r64 = lax.broadcasted_iota(jnp.int32, (t_n, t_n), 0)
    c64 = lax.broadcasted_iota(jnp.int32, (t_n, t_n), 1)

    def rows_of(c):
        return pl.ds(pl.multiple_of(c * t_n, t_n), t_n)

    f_refs = (ff_ref, fb_ref)
    lbs, incl = [], []
    tri3_b = [_cumsum_matrix(t_n, rev) for rev in (False, True)]
    for d in (0, 1):
        low = low_ref[d]
        e = jnp.exp(low - jnp.max(low, axis=0, keepdims=True))
        sm = e / jnp.sum(e, axis=0, keepdims=True)
        cs = sm[0:1]
        for rr in range(1, layer_idx + 1):
            cs = cs + sm[rr:rr + 1]
        lbs.append(cs - sm[0:1])
        incl.append((c64 >= r64) if d == 1 else (c64 <= r64))
    group = _pick(nc, (4, 2, 1))

    def stage_cum(d, c):
        rows = rows_of(c)
        f = lbs[d] + (1.0 - lbs[d]) * _sigmoid(f_refs[d][rows, :])
        return dict(d=d, rows=rows, f=f, cum=_cumsum(tri3_b[d], jnp.log(f)))

    def stage_att(s):
        d, cum, rows = s["d"], s["cum"], s["rows"]
        b_end = cum[0:1, :] if d == 1 else cum[t_n - 1:t_n, :]
        qv = q_ref[rows, :]
        qd = (qv * _sigmoid(qv) * jnp.exp(cum)).astype(BF16)
        kk = 1.0 - s["f"]
        v = i_ref[rows, :]
        kd = (kk * jnp.exp(-cum)).astype(BF16)
        ke = (kk * jnp.exp(b_end - cum)).astype(BF16)
        return dict(d=d, rows=rows, qd=qd, vb=v.astype(BF16), dec=jnp.exp(b_end),
                    att=lax.dot_general(qd, kd, NT, preferred_element_type=F32),
                    upd=jnp.dot(v.T.astype(BF16), ke, preferred_element_type=F32))

    def stage_intra(s):
        att = jnp.where(incl[s["d"]], s["att"], 0.0).astype(BF16)
        return dict(s, o=jnp.dot(att, s["vb"], preferred_element_type=F32))

    def body(i, states):
        items = [(d, _chunk_of(i * group + g, ncc, nc, d == 1)) for g in range(group) for d in (0, 1)]
        sts = [stage_cum(d, c) for d, c in items]
        sts = [stage_att(s) for s in sts]
        sts = [stage_intra(s) for s in sts]
        states = list(states)
        for s in sts:
            d = s["d"]
            o = s["o"] + lax.dot_general(s["qd"], states[d].astype(BF16), NT, preferred_element_type=F32)
            o_scr[d, s["rows"], :] = o
            states[d] = states[d] * s["dec"] + s["upd"]
        return tuple(states)

    zero = jnp.zeros((LANES, LANES), F32)
    lax.fori_loop(0, nc // group, body, (zero, zero))

    p_n = t_n * _pick(nc, (4, 3, 2, 1))

    def post(c, carry):
        rows = pl.ds(pl.multiple_of(c * p_n, p_n), p_n)
        o = o_scr[0, rows, :] + o_scr[1, rows, :]
        o = o * lax.rsqrt(jnp.mean(o * o, axis=-1, keepdims=True) + 1e-5) * ng_ref[...]
        gv = g_ref[rows, :]
        o_ref[rows, :] = (o * (gv * _sigmoid(gv))).astype(BF16)
        return carry

    lax.fori_loop(0, l_n // p_n, post, 0)


def _hgrn2_layer(h, n_ctx, layer_idx, w_in, lower, norm_g):
    b_n, l_n, d = h.shape
    m_n = b_n * l_n
    nh = d // HG_EXPAND
    proj = _mm2(h.reshape(m_n, d), w_in).reshape(b_n, l_n, 5 * d)
    col = lambda n: pl.BlockSpec((None, l_n, LANES), lambda b, p, n=n: (b, 0, n * nh + p))
    o = pl.pallas_call(
        functools.partial(_hgrn_kernel, n_ctx=n_ctx, layer_idx=layer_idx),
        grid=(b_n, nh),
        in_specs=[col(0), col(1), col(2), col(3), col(4),
                  pl.BlockSpec((2, lower.shape[1], LANES), lambda b, p: (0, 0, p)),
                  pl.BlockSpec((1, LANES), lambda b, p: (0, 0))],
        out_specs=pl.BlockSpec((None, l_n, LANES), lambda b, p: (b, 0, p)),
        out_shape=jax.ShapeDtypeStruct((b_n, l_n, d), BF16),
        scratch_shapes=[pltpu.VMEM((2, l_n, LANES), F32)],
        compiler_params=_cparams(("parallel", "parallel"), 40),
        name="hgrn_scan",
    )(proj, proj, proj, proj, proj, lower, norm_g.reshape(1, LANES))
    return o


def _gelu_tanh(x):
    return 0.5 * x * (1.0 + jnp.tanh(math.sqrt(2.0 / math.pi) * (x + 0.044715 * (x * x * x))))


def _softplus(x):
    return jnp.maximum(x, 0.0) + jnp.log1p(jnp.exp(-jnp.abs(x)))


SEG_PAD = 8


def _lin_scan(a_ref, u_ref, hl_s, cp_s, h_s, base, row0, n, rev, h_in, accumulate):
    seg = n // 8
    stride = seg + SEG_PAD
    n_p = a_ref.shape[0]

    def step(i, carry):
        t = (seg - 1 - i) if rev else i
        idx = pl.ds(base + t, 8, stride=stride)
        out = []
        for j in range(n_p):
            hl, cp = carry[j]
            a = a_ref[j, idx, :]
            hl = a * hl + u_ref[j, idx, :]
            cp = a * cp
            hl_s[j, idx, :] = hl
            cp_s[j, idx, :] = cp
            out.append((hl, cp))
        return tuple(out)

    init = tuple((jnp.zeros((8, LANES), F32), jnp.ones((8, LANES), F32)) for _ in range(n_p))
    ends = lax.fori_loop(0, seg, step, init)
    order = range(7, -1, -1) if rev else range(8)
    h_out = []
    for j in range(n_p):
        hl_e, cp_e = ends[j]
        carry = h_in[j]
        for s in order:
            r0, p0 = row0 + s * seg, base + s * stride
            blk = hl_s[j, p0:p0 + seg, :] + cp_s[j, p0:p0 + seg, :] * carry
            if accumulate:
                h_s[j, r0:r0 + seg, :] += blk
            else:
                h_s[j, r0:r0 + seg, :] = blk
            carry = hl_e[s:s + 1, :] + cp_e[s:s + 1, :] * carry
        h_out.append(carry)
    return h_out


def _rglru_kernel(gb_ref, xb_ref, cw_ref, cb_ref, wg_ref, bg_ref, lam_ref, o_ref, a_s, u_s, h_s, hl_s, cp_s, *,
                  n_ctx):
    l_n = xb_ref.shape[0]
    n_lat = l_n - n_ctx
    x = xb_ref[...]
    row = lax.broadcasted_iota(jnp.int32, x.shape, 0)
    k_w = cw_ref.shape[0]
    xc = cb_ref[...] + sum(_seg_shift(x, row, j - (k_w - 1) // 2, n_ctx) * cw_ref[j:j + 1, :]
                           for j in range(k_w))
    xcb = xc.astype(BF16)
    n_p = x.shape[1] // LANES
    for d in (0, 1):
        gate = lambda g: _sigmoid(jnp.dot(xcb, wg_ref[d, g].astype(BF16), preferred_element_type=F32)
                                  + bg_ref[d, g:g + 1, :])
        log_a = -LR_C * gate(0) * _softplus(-lam_ref[d:d + 1, :])
        a = jnp.exp(log_a)
        u = jnp.sqrt(jnp.tanh(-log_a) * (jnp.exp(2.0 * log_a) + 1.0)) * gate(1) * xc
        h = [jnp.zeros((1, LANES), F32)] * n_p
        base = 0
        for row0, n in ((0, n_ctx), (n_ctx, n_lat)):
            if n:
                seg = n // 8
                for j in range(n_p):
                    for s in range(8):
                        src = slice(row0 + s * seg, row0 + (s + 1) * seg)
                        dst = slice(base + s * (seg + SEG_PAD), base + s * (seg + SEG_PAD) + seg)
                        a_s[j, dst, :] = a[src, j * LANES:(j + 1) * LANES]
                        u_s[j, dst, :] = u[src, j * LANES:(j + 1) * LANES]
                h = _lin_scan(a_s, u_s, hl_s, cp_s, h_s, base, row0, n, d == 1, h, d == 1)
                base += 8 * (seg + SEG_PAD)
    for j in range(n_p):
        cols = slice(j * LANES, (j + 1) * LANES)
        o_ref[:, cols] = (h_s[j, n_ctx:l_n, :] * _gelu_tanh(gb_ref[n_ctx:, cols])).astype(BF16)


def _rglru_layer(h, n_ctx, w_in, conv_w, conv_b, w_gate, b_gate, lam):
    b_n, l_n, d = h.shape
    n_lat = l_n - n_ctx
    nb = d // LR_BS
    proj = _mm2(h.reshape(b_n * l_n, d), w_in).reshape(b_n, l_n, 2 * d)
    k_w = conv_w.shape[0]
    o = pl.pallas_call(
        functools.partial(_rglru_kernel, n_ctx=n_ctx),
        grid=(b_n, nb),
        in_specs=[pl.BlockSpec((None, l_n, LR_BS), lambda b, j: (b, 0, j)),
                  pl.BlockSpec((None, l_n, LR_BS), lambda b, j: (b, 0, nb + j)),
                  pl.BlockSpec((k_w, LR_BS), lambda b, j: (0, j)),
                  pl.BlockSpec((1, LR_BS), lambda b, j: (0, j)),
                  pl.BlockSpec((2, 2, None, LR_BS, LR_BS), lambda b, j: (0, 0, j, 0, 0)),
                  pl.BlockSpec((2, 2, LR_BS), lambda b, j: (0, 0, j)),
                  pl.BlockSpec((2, LR_BS), lambda b, j: (0, j))],
        out_specs=pl.BlockSpec((None, n_lat, LR_BS), lambda b, j: (b, 0, j)),
        out_shape=jax.ShapeDtypeStruct((b_n, n_lat, d), BF16),
        scratch_shapes=[pltpu.VMEM((LR_BS // LANES, l_n + 16 * SEG_PAD, LANES), F32)] * 5,
        compiler_params=_cparams(("parallel", "parallel"), 56),
        name="rglru",
    )(proj, proj, conv_w, conv_b.reshape(1, d), w_gate, b_gate, lam)
    return o


def _ffn_up_kernel(h_ref, wg_ref, wv_ref, cg_ref, cv_ref, bg_ref, bv_ref, o_ref, u_scr, *, n_ctx, tr):
    l_n, tf = o_ref.shape
    k_w = cg_ref.shape[0]
    half = (k_w - 1) // 2
    n_buf, pad = u_scr.shape[0], (u_scr.shape[1] - l_n) // 2
    rblk = tr * _pick(l_n // tr, (3, 4, 2, 1))
    row = lax.broadcasted_iota(jnp.int32, (tr, LANES), 0)
    for p in range(n_buf):
        u_scr[p, 0:pad, :] = jnp.zeros((pad, 2 * LANES), F32)
        u_scr[p, pad + l_n:, :] = jnp.zeros((pad, 2 * LANES), F32)

    def weights(s):
        cols = slice(s * LANES, (s + 1) * LANES)
        return jnp.concatenate([wg_ref[:, cols], wv_ref[:, cols]], axis=1).astype(BF16)

    def product(s, w, r0):
        u_scr[s % n_buf, pad + r0:pad + r0 + rblk, :] = jnp.dot(h_ref[r0:r0 + rblk, :], w,
                                                                preferred_element_type=F32)

    def finish(s, r0):
        p, cols = s % n_buf, slice(s * LANES, (s + 1) * LANES)

        def conv(lane0, w_ref, b_ref):
            acc = None
            for j in range(k_w):
                sh = j - half
                x = u_scr[p, pad + r0 + sh:pad + r0 + sh + tr, lane0:lane0 + LANES]
                if sh < 0 and r0 in (0, n_ctx):
                    x = jnp.where(row < -sh, 0.0, x)
                if sh > 0 and r0 + tr in (n_ctx, l_n):
                    x = jnp.where(row >= tr - sh, 0.0, x)
                t = x * w_ref[j:j + 1, cols]
                acc = t if acc is None else acc + t
            return b_ref[:, cols] + acc
        gate = conv(0, cg_ref, bg_ref)
        val = conv(LANES, cv_ref, bv_ref)
        o_ref[r0:r0 + tr, cols] = (gate * _sigmoid(gate) * val).astype(BF16)

    n_s = tf // LANES
    for s in range(n_s + 1):
        w = weights(s) if s < n_s else None
        for r0 in range(0, l_n, rblk):
            if s < n_s:
                product(s, w, r0)
            if s > 0:
                for r1 in range(r0, r0 + rblk, tr):
                    finish(s - 1, r1)


def _conv_ffn(h, n_ctx, w_up, layer, conv_w, conv_b, *, tr):
    b_n, l_n, d = h.shape
    f = w_up.shape[2] // 2
    tf = _pick(f, (512, 256, 128))
    nf = f // tf
    k_w = conv_w.shape[0]
    cb = conv_b.reshape(1, 2 * f)
    act = pl.pallas_call(
        functools.partial(_ffn_up_kernel, n_ctx=n_ctx, tr=tr),
        grid=(b_n, nf),
        in_specs=[pl.BlockSpec((None, l_n, d), lambda b, j: (b, 0, 0), pipeline_mode=pl.Buffered(1)),
                  pl.BlockSpec((None, d, tf), lambda b, j: (layer, 0, j)),
                  pl.BlockSpec((None, d, tf), lambda b, j: (layer, 0, nf + j)),
                  pl.BlockSpec((k_w, tf), lambda b, j: (0, j)),
                  pl.BlockSpec((k_w, tf), lambda b, j: (0, nf + j)),
                  pl.BlockSpec((1, tf), lambda b, j: (0, j)),
                  pl.BlockSpec((1, tf), lambda b, j: (0, nf + j))],
        out_specs=pl.BlockSpec((None, l_n, tf), lambda b, j: (b, 0, j)),
        out_shape=jax.ShapeDtypeStruct((b_n, l_n, f), BF16),
        scratch_shapes=[pltpu.VMEM((3, l_n + 16, 2 * LANES), F32)],
        compiler_params=_cparams(("parallel", "arbitrary"), 56),
        name="ffn_up",
    )(h, w_up, w_up, conv_w, conv_w, cb, cb)
    return act


def kernel(x, c, ctx, c_ctx, ada_w, ada_b, ln_g, ln_b, ffn_w_up, ffn_conv_w, ffn_conv_b, ffn_w_down, rw_mu, rw_w_rkv, rw_w0, rw_w1, rw_w2, rw_a0, rw_a1, rw_a2, rw_g1, rw_g2, rw_k_k, rw_k_a, rw_r_k, rw_gn_g, rw_gn_b, rw_w_o, da_w_qkv, da_lambda, da_sub_g, da_w_o, hg_w_in, hg_lower, hg_norm_g, hg_w_o, lr_w_in, lr_conv_w, lr_conv_b, lr_w_gate, lr_b_gate, lr_lambda, lr_w_o):
    b_n, n_lat, d = x.shape
    n_ctx = ctx.shape[1]
    depth = ada_w.shape[0]
    assert depth == 4 and rw_mu.shape[0] == 1, "one occurrence of each of the four mixers"
    assert b_n + 1 <= 8 and n_ctx % CHUNK == 0 and n_lat % CHUNK == 0
    tr = math.gcd(math.gcd(n_ctx, n_lat), 256)
    alpha = (2 * depth) ** 0.25

    c8 = jnp.concatenate([c, c_ctx[None], jnp.zeros((8 - b_n - 1, d), F32)], axis=0)
    m = _ada(c8, ada_w, ada_b)
    m_lat = m[:, :b_n].reshape(depth, b_n, 1, 6, d)
    m_ctx = jnp.broadcast_to(m[:, b_n].reshape(depth, 1, 1, 6, d), (depth, b_n, 1, 6, d))
    mod = jnp.concatenate([m_ctx, m_lat], axis=2)

    z = jnp.concatenate([ctx, x], axis=1)
    w_down_b = _to_bf16(ffn_w_down)
    h = None
    for i in range(depth):
        last = i == depth - 1
        if i == 0:
            o = _rwkv7_layer(z, mod[0], n_ctx, rw_mu[0], rw_w_rkv[0], rw_w0[0], rw_w1[0], rw_w2[0], rw_a0[0],
                             rw_a1[0], rw_a2[0], rw_g1[0], rw_g2[0], rw_k_k[0], rw_k_a[0], rw_r_k[0],
                             rw_gn_g[0], rw_gn_b[0])
            w_o = rw_w_o
        elif i == 1:
            o = _diff_attention_layer(h, n_ctx, i, da_w_qkv[0], da_lambda[0], da_sub_g[0], tr=tr)
            w_o = da_w_o
        elif i == 2:
            o = _hgrn2_layer(h, n_ctx, i, hg_w_in[0], hg_lower, hg_norm_g[0])
            w_o = hg_w_o
        else:
            o = _rglru_layer(h, n_ctx, lr_w_in[0], lr_conv_w[0], lr_conv_b[0], lr_w_gate[0], lr_b_gate[0],
                             lr_lambda[0])
            w_o = lr_w_o
        z, h = _proj_ln(o, w_o, 0, z, mod[i], ln_g[i, 0], ln_b[i, 0], mod[i], gate_j=2, mod_j=3, tr=tr,
                        n_ctx=n_ctx, alpha=alpha)
        if last:
            n_ctx = 0
        act = _conv_ffn(h, n_ctx, ffn_w_up, i, ffn_conv_w[i], ffn_conv_b[i], tr=tr)
        z, h = _proj_ln(act, w_down_b, i, z, mod[i], ln_g[i, 1], ln_b[i, 1], mod[min(i + 1, depth - 1)],
                        gate_j=5, mod_j=None if last else 0, tr=tr, n_ctx=n_ctx, alpha=alpha)
    return z
```

```python
import functools
import math

import jax
import jax.numpy as jnp
from jax import lax
from jax.experimental import pallas as pl
from jax.experimental.pallas import tpu as pltpu

F32, BF16 = jnp.float32, jnp.bfloat16

LANES = 128
CHUNK = 64
LN_EPS = 1e-5
GRID_W = 64
ROPE_BASE = 10000.0
RW_HEAD = 64
RW_DECAY_SCALE = 0.606531
RW_GN_EPS = 64e-5
DA_HEAD = 128
HG_EXPAND = 128
LR_BS = 256
LR_C = 8.0
MIB = 1024 * 1024


def _pick(n, cands):
    for c in cands:
        if n % c == 0:
            return c
    return n


def _cparams(sem, vmem_mib):
    return pltpu.CompilerParams(dimension_semantics=sem, vmem_limit_bytes=vmem_mib * MIB)


def _sigmoid(x):
    return jax.nn.sigmoid(x)


NN = (((1,), (0,)), ((), ()))
NT = (((1,), (1,)), ((), ()))


def _parts(x, n):
    out = []
    for i in range(n):
        p = x.astype(BF16)
        out.append(p)
        if i + 1 < n:
            x = x - p.astype(F32)
    return out


def _mdot(ap, bp, dims=NN, order=2):
    pairs = [(a, b) for i, a in enumerate(ap) for j, b in enumerate(bp) if i + j < order]
    (ca,), (cb,) = dims[0]
    lhs = jnp.concatenate([a for a, _ in pairs], axis=ca) if len(pairs) > 1 else pairs[0][0]
    rhs = jnp.concatenate([b for _, b in pairs], axis=cb) if len(pairs) > 1 else pairs[0][1]
    return lax.dot_general(lhs, rhs, dims, preferred_element_type=F32)


def _cumsum_matrix(t_n, rev):
    r = lax.broadcasted_iota(jnp.int32, (t_n, 3 * t_n), 0)
    c = lax.broadcasted_iota(jnp.int32, (t_n, 3 * t_n), 1) % t_n
    return jnp.where((c >= r) if rev else (c <= r), 1.0, 0.0).astype(BF16)


def _cumsum(tri3_b, x):
    return jnp.dot(tri3_b, jnp.concatenate(_parts(x, 3), axis=0), preferred_element_type=F32)


def _cat_parts(xs, axis):
    return [jnp.concatenate(ps, axis=axis) for ps in zip(*xs)]


def _mm_kernel(a_ref, w_ref, o_ref, acc_ref, *, nk):
    if w_ref.dtype == BF16:
        prod = jnp.dot(a_ref[...], w_ref[...], preferred_element_type=F32)
    else:
        tk = w_ref.shape[0]
        n_kc = 2 if tk % (2 * LANES) == 0 else 1
        prod = None
        for c in range(n_kc):
            ks = slice(c * (tk // n_kc), (c + 1) * (tk // n_kc))
            t = jnp.dot(a_ref[:, ks], w_ref[ks, :].astype(BF16), preferred_element_type=F32)
            prod = t if prod is None else prod + t
    if nk == 1:
        o_ref[...] = prod.astype(o_ref.dtype)
    else:
        k = pl.program_id(3)

        @pl.when(k == 0)
        def _():
            acc_ref[...] = prod

        @pl.when(k > 0)
        def _():
            acc_ref[...] += prod

        @pl.when(k == nk - 1)
        def _():
            o_ref[...] = acc_ref[...].astype(o_ref.dtype)


def _mm_wres_kernel(a_ref, w_ref, o_ref, wb_ref):
    i = pl.program_id(2)
    k_n = w_ref.shape[0]
    n_kc = 4 if k_n % (4 * LANES) == 0 else 1

    @pl.when(i == 0)
    def _():
        kc = k_n // n_kc
        acc = None
        for c in range(n_kc):
            ks = slice(c * kc, (c + 1) * kc)
            wb = w_ref[ks, :].astype(BF16)
            wb_ref[ks, :] = wb
            t = jnp.dot(a_ref[:, ks], wb, preferred_element_type=F32)
            acc = t if acc is None else acc + t
        o_ref[...] = acc.astype(o_ref.dtype)

    @pl.when(i > 0)
    def _():
        o_ref[...] = jnp.dot(a_ref[...], wb_ref[...], preferred_element_type=F32).astype(o_ref.dtype)


def _matmul(a, w, *, out_dtype=F32, a_off=0):
    g_n, k_n, n_n = w.shape
    m_n = a.shape[1]
    tm = _pick(m_n, (1024, 512, 256, 128, 64))
    tn = _pick(n_n, (1024, 512, 256, 128))
    tk = k_n if k_n <= 2048 else _pick(k_n, (2816, 2048, 1024, 512))
    nk = k_n // tk
    if w.dtype == F32 and nk > 1:
        tn = _pick(n_n, (512, 256, 128))
    if w.dtype == F32 and nk == 1:
        return pl.pallas_call(
            _mm_wres_kernel,
            grid=(g_n, n_n // tn, m_n // tm),
            in_specs=[pl.BlockSpec((None, tm, k_n), lambda g, j, i: (g + a_off, i, 0)),
                      pl.BlockSpec((None, k_n, tn), lambda g, j, i: (g, 0, j))],
            out_specs=pl.BlockSpec((None, tm, tn), lambda g, j, i: (g, i, j)),
            out_shape=jax.ShapeDtypeStruct((g_n, m_n, n_n), out_dtype),
            scratch_shapes=[pltpu.VMEM((k_n, tn), BF16)],
            compiler_params=_cparams(("parallel", "parallel", "arbitrary"), 48),
            name="matmul_wres",
        )(a, w)
    return pl.pallas_call(
        functools.partial(_mm_kernel, nk=nk),
        grid=(g_n, m_n // tm, n_n // tn, nk),
        in_specs=[pl.BlockSpec((None, tm, tk), lambda g, i, j, k: (g + a_off, i, k)),
                  pl.BlockSpec((None, tk, tn), lambda g, i, j, k: (g, k, j))],
        out_specs=pl.BlockSpec((None, tm, tn), lambda g, i, j, k: (g, i, j)),
        out_shape=jax.ShapeDtypeStruct((g_n, m_n, n_n), out_dtype),
        scratch_shapes=[pltpu.VMEM((tm, tn), F32)],
        compiler_params=_cparams(("parallel", "parallel", "parallel", "arbitrary"), 48),
        name="matmul",
    )(a, w)


def _mm2(a, w, **kw):
    return _matmul(a[None], w[None], **kw)[0]


def _ada_kernel(c_ref, w_ref, b_ref, o_ref):
    c = c_ref[...]
    s = (c * _sigmoid(c)).astype(BF16)
    o_ref[...] = jnp.dot(s, w_ref[...].astype(BF16), preferred_element_type=F32) + b_ref[...]


def _ada(c8, ada_w, ada_b):
    depth, d, n = ada_w.shape
    tn = _pick(n, (1024, 512, 256, 128))
    return pl.pallas_call(
        _ada_kernel,
        grid=(depth, n // tn),
        in_specs=[pl.BlockSpec((8, d), lambda l, j: (0, 0)),
                  pl.BlockSpec((None, d, tn), lambda l, j: (l, 0, j)),
                  pl.BlockSpec((None, 1, tn), lambda l, j: (l, 0, j))],
        out_specs=pl.BlockSpec((None, 8, tn), lambda l, j: (l, 0, j)),
        out_shape=jax.ShapeDtypeStruct((depth, 8, n), F32),
        compiler_params=_cparams(("parallel", "parallel"), 40),
        name="ada",
    )(c8, ada_w, ada_b.reshape(depth, 1, n))


def _ln_mod_kernel(z_ref, y_ref, mod_ref, g_ref, b_ref, mod2_ref, *out_refs, gate_j, mod_j, alpha):
    m = mod_ref[...]
    zz = alpha * z_ref[...] + y_ref[...] * m[gate_j:gate_j + 1]
    mu = jnp.mean(zz, axis=-1, keepdims=True)
    zc = zz - mu
    var = jnp.mean(zc * zc, axis=-1, keepdims=True)
    zn = zc * lax.rsqrt(var + LN_EPS) * g_ref[...] + b_ref[...]
    out_refs[0][...] = zn
    if mod_j is not None:
        m2 = mod2_ref[...]
        out_refs[1][...] = (zn * (1 + m2[mod_j + 1:mod_j + 2]) + m2[mod_j:mod_j + 1]).astype(BF16)


def _ln_mod(z, y, mod, ln_g, ln_b, mod2, *, gate_j, mod_j, tr, n_ctx, alpha):
    b_n, l_z, d = z.shape
    l_y = y.shape[1]
    z_off = (l_z - l_y) // tr
    ncb = (n_ctx - (l_z - l_y)) // tr
    seg = lambda b, t: (b, jnp.where(t < ncb, 0, 1), 0, 0)
    row = pl.BlockSpec((None, tr, d), lambda b, t: (b, t, 0))
    out_shape = [jax.ShapeDtypeStruct((b_n, l_y, d), F32)]
    out_specs = [row]
    if mod_j is not None:
        out_shape.append(jax.ShapeDtypeStruct((b_n, l_y, d), BF16))
        out_specs.append(row)
    res = pl.pallas_call(
        functools.partial(_ln_mod_kernel, gate_j=gate_j, mod_j=mod_j, alpha=alpha),
        grid=(b_n, l_y // tr),
        in_specs=[pl.BlockSpec((None, tr, d), lambda b, t: (b, t + z_off, 0)),
                  row,
                  pl.BlockSpec((None, None, 6, d), seg),
                  pl.BlockSpec((1, d), lambda b, t: (0, 0)),
                  pl.BlockSpec((1, d), lambda b, t: (0, 0)),
                  pl.BlockSpec((None, None, 6, d), seg)],
        out_specs=out_specs,
        out_shape=out_shape,
        compiler_params=_cparams(("parallel", "parallel"), 40),
        name="ln_mod",
    )(z, y, mod, ln_g.reshape(1, d), ln_b.reshape(1, d), mod2)
    return res if mod_j is not None else (res[0], None)


def _cast_kernel(x_ref, o_ref):
    o_ref[...] = x_ref[...].astype(o_ref.dtype)


def _to_bf16(w):
    g_n, k_n, n_n = w.shape
    tk = _pick(k_n, (512, 256, 128))
    spec = pl.BlockSpec((None, tk, n_n), lambda g, i: (g, i, 0))
    return pl.pallas_call(
        _cast_kernel, grid=(g_n, k_n // tk), in_specs=[spec], out_specs=spec,
        out_shape=jax.ShapeDtypeStruct(w.shape, BF16),
        compiler_params=_cparams(("parallel", "parallel"), 32),
        name="to_bf16",
    )(w)


def _proj_ln_kernel(a_ref, w_ref, z_ref, mod_ref, g_ref, b_ref, mod2_ref, zo_ref, *ho_refs,
                    nk, gate_j, mod_j, alpha, tr, ncb):
    t, k = pl.program_id(1), pl.program_id(2)
    tm = zo_ref.shape[0]
    rb = tm // 2 if tm % 16 == 0 else tm

    def accumulate(first):
        for r0 in range(0, tm, rb):
            p = jnp.dot(a_ref[r0:r0 + rb, :], w_ref[...], preferred_element_type=F32)
            if first:
                zo_ref[r0:r0 + rb, :] = p
            else:
                zo_ref[r0:r0 + rb, :] += p

    rs = math.gcd(tr, 64)

    def finish():
        for sb in range(tm // rs):
            rows = slice(sb * rs, (sb + 1) * rs)
            is_ctx = t * (tm // tr) + (sb * rs) // tr < ncb
            m = jnp.where(is_ctx, mod_ref[0], mod_ref[1])
            zz = alpha * z_ref[rows, :] + zo_ref[rows, :] * m[gate_j:gate_j + 1]
            mu = jnp.mean(zz, axis=-1, keepdims=True)
            zc = zz - mu
            var = jnp.mean(zc * zc, axis=-1, keepdims=True)
            zn = zc * lax.rsqrt(var + LN_EPS) * g_ref[...] + b_ref[...]
            zo_ref[rows, :] = zn
            if mod_j is not None:
                m2 = jnp.where(is_ctx, mod2_ref[0], mod2_ref[1])
                ho_refs[0][rows, :] = (zn * (1 + m2[mod_j + 1:mod_j + 2]) + m2[mod_j:mod_j + 1]).astype(BF16)

    if nk == 1:
        accumulate(True)
        finish()
    else:
        @pl.when(k == 0)
        def _():
            accumulate(True)

        @pl.when(k > 0)
        def _():
            accumulate(False)

        @pl.when(k == nk - 1)
        def _():
            finish()


def _proj_ln(a, w, g, z, mod, ln_g, ln_b, mod2, *, gate_j, mod_j, tr, n_ctx, alpha):
    b_n, l_a, k_n = a.shape
    d = w.shape[2]
    if z.shape[1] != l_a:
        y = _matmul(a.reshape(1, b_n * l_a, k_n), w[g:g + 1])[0].reshape(b_n, l_a, d)
        return _ln_mod(z, y, mod, ln_g, ln_b, mod2, gate_j=gate_j, mod_j=mod_j, tr=tr, n_ctx=n_ctx, alpha=alpha)
    if w.dtype != BF16:
        w = _to_bf16(w)
    tm = tr * _pick(l_a // tr, (3, 2, 1))
    tk = k_n if k_n <= 2048 else _pick(k_n, (1408, 1024, 512, 256, 128))
    nk = k_n // tk
    row = pl.BlockSpec((None, tm, d), lambda b, t, k: (b, t, 0))
    seg = pl.BlockSpec((None, 2, 6, d), lambda b, t, k: (b, 0, 0, 0))
    vec = pl.BlockSpec((1, d), lambda b, t, k: (0, 0))
    out_shape = [jax.ShapeDtypeStruct((b_n, l_a, d), F32)]
    out_specs = [row]
    if mod_j is not None:
        out_shape.append(jax.ShapeDtypeStruct((b_n, l_a, d), BF16))
        out_specs.append(row)
    res = pl.pallas_call(
        functools.partial(_proj_ln_kernel, nk=nk, gate_j=gate_j, mod_j=mod_j, alpha=alpha, tr=tr,
                          ncb=n_ctx // tr),
        grid=(b_n, l_a // tm, nk),
        in_specs=[pl.BlockSpec((None, tm, tk), lambda b, t, k: (b, t, k)),
                  pl.BlockSpec((None, tk, d), lambda b, t, k: (g, k, 0),
                               pipeline_mode=pl.Buffered(1) if nk == 1 else None),
                  row, seg, vec, vec, seg],
        out_specs=out_specs,
        out_shape=out_shape,
        compiler_params=_cparams(("parallel", "parallel", "arbitrary"), 56),
        name="proj_ln",
    )(a, w, z, mod, ln_g.reshape(1, d), ln_b.reshape(1, d), mod2)
    return res if mod_j is not None else (res[0], None)


def _seg_shift(x, row, shift, n_ctx):
    l_n = x.shape[0]
    rolled = pltpu.roll(x, (-shift) % l_n, 0)
    src = row + shift
    same_seg = (src >= 0) & (src < l_n) & ((src < n_ctx) == (row < n_ctx))
    return jnp.where(same_seg, rolled, 0.0)


def _rw_mix_kernel(z_ref, mod_ref, mu_ref, o_ref, *, n_ctx):
    z = z_ref[...]
    row = lax.broadcasted_iota(jnp.int32, z.shape, 0)
    is_ctx = row < n_ctx
    shift = jnp.where(is_ctx, mod_ref[0, 0:1, :], mod_ref[1, 0:1, :])
    scale = jnp.where(is_ctx, mod_ref[0, 1:2, :], mod_ref[1, 1:2, :])
    h = z * (1 + scale) + shift
    dx = 0.5 * (_seg_shift(h, row, -1, n_ctx) + _seg_shift(h, row, 1, n_ctx)) - h
    for n in range(6):
        o_ref[n] = (h + dx * mu_ref[n:n + 1, :]).astype(BF16)


def _rw_mix(z, mod, mu, *, n_ctx):
    b_n, l_n, d = z.shape
    tc = _pick(d, (256, 128))
    return pl.pallas_call(
        functools.partial(_rw_mix_kernel, n_ctx=n_ctx),
        grid=(b_n, d // tc),
        in_specs=[pl.BlockSpec((None, l_n, tc), lambda b, j: (b, 0, j)),
                  pl.BlockSpec((None, 2, 6, tc), lambda b, j: (b, 0, 0, j)),
                  pl.BlockSpec((6, tc), lambda b, j: (0, j))],
        out_specs=pl.BlockSpec((6, None, l_n, tc), lambda b, j: (0, b, 0, j)),
        out_shape=jax.ShapeDtypeStruct((6, b_n, l_n, d), BF16),
        compiler_params=_cparams(("parallel", "parallel"), 48),
        name="rw_mix",
    )(z, mod, mu)


def _lora_kernel(x_ref, a_ref, b_ref, o_ref, *, act):
    t = jnp.dot(x_ref[...], a_ref[...], preferred_element_type=F32)
    if act == "tanh":
        t = jnp.tanh(t)
    elif act == "sigmoid":
        t = _sigmoid(t)
    o_ref[...] = jnp.dot(t.astype(BF16), b_ref[...], preferred_element_type=F32)


def _lora(xs, x_idx, a, b, act):
    g_n, d, r = a.shape
    m_n = xs.shape[1]
    tm = _pick(m_n, (512, 256, 128, 64))
    return pl.pallas_call(
        functools.partial(_lora_kernel, act=act),
        grid=(g_n, m_n // tm),
        in_specs=[pl.BlockSpec((None, tm, d), lambda g, i: (x_idx, i, 0)),
                  pl.BlockSpec((None, d, r), lambda g, i: (g, 0, 0)),
                  pl.BlockSpec((None, r, d), lambda g, i: (g, 0, 0))],
        out_specs=pl.BlockSpec((None, tm, d), lambda g, i: (g, i, 0)),
        out_shape=jax.ShapeDtypeStruct((g_n, m_n, d), F32),
        compiler_params=_cparams(("parallel", "parallel"), 40),
        name="lora",
    )(xs, a, b)


def _chunk_of(q, ncc, nc, rev):
    if not rev:
        return q
    return jnp.where(q < ncc, ncc - 1 - q, nc - 1 - (q - ncc))


def _rwkv_kernel(r_ref, k_ref, v_ref, lw_ref, la_ref, g_ref, w0_ref, a0_ref, kk_ref, ka_ref, rk_ref,
                 gng_ref, gnb_ref, o_ref,
                 y_scr, mr_s, n_s, *, n_ctx):
    t_n = CHUNK
    h2 = 2 * t_n
    l_n = r_ref.shape[0]
    nc, ncc = l_n // t_n, n_ctx // t_n
    group = _pick(nc, (6, 4, 3, 2, 1))
    lane = lax.broadcasted_iota(jnp.int32, (1, LANES), 1)
    m1 = jnp.where(lane < RW_HEAD, 1.0, 0.0)
    m2 = 1.0 - m1
    ri = lax.broadcasted_iota(jnp.int32, (LANES, LANES), 0)
    ci = lax.broadcasted_iota(jnp.int32, (LANES, LANES), 1)
    same_head = (ri // RW_HEAD) == (ci // RW_HEAD)
    gsum_b = jnp.where(same_head, 1.0, 0.0).astype(BF16)
    gavg_b = jnp.where(same_head, 1.0 / RW_HEAD, 0.0).astype(BF16)
    eye = jnp.where(ri == ci, 1.0, 0.0)
    tr_i, tc_i = ri % t_n, ci % t_n
    k_k, k_a = kk_ref[...], ka_ref[...]

    def stack(x):
        return jnp.concatenate([x * m1, x * m2], axis=0)

    def rows_of(c):
        return pl.ds(pl.multiple_of(c * t_n, t_n), t_n)

    def head_sum(x, w_b):
        return _mdot(_parts(x, 3), [w_b], order=3)

    tri3_b = [_cumsum_matrix(t_n, rev) for rev in (False, True)]
    strict = [(tc_i > tr_i) if rev else (tc_i < tr_i) for rev in (False, True)]
    incl = [(tc_i >= tr_i) if rev else (tc_i <= tr_i) for rev in (False, True)]

    def stage_prep(d, c):
        rows = rows_of(c)
        k, r, v = k_ref[rows, :], r_ref[rows, :], v_ref[rows, :]
        kkr = k * k_k
        both = dict(kkr=kkr, ss=head_sum(kkr * kkr, gsum_b))
        lw = -RW_DECAY_SCALE * _sigmoid(w0_ref[d:d + 1, :] + lw_ref[d, rows, :])
        a = _sigmoid(a0_ref[d:d + 1, :] + la_ref[d, rows, :])
        return dict(d=d, c=c, k=k, r=r, v=v, lw=lw, a=a, both=both, cum=_cumsum(tri3_b[d], lw))

    def stage_amat(s):
        d, cum, lw, a, both = s["d"], s["cum"], s["lw"], s["a"], s["both"]
        if "kk" not in both:
            both["kk"] = both["kkr"] * lax.rsqrt(both["ss"] + 1e-12)
            both["vp"] = _parts(stack(s["v"]), 1)
        kk = both["kk"]
        kd = s["k"] * (1 + (a - 1) * k_a)
        bv = kk * a
        p_end = cum[0:1, :] if d == 1 else cum[t_n - 1:t_n, :]
        e_m = jnp.exp(-cum)
        e_h = jnp.exp(p_end - cum)
        ktp = _parts(stack(kk * jnp.exp(cum - lw)), 2)
        rt = stack(s["r"] * jnp.exp(cum))
        k2p = _cat_parts([_parts(stack(bv * e_m), 2), _parts(stack(kd * e_m), 2)], 0)
        return dict(d=d, c=s["c"], ktp=ktp, rt=rt, vp=both["vp"], p_end=p_end,
                    bh=stack(bv * e_h), kh=stack(kd * e_h),
                    amat=_mdot(_cat_parts([ktp, _parts(rt, 2)], 0), k2p, NT))

    def stage_square(s):
        d, amat = s["d"], s["amat"]
        lt = jnp.where(strict[d], amat[:h2, :h2], 0.0).T
        ltp = _parts(lt, 2)
        msk = jnp.concatenate([jnp.where(strict[d], amat[:h2, h2:], 0.0),
                               jnp.where(incl[d], amat[h2:, h2:], 0.0)], axis=0)
        s = dict(s, pt=eye - lt, xt=_mdot(ltp, ltp),
                 av=_mdot(_parts(msk, 2), s["vp"][:1]),
                 arbp=_parts(jnp.where(incl[d], amat[h2:, :h2], 0.0), 2))
        del s["amat"]
        return s

    def stage_double(s, final):
        xh = _parts(s["xt"], 1)
        ptp = _parts(s["pt"], 2)
        if final:
            return dict(s, pt=s["pt"] + _mdot(xh, ptp))
        rhs = [jnp.concatenate([ptp[0], xh[0]], axis=1), jnp.concatenate([ptp[1], jnp.zeros_like(xh[0])], axis=1)]
        both = _mdot(xh, rhs)
        return dict(s, pt=s["pt"] + both[:, :LANES], xt=both[:, LANES:])

    def stage_solve(s):
        rhs = jnp.concatenate([s["ktp"][0], (-s["av"][:h2]).astype(BF16)], axis=1)
        return dict(s, wub=_mdot(_parts(s["pt"].T, 2), [rhs]).astype(BF16))

    def stage_fold(s):
        d, c, wub = s["d"], s["c"], s["wub"]
        aw = _mdot(s["arbp"], [wub])
        zb = jnp.zeros((h2, LANES), BF16)
        lhs = _cat_parts([_parts(s["bh"].T, 2), _parts(s["kh"].T, 2)], 1)
        rhs = jnp.concatenate([wub, jnp.concatenate([zb, s["vp"][0]], axis=1)], axis=0)
        mn = _mdot(lhs, [rhs])
        dg = jnp.where(ri == ci, jnp.broadcast_to(jnp.exp(s["p_end"]), (LANES, LANES)), 0.0)
        rp = s["rt"] - aw[:, :LANES]
        mrp = _parts(jnp.concatenate([dg - mn[:, :LANES], rp[:t_n] + rp[t_n:]], axis=0), 2)
        for i in range(2):
            mr_s[d, c, i] = mrp[i]
        n_s[d, c] = mn[:, LANES:]
        y0 = s["av"][h2:] + aw[:, LANES:]
        return y0[:t_n] + y0[t_n:]

    def seq(q, hs):
        cs = (q, _chunk_of(q, ncc, nc, True))
        mh = [_mdot([mr_s[d, cs[d], 0], mr_s[d, cs[d], 1]], _parts(hs[d], 2)) for d in (0, 1)]
        for d in (0, 1):
            y_scr[rows_of(cs[d]), :] += mh[d][h2:]
        return tuple(mh[d][:h2] + n_s[d, cs[d]] for d in (0, 1))

    def local(i, hs, with_seq):
        steps = [i * group + g for g in range(group)]
        pending = [q - group for q in steps] if with_seq else []
        sts = [stage_prep(d, q if d == 0 else _chunk_of(q, ncc, nc, True)) for q in steps for d in (0, 1)]
        stages = ([stage_amat, stage_square] + [functools.partial(stage_double, final=f) for f in (False,) * 4 + (True,)]
                  + [stage_solve])
        for stage in stages:
            sts = [stage(s) for s in sts]
            if pending:
                hs = seq(pending.pop(0), hs)
        for s in sts:
            y0 = stage_fold(s)
            y_scr[rows_of(s["c"]), :] += y0
        while pending:
            hs = seq(pending.pop(0), hs)
        return hs

    y_scr[...] = jnp.zeros(y_scr.shape, F32)
    zero = jnp.zeros((LANES, LANES), F32)
    n_trip = nc // group
    hs = local(0, (zero, zero), False)
    hs = lax.fori_loop(1, n_trip, functools.partial(local, with_seq=True), hs)
    for q in range((n_trip - 1) * group, nc):
        hs = seq(q, hs)

    n_post = _pick(nc, (9, 6, 4, 3, 2, 1))

    def post(i, carry):
        rows = [rows_of(i * n_post + g) for g in range(n_post)]

        def bonus_sum(rw):
            k, r = k_ref[rw, :], r_ref[rw, :]
            kd_f = k * (1 + (_sigmoid(a0_ref[0:1, :] + la_ref[0, rw, :]) - 1) * k_a)
            kd_b = k * (1 + (_sigmoid(a0_ref[1:2, :] + la_ref[1, rw, :]) - 1) * k_a)
            return head_sum(r * (kd_f + kd_b) * rk_ref[...], gsum_b)
        bsum = [bonus_sum(rw) for rw in rows]
        ys = [y_scr[rw, :] for rw in rows]
        ycs = [y - m for y, m in zip(ys, [head_sum(y, gavg_b) for y in ys])]
        var = [head_sum(yc * yc, gavg_b) for yc in ycs]
        for rw, yc, vr, bs in zip(rows, ycs, var, bsum):
            yn = yc * lax.rsqrt(vr + RW_GN_EPS) * gng_ref[...] + gnb_ref[...]
            o_ref[rw, :] = ((yn + bs * v_ref[rw, :]) * g_ref[rw, :]).astype(BF16)
        return carry

    lax.fori_loop(0, nc // n_post, post, 0)


def _rwkv_scan(rkv, lw, la, g, w0, a0, k_k, k_a, r_k, gn_g, gn_b, *, n_ctx):
    _, b_n, l_n, d = rkv.shape
    nc = l_n // CHUNK
    col = lambda n: pl.BlockSpec((None, None, l_n, LANES), lambda b, p, n=n: (n, b, 0, p))
    two = pl.BlockSpec((2, None, l_n, LANES), lambda b, p: (0, b, 0, p))
    par = lambda rows: pl.BlockSpec((rows, LANES), lambda b, p: (0, p))
    return pl.pallas_call(
        functools.partial(_rwkv_kernel, n_ctx=n_ctx),
        grid=(b_n, d // LANES),
        in_specs=[col(0), col(1), col(2), two, two,
                  pl.BlockSpec((None, l_n, LANES), lambda b, p: (b, 0, p)),
                  par(2), par(2), par(1), par(1), par(1), par(1), par(1)],
        out_specs=pl.BlockSpec((None, l_n, LANES), lambda b, p: (b, 0, p)),
        out_shape=jax.ShapeDtypeStruct((b_n, l_n, d), BF16),
        scratch_shapes=[pltpu.VMEM((l_n, LANES), F32),
                        pltpu.VMEM((2, nc, 2, LANES + CHUNK, LANES), BF16),
                        pltpu.VMEM((2, nc, LANES, LANES), F32)],
        compiler_params=_cparams(("parallel", "parallel"), 56),
        name="rwkv_scan",
    )(rkv, rkv, rkv, lw, la, g, w0, a0, k_k.reshape(1, d), k_a.reshape(1, d), r_k.reshape(1, d),
      gn_g.reshape(1, d), gn_b.reshape(1, d))


def _pad_axis(w, axis, to):
    pad = [(0, 0)] * w.ndim
    pad[axis] = (0, to - w.shape[axis])
    return jnp.pad(w, pad)


def _rwkv7_layer(z, mod, n_ctx, mu, w_rkv, w0, w1, w2, a0, a1, a2, g1, g2, k_k, k_a, r_k, gn_g, gn_b):
    b_n, l_n, d = z.shape
    m_n = b_n * l_n
    xs = _rw_mix(z, mod, mu, n_ctx=n_ctx).reshape(6, m_n, d)
    rkv = _matmul(xs, w_rkv)
    r_w = -(-w1.shape[-1] // LANES) * LANES
    r_a = -(-a1.shape[-1] // LANES) * LANES
    lw = _lora(xs, 3, _pad_axis(w1, 2, r_w).astype(BF16), _pad_axis(w2, 1, r_w).astype(BF16), "tanh")
    la = _lora(xs, 4, _pad_axis(a1, 2, r_a).astype(BF16), _pad_axis(a2, 1, r_a).astype(BF16), None)
    gate = _lora(xs, 5, g1[None].astype(BF16), g2[None].astype(BF16), "sigmoid")
    o = _rwkv_scan(rkv.reshape(3, b_n, l_n, d), lw.reshape(2, b_n, l_n, d), la.reshape(2, b_n, l_n, d),
                   gate.reshape(b_n, l_n, d), w0, a0, k_k, k_a, r_k, gn_g, gn_b, n_ctx=n_ctx)
    return o


def _qkv_rope_kernel(a_ref, w_ref, cos_ref, sa_ref, sb_ref, o_ref, wb_ref, *, n_q, n_qk):
    j, i = pl.program_id(0), pl.program_id(1)

    @pl.when(i == 0)
    def _():
        wb_ref[...] = w_ref[...].astype(BF16)

    tn = o_ref.shape[1]
    sw = math.gcd(tn, 2 * DA_HEAD)

    def product(s):
        return jnp.dot(a_ref[...], wb_ref[:, s * sw:(s + 1) * sw], preferred_element_type=F32)

    @pl.when(j < n_qk)
    def _():
        q = DA_HEAD // 4
        q_scale = jnp.where(j < n_q, DA_HEAD ** -0.5 * math.log2(math.e), 1.0)
        cos, s_a, s_b = cos_ref[...] * q_scale, sa_ref[...] * q_scale, sb_ref[...] * q_scale

        def rotary(s, x):
            for c in range(0, sw, DA_HEAD):
                xs = x[:, c:c + DA_HEAD]
                rot = xs * cos + pltpu.roll(xs, DA_HEAD - q, 1) * s_a + pltpu.roll(xs, q, 1) * s_b
                o_ref[:, s * sw + c:s * sw + c + DA_HEAD] = rot.astype(BF16)
        x_prev = product(0)
        for s in range(1, tn // sw):
            x_next = product(s)
            rotary(s - 1, x_prev)
            x_prev = x_next
        rotary(tn // sw - 1, x_prev)

    @pl.when(j >= n_qk)
    def _():
        for s in range(tn // sw):
            o_ref[:, s * sw:(s + 1) * sw] = product(s).astype(BF16)


def _qkv_rope(h, w_qkv, cos, s_a, s_b):
    b_n, l_n, d = h.shape
    d3 = w_qkv.shape[1]
    tm = _pick(l_n, (768, 1024, 512, 256, 128, 64))
    tn = _pick(d, (1024, 512, 256, 128))
    n_t = l_n // tm
    tab = pl.BlockSpec((tm, DA_HEAD), lambda j, i: (i % n_t, 0))
    out = pl.pallas_call(
        functools.partial(_qkv_rope_kernel, n_q=d // tn, n_qk=2 * d // tn),
        grid=(d3 // tn, b_n * n_t),
        in_specs=[pl.BlockSpec((tm, d), lambda j, i: (i, 0)),
                  pl.BlockSpec((d, tn), lambda j, i: (0, j)), tab, tab, tab],
        out_specs=pl.BlockSpec((tm, tn), lambda j, i: (i, j)),
        out_shape=jax.ShapeDtypeStruct((b_n * l_n, d3), BF16),
        scratch_shapes=[pltpu.VMEM((d, tn), BF16)],
        compiler_params=_cparams(("parallel", "arbitrary"), 48),
        name="qkv_rope",
    )(h.reshape(b_n * l_n, d), w_qkv, cos, s_a, s_b)
    return out.reshape(b_n, l_n, d3)


def _attn_kernel(q_ref, k_ref, v_ref, lam_ref, sg_ref, o_ref, *, ncb, n_ctx, lam_init):
    qi = pl.program_id(2)
    lv = lam_ref[...]
    lam = (jnp.exp(jnp.sum(lv[0:1] * lv[1:2], axis=-1, keepdims=True))
           - jnp.exp(jnp.sum(lv[2:3] * lv[3:4], axis=-1, keepdims=True)) + lam_init)

    hw = 2 * DA_HEAD
    n_hh = o_ref.shape[1] // hw

    def attend(nk):
        def scores(hh, m):
            cols = slice(hh * hw + m * DA_HEAD, hh * hw + (m + 1) * DA_HEAD)
            return lax.dot_general(q_ref[:, cols], k_ref[0:nk, cols], NT, preferred_element_type=F32)
        s_all = [[scores(hh, m) for m in (0, 1)] for hh in range(n_hh)]
        for hh in range(n_hh):
            def probs(s):
                e = jnp.exp2(s - jnp.max(s, axis=-1, keepdims=True))
                return e, 1.0 / jnp.sum(e, axis=-1, keepdims=True)
            e0, i0 = probs(s_all[hh][0])
            e1, i1 = probs(s_all[hh][1])
            v = v_ref[0:nk, hh * hw:(hh + 1) * hw]
            o = (jnp.dot(e0.astype(BF16), v, preferred_element_type=F32) * i0
                 - jnp.dot(e1.astype(BF16), v, preferred_element_type=F32) * (lam * i1))
            o = o * lax.rsqrt(jnp.mean(o * o, axis=-1, keepdims=True) + 1e-5) * sg_ref[...] * (1 - lam_init)
            o_ref[:, hh * hw:(hh + 1) * hw] = o.astype(BF16)

    if ncb > 0:
        @pl.when(qi < ncb)
        def _():
            attend(n_ctx)

    @pl.when(qi >= ncb)
    def _():
        attend(k_ref.shape[0])


def _attention(qkv, lam_vec, sub_g, *, tq, n_ctx, lam_init):
    b_n, l_n, d3 = qkv.shape
    d = d3 // 3
    hw = 2 * DA_HEAD
    n_hh = 2 if (d // hw) % 2 == 0 else 1
    bw = n_hh * hw
    nh = d // bw
    return pl.pallas_call(
        functools.partial(_attn_kernel, ncb=n_ctx // tq, n_ctx=n_ctx, lam_init=lam_init),
        grid=(b_n, nh, l_n // tq),
        in_specs=[pl.BlockSpec((None, tq, bw), lambda b, h, t: (b, t, h)),
                  pl.BlockSpec((None, l_n, bw), lambda b, h, t: (b, 0, nh + h)),
                  pl.BlockSpec((None, l_n, bw), lambda b, h, t: (b, 0, 2 * nh + h)),
                  pl.BlockSpec((4, DA_HEAD), lambda b, h, t: (0, 0)),
                  pl.BlockSpec((1, hw), lambda b, h, t: (0, 0))],
        out_specs=pl.BlockSpec((None, tq, bw), lambda b, h, t: (b, t, h)),
        out_shape=jax.ShapeDtypeStruct((b_n, l_n, d), BF16),
        compiler_params=_cparams(("parallel", "parallel", "arbitrary"), 48),
        name="diff_attn",
    )(qkv, qkv, qkv, lam_vec, sub_g.reshape(1, hw))


def _rope_tables(n_ctx, n_lat):
    n_rows = n_lat // GRID_W
    row = jnp.repeat(jnp.arange(n_rows, dtype=F32), GRID_W)
    col = jnp.tile(jnp.arange(GRID_W, dtype=F32), n_rows)
    nf = DA_HEAD // 4
    inv_freq = ROPE_BASE ** (-jnp.arange(nf, dtype=F32) / nf)
    ang_r, ang_c = row[:, None] * inv_freq, col[:, None] * inv_freq
    ang = jnp.concatenate([ang_r, ang_r, ang_c, ang_c], axis=-1)
    ang = jnp.concatenate([jnp.zeros((n_ctx, DA_HEAD), F32), ang], axis=0)
    cos, sin = jnp.cos(ang), jnp.sin(ang)
    even_q = (jnp.arange(DA_HEAD) // nf) % 2 == 0
    return cos, jnp.where(even_q, -sin, 0.0), jnp.where(even_q, 0.0, sin)


def _diff_attention_layer(h, n_ctx, layer_idx, w_qkv, lam_vec, sub_g, *, tr):
    b_n, l_n, d = h.shape
    cos, s_a, s_b = _rope_tables(n_ctx, l_n - n_ctx)
    qkv = _qkv_rope(h, w_qkv, cos, s_a, s_b)
    lam_init = 0.8 - 0.6 * math.exp(-0.3 * layer_idx)
    o = _attention(qkv, lam_vec, sub_g, tq=tr, n_ctx=n_ctx, lam_init=lam_init)
    return o


def _hgrn_kernel(q_ref, i_ref, g_ref, ff_ref, fb_ref, low_ref, ng_ref, o_ref, o_scr, *, n_ctx, layer_idx):
    t_n = CHUNK
    l_n = q_ref.shape[0]
    nc, ncc = l_n // t_n, n_ctx // t_n
    r64 = lax.broadcasted_iota(jnp.int32, (t_n, t_n), 0)
    c64 = lax.broadcasted_iota(jnp.int32, (t_n, t_n), 1)

    def rows_of(c):
        return pl.ds(pl.multiple_of(c * t_n, t_n), t_n)

    f_refs = (ff_ref, fb_ref)
    lbs, incl = [], []
    tri3_b = [_cumsum_matrix(t_n, rev) for rev in (False, True)]
    for d in (0, 1):
        low = low_ref[d]
        e = jnp.exp(low - jnp.max(low, axis=0, keepdims=True))
        sm = e / jnp.sum(e, axis=0, keepdims=True)
        cs = sm[0:1]
        for rr in range(1, layer_idx + 1):
            cs = cs + sm[rr:rr + 1]
        lbs.append(cs - sm[0:1])
        incl.append((c64 >= r64) if d == 1 else (c64 <= r64))
    group = _pick(nc, (4, 2, 1))

    def stage_cum(d, c):
        rows = rows_of(c)
        f = lbs[d] + (1.0 - lbs[d]) * _sigmoid(f_refs[d][rows, :])
        return dict(d=d, rows=rows, f=f, cum=_cumsum(tri3_b[d], jnp.log(f)))

    def stage_att(s):
        d, cum, rows = s["d"], s["cum"], s["rows"]
        b_end = cum[0:1, :] if d == 1 else cum[t_n - 1:t_n, :]
        qv = q_ref[rows, :]
        qd = (qv * _sigmoid(qv) * jnp.exp(cum)).astype(BF16)
        kk = 1.0 - s["f"]
        v = i_ref[rows, :]
        kd = (kk * jnp.exp(-cum)).astype(BF16)
        ke = (kk * jnp.exp(b_end - cum)).astype(BF16)
        return dict(d=d, rows=rows, qd=qd, vb=v.astype(BF16), dec=jnp.exp(b_end),
                    att=lax.dot_general(qd, kd, NT, preferred_element_type=F32),
                    upd=jnp.dot(v.T.astype(BF16), ke, preferred_element_type=F32))

    def stage_intra(s):
        att = jnp.where(incl[s["d"]], s["att"], 0.0).astype(BF16)
        return dict(s, o=jnp.dot(att, s["vb"], preferred_element_type=F32))

    def body(i, states):
        items = [(d, _chunk_of(i * group + g, ncc, nc, d == 1)) for g in range(group) for d in (0, 1)]
        sts = [stage_cum(d, c) for d, c in items]
        sts = [stage_att(s) for s in sts]
        sts = [stage_intra(s) for s in sts]
        states = list(states)
        for s in sts:
            d = s["d"]
            o = s["o"] + lax.dot_general(s["qd"], states[d].astype(BF16), NT, preferred_element_type=F32)
            o_scr[d, s["rows"], :] = o
            states[d] = states[d] * s["dec"] + s["upd"]
        return tuple(states)

    zero = jnp.zeros((LANES, LANES), F32)
    lax.fori_loop(0, nc // group, body, (zero, zero))

    p_n = t_n * _pick(nc, (4, 3, 2, 1))

    def post(c, carry):
        rows = pl.ds(pl.multiple_of(c * p_n, p_n), p_n)
        o = o_scr[0, rows, :] + o_scr[1, rows, :]
        o = o * lax.rsqrt(jnp.mean(o * o, axis=-1, keepdims=True) + 1e-5) * ng_ref[...]
        gv = g_ref[rows, :]
        o_ref[rows, :] = (o * (gv * _sigmoid(gv))).astype(BF16)
        return carry

    lax.fori_loop(0, l_n // p_n, post, 0)


def _hgrn2_layer(h, n_ctx, layer_idx, w_in, lower, norm_g):
    b_n, l_n, d = h.shape
    m_n = b_n * l_n
    nh = d // HG_EXPAND
    proj = _mm2(h.reshape(m_n, d), w_in).reshape(b_n, l_n, 5 * d)
    col = lambda n: pl.BlockSpec((None, l_n, LANES), lambda b, p, n=n: (b, 0, n * nh + p))
    o = pl.pallas_call(
        functools.partial(_hgrn_kernel, n_ctx=n_ctx, layer_idx=layer_idx),
        grid=(b_n, nh),
        in_specs=[col(0), col(1), col(2), col(3), col(4),
                  pl.BlockSpec((2, lower.shape[1], LANES), lambda b, p: (0, 0, p)),
                  pl.BlockSpec((1, LANES), lambda b, p: (0, 0))],
        out_specs=pl.BlockSpec((None, l_n, LANES), lambda b, p: (b, 0, p)),
        out_shape=jax.ShapeDtypeStruct((b_n, l_n, d), BF16),
        scratch_shapes=[pltpu.VMEM((2, l_n, LANES), F32)],
        compiler_params=_cparams(("parallel", "parallel"), 40),
        name="hgrn_scan",
    )(proj, proj, proj, proj, proj, lower, norm_g.reshape(1, LANES))
    return o


def _gelu_tanh(x):
    return 0.5 * x * (1.0 + jnp.tanh(math.sqrt(2.0 / math.pi) * (x + 0.044715 * (x * x * x))))


def _softplus(x):
    return jnp.maximum(x, 0.0) + jnp.log1p(jnp.exp(-jnp.abs(x)))


SEG_PAD = 8


def _lin_scan(a_ref, u_ref, hl_s, cp_s, h_s, base, row0, n, rev, h_in, accumulate):
    seg = n // 8
    stride = seg + SEG_PAD
    n_p = a_ref.shape[0]

    def step(i, carry):
        t = (seg - 1 - i) if rev else i
        idx = pl.ds(base + t, 8, stride=stride)
        out = []
        for j in range(n_p):
            hl, cp = carry[j]
            a = a_ref[j, idx, :]
            hl = a * hl + u_ref[j, idx, :]
            cp = a * cp
            hl_s[j, idx, :] = hl
            cp_s[j, idx, :] = cp
            out.append((hl, cp))
        return tuple(out)

    init = tuple((jnp.zeros((8, LANES), F32), jnp.ones((8, LANES), F32)) for _ in range(n_p))
    ends = lax.fori_loop(0, seg, step, init)
    order = range(7, -1, -1) if rev else range(8)
    h_out = []
    for j in range(n_p):
        hl_e, cp_e = ends[j]
        carry = h_in[j]
        for s in order:
            r0, p0 = row0 + s * seg, base + s * stride
            blk = hl_s[j, p0:p0 + seg, :] + cp_s[j, p0:p0 + seg, :] * carry
            if accumulate:
                h_s[j, r0:r0 + seg, :] += blk
            else:
                h_s[j, r0:r0 + seg, :] = blk
            carry = hl_e[s:s + 1, :] + cp_e[s:s + 1, :] * carry
        h_out.append(carry)
    return h_out


def _rglru_kernel(gb_ref, xb_ref, cw_ref, cb_ref, wg_ref, bg_ref, lam_ref, o_ref, a_s, u_s, h_s, hl_s, cp_s, *,
                  n_ctx):
    l_n = xb_ref.shape[0]
    n_lat = l_n - n_ctx
    x = xb_ref[...]
    row = lax.broadcasted_iota(jnp.int32, x.shape, 0)
    k_w = cw_ref.shape[0]
    xc = cb_ref[...] + sum(_seg_shift(x, row, j - (k_w - 1) // 2, n_ctx) * cw_ref[j:j + 1, :]
                           for j in range(k_w))
    xcb = xc.astype(BF16)
    n_p = x.shape[1] // LANES
    for d in (0, 1):
        gate = lambda g: _sigmoid(jnp.dot(xcb, wg_ref[d, g].astype(BF16), preferred_element_type=F32)
                                  + bg_ref[d, g:g + 1, :])
        log_a = -LR_C * gate(0) * _softplus(-lam_ref[d:d + 1, :])
        a = jnp.exp(log_a)
        u = jnp.sqrt(jnp.tanh(-log_a) * (jnp.exp(2.0 * log_a) + 1.0)) * gate(1) * xc
        h = [jnp.zeros((1, LANES), F32)] * n_p
        base = 0
        for row0, n in ((0, n_ctx), (n_ctx, n_lat)):
            if n:
                seg = n // 8
                for j in range(n_p):
                    for s in range(8):
                        src = slice(row0 + s * seg, row0 + (s + 1) * seg)
                        dst = slice(base + s * (seg + SEG_PAD), base + s * (seg + SEG_PAD) + seg)
                        a_s[j, dst, :] = a[src, j * LANES:(j + 1) * LANES]
                        u_s[j, dst, :] = u[src, j * LANES:(j + 1) * LANES]
                h = _lin_scan(a_s, u_s, hl_s, cp_s, h_s, base, row0, n, d == 1, h, d == 1)
                base += 8 * (seg + SEG_PAD)
    for j in range(n_p):
        cols = slice(j * LANES, (j + 1) * LANES)
        o_ref[:, cols] = (h_s[j, n_ctx:l_n, :] * _gelu_tanh(gb_ref[n_ctx:, cols])).astype(BF16)


def _rglru_layer(h, n_ctx, w_in, conv_w, conv_b, w_gate, b_gate, lam):
    b_n, l_n, d = h.shape
    n_lat = l_n - n_ctx
    nb = d // LR_BS
    proj = _mm2(h.reshape(b_n * l_n, d), w_in).reshape(b_n, l_n, 2 * d)
    k_w = conv_w.shape[0]
    o = pl.pallas_call(
        functools.partial(_rglru_kernel, n_ctx=n_ctx),
        grid=(b_n, nb),
        in_specs=[pl.BlockSpec((None, l_n, LR_BS), lambda b, j: (b, 0, j)),
                  pl.BlockSpec((None, l_n, LR_BS), lambda b, j: (b, 0, nb + j)),
                  pl.BlockSpec((k_w, LR_BS), lambda b, j: (0, j)),
                  pl.BlockSpec((1, LR_BS), lambda b, j: (0, j)),
                  pl.BlockSpec((2, 2, None, LR_BS, LR_BS), lambda b, j: (0, 0, j, 0, 0)),
                  pl.BlockSpec((2, 2, LR_BS), lambda b, j: (0, 0, j)),
                  pl.BlockSpec((2, LR_BS), lambda b, j: (0, j))],
        out_specs=pl.BlockSpec((None, n_lat, LR_BS), lambda b, j: (b, 0, j)),
        out_shape=jax.ShapeDtypeStruct((b_n, n_lat, d), BF16),
        scratch_shapes=[pltpu.VMEM((LR_BS // LANES, l_n + 16 * SEG_PAD, LANES), F32)] * 5,
        compiler_params=_cparams(("parallel", "parallel"), 56),
        name="rglru",
    )(proj, proj, conv_w, conv_b.reshape(1, d), w_gate, b_gate, lam)
    return o


def _ffn_up_kernel(h_ref, wg_ref, wv_ref, cg_ref, cv_ref, bg_ref, bv_ref, o_ref, u_scr, *, n_ctx, tr):
    l_n, tf = o_ref.shape
    k_w = cg_ref.shape[0]
    half = (k_w - 1) // 2
    n_buf, pad = u_scr.shape[0], (u_scr.shape[1] - l_n) // 2
    rblk = tr * _pick(l_n // tr, (3, 4, 2, 1))
    row = lax.broadcasted_iota(jnp.int32, (tr, LANES), 0)
    for p in range(n_buf):
        u_scr[p, 0:pad, :] = jnp.zeros((pad, 2 * LANES), F32)
        u_scr[p, pad + l_n:, :] = jnp.zeros((pad, 2 * LANES), F32)

    def weights(s):
        cols = slice(s * LANES, (s + 1) * LANES)
        return jnp.concatenate([wg_ref[:, cols], wv_ref[:, cols]], axis=1).astype(BF16)

    def product(s, w, r0):
        u_scr[s % n_buf, pad + r0:pad + r0 + rblk, :] = jnp.dot(h_ref[r0:r0 + rblk, :], w,
                                                                preferred_element_type=F32)

    def finish(s, r0):
        p, cols = s % n_buf, slice(s * LANES, (s + 1) * LANES)

        def conv(lane0, w_ref, b_ref):
            acc = None
            for j in range(k_w):
                sh = j - half
                x = u_scr[p, pad + r0 + sh:pad + r0 + sh + tr, lane0:lane0 + LANES]
                if sh < 0 and r0 in (0, n_ctx):
                    x = jnp.where(row < -sh, 0.0, x)
                if sh > 0 and r0 + tr in (n_ctx, l_n):
                    x = jnp.where(row >= tr - sh, 0.0, x)
                t = x * w_ref[j:j + 1, cols]
                acc = t if acc is None else acc + t
            return b_ref[:, cols] + acc
        gate = conv(0, cg_ref, bg_ref)
        val = conv(LANES, cv_ref, bv_ref)
        o_ref[r0:r0 + tr, cols] = (gate * _sigmoid(gate) * val).astype(BF16)

    n_s = tf // LANES
    for s in range(n_s + 1):
        w = weights(s) if s < n_s else None
        for r0 in range(0, l_n, rblk):
            if s < n_s:
                product(s, w, r0)
            if s > 0:
                for r1 in range(r0, r0 + rblk, tr):
                    finish(s - 1, r1)


def _conv_ffn(h, n_ctx, w_up, layer, conv_w, conv_b, *, tr):
    b_n, l_n, d = h.shape
    f = w_up.shape[2] // 2
    tf = _pick(f, (512, 256, 128))
    nf = f // tf
    k_w = conv_w.shape[0]
    cb = conv_b.reshape(1, 2 * f)
    act = pl.pallas_call(
        functools.partial(_ffn_up_kernel, n_ctx=n_ctx, tr=tr),
        grid=(b_n, nf),
        in_specs=[pl.BlockSpec((None, l_n, d), lambda b, j: (b, 0, 0), pipeline_mode=pl.Buffered(1)),
                  pl.BlockSpec((None, d, tf), lambda b, j: (layer, 0, j)),
                  pl.BlockSpec((None, d, tf), lambda b, j: (layer, 0, nf + j)),
                  pl.BlockSpec((k_w, tf), lambda b, j: (0, j)),
                  pl.BlockSpec((k_w, tf), lambda b, j: (0, nf + j)),
                  pl.BlockSpec((1, tf), lambda b, j: (0, j)),
                  pl.BlockSpec((1, tf), lambda b, j: (0, nf + j))],
        out_specs=pl.BlockSpec((None, l_n, tf), lambda b, j: (b, 0, j)),
        out_shape=jax.ShapeDtypeStruct((b_n, l_n, f), BF16),
        scratch_shapes=[pltpu.VMEM((3, l_n + 16, 2 * LANES), F32)],
        compiler_params=_cparams(("parallel", "arbitrary"), 56),
        name="ffn_up",
    )(h, w_up, w_up, conv_w, conv_w, cb, cb)
    return act


def kernel(x, c, ctx, c_ctx, ada_w, ada_b, ln_g, ln_b, ffn_w_up, ffn_conv_w, ffn_conv_b, ffn_w_down, rw_mu, rw_w_rkv, rw_w0, rw_w1, rw_w2, rw_a0, rw_a1, rw_a2, rw_g1, rw_g2, rw_k_k, rw_k_a, rw_r_k, rw_gn_g, rw_gn_b, rw_w_o, da_w_qkv, da_lambda, da_sub_g, da_w_o, hg_w_in, hg_lower, hg_norm_g, hg_w_o, lr_w_in, lr_conv_w, lr_conv_b, lr_w_gate, lr_b_gate, lr_lambda, lr_w_o):
    b_n, n_lat, d = x.shape
    n_ctx = ctx.shape[1]
    depth = ada_w.shape[0]
    assert depth == 4 and rw_mu.shape[0] == 1, "one occurrence of each of the four mixers"
    assert b_n + 1 <= 8 and n_ctx % CHUNK == 0 and n_lat % CHUNK == 0
    tr = math.gcd(math.gcd(n_ctx, n_lat), 256)
    alpha = (2 * depth) ** 0.25

    c8 = jnp.concatenate([c, c_ctx[None], jnp.zeros((8 - b_n - 1, d), F32)], axis=0)
    m = _ada(c8, ada_w, ada_b)
    m_lat = m[:, :b_n].reshape(depth, b_n, 1, 6, d)
    m_ctx = jnp.broadcast_to(m[:, b_n].reshape(depth, 1, 1, 6, d), (depth, b_n, 1, 6, d))
    mod = jnp.concatenate([m_ctx, m_lat], axis=2)

    z = jnp.concatenate([ctx, x], axis=1)
    w_down_b = _to_bf16(ffn_w_down)
    h = None
    for i in range(depth):
        last = i == depth - 1
        if i == 0:
            o = _rwkv7_layer(z, mod[0], n_ctx, rw_mu[0], rw_w_rkv[0], rw_w0[0], rw_w1[0], rw_w2[0], rw_a0[0],
                             rw_a1[0], rw_a2[0], rw_g1[0], rw_g2[0], rw_k_k[0], rw_k_a[0], rw_r_k[0],
                             rw_gn_g[0], rw_gn_b[0])
            w_o = rw_w_o
        elif i == 1:
            o = _diff_attention_layer(h, n_ctx, i, da_w_qkv[0], da_lambda[0], da_sub_g[0], tr=tr)
            w_o = da_w_o
        elif i == 2:
            o = _hgrn2_layer(h, n_ctx, i, hg_w_in[0], hg_lower, hg_norm_g[0])
            w_o = hg_w_o
        else:
            o = _rglru_layer(h, n_ctx, lr_w_in[0], lr_conv_w[0], lr_conv_b[0], lr_w_gate[0], lr_b_gate[0],
                             lr_lambda[0])
            w_o = lr_w_o
        z, h = _proj_ln(o, w_o, 0, z, mod[i], ln_g[i, 0], ln_b[i, 0], mod[i], gate_j=2, mod_j=3, tr=tr,
                        n_ctx=n_ctx, alpha=alpha)
        if last:
            n_ctx = 0
        act = _conv_ffn(h, n_ctx, ffn_w_up, i, ffn_conv_w[i], ffn_conv_b[i], tr=tr)
        z, h = _proj_ln(act, w_down_b, i, z, mod[i], ln_g[i, 1], ln_b[i, 1], mod[min(i + 1, depth - 1)],
                        gate_j=5, mod_j=None if last else 0, tr=tr, n_ctx=n_ctx, alpha=alpha)
    return z
```

```python
import functools
import math

import jax
import jax.numpy as jnp
from jax import lax
from jax.experimental import pallas as pl
from jax.experimental.pallas import tpu as pltpu

F32, BF16 = jnp.float32, jnp.bfloat16

LANES = 128
CHUNK = 64
LN_EPS = 1e-5
GRID_W = 64
ROPE_BASE = 10000.0
RW_HEAD = 64
RW_DECAY_SCALE = 0.606531
RW_GN_EPS = 64e-5
DA_HEAD = 128
HG_EXPAND = 128
LR_BS = 256
LR_C = 8.0
MIB = 1024 * 1024


def _pick(n, cands):
    for c in cands:
        if n % c == 0:
            return c
    return n


def _cparams(sem, vmem_mib):
    return pltpu.CompilerParams(dimension_semantics=sem, vmem_limit_bytes=vmem_mib * MIB)


def _sigmoid(x):
    return jax.nn.sigmoid(x)


NN = (((1,), (0,)), ((), ()))
NT = (((1,), (1,)), ((), ()))


def _parts(x, n):
    out = []
    for i in range(n):
        p = x.astype(BF16)
        out.append(p)
        if i + 1 < n:
            x = x - p.astype(F32)
    return out


def _mdot(ap, bp, dims=NN, order=2):
    pairs = [(a, b) for i, a in enumerate(ap) for j, b in enumerate(bp) if i + j < order]
    (ca,), (cb,) = dims[0]
    lhs = jnp.concatenate([a for a, _ in pairs], axis=ca) if len(pairs) > 1 else pairs[0][0]
    rhs = jnp.concatenate([b for _, b in pairs], axis=cb) if len(pairs) > 1 else pairs[0][1]
    return lax.dot_general(lhs, rhs, dims, preferred_element_type=F32)


def _cumsum_matrix(t_n, rev):
    r = lax.broadcasted_iota(jnp.int32, (t_n, 3 * t_n), 0)
    c = lax.broadcasted_iota(jnp.int32, (t_n, 3 * t_n), 1) % t_n
    return jnp.where((c >= r) if rev else (c <= r), 1.0, 0.0).astype(BF16)


def _cumsum(tri3_b, x):
    return jnp.dot(tri3_b, jnp.concatenate(_parts(x, 3), axis=0), preferred_element_type=F32)


def _cat_parts(xs, axis):
    return [jnp.concatenate(ps, axis=axis) for ps in zip(*xs)]


def _mm_kernel(a_ref, w_ref, o_ref, acc_ref, *, nk):
    if w_ref.dtype == BF16:
        prod = jnp.dot(a_ref[...], w_ref[...], preferred_element_type=F32)
    else:
        tk = w_ref.shape[0]
        n_kc = 2 if tk % (2 * LANES) == 0 else 1
        prod = None
        for c in range(n_kc):
            ks = slice(c * (tk // n_kc), (c + 1) * (tk // n_kc))
            t = jnp.dot(a_ref[:, ks], w_ref[ks, :].astype(BF16), preferred_element_type=F32)
            prod = t if prod is None else prod + t
    if nk == 1:
        o_ref[...] = prod.astype(o_ref.dtype)
    else:
        k = pl.program_id(3)

        @pl.when(k == 0)
        def _():
            acc_ref[...] = prod

        @pl.when(k > 0)
        def _():
            acc_ref[...] += prod

        @pl.when(k == nk - 1)
        def _():
            o_ref[...] = acc_ref[...].astype(o_ref.dtype)


def _mm_wres_kernel(a_ref, w_ref, o_ref, wb_ref):
    i = pl.program_id(2)
    k_n = w_ref.shape[0]
    n_kc = 4 if k_n % (4 * LANES) == 0 else 1

    @pl.when(i == 0)
    def _():
        kc = k_n // n_kc
        acc = None
        for c in range(n_kc):
            ks = slice(c * kc, (c + 1) * kc)
            wb = w_ref[ks, :].astype(BF16)
            wb_ref[ks, :] = wb
            t = jnp.dot(a_ref[:, ks], wb, preferred_element_type=F32)
            acc = t if acc is None else acc + t
        o_ref[...] = acc.astype(o_ref.dtype)

    @pl.when(i > 0)
    def _():
        o_ref[...] = jnp.dot(a_ref[...], wb_ref[...], preferred_element_type=F32).astype(o_ref.dtype)


def _matmul(a, w, *, out_dtype=F32, a_off=0):
    g_n, k_n, n_n = w.shape
    m_n = a.shape[1]
    tm = _pick(m_n, (1024, 512, 256, 128, 64))
    tn = _pick(n_n, (1024, 512, 256, 128))
    tk = k_n if k_n <= 2048 else _pick(k_n, (2816, 2048, 1024, 512))
    nk = k_n // tk
    if w.dtype == F32 and nk > 1:
        tn = _pick(n_n, (512, 256, 128))
    if w.dtype == F32 and nk == 1:
        return pl.pallas_call(
            _mm_wres_kernel,
            grid=(g_n, n_n // tn, m_n // tm),
            in_specs=[pl.BlockSpec((None, tm, k_n), lambda g, j, i: (g + a_off, i, 0)),
                      pl.BlockSpec((None, k_n, tn), lambda g, j, i: (g, 0, j))],
            out_specs=pl.BlockSpec((None, tm, tn), lambda g, j, i: (g, i, j)),
            out_shape=jax.ShapeDtypeStruct((g_n, m_n, n_n), out_dtype),
            scratch_shapes=[pltpu.VMEM((k_n, tn), BF16)],
            compiler_params=_cparams(("parallel", "parallel", "arbitrary"), 48),
            name="matmul_wres",
        )(a, w)
    return pl.pallas_call(
        functools.partial(_mm_kernel, nk=nk),
        grid=(g_n, m_n // tm, n_n // tn, nk),
        in_specs=[pl.BlockSpec((None, tm, tk), lambda g, i, j, k: (g + a_off, i, k)),
                  pl.BlockSpec((None, tk, tn), lambda g, i, j, k: (g, k, j))],
        out_specs=pl.BlockSpec((None, tm, tn), lambda g, i, j, k: (g, i, j)),
        out_shape=jax.ShapeDtypeStruct((g_n, m_n, n_n), out_dtype),
        scratch_shapes=[pltpu.VMEM((tm, tn), F32)],
        compiler_params=_cparams(("parallel", "parallel", "parallel", "arbitrary"), 48),
        name="matmul",
    )(a, w)


def _mm2(a, w, **kw):
    return _matmul(a[None], w[None], **kw)[0]


def _ada_kernel(c_ref, w_ref, b_ref, o_ref):
    c = c_ref[...]
    s = (c * _sigmoid(c)).astype(BF16)
    o_ref[...] = jnp.dot(s, w_ref[...].astype(BF16), preferred_element_type=F32) + b_ref[...]


def _ada(c8, ada_w, ada_b):
    depth, d, n = ada_w.shape
    tn = _pick(n, (1024, 512, 256, 128))
    return pl.pallas_call(
        _ada_kernel,
        grid=(depth, n // tn),
        in_specs=[pl.BlockSpec((8, d), lambda l, j: (0, 0)),
                  pl.BlockSpec((None, d, tn), lambda l, j: (l, 0, j)),
                  pl.BlockSpec((None, 1, tn), lambda l, j: (l, 0, j))],
        out_specs=pl.BlockSpec((None, 8, tn), lambda l, j: (l, 0, j)),
        out_shape=jax.ShapeDtypeStruct((depth, 8, n), F32),
        compiler_params=_cparams(("parallel", "parallel"), 40),
        name="ada",
    )(c8, ada_w, ada_b.reshape(depth, 1, n))


def _ln_mod_kernel(z_ref, y_ref, mod_ref, g_ref, b_ref, mod2_ref, *out_refs, gate_j, mod_j, alpha):
    m = mod_ref[...]
    zz = alpha * z_ref[...] + y_ref[...] * m[gate_j:gate_j + 1]
    mu = jnp.mean(zz, axis=-1, keepdims=True)
    zc = zz - mu
    var = jnp.mean(zc * zc, axis=-1, keepdims=True)
    zn = zc * lax.rsqrt(var + LN_EPS) * g_ref[...] + b_ref[...]
    out_refs[0][...] = zn
    if mod_j is not None:
        m2 = mod2_ref[...]
        out_refs[1][...] = (zn * (1 + m2[mod_j + 1:mod_j + 2]) + m2[mod_j:mod_j + 1]).astype(BF16)


def _ln_mod(z, y, mod, ln_g, ln_b, mod2, *, gate_j, mod_j, tr, n_ctx, alpha):
    b_n, l_z, d = z.shape
    l_y = y.shape[1]
    z_off = (l_z - l_y) // tr
    ncb = (n_ctx - (l_z - l_y)) // tr
    seg = lambda b, t: (b, jnp.where(t < ncb, 0, 1), 0, 0)
    row = pl.BlockSpec((None, tr, d), lambda b, t: (b, t, 0))
    out_shape = [jax.ShapeDtypeStruct((b_n, l_y, d), F32)]
    out_specs = [row]
    if mod_j is not None:
        out_shape.append(jax.ShapeDtypeStruct((b_n, l_y, d), BF16))
        out_specs.append(row)
    res = pl.pallas_call(
        functools.partial(_ln_mod_kernel, gate_j=gate_j, mod_j=mod_j, alpha=alpha),
        grid=(b_n, l_y // tr),
        in_specs=[pl.BlockSpec((None, tr, d), lambda b, t: (b, t + z_off, 0)),
                  row,
                  pl.BlockSpec((None, None, 6, d), seg),
                  pl.BlockSpec((1, d), lambda b, t: (0, 0)),
                  pl.BlockSpec((1, d), lambda b, t: (0, 0)),
                  pl.BlockSpec((None, None, 6, d), seg)],
        out_specs=out_specs,
        out_shape=out_shape,
        compiler_params=_cparams(("parallel", "parallel"), 40),
        name="ln_mod",
    )(z, y, mod, ln_g.reshape(1, d), ln_b.reshape(1, d), mod2)
    return res if mod_j is not None else (res[0], None)


def _cast_kernel(x_ref, o_ref):
    o_ref[...] = x_ref[...].astype(o_ref.dtype)


def _to_bf16(w):
    g_n, k_n, n_n = w.shape
    tk = _pick(k_n, (512, 256, 128))
    spec = pl.BlockSpec((None, tk, n_n), lambda g, i: (g, i, 0))
    return pl.pallas_call(
        _cast_kernel, grid=(g_n, k_n // tk), in_specs=[spec], out_specs=spec,
        out_shape=jax.ShapeDtypeStruct(w.shape, BF16),
        compiler_params=_cparams(("parallel", "parallel"), 32),
        name="to_bf16",
    )(w)


def _proj_ln_kernel(a_ref, w_ref, z_ref, mod_ref, g_ref, b_ref, mod2_ref, zo_ref, *ho_refs,
                    nk, gate_j, mod_j, alpha, tr, ncb):
    t, k = pl.program_id(1), pl.program_id(2)
    tm = zo_ref.shape[0]
    rb = tm // 2 if tm % 16 == 0 else tm

    def accumulate(first):
        for r0 in range(0, tm, rb):
            p = jnp.dot(a_ref[r0:r0 + rb, :], w_ref[...], preferred_element_type=F32)
            if first:
                zo_ref[r0:r0 + rb, :] = p
            else:
                zo_ref[r0:r0 + rb, :] += p

    rs = math.gcd(tr, 64)

    def finish():
        for sb in range(tm // rs):
            rows = slice(sb * rs, (sb + 1) * rs)
            is_ctx = t * (tm // tr) + (sb * rs) // tr < ncb
            m = jnp.where(is_ctx, mod_ref[0], mod_ref[1])
            zz = alpha * z_ref[rows, :] + zo_ref[rows, :] * m[gate_j:gate_j + 1]
            mu = jnp.mean(zz, axis=-1, keepdims=True)
            zc = zz - mu
            var = jnp.mean(zc * zc, axis=-1, keepdims=True)
            zn = zc * lax.rsqrt(var + LN_EPS) * g_ref[...] + b_ref[...]
            zo_ref[rows, :] = zn
            if mod_j is not None:
                m2 = jnp.where(is_ctx, mod2_ref[0], mod2_ref[1])
                ho_refs[0][rows, :] = (zn * (1 + m2[mod_j + 1:mod_j + 2]) + m2[mod_j:mod_j + 1]).astype(BF16)

    if nk == 1:
        accumulate(True)
        finish()
    else:
        @pl.when(k == 0)
        def _():
            accumulate(True)

        @pl.when((k > 0) & (k < nk - 1))
        def _():
            accumulate(False)

        @pl.when(k == nk - 1)
        def _():
            accumulate(False)
            finish()


def _proj_ln(a, w, g, z, mod, ln_g, ln_b, mod2, *, gate_j, mod_j, tr, n_ctx, alpha):
    b_n, l_a, k_n = a.shape
    d = w.shape[2]
    if z.shape[1] != l_a:
        y = _matmul(a.reshape(1, b_n * l_a, k_n), w[g:g + 1])[0].reshape(b_n, l_a, d)
        return _ln_mod(z, y, mod, ln_g, ln_b, mod2, gate_j=gate_j, mod_j=mod_j, tr=tr, n_ctx=n_ctx, alpha=alpha)
    if w.dtype != BF16:
        w = _to_bf16(w)
    tm = tr * _pick(l_a // tr, (3, 2, 1))
    tk = k_n if k_n <= 2048 else _pick(k_n, (1408, 1024, 512, 256, 128))
    nk = k_n // tk
    row = pl.BlockSpec((None, tm, d), lambda b, t, k: (b, t, 0))
    seg = pl.BlockSpec((None, 2, 6, d), lambda b, t, k: (b, 0, 0, 0))
    vec = pl.BlockSpec((1, d), lambda b, t, k: (0, 0))
    out_shape = [jax.ShapeDtypeStruct((b_n, l_a, d), F32)]
    out_specs = [row]
    if mod_j is not None:
        out_shape.append(jax.ShapeDtypeStruct((b_n, l_a, d), BF16))
        out_specs.append(row)
    res = pl.pallas_call(
        functools.partial(_proj_ln_kernel, nk=nk, gate_j=gate_j, mod_j=mod_j, alpha=alpha, tr=tr,
                          ncb=n_ctx // tr),
        grid=(b_n, l_a // tm, nk),
        in_specs=[pl.BlockSpec((None, tm, tk), lambda b, t, k: (b, t, k)),
                  pl.BlockSpec((None, tk, d), lambda b, t, k: (g, k, 0),
                               pipeline_mode=pl.Buffered(1) if nk == 1 else None),
                  row, seg, vec, vec, seg],
        out_specs=out_specs,
        out_shape=out_shape,
        compiler_params=_cparams(("parallel", "parallel", "arbitrary"), 56),
        name="proj_ln",
    )(a, w, z, mod, ln_g.reshape(1, d), ln_b.reshape(1, d), mod2)
    return res if mod_j is not None else (res[0], None)


def _seg_shift(x, row, shift, n_ctx):
    l_n = x.shape[0]
    rolled = pltpu.roll(x, (-shift) % l_n, 0)
    src = row + shift
    same_seg = (src >= 0) & (src < l_n) & ((src < n_ctx) == (row < n_ctx))
    return jnp.where(same_seg, rolled, 0.0)


def _rw_mix_kernel(z_ref, mod_ref, mu_ref, o_ref, *, n_ctx):
    z = z_ref[...]
    row = lax.broadcasted_iota(jnp.int32, z.shape, 0)
    is_ctx = row < n_ctx
    shift = jnp.where(is_ctx, mod_ref[0, 0:1, :], mod_ref[1, 0:1, :])
    scale = jnp.where(is_ctx, mod_ref[0, 1:2, :], mod_ref[1, 1:2, :])
    h = z * (1 + scale) + shift
    dx = 0.5 * (_seg_shift(h, row, -1, n_ctx) + _seg_shift(h, row, 1, n_ctx)) - h
    for n in range(6):
        o_ref[n] = (h + dx * mu_ref[n:n + 1, :]).astype(BF16)


def _rw_mix(z, mod, mu, *, n_ctx):
    b_n, l_n, d = z.shape
    tc = _pick(d, (256, 128))
    return pl.pallas_call(
        functools.partial(_rw_mix_kernel, n_ctx=n_ctx),
        grid=(b_n, d // tc),
        in_specs=[pl.BlockSpec((None, l_n, tc), lambda b, j: (b, 0, j)),
                  pl.BlockSpec((None, 2, 6, tc), lambda b, j: (b, 0, 0, j)),
                  pl.BlockSpec((6, tc), lambda b, j: (0, j))],
        out_specs=pl.BlockSpec((6, None, l_n, tc), lambda b, j: (0, b, 0, j)),
        out_shape=jax.ShapeDtypeStruct((6, b_n, l_n, d), BF16),
        compiler_params=_cparams(("parallel", "parallel"), 48),
        name="rw_mix",
    )(z, mod, mu)


def _lora_kernel(x_ref, a_ref, b_ref, o_ref, *, act):
    t = jnp.dot(x_ref[...], a_ref[...], preferred_element_type=F32)
    if act == "tanh":
        t = jnp.tanh(t)
    elif act == "sigmoid":
        t = _sigmoid(t)
    o_ref[...] = jnp.dot(t.astype(BF16), b_ref[...], preferred_element_type=F32)


def _lora(xs, x_idx, a, b, act):
    g_n, d, r = a.shape
    m_n = xs.shape[1]
    tm = _pick(m_n, (512, 256, 128, 64))
    return pl.pallas_call(
        functools.partial(_lora_kernel, act=act),
        grid=(g_n, m_n // tm),
        in_specs=[pl.BlockSpec((None, tm, d), lambda g, i: (x_idx, i, 0)),
                  pl.BlockSpec((None, d, r), lambda g, i: (g, 0, 0)),
                  pl.BlockSpec((None, r, d), lambda g, i: (g, 0, 0))],
        out_specs=pl.BlockSpec((None, tm, d), lambda g, i: (g, i, 0)),
        out_shape=jax.ShapeDtypeStruct((g_n, m_n, d), F32),
        compiler_params=_cparams(("parallel", "parallel"), 40),
        name="lora",
    )(xs, a, b)


def _chunk_of(q, ncc, nc, rev):
    if not rev:
        return q
    return jnp.where(q < ncc, ncc - 1 - q, nc - 1 - (q - ncc))


def _rwkv_kernel(r_ref, k_ref, v_ref, lw_ref, la_ref, g_ref, w0_ref, a0_ref, kk_ref, ka_ref, rk_ref,
                 gng_ref, gnb_ref, o_ref,
                 y_scr, mr_s, n_s, *, n_ctx):
    t_n = CHUNK
    h2 = 2 * t_n
    l_n = r_ref.shape[0]
    nc, ncc = l_n // t_n, n_ctx // t_n
    group = _pick(nc, (6, 4, 3, 2, 1))
    lane = lax.broadcasted_iota(jnp.int32, (1, LANES), 1)
    m1 = jnp.where(lane < RW_HEAD, 1.0, 0.0)
    m2 = 1.0 - m1
    ri = lax.broadcasted_iota(jnp.int32, (LANES, LANES), 0)
    ci = lax.broadcasted_iota(jnp.int32, (LANES, LANES), 1)
    same_head = (ri // RW_HEAD) == (ci // RW_HEAD)
    gsum_b = jnp.where(same_head, 1.0, 0.0).astype(BF16)
    gavg_b = jnp.where(same_head, 1.0 / RW_HEAD, 0.0).astype(BF16)
    eye = jnp.where(ri == ci, 1.0, 0.0)
    tr_i, tc_i = ri % t_n, ci % t_n
    k_k, k_a = kk_ref[...], ka_ref[...]

    def stack(x):
        return jnp.concatenate([x * m1, x * m2], axis=0)

    def rows_of(c):
        return pl.ds(pl.multiple_of(c * t_n, t_n), t_n)

    def head_sum(x, w_b):
        return _mdot(_parts(x, 3), [w_b], order=3)

    tri3_b = [_cumsum_matrix(t_n, rev) for rev in (False, True)]
    strict = [(tc_i > tr_i) if rev else (tc_i < tr_i) for rev in (False, True)]
    incl = [(tc_i >= tr_i) if rev else (tc_i <= tr_i) for rev in (False, True)]

    def stage_prep(d, c):
        rows = rows_of(c)
        k, r, v = k_ref[rows, :], r_ref[rows, :], v_ref[rows, :]
        kkr = k * k_k
        both = dict(kkr=kkr, ss=head_sum(kkr * kkr, gsum_b))
        lw = -RW_DECAY_SCALE * _sigmoid(w0_ref[d:d + 1, :] + lw_ref[d, rows, :])
        a = _sigmoid(a0_ref[d:d + 1, :] + la_ref[d, rows, :])
        return dict(d=d, c=c, k=k, r=r, v=v, lw=lw, a=a, both=both, cum=_cumsum(tri3_b[d], lw))

    def stage_amat(s):
        d, cum, lw, a, both = s["d"], s["cum"], s["lw"], s["a"], s["both"]
        if "kk" not in both:
            both["kk"] = both["kkr"] * lax.rsqrt(both["ss"] + 1e-12)
            both["vp"] = _parts(stack(s["v"]), 1)
        kk = both["kk"]
        kd = s["k"] * (1 + (a - 1) * k_a)
        bv = kk * a
        p_end = cum[0:1, :] if d == 1 else cum[t_n - 1:t_n, :]
        e_m = jnp.exp(-cum)
        e_h = jnp.exp(p_end - cum)
        ktp = _parts(stack(kk * jnp.exp(cum - lw)), 2)
        rt = stack(s["r"] * jnp.exp(cum))
        k2p = _cat_parts([_parts(stack(bv * e_m), 2), _parts(stack(kd * e_m), 2)], 0)
        return dict(d=d, c=s["c"], ktp=ktp, rt=rt, vp=both["vp"], p_end=p_end,
                    bh=stack(bv * e_h), kh=stack(kd * e_h),
                    amat=_mdot(_cat_parts([ktp, _parts(rt, 2)], 0), k2p, NT))

    def stage_square(s):
        d, amat = s["d"], s["amat"]
        lt = jnp.where(strict[d], amat[:h2, :h2], 0.0).T
        ltp = _parts(lt, 2)
        msk = jnp.concatenate([jnp.where(strict[d], amat[:h2, h2:], 0.0),
                               jnp.where(incl[d], amat[h2:, h2:], 0.0)], axis=0)
        s = dict(s, pt=eye - lt, xt=_mdot(ltp, ltp),
                 av=_mdot(_parts(msk, 2), s["vp"][:1]),
                 arbp=_parts(jnp.where(incl[d], amat[h2:, :h2], 0.0), 2))
        del s["amat"]
        return s

    def stage_double(s, final):
        xh = _parts(s["xt"], 1)
        ptp = _parts(s["pt"], 2)
        if final:
            return dict(s, pt=s["pt"] + _mdot(xh, ptp))
        rhs = [jnp.concatenate([ptp[0], xh[0]], axis=1), jnp.concatenate([ptp[1], jnp.zeros_like(xh[0])], axis=1)]
        both = _mdot(xh, rhs)
        return dict(s, pt=s["pt"] + both[:, :LANES], xt=both[:, LANES:])

    def stage_solve(s):
        rhs = jnp.concatenate([s["ktp"][0], (-s["av"][:h2]).astype(BF16)], axis=1)
        return dict(s, wub=_mdot(_parts(s["pt"].T, 2), [rhs]).astype(BF16))

    def stage_fold(s):
        d, c, wub = s["d"], s["c"], s["wub"]
        aw = _mdot(s["arbp"], [wub])
        zb = jnp.zeros((h2, LANES), BF16)
        lhs = _cat_parts([_parts(s["bh"].T, 2), _parts(s["kh"].T, 2)], 1)
        rhs = jnp.concatenate([wub, jnp.concatenate([zb, s["vp"][0]], axis=1)], axis=0)
        mn = _mdot(lhs, [rhs])
        dg = jnp.where(ri == ci, jnp.broadcast_to(jnp.exp(s["p_end"]), (LANES, LANES)), 0.0)
        rp = s["rt"] - aw[:, :LANES]
        mrp = _parts(jnp.concatenate([dg - mn[:, :LANES], rp[:t_n] + rp[t_n:]], axis=0), 2)
        for i in range(2):
            mr_s[d, c, i] = mrp[i]
        n_s[d, c] = mn[:, LANES:]
        y0 = s["av"][h2:] + aw[:, LANES:]
        return y0[:t_n] + y0[t_n:]

    def seq(q, hs):
        cs = (q, _chunk_of(q, ncc, nc, True))
        mh = [_mdot([mr_s[d, cs[d], 0], mr_s[d, cs[d], 1]], _parts(hs[d], 2)) for d in (0, 1)]
        for d in (0, 1):
            y_scr[rows_of(cs[d]), :] += mh[d][h2:]
        return tuple(mh[d][:h2] + n_s[d, cs[d]] for d in (0, 1))

    def local(i, hs, with_seq):
        steps = [i * group + g for g in range(group)]
        pending = [q - group for q in steps] if with_seq else []
        sts = [stage_prep(d, q if d == 0 else _chunk_of(q, ncc, nc, True)) for q in steps for d in (0, 1)]
        stages = ([stage_amat, stage_square] + [functools.partial(stage_double, final=f) for f in (False,) * 4 + (True,)]
                  + [stage_solve])
        for stage in stages:
            sts = [stage(s) for s in sts]
            if pending:
                hs = seq(pending.pop(0), hs)
        for s in sts:
            y0 = stage_fold(s)
            y_scr[rows_of(s["c"]), :] += y0
        while pending:
            hs = seq(pending.pop(0), hs)
        return hs

    y_scr[...] = jnp.zeros(y_scr.shape, F32)
    zero = jnp.zeros((LANES, LANES), F32)
    n_trip = nc // group
    hs = local(0, (zero, zero), False)
    hs = lax.fori_loop(1, n_trip, functools.partial(local, with_seq=True), hs)
    for q in range((n_trip - 1) * group, nc):
        hs = seq(q, hs)

    n_post = _pick(nc, (9, 6, 4, 3, 2, 1))

    def post(i, carry):
        rows = [rows_of(i * n_post + g) for g in range(n_post)]

        def bonus_sum(rw):
            k, r = k_ref[rw, :], r_ref[rw, :]
            kd_f = k * (1 + (_sigmoid(a0_ref[0:1, :] + la_ref[0, rw, :]) - 1) * k_a)
            kd_b = k * (1 + (_sigmoid(a0_ref[1:2, :] + la_ref[1, rw, :]) - 1) * k_a)
            return head_sum(r * (kd_f + kd_b) * rk_ref[...], gsum_b)
        bsum = [bonus_sum(rw) for rw in rows]
        ys = [y_scr[rw, :] for rw in rows]
        ycs = [y - m for y, m in zip(ys, [head_sum(y, gavg_b) for y in ys])]
        var = [head_sum(yc * yc, gavg_b) for yc in ycs]
        for rw, yc, vr, bs in zip(rows, ycs, var, bsum):
            yn = yc * lax.rsqrt(vr + RW_GN_EPS) * gng_ref[...] + gnb_ref[...]
            o_ref[rw, :] = ((yn + bs * v_ref[rw, :]) * g_ref[rw, :]).astype(BF16)
        return carry

    lax.fori_loop(0, nc // n_post, post, 0)


def _rwkv_scan(rkv, lw, la, g, w0, a0, k_k, k_a, r_k, gn_g, gn_b, *, n_ctx):
    _, b_n, l_n, d = rkv.shape
    nc = l_n // CHUNK
    col = lambda n: pl.BlockSpec((None, None, l_n, LANES), lambda b, p, n=n: (n, b, 0, p))
    two = pl.BlockSpec((2, None, l_n, LANES), lambda b, p: (0, b, 0, p))
    par = lambda rows: pl.BlockSpec((rows, LANES), lambda b, p: (0, p))
    return pl.pallas_call(
        functools.partial(_rwkv_kernel, n_ctx=n_ctx),
        grid=(b_n, d // LANES),
        in_specs=[col(0), col(1), col(2), two, two,
                  pl.BlockSpec((None, l_n, LANES), lambda b, p: (b, 0, p)),
                  par(2), par(2), par(1), par(1), par(1), par(1), par(1)],
        out_specs=pl.BlockSpec((None, l_n, LANES), lambda b, p: (b, 0, p)),
        out_shape=jax.ShapeDtypeStruct((b_n, l_n, d), BF16),
        scratch_shapes=[pltpu.VMEM((l_n, LANES), F32),
                        pltpu.VMEM((2, nc, 2, LANES + CHUNK, LANES), BF16),
                        pltpu.VMEM((2, nc, LANES, LANES), F32)],
        compiler_params=_cparams(("parallel", "parallel"), 56),
        name="rwkv_scan",
    )(rkv, rkv, rkv, lw, la, g, w0, a0, k_k.reshape(1, d), k_a.reshape(1, d), r_k.reshape(1, d),
      gn_g.reshape(1, d), gn_b.reshape(1, d))


def _pad_axis(w, axis, to):
    pad = [(0, 0)] * w.ndim
    pad[axis] = (0, to - w.shape[axis])
    return jnp.pad(w, pad)


def _rwkv7_layer(z, mod, n_ctx, mu, w_rkv, w0, w1, w2, a0, a1, a2, g1, g2, k_k, k_a, r_k, gn_g, gn_b):
    b_n, l_n, d = z.shape
    m_n = b_n * l_n
    xs = _rw_mix(z, mod, mu, n_ctx=n_ctx).reshape(6, m_n, d)
    rkv = _matmul(xs, w_rkv)
    r_w = -(-w1.shape[-1] // LANES) * LANES
    r_a = -(-a1.shape[-1] // LANES) * LANES
    lw = _lora(xs, 3, _pad_axis(w1, 2, r_w).astype(BF16), _pad_axis(w2, 1, r_w).astype(BF16), "tanh")
    la = _lora(xs, 4, _pad_axis(a1, 2, r_a).astype(BF16), _pad_axis(a2, 1, r_a).astype(BF16), None)
    gate = _lora(xs, 5, g1[None].astype(BF16), g2[None].astype(BF16), "sigmoid")
    o = _rwkv_scan(rkv.reshape(3, b_n, l_n, d), lw.reshape(2, b_n, l_n, d), la.reshape(2, b_n, l_n, d),
                   gate.reshape(b_n, l_n, d), w0, a0, k_k, k_a, r_k, gn_g, gn_b, n_ctx=n_ctx)
    return o


def _qkv_rope_kernel(a_ref, w_ref, cos_ref, sa_ref, sb_ref, o_ref, wb_ref, *, n_q, n_qk):
    j, i = pl.program_id(0), pl.program_id(1)

    @pl.when(i == 0)
    def _():
        wb_ref[...] = w_ref[...].astype(BF16)

    tn = o_ref.shape[1]
    sw = math.gcd(tn, 2 * DA_HEAD)

    def product(s):
        return jnp.dot(a_ref[...], wb_ref[:, s * sw:(s + 1) * sw], preferred_element_type=F32)

    @pl.when(j < n_qk)
    def _():
        q = DA_HEAD // 4
        q_scale = jnp.where(j < n_q, DA_HEAD ** -0.5 * math.log2(math.e), 1.0)
        cos, s_a, s_b = cos_ref[...] * q_scale, sa_ref[...] * q_scale, sb_ref[...] * q_scale

        def rotary(s, x):
            for c in range(0, sw, DA_HEAD):
                xs = x[:, c:c + DA_HEAD]
                rot = xs * cos + pltpu.roll(xs, DA_HEAD - q, 1) * s_a + pltpu.roll(xs, q, 1) * s_b
                o_ref[:, s * sw + c:s * sw + c + DA_HEAD] = rot.astype(BF16)
        x_prev = product(0)
        for s in range(1, tn // sw):
            x_next = product(s)
            rotary(s - 1, x_prev)
            x_prev = x_next
        rotary(tn // sw - 1, x_prev)

    @pl.when(j >= n_qk)
    def _():
        for s in range(tn // sw):
            o_ref[:, s * sw:(s + 1) * sw] = product(s).astype(BF16)


def _qkv_rope(h, w_qkv, cos, s_a, s_b):
    b_n, l_n, d = h.shape
    d3 = w_qkv.shape[1]
    tm = _pick(l_n, (768, 1024, 512, 256, 128, 64))
    tn = _pick(d, (1024, 512, 256, 128))
    n_t = l_n // tm
    tab = pl.BlockSpec((tm, DA_HEAD), lambda j, i: (i % n_t, 0))
    out = pl.pallas_call(
        functools.partial(_qkv_rope_kernel, n_q=d // tn, n_qk=2 * d // tn),
        grid=(d3 // tn, b_n * n_t),
        in_specs=[pl.BlockSpec((tm, d), lambda j, i: (i, 0)),
                  pl.BlockSpec((d, tn), lambda j, i: (0, j)), tab, tab, tab],
        out_specs=pl.BlockSpec((tm, tn), lambda j, i: (i, j)),
        out_shape=jax.ShapeDtypeStruct((b_n * l_n, d3), BF16),
        scratch_shapes=[pltpu.VMEM((d, tn), BF16)],
        compiler_params=_cparams(("parallel", "arbitrary"), 48),
        name="qkv_rope",
    )(h.reshape(b_n * l_n, d), w_qkv, cos, s_a, s_b)
    return out.reshape(b_n, l_n, d3)


def _attn_kernel(q_ref, k_ref, v_ref, lam_ref, sg_ref, o_ref, *, ncb, n_ctx, lam_init):
    qi = pl.program_id(2)
    lv = lam_ref[...]
    lam = (jnp.exp(jnp.sum(lv[0:1] * lv[1:2], axis=-1, keepdims=True))
           - jnp.exp(jnp.sum(lv[2:3] * lv[3:4], axis=-1, keepdims=True)) + lam_init)

    hw = 2 * DA_HEAD
    n_hh = o_ref.shape[1] // hw

    def attend(nk):
        def scores(hh, m):
            cols = slice(hh * hw + m * DA_HEAD, hh * hw + (m + 1) * DA_HEAD)
            return lax.dot_general(q_ref[:, cols], k_ref[0:nk, cols], NT, preferred_element_type=F32)
        s_all = [[scores(hh, m) for m in (0, 1)] for hh in range(n_hh)]
        for hh in range(n_hh):
            def probs(s):
                e = jnp.exp2(s - jnp.max(s, axis=-1, keepdims=True))
                return e, 1.0 / jnp.sum(e, axis=-1, keepdims=True)
            e0, i0 = probs(s_all[hh][0])
            e1, i1 = probs(s_all[hh][1])
            v = v_ref[0:nk, hh * hw:(hh + 1) * hw]
            o = (jnp.dot(e0.astype(BF16), v, preferred_element_type=F32) * i0
                 - jnp.dot(e1.astype(BF16), v, preferred_element_type=F32) * (lam * i1))
            o = o * lax.rsqrt(jnp.mean(o * o, axis=-1, keepdims=True) + 1e-5) * sg_ref[...] * (1 - lam_init)
            o_ref[:, hh * hw:(hh + 1) * hw] = o.astype(BF16)

    if ncb > 0:
        @pl.when(qi < ncb)
        def _():
            attend(n_ctx)

    @pl.when(qi >= ncb)
    def _():
        attend(k_ref.shape[0])


def _attention(qkv, lam_vec, sub_g, *, tq, n_ctx, lam_init):
    b_n, l_n, d3 = qkv.shape
    d = d3 // 3
    hw = 2 * DA_HEAD
    n_hh = 2 if (d // hw) % 2 == 0 else 1
    bw = n_hh * hw
    nh = d // bw
    return pl.pallas_call(
        functools.partial(_attn_kernel, ncb=n_ctx // tq, n_ctx=n_ctx, lam_init=lam_init),
        grid=(b_n, nh, l_n // tq),
        in_specs=[pl.BlockSpec((None, tq, bw), lambda b, h, t: (b, t, h)),
                  pl.BlockSpec((None, l_n, bw), lambda b, h, t: (b, 0, nh + h)),
                  pl.BlockSpec((None, l_n, bw), lambda b, h, t: (b, 0, 2 * nh + h)),
                  pl.BlockSpec((4, DA_HEAD), lambda b, h, t: (0, 0)),
                  pl.BlockSpec((1, hw), lambda b, h, t: (0, 0))],
        out_specs=pl.BlockSpec((None, tq, bw), lambda b, h, t: (b, t, h)),
        out_shape=jax.ShapeDtypeStruct((b_n, l_n, d), BF16),
        compiler_params=_cparams(("parallel", "parallel", "arbitrary"), 48),
        name="diff_attn",
    )(qkv, qkv, qkv, lam_vec, sub_g.reshape(1, hw))


def _rope_tables(n_ctx, n_lat):
    n_rows = n_lat // GRID_W
    row = jnp.repeat(jnp.arange(n_rows, dtype=F32), GRID_W)
    col = jnp.tile(jnp.arange(GRID_W, dtype=F32), n_rows)
    nf = DA_HEAD // 4
    inv_freq = ROPE_BASE ** (-jnp.arange(nf, dtype=F32) / nf)
    ang_r, ang_c = row[:, None] * inv_freq, col[:, None] * inv_freq
    ang = jnp.concatenate([ang_r, ang_r, ang_c, ang_c], axis=-1)
    ang = jnp.concatenate([jnp.zeros((n_ctx, DA_HEAD), F32), ang], axis=0)
    cos, sin = jnp.cos(ang), jnp.sin(ang)
    even_q = (jnp.arange(DA_HEAD) // nf) % 2 == 0
    return cos, jnp.where(even_q, -sin, 0.0), jnp.where(even_q, 0.0, sin)


def _diff_attention_layer(h, n_ctx, layer_idx, w_qkv, lam_vec, sub_g, *, tr):
    b_n, l_n, d = h.shape
    cos, s_a, s_b = _rope_tables(n_ctx, l_n - n_ctx)
    qkv = _qkv_rope(h, w_qkv, cos, s_a, s_b)
    lam_init = 0.8 - 0.6 * math.exp(-0.3 * layer_idx)
    o = _attention(qkv, lam_vec, sub_g, tq=tr, n_ctx=n_ctx, lam_init=lam_init)
    return o


def _hgrn_kernel(q_ref, i_ref, g_ref, ff_ref, fb_ref, low_ref, ng_ref, o_ref, o_scr, *, n_ctx, layer_idx):
    t_n = CHUNK
    l_n = q_ref.shape[0]
    nc, ncc = l_n // t_n, n_ctx // t_n
    r64 = lax.broadcasted_iota(jnp.int32, (t_n, t_n), 0)
    c64 = lax.broadcasted_iota(jnp.int32, (t_n, t_n), 1)

    def rows_of(c):
        return pl.ds(pl.multiple_of(c * t_n, t_n), t_n)

    f_refs = (ff_ref, fb_ref)
    lbs, incl = [], []
    tri3_b = [_cumsum_matrix(t_n, rev) for rev in (False, True)]
    for d in (0, 1):
        low = low_ref[d]
        e = jnp.exp(low - jnp.max(low, axis=0, keepdims=True))
        sm = e / jnp.sum(e, axis=0, keepdims=True)
        cs = sm[0:1]
        for rr in range(1, layer_idx + 1):
            cs = cs + sm[rr:rr + 1]
        lbs.append(cs - sm[0:1])
        incl.append((c64 >= r64) if d == 1 else (c64 <= r64))
    group = _pick(nc, (4, 2, 1))

    def stage_cum(d, c):
        rows = rows_of(c)
        f = lbs[d] + (1.0 - lbs[d]) * _sigmoid(f_refs[d][rows, :])
        return dict(d=d, rows=rows, f=f, cum=_cumsum(tri3_b[d], jnp.log(f)))

    def stage_att(s):
        d, cum, rows = s["d"], s["cum"], s["rows"]
        b_end = cum[0:1, :] if d == 1 else cum[t_n - 1:t_n, :]
        qv = q_ref[rows, :]
        qd = (qv * _sigmoid(qv) * jnp.exp(cum)).astype(BF16)
        kk = 1.0 - s["f"]
        v = i_ref[rows, :]
        kd = (kk * jnp.exp(-cum)).astype(BF16)
        ke = (kk * jnp.exp(b_end - cum)).astype(BF16)
        return dict(d=d, rows=rows, qd=qd, vb=v.astype(BF16), dec=jnp.exp(b_end),
                    att=lax.dot_general(qd, kd, NT, preferred_element_type=F32),
                    upd=jnp.dot(v.T.astype(BF16), ke, preferred_element_type=F32))

    def stage_intra(s):
        att = jnp.where(incl[s["d"]], s["att"], 0.0).astype(BF16)
        return dict(s, o=jnp.dot(att, s["vb"], preferred_element_type=F32))

    def body(i, states):
        items = [(d, _chunk_of(i * group + g, ncc, nc, d == 1)) for g in range(group) for d in (0, 1)]
        sts = [stage_cum(d, c) for d, c in items]
        sts = [stage_att(s) for s in sts]
        sts = [stage_intra(s) for s in sts]
        states = list(states)
        for s in sts:
            d = s["d"]
            o = s["o"] + lax.dot_general(s["qd"], states[d].astype(BF16), NT, preferred_element_type=F32)
            o_scr[d, s["rows"], :] = o
            states[d] = states[d] * s["dec"] + s["upd"]
        return tuple(states)

    zero = jnp.zeros((LANES, LANES), F32)
    lax.fori_loop(0, nc // group, body, (zero, zero))

    p_n = t_n * _pick(nc, (4, 3, 2, 1))

    def post(c, carry):
        rows = pl.ds(pl.multiple_of(c * p_n, p_n), p_n)
        o = o_scr[0, rows, :] + o_scr[1, rows, :]
        o = o * lax.rsqrt(jnp.mean(o * o, axis=-1, keepdims=True) + 1e-5) * ng_ref[...]
        gv = g_ref[rows, :]
        o_ref[rows, :] = (o * (gv * _sigmoid(gv))).astype(BF16)
        return carry

    lax.fori_loop(0, l_n // p_n, post, 0)


def _hgrn2_layer(h, n_ctx, layer_idx, w_in, lower, norm_g):
    b_n, l_n, d = h.shape
    m_n = b_n * l_n
    nh = d // HG_EXPAND
    proj = _mm2(h.reshape(m_n, d), w_in).reshape(b_n, l_n, 5 * d)
    col = lambda n: pl.BlockSpec((None, l_n, LANES), lambda b, p, n=n: (b, 0, n * nh + p))
    o = pl.pallas_call(
        functools.partial(_hgrn_kernel, n_ctx=n_ctx, layer_idx=layer_idx),
        grid=(b_n, nh),
        in_specs=[col(0), col(1), col(2), col(3), col(4),
                  pl.BlockSpec((2, lower.shape[1], LANES), lambda b, p: (0, 0, p)),
                  pl.BlockSpec((1, LANES), lambda b, p: (0, 0))],
        out_specs=pl.BlockSpec((None, l_n, LANES), lambda b, p: (b, 0, p)),
        out_shape=jax.ShapeDtypeStruct((b_n, l_n, d), BF16),
        scratch_shapes=[pltpu.VMEM((2, l_n, LANES), F32)],
        compiler_params=_cparams(("parallel", "parallel"), 40),
        name="hgrn_scan",
    )(proj, proj, proj, proj, proj, lower, norm_g.reshape(1, LANES))
    return o


def _gelu_tanh(x):
    return 0.5 * x * (1.0 + jnp.tanh(math.sqrt(2.0 / math.pi) * (x + 0.044715 * (x * x * x))))


def _softplus(x):
    return jnp.maximum(x, 0.0) + jnp.log1p(jnp.exp(-jnp.abs(x)))


SEG_PAD = 8


def _lin_scan(a_ref, u_ref, hl_s, cp_s, h_s, base, row0, n, h_in):
    seg = n // 8
    stride = seg + SEG_PAD
    n_p = a_ref.shape[1]
    chains = [(d, j) for d in (0, 1) for j in range(n_p)]

    def step(i, carry):
        out = []
        for (d, j), (hl, cp) in zip(chains, carry):
            idx = pl.ds(base + (i if d == 0 else seg - 1 - i), 8, stride=stride)
            a = a_ref[d, j, idx, :]
            hl = a * hl + u_ref[d, j, idx, :]
            cp = a * cp
            hl_s[d, j, idx, :] = hl
            cp_s[d, j, idx, :] = cp
            out.append((hl, cp))
        return tuple(out)

    init = tuple((jnp.zeros((8, LANES), F32), jnp.ones((8, LANES), F32)) for _ in chains)
    ends = lax.fori_loop(0, seg, step, init)
    h_out = [[None] * n_p, [None] * n_p]
    for (d, j), (hl_e, cp_e) in zip(chains, ends):
        carry = h_in[d][j]
        for s in (range(8) if d == 0 else range(7, -1, -1)):
            r0, p0 = row0 + s * seg, base + s * stride
            blk = hl_s[d, j, p0:p0 + seg, :] + cp_s[d, j, p0:p0 + seg, :] * carry
            if d == 0:
                h_s[j, r0:r0 + seg, :] = blk
            else:
                h_s[j, r0:r0 + seg, :] += blk
            carry = hl_e[s:s + 1, :] + cp_e[s:s + 1, :] * carry
        h_out[d][j] = carry
    return h_out


def _rglru_kernel(gb_ref, xb_ref, cw_ref, cb_ref, wg_ref, bg_ref, lam_ref, o_ref, a_s, u_s, h_s, hl_s, cp_s, *,
                  n_ctx):
    l_n = xb_ref.shape[0]
    n_lat = l_n - n_ctx
    x = xb_ref[...]
    row = lax.broadcasted_iota(jnp.int32, x.shape, 0)
    k_w = cw_ref.shape[0]
    xc = cb_ref[...] + sum(_seg_shift(x, row, j - (k_w - 1) // 2, n_ctx) * cw_ref[j:j + 1, :]
                           for j in range(k_w))
    xcb = xc.astype(BF16)
    n_p = x.shape[1] // LANES
    for d in (0, 1):
        gate = lambda g: _sigmoid(jnp.dot(xcb, wg_ref[d, g].astype(BF16), preferred_element_type=F32)
                                  + bg_ref[d, g:g + 1, :])
        log_a = -LR_C * gate(0) * _softplus(-lam_ref[d:d + 1, :])
        a = jnp.exp(log_a)
        u = jnp.sqrt(jnp.tanh(-log_a) * (jnp.exp(2.0 * log_a) + 1.0)) * gate(1) * xc
        base = 0
        for row0, n in ((0, n_ctx), (n_ctx, n_lat)):
            seg = n // 8
            for j in range(n_p if n else 0):
                for s in range(8):
                    src = slice(row0 + s * seg, row0 + (s + 1) * seg)
                    dst = slice(base + s * (seg + SEG_PAD), base + s * (seg + SEG_PAD) + seg)
                    a_s[d, j, dst, :] = a[src, j * LANES:(j + 1) * LANES]
                    u_s[d, j, dst, :] = u[src, j * LANES:(j + 1) * LANES]
            base += 8 * (seg + SEG_PAD) if n else 0
    h = [[jnp.zeros((1, LANES), F32)] * n_p] * 2
    base = 0
    for row0, n in ((0, n_ctx), (n_ctx, n_lat)):
        if n:
            h = _lin_scan(a_s, u_s, hl_s, cp_s, h_s, base, row0, n, h)
            base += 8 * (n // 8 + SEG_PAD)
    for j in range(n_p):
        cols = slice(j * LANES, (j + 1) * LANES)
        o_ref[:, cols] = (h_s[j, n_ctx:l_n, :] * _gelu_tanh(gb_ref[n_ctx:, cols])).astype(BF16)


def _rglru_layer(h, n_ctx, w_in, conv_w, conv_b, w_gate, b_gate, lam):
    b_n, l_n, d = h.shape
    n_lat = l_n - n_ctx
    nb = d // LR_BS
    proj = _mm2(h.reshape(b_n * l_n, d), w_in).reshape(b_n, l_n, 2 * d)
    k_w = conv_w.shape[0]
    plane = (LR_BS // LANES, l_n + 16 * SEG_PAD, LANES)
    o = pl.pallas_call(
        functools.partial(_rglru_kernel, n_ctx=n_ctx),
        grid=(b_n, nb),
        in_specs=[pl.BlockSpec((None, l_n, LR_BS), lambda b, j: (b, 0, j)),
                  pl.BlockSpec((None, l_n, LR_BS), lambda b, j: (b, 0, nb + j)),
                  pl.BlockSpec((k_w, LR_BS), lambda b, j: (0, j)),
                  pl.BlockSpec((1, LR_BS), lambda b, j: (0, j)),
                  pl.BlockSpec((2, 2, None, LR_BS, LR_BS), lambda b, j: (0, 0, j, 0, 0)),
                  pl.BlockSpec((2, 2, LR_BS), lambda b, j: (0, 0, j)),
                  pl.BlockSpec((2, LR_BS), lambda b, j: (0, j))],
        out_specs=pl.BlockSpec((None, n_lat, LR_BS), lambda b, j: (b, 0, j)),
        out_shape=jax.ShapeDtypeStruct((b_n, n_lat, d), BF16),
        scratch_shapes=[pltpu.VMEM((2,) + plane, F32), pltpu.VMEM((2,) + plane, F32), pltpu.VMEM(plane, F32),
                        pltpu.VMEM((2,) + plane, F32), pltpu.VMEM((2,) + plane, F32)],
        compiler_params=_cparams(("parallel", "parallel"), 56),
        name="rglru",
    )(proj, proj, conv_w, conv_b.reshape(1, d), w_gate, b_gate, lam)
    return o


def _ffn_up_kernel(h_ref, wg_ref, wv_ref, cg_ref, cv_ref, bg_ref, bv_ref, o_ref, u_scr, *, n_ctx, tr):
    l_n, tf = o_ref.shape
    k_w = cg_ref.shape[0]
    half = (k_w - 1) // 2
    n_buf, pad = u_scr.shape[0], (u_scr.shape[1] - l_n) // 2
    rblk = tr * _pick(l_n // tr, (3, 4, 2, 1))
    row = lax.broadcasted_iota(jnp.int32, (tr, LANES), 0)
    for p in range(n_buf):
        u_scr[p, 0:pad, :] = jnp.zeros((pad, 2 * LANES), F32)
        u_scr[p, pad + l_n:, :] = jnp.zeros((pad, 2 * LANES), F32)

    def weights(s):
        cols = slice(s * LANES, (s + 1) * LANES)
        return jnp.concatenate([wg_ref[:, cols], wv_ref[:, cols]], axis=1).astype(BF16)

    def product(s, w, r0):
        u_scr[s % n_buf, pad + r0:pad + r0 + rblk, :] = jnp.dot(h_ref[r0:r0 + rblk, :], w,
                                                                preferred_element_type=F32)

    def finish(s, r0):
        p, cols = s % n_buf, slice(s * LANES, (s + 1) * LANES)

        def conv(lane0, w_ref, b_ref):
            acc = None
            for j in range(k_w):
                sh = j - half
                x = u_scr[p, pad + r0 + sh:pad + r0 + sh + tr, lane0:lane0 + LANES]
                if sh < 0 and r0 in (0, n_ctx):
                    x = jnp.where(row < -sh, 0.0, x)
                if sh > 0 and r0 + tr in (n_ctx, l_n):
                    x = jnp.where(row >= tr - sh, 0.0, x)
                t = x * w_ref[j:j + 1, cols]
                acc = t if acc is None else acc + t
            return b_ref[:, cols] + acc
        gate = conv(0, cg_ref, bg_ref)
        val = conv(LANES, cv_ref, bv_ref)
        o_ref[r0:r0 + tr, cols] = (gate * _sigmoid(gate) * val).astype(BF16)

    n_s = tf // LANES
    for s in range(n_s + 1):
        w = weights(s) if s < n_s else None
        for r0 in range(0, l_n, rblk):
            if s < n_s:
                product(s, w, r0)
            if s > 0:
                for r1 in range(r0, r0 + rblk, tr):
                    finish(s - 1, r1)


def _conv_ffn(h, n_ctx, w_up, layer, conv_w, conv_b, *, tr):
    b_n, l_n, d = h.shape
    f = w_up.shape[2] // 2
    tf = _pick(f, (512, 256, 128))
    nf = f // tf
    k_w = conv_w.shape[0]
    cb = conv_b.reshape(1, 2 * f)
    act = pl.pallas_call(
        functools.partial(_ffn_up_kernel, n_ctx=n_ctx, tr=tr),
        grid=(b_n, nf),
        in_specs=[pl.BlockSpec((None, l_n, d), lambda b, j: (b, 0, 0), pipeline_mode=pl.Buffered(1)),
                  pl.BlockSpec((None, d, tf), lambda b, j: (layer, 0, j)),
                  pl.BlockSpec((None, d, tf), lambda b, j: (layer, 0, nf + j)),
                  pl.BlockSpec((k_w, tf), lambda b, j: (0, j)),
                  pl.BlockSpec((k_w, tf), lambda b, j: (0, nf + j)),
                  pl.BlockSpec((1, tf), lambda b, j: (0, j)),
                  pl.BlockSpec((1, tf), lambda b, j: (0, nf + j))],
        out_specs=pl.BlockSpec((None, l_n, tf), lambda b, j: (b, 0, j)),
        out_shape=jax.ShapeDtypeStruct((b_n, l_n, f), BF16),
        scratch_shapes=[pltpu.VMEM((3, l_n + 16, 2 * LANES), F32)],
        compiler_params=_cparams(("parallel", "arbitrary"), 56),
        name="ffn_up",
    )(h, w_up, w_up, conv_w, conv_w, cb, cb)
    return act


def kernel(x, c, ctx, c_ctx, ada_w, ada_b, ln_g, ln_b, ffn_w_up, ffn_conv_w, ffn_conv_b, ffn_w_down, rw_mu, rw_w_rkv, rw_w0, rw_w1, rw_w2, rw_a0, rw_a1, rw_a2, rw_g1, rw_g2, rw_k_k, rw_k_a, rw_r_k, rw_gn_g, rw_gn_b, rw_w_o, da_w_qkv, da_lambda, da_sub_g, da_w_o, hg_w_in, hg_lower, hg_norm_g, hg_w_o, lr_w_in, lr_conv_w, lr_conv_b, lr_w_gate, lr_b_gate, lr_lambda, lr_w_o):
    b_n, n_lat, d = x.shape
    n_ctx = ctx.shape[1]
    depth = ada_w.shape[0]
    assert depth == 4 and rw_mu.shape[0] == 1, "one occurrence of each of the four mixers"
    assert b_n + 1 <= 8 and n_ctx % CHUNK == 0 and n_lat % CHUNK == 0
    tr = math.gcd(math.gcd(n_ctx, n_lat), 256)
    alpha = (2 * depth) ** 0.25

    c8 = jnp.concatenate([c, c_ctx[None], jnp.zeros((8 - b_n - 1, d), F32)], axis=0)
    m = _ada(c8, ada_w, ada_b)
    m_lat = m[:, :b_n].reshape(depth, b_n, 1, 6, d)
    m_ctx = jnp.broadcast_to(m[:, b_n].reshape(depth, 1, 1, 6, d), (depth, b_n, 1, 6, d))
    mod = jnp.concatenate([m_ctx, m_lat], axis=2)

    z = jnp.concatenate([ctx, x], axis=1)
    w_down_b = _to_bf16(ffn_w_down)
    h = None
    for i in range(depth):
        last = i == depth - 1
        if i == 0:
            o = _rwkv7_layer(z, mod[0], n_ctx, rw_mu[0], rw_w_rkv[0], rw_w0[0], rw_w1[0], rw_w2[0], rw_a0[0],
                             rw_a1[0], rw_a2[0], rw_g1[0], rw_g2[0], rw_k_k[0], rw_k_a[0], rw_r_k[0],
                             rw_gn_g[0], rw_gn_b[0])
            w_o = rw_w_o
        elif i == 1:
            o = _diff_attention_layer(h, n_ctx, i, da_w_qkv[0], da_lambda[0], da_sub_g[0], tr=tr)
            w_o = da_w_o
        elif i == 2:
            o = _hgrn2_layer(h, n_ctx, i, hg_w_in[0], hg_lower, hg_norm_g[0])
            w_o = hg_w_o
        else:
            o = _rglru_layer(h, n_ctx, lr_w_in[0], lr_conv_w[0], lr_conv_b[0], lr_w_gate[0], lr_b_gate[0],
                             lr_lambda[0])
            w_o = lr_w_o
        z, h = _proj_ln(o, w_o, 0, z, mod[i], ln_g[i, 0], ln_b[i, 0], mod[i], gate_j=2, mod_j=3, tr=tr,
                        n_ctx=n_ctx, alpha=alpha)
        if last:
            n_ctx = 0
        act = _conv_ffn(h, n_ctx, ffn_w_up, i, ffn_conv_w[i], ffn_conv_b[i], tr=tr)
        z, h = _proj_ln(act, w_down_b, i, z, mod[i], ln_g[i, 1], ln_b[i, 1], mod[min(i + 1, depth - 1)],
                        gate_j=5, mod_j=None if last else 0, tr=tr, n_ctx=n_ctx, alpha=alpha)
    return z
```

```python
import functools
import math

import jax
import jax.numpy as jnp
from jax import lax
from jax.experimental import pallas as pl
from jax.experimental.pallas import tpu as pltpu

F32, BF16 = jnp.float32, jnp.bfloat16

LANES = 128
CHUNK = 64
LN_EPS = 1e-5
GRID_W = 64
ROPE_BASE = 10000.0
RW_HEAD = 64
RW_DECAY_SCALE = 0.606531
RW_GN_EPS = 64e-5
DA_HEAD = 128
HG_EXPAND = 128
LR_BS = 256
LR_C = 8.0
MIB = 1024 * 1024


def _pick(n, cands):
    for c in cands:
        if n % c == 0:
            return c
    return n


def _cparams(sem, vmem_mib):
    return pltpu.CompilerParams(dimension_semantics=sem, vmem_limit_bytes=vmem_mib * MIB)


def _sigmoid(x):
    return jax.nn.sigmoid(x)


NN = (((1,), (0,)), ((), ()))
NT = (((1,), (1,)), ((), ()))


def _parts(x, n):
    out = []
    for i in range(n):
        p = x.astype(BF16)
        out.append(p)
        if i + 1 < n:
            x = x - p.astype(F32)
    return out


def _mdot(ap, bp, dims=NN, order=2):
    pairs = [(a, b) for i, a in enumerate(ap) for j, b in enumerate(bp) if i + j < order]
    (ca,), (cb,) = dims[0]
    lhs = jnp.concatenate([a for a, _ in pairs], axis=ca) if len(pairs) > 1 else pairs[0][0]
    rhs = jnp.concatenate([b for _, b in pairs], axis=cb) if len(pairs) > 1 else pairs[0][1]
    return lax.dot_general(lhs, rhs, dims, preferred_element_type=F32)


def _cumsum_matrix(t_n, rev):
    r = lax.broadcasted_iota(jnp.int32, (t_n, 3 * t_n), 0)
    c = lax.broadcasted_iota(jnp.int32, (t_n, 3 * t_n), 1) % t_n
    return jnp.where((c >= r) if rev else (c <= r), 1.0, 0.0).astype(BF16)


def _cumsum(tri3_b, x):
    return jnp.dot(tri3_b, jnp.concatenate(_parts(x, 3), axis=0), preferred_element_type=F32)


def _cat_parts(xs, axis):
    return [jnp.concatenate(ps, axis=axis) for ps in zip(*xs)]


def _mm_kernel(a_ref, w_ref, o_ref, acc_ref, *, nk):
    if w_ref.dtype == BF16:
        prod = jnp.dot(a_ref[...], w_ref[...], preferred_element_type=F32)
    else:
        tk = w_ref.shape[0]
        n_kc = 2 if tk % (2 * LANES) == 0 else 1
        prod = None
        for c in range(n_kc):
            ks = slice(c * (tk // n_kc), (c + 1) * (tk // n_kc))
            t = jnp.dot(a_ref[:, ks], w_ref[ks, :].astype(BF16), preferred_element_type=F32)
            prod = t if prod is None else prod + t
    if nk == 1:
        o_ref[...] = prod.astype(o_ref.dtype)
    else:
        k = pl.program_id(3)

        @pl.when(k == 0)
        def _():
            acc_ref[...] = prod

        @pl.when(k > 0)
        def _():
            acc_ref[...] += prod

        @pl.when(k == nk - 1)
        def _():
            o_ref[...] = acc_ref[...].astype(o_ref.dtype)


def _mm_wres_kernel(a_ref, w_ref, o_ref, wb_ref):
    i = pl.program_id(2)
    k_n = w_ref.shape[0]
    n_kc = 4 if k_n % (4 * LANES) == 0 else 1

    @pl.when(i == 0)
    def _():
        kc = k_n // n_kc
        acc = None
        for c in range(n_kc):
            ks = slice(c * kc, (c + 1) * kc)
            wb = w_ref[ks, :].astype(BF16)
            wb_ref[ks, :] = wb
            t = jnp.dot(a_ref[:, ks], wb, preferred_element_type=F32)
            acc = t if acc is None else acc + t
        o_ref[...] = acc.astype(o_ref.dtype)

    @pl.when(i > 0)
    def _():
        o_ref[...] = jnp.dot(a_ref[...], wb_ref[...], preferred_element_type=F32).astype(o_ref.dtype)


def _matmul(a, w, *, out_dtype=F32, a_off=0):
    g_n, k_n, n_n = w.shape
    m_n = a.shape[1]
    tm = _pick(m_n, (1024, 512, 256, 128, 64))
    tn = _pick(n_n, (1024, 512, 256, 128))
    tk = k_n if k_n <= 2048 else _pick(k_n, (2816, 2048, 1024, 512))
    nk = k_n // tk
    if w.dtype == F32 and nk > 1:
        tn = _pick(n_n, (512, 256, 128))
    if w.dtype == F32 and nk == 1:
        return pl.pallas_call(
            _mm_wres_kernel,
            grid=(g_n, n_n // tn, m_n // tm),
            in_specs=[pl.BlockSpec((None, tm, k_n), lambda g, j, i: (g + a_off, i, 0)),
                      pl.BlockSpec((None, k_n, tn), lambda g, j, i: (g, 0, j))],
            out_specs=pl.BlockSpec((None, tm, tn), lambda g, j, i: (g, i, j)),
            out_shape=jax.ShapeDtypeStruct((g_n, m_n, n_n), out_dtype),
            scratch_shapes=[pltpu.VMEM((k_n, tn), BF16)],
            compiler_params=_cparams(("parallel", "parallel", "arbitrary"), 48),
            name="matmul_wres",
        )(a, w)
    return pl.pallas_call(
        functools.partial(_mm_kernel, nk=nk),
        grid=(g_n, m_n // tm, n_n // tn, nk),
        in_specs=[pl.BlockSpec((None, tm, tk), lambda g, i, j, k: (g + a_off, i, k)),
                  pl.BlockSpec((None, tk, tn), lambda g, i, j, k: (g, k, j))],
        out_specs=pl.BlockSpec((None, tm, tn), lambda g, i, j, k: (g, i, j)),
        out_shape=jax.ShapeDtypeStruct((g_n, m_n, n_n), out_dtype),
        scratch_shapes=[pltpu.VMEM((tm, tn), F32)],
        compiler_params=_cparams(("parallel", "parallel", "parallel", "arbitrary"), 48),
        name="matmul",
    )(a, w)


def _mm2(a, w, **kw):
    return _matmul(a[None], w[None], **kw)[0]


def _ada_kernel(c_ref, w_ref, b_ref, o_ref):
    c = c_ref[...]
    s = (c * _sigmoid(c)).astype(BF16)
    o_ref[...] = jnp.dot(s, w_ref[...].astype(BF16), preferred_element_type=F32) + b_ref[...]


def _ada(c8, ada_w, ada_b):
    depth, d, n = ada_w.shape
    tn = _pick(n, (1024, 512, 256, 128))
    return pl.pallas_call(
        _ada_kernel,
        grid=(depth, n // tn),
        in_specs=[pl.BlockSpec((8, d), lambda l, j: (0, 0)),
                  pl.BlockSpec((None, d, tn), lambda l, j: (l, 0, j)),
                  pl.BlockSpec((None, 1, tn), lambda l, j: (l, 0, j))],
        out_specs=pl.BlockSpec((None, 8, tn), lambda l, j: (l, 0, j)),
        out_shape=jax.ShapeDtypeStruct((depth, 8, n), F32),
        compiler_params=_cparams(("parallel", "parallel"), 40),
        name="ada",
    )(c8, ada_w, ada_b.reshape(depth, 1, n))


def _ln_mod_kernel(z_ref, y_ref, mod_ref, g_ref, b_ref, mod2_ref, *out_refs, gate_j, mod_j, alpha):
    m = mod_ref[...]
    zz = alpha * z_ref[...] + y_ref[...] * m[gate_j:gate_j + 1]
    mu = jnp.mean(zz, axis=-1, keepdims=True)
    zc = zz - mu
    var = jnp.mean(zc * zc, axis=-1, keepdims=True)
    zn = zc * lax.rsqrt(var + LN_EPS) * g_ref[...] + b_ref[...]
    out_refs[0][...] = zn
    if mod_j is not None:
        m2 = mod2_ref[...]
        out_refs[1][...] = (zn * (1 + m2[mod_j + 1:mod_j + 2]) + m2[mod_j:mod_j + 1]).astype(BF16)


def _ln_mod(z, y, mod, ln_g, ln_b, mod2, *, gate_j, mod_j, tr, n_ctx, alpha):
    b_n, l_z, d = z.shape
    l_y = y.shape[1]
    z_off = (l_z - l_y) // tr
    ncb = (n_ctx - (l_z - l_y)) // tr
    seg = lambda b, t: (b, jnp.where(t < ncb, 0, 1), 0, 0)
    row = pl.BlockSpec((None, tr, d), lambda b, t: (b, t, 0))
    out_shape = [jax.ShapeDtypeStruct((b_n, l_y, d), F32)]
    out_specs = [row]
    if mod_j is not None:
        out_shape.append(jax.ShapeDtypeStruct((b_n, l_y, d), BF16))
        out_specs.append(row)
    res = pl.pallas_call(
        functools.partial(_ln_mod_kernel, gate_j=gate_j, mod_j=mod_j, alpha=alpha),
        grid=(b_n, l_y // tr),
        in_specs=[pl.BlockSpec((None, tr, d), lambda b, t: (b, t + z_off, 0)),
                  row,
                  pl.BlockSpec((None, None, 6, d), seg),
                  pl.BlockSpec((1, d), lambda b, t: (0, 0)),
                  pl.BlockSpec((1, d), lambda b, t: (0, 0)),
                  pl.BlockSpec((None, None, 6, d), seg)],
        out_specs=out_specs,
        out_shape=out_shape,
        compiler_params=_cparams(("parallel", "parallel"), 40),
        name="ln_mod",
    )(z, y, mod, ln_g.reshape(1, d), ln_b.reshape(1, d), mod2)
    return res if mod_j is not None else (res[0], None)


def _cast_kernel(x_ref, o_ref):
    o_ref[...] = x_ref[...].astype(o_ref.dtype)


def _to_bf16(w):
    g_n, k_n, n_n = w.shape
    tk = _pick(k_n, (512, 256, 128))
    spec = pl.BlockSpec((None, tk, n_n), lambda g, i: (g, i, 0))
    return pl.pallas_call(
        _cast_kernel, grid=(g_n, k_n // tk), in_specs=[spec], out_specs=spec,
        out_shape=jax.ShapeDtypeStruct(w.shape, BF16),
        compiler_params=_cparams(("parallel", "parallel"), 32),
        name="to_bf16",
    )(w)


def _proj_ln_kernel(a_ref, w_ref, z_ref, mod_ref, g_ref, b_ref, mod2_ref, zo_ref, *ho_refs,
                    nk, gate_j, mod_j, alpha, tr, ncb):
    t, k = pl.program_id(1), pl.program_id(2)
    tm = zo_ref.shape[0]
    rb = tm // 2 if tm % 16 == 0 else tm

    def accumulate(first):
        for r0 in range(0, tm, rb):
            p = jnp.dot(a_ref[r0:r0 + rb, :], w_ref[...], preferred_element_type=F32)
            if first:
                zo_ref[r0:r0 + rb, :] = p
            else:
                zo_ref[r0:r0 + rb, :] += p

    rs = math.gcd(tr, 64)

    def finish():
        for sb in range(tm // rs):
            rows = slice(sb * rs, (sb + 1) * rs)
            is_ctx = t * (tm // tr) + (sb * rs) // tr < ncb
            m = jnp.where(is_ctx, mod_ref[0], mod_ref[1])
            zz = alpha * z_ref[rows, :] + zo_ref[rows, :] * m[gate_j:gate_j + 1]
            mu = jnp.mean(zz, axis=-1, keepdims=True)
            zc = zz - mu
            var = jnp.mean(zc * zc, axis=-1, keepdims=True)
            zn = zc * lax.rsqrt(var + LN_EPS) * g_ref[...] + b_ref[...]
            zo_ref[rows, :] = zn
            if mod_j is not None:
                m2 = jnp.where(is_ctx, mod2_ref[0], mod2_ref[1])
                ho_refs[0][rows, :] = (zn * (1 + m2[mod_j + 1:mod_j + 2]) + m2[mod_j:mod_j + 1]).astype(BF16)

    if nk == 1:
        accumulate(True)
        finish()
    else:
        @pl.when(k == 0)
        def _():
            accumulate(True)

        @pl.when((k > 0) & (k < nk - 1))
        def _():
            accumulate(False)

        @pl.when(k == nk - 1)
        def _():
            accumulate(False)
            finish()


def _proj_ln(a, w, g, z, mod, ln_g, ln_b, mod2, *, gate_j, mod_j, tr, n_ctx, alpha):
    b_n, l_a, k_n = a.shape
    d = w.shape[2]
    if z.shape[1] != l_a:
        y = _matmul(a.reshape(1, b_n * l_a, k_n), w[g:g + 1])[0].reshape(b_n, l_a, d)
        return _ln_mod(z, y, mod, ln_g, ln_b, mod2, gate_j=gate_j, mod_j=mod_j, tr=tr, n_ctx=n_ctx, alpha=alpha)
    if w.dtype != BF16:
        w = _to_bf16(w)
    tm = tr * _pick(l_a // tr, (3, 2, 1))
    tk = k_n if k_n <= 2048 else _pick(k_n, (1408, 1024, 512, 256, 128))
    nk = k_n // tk
    row = pl.BlockSpec((None, tm, d), lambda b, t, k: (b, t, 0))
    seg = pl.BlockSpec((None, 2, 6, d), lambda b, t, k: (b, 0, 0, 0))
    vec = pl.BlockSpec((1, d), lambda b, t, k: (0, 0))
    out_shape = [jax.ShapeDtypeStruct((b_n, l_a, d), F32)]
    out_specs = [row]
    if mod_j is not None:
        out_shape.append(jax.ShapeDtypeStruct((b_n, l_a, d), BF16))
        out_specs.append(row)
    res = pl.pallas_call(
        functools.partial(_proj_ln_kernel, nk=nk, gate_j=gate_j, mod_j=mod_j, alpha=alpha, tr=tr,
                          ncb=n_ctx // tr),
        grid=(b_n, l_a // tm, nk),
        in_specs=[pl.BlockSpec((None, tm, tk), lambda b, t, k: (b, t, k)),
                  pl.BlockSpec((None, tk, d), lambda b, t, k: (g, k, 0),
                               pipeline_mode=pl.Buffered(1) if nk == 1 else None),
                  row, seg, vec, vec, seg],
        out_specs=out_specs,
        out_shape=out_shape,
        compiler_params=_cparams(("parallel", "parallel", "arbitrary"), 56),
        name="proj_ln",
    )(a, w, z, mod, ln_g.reshape(1, d), ln_b.reshape(1, d), mod2)
    return res if mod_j is not None else (res[0], None)


def _seg_shift(x, row, shift, n_ctx):
    l_n = x.shape[0]
    rolled = pltpu.roll(x, (-shift) % l_n, 0)
    src = row + shift
    same_seg = (src >= 0) & (src < l_n) & ((src < n_ctx) == (row < n_ctx))
    return jnp.where(same_seg, rolled, 0.0)


def _rw_mix_kernel(z_ref, mod_ref, mu_ref, o_ref, *, n_ctx):
    z = z_ref[...]
    row = lax.broadcasted_iota(jnp.int32, z.shape, 0)
    is_ctx = row < n_ctx
    shift = jnp.where(is_ctx, mod_ref[0, 0:1, :], mod_ref[1, 0:1, :])
    scale = jnp.where(is_ctx, mod_ref[0, 1:2, :], mod_ref[1, 1:2, :])
    h = z * (1 + scale) + shift
    dx = 0.5 * (_seg_shift(h, row, -1, n_ctx) + _seg_shift(h, row, 1, n_ctx)) - h
    for n in range(6):
        o_ref[n] = (h + dx * mu_ref[n:n + 1, :]).astype(BF16)


def _rw_mix(z, mod, mu, *, n_ctx):
    b_n, l_n, d = z.shape
    tc = _pick(d, (256, 128))
    return pl.pallas_call(
        functools.partial(_rw_mix_kernel, n_ctx=n_ctx),
        grid=(b_n, d // tc),
        in_specs=[pl.BlockSpec((None, l_n, tc), lambda b, j: (b, 0, j)),
                  pl.BlockSpec((None, 2, 6, tc), lambda b, j: (b, 0, 0, j)),
                  pl.BlockSpec((6, tc), lambda b, j: (0, j))],
        out_specs=pl.BlockSpec((6, None, l_n, tc), lambda b, j: (0, b, 0, j)),
        out_shape=jax.ShapeDtypeStruct((6, b_n, l_n, d), BF16),
        compiler_params=_cparams(("parallel", "parallel"), 48),
        name="rw_mix",
    )(z, mod, mu)


def _lora_kernel(x_ref, a_ref, b_ref, o_ref, *, act):
    t = jnp.dot(x_ref[...], a_ref[...], preferred_element_type=F32)
    if act == "tanh":
        t = jnp.tanh(t)
    elif act == "sigmoid":
        t = _sigmoid(t)
    o_ref[...] = jnp.dot(t.astype(BF16), b_ref[...], preferred_element_type=F32)


def _lora(xs, x_idx, a, b, act):
    g_n, d, r = a.shape
    m_n = xs.shape[1]
    tm = _pick(m_n, (512, 256, 128, 64))
    return pl.pallas_call(
        functools.partial(_lora_kernel, act=act),
        grid=(g_n, m_n // tm),
        in_specs=[pl.BlockSpec((None, tm, d), lambda g, i: (x_idx, i, 0)),
                  pl.BlockSpec((None, d, r), lambda g, i: (g, 0, 0)),
                  pl.BlockSpec((None, r, d), lambda g, i: (g, 0, 0))],
        out_specs=pl.BlockSpec((None, tm, d), lambda g, i: (g, i, 0)),
        out_shape=jax.ShapeDtypeStruct((g_n, m_n, d), F32),
        compiler_params=_cparams(("parallel", "parallel"), 40),
        name="lora",
    )(xs, a, b)


def _chunk_of(q, ncc, nc, rev):
    if not rev:
        return q
    return jnp.where(q < ncc, ncc - 1 - q, nc - 1 - (q - ncc))


def _rwkv_kernel(r_ref, k_ref, v_ref, lw_ref, la_ref, g_ref, w0_ref, a0_ref, kk_ref, ka_ref, rk_ref,
                 gng_ref, gnb_ref, o_ref,
                 y_scr, mr_s, n_s, *, n_ctx):
    t_n = CHUNK
    h2 = 2 * t_n
    l_n = r_ref.shape[0]
    nc, ncc = l_n // t_n, n_ctx // t_n
    group = _pick(nc, (6, 4, 3, 2, 1))
    lane = lax.broadcasted_iota(jnp.int32, (1, LANES), 1)
    m1 = jnp.where(lane < RW_HEAD, 1.0, 0.0)
    m2 = 1.0 - m1
    ri = lax.broadcasted_iota(jnp.int32, (LANES, LANES), 0)
    ci = lax.broadcasted_iota(jnp.int32, (LANES, LANES), 1)
    same_head = (ri // RW_HEAD) == (ci // RW_HEAD)
    gsum_b = jnp.where(same_head, 1.0, 0.0).astype(BF16)
    gavg_b = jnp.where(same_head, 1.0 / RW_HEAD, 0.0).astype(BF16)
    eye = jnp.where(ri == ci, 1.0, 0.0)
    tr_i, tc_i = ri % t_n, ci % t_n
    k_k, k_a = kk_ref[...], ka_ref[...]

    def stack(x):
        return jnp.concatenate([x * m1, x * m2], axis=0)

    def rows_of(c):
        return pl.ds(pl.multiple_of(c * t_n, t_n), t_n)

    def head_sum(x, w_b):
        return _mdot(_parts(x, 2), [w_b], order=2)

    tri3_b = [_cumsum_matrix(t_n, rev) for rev in (False, True)]
    strict = [(tc_i > tr_i) if rev else (tc_i < tr_i) for rev in (False, True)]
    incl = [(tc_i >= tr_i) if rev else (tc_i <= tr_i) for rev in (False, True)]

    def stage_prep(d, c):
        rows = rows_of(c)
        k, r, v = k_ref[rows, :], r_ref[rows, :], v_ref[rows, :]
        kkr = k * k_k
        both = dict(kkr=kkr, ss=head_sum(kkr * kkr, gsum_b))
        lw = -RW_DECAY_SCALE * _sigmoid(w0_ref[d:d + 1, :] + lw_ref[d, rows, :])
        a = _sigmoid(a0_ref[d:d + 1, :] + la_ref[d, rows, :])
        return dict(d=d, c=c, k=k, r=r, v=v, lw=lw, a=a, both=both, cum=_cumsum(tri3_b[d], lw))

    def stage_amat(s):
        d, cum, lw, a, both = s["d"], s["cum"], s["lw"], s["a"], s["both"]
        if "kk" not in both:
            both["kk"] = both["kkr"] * lax.rsqrt(both["ss"] + 1e-12)
            both["vp"] = _parts(stack(s["v"]), 1)
        kk = both["kk"]
        kd = s["k"] * (1 + (a - 1) * k_a)
        bv = kk * a
        p_end = cum[0:1, :] if d == 1 else cum[t_n - 1:t_n, :]
        e_m = jnp.exp(-cum)
        e_h = jnp.exp(p_end - cum)
        ktp = _parts(stack(kk * jnp.exp(cum - lw)), 2)
        rt = stack(s["r"] * jnp.exp(cum))
        k2p = _cat_parts([_parts(stack(bv * e_m), 1), _parts(stack(kd * e_m), 1)], 0)
        return dict(d=d, c=s["c"], ktp=ktp, rt=rt, vp=both["vp"], p_end=p_end,
                    bh=stack(bv * e_h), kh=stack(kd * e_h),
                    amat=_mdot(_cat_parts([ktp, _parts(rt, 2)], 0), k2p, NT))

    def stage_square(s):
        d, amat = s["d"], s["amat"]
        lt = jnp.where(strict[d], amat[:h2, :h2], 0.0).T
        ltp = _parts(lt, 2)
        msk = jnp.concatenate([jnp.where(strict[d], amat[:h2, h2:], 0.0),
                               jnp.where(incl[d], amat[h2:, h2:], 0.0)], axis=0)
        s = dict(s, pt=eye - lt, xt=_mdot(ltp, ltp[:1]),
                 av=_mdot(_parts(msk, 2), s["vp"][:1]),
                 arbp=_parts(jnp.where(incl[d], amat[h2:, :h2], 0.0), 2))
        del s["amat"]
        return s

    def stage_double(s, final):
        xh = _parts(s["xt"], 1)
        ptp = _parts(s["pt"], 2)
        if final:
            return dict(s, pt=s["pt"] + _mdot(xh, ptp))
        rhs = [jnp.concatenate([ptp[0], xh[0]], axis=1), jnp.concatenate([ptp[1], jnp.zeros_like(xh[0])], axis=1)]
        both = _mdot(xh, rhs)
        return dict(s, pt=s["pt"] + both[:, :LANES], xt=both[:, LANES:])

    def stage_solve(s):
        rhs = jnp.concatenate([s["ktp"][0], (-s["av"][:h2]).astype(BF16)], axis=1)
        return dict(s, wub=_mdot(_parts(s["pt"].T, 2), [rhs]).astype(BF16))

    def stage_fold(s):
        d, c, wub = s["d"], s["c"], s["wub"]
        aw = _mdot(s["arbp"], [wub])
        zb = jnp.zeros((h2, LANES), BF16)
        lhs = _cat_parts([_parts(s["bh"].T, 2), _parts(s["kh"].T, 2)], 1)
        rhs = jnp.concatenate([wub, jnp.concatenate([zb, s["vp"][0]], axis=1)], axis=0)
        mn = _mdot(lhs, [rhs])
        dg = jnp.where(ri == ci, jnp.broadcast_to(jnp.exp(s["p_end"]), (LANES, LANES)), 0.0)
        rp = s["rt"] - aw[:, :LANES]
        mrp = _parts(jnp.concatenate([dg - mn[:, :LANES], rp[:t_n] + rp[t_n:]], axis=0), 2)
        for i in range(2):
            mr_s[d, c, i] = mrp[i]
        n_s[d, c] = mn[:, LANES:]
        y0 = s["av"][h2:] + aw[:, LANES:]
        return y0[:t_n] + y0[t_n:]

    def seq(q, hs):
        cs = (q, _chunk_of(q, ncc, nc, True))
        mh = [_mdot([mr_s[d, cs[d], 0], mr_s[d, cs[d], 1]], _parts(hs[d], 2)) for d in (0, 1)]
        for d in (0, 1):
            y_scr[rows_of(cs[d]), :] += mh[d][h2:]
        return tuple(mh[d][:h2] + n_s[d, cs[d]] for d in (0, 1))

    def local(i, hs, with_seq):
        steps = [i * group + g for g in range(group)]
        pending = [q - group for q in steps] if with_seq else []
        sts = [stage_prep(d, q if d == 0 else _chunk_of(q, ncc, nc, True)) for q in steps for d in (0, 1)]
        stages = ([stage_amat, stage_square] + [functools.partial(stage_double, final=f) for f in (False,) * 4 + (True,)]
                  + [stage_solve])
        for stage in stages:
            sts = [stage(s) for s in sts]
            if pending:
                hs = seq(pending.pop(0), hs)
        for s in sts:
            y0 = stage_fold(s)
            y_scr[rows_of(s["c"]), :] += y0
        while pending:
            hs = seq(pending.pop(0), hs)
        return hs

    y_scr[...] = jnp.zeros(y_scr.shape, F32)
    zero = jnp.zeros((LANES, LANES), F32)
    n_trip = nc // group
    hs = local(0, (zero, zero), False)
    hs = lax.fori_loop(1, n_trip, functools.partial(local, with_seq=True), hs)
    for q in range((n_trip - 1) * group, nc):
        hs = seq(q, hs)

    n_post = _pick(nc, (9, 6, 4, 3, 2, 1))

    def post(i, carry):
        rows = [rows_of(i * n_post + g) for g in range(n_post)]

        def bonus_sum(rw):
            k, r = k_ref[rw, :], r_ref[rw, :]
            kd_f = k * (1 + (_sigmoid(a0_ref[0:1, :] + la_ref[0, rw, :]) - 1) * k_a)
            kd_b = k * (1 + (_sigmoid(a0_ref[1:2, :] + la_ref[1, rw, :]) - 1) * k_a)
            return head_sum(r * (kd_f + kd_b) * rk_ref[...], gsum_b)
        bsum = [bonus_sum(rw) for rw in rows]
        ys = [y_scr[rw, :] for rw in rows]
        ycs = [y - m for y, m in zip(ys, [head_sum(y, gavg_b) for y in ys])]
        var = [head_sum(yc * yc, gavg_b) for yc in ycs]
        for rw, yc, vr, bs in zip(rows, ycs, var, bsum):
            yn = yc * lax.rsqrt(vr + RW_GN_EPS) * gng_ref[...] + gnb_ref[...]
            o_ref[rw, :] = ((yn + bs * v_ref[rw, :]) * g_ref[rw, :]).astype(BF16)
        return carry

    lax.fori_loop(0, nc // n_post, post, 0)


def _rwkv_scan(rkv, lw, la, g, w0, a0, k_k, k_a, r_k, gn_g, gn_b, *, n_ctx):
    _, b_n, l_n, d = rkv.shape
    nc = l_n // CHUNK
    col = lambda n: pl.BlockSpec((None, None, l_n, LANES), lambda b, p, n=n: (n, b, 0, p))
    two = pl.BlockSpec((2, None, l_n, LANES), lambda b, p: (0, b, 0, p))
    par = lambda rows: pl.BlockSpec((rows, LANES), lambda b, p: (0, p))
    return pl.pallas_call(
        functools.partial(_rwkv_kernel, n_ctx=n_ctx),
        grid=(b_n, d // LANES),
        in_specs=[col(0), col(1), col(2), two, two,
                  pl.BlockSpec((None, l_n, LANES), lambda b, p: (b, 0, p)),
                  par(2), par(2), par(1), par(1), par(1), par(1), par(1)],
        out_specs=pl.BlockSpec((None, l_n, LANES), lambda b, p: (b, 0, p)),
        out_shape=jax.ShapeDtypeStruct((b_n, l_n, d), BF16),
        scratch_shapes=[pltpu.VMEM((l_n, LANES), F32),
                        pltpu.VMEM((2, nc, 2, LANES + CHUNK, LANES), BF16),
                        pltpu.VMEM((2, nc, LANES, LANES), F32)],
        compiler_params=_cparams(("parallel", "parallel"), 56),
        name="rwkv_scan",
    )(rkv, rkv, rkv, lw, la, g, w0, a0, k_k.reshape(1, d), k_a.reshape(1, d), r_k.reshape(1, d),
      gn_g.reshape(1, d), gn_b.reshape(1, d))


def _pad_axis(w, axis, to):
    pad = [(0, 0)] * w.ndim
    pad[axis] = (0, to - w.shape[axis])
    return jnp.pad(w, pad)


def _rwkv7_layer(z, mod, n_ctx, mu, w_rkv, w0, w1, w2, a0, a1, a2, g1, g2, k_k, k_a, r_k, gn_g, gn_b):
    b_n, l_n, d = z.shape
    m_n = b_n * l_n
    xs = _rw_mix(z, mod, mu, n_ctx=n_ctx).reshape(6, m_n, d)
    rkv = _matmul(xs, w_rkv)
    r_w = -(-w1.shape[-1] // LANES) * LANES
    r_a = -(-a1.shape[-1] // LANES) * LANES
    lw = _lora(xs, 3, _pad_axis(w1, 2, r_w).astype(BF16), _pad_axis(w2, 1, r_w).astype(BF16), "tanh")
    la = _lora(xs, 4, _pad_axis(a1, 2, r_a).astype(BF16), _pad_axis(a2, 1, r_a).astype(BF16), None)
    gate = _lora(xs, 5, g1[None].astype(BF16), g2[None].astype(BF16), "sigmoid")
    o = _rwkv_scan(rkv.reshape(3, b_n, l_n, d), lw.reshape(2, b_n, l_n, d), la.reshape(2, b_n, l_n, d),
                   gate.reshape(b_n, l_n, d), w0, a0, k_k, k_a, r_k, gn_g, gn_b, n_ctx=n_ctx)
    return o


def _qkv_rope_kernel(a_ref, w_ref, cos_ref, sa_ref, sb_ref, o_ref, wb_ref, *, n_q, n_qk):
    j, i = pl.program_id(0), pl.program_id(1)

    @pl.when(i == 0)
    def _():
        wb_ref[...] = w_ref[...].astype(BF16)

    tn = o_ref.shape[1]
    sw = math.gcd(tn, 2 * DA_HEAD)

    def product(s):
        return jnp.dot(a_ref[...], wb_ref[:, s * sw:(s + 1) * sw], preferred_element_type=F32)

    @pl.when(j < n_qk)
    def _():
        q = DA_HEAD // 4
        q_scale = jnp.where(j < n_q, DA_HEAD ** -0.5 * math.log2(math.e), 1.0)
        cos, s_a, s_b = cos_ref[...] * q_scale, sa_ref[...] * q_scale, sb_ref[...] * q_scale

        def rotary(s, x):
            for c in range(0, sw, DA_HEAD):
                xs = x[:, c:c + DA_HEAD]
                rot = xs * cos + pltpu.roll(xs, DA_HEAD - q, 1) * s_a + pltpu.roll(xs, q, 1) * s_b
                o_ref[:, s * sw + c:s * sw + c + DA_HEAD] = rot.astype(BF16)
        x_prev = product(0)
        for s in range(1, tn // sw):
            x_next = product(s)
            rotary(s - 1, x_prev)
            x_prev = x_next
        rotary(tn // sw - 1, x_prev)

    @pl.when(j >= n_qk)
    def _():
        for s in range(tn // sw):
            o_ref[:, s * sw:(s + 1) * sw] = product(s).astype(BF16)


def _qkv_rope(h, w_qkv, cos, s_a, s_b):
    b_n, l_n, d = h.shape
    d3 = w_qkv.shape[1]
    tm = _pick(l_n, (768, 1024, 512, 256, 128, 64))
    tn = _pick(d, (1024, 512, 256, 128))
    n_t = l_n // tm
    tab = pl.BlockSpec((tm, DA_HEAD), lambda j, i: (i % n_t, 0))
    out = pl.pallas_call(
        functools.partial(_qkv_rope_kernel, n_q=d // tn, n_qk=2 * d // tn),
        grid=(d3 // tn, b_n * n_t),
        in_specs=[pl.BlockSpec((tm, d), lambda j, i: (i, 0)),
                  pl.BlockSpec((d, tn), lambda j, i: (0, j)), tab, tab, tab],
        out_specs=pl.BlockSpec((tm, tn), lambda j, i: (i, j)),
        out_shape=jax.ShapeDtypeStruct((b_n * l_n, d3), BF16),
        scratch_shapes=[pltpu.VMEM((d, tn), BF16)],
        compiler_params=_cparams(("parallel", "arbitrary"), 48),
        name="qkv_rope",
    )(h.reshape(b_n * l_n, d), w_qkv, cos, s_a, s_b)
    return out.reshape(b_n, l_n, d3)


def _attn_kernel(q_ref, k_ref, v_ref, lam_ref, sg_ref, o_ref, *, ncb, n_ctx, lam_init):
    qi = pl.program_id(2)
    lv = lam_ref[...]
    lam = (jnp.exp(jnp.sum(lv[0:1] * lv[1:2], axis=-1, keepdims=True))
           - jnp.exp(jnp.sum(lv[2:3] * lv[3:4], axis=-1, keepdims=True)) + lam_init)

    hw = 2 * DA_HEAD
    n_hh = o_ref.shape[1] // hw

    def attend(nk):
        def scores(hh, m):
            cols = slice(hh * hw + m * DA_HEAD, hh * hw + (m + 1) * DA_HEAD)
            return lax.dot_general(q_ref[:, cols], k_ref[0:nk, cols], NT, preferred_element_type=F32)
        s_all = [[scores(hh, m) for m in (0, 1)] for hh in range(n_hh)]
        for hh in range(n_hh):
            def probs(s):
                e = jnp.exp2(s - jnp.max(s, axis=-1, keepdims=True))
                return e, 1.0 / jnp.sum(e, axis=-1, keepdims=True)
            e0, i0 = probs(s_all[hh][0])
            e1, i1 = probs(s_all[hh][1])
            v = v_ref[0:nk, hh * hw:(hh + 1) * hw]
            o = (jnp.dot(e0.astype(BF16), v, preferred_element_type=F32) * i0
                 - jnp.dot(e1.astype(BF16), v, preferred_element_type=F32) * (lam * i1))
            o = o * lax.rsqrt(jnp.mean(o * o, axis=-1, keepdims=True) + 1e-5) * sg_ref[...] * (1 - lam_init)
            o_ref[:, hh * hw:(hh + 1) * hw] = o.astype(BF16)

    if ncb > 0:
        @pl.when(qi < ncb)
        def _():
            attend(n_ctx)

    @pl.when(qi >= ncb)
    def _():
        attend(k_ref.shape[0])


def _attention(qkv, lam_vec, sub_g, *, tq, n_ctx, lam_init):
    b_n, l_n, d3 = qkv.shape
    d = d3 // 3
    hw = 2 * DA_HEAD
    n_hh = 2 if (d // hw) % 2 == 0 else 1
    bw = n_hh * hw
    nh = d // bw
    return pl.pallas_call(
        functools.partial(_attn_kernel, ncb=n_ctx // tq, n_ctx=n_ctx, lam_init=lam_init),
        grid=(b_n, nh, l_n // tq),
        in_specs=[pl.BlockSpec((None, tq, bw), lambda b, h, t: (b, t, h)),
                  pl.BlockSpec((None, l_n, bw), lambda b, h, t: (b, 0, nh + h)),
                  pl.BlockSpec((None, l_n, bw), lambda b, h, t: (b, 0, 2 * nh + h)),
                  pl.BlockSpec((4, DA_HEAD), lambda b, h, t: (0, 0)),
                  pl.BlockSpec((1, hw), lambda b, h, t: (0, 0))],
        out_specs=pl.BlockSpec((None, tq, bw), lambda b, h, t: (b, t, h)),
        out_shape=jax.ShapeDtypeStruct((b_n, l_n, d), BF16),
        compiler_params=_cparams(("parallel", "parallel", "arbitrary"), 48),
        name="diff_attn",
    )(qkv, qkv, qkv, lam_vec, sub_g.reshape(1, hw))


def _rope_tables(n_ctx, n_lat):
    n_rows = n_lat // GRID_W
    row = jnp.repeat(jnp.arange(n_rows, dtype=F32), GRID_W)
    col = jnp.tile(jnp.arange(GRID_W, dtype=F32), n_rows)
    nf = DA_HEAD // 4
    inv_freq = ROPE_BASE ** (-jnp.arange(nf, dtype=F32) / nf)
    ang_r, ang_c = row[:, None] * inv_freq, col[:, None] * inv_freq
    ang = jnp.concatenate([ang_r, ang_r, ang_c, ang_c], axis=-1)
    ang = jnp.concatenate([jnp.zeros((n_ctx, DA_HEAD), F32), ang], axis=0)
    cos, sin = jnp.cos(ang), jnp.sin(ang)
    even_q = (jnp.arange(DA_HEAD) // nf) % 2 == 0
    return cos, jnp.where(even_q, -sin, 0.0), jnp.where(even_q, 0.0, sin)


def _diff_attention_layer(h, n_ctx, layer_idx, w_qkv, lam_vec, sub_g, *, tr):
    b_n, l_n, d = h.shape
    cos, s_a, s_b = _rope_tables(n_ctx, l_n - n_ctx)
    qkv = _qkv_rope(h, w_qkv, cos, s_a, s_b)
    lam_init = 0.8 - 0.6 * math.exp(-0.3 * layer_idx)
    o = _attention(qkv, lam_vec, sub_g, tq=tr, n_ctx=n_ctx, lam_init=lam_init)
    return o


def _hgrn_kernel(q_ref, i_ref, g_ref, ff_ref, fb_ref, low_ref, ng_ref, o_ref, o_scr, *, n_ctx, layer_idx):
    t_n = CHUNK
    l_n = q_ref.shape[0]
    nc, ncc = l_n // t_n, n_ctx // t_n
    r64 = lax.broadcasted_iota(jnp.int32, (t_n, t_n), 0)
    c64 = lax.broadcasted_iota(jnp.int32, (t_n, t_n), 1)

    def rows_of(c):
        return pl.ds(pl.multiple_of(c * t_n, t_n), t_n)

    f_refs = (ff_ref, fb_ref)
    lbs, incl = [], []
    tri3_b = [_cumsum_matrix(t_n, rev) for rev in (False, True)]
    for d in (0, 1):
        low = low_ref[d]
        e = jnp.exp(low - jnp.max(low, axis=0, keepdims=True))
        sm = e / jnp.sum(e, axis=0, keepdims=True)
        cs = sm[0:1]
        for rr in range(1, layer_idx + 1):
            cs = cs + sm[rr:rr + 1]
        lbs.append(cs - sm[0:1])
        incl.append((c64 >= r64) if d == 1 else (c64 <= r64))
    group = _pick(nc, (6, 4, 2, 1))

    def stage_cum(d, c):
        rows = rows_of(c)
        f = lbs[d] + (1.0 - lbs[d]) * _sigmoid(f_refs[d][rows, :])
        return dict(d=d, rows=rows, f=f, cum=_cumsum(tri3_b[d], jnp.log(f)))

    def stage_att(s):
        d, cum, rows = s["d"], s["cum"], s["rows"]
        b_end = cum[0:1, :] if d == 1 else cum[t_n - 1:t_n, :]
        qv = q_ref[rows, :]
        qd = (qv * _sigmoid(qv) * jnp.exp(cum)).astype(BF16)
        kk = 1.0 - s["f"]
        v = i_ref[rows, :]
        kd = (kk * jnp.exp(-cum)).astype(BF16)
        ke = (kk * jnp.exp(b_end - cum)).astype(BF16)
        return dict(d=d, rows=rows, qd=qd, vb=v.astype(BF16), dec=jnp.exp(b_end),
                    att=lax.dot_general(qd, kd, NT, preferred_element_type=F32),
                    upd=jnp.dot(v.T.astype(BF16), ke, preferred_element_type=F32))

    def stage_intra(s):
        att = jnp.where(incl[s["d"]], s["att"], 0.0).astype(BF16)
        return dict(s, o=jnp.dot(att, s["vb"], preferred_element_type=F32))

    def body(i, states):
        items = [(d, _chunk_of(i * group + g, ncc, nc, d == 1)) for g in range(group) for d in (0, 1)]
        sts = [stage_cum(d, c) for d, c in items]
        sts = [stage_att(s) for s in sts]
        sts = [stage_intra(s) for s in sts]
        states = list(states)
        for s in sts:
            d = s["d"]
            o = s["o"] + lax.dot_general(s["qd"], states[d].astype(BF16), NT, preferred_element_type=F32)
            o_scr[d, s["rows"], :] = o
            states[d] = states[d] * s["dec"] + s["upd"]
        return tuple(states)

    zero = jnp.zeros((LANES, LANES), F32)
    lax.fori_loop(0, nc // group, body, (zero, zero))

    p_n = t_n * _pick(nc, (4, 3, 2, 1))

    def post(c, carry):
        rows = pl.ds(pl.multiple_of(c * p_n, p_n), p_n)
        o = o_scr[0, rows, :] + o_scr[1, rows, :]
        o = o * lax.rsqrt(jnp.mean(o * o, axis=-1, keepdims=True) + 1e-5) * ng_ref[...]
        gv = g_ref[rows, :]
        o_ref[rows, :] = (o * (gv * _sigmoid(gv))).astype(BF16)
        return carry

    lax.fori_loop(0, l_n // p_n, post, 0)


def _hgrn2_layer(h, n_ctx, layer_idx, w_in, lower, norm_g):
    b_n, l_n, d = h.shape
    m_n = b_n * l_n
    nh = d // HG_EXPAND
    proj = _mm2(h.reshape(m_n, d), w_in).reshape(b_n, l_n, 5 * d)
    col = lambda n: pl.BlockSpec((None, l_n, LANES), lambda b, p, n=n: (b, 0, n * nh + p))
    o = pl.pallas_call(
        functools.partial(_hgrn_kernel, n_ctx=n_ctx, layer_idx=layer_idx),
        grid=(b_n, nh),
        in_specs=[col(0), col(1), col(2), col(3), col(4),
                  pl.BlockSpec((2, lower.shape[1], LANES), lambda b, p: (0, 0, p)),
                  pl.BlockSpec((1, LANES), lambda b, p: (0, 0))],
        out_specs=pl.BlockSpec((None, l_n, LANES), lambda b, p: (b, 0, p)),
        out_shape=jax.ShapeDtypeStruct((b_n, l_n, d), BF16),
        scratch_shapes=[pltpu.VMEM((2, l_n, LANES), F32)],
        compiler_params=_cparams(("parallel", "parallel"), 40),
        name="hgrn_scan",
    )(proj, proj, proj, proj, proj, lower, norm_g.reshape(1, LANES))
    return o


def _gelu_tanh(x):
    return 0.5 * x * (1.0 + jnp.tanh(math.sqrt(2.0 / math.pi) * (x + 0.044715 * (x * x * x))))


def _softplus(x):
    return jnp.maximum(x, 0.0) + jnp.log1p(jnp.exp(-jnp.abs(x)))


SEG_PAD = 8


def _lin_scan(a_ref, u_ref, hl_s, cp_s, h_s, base, row0, n, h_in):
    seg = n // 8
    stride = seg + SEG_PAD
    n_p = a_ref.shape[1]
    chains = [(d, j) for d in (0, 1) for j in range(n_p)]

    def step(i, carry):
        out = []
        for (d, j), (hl, cp) in zip(chains, carry):
            idx = pl.ds(base + (i if d == 0 else seg - 1 - i), 8, stride=stride)
            a = a_ref[d, j, idx, :]
            hl = a * hl + u_ref[d, j, idx, :]
            cp = a * cp
            hl_s[d, j, idx, :] = hl
            cp_s[d, j, idx, :] = cp
            out.append((hl, cp))
        return tuple(out)

    init = tuple((jnp.zeros((8, LANES), F32), jnp.ones((8, LANES), F32)) for _ in chains)
    ends = lax.fori_loop(0, seg, step, init)
    h_out = [[None] * n_p, [None] * n_p]
    for (d, j), (hl_e, cp_e) in zip(chains, ends):
        carry = h_in[d][j]
        for s in (range(8) if d == 0 else range(7, -1, -1)):
            r0, p0 = row0 + s * seg, base + s * stride
            blk = hl_s[d, j, p0:p0 + seg, :] + cp_s[d, j, p0:p0 + seg, :] * carry
            if d == 0:
                h_s[j, r0:r0 + seg, :] = blk
            else:
                h_s[j, r0:r0 + seg, :] += blk
            carry = hl_e[s:s + 1, :] + cp_e[s:s + 1, :] * carry
        h_out[d][j] = carry
    return h_out


def _rglru_kernel(gb_ref, xb_ref, cw_ref, cb_ref, wg_ref, bg_ref, lam_ref, o_ref, a_s, u_s, h_s, hl_s, cp_s, *,
                  n_ctx):
    l_n = xb_ref.shape[0]
    n_lat = l_n - n_ctx
    x = xb_ref[...]
    row = lax.broadcasted_iota(jnp.int32, x.shape, 0)
    k_w = cw_ref.shape[0]
    xc = cb_ref[...] + sum(_seg_shift(x, row, j - (k_w - 1) // 2, n_ctx) * cw_ref[j:j + 1, :]
                           for j in range(k_w))
    xcb = xc.astype(BF16)
    n_p = x.shape[1] // LANES
    for d in (0, 1):
        gate = lambda g: _sigmoid(jnp.dot(xcb, wg_ref[d, g].astype(BF16), preferred_element_type=F32)
                                  + bg_ref[d, g:g + 1, :])
        log_a = -LR_C * gate(0) * _softplus(-lam_ref[d:d + 1, :])
        a = jnp.exp(log_a)
        u = jnp.sqrt(jnp.tanh(-log_a) * (jnp.exp(2.0 * log_a) + 1.0)) * gate(1) * xc
        base = 0
        for row0, n in ((0, n_ctx), (n_ctx, n_lat)):
            seg = n // 8
            for j in range(n_p if n else 0):
                for s in range(8):
                    src = slice(row0 + s * seg, row0 + (s + 1) * seg)
                    dst = slice(base + s * (seg + SEG_PAD), base + s * (seg + SEG_PAD) + seg)
                    a_s[d, j, dst, :] = a[src, j * LANES:(j + 1) * LANES]
                    u_s[d, j, dst, :] = u[src, j * LANES:(j + 1) * LANES]
            base += 8 * (seg + SEG_PAD) if n else 0
    h = [[jnp.zeros((1, LANES), F32)] * n_p] * 2
    base = 0
    for row0, n in ((0, n_ctx), (n_ctx, n_lat)):
        if n:
            h = _lin_scan(a_s, u_s, hl_s, cp_s, h_s, base, row0, n, h)
            base += 8 * (n // 8 + SEG_PAD)
    for j in range(n_p):
        cols = slice(j * LANES, (j + 1) * LANES)
        o_ref[:, cols] = (h_s[j, n_ctx:l_n, :] * _gelu_tanh(gb_ref[n_ctx:, cols])).astype(BF16)


def _rglru_layer(h, n_ctx, w_in, conv_w, conv_b, w_gate, b_gate, lam):
    b_n, l_n, d = h.shape
    n_lat = l_n - n_ctx
    nb = d // LR_BS
    proj = _mm2(h.reshape(b_n * l_n, d), w_in).reshape(b_n, l_n, 2 * d)
    k_w = conv_w.shape[0]
    plane = (LR_BS // LANES, l_n + 16 * SEG_PAD, LANES)
    o = pl.pallas_call(
        functools.partial(_rglru_kernel, n_ctx=n_ctx),
        grid=(b_n, nb),
        in_specs=[pl.BlockSpec((None, l_n, LR_BS), lambda b, j: (b, 0, j)),
                  pl.BlockSpec((None, l_n, LR_BS), lambda b, j: (b, 0, nb + j)),
                  pl.BlockSpec((k_w, LR_BS), lambda b, j: (0, j)),
                  pl.BlockSpec((1, LR_BS), lambda b, j: (0, j)),
                  pl.BlockSpec((2, 2, None, LR_BS, LR_BS), lambda b, j: (0, 0, j, 0, 0)),
                  pl.BlockSpec((2, 2, LR_BS), lambda b, j: (0, 0, j)),
                  pl.BlockSpec((2, LR_BS), lambda b, j: (0, j))],
        out_specs=pl.BlockSpec((None, n_lat, LR_BS), lambda b, j: (b, 0, j)),
        out_shape=jax.ShapeDtypeStruct((b_n, n_lat, d), BF16),
        scratch_shapes=[pltpu.VMEM((2,) + plane, F32), pltpu.VMEM((2,) + plane, F32), pltpu.VMEM(plane, F32),
                        pltpu.VMEM((2,) + plane, F32), pltpu.VMEM((2,) + plane, F32)],
        compiler_params=_cparams(("parallel", "parallel"), 56),
        name="rglru",
    )(proj, proj, conv_w, conv_b.reshape(1, d), w_gate, b_gate, lam)
    return o


def _ffn_up_kernel(h_ref, wg_ref, wv_ref, cg_ref, cv_ref, bg_ref, bv_ref, o_ref, u_scr, *, n_ctx, tr):
    l_n, tf = o_ref.shape
    k_w = cg_ref.shape[0]
    half = (k_w - 1) // 2
    n_buf, pad = u_scr.shape[0], (u_scr.shape[1] - l_n) // 2
    rblk = tr * _pick(l_n // tr, (3, 4, 2, 1))
    row = lax.broadcasted_iota(jnp.int32, (tr, LANES), 0)
    for p in range(n_buf):
        u_scr[p, 0:pad, :] = jnp.zeros((pad, 2 * LANES), F32)
        u_scr[p, pad + l_n:, :] = jnp.zeros((pad, 2 * LANES), F32)

    def weights(s):
        cols = slice(s * LANES, (s + 1) * LANES)
        return jnp.concatenate([wg_ref[:, cols], wv_ref[:, cols]], axis=1).astype(BF16)

    def product(s, w, r0):
        u_scr[s % n_buf, pad + r0:pad + r0 + rblk, :] = jnp.dot(h_ref[r0:r0 + rblk, :], w,
                                                                preferred_element_type=F32)

    def finish(s, r0):
        p, cols = s % n_buf, slice(s * LANES, (s + 1) * LANES)

        def conv(lane0, w_ref, b_ref):
            acc = None
            for j in range(k_w):
                sh = j - half
                x = u_scr[p, pad + r0 + sh:pad + r0 + sh + tr, lane0:lane0 + LANES]
                if sh < 0 and r0 in (0, n_ctx):
                    x = jnp.where(row < -sh, 0.0, x)
                if sh > 0 and r0 + tr in (n_ctx, l_n):
                    x = jnp.where(row >= tr - sh, 0.0, x)
                t = x * w_ref[j:j + 1, cols]
                acc = t if acc is None else acc + t
            return b_ref[:, cols] + acc
        gate = conv(0, cg_ref, bg_ref)
        val = conv(LANES, cv_ref, bv_ref)
        o_ref[r0:r0 + tr, cols] = (gate * _sigmoid(gate) * val).astype(BF16)

    n_s = tf // LANES
    for s in range(n_s + 1):
        w = weights(s) if s < n_s else None
        for r0 in range(0, l_n, rblk):
            if s < n_s:
                product(s, w, r0)
            if s > 0:
                for r1 in range(r0, r0 + rblk, tr):
                    finish(s - 1, r1)


def _conv_ffn(h, n_ctx, w_up, layer, conv_w, conv_b, *, tr):
    b_n, l_n, d = h.shape
    f = w_up.shape[2] // 2
    tf = _pick(f, (512, 256, 128))
    nf = f // tf
    k_w = conv_w.shape[0]
    cb = conv_b.reshape(1, 2 * f)
    act = pl.pallas_call(
        functools.partial(_ffn_up_kernel, n_ctx=n_ctx, tr=tr),
        grid=(b_n, nf),
        in_specs=[pl.BlockSpec((None, l_n, d), lambda b, j: (b, 0, 0), pipeline_mode=pl.Buffered(1)),
                  pl.BlockSpec((None, d, tf), lambda b, j: (layer, 0, j)),
                  pl.BlockSpec((None, d, tf), lambda b, j: (layer, 0, nf + j)),
                  pl.BlockSpec((k_w, tf), lambda b, j: (0, j)),
                  pl.BlockSpec((k_w, tf), lambda b, j: (0, nf + j)),
                  pl.BlockSpec((1, tf), lambda b, j: (0, j)),
                  pl.BlockSpec((1, tf), lambda b, j: (0, nf + j))],
        out_specs=pl.BlockSpec((None, l_n, tf), lambda b, j: (b, 0, j)),
        out_shape=jax.ShapeDtypeStruct((b_n, l_n, f), BF16),
        scratch_shapes=[pltpu.VMEM((3, l_n + 16, 2 * LANES), F32)],
        compiler_params=_cparams(("parallel", "arbitrary"), 56),
        name="ffn_up",
    )(h, w_up, w_up, conv_w, conv_w, cb, cb)
    return act


def kernel(x, c, ctx, c_ctx, ada_w, ada_b, ln_g, ln_b, ffn_w_up, ffn_conv_w, ffn_conv_b, ffn_w_down, rw_mu, rw_w_rkv, rw_w0, rw_w1, rw_w2, rw_a0, rw_a1, rw_a2, rw_g1, rw_g2, rw_k_k, rw_k_a, rw_r_k, rw_gn_g, rw_gn_b, rw_w_o, da_w_qkv, da_lambda, da_sub_g, da_w_o, hg_w_in, hg_lower, hg_norm_g, hg_w_o, lr_w_in, lr_conv_w, lr_conv_b, lr_w_gate, lr_b_gate, lr_lambda, lr_w_o):
    b_n, n_lat, d = x.shape
    n_ctx = ctx.shape[1]
    depth = ada_w.shape[0]
    assert depth == 4 and rw_mu.shape[0] == 1, "one occurrence of each of the four mixers"
    assert b_n + 1 <= 8 and n_ctx % CHUNK == 0 and n_lat % CHUNK == 0
    tr = math.gcd(math.gcd(n_ctx, n_lat), 256)
    alpha = (2 * depth) ** 0.25

    c8 = jnp.concatenate([c, c_ctx[None], jnp.zeros((8 - b_n - 1, d), F32)], axis=0)
    m = _ada(c8, ada_w, ada_b)
    m_lat = m[:, :b_n].reshape(depth, b_n, 1, 6, d)
    m_ctx = jnp.broadcast_to(m[:, b_n].reshape(depth, 1, 1, 6, d), (depth, b_n, 1, 6, d))
    mod = jnp.concatenate([m_ctx, m_lat], axis=2)

    z = jnp.concatenate([ctx, x], axis=1)
    w_down_b = _to_bf16(ffn_w_down)
    h = None
    for i in range(depth):
        last = i == depth - 1
        if i == 0:
            o = _rwkv7_layer(z, mod[0], n_ctx, rw_mu[0], rw_w_rkv[0], rw_w0[0], rw_w1[0], rw_w2[0], rw_a0[0],
                             rw_a1[0], rw_a2[0], rw_g1[0], rw_g2[0], rw_k_k[0], rw_k_a[0], rw_r_k[0],
                             rw_gn_g[0], rw_gn_b[0])
            w_o = rw_w_o
        elif i == 1:
            o = _diff_attention_layer(h, n_ctx, i, da_w_qkv[0], da_lambda[0], da_sub_g[0], tr=tr)
            w_o = da_w_o
        elif i == 2:
            o = _hgrn2_layer(h, n_ctx, i, hg_w_in[0], hg_lower, hg_norm_g[0])
            w_o = hg_w_o
        else:
            o = _rglru_layer(h, n_ctx, lr_w_in[0], lr_conv_w[0], lr_conv_b[0], lr_w_gate[0], lr_b_gate[0],
                             lr_lambda[0])
            w_o = lr_w_o
        z, h = _proj_ln(o, w_o, 0, z, mod[i], ln_g[i, 0], ln_b[i, 0], mod[i], gate_j=2, mod_j=3, tr=tr,
                        n_ctx=n_ctx, alpha=alpha)
        if last:
            n_ctx = 0
        act = _conv_ffn(h, n_ctx, ffn_w_up, i, ffn_conv_w[i], ffn_conv_b[i], tr=tr)
        z, h = _proj_ln(act, w_down_b, i, z, mod[i], ln_g[i, 1], ln_b[i, 1], mod[min(i + 1, depth - 1)],
                        gate_j=5, mod_j=None if last else 0, tr=tr, n_ctx=n_ctx, alpha=alpha)
    return z
```

```python
import functools
import math

import jax
import jax.numpy as jnp
from jax import lax
from jax.experimental import pallas as pl
from jax.experimental.pallas import tpu as pltpu

F32, BF16 = jnp.float32, jnp.bfloat16

LANES = 128
CHUNK = 64
LN_EPS = 1e-5
GRID_W = 64
ROPE_BASE = 10000.0
RW_HEAD = 64
RW_DECAY_SCALE = 0.606531
RW_GN_EPS = 64e-5
DA_HEAD = 128
HG_EXPAND = 128
LR_BS = 256
LR_C = 8.0
MIB = 1024 * 1024


def _pick(n, cands):
    for c in cands:
        if n % c == 0:
            return c
    return n


V7X_VMEM_MIB = 64


def _cparams(sem, vmem_mib):
    assert vmem_mib < V7X_VMEM_MIB
    return pltpu.CompilerParams(dimension_semantics=sem, vmem_limit_bytes=vmem_mib * MIB)


def _sigmoid(x):
    return jax.nn.sigmoid(x)


NN = (((1,), (0,)), ((), ()))
NT = (((1,), (1,)), ((), ()))


def _parts(x, n):
    out = []
    for i in range(n):
        p = x.astype(BF16)
        out.append(p)
        if i + 1 < n:
            x = x - p.astype(F32)
    return out


def _mdot(ap, bp, dims=NN, order=2):
    pairs = [(a, b) for i, a in enumerate(ap) for j, b in enumerate(bp) if i + j < order]
    (ca,), (cb,) = dims[0]
    lhs = jnp.concatenate([a for a, _ in pairs], axis=ca) if len(pairs) > 1 else pairs[0][0]
    rhs = jnp.concatenate([b for _, b in pairs], axis=cb) if len(pairs) > 1 else pairs[0][1]
    return lax.dot_general(lhs, rhs, dims, preferred_element_type=F32)


def _cumsum_matrix(t_n, rev):
    r = lax.broadcasted_iota(jnp.int32, (t_n, 3 * t_n), 0)
    c = lax.broadcasted_iota(jnp.int32, (t_n, 3 * t_n), 1) % t_n
    return jnp.where((c >= r) if rev else (c <= r), 1.0, 0.0).astype(BF16)


def _cumsum(tri3_b, x):
    return jnp.dot(tri3_b, jnp.concatenate(_parts(x, 3), axis=0), preferred_element_type=F32)


def _cat_parts(xs, axis):
    return [jnp.concatenate(ps, axis=axis) for ps in zip(*xs)]


def _mm_wres_kernel(a_ref, w_ref, o_ref, wb_ref):
    i = pl.program_id(2)
    k_n = w_ref.shape[0]
    n_kc = 4 if k_n % (4 * LANES) == 0 else 1

    @pl.when(i == 0)
    def _():
        kc = k_n // n_kc
        acc = None
        for c in range(n_kc):
            ks = slice(c * kc, (c + 1) * kc)
            wb = w_ref[ks, :].astype(BF16)
            wb_ref[ks, :] = wb
            t = jnp.dot(a_ref[:, ks], wb, preferred_element_type=F32)
            acc = t if acc is None else acc + t
        o_ref[...] = acc.astype(o_ref.dtype)

    @pl.when(i > 0)
    def _():
        o_ref[...] = jnp.dot(a_ref[...], wb_ref[...], preferred_element_type=F32).astype(o_ref.dtype)


def _matmul(a, w, *, out_dtype=F32, a_off=0):
    g_n, k_n, n_n = w.shape
    m_n = a.shape[1]
    assert w.dtype == F32 and k_n <= 2048, "weight block = all of K for one column block"
    tm = _pick(m_n, (1024, 512, 256, 128, 64))
    tn = _pick(n_n, (1024, 512, 256, 128))
    return pl.pallas_call(
        _mm_wres_kernel,
        grid=(g_n, n_n // tn, m_n // tm),
        in_specs=[pl.BlockSpec((None, tm, k_n), lambda g, j, i: (g + a_off, i, 0)),
                  pl.BlockSpec((None, k_n, tn), lambda g, j, i: (g, 0, j))],
        out_specs=pl.BlockSpec((None, tm, tn), lambda g, j, i: (g, i, j)),
        out_shape=jax.ShapeDtypeStruct((g_n, m_n, n_n), out_dtype),
        scratch_shapes=[pltpu.VMEM((k_n, tn), BF16)],
        compiler_params=_cparams(("parallel", "parallel", "arbitrary"), 48),
        name="matmul_wres",
    )(a, w)


def _mm2(a, w, **kw):
    return _matmul(a[None], w[None], **kw)[0]


def _ada_kernel(c_ref, w_ref, b_ref, o_ref):
    c = c_ref[...]
    s = (c * _sigmoid(c)).astype(BF16)
    o_ref[...] = jnp.dot(s, w_ref[...].astype(BF16), preferred_element_type=F32) + b_ref[...]


def _ada(c8, ada_w, ada_b):
    depth, d, n = ada_w.shape
    tn = _pick(n, (1024, 512, 256, 128))
    return pl.pallas_call(
        _ada_kernel,
        grid=(depth, n // tn),
        in_specs=[pl.BlockSpec((8, d), lambda l, j: (0, 0)),
                  pl.BlockSpec((None, d, tn), lambda l, j: (l, 0, j)),
                  pl.BlockSpec((None, 1, tn), lambda l, j: (l, 0, j))],
        out_specs=pl.BlockSpec((None, 8, tn), lambda l, j: (l, 0, j)),
        out_shape=jax.ShapeDtypeStruct((depth, 8, n), F32),
        compiler_params=_cparams(("parallel", "parallel"), 40),
        name="ada",
    )(c8, ada_w, ada_b.reshape(depth, 1, n))


def _ln_mod_kernel(z_ref, y_ref, mod_ref, g_ref, b_ref, mod2_ref, *out_refs, gate_j, mod_j, alpha):
    m = mod_ref[...]
    zz = alpha * z_ref[...] + y_ref[...] * m[gate_j:gate_j + 1]
    mu = jnp.mean(zz, axis=-1, keepdims=True)
    zc = zz - mu
    var = jnp.mean(zc * zc, axis=-1, keepdims=True)
    zn = zc * lax.rsqrt(var + LN_EPS) * g_ref[...] + b_ref[...]
    out_refs[0][...] = zn
    if mod_j is not None:
        m2 = mod2_ref[...]
        out_refs[1][...] = (zn * (1 + m2[mod_j + 1:mod_j + 2]) + m2[mod_j:mod_j + 1]).astype(BF16)


def _ln_mod(z, y, mod, ln_g, ln_b, mod2, *, gate_j, mod_j, tr, n_ctx, alpha):
    b_n, l_z, d = z.shape
    l_y = y.shape[1]
    z_off = (l_z - l_y) // tr
    ncb = (n_ctx - (l_z - l_y)) // tr
    seg = lambda b, t: (b, jnp.where(t < ncb, 0, 1), 0, 0)
    row = pl.BlockSpec((None, tr, d), lambda b, t: (b, t, 0))
    out_shape = [jax.ShapeDtypeStruct((b_n, l_y, d), F32)]
    out_specs = [row]
    if mod_j is not None:
        out_shape.append(jax.ShapeDtypeStruct((b_n, l_y, d), BF16))
        out_specs.append(row)
    res = pl.pallas_call(
        functools.partial(_ln_mod_kernel, gate_j=gate_j, mod_j=mod_j, alpha=alpha),
        grid=(b_n, l_y // tr),
        in_specs=[pl.BlockSpec((None, tr, d), lambda b, t: (b, t + z_off, 0)),
                  row,
                  pl.BlockSpec((None, None, 6, d), seg),
                  pl.BlockSpec((1, d), lambda b, t: (0, 0)),
                  pl.BlockSpec((1, d), lambda b, t: (0, 0)),
                  pl.BlockSpec((None, None, 6, d), seg)],
        out_specs=out_specs,
        out_shape=out_shape,
        compiler_params=_cparams(("parallel", "parallel"), 40),
        name="ln_mod",
    )(z, y, mod, ln_g.reshape(1, d), ln_b.reshape(1, d), mod2)
    return res if mod_j is not None else (res[0], None)


def _cast_kernel(x_ref, o_ref):
    o_ref[...] = x_ref[...].astype(o_ref.dtype)


def _to_bf16(w):
    g_n, k_n, n_n = w.shape
    tk = _pick(k_n, (512, 256, 128))
    spec = pl.BlockSpec((None, tk, n_n), lambda g, i: (g, i, 0))
    return pl.pallas_call(
        _cast_kernel, grid=(g_n, k_n // tk), in_specs=[spec], out_specs=spec,
        out_shape=jax.ShapeDtypeStruct(w.shape, BF16),
        compiler_params=_cparams(("parallel", "parallel"), 32),
        name="to_bf16",
    )(w)


def _proj_ln_kernel(a_ref, w_ref, z_ref, mod_ref, g_ref, b_ref, mod2_ref, zo_ref, *ho_refs,
                    nk, gate_j, mod_j, alpha, tr, ncb):
    t, k = pl.program_id(1), pl.program_id(2)
    tm = zo_ref.shape[0]
    rb = tm // 2 if tm % 16 == 0 else tm

    def accumulate(first):
        for r0 in range(0, tm, rb):
            p = jnp.dot(a_ref[r0:r0 + rb, :], w_ref[...], preferred_element_type=F32)
            if first:
                zo_ref[r0:r0 + rb, :] = p
            else:
                zo_ref[r0:r0 + rb, :] += p

    rs = math.gcd(tr, 64)

    def finish():
        for sb in range(tm // rs):
            rows = slice(sb * rs, (sb + 1) * rs)
            is_ctx = t * (tm // tr) + (sb * rs) // tr < ncb
            m = jnp.where(is_ctx, mod_ref[0], mod_ref[1])
            zz = alpha * z_ref[rows, :] + zo_ref[rows, :] * m[gate_j:gate_j + 1]
            mu = jnp.mean(zz, axis=-1, keepdims=True)
            zc = zz - mu
            var = jnp.mean(zc * zc, axis=-1, keepdims=True)
            zn = zc * lax.rsqrt(var + LN_EPS) * g_ref[...] + b_ref[...]
            zo_ref[rows, :] = zn
            if mod_j is not None:
                m2 = jnp.where(is_ctx, mod2_ref[0], mod2_ref[1])
                ho_refs[0][rows, :] = (zn * (1 + m2[mod_j + 1:mod_j + 2]) + m2[mod_j:mod_j + 1]).astype(BF16)

    if nk == 1:
        accumulate(True)
        finish()
    else:
        @pl.when(k == 0)
        def _():
            accumulate(True)

        @pl.when((k > 0) & (k < nk - 1))
        def _():
            accumulate(False)

        @pl.when(k == nk - 1)
        def _():
            accumulate(False)
            finish()


def _proj_ln(a, w, g, z, mod, ln_g, ln_b, mod2, *, gate_j, mod_j, tr, n_ctx, alpha):
    b_n, l_a, k_n = a.shape
    d = w.shape[2]
    if z.shape[1] != l_a:
        y = _matmul(a.reshape(1, b_n * l_a, k_n), w[g:g + 1])[0].reshape(b_n, l_a, d)
        return _ln_mod(z, y, mod, ln_g, ln_b, mod2, gate_j=gate_j, mod_j=mod_j, tr=tr, n_ctx=n_ctx, alpha=alpha)
    if w.dtype != BF16:
        w = _to_bf16(w)
    tm = tr * _pick(l_a // tr, (3, 2, 1))
    tk = k_n if k_n <= 2048 else _pick(k_n, (1408, 1024, 512, 256, 128))
    nk = k_n // tk
    row = pl.BlockSpec((None, tm, d), lambda b, t, k: (b, t, 0))
    seg = pl.BlockSpec((None, 2, 6, d), lambda b, t, k: (b, 0, 0, 0))
    vec = pl.BlockSpec((1, d), lambda b, t, k: (0, 0))
    out_shape = [jax.ShapeDtypeStruct((b_n, l_a, d), F32)]
    out_specs = [row]
    if mod_j is not None:
        out_shape.append(jax.ShapeDtypeStruct((b_n, l_a, d), BF16))
        out_specs.append(row)
    res = pl.pallas_call(
        functools.partial(_proj_ln_kernel, nk=nk, gate_j=gate_j, mod_j=mod_j, alpha=alpha, tr=tr,
                          ncb=n_ctx // tr),
        grid=(b_n, l_a // tm, nk),
        in_specs=[pl.BlockSpec((None, tm, tk), lambda b, t, k: (b, t, k)),
                  pl.BlockSpec((None, tk, d), lambda b, t, k: (g, k, 0),
                               pipeline_mode=pl.Buffered(1) if nk == 1 else None),
                  row, seg, vec, vec, seg],
        out_specs=out_specs,
        out_shape=out_shape,
        compiler_params=_cparams(("parallel", "parallel", "arbitrary"), 56),
        name="proj_ln",
    )(a, w, z, mod, ln_g.reshape(1, d), ln_b.reshape(1, d), mod2)
    return res if mod_j is not None else (res[0], None)


def _seg_shift(x, row, shift, n_ctx):
    l_n = x.shape[0]
    rolled = pltpu.roll(x, (-shift) % l_n, 0)
    src = row + shift
    same_seg = (src >= 0) & (src < l_n) & ((src < n_ctx) == (row < n_ctx))
    return jnp.where(same_seg, rolled, 0.0)


def _rw_mix_kernel(z_ref, mod_ref, mu_ref, o_ref, *, n_ctx):
    z = z_ref[...]
    row = lax.broadcasted_iota(jnp.int32, z.shape, 0)
    is_ctx = row < n_ctx
    shift = jnp.where(is_ctx, mod_ref[0, 0:1, :], mod_ref[1, 0:1, :])
    scale = jnp.where(is_ctx, mod_ref[0, 1:2, :], mod_ref[1, 1:2, :])
    h = z * (1 + scale) + shift
    dx = 0.5 * (_seg_shift(h, row, -1, n_ctx) + _seg_shift(h, row, 1, n_ctx)) - h
    for n in range(6):
        o_ref[n] = (h + dx * mu_ref[n:n + 1, :]).astype(BF16)


def _rw_mix(z, mod, mu, *, n_ctx):
    b_n, l_n, d = z.shape
    tc = _pick(d, (256, 128))
    return pl.pallas_call(
        functools.partial(_rw_mix_kernel, n_ctx=n_ctx),
        grid=(b_n, d // tc),
        in_specs=[pl.BlockSpec((None, l_n, tc), lambda b, j: (b, 0, j)),
                  pl.BlockSpec((None, 2, 6, tc), lambda b, j: (b, 0, 0, j)),
                  pl.BlockSpec((6, tc), lambda b, j: (0, j))],
        out_specs=pl.BlockSpec((6, None, l_n, tc), lambda b, j: (0, b, 0, j)),
        out_shape=jax.ShapeDtypeStruct((6, b_n, l_n, d), BF16),
        compiler_params=_cparams(("parallel", "parallel"), 48),
        name="rw_mix",
    )(z, mod, mu)


def _lora_kernel(x_ref, a_ref, b_ref, o_ref, *, act):
    t = jnp.dot(x_ref[...], a_ref[...], preferred_element_type=F32)
    if act == "tanh":
        t = jnp.tanh(t)
    elif act == "sigmoid":
        t = _sigmoid(t)
    o_ref[...] = jnp.dot(t.astype(BF16), b_ref[...], preferred_element_type=F32)


def _lora(xs, x_idx, a, b, act):
    g_n, d, r = a.shape
    m_n = xs.shape[1]
    tm = _pick(m_n, (512, 256, 128, 64))
    return pl.pallas_call(
        functools.partial(_lora_kernel, act=act),
        grid=(g_n, m_n // tm),
        in_specs=[pl.BlockSpec((None, tm, d), lambda g, i: (x_idx, i, 0)),
                  pl.BlockSpec((None, d, r), lambda g, i: (g, 0, 0)),
                  pl.BlockSpec((None, r, d), lambda g, i: (g, 0, 0))],
        out_specs=pl.BlockSpec((None, tm, d), lambda g, i: (g, i, 0)),
        out_shape=jax.ShapeDtypeStruct((g_n, m_n, d), F32),
        compiler_params=_cparams(("parallel", "parallel"), 40),
        name="lora",
    )(xs, a, b)


def _chunk_of(q, ncc, nc, rev):
    if not rev:
        return q
    return jnp.where(q < ncc, ncc - 1 - q, nc - 1 - (q - ncc))


def _rwkv_kernel(r_ref, k_ref, v_ref, lw_ref, la_ref, g_ref, w0_ref, a0_ref, kk_ref, ka_ref, rk_ref,
                 gng_ref, gnb_ref, o_ref,
                 y_scr, mr_s, n_s, *, n_ctx):
    t_n = CHUNK
    h2 = 2 * t_n
    l_n = r_ref.shape[0]
    nc, ncc = l_n // t_n, n_ctx // t_n
    group = _pick(nc, (6, 4, 3, 2, 1))
    lane = lax.broadcasted_iota(jnp.int32, (1, LANES), 1)
    m1 = jnp.where(lane < RW_HEAD, 1.0, 0.0)
    m2 = 1.0 - m1
    ri = lax.broadcasted_iota(jnp.int32, (LANES, LANES), 0)
    ci = lax.broadcasted_iota(jnp.int32, (LANES, LANES), 1)
    same_head = (ri // RW_HEAD) == (ci // RW_HEAD)
    gsum_b = jnp.where(same_head, 1.0, 0.0).astype(BF16)
    gavg_b = jnp.where(same_head, 1.0 / RW_HEAD, 0.0).astype(BF16)
    eye = jnp.where(ri == ci, 1.0, 0.0)
    tr_i, tc_i = ri % t_n, ci % t_n
    k_k, k_a = kk_ref[...], ka_ref[...]

    def stack(x):
        return jnp.concatenate([x * m1, x * m2], axis=0)

    def rows_of(c):
        return pl.ds(pl.multiple_of(c * t_n, t_n), t_n)

    def head_sum(x, w_b):
        return _mdot(_parts(x, 2), [w_b], order=2)

    tri3_b = [_cumsum_matrix(t_n, rev) for rev in (False, True)]
    strict = [(tc_i > tr_i) if rev else (tc_i < tr_i) for rev in (False, True)]
    incl = [(tc_i >= tr_i) if rev else (tc_i <= tr_i) for rev in (False, True)]

    def stage_prep(d, c):
        rows = rows_of(c)
        k, r, v = k_ref[rows, :], r_ref[rows, :], v_ref[rows, :]
        kkr = k * k_k
        lw = -RW_DECAY_SCALE * _sigmoid(w0_ref[d:d + 1, :] + lw_ref[d, rows, :])
        a = _sigmoid(a0_ref[d:d + 1, :] + la_ref[d, rows, :])
        return dict(d=d, c=c, k=k, r=r, v=v, lw=lw, a=a, kkr=kkr, ss=head_sum(kkr * kkr, gsum_b),
                    cum=_cumsum(tri3_b[d], lw))

    def stage_amat(s):
        d, cum, lw, a = s["d"], s["cum"], s["lw"], s["a"]
        kk = s["kkr"] * lax.rsqrt(s["ss"] + 1e-12)
        kd = s["k"] * (1 + (a - 1) * k_a)
        bv = kk * a
        p_end = cum[0:1, :] if d == 1 else cum[t_n - 1:t_n, :]
        e_m = jnp.exp(-cum)
        e_h = jnp.exp(p_end - cum)
        ktp = _parts(stack(kk * jnp.exp(cum - lw)), 2)
        rt = stack(s["r"] * jnp.exp(cum))
        k2p = _cat_parts([_parts(stack(bv * e_m), 1), _parts(stack(kd * e_m), 1)], 0)
        return dict(d=d, c=s["c"], ktp=ktp, rt=rt, vp=_parts(stack(s["v"]), 1), p_end=p_end,
                    bh=stack(bv * e_h), kh=stack(kd * e_h),
                    amat=_mdot(_cat_parts([ktp, _parts(rt, 2)], 0), k2p, NT))

    def stage_square(s):
        d, amat = s["d"], s["amat"]
        lt = jnp.where(strict[d], amat[:h2, :h2], 0.0).T
        ltp = _parts(lt, 2)
        msk = jnp.concatenate([jnp.where(strict[d], amat[:h2, h2:], 0.0),
                               jnp.where(incl[d], amat[h2:, h2:], 0.0)], axis=0)
        s = dict(s, pt=eye - lt, xt=_mdot(ltp, ltp[:1]),
                 av=_mdot(_parts(msk, 2), s["vp"][:1]),
                 arbp=_parts(jnp.where(incl[d], amat[h2:, :h2], 0.0), 2))
        del s["amat"]
        return s

    def stage_double(s, final):
        xh = _parts(s["xt"], 1)
        ptp = _parts(s["pt"], 2)
        if final:
            return dict(s, pt=s["pt"] + _mdot(xh, ptp))
        rhs = [jnp.concatenate([ptp[0], xh[0]], axis=1), jnp.concatenate([ptp[1], jnp.zeros_like(xh[0])], axis=1)]
        both = _mdot(xh, rhs)
        return dict(s, pt=s["pt"] + both[:, :LANES], xt=both[:, LANES:])

    def stage_solve(s):
        rhs = jnp.concatenate([s["ktp"][0], (-s["av"][:h2]).astype(BF16)], axis=1)
        return dict(s, wub=_mdot(_parts(s["pt"].T, 2), [rhs]).astype(BF16))

    def stage_fold(s):
        d, c, wub = s["d"], s["c"], s["wub"]
        aw = _mdot(s["arbp"], [wub])
        zb = jnp.zeros((h2, LANES), BF16)
        lhs = _cat_parts([_parts(s["bh"].T, 2), _parts(s["kh"].T, 2)], 1)
        rhs = jnp.concatenate([wub, jnp.concatenate([zb, s["vp"][0]], axis=1)], axis=0)
        mn = _mdot(lhs, [rhs])
        dg = jnp.where(ri == ci, jnp.broadcast_to(jnp.exp(s["p_end"]), (LANES, LANES)), 0.0)
        rp = s["rt"] - aw[:, :LANES]
        mrp = _parts(jnp.concatenate([dg - mn[:, :LANES], rp[:t_n] + rp[t_n:]], axis=0), 2)
        for i in range(2):
            mr_s[d, c, i] = mrp[i]
        n_s[d, c] = mn[:, LANES:]
        y0 = s["av"][h2:] + aw[:, LANES:]
        return y0[:t_n] + y0[t_n:]

    def seq(q, hs):
        cs = (q, _chunk_of(q, ncc, nc, True))
        mh = [_mdot([mr_s[d, cs[d], 0], mr_s[d, cs[d], 1]], _parts(hs[d], 2)) for d in (0, 1)]
        for d in (0, 1):
            y_scr[rows_of(cs[d]), :] += mh[d][h2:]
        return tuple(mh[d][:h2] + n_s[d, cs[d]] for d in (0, 1))

    def local(i, hs, with_seq):
        steps = [i * group + g for g in range(group)]
        pending = [q - group for q in steps] if with_seq else []
        sts = [stage_prep(d, q if d == 0 else _chunk_of(q, ncc, nc, True)) for q in steps for d in (0, 1)]
        stages = ([stage_amat, stage_square] + [functools.partial(stage_double, final=f) for f in (False,) * 4 + (True,)]
                  + [stage_solve])
        for stage in stages:
            sts = [stage(s) for s in sts]
            if pending:
                hs = seq(pending.pop(0), hs)
        for s in sts:
            y0 = stage_fold(s)
            y_scr[rows_of(s["c"]), :] += y0
        while pending:
            hs = seq(pending.pop(0), hs)
        return hs

    y_scr[...] = jnp.zeros(y_scr.shape, F32)
    zero = jnp.zeros((LANES, LANES), F32)
    n_trip = nc // group
    hs = local(0, (zero, zero), False)
    hs = lax.fori_loop(1, n_trip, functools.partial(local, with_seq=True), hs)
    for q in range((n_trip - 1) * group, nc):
        hs = seq(q, hs)

    n_post = _pick(nc, (9, 6, 4, 3, 2, 1))

    def post(i, carry):
        rows = [rows_of(i * n_post + g) for g in range(n_post)]

        def bonus_sum(rw):
            k, r = k_ref[rw, :], r_ref[rw, :]
            kd_f = k * (1 + (_sigmoid(a0_ref[0:1, :] + la_ref[0, rw, :]) - 1) * k_a)
            kd_b = k * (1 + (_sigmoid(a0_ref[1:2, :] + la_ref[1, rw, :]) - 1) * k_a)
            return head_sum(r * (kd_f + kd_b) * rk_ref[...], gsum_b)
        bsum = [bonus_sum(rw) for rw in rows]
        ys = [y_scr[rw, :] for rw in rows]
        ycs = [y - m for y, m in zip(ys, [head_sum(y, gavg_b) for y in ys])]
        var = [head_sum(yc * yc, gavg_b) for yc in ycs]
        for rw, yc, vr, bs in zip(rows, ycs, var, bsum):
            yn = yc * lax.rsqrt(vr + RW_GN_EPS) * gng_ref[...] + gnb_ref[...]
            o_ref[rw, :] = ((yn + bs * v_ref[rw, :]) * g_ref[rw, :]).astype(BF16)
        return carry

    lax.fori_loop(0, nc // n_post, post, 0)


def _rwkv_scan(rkv, lw, la, g, w0, a0, k_k, k_a, r_k, gn_g, gn_b, *, n_ctx):
    _, b_n, l_n, d = rkv.shape
    nc = l_n // CHUNK
    col = lambda n: pl.BlockSpec((None, None, l_n, LANES), lambda b, p, n=n: (n, b, 0, p))
    two = pl.BlockSpec((2, None, l_n, LANES), lambda b, p: (0, b, 0, p))
    par = lambda rows: pl.BlockSpec((rows, LANES), lambda b, p: (0, p))
    return pl.pallas_call(
        functools.partial(_rwkv_kernel, n_ctx=n_ctx),
        grid=(b_n, d // LANES),
        in_specs=[col(0), col(1), col(2), two, two,
                  pl.BlockSpec((None, l_n, LANES), lambda b, p: (b, 0, p)),
                  par(2), par(2), par(1), par(1), par(1), par(1), par(1)],
        out_specs=pl.BlockSpec((None, l_n, LANES), lambda b, p: (b, 0, p)),
        out_shape=jax.ShapeDtypeStruct((b_n, l_n, d), BF16),
        scratch_shapes=[pltpu.VMEM((l_n, LANES), F32),
                        pltpu.VMEM((2, nc, 2, LANES + CHUNK, LANES), BF16),
                        pltpu.VMEM((2, nc, LANES, LANES), F32)],
        compiler_params=_cparams(("parallel", "parallel"), 56),
        name="rwkv_scan",
    )(rkv, rkv, rkv, lw, la, g, w0, a0, k_k.reshape(1, d), k_a.reshape(1, d), r_k.reshape(1, d),
      gn_g.reshape(1, d), gn_b.reshape(1, d))


def _pad_axis(w, axis, to):
    pad = [(0, 0)] * w.ndim
    pad[axis] = (0, to - w.shape[axis])
    return jnp.pad(w, pad)


def _rwkv7_layer(z, mod, n_ctx, mu, w_rkv, w0, w1, w2, a0, a1, a2, g1, g2, k_k, k_a, r_k, gn_g, gn_b):
    b_n, l_n, d = z.shape
    m_n = b_n * l_n
    xs = _rw_mix(z, mod, mu, n_ctx=n_ctx).reshape(6, m_n, d)
    rkv = _matmul(xs, w_rkv)
    r_w = -(-w1.shape[-1] // LANES) * LANES
    r_a = -(-a1.shape[-1] // LANES) * LANES
    lw = _lora(xs, 3, _pad_axis(w1, 2, r_w).astype(BF16), _pad_axis(w2, 1, r_w).astype(BF16), "tanh")
    la = _lora(xs, 4, _pad_axis(a1, 2, r_a).astype(BF16), _pad_axis(a2, 1, r_a).astype(BF16), None)
    gate = _lora(xs, 5, g1[None].astype(BF16), g2[None].astype(BF16), "sigmoid")
    o = _rwkv_scan(rkv.reshape(3, b_n, l_n, d), lw.reshape(2, b_n, l_n, d), la.reshape(2, b_n, l_n, d),
                   gate.reshape(b_n, l_n, d), w0, a0, k_k, k_a, r_k, gn_g, gn_b, n_ctx=n_ctx)
    return o


def _qkv_rope_kernel(a_ref, w_ref, cos_ref, sa_ref, sb_ref, o_ref, wb_ref, *, n_q, n_qk):
    j, i = pl.program_id(0), pl.program_id(1)

    @pl.when(i == 0)
    def _():
        wb_ref[...] = w_ref[...].astype(BF16)

    tn = o_ref.shape[1]
    sw = math.gcd(tn, 2 * DA_HEAD)

    def product(s):
        return jnp.dot(a_ref[...], wb_ref[:, s * sw:(s + 1) * sw], preferred_element_type=F32)

    @pl.when(j < n_qk)
    def _():
        q = DA_HEAD // 4
        q_scale = jnp.where(j < n_q, DA_HEAD ** -0.5 * math.log2(math.e), 1.0)
        cos, s_a, s_b = cos_ref[...] * q_scale, sa_ref[...] * q_scale, sb_ref[...] * q_scale

        def rotary(s, x):
            for c in range(0, sw, DA_HEAD):
                xs = x[:, c:c + DA_HEAD]
                rot = xs * cos + pltpu.roll(xs, DA_HEAD - q, 1) * s_a + pltpu.roll(xs, q, 1) * s_b
                o_ref[:, s * sw + c:s * sw + c + DA_HEAD] = rot.astype(BF16)
        x_prev = product(0)
        for s in range(1, tn // sw):
            x_next = product(s)
            rotary(s - 1, x_prev)
            x_prev = x_next
        rotary(tn // sw - 1, x_prev)

    @pl.when(j >= n_qk)
    def _():
        for s in range(tn // sw):
            o_ref[:, s * sw:(s + 1) * sw] = product(s).astype(BF16)


def _qkv_rope(h, w_qkv, cos, s_a, s_b):
    b_n, l_n, d = h.shape
    d3 = w_qkv.shape[1]
    tm = _pick(l_n, (768, 1024, 512, 256, 128, 64))
    tn = _pick(d, (1024, 512, 256, 128))
    n_t = l_n // tm
    tab = pl.BlockSpec((tm, DA_HEAD), lambda j, i: (i % n_t, 0))
    out = pl.pallas_call(
        functools.partial(_qkv_rope_kernel, n_q=d // tn, n_qk=2 * d // tn),
        grid=(d3 // tn, b_n * n_t),
        in_specs=[pl.BlockSpec((tm, d), lambda j, i: (i, 0)),
                  pl.BlockSpec((d, tn), lambda j, i: (0, j)), tab, tab, tab],
        out_specs=pl.BlockSpec((tm, tn), lambda j, i: (i, j)),
        out_shape=jax.ShapeDtypeStruct((b_n * l_n, d3), BF16),
        scratch_shapes=[pltpu.VMEM((d, tn), BF16)],
        compiler_params=_cparams(("parallel", "arbitrary"), 48),
        name="qkv_rope",
    )(h.reshape(b_n * l_n, d), w_qkv, cos, s_a, s_b)
    return out.reshape(b_n, l_n, d3)


def _attn_kernel(q_ref, k_ref, v_ref, lam_ref, sg_ref, o_ref, *, ncb, n_ctx, lam_init):
    qi = pl.program_id(2)
    lv = lam_ref[...]
    lam = (jnp.exp(jnp.sum(lv[0:1] * lv[1:2], axis=-1, keepdims=True))
           - jnp.exp(jnp.sum(lv[2:3] * lv[3:4], axis=-1, keepdims=True)) + lam_init)

    hw = 2 * DA_HEAD
    n_hh = o_ref.shape[1] // hw

    def attend(nk):
        def scores(hh, m):
            cols = slice(hh * hw + m * DA_HEAD, hh * hw + (m + 1) * DA_HEAD)
            return lax.dot_general(q_ref[:, cols], k_ref[0:nk, cols], NT, preferred_element_type=F32)
        s_all = [[scores(hh, m) for m in (0, 1)] for hh in range(n_hh)]
        for hh in range(n_hh):
            def probs(s):
                e = jnp.exp2(s - jnp.max(s, axis=-1, keepdims=True))
                return e, 1.0 / jnp.sum(e, axis=-1, keepdims=True)
            e0, i0 = probs(s_all[hh][0])
            e1, i1 = probs(s_all[hh][1])
            v = v_ref[0:nk, hh * hw:(hh + 1) * hw]
            o = (jnp.dot(e0.astype(BF16), v, preferred_element_type=F32) * i0
                 - jnp.dot(e1.astype(BF16), v, preferred_element_type=F32) * (lam * i1))
            o = o * lax.rsqrt(jnp.mean(o * o, axis=-1, keepdims=True) + 1e-5) * sg_ref[...] * (1 - lam_init)
            o_ref[:, hh * hw:(hh + 1) * hw] = o.astype(BF16)

    if ncb > 0:
        @pl.when(qi < ncb)
        def _():
            attend(n_ctx)

    @pl.when(qi >= ncb)
    def _():
        attend(k_ref.shape[0])


def _attention(qkv, lam_vec, sub_g, *, tq, n_ctx, lam_init):
    b_n, l_n, d3 = qkv.shape
    d = d3 // 3
    hw = 2 * DA_HEAD
    n_hh = 2 if (d // hw) % 2 == 0 else 1
    bw = n_hh * hw
    nh = d // bw
    return pl.pallas_call(
        functools.partial(_attn_kernel, ncb=n_ctx // tq, n_ctx=n_ctx, lam_init=lam_init),
        grid=(b_n, nh, l_n // tq),
        in_specs=[pl.BlockSpec((None, tq, bw), lambda b, h, t: (b, t, h)),
                  pl.BlockSpec((None, l_n, bw), lambda b, h, t: (b, 0, nh + h)),
                  pl.BlockSpec((None, l_n, bw), lambda b, h, t: (b, 0, 2 * nh + h)),
                  pl.BlockSpec((4, DA_HEAD), lambda b, h, t: (0, 0)),
                  pl.BlockSpec((1, hw), lambda b, h, t: (0, 0))],
        out_specs=pl.BlockSpec((None, tq, bw), lambda b, h, t: (b, t, h)),
        out_shape=jax.ShapeDtypeStruct((b_n, l_n, d), BF16),
        compiler_params=_cparams(("parallel", "parallel", "arbitrary"), 48),
        name="diff_attn",
    )(qkv, qkv, qkv, lam_vec, sub_g.reshape(1, hw))


def _rope_tables(n_ctx, n_lat):
    n_rows = n_lat // GRID_W
    row = jnp.repeat(jnp.arange(n_rows, dtype=F32), GRID_W)
    col = jnp.tile(jnp.arange(GRID_W, dtype=F32), n_rows)
    nf = DA_HEAD // 4
    inv_freq = ROPE_BASE ** (-jnp.arange(nf, dtype=F32) / nf)
    ang_r, ang_c = row[:, None] * inv_freq, col[:, None] * inv_freq
    ang = jnp.concatenate([ang_r, ang_r, ang_c, ang_c], axis=-1)
    ang = jnp.concatenate([jnp.zeros((n_ctx, DA_HEAD), F32), ang], axis=0)
    cos, sin = jnp.cos(ang), jnp.sin(ang)
    even_q = (jnp.arange(DA_HEAD) // nf) % 2 == 0
    return cos, jnp.where(even_q, -sin, 0.0), jnp.where(even_q, 0.0, sin)


def _diff_attention_layer(h, n_ctx, layer_idx, w_qkv, lam_vec, sub_g, *, tr):
    b_n, l_n, d = h.shape
    cos, s_a, s_b = _rope_tables(n_ctx, l_n - n_ctx)
    qkv = _qkv_rope(h, w_qkv, cos, s_a, s_b)
    lam_init = 0.8 - 0.6 * math.exp(-0.3 * layer_idx)
    o = _attention(qkv, lam_vec, sub_g, tq=tr, n_ctx=n_ctx, lam_init=lam_init)
    return o


def _hgrn_kernel(q_ref, i_ref, g_ref, ff_ref, fb_ref, low_ref, ng_ref, o_ref, o_scr, *, n_ctx, layer_idx):
    t_n = CHUNK
    l_n = q_ref.shape[0]
    nc, ncc = l_n // t_n, n_ctx // t_n
    r64 = lax.broadcasted_iota(jnp.int32, (t_n, t_n), 0)
    c64 = lax.broadcasted_iota(jnp.int32, (t_n, t_n), 1)

    def rows_of(c):
        return pl.ds(pl.multiple_of(c * t_n, t_n), t_n)

    f_refs = (ff_ref, fb_ref)
    lbs, incl = [], []
    tri3_b = [_cumsum_matrix(t_n, rev) for rev in (False, True)]
    for d in (0, 1):
        low = low_ref[d]
        e = jnp.exp(low - jnp.max(low, axis=0, keepdims=True))
        sm = e / jnp.sum(e, axis=0, keepdims=True)
        cs = sm[0:1]
        for rr in range(1, layer_idx + 1):
            cs = cs + sm[rr:rr + 1]
        lbs.append(cs - sm[0:1])
        incl.append((c64 >= r64) if d == 1 else (c64 <= r64))
    group = _pick(nc, (6, 4, 2, 1))

    def stage_cum(d, c):
        rows = rows_of(c)
        f = lbs[d] + (1.0 - lbs[d]) * _sigmoid(f_refs[d][rows, :])
        return dict(d=d, rows=rows, f=f, cum=_cumsum(tri3_b[d], jnp.log(f)))

    def stage_att(s):
        d, cum, rows = s["d"], s["cum"], s["rows"]
        b_end = cum[0:1, :] if d == 1 else cum[t_n - 1:t_n, :]
        qv = q_ref[rows, :]
        qd = (qv * _sigmoid(qv) * jnp.exp(cum)).astype(BF16)
        kk = 1.0 - s["f"]
        v = i_ref[rows, :]
        kd = (kk * jnp.exp(-cum)).astype(BF16)
        ke = (kk * jnp.exp(b_end - cum)).astype(BF16)
        return dict(d=d, rows=rows, qd=qd, vb=v.astype(BF16), dec=jnp.exp(b_end),
                    att=lax.dot_general(qd, kd, NT, preferred_element_type=F32),
                    upd=jnp.dot(v.T.astype(BF16), ke, preferred_element_type=F32))

    def stage_intra(s):
        att = jnp.where(incl[s["d"]], s["att"], 0.0).astype(BF16)
        return dict(s, o=jnp.dot(att, s["vb"], preferred_element_type=F32))

    def body(i, states):
        items = [(d, _chunk_of(i * group + g, ncc, nc, d == 1)) for g in range(group) for d in (0, 1)]
        sts = [stage_cum(d, c) for d, c in items]
        sts = [stage_att(s) for s in sts]
        sts = [stage_intra(s) for s in sts]
        states = list(states)
        for s in sts:
            d = s["d"]
            o = s["o"] + lax.dot_general(s["qd"], states[d].astype(BF16), NT, preferred_element_type=F32)
            o_scr[d, s["rows"], :] = o
            states[d] = states[d] * s["dec"] + s["upd"]
        return tuple(states)

    zero = jnp.zeros((LANES, LANES), F32)
    lax.fori_loop(0, nc // group, body, (zero, zero))

    p_n = t_n * _pick(nc, (4, 3, 2, 1))

    def post(c, carry):
        rows = pl.ds(pl.multiple_of(c * p_n, p_n), p_n)
        o = o_scr[0, rows, :] + o_scr[1, rows, :]
        o = o * lax.rsqrt(jnp.mean(o * o, axis=-1, keepdims=True) + 1e-5) * ng_ref[...]
        gv = g_ref[rows, :]
        o_ref[rows, :] = (o * (gv * _sigmoid(gv))).astype(BF16)
        return carry

    lax.fori_loop(0, l_n // p_n, post, 0)


def _hgrn2_layer(h, n_ctx, layer_idx, w_in, lower, norm_g):
    b_n, l_n, d = h.shape
    m_n = b_n * l_n
    nh = d // HG_EXPAND
    proj = _mm2(h.reshape(m_n, d), w_in).reshape(b_n, l_n, 5 * d)
    col = lambda n: pl.BlockSpec((None, l_n, LANES), lambda b, p, n=n: (b, 0, n * nh + p))
    o = pl.pallas_call(
        functools.partial(_hgrn_kernel, n_ctx=n_ctx, layer_idx=layer_idx),
        grid=(b_n, nh),
        in_specs=[col(0), col(1), col(2), col(3), col(4),
                  pl.BlockSpec((2, lower.shape[1], LANES), lambda b, p: (0, 0, p)),
                  pl.BlockSpec((1, LANES), lambda b, p: (0, 0))],
        out_specs=pl.BlockSpec((None, l_n, LANES), lambda b, p: (b, 0, p)),
        out_shape=jax.ShapeDtypeStruct((b_n, l_n, d), BF16),
        scratch_shapes=[pltpu.VMEM((2, l_n, LANES), F32)],
        compiler_params=_cparams(("parallel", "parallel"), 40),
        name="hgrn_scan",
    )(proj, proj, proj, proj, proj, lower, norm_g.reshape(1, LANES))
    return o


def _gelu_tanh(x):
    return 0.5 * x * (1.0 + jnp.tanh(math.sqrt(2.0 / math.pi) * (x + 0.044715 * (x * x * x))))


def _softplus(x):
    return jnp.maximum(x, 0.0) + jnp.log1p(jnp.exp(-jnp.abs(x)))


SEG_PAD = 8


def _lin_scan(a_ref, u_ref, hl_s, cp_s, h_s, base, row0, n, h_in):
    seg = n // 8
    stride = seg + SEG_PAD
    n_p = a_ref.shape[1]
    chains = [(d, j) for d in (0, 1) for j in range(n_p)]

    def step(i, carry):
        out = []
        for (d, j), (hl, cp) in zip(chains, carry):
            idx = pl.ds(base + (i if d == 0 else seg - 1 - i), 8, stride=stride)
            a = a_ref[d, j, idx, :]
            hl = a * hl + u_ref[d, j, idx, :]
            cp = a * cp
            hl_s[d, j, idx, :] = hl
            cp_s[d, j, idx, :] = cp
            out.append((hl, cp))
        return tuple(out)

    init = tuple((jnp.zeros((8, LANES), F32), jnp.ones((8, LANES), F32)) for _ in chains)
    ends = lax.fori_loop(0, seg, step, init)
    h_out = [[None] * n_p, [None] * n_p]
    for (d, j), (hl_e, cp_e) in zip(chains, ends):
        carry = h_in[d][j]
        for s in (range(8) if d == 0 else range(7, -1, -1)):
            r0, p0 = row0 + s * seg, base + s * stride
            blk = hl_s[d, j, p0:p0 + seg, :] + cp_s[d, j, p0:p0 + seg, :] * carry
            if d == 0:
                h_s[j, r0:r0 + seg, :] = blk
            else:
                h_s[j, r0:r0 + seg, :] += blk
            carry = hl_e[s:s + 1, :] + cp_e[s:s + 1, :] * carry
        h_out[d][j] = carry
    return h_out


def _rglru_kernel(gb_ref, xb_ref, cw_ref, cb_ref, wg_ref, bg_ref, lam_ref, o_ref, a_s, u_s, h_s, hl_s, cp_s, *,
                  n_ctx):
    l_n = xb_ref.shape[0]
    n_lat = l_n - n_ctx
    x = xb_ref[...]
    row = lax.broadcasted_iota(jnp.int32, x.shape, 0)
    k_w = cw_ref.shape[0]
    xc = cb_ref[...] + sum(_seg_shift(x, row, j - (k_w - 1) // 2, n_ctx) * cw_ref[j:j + 1, :]
                           for j in range(k_w))
    xcb = xc.astype(BF16)
    n_p = x.shape[1] // LANES
    for d in (0, 1):
        gate = lambda g: _sigmoid(jnp.dot(xcb, wg_ref[d, g].astype(BF16), preferred_element_type=F32)
                                  + bg_ref[d, g:g + 1, :])
        log_a = -LR_C * gate(0) * _softplus(-lam_ref[d:d + 1, :])
        a = jnp.exp(log_a)
        u = jnp.sqrt(jnp.tanh(-log_a) * (jnp.exp(2.0 * log_a) + 1.0)) * gate(1) * xc
        base = 0
        for row0, n in ((0, n_ctx), (n_ctx, n_lat)):
            seg = n // 8
            for j in range(n_p if n else 0):
                for s in range(8):
                    src = slice(row0 + s * seg, row0 + (s + 1) * seg)
                    dst = slice(base + s * (seg + SEG_PAD), base + s * (seg + SEG_PAD) + seg)
                    a_s[d, j, dst, :] = a[src, j * LANES:(j + 1) * LANES]
                    u_s[d, j, dst, :] = u[src, j * LANES:(j + 1) * LANES]
            base += 8 * (seg + SEG_PAD) if n else 0
    h = [[jnp.zeros((1, LANES), F32)] * n_p] * 2
    base = 0
    for row0, n in ((0, n_ctx), (n_ctx, n_lat)):
        if n:
            h = _lin_scan(a_s, u_s, hl_s, cp_s, h_s, base, row0, n, h)
            base += 8 * (n // 8 + SEG_PAD)
    for j in range(n_p):
        cols = slice(j * LANES, (j + 1) * LANES)
        o_ref[:, cols] = (h_s[j, n_ctx:l_n, :] * _gelu_tanh(gb_ref[n_ctx:, cols])).astype(BF16)


def _rglru_layer(h, n_ctx, w_in, conv_w, conv_b, w_gate, b_gate, lam):
    b_n, l_n, d = h.shape
    n_lat = l_n - n_ctx
    nb = d // LR_BS
    proj = _mm2(h.reshape(b_n * l_n, d), w_in).reshape(b_n, l_n, 2 * d)
    k_w = conv_w.shape[0]
    plane = (LR_BS // LANES, l_n + 16 * SEG_PAD, LANES)
    o = pl.pallas_call(
        functools.partial(_rglru_kernel, n_ctx=n_ctx),
        grid=(b_n, nb),
        in_specs=[pl.BlockSpec((None, l_n, LR_BS), lambda b, j: (b, 0, j)),
                  pl.BlockSpec((None, l_n, LR_BS), lambda b, j: (b, 0, nb + j)),
                  pl.BlockSpec((k_w, LR_BS), lambda b, j: (0, j)),
                  pl.BlockSpec((1, LR_BS), lambda b, j: (0, j)),
                  pl.BlockSpec((2, 2, None, LR_BS, LR_BS), lambda b, j: (0, 0, j, 0, 0)),
                  pl.BlockSpec((2, 2, LR_BS), lambda b, j: (0, 0, j)),
                  pl.BlockSpec((2, LR_BS), lambda b, j: (0, j))],
        out_specs=pl.BlockSpec((None, n_lat, LR_BS), lambda b, j: (b, 0, j)),
        out_shape=jax.ShapeDtypeStruct((b_n, n_lat, d), BF16),
        scratch_shapes=[pltpu.VMEM((2,) + plane, F32), pltpu.VMEM((2,) + plane, F32), pltpu.VMEM(plane, F32),
                        pltpu.VMEM((2,) + plane, F32), pltpu.VMEM((2,) + plane, F32)],
        compiler_params=_cparams(("parallel", "parallel"), 56),
        name="rglru",
    )(proj, proj, conv_w, conv_b.reshape(1, d), w_gate, b_gate, lam)
    return o


def _ffn_up_kernel(h_ref, wg_ref, wv_ref, cg_ref, cv_ref, bg_ref, bv_ref, o_ref, u_scr, *, n_ctx, tr):
    l_n, tf = o_ref.shape
    k_w = cg_ref.shape[0]
    half = (k_w - 1) // 2
    n_buf, pad = u_scr.shape[0], (u_scr.shape[1] - l_n) // 2
    rblk = tr * _pick(l_n // tr, (3, 4, 2, 1))
    row = lax.broadcasted_iota(jnp.int32, (tr, LANES), 0)
    for p in range(n_buf):
        u_scr[p, 0:pad, :] = jnp.zeros((pad, 2 * LANES), F32)
        u_scr[p, pad + l_n:, :] = jnp.zeros((pad, 2 * LANES), F32)

    def weights(s):
        cols = slice(s * LANES, (s + 1) * LANES)
        return jnp.concatenate([wg_ref[:, cols], wv_ref[:, cols]], axis=1).astype(BF16)

    def product(s, w, r0):
        u_scr[s % n_buf, pad + r0:pad + r0 + rblk, :] = jnp.dot(h_ref[r0:r0 + rblk, :], w,
                                                                preferred_element_type=F32)

    def finish(s, r0):
        p, cols = s % n_buf, slice(s * LANES, (s + 1) * LANES)

        def conv(lane0, w_ref, b_ref):
            acc = None
            for j in range(k_w):
                sh = j - half
                x = u_scr[p, pad + r0 + sh:pad + r0 + sh + tr, lane0:lane0 + LANES]
                if sh < 0 and r0 in (0, n_ctx):
                    x = jnp.where(row < -sh, 0.0, x)
                if sh > 0 and r0 + tr in (n_ctx, l_n):
                    x = jnp.where(row >= tr - sh, 0.0, x)
                t = x * w_ref[j:j + 1, cols]
                acc = t if acc is None else acc + t
            return b_ref[:, cols] + acc
        gate = conv(0, cg_ref, bg_ref)
        val = conv(LANES, cv_ref, bv_ref)
        o_ref[r0:r0 + tr, cols] = (gate * _sigmoid(gate) * val).astype(BF16)

    n_s = tf // LANES
    for s in range(n_s + 1):
        w = weights(s) if s < n_s else None
        for r0 in range(0, l_n, rblk):
            if s < n_s:
                product(s, w, r0)
            if s > 0:
                for r1 in range(r0, r0 + rblk, tr):
                    finish(s - 1, r1)


def _conv_ffn(h, n_ctx, w_up, layer, conv_w, conv_b, *, tr):
    b_n, l_n, d = h.shape
    f = w_up.shape[2] // 2
    tf = _pick(f, (512, 256, 128))
    nf = f // tf
    k_w = conv_w.shape[0]
    cb = conv_b.reshape(1, 2 * f)
    act = pl.pallas_call(
        functools.partial(_ffn_up_kernel, n_ctx=n_ctx, tr=tr),
        grid=(b_n, nf),
        in_specs=[pl.BlockSpec((None, l_n, d), lambda b, j: (b, 0, 0), pipeline_mode=pl.Buffered(1)),
                  pl.BlockSpec((None, d, tf), lambda b, j: (layer, 0, j)),
                  pl.BlockSpec((None, d, tf), lambda b, j: (layer, 0, nf + j)),
                  pl.BlockSpec((k_w, tf), lambda b, j: (0, j)),
                  pl.BlockSpec((k_w, tf), lambda b, j: (0, nf + j)),
                  pl.BlockSpec((1, tf), lambda b, j: (0, j)),
                  pl.BlockSpec((1, tf), lambda b, j: (0, nf + j))],
        out_specs=pl.BlockSpec((None, l_n, tf), lambda b, j: (b, 0, j)),
        out_shape=jax.ShapeDtypeStruct((b_n, l_n, f), BF16),
        scratch_shapes=[pltpu.VMEM((3, l_n + 16, 2 * LANES), F32)],
        compiler_params=_cparams(("parallel", "arbitrary"), 56),
        name="ffn_up",
    )(h, w_up, w_up, conv_w, conv_w, cb, cb)
    return act


def kernel(x, c, ctx, c_ctx, ada_w, ada_b, ln_g, ln_b, ffn_w_up, ffn_conv_w, ffn_conv_b, ffn_w_down, rw_mu, rw_w_rkv, rw_w0, rw_w1, rw_w2, rw_a0, rw_a1, rw_a2, rw_g1, rw_g2, rw_k_k, rw_k_a, rw_r_k, rw_gn_g, rw_gn_b, rw_w_o, da_w_qkv, da_lambda, da_sub_g, da_w_o, hg_w_in, hg_lower, hg_norm_g, hg_w_o, lr_w_in, lr_conv_w, lr_conv_b, lr_w_gate, lr_b_gate, lr_lambda, lr_w_o):
    b_n, n_lat, d = x.shape
    n_ctx = ctx.shape[1]
    depth = ada_w.shape[0]
    assert depth == 4 and rw_mu.shape[0] == 1, "one occurrence of each of the four mixers"
    assert b_n + 1 <= 8 and n_ctx % CHUNK == 0 and n_lat % CHUNK == 0
    tr = math.gcd(math.gcd(n_ctx, n_lat), 256)
    alpha = (2 * depth) ** 0.25

    c8 = jnp.concatenate([c, c_ctx[None], jnp.zeros((8 - b_n - 1, d), F32)], axis=0)
    m = _ada(c8, ada_w, ada_b)
    m_lat = m[:, :b_n].reshape(depth, b_n, 1, 6, d)
    m_ctx = jnp.broadcast_to(m[:, b_n].reshape(depth, 1, 1, 6, d), (depth, b_n, 1, 6, d))
    mod = jnp.concatenate([m_ctx, m_lat], axis=2)

    z = jnp.concatenate([ctx, x], axis=1)
    w_down_b = _to_bf16(ffn_w_down)
    h = None
    for i in range(depth):
        last = i == depth - 1
        if i == 0:
            o = _rwkv7_layer(z, mod[0], n_ctx, rw_mu[0], rw_w_rkv[0], rw_w0[0], rw_w1[0], rw_w2[0], rw_a0[0],
                             rw_a1[0], rw_a2[0], rw_g1[0], rw_g2[0], rw_k_k[0], rw_k_a[0], rw_r_k[0],
                             rw_gn_g[0], rw_gn_b[0])
            w_o = rw_w_o
        elif i == 1:
            o = _diff_attention_layer(h, n_ctx, i, da_w_qkv[0], da_lambda[0], da_sub_g[0], tr=tr)
            w_o = da_w_o
        elif i == 2:
            o = _hgrn2_layer(h, n_ctx, i, hg_w_in[0], hg_lower, hg_norm_g[0])
            w_o = hg_w_o
        else:
            o = _rglru_layer(h, n_ctx, lr_w_in[0], lr_conv_w[0], lr_conv_b[0], lr_w_gate[0], lr_b_gate[0],
                             lr_lambda[0])
            w_o = lr_w_o
        z, h = _proj_ln(o, w_o, 0, z, mod[i], ln_g[i, 0], ln_b[i, 0], mod[i], gate_j=2, mod_j=3, tr=tr,
                        n_ctx=n_ctx, alpha=alpha)
        if last:
            n_ctx = 0
        act = _conv_ffn(h, n_ctx, ffn_w_up, i, ffn_conv_w[i], ffn_conv_b[i], tr=tr)
        z, h = _proj_ln(act, w_down_b, i, z, mod[i], ln_g[i, 1], ln_b[i, 1], mod[min(i + 1, depth - 1)],
                        gate_j=5, mod_j=None if last else 0, tr=tr, n_ctx=n_ctx, alpha=alpha)
    return z
```

```python
import functools
import math

import jax
import jax.numpy as jnp
from jax import lax
from jax.experimental import pallas as pl
from jax.experimental.pallas import tpu as pltpu

F32, BF16 = jnp.float32, jnp.bfloat16

LANES = 128
CHUNK = 64
INV_BASE = 16
LN_EPS = 1e-5
GRID_W = 64
ROPE_BASE = 10000.0
RW_HEAD = 64
RW_DECAY_SCALE = 0.606531
RW_GN_EPS = 64e-5
DA_HEAD = 128
HG_EXPAND = 128
LR_BS = 256
LR_C = 8.0
MIB = 1024 * 1024


def _pick(n, cands):
    for c in cands:
        if n % c == 0:
            return c
    return n


V7X_VMEM_MIB = 64


def _cparams(sem, vmem_mib):
    assert vmem_mib < V7X_VMEM_MIB
    return pltpu.CompilerParams(dimension_semantics=sem, vmem_limit_bytes=vmem_mib * MIB)


def _sigmoid(x):
    return jax.nn.sigmoid(x)


NN = (((1,), (0,)), ((), ()))
NT = (((1,), (1,)), ((), ()))


def _parts(x, n):
    out = []
    for i in range(n):
        p = x.astype(BF16)
        out.append(p)
        if i + 1 < n:
            x = x - p.astype(F32)
    return out


def _mdot(ap, bp, dims=NN, order=2):
    pairs = [(a, b) for i, a in enumerate(ap) for j, b in enumerate(bp) if i + j < order]
    (ca,), (cb,) = dims[0]
    lhs = jnp.concatenate([a for a, _ in pairs], axis=ca) if len(pairs) > 1 else pairs[0][0]
    rhs = jnp.concatenate([b for _, b in pairs], axis=cb) if len(pairs) > 1 else pairs[0][1]
    return lax.dot_general(lhs, rhs, dims, preferred_element_type=F32)


def _cumsum_matrix(t_n, rev):
    r = lax.broadcasted_iota(jnp.int32, (t_n, 3 * t_n), 0)
    c = lax.broadcasted_iota(jnp.int32, (t_n, 3 * t_n), 1) % t_n
    return jnp.where((c >= r) if rev else (c <= r), 1.0, 0.0).astype(BF16)


def _cumsum(tri3_b, x):
    return jnp.dot(tri3_b, jnp.concatenate(_parts(x, 3), axis=0), preferred_element_type=F32)


def _cat_parts(xs, axis):
    return [jnp.concatenate(ps, axis=axis) for ps in zip(*xs)]


def _mm_wres_kernel(a_ref, w_ref, o_ref, wb_ref):
    i = pl.program_id(2)
    k_n = w_ref.shape[0]
    n_kc = 4 if k_n % (4 * LANES) == 0 else 1

    @pl.when(i == 0)
    def _():
        kc = k_n // n_kc
        acc = None
        for c in range(n_kc):
            ks = slice(c * kc, (c + 1) * kc)
            wb = w_ref[ks, :].astype(BF16)
            wb_ref[ks, :] = wb
            t = jnp.dot(a_ref[:, ks], wb, preferred_element_type=F32)
            acc = t if acc is None else acc + t
        o_ref[...] = acc.astype(o_ref.dtype)

    @pl.when(i > 0)
    def _():
        o_ref[...] = jnp.dot(a_ref[...], wb_ref[...], preferred_element_type=F32).astype(o_ref.dtype)


def _matmul(a, w, *, out_dtype=F32, a_off=0):
    g_n, k_n, n_n = w.shape
    m_n = a.shape[1]
    assert w.dtype == F32 and k_n <= 2048, "weight block = all of K for one column block"
    tm = _pick(m_n, (1024, 512, 256, 128, 64))
    tn = _pick(n_n, (1024, 512, 256, 128))
    return pl.pallas_call(
        _mm_wres_kernel,
        grid=(g_n, n_n // tn, m_n // tm),
        in_specs=[pl.BlockSpec((None, tm, k_n), lambda g, j, i: (g + a_off, i, 0)),
                  pl.BlockSpec((None, k_n, tn), lambda g, j, i: (g, 0, j))],
        out_specs=pl.BlockSpec((None, tm, tn), lambda g, j, i: (g, i, j)),
        out_shape=jax.ShapeDtypeStruct((g_n, m_n, n_n), out_dtype),
        scratch_shapes=[pltpu.VMEM((k_n, tn), BF16)],
        compiler_params=_cparams(("parallel", "parallel", "arbitrary"), 48),
        name="matmul_wres",
    )(a, w)


def _mm2(a, w, **kw):
    return _matmul(a[None], w[None], **kw)[0]


def _ada_kernel(c_ref, w_ref, b_ref, o_ref):
    c = c_ref[...]
    s = (c * _sigmoid(c)).astype(BF16)
    o_ref[...] = jnp.dot(s, w_ref[...].astype(BF16), preferred_element_type=F32) + b_ref[...]


def _ada(c8, ada_w, ada_b):
    depth, d, n = ada_w.shape
    tn = _pick(n, (1024, 512, 256, 128))
    return pl.pallas_call(
        _ada_kernel,
        grid=(depth, n // tn),
        in_specs=[pl.BlockSpec((8, d), lambda l, j: (0, 0)),
                  pl.BlockSpec((None, d, tn), lambda l, j: (l, 0, j)),
                  pl.BlockSpec((None, 1, tn), lambda l, j: (l, 0, j))],
        out_specs=pl.BlockSpec((None, 8, tn), lambda l, j: (l, 0, j)),
        out_shape=jax.ShapeDtypeStruct((depth, 8, n), F32),
        compiler_params=_cparams(("parallel", "parallel"), 40),
        name="ada",
    )(c8, ada_w, ada_b.reshape(depth, 1, n))


def _ln_mod_kernel(z_ref, y_ref, mod_ref, g_ref, b_ref, mod2_ref, *out_refs, gate_j, mod_j, alpha):
    m = mod_ref[...]
    zz = alpha * z_ref[...] + y_ref[...] * m[gate_j:gate_j + 1]
    mu = jnp.mean(zz, axis=-1, keepdims=True)
    zc = zz - mu
    var = jnp.mean(zc * zc, axis=-1, keepdims=True)
    zn = zc * lax.rsqrt(var + LN_EPS) * g_ref[...] + b_ref[...]
    out_refs[0][...] = zn
    if mod_j is not None:
        m2 = mod2_ref[...]
        out_refs[1][...] = (zn * (1 + m2[mod_j + 1:mod_j + 2]) + m2[mod_j:mod_j + 1]).astype(BF16)


def _ln_mod(z, y, mod, ln_g, ln_b, mod2, *, gate_j, mod_j, tr, n_ctx, alpha):
    b_n, l_z, d = z.shape
    l_y = y.shape[1]
    z_off = (l_z - l_y) // tr
    ncb = (n_ctx - (l_z - l_y)) // tr
    seg = lambda b, t: (b, jnp.where(t < ncb, 0, 1), 0, 0)
    row = pl.BlockSpec((None, tr, d), lambda b, t: (b, t, 0))
    out_shape = [jax.ShapeDtypeStruct((b_n, l_y, d), F32)]
    out_specs = [row]
    if mod_j is not None:
        out_shape.append(jax.ShapeDtypeStruct((b_n, l_y, d), BF16))
        out_specs.append(row)
    res = pl.pallas_call(
        functools.partial(_ln_mod_kernel, gate_j=gate_j, mod_j=mod_j, alpha=alpha),
        grid=(b_n, l_y // tr),
        in_specs=[pl.BlockSpec((None, tr, d), lambda b, t: (b, t + z_off, 0)),
                  row,
                  pl.BlockSpec((None, None, 6, d), seg),
                  pl.BlockSpec((1, d), lambda b, t: (0, 0)),
                  pl.BlockSpec((1, d), lambda b, t: (0, 0)),
                  pl.BlockSpec((None, None, 6, d), seg)],
        out_specs=out_specs,
        out_shape=out_shape,
        compiler_params=_cparams(("parallel", "parallel"), 40),
        name="ln_mod",
    )(z, y, mod, ln_g.reshape(1, d), ln_b.reshape(1, d), mod2)
    return res if mod_j is not None else (res[0], None)


def _cast_kernel(x_ref, o_ref):
    o_ref[...] = x_ref[...].astype(o_ref.dtype)


def _to_bf16(w):
    g_n, k_n, n_n = w.shape
    tk = _pick(k_n, (512, 256, 128))
    spec = pl.BlockSpec((None, tk, n_n), lambda g, i: (g, i, 0))
    return pl.pallas_call(
        _cast_kernel, grid=(g_n, k_n // tk), in_specs=[spec], out_specs=spec,
        out_shape=jax.ShapeDtypeStruct(w.shape, BF16),
        compiler_params=_cparams(("parallel", "parallel"), 32),
        name="to_bf16",
    )(w)


def _proj_ln_kernel(a_ref, w_ref, z_ref, mod_ref, g_ref, b_ref, mod2_ref, zo_ref, *ho_refs,
                    nk, gate_j, mod_j, alpha, tr, ncb):
    t, k = pl.program_id(1), pl.program_id(2)
    tm = zo_ref.shape[0]
    rb = tm // 2 if tm % 16 == 0 else tm

    def accumulate(first):
        for r0 in range(0, tm, rb):
            p = jnp.dot(a_ref[r0:r0 + rb, :], w_ref[...], preferred_element_type=F32)
            if first:
                zo_ref[r0:r0 + rb, :] = p
            else:
                zo_ref[r0:r0 + rb, :] += p

    rs = math.gcd(tr, 64)

    def finish():
        for sb in range(tm // rs):
            rows = slice(sb * rs, (sb + 1) * rs)
            is_ctx = t * (tm // tr) + (sb * rs) // tr < ncb
            m = jnp.where(is_ctx, mod_ref[0], mod_ref[1])
            zz = alpha * z_ref[rows, :] + zo_ref[rows, :] * m[gate_j:gate_j + 1]
            mu = jnp.mean(zz, axis=-1, keepdims=True)
            zc = zz - mu
            var = jnp.mean(zc * zc, axis=-1, keepdims=True)
            zn = zc * lax.rsqrt(var + LN_EPS) * g_ref[...] + b_ref[...]
            zo_ref[rows, :] = zn
            if mod_j is not None:
                m2 = jnp.where(is_ctx, mod2_ref[0], mod2_ref[1])
                ho_refs[0][rows, :] = (zn * (1 + m2[mod_j + 1:mod_j + 2]) + m2[mod_j:mod_j + 1]).astype(BF16)

    if nk == 1:
        accumulate(True)
        finish()
    else:
        @pl.when(k == 0)
        def _():
            accumulate(True)

        @pl.when((k > 0) & (k < nk - 1))
        def _():
            accumulate(False)

        @pl.when(k == nk - 1)
        def _():
            accumulate(False)
            finish()


def _proj_ln(a, w, g, z, mod, ln_g, ln_b, mod2, *, gate_j, mod_j, tr, n_ctx, alpha):
    b_n, l_a, k_n = a.shape
    d = w.shape[2]
    if z.shape[1] != l_a:
        y = _matmul(a.reshape(1, b_n * l_a, k_n), w[g:g + 1])[0].reshape(b_n, l_a, d)
        return _ln_mod(z, y, mod, ln_g, ln_b, mod2, gate_j=gate_j, mod_j=mod_j, tr=tr, n_ctx=n_ctx, alpha=alpha)
    if w.dtype != BF16:
        w = _to_bf16(w)
    tm = tr * _pick(l_a // tr, (3, 2, 1))
    tk = k_n if k_n <= 2048 else _pick(k_n, (1408, 1024, 512, 256, 128))
    nk = k_n // tk
    row = pl.BlockSpec((None, tm, d), lambda b, t, k: (b, t, 0))
    seg = pl.BlockSpec((None, 2, 6, d), lambda b, t, k: (b, 0, 0, 0))
    vec = pl.BlockSpec((1, d), lambda b, t, k: (0, 0))
    out_shape = [jax.ShapeDtypeStruct((b_n, l_a, d), F32)]
    out_specs = [row]
    if mod_j is not None:
        out_shape.append(jax.ShapeDtypeStruct((b_n, l_a, d), BF16))
        out_specs.append(row)
    res = pl.pallas_call(
        functools.partial(_proj_ln_kernel, nk=nk, gate_j=gate_j, mod_j=mod_j, alpha=alpha, tr=tr,
                          ncb=n_ctx // tr),
        grid=(b_n, l_a // tm, nk),
        in_specs=[pl.BlockSpec((None, tm, tk), lambda b, t, k: (b, t, k)),
                  pl.BlockSpec((None, tk, d), lambda b, t, k: (g, k, 0),
                               pipeline_mode=pl.Buffered(1) if nk == 1 else None),
                  row, seg, vec, vec, seg],
        out_specs=out_specs,
        out_shape=out_shape,
        compiler_params=_cparams(("parallel", "parallel", "arbitrary"), 56),
        name="proj_ln",
    )(a, w, z, mod, ln_g.reshape(1, d), ln_b.reshape(1, d), mod2)
    return res if mod_j is not None else (res[0], None)


def _seg_shift(x, row, shift, n_ctx):
    l_n = x.shape[0]
    rolled = pltpu.roll(x, (-shift) % l_n, 0)
    src = row + shift
    same_seg = (src >= 0) & (src < l_n) & ((src < n_ctx) == (row < n_ctx))
    return jnp.where(same_seg, rolled, 0.0)


def _rw_mix_kernel(z_ref, mod_ref, mu_ref, o_ref, *, n_ctx):
    z = z_ref[...]
    row = lax.broadcasted_iota(jnp.int32, z.shape, 0)
    is_ctx = row < n_ctx
    shift = jnp.where(is_ctx, mod_ref[0, 0:1, :], mod_ref[1, 0:1, :])
    scale = jnp.where(is_ctx, mod_ref[0, 1:2, :], mod_ref[1, 1:2, :])
    h = z * (1 + scale) + shift
    dx = 0.5 * (_seg_shift(h, row, -1, n_ctx) + _seg_shift(h, row, 1, n_ctx)) - h
    for n in range(6):
        o_ref[n] = (h + dx * mu_ref[n:n + 1, :]).astype(BF16)


def _rw_mix(z, mod, mu, *, n_ctx):
    b_n, l_n, d = z.shape
    tc = _pick(d, (256, 128))
    return pl.pallas_call(
        functools.partial(_rw_mix_kernel, n_ctx=n_ctx),
        grid=(b_n, d // tc),
        in_specs=[pl.BlockSpec((None, l_n, tc), lambda b, j: (b, 0, j)),
                  pl.BlockSpec((None, 2, 6, tc), lambda b, j: (b, 0, 0, j)),
                  pl.BlockSpec((6, tc), lambda b, j: (0, j))],
        out_specs=pl.BlockSpec((6, None, l_n, tc), lambda b, j: (0, b, 0, j)),
        out_shape=jax.ShapeDtypeStruct((6, b_n, l_n, d), BF16),
        compiler_params=_cparams(("parallel", "parallel"), 48),
        name="rw_mix",
    )(z, mod, mu)


def _lora_kernel(x_ref, a_ref, b_ref, o_ref, *, act):
    t = jnp.dot(x_ref[...], a_ref[...], preferred_element_type=F32)
    if act == "tanh":
        t = jnp.tanh(t)
    elif act == "sigmoid":
        t = _sigmoid(t)
    o_ref[...] = jnp.dot(t.astype(BF16), b_ref[...], preferred_element_type=F32)


def _lora(xs, x_idx, a, b, act):
    g_n, d, r = a.shape
    m_n = xs.shape[1]
    tm = _pick(m_n, (512, 256, 128, 64))
    return pl.pallas_call(
        functools.partial(_lora_kernel, act=act),
        grid=(g_n, m_n // tm),
        in_specs=[pl.BlockSpec((None, tm, d), lambda g, i: (x_idx, i, 0)),
                  pl.BlockSpec((None, d, r), lambda g, i: (g, 0, 0)),
                  pl.BlockSpec((None, r, d), lambda g, i: (g, 0, 0))],
        out_specs=pl.BlockSpec((None, tm, d), lambda g, i: (g, i, 0)),
        out_shape=jax.ShapeDtypeStruct((g_n, m_n, d), F32),
        compiler_params=_cparams(("parallel", "parallel"), 40),
        name="lora",
    )(xs, a, b)


def _chunk_of(q, ncc, nc, rev):
    if not rev:
        return q
    return jnp.where(q < ncc, ncc - 1 - q, nc - 1 - (q - ncc))


def _rwkv_kernel(r_ref, k_ref, v_ref, lw_ref, la_ref, g_ref, w0_ref, a0_ref, kk_ref, ka_ref, rk_ref,
                 gng_ref, gnb_ref, o_ref,
                 y_scr, mr_s, n_s, *, n_ctx):
    t_n = CHUNK
    h2 = 2 * t_n
    l_n = r_ref.shape[0]
    nc, ncc = l_n // t_n, n_ctx // t_n
    group = _pick(nc, (6, 4, 3, 2, 1))
    lane = lax.broadcasted_iota(jnp.int32, (1, LANES), 1)
    m1 = jnp.where(lane < RW_HEAD, 1.0, 0.0)
    m2 = 1.0 - m1
    ri = lax.broadcasted_iota(jnp.int32, (LANES, LANES), 0)
    ci = lax.broadcasted_iota(jnp.int32, (LANES, LANES), 1)
    same_head = (ri // RW_HEAD) == (ci // RW_HEAD)
    gsum_b = jnp.where(same_head, 1.0, 0.0).astype(BF16)
    gavg_b = jnp.where(same_head, 1.0 / RW_HEAD, 0.0).astype(BF16)
    eye = jnp.where(ri == ci, 1.0, 0.0)
    tr_i, tc_i = ri % t_n, ci % t_n
    blk_base = (ri // INV_BASE) == (ci // INV_BASE)
    sizes = [INV_BASE * 2 ** i for i in range(1, 8) if INV_BASE * 2 ** i <= t_n]
    blk_sibling = [((ri // b) == (ci // b)) & ((ri // (b // 2)) != (ci // (b // 2))) for b in sizes]
    k_k, k_a = kk_ref[...], ka_ref[...]

    def stack(x):
        return jnp.concatenate([x * m1, x * m2], axis=0)

    def rows_of(c):
        return pl.ds(pl.multiple_of(c * t_n, t_n), t_n)

    def head_sum(x, w_b):
        return _mdot(_parts(x, 2), [w_b], order=2)

    tri3_b = [_cumsum_matrix(t_n, rev) for rev in (False, True)]
    strict = [(tc_i > tr_i) if rev else (tc_i < tr_i) for rev in (False, True)]
    incl = [(tc_i >= tr_i) if rev else (tc_i <= tr_i) for rev in (False, True)]

    def stage_prep(d, c):
        rows = rows_of(c)
        k, r, v = k_ref[rows, :], r_ref[rows, :], v_ref[rows, :]
        kkr = k * k_k
        lw = -RW_DECAY_SCALE * _sigmoid(w0_ref[d:d + 1, :] + lw_ref[d, rows, :])
        a = _sigmoid(a0_ref[d:d + 1, :] + la_ref[d, rows, :])
        return dict(d=d, c=c, k=k, r=r, v=v, lw=lw, a=a, kkr=kkr, ss=head_sum(kkr * kkr, gsum_b),
                    cum=_cumsum(tri3_b[d], lw))

    def stage_amat(s):
        d, cum, lw, a = s["d"], s["cum"], s["lw"], s["a"]
        kk = s["kkr"] * lax.rsqrt(s["ss"] + 1e-12)
        kd = s["k"] * (1 + (a - 1) * k_a)
        bv = kk * a
        p_end = cum[0:1, :] if d == 1 else cum[t_n - 1:t_n, :]
        e_m = jnp.exp(-cum)
        e_h = jnp.exp(p_end - cum)
        ktp = _parts(stack(kk * jnp.exp(cum - lw)), 2)
        rt = stack(s["r"] * jnp.exp(cum))
        k2p = _cat_parts([_parts(stack(bv * e_m), 1), _parts(stack(kd * e_m), 1)], 0)
        return dict(d=d, c=s["c"], ktp=ktp, rt=rt, vp=_parts(stack(s["v"]), 1), p_end=p_end,
                    bh=stack(bv * e_h), kh=stack(kd * e_h),
                    amat=_mdot(_cat_parts([ktp, _parts(rt, 2)], 0), k2p, NT))

    def stage_square(s):
        d, amat = s["d"], s["amat"]
        lt = jnp.where(strict[d], amat[:h2, :h2], 0.0).T
        ldt = jnp.where(blk_base, lt, 0.0)
        ltp = _parts(ldt, 2)
        msk = jnp.concatenate([jnp.where(strict[d], amat[:h2, h2:], 0.0),
                               jnp.where(incl[d], amat[h2:, h2:], 0.0)], axis=0)
        s = dict(s, pt=eye - ldt, xt=_mdot(ltp, ltp[:1]),
                 ct=[jnp.where(m, lt, 0.0).astype(BF16) for m in blk_sibling],
                 av=_mdot(_parts(msk, 2), s["vp"][:1]),
                 arbp=_parts(jnp.where(incl[d], amat[h2:, :h2], 0.0), 2))
        del s["amat"]
        return s

    def stage_merge_a(s, level):
        return dict(s, t1=_mdot([s["ct"][level]], _parts(s["pt"], 2)).astype(BF16))

    def stage_merge_b(s):
        return dict(s, pt=s["pt"] - _mdot(_parts(s["pt"], 2), [s["t1"]]))

    def stage_double(s, final):
        xh = _parts(s["xt"], 1)
        ptp = _parts(s["pt"], 2)
        if final:
            return dict(s, pt=s["pt"] + _mdot(xh, ptp))
        rhs = [jnp.concatenate([ptp[0], xh[0]], axis=1), jnp.concatenate([ptp[1], jnp.zeros_like(xh[0])], axis=1)]
        both = _mdot(xh, rhs)
        return dict(s, pt=s["pt"] + both[:, :LANES], xt=both[:, LANES:])

    def stage_solve(s):
        rhs = jnp.concatenate([s["ktp"][0], (-s["av"][:h2]).astype(BF16)], axis=1)
        return dict(s, wub=_mdot(_parts(s["pt"].T, 2), [rhs]).astype(BF16))

    def stage_fold(s):
        d, c, wub = s["d"], s["c"], s["wub"]
        aw = _mdot(s["arbp"], [wub])
        zb = jnp.zeros((h2, LANES), BF16)
        lhs = _cat_parts([_parts(s["bh"].T, 2), _parts(s["kh"].T, 2)], 1)
        rhs = jnp.concatenate([wub, jnp.concatenate([zb, s["vp"][0]], axis=1)], axis=0)
        mn = _mdot(lhs, [rhs])
        dg = jnp.where(ri == ci, jnp.broadcast_to(jnp.exp(s["p_end"]), (LANES, LANES)), 0.0)
        rp = s["rt"] - aw[:, :LANES]
        mrp = _parts(jnp.concatenate([dg - mn[:, :LANES], rp[:t_n] + rp[t_n:]], axis=0), 2)
        for i in range(2):
            mr_s[d, c, i] = mrp[i]
        n_s[d, c] = mn[:, LANES:]
        y0 = s["av"][h2:] + aw[:, LANES:]
        return y0[:t_n] + y0[t_n:]

    def seq(q, hs):
        cs = (q, _chunk_of(q, ncc, nc, True))
        mh = [_mdot([mr_s[d, cs[d], 0], mr_s[d, cs[d], 1]], _parts(hs[d], 2)) for d in (0, 1)]
        for d in (0, 1):
            y_scr[rows_of(cs[d]), :] += mh[d][h2:]
        return tuple(mh[d][:h2] + n_s[d, cs[d]] for d in (0, 1))

    def local(i, hs, with_seq):
        steps = [i * group + g for g in range(group)]
        pending = [q - group for q in steps] if with_seq else []
        sts = [stage_prep(d, q if d == 0 else _chunk_of(q, ncc, nc, True)) for q in steps for d in (0, 1)]
        n_dbl = INV_BASE.bit_length() - 2
        stages = [stage_amat, stage_square]
        stages += [functools.partial(stage_double, final=i == n_dbl - 1) for i in range(n_dbl)]
        for level in range(len(blk_sibling)):
            stages += [functools.partial(stage_merge_a, level=level), stage_merge_b]
        stages += [stage_solve]
        for stage in stages:
            sts = [stage(s) for s in sts]
            if pending:
                hs = seq(pending.pop(0), hs)
        for s in sts:
            y0 = stage_fold(s)
            y_scr[rows_of(s["c"]), :] += y0
        while pending:
            hs = seq(pending.pop(0), hs)
        return hs

    y_scr[...] = jnp.zeros(y_scr.shape, F32)
    zero = jnp.zeros((LANES, LANES), F32)
    n_trip = nc // group
    hs = local(0, (zero, zero), False)
    hs = lax.fori_loop(1, n_trip, functools.partial(local, with_seq=True), hs)
    for q in range((n_trip - 1) * group, nc):
        hs = seq(q, hs)

    n_post = _pick(nc, (9, 6, 4, 3, 2, 1))

    def post(i, carry):
        rows = [rows_of(i * n_post + g) for g in range(n_post)]

        def bonus_sum(rw):
            k, r = k_ref[rw, :], r_ref[rw, :]
            kd_f = k * (1 + (_sigmoid(a0_ref[0:1, :] + la_ref[0, rw, :]) - 1) * k_a)
            kd_b = k * (1 + (_sigmoid(a0_ref[1:2, :] + la_ref[1, rw, :]) - 1) * k_a)
            return head_sum(r * (kd_f + kd_b) * rk_ref[...], gsum_b)
        bsum = [bonus_sum(rw) for rw in rows]
        ys = [y_scr[rw, :] for rw in rows]
        ycs = [y - m for y, m in zip(ys, [head_sum(y, gavg_b) for y in ys])]
        var = [head_sum(yc * yc, gavg_b) for yc in ycs]
        for rw, yc, vr, bs in zip(rows, ycs, var, bsum):
            yn = yc * lax.rsqrt(vr + RW_GN_EPS) * gng_ref[...] + gnb_ref[...]
            o_ref[rw, :] = ((yn + bs * v_ref[rw, :]) * g_ref[rw, :]).astype(BF16)
        return carry

    lax.fori_loop(0, nc // n_post, post, 0)


def _rwkv_scan(rkv, lw, la, g, w0, a0, k_k, k_a, r_k, gn_g, gn_b, *, n_ctx):
    _, b_n, l_n, d = rkv.shape
    nc = l_n // CHUNK
    col = lambda n: pl.BlockSpec((None, None, l_n, LANES), lambda b, p, n=n: (n, b, 0, p))
    two = pl.BlockSpec((2, None, l_n, LANES), lambda b, p: (0, b, 0, p))
    par = lambda rows: pl.BlockSpec((rows, LANES), lambda b, p: (0, p))
    return pl.pallas_call(
        functools.partial(_rwkv_kernel, n_ctx=n_ctx),
        grid=(b_n, d // LANES),
        in_specs=[col(0), col(1), col(2), two, two,
                  pl.BlockSpec((None, l_n, LANES), lambda b, p: (b, 0, p)),
                  par(2), par(2), par(1), par(1), par(1), par(1), par(1)],
        out_specs=pl.BlockSpec((None, l_n, LANES), lambda b, p: (b, 0, p)),
        out_shape=jax.ShapeDtypeStruct((b_n, l_n, d), BF16),
        scratch_shapes=[pltpu.VMEM((l_n, LANES), F32),
                        pltpu.VMEM((2, nc, 2, LANES + CHUNK, LANES), BF16),
                        pltpu.VMEM((2, nc, LANES, LANES), F32)],
        compiler_params=_cparams(("parallel", "parallel"), 56),
        name="rwkv_scan",
    )(rkv, rkv, rkv, lw, la, g, w0, a0, k_k.reshape(1, d), k_a.reshape(1, d), r_k.reshape(1, d),
      gn_g.reshape(1, d), gn_b.reshape(1, d))


def _pad_axis(w, axis, to):
    pad = [(0, 0)] * w.ndim
    pad[axis] = (0, to - w.shape[axis])
    return jnp.pad(w, pad)


def _rwkv7_layer(z, mod, n_ctx, mu, w_rkv, w0, w1, w2, a0, a1, a2, g1, g2, k_k, k_a, r_k, gn_g, gn_b):
    b_n, l_n, d = z.shape
    m_n = b_n * l_n
    xs = _rw_mix(z, mod, mu, n_ctx=n_ctx).reshape(6, m_n, d)
    rkv = _matmul(xs, w_rkv)
    r_w = -(-w1.shape[-1] // LANES) * LANES
    r_a = -(-a1.shape[-1] // LANES) * LANES
    lw = _lora(xs, 3, _pad_axis(w1, 2, r_w).astype(BF16), _pad_axis(w2, 1, r_w).astype(BF16), "tanh")
    la = _lora(xs, 4, _pad_axis(a1, 2, r_a).astype(BF16), _pad_axis(a2, 1, r_a).astype(BF16), None)
    gate = _lora(xs, 5, g1[None].astype(BF16), g2[None].astype(BF16), "sigmoid")
    o = _rwkv_scan(rkv.reshape(3, b_n, l_n, d), lw.reshape(2, b_n, l_n, d), la.reshape(2, b_n, l_n, d),
                   gate.reshape(b_n, l_n, d), w0, a0, k_k, k_a, r_k, gn_g, gn_b, n_ctx=n_ctx)
    return o


def _qkv_rope_kernel(a_ref, w_ref, cos_ref, sa_ref, sb_ref, o_ref, wb_ref, *, n_q, n_qk):
    j, i = pl.program_id(0), pl.program_id(1)

    @pl.when(i == 0)
    def _():
        wb_ref[...] = w_ref[...].astype(BF16)

    tn = o_ref.shape[1]
    sw = math.gcd(tn, 2 * DA_HEAD)

    def product(s):
        return jnp.dot(a_ref[...], wb_ref[:, s * sw:(s + 1) * sw], preferred_element_type=F32)

    @pl.when(j < n_qk)
    def _():
        q = DA_HEAD // 4
        q_scale = jnp.where(j < n_q, DA_HEAD ** -0.5 * math.log2(math.e), 1.0)
        cos, s_a, s_b = cos_ref[...] * q_scale, sa_ref[...] * q_scale, sb_ref[...] * q_scale

        def rotary(s, x):
            for c in range(0, sw, DA_HEAD):
                xs = x[:, c:c + DA_HEAD]
                rot = xs * cos + pltpu.roll(xs, DA_HEAD - q, 1) * s_a + pltpu.roll(xs, q, 1) * s_b
                o_ref[:, s * sw + c:s * sw + c + DA_HEAD] = rot.astype(BF16)
        x_prev = product(0)
        for s in range(1, tn // sw):
            x_next = product(s)
            rotary(s - 1, x_prev)
            x_prev = x_next
        rotary(tn // sw - 1, x_prev)

    @pl.when(j >= n_qk)
    def _():
        for s in range(tn // sw):
            o_ref[:, s * sw:(s + 1) * sw] = product(s).astype(BF16)


def _qkv_rope(h, w_qkv, cos, s_a, s_b):
    b_n, l_n, d = h.shape
    d3 = w_qkv.shape[1]
    tm = _pick(l_n, (768, 1024, 512, 256, 128, 64))
    tn = _pick(d, (1024, 512, 256, 128))
    n_t = l_n // tm
    tab = pl.BlockSpec((tm, DA_HEAD), lambda j, i: (i % n_t, 0))
    out = pl.pallas_call(
        functools.partial(_qkv_rope_kernel, n_q=d // tn, n_qk=2 * d // tn),
        grid=(d3 // tn, b_n * n_t),
        in_specs=[pl.BlockSpec((tm, d), lambda j, i: (i, 0)),
                  pl.BlockSpec((d, tn), lambda j, i: (0, j)), tab, tab, tab],
        out_specs=pl.BlockSpec((tm, tn), lambda j, i: (i, j)),
        out_shape=jax.ShapeDtypeStruct((b_n * l_n, d3), BF16),
        scratch_shapes=[pltpu.VMEM((d, tn), BF16)],
        compiler_params=_cparams(("parallel", "arbitrary"), 48),
        name="qkv_rope",
    )(h.reshape(b_n * l_n, d), w_qkv, cos, s_a, s_b)
    return out.reshape(b_n, l_n, d3)


def _attn_kernel(q_ref, k_ref, v_ref, lam_ref, sg_ref, o_ref, *, ncb, n_ctx, lam_init):
    qi = pl.program_id(2)
    lv = lam_ref[...]
    lam = (jnp.exp(jnp.sum(lv[0:1] * lv[1:2], axis=-1, keepdims=True))
           - jnp.exp(jnp.sum(lv[2:3] * lv[3:4], axis=-1, keepdims=True)) + lam_init)

    hw = 2 * DA_HEAD
    n_hh = o_ref.shape[1] // hw

    def attend(nk):
        def scores(hh, m):
            cols = slice(hh * hw + m * DA_HEAD, hh * hw + (m + 1) * DA_HEAD)
            return lax.dot_general(q_ref[:, cols], k_ref[0:nk, cols], NT, preferred_element_type=F32)
        s_all = [[scores(hh, m) for m in (0, 1)] for hh in range(n_hh)]
        for hh in range(n_hh):
            def probs(s):
                e = jnp.exp2(s - jnp.max(s, axis=-1, keepdims=True))
                return e, 1.0 / jnp.sum(e, axis=-1, keepdims=True)
            e0, i0 = probs(s_all[hh][0])
            e1, i1 = probs(s_all[hh][1])
            v = v_ref[0:nk, hh * hw:(hh + 1) * hw]
            o = (jnp.dot(e0.astype(BF16), v, preferred_element_type=F32) * i0
                 - jnp.dot(e1.astype(BF16), v, preferred_element_type=F32) * (lam * i1))
            o = o * lax.rsqrt(jnp.mean(o * o, axis=-1, keepdims=True) + 1e-5) * sg_ref[...] * (1 - lam_init)
            o_ref[:, hh * hw:(hh + 1) * hw] = o.astype(BF16)

    if ncb > 0:
        @pl.when(qi < ncb)
        def _():
            attend(n_ctx)

    @pl.when(qi >= ncb)
    def _():
        attend(k_ref.shape[0])


def _attention(qkv, lam_vec, sub_g, *, tq, n_ctx, lam_init):
    b_n, l_n, d3 = qkv.shape
    d = d3 // 3
    hw = 2 * DA_HEAD
    n_hh = 2 if (d // hw) % 2 == 0 else 1
    bw = n_hh * hw
    nh = d // bw
    return pl.pallas_call(
        functools.partial(_attn_kernel, ncb=n_ctx // tq, n_ctx=n_ctx, lam_init=lam_init),
        grid=(b_n, nh, l_n // tq),
        in_specs=[pl.BlockSpec((None, tq, bw), lambda b, h, t: (b, t, h)),
                  pl.BlockSpec((None, l_n, bw), lambda b, h, t: (b, 0, nh + h)),
                  pl.BlockSpec((None, l_n, bw), lambda b, h, t: (b, 0, 2 * nh + h)),
                  pl.BlockSpec((4, DA_HEAD), lambda b, h, t: (0, 0)),
                  pl.BlockSpec((1, hw), lambda b, h, t: (0, 0))],
        out_specs=pl.BlockSpec((None, tq, bw), lambda b, h, t: (b, t, h)),
        out_shape=jax.ShapeDtypeStruct((b_n, l_n, d), BF16),
        compiler_params=_cparams(("parallel", "parallel", "arbitrary"), 48),
        name="diff_attn",
    )(qkv, qkv, qkv, lam_vec, sub_g.reshape(1, hw))


def _rope_tables(n_ctx, n_lat):
    n_rows = n_lat // GRID_W
    row = jnp.repeat(jnp.arange(n_rows, dtype=F32), GRID_W)
    col = jnp.tile(jnp.arange(GRID_W, dtype=F32), n_rows)
    nf = DA_HEAD // 4
    inv_freq = ROPE_BASE ** (-jnp.arange(nf, dtype=F32) / nf)
    ang_r, ang_c = row[:, None] * inv_freq, col[:, None] * inv_freq
    ang = jnp.concatenate([ang_r, ang_r, ang_c, ang_c], axis=-1)
    ang = jnp.concatenate([jnp.zeros((n_ctx, DA_HEAD), F32), ang], axis=0)
    cos, sin = jnp.cos(ang), jnp.sin(ang)
    even_q = (jnp.arange(DA_HEAD) // nf) % 2 == 0
    return cos, jnp.where(even_q, -sin, 0.0), jnp.where(even_q, 0.0, sin)


def _diff_attention_layer(h, n_ctx, layer_idx, w_qkv, lam_vec, sub_g, *, tr):
    b_n, l_n, d = h.shape
    cos, s_a, s_b = _rope_tables(n_ctx, l_n - n_ctx)
    qkv = _qkv_rope(h, w_qkv, cos, s_a, s_b)
    lam_init = 0.8 - 0.6 * math.exp(-0.3 * layer_idx)
    o = _attention(qkv, lam_vec, sub_g, tq=tr, n_ctx=n_ctx, lam_init=lam_init)
    return o


def _hgrn_kernel(q_ref, i_ref, g_ref, ff_ref, fb_ref, low_ref, ng_ref, o_ref, o_scr, *, n_ctx, layer_idx):
    t_n = CHUNK
    l_n = q_ref.shape[0]
    nc, ncc = l_n // t_n, n_ctx // t_n
    r64 = lax.broadcasted_iota(jnp.int32, (t_n, t_n), 0)
    c64 = lax.broadcasted_iota(jnp.int32, (t_n, t_n), 1)

    def rows_of(c):
        return pl.ds(pl.multiple_of(c * t_n, t_n), t_n)

    f_refs = (ff_ref, fb_ref)
    lbs, incl = [], []
    tri3_b = [_cumsum_matrix(t_n, rev) for rev in (False, True)]
    for d in (0, 1):
        low = low_ref[d]
        e = jnp.exp(low - jnp.max(low, axis=0, keepdims=True))
        sm = e / jnp.sum(e, axis=0, keepdims=True)
        cs = sm[0:1]
        for rr in range(1, layer_idx + 1):
            cs = cs + sm[rr:rr + 1]
        lbs.append(cs - sm[0:1])
        incl.append((c64 >= r64) if d == 1 else (c64 <= r64))
    group = _pick(nc, (6, 4, 2, 1))

    def stage_cum(d, c):
        rows = rows_of(c)
        f = lbs[d] + (1.0 - lbs[d]) * _sigmoid(f_refs[d][rows, :])
        return dict(d=d, rows=rows, f=f, cum=_cumsum(tri3_b[d], jnp.log(f)))

    def stage_att(s):
        d, cum, rows = s["d"], s["cum"], s["rows"]
        b_end = cum[0:1, :] if d == 1 else cum[t_n - 1:t_n, :]
        qv = q_ref[rows, :]
        qd = (qv * _sigmoid(qv) * jnp.exp(cum)).astype(BF16)
        kk = 1.0 - s["f"]
        v = i_ref[rows, :]
        kd = (kk * jnp.exp(-cum)).astype(BF16)
        ke = (kk * jnp.exp(b_end - cum)).astype(BF16)
        return dict(d=d, rows=rows, qd=qd, vb=v.astype(BF16), dec=jnp.exp(b_end),
                    att=lax.dot_general(qd, kd, NT, preferred_element_type=F32),
                    upd=jnp.dot(v.T.astype(BF16), ke, preferred_element_type=F32))

    def stage_intra(s):
        att = jnp.where(incl[s["d"]], s["att"], 0.0).astype(BF16)
        return dict(s, o=jnp.dot(att, s["vb"], preferred_element_type=F32))

    def body(i, states):
        items = [(d, _chunk_of(i * group + g, ncc, nc, d == 1)) for g in range(group) for d in (0, 1)]
        sts = [stage_cum(d, c) for d, c in items]
        sts = [stage_att(s) for s in sts]
        sts = [stage_intra(s) for s in sts]
        states = list(states)
        for s in sts:
            d = s["d"]
            o = s["o"] + lax.dot_general(s["qd"], states[d].astype(BF16), NT, preferred_element_type=F32)
            o_scr[d, s["rows"], :] = o
            states[d] = states[d] * s["dec"] + s["upd"]
        return tuple(states)

    zero = jnp.zeros((LANES, LANES), F32)
    lax.fori_loop(0, nc // group, body, (zero, zero))

    p_n = t_n * _pick(nc, (4, 3, 2, 1))

    def post(c, carry):
        rows = pl.ds(pl.multiple_of(c * p_n, p_n), p_n)
        o = o_scr[0, rows, :] + o_scr[1, rows, :]
        o = o * lax.rsqrt(jnp.mean(o * o, axis=-1, keepdims=True) + 1e-5) * ng_ref[...]
        gv = g_ref[rows, :]
        o_ref[rows, :] = (o * (gv * _sigmoid(gv))).astype(BF16)
        return carry

    lax.fori_loop(0, l_n // p_n, post, 0)


def _hgrn2_layer(h, n_ctx, layer_idx, w_in, lower, norm_g):
    b_n, l_n, d = h.shape
    m_n = b_n * l_n
    nh = d // HG_EXPAND
    proj = _mm2(h.reshape(m_n, d), w_in).reshape(b_n, l_n, 5 * d)
    col = lambda n: pl.BlockSpec((None, l_n, LANES), lambda b, p, n=n: (b, 0, n * nh + p))
    o = pl.pallas_call(
        functools.partial(_hgrn_kernel, n_ctx=n_ctx, layer_idx=layer_idx),
        grid=(b_n, nh),
        in_specs=[col(0), col(1), col(2), col(3), col(4),
                  pl.BlockSpec((2, lower.shape[1], LANES), lambda b, p: (0, 0, p)),
                  pl.BlockSpec((1, LANES), lambda b, p: (0, 0))],
        out_specs=pl.BlockSpec((None, l_n, LANES), lambda b, p: (b, 0, p)),
        out_shape=jax.ShapeDtypeStruct((b_n, l_n, d), BF16),
        scratch_shapes=[pltpu.VMEM((2, l_n, LANES), F32)],
        compiler_params=_cparams(("parallel", "parallel"), 40),
        name="hgrn_scan",
    )(proj, proj, proj, proj, proj, lower, norm_g.reshape(1, LANES))
    return o


def _gelu_tanh(x):
    return 0.5 * x * (1.0 + jnp.tanh(math.sqrt(2.0 / math.pi) * (x + 0.044715 * (x * x * x))))


def _softplus(x):
    return jnp.maximum(x, 0.0) + jnp.log1p(jnp.exp(-jnp.abs(x)))


SEG_PAD = 8


def _lin_scan(a_ref, u_ref, hl_s, cp_s, h_s, base, row0, n, h_in):
    seg = n // 8
    stride = seg + SEG_PAD
    n_p = a_ref.shape[1]
    chains = [(d, j) for d in (0, 1) for j in range(n_p)]

    def step(i, carry):
        out = []
        for (d, j), (hl, cp) in zip(chains, carry):
            idx = pl.ds(base + (i if d == 0 else seg - 1 - i), 8, stride=stride)
            a = a_ref[d, j, idx, :]
            hl = a * hl + u_ref[d, j, idx, :]
            cp = a * cp
            hl_s[d, j, idx, :] = hl
            cp_s[d, j, idx, :] = cp
            out.append((hl, cp))
        return tuple(out)

    init = tuple((jnp.zeros((8, LANES), F32), jnp.ones((8, LANES), F32)) for _ in chains)
    ends = lax.fori_loop(0, seg, step, init)
    h_out = [[None] * n_p, [None] * n_p]
    for (d, j), (hl_e, cp_e) in zip(chains, ends):
        carry = h_in[d][j]
        for s in (range(8) if d == 0 else range(7, -1, -1)):
            r0, p0 = row0 + s * seg, base + s * stride
            blk = hl_s[d, j, p0:p0 + seg, :] + cp_s[d, j, p0:p0 + seg, :] * carry
            if d == 0:
                h_s[j, r0:r0 + seg, :] = blk
            else:
                h_s[j, r0:r0 + seg, :] += blk
            carry = hl_e[s:s + 1, :] + cp_e[s:s + 1, :] * carry
        h_out[d][j] = carry
    return h_out


def _rglru_kernel(gb_ref, xb_ref, cw_ref, cb_ref, wg_ref, bg_ref, lam_ref, o_ref, a_s, u_s, h_s, hl_s, cp_s, *,
                  n_ctx):
    l_n = xb_ref.shape[0]
    n_lat = l_n - n_ctx
    x = xb_ref[...]
    row = lax.broadcasted_iota(jnp.int32, x.shape, 0)
    k_w = cw_ref.shape[0]
    xc = cb_ref[...] + sum(_seg_shift(x, row, j - (k_w - 1) // 2, n_ctx) * cw_ref[j:j + 1, :]
                           for j in range(k_w))
    xcb = xc.astype(BF16)
    n_p = x.shape[1] // LANES
    for d in (0, 1):
        gate = lambda g: _sigmoid(jnp.dot(xcb, wg_ref[d, g].astype(BF16), preferred_element_type=F32)
                                  + bg_ref[d, g:g + 1, :])
        log_a = -LR_C * gate(0) * _softplus(-lam_ref[d:d + 1, :])
        a = jnp.exp(log_a)
        u = jnp.sqrt(jnp.tanh(-log_a) * (jnp.exp(2.0 * log_a) + 1.0)) * gate(1) * xc
        base = 0
        for row0, n in ((0, n_ctx), (n_ctx, n_lat)):
            seg = n // 8
            for j in range(n_p if n else 0):
                for s in range(8):
                    src = slice(row0 + s * seg, row0 + (s + 1) * seg)
                    dst = slice(base + s * (seg + SEG_PAD), base + s * (seg + SEG_PAD) + seg)
                    a_s[d, j, dst, :] = a[src, j * LANES:(j + 1) * LANES]
                    u_s[d, j, dst, :] = u[src, j * LANES:(j + 1) * LANES]
            base += 8 * (seg + SEG_PAD) if n else 0
    h = [[jnp.zeros((1, LANES), F32)] * n_p] * 2
    base = 0
    for row0, n in ((0, n_ctx), (n_ctx, n_lat)):
        if n:
            h = _lin_scan(a_s, u_s, hl_s, cp_s, h_s, base, row0, n, h)
            base += 8 * (n // 8 + SEG_PAD)
    for j in range(n_p):
        cols = slice(j * LANES, (j + 1) * LANES)
        o_ref[:, cols] = (h_s[j, n_ctx:l_n, :] * _gelu_tanh(gb_ref[n_ctx:, cols])).astype(BF16)


def _rglru_layer(h, n_ctx, w_in, conv_w, conv_b, w_gate, b_gate, lam):
    b_n, l_n, d = h.shape
    n_lat = l_n - n_ctx
    nb = d // LR_BS
    proj = _mm2(h.reshape(b_n * l_n, d), w_in).reshape(b_n, l_n, 2 * d)
    k_w = conv_w.shape[0]
    plane = (LR_BS // LANES, l_n + 16 * SEG_PAD, LANES)
    o = pl.pallas_call(
        functools.partial(_rglru_kernel, n_ctx=n_ctx),
        grid=(b_n, nb),
        in_specs=[pl.BlockSpec((None, l_n, LR_BS), lambda b, j: (b, 0, j)),
                  pl.BlockSpec((None, l_n, LR_BS), lambda b, j: (b, 0, nb + j)),
                  pl.BlockSpec((k_w, LR_BS), lambda b, j: (0, j)),
                  pl.BlockSpec((1, LR_BS), lambda b, j: (0, j)),
                  pl.BlockSpec((2, 2, None, LR_BS, LR_BS), lambda b, j: (0, 0, j, 0, 0)),
                  pl.BlockSpec((2, 2, LR_BS), lambda b, j: (0, 0, j)),
                  pl.BlockSpec((2, LR_BS), lambda b, j: (0, j))],
        out_specs=pl.BlockSpec((None, n_lat, LR_BS), lambda b, j: (b, 0, j)),
        out_shape=jax.ShapeDtypeStruct((b_n, n_lat, d), BF16),
        scratch_shapes=[pltpu.VMEM((2,) + plane, F32), pltpu.VMEM((2,) + plane, F32), pltpu.VMEM(plane, F32),
                        pltpu.VMEM((2,) + plane, F32), pltpu.VMEM((2,) + plane, F32)],
        compiler_params=_cparams(("parallel", "parallel"), 56),
        name="rglru",
    )(proj, proj, conv_w, conv_b.reshape(1, d), w_gate, b_gate, lam)
    return o


def _ffn_up_kernel(h_ref, wg_ref, wv_ref, cg_ref, cv_ref, bg_ref, bv_ref, o_ref, u_scr, *, n_ctx, tr):
    l_n, tf = o_ref.shape
    k_w = cg_ref.shape[0]
    half = (k_w - 1) // 2
    n_buf, pad = u_scr.shape[0], (u_scr.shape[1] - l_n) // 2
    rblk = tr * _pick(l_n // tr, (3, 4, 2, 1))
    row = lax.broadcasted_iota(jnp.int32, (tr, LANES), 0)
    for p in range(n_buf):
        u_scr[p, 0:pad, :] = jnp.zeros((pad, 2 * LANES), F32)
        u_scr[p, pad + l_n:, :] = jnp.zeros((pad, 2 * LANES), F32)

    def weights(s):
        cols = slice(s * LANES, (s + 1) * LANES)
        return jnp.concatenate([wg_ref[:, cols], wv_ref[:, cols]], axis=1).astype(BF16)

    def product(s, w, r0):
        u_scr[s % n_buf, pad + r0:pad + r0 + rblk, :] = jnp.dot(h_ref[r0:r0 + rblk, :], w,
                                                                preferred_element_type=F32)

    def finish(s, r0):
        p, cols = s % n_buf, slice(s * LANES, (s + 1) * LANES)

        def conv(lane0, w_ref, b_ref):
            acc = None
            for j in range(k_w):
                sh = j - half
                x = u_scr[p, pad + r0 + sh:pad + r0 + sh + tr, lane0:lane0 + LANES]
                if sh < 0 and r0 in (0, n_ctx):
                    x = jnp.where(row < -sh, 0.0, x)
                if sh > 0 and r0 + tr in (n_ctx, l_n):
                    x = jnp.where(row >= tr - sh, 0.0, x)
                t = x * w_ref[j:j + 1, cols]
                acc = t if acc is None else acc + t
            return b_ref[:, cols] + acc
        gate = conv(0, cg_ref, bg_ref)
        val = conv(LANES, cv_ref, bv_ref)
        o_ref[r0:r0 + tr, cols] = (gate * _sigmoid(gate) * val).astype(BF16)

    n_s = tf // LANES
    for s in range(n_s + 1):
        w = weights(s) if s < n_s else None
        for r0 in range(0, l_n, rblk):
            if s < n_s:
                product(s, w, r0)
            if s > 0:
                for r1 in range(r0, r0 + rblk, tr):
                    finish(s - 1, r1)


def _conv_ffn(h, n_ctx, w_up, layer, conv_w, conv_b, *, tr):
    b_n, l_n, d = h.shape
    f = w_up.shape[2] // 2
    tf = _pick(f, (512, 256, 128))
    nf = f // tf
    k_w = conv_w.shape[0]
    cb = conv_b.reshape(1, 2 * f)
    act = pl.pallas_call(
        functools.partial(_ffn_up_kernel, n_ctx=n_ctx, tr=tr),
        grid=(b_n, nf),
        in_specs=[pl.BlockSpec((None, l_n, d), lambda b, j: (b, 0, 0), pipeline_mode=pl.Buffered(1)),
                  pl.BlockSpec((None, d, tf), lambda b, j: (layer, 0, j)),
                  pl.BlockSpec((None, d, tf), lambda b, j: (layer, 0, nf + j)),
                  pl.BlockSpec((k_w, tf), lambda b, j: (0, j)),
                  pl.BlockSpec((k_w, tf), lambda b, j: (0, nf + j)),
                  pl.BlockSpec((1, tf), lambda b, j: (0, j)),
                  pl.BlockSpec((1, tf), lambda b, j: (0, nf + j))],
        out_specs=pl.BlockSpec((None, l_n, tf), lambda b, j: (b, 0, j)),
        out_shape=jax.ShapeDtypeStruct((b_n, l_n, f), BF16),
        scratch_shapes=[pltpu.VMEM((3, l_n + 16, 2 * LANES), F32)],
        compiler_params=_cparams(("parallel", "arbitrary"), 56),
        name="ffn_up",
    )(h, w_up, w_up, conv_w, conv_w, cb, cb)
    return act


def kernel(x, c, ctx, c_ctx, ada_w, ada_b, ln_g, ln_b, ffn_w_up, ffn_conv_w, ffn_conv_b, ffn_w_down, rw_mu, rw_w_rkv, rw_w0, rw_w1, rw_w2, rw_a0, rw_a1, rw_a2, rw_g1, rw_g2, rw_k_k, rw_k_a, rw_r_k, rw_gn_g, rw_gn_b, rw_w_o, da_w_qkv, da_lambda, da_sub_g, da_w_o, hg_w_in, hg_lower, hg_norm_g, hg_w_o, lr_w_in, lr_conv_w, lr_conv_b, lr_w_gate, lr_b_gate, lr_lambda, lr_w_o):
    b_n, n_lat, d = x.shape
    n_ctx = ctx.shape[1]
    depth = ada_w.shape[0]
    assert depth == 4 and rw_mu.shape[0] == 1, "one occurrence of each of the four mixers"
    assert b_n + 1 <= 8 and n_ctx % CHUNK == 0 and n_lat % CHUNK == 0
    tr = math.gcd(math.gcd(n_ctx, n_lat), 256)
    alpha = (2 * depth) ** 0.25

    c8 = jnp.concatenate([c, c_ctx[None], jnp.zeros((8 - b_n - 1, d), F32)], axis=0)
    m = _ada(c8, ada_w, ada_b)
    m_lat = m[:, :b_n].reshape(depth, b_n, 1, 6, d)
    m_ctx = jnp.broadcast_to(m[:, b_n].reshape(depth, 1, 1, 6, d), (depth, b_n, 1, 6, d))
    mod = jnp.concatenate([m_ctx, m_lat], axis=2)

    z = jnp.concatenate([ctx, x], axis=1)
    w_down_b = _to_bf16(ffn_w_down)
    h = None
    for i in range(depth):
        last = i == depth - 1
        if i == 0:
            o = _rwkv7_layer(z, mod[0], n_ctx, rw_mu[0], rw_w_rkv[0], rw_w0[0], rw_w1[0], rw_w2[0], rw_a0[0],
                             rw_a1[0], rw_a2[0], rw_g1[0], rw_g2[0], rw_k_k[0], rw_k_a[0], rw_r_k[0],
                             rw_gn_g[0], rw_gn_b[0])
            w_o = rw_w_o
        elif i == 1:
            o = _diff_attention_layer(h, n_ctx, i, da_w_qkv[0], da_lambda[0], da_sub_g[0], tr=tr)
            w_o = da_w_o
        elif i == 2:
            o = _hgrn2_layer(h, n_ctx, i, hg_w_in[0], hg_lower, hg_norm_g[0])
            w_o = hg_w_o
        else:
            o = _rglru_layer(h, n_ctx, lr_w_in[0], lr_conv_w[0], lr_conv_b[0], lr_w_gate[0], lr_b_gate[0],
                             lr_lambda[0])
            w_o = lr_w_o
        z, h = _proj_ln(o, w_o, 0, z, mod[i], ln_g[i, 0], ln_b[i, 0], mod[i], gate_j=2, mod_j=3, tr=tr,
                        n_ctx=n_ctx, alpha=alpha)
        if last:
            n_ctx = 0
        act = _conv_ffn(h, n_ctx, ffn_w_up, i, ffn_conv_w[i], ffn_conv_b[i], tr=tr)
        z, h = _proj_ln(act, w_down_b, i, z, mod[i], ln_g[i, 1], ln_b[i, 1], mod[min(i + 1, depth - 1)],
                        gate_j=5, mod_j=None if last else 0, tr=tr, n_ctx=n_ctx, alpha=alpha)
    return z
```

```python
import functools
import math

import jax
import jax.numpy as jnp
from jax import lax
from jax.experimental import pallas as pl
from jax.experimental.pallas import tpu as pltpu

F32, BF16 = jnp.float32, jnp.bfloat16

LANES = 128
CHUNK = 64
INV_BASE = 8
LN_EPS = 1e-5
GRID_W = 64
ROPE_BASE = 10000.0
RW_HEAD = 64
RW_DECAY_SCALE = 0.606531
RW_GN_EPS = 64e-5
DA_HEAD = 128
HG_EXPAND = 128
LR_BS = 256
LR_C = 8.0
MIB = 1024 * 1024


def _pick(n, cands):
    for c in cands:
        if n % c == 0:
            return c
    return n


V7X_VMEM_MIB = 64


def _cparams(sem, vmem_mib):
    assert vmem_mib < V7X_VMEM_MIB
    return pltpu.CompilerParams(dimension_semantics=sem, vmem_limit_bytes=vmem_mib * MIB)


def _sigmoid(x):
    return jax.nn.sigmoid(x)


NN = (((1,), (0,)), ((), ()))
NT = (((1,), (1,)), ((), ()))


def _parts(x, n):
    out = []
    for i in range(n):
        p = x.astype(BF16)
        out.append(p)
        if i + 1 < n:
            x = x - p.astype(F32)
    return out


def _mdot(ap, bp, dims=NN, order=2):
    pairs = [(a, b) for i, a in enumerate(ap) for j, b in enumerate(bp) if i + j < order]
    (ca,), (cb,) = dims[0]
    lhs = jnp.concatenate([a for a, _ in pairs], axis=ca) if len(pairs) > 1 else pairs[0][0]
    rhs = jnp.concatenate([b for _, b in pairs], axis=cb) if len(pairs) > 1 else pairs[0][1]
    return lax.dot_general(lhs, rhs, dims, preferred_element_type=F32)


def _cumsum_matrix(t_n, rev):
    r = lax.broadcasted_iota(jnp.int32, (t_n, 3 * t_n), 0)
    c = lax.broadcasted_iota(jnp.int32, (t_n, 3 * t_n), 1) % t_n
    return jnp.where((c >= r) if rev else (c <= r), 1.0, 0.0).astype(BF16)


def _cumsum(tri3_b, x):
    return jnp.dot(tri3_b, jnp.concatenate(_parts(x, 3), axis=0), preferred_element_type=F32)


def _cat_parts(xs, axis):
    return [jnp.concatenate(ps, axis=axis) for ps in zip(*xs)]


def _mm_wres_kernel(a_ref, w_ref, o_ref, wb_ref):
    i = pl.program_id(2)
    k_n = w_ref.shape[0]
    n_kc = 4 if k_n % (4 * LANES) == 0 else 1

    @pl.when(i == 0)
    def _():
        kc = k_n // n_kc
        acc = None
        for c in range(n_kc):
            ks = slice(c * kc, (c + 1) * kc)
            wb = w_ref[ks, :].astype(BF16)
            wb_ref[ks, :] = wb
            t = jnp.dot(a_ref[:, ks], wb, preferred_element_type=F32)
            acc = t if acc is None else acc + t
        o_ref[...] = acc.astype(o_ref.dtype)

    @pl.when(i > 0)
    def _():
        o_ref[...] = jnp.dot(a_ref[...], wb_ref[...], preferred_element_type=F32).astype(o_ref.dtype)


def _matmul(a, w, *, out_dtype=F32, a_off=0):
    g_n, k_n, n_n = w.shape
    m_n = a.shape[1]
    assert w.dtype == F32 and k_n <= 2048, "weight block = all of K for one column block"
    tm = _pick(m_n, (1024, 512, 256, 128, 64))
    tn = _pick(n_n, (1024, 512, 256, 128))
    return pl.pallas_call(
        _mm_wres_kernel,
        grid=(g_n, n_n // tn, m_n // tm),
        in_specs=[pl.BlockSpec((None, tm, k_n), lambda g, j, i: (g + a_off, i, 0)),
                  pl.BlockSpec((None, k_n, tn), lambda g, j, i: (g, 0, j))],
        out_specs=pl.BlockSpec((None, tm, tn), lambda g, j, i: (g, i, j)),
        out_shape=jax.ShapeDtypeStruct((g_n, m_n, n_n), out_dtype),
        scratch_shapes=[pltpu.VMEM((k_n, tn), BF16)],
        compiler_params=_cparams(("parallel", "parallel", "arbitrary"), 48),
        name="matmul_wres",
    )(a, w)


def _mm2(a, w, **kw):
    return _matmul(a[None], w[None], **kw)[0]


def _ada_kernel(c_ref, w_ref, b_ref, o_ref):
    c = c_ref[...]
    s = (c * _sigmoid(c)).astype(BF16)
    o_ref[...] = jnp.dot(s, w_ref[...].astype(BF16), preferred_element_type=F32) + b_ref[...]


def _ada(c8, ada_w, ada_b):
    depth, d, n = ada_w.shape
    tn = _pick(n, (1024, 512, 256, 128))
    return pl.pallas_call(
        _ada_kernel,
        grid=(depth, n // tn),
        in_specs=[pl.BlockSpec((8, d), lambda l, j: (0, 0)),
                  pl.BlockSpec((None, d, tn), lambda l, j: (l, 0, j)),
                  pl.BlockSpec((None, 1, tn), lambda l, j: (l, 0, j))],
        out_specs=pl.BlockSpec((None, 8, tn), lambda l, j: (l, 0, j)),
        out_shape=jax.ShapeDtypeStruct((depth, 8, n), F32),
        compiler_params=_cparams(("parallel", "parallel"), 40),
        name="ada",
    )(c8, ada_w, ada_b.reshape(depth, 1, n))


def _ln_mod_kernel(z_ref, y_ref, mod_ref, g_ref, b_ref, mod2_ref, *out_refs, gate_j, mod_j, alpha):
    m = mod_ref[...]
    zz = alpha * z_ref[...] + y_ref[...] * m[gate_j:gate_j + 1]
    mu = jnp.mean(zz, axis=-1, keepdims=True)
    zc = zz - mu
    var = jnp.mean(zc * zc, axis=-1, keepdims=True)
    zn = zc * lax.rsqrt(var + LN_EPS) * g_ref[...] + b_ref[...]
    out_refs[0][...] = zn
    if mod_j is not None:
        m2 = mod2_ref[...]
        out_refs[1][...] = (zn * (1 + m2[mod_j + 1:mod_j + 2]) + m2[mod_j:mod_j + 1]).astype(BF16)


def _ln_mod(z, y, mod, ln_g, ln_b, mod2, *, gate_j, mod_j, tr, n_ctx, alpha):
    b_n, l_z, d = z.shape
    l_y = y.shape[1]
    z_off = (l_z - l_y) // tr
    ncb = (n_ctx - (l_z - l_y)) // tr
    seg = lambda b, t: (b, jnp.where(t < ncb, 0, 1), 0, 0)
    row = pl.BlockSpec((None, tr, d), lambda b, t: (b, t, 0))
    out_shape = [jax.ShapeDtypeStruct((b_n, l_y, d), F32)]
    out_specs = [row]
    if mod_j is not None:
        out_shape.append(jax.ShapeDtypeStruct((b_n, l_y, d), BF16))
        out_specs.append(row)
    res = pl.pallas_call(
        functools.partial(_ln_mod_kernel, gate_j=gate_j, mod_j=mod_j, alpha=alpha),
        grid=(b_n, l_y // tr),
        in_specs=[pl.BlockSpec((None, tr, d), lambda b, t: (b, t + z_off, 0)),
                  row,
                  pl.BlockSpec((None, None, 6, d), seg),
                  pl.BlockSpec((1, d), lambda b, t: (0, 0)),
                  pl.BlockSpec((1, d), lambda b, t: (0, 0)),
                  pl.BlockSpec((None, None, 6, d), seg)],
        out_specs=out_specs,
        out_shape=out_shape,
        compiler_params=_cparams(("parallel", "parallel"), 40),
        name="ln_mod",
    )(z, y, mod, ln_g.reshape(1, d), ln_b.reshape(1, d), mod2)
    return res if mod_j is not None else (res[0], None)


def _cast_kernel(x_ref, o_ref):
    o_ref[...] = x_ref[...].astype(o_ref.dtype)


def _to_bf16(w):
    g_n, k_n, n_n = w.shape
    tk = _pick(k_n, (512, 256, 128))
    spec = pl.BlockSpec((None, tk, n_n), lambda g, i: (g, i, 0))
    return pl.pallas_call(
        _cast_kernel, grid=(g_n, k_n // tk), in_specs=[spec], out_specs=spec,
        out_shape=jax.ShapeDtypeStruct(w.shape, BF16),
        compiler_params=_cparams(("parallel", "parallel"), 32),
        name="to_bf16",
    )(w)


def _proj_ln_kernel(a_ref, w_ref, z_ref, mod_ref, g_ref, b_ref, mod2_ref, zo_ref, *ho_refs,
                    nk, gate_j, mod_j, alpha, tr, ncb):
    t, k = pl.program_id(1), pl.program_id(2)
    tm = zo_ref.shape[0]
    rb = tm // 2 if tm % 16 == 0 else tm

    def accumulate(first):
        for r0 in range(0, tm, rb):
            p = jnp.dot(a_ref[r0:r0 + rb, :], w_ref[...], preferred_element_type=F32)
            if first:
                zo_ref[r0:r0 + rb, :] = p
            else:
                zo_ref[r0:r0 + rb, :] += p

    rs = math.gcd(tr, 64)

    def finish():
        for sb in range(tm // rs):
            rows = slice(sb * rs, (sb + 1) * rs)
            is_ctx = t * (tm // tr) + (sb * rs) // tr < ncb
            m = jnp.where(is_ctx, mod_ref[0], mod_ref[1])
            zz = alpha * z_ref[rows, :] + zo_ref[rows, :] * m[gate_j:gate_j + 1]
            mu = jnp.mean(zz, axis=-1, keepdims=True)
            zc = zz - mu
            var = jnp.mean(zc * zc, axis=-1, keepdims=True)
            zn = zc * lax.rsqrt(var + LN_EPS) * g_ref[...] + b_ref[...]
            zo_ref[rows, :] = zn
            if mod_j is not None:
                m2 = jnp.where(is_ctx, mod2_ref[0], mod2_ref[1])
                ho_refs[0][rows, :] = (zn * (1 + m2[mod_j + 1:mod_j + 2]) + m2[mod_j:mod_j + 1]).astype(BF16)

    if nk == 1:
        accumulate(True)
        finish()
    else:
        @pl.when(k == 0)
        def _():
            accumulate(True)

        @pl.when((k > 0) & (k < nk - 1))
        def _():
            accumulate(False)

        @pl.when(k == nk - 1)
        def _():
            accumulate(False)
            finish()


def _proj_ln(a, w, g, z, mod, ln_g, ln_b, mod2, *, gate_j, mod_j, tr, n_ctx, alpha):
    b_n, l_a, k_n = a.shape
    d = w.shape[2]
    if z.shape[1] != l_a:
        y = _matmul(a.reshape(1, b_n * l_a, k_n), w[g:g + 1])[0].reshape(b_n, l_a, d)
        return _ln_mod(z, y, mod, ln_g, ln_b, mod2, gate_j=gate_j, mod_j=mod_j, tr=tr, n_ctx=n_ctx, alpha=alpha)
    if w.dtype != BF16:
        w = _to_bf16(w)
    tm = tr * _pick(l_a // tr, (3, 2, 1))
    tk = k_n if k_n <= 2048 else _pick(k_n, (1408, 1024, 512, 256, 128))
    nk = k_n // tk
    row = pl.BlockSpec((None, tm, d), lambda b, t, k: (b, t, 0))
    seg = pl.BlockSpec((None, 2, 6, d), lambda b, t, k: (b, 0, 0, 0))
    vec = pl.BlockSpec((1, d), lambda b, t, k: (0, 0))
    out_shape = [jax.ShapeDtypeStruct((b_n, l_a, d), F32)]
    out_specs = [row]
    if mod_j is not None:
        out_shape.append(jax.ShapeDtypeStruct((b_n, l_a, d), BF16))
        out_specs.append(row)
    res = pl.pallas_call(
        functools.partial(_proj_ln_kernel, nk=nk, gate_j=gate_j, mod_j=mod_j, alpha=alpha, tr=tr,
                          ncb=n_ctx // tr),
        grid=(b_n, l_a // tm, nk),
        in_specs=[pl.BlockSpec((None, tm, tk), lambda b, t, k: (b, t, k)),
                  pl.BlockSpec((None, tk, d), lambda b, t, k: (g, k, 0),
                               pipeline_mode=pl.Buffered(1) if nk == 1 else None),
                  row, seg, vec, vec, seg],
        out_specs=out_specs,
        out_shape=out_shape,
        compiler_params=_cparams(("parallel", "parallel", "arbitrary"), 56),
        name="proj_ln",
    )(a, w, z, mod, ln_g.reshape(1, d), ln_b.reshape(1, d), mod2)
    return res if mod_j is not None else (res[0], None)


def _seg_shift(x, row, shift, n_ctx):
    l_n = x.shape[0]
    rolled = pltpu.roll(x, (-shift) % l_n, 0)
    src = row + shift
    same_seg = (src >= 0) & (src < l_n) & ((src < n_ctx) == (row < n_ctx))
    return jnp.where(same_seg, rolled, 0.0)


def _rw_mix_kernel(z_ref, mod_ref, mu_ref, o_ref, *, n_ctx):
    z = z_ref[...]
    row = lax.broadcasted_iota(jnp.int32, z.shape, 0)
    is_ctx = row < n_ctx
    shift = jnp.where(is_ctx, mod_ref[0, 0:1, :], mod_ref[1, 0:1, :])
    scale = jnp.where(is_ctx, mod_ref[0, 1:2, :], mod_ref[1, 1:2, :])
    h = z * (1 + scale) + shift
    dx = 0.5 * (_seg_shift(h, row, -1, n_ctx) + _seg_shift(h, row, 1, n_ctx)) - h
    for n in range(6):
        o_ref[n] = (h + dx * mu_ref[n:n + 1, :]).astype(BF16)


def _rw_mix(z, mod, mu, *, n_ctx):
    b_n, l_n, d = z.shape
    tc = _pick(d, (256, 128))
    return pl.pallas_call(
        functools.partial(_rw_mix_kernel, n_ctx=n_ctx),
        grid=(b_n, d // tc),
        in_specs=[pl.BlockSpec((None, l_n, tc), lambda b, j: (b, 0, j)),
                  pl.BlockSpec((None, 2, 6, tc), lambda b, j: (b, 0, 0, j)),
                  pl.BlockSpec((6, tc), lambda b, j: (0, j))],
        out_specs=pl.BlockSpec((6, None, l_n, tc), lambda b, j: (0, b, 0, j)),
        out_shape=jax.ShapeDtypeStruct((6, b_n, l_n, d), BF16),
        compiler_params=_cparams(("parallel", "parallel"), 48),
        name="rw_mix",
    )(z, mod, mu)


def _lora_kernel(x_ref, a_ref, b_ref, o_ref, *, act):
    t = jnp.dot(x_ref[...], a_ref[...], preferred_element_type=F32)
    if act == "tanh":
        t = jnp.tanh(t)
    elif act == "sigmoid":
        t = _sigmoid(t)
    o_ref[...] = jnp.dot(t.astype(BF16), b_ref[...], preferred_element_type=F32)


def _lora(xs, x_idx, a, b, act):
    g_n, d, r = a.shape
    m_n = xs.shape[1]
    tm = _pick(m_n, (512, 256, 128, 64))
    return pl.pallas_call(
        functools.partial(_lora_kernel, act=act),
        grid=(g_n, m_n // tm),
        in_specs=[pl.BlockSpec((None, tm, d), lambda g, i: (x_idx, i, 0)),
                  pl.BlockSpec((None, d, r), lambda g, i: (g, 0, 0)),
                  pl.BlockSpec((None, r, d), lambda g, i: (g, 0, 0))],
        out_specs=pl.BlockSpec((None, tm, d), lambda g, i: (g, i, 0)),
        out_shape=jax.ShapeDtypeStruct((g_n, m_n, d), F32),
        compiler_params=_cparams(("parallel", "parallel"), 40),
        name="lora",
    )(xs, a, b)


def _chunk_of(q, ncc, nc, rev):
    if not rev:
        return q
    return jnp.where(q < ncc, ncc - 1 - q, nc - 1 - (q - ncc))


def _rwkv_kernel(r_ref, k_ref, v_ref, lw_ref, la_ref, g_ref, w0_ref, a0_ref, kk_ref, ka_ref, rk_ref,
                 gng_ref, gnb_ref, o_ref,
                 y_scr, mr_s, n_s, *, n_ctx):
    t_n = CHUNK
    h2 = 2 * t_n
    l_n = r_ref.shape[0]
    nc, ncc = l_n // t_n, n_ctx // t_n
    group = _pick(nc, (6, 4, 3, 2, 1))
    lane = lax.broadcasted_iota(jnp.int32, (1, LANES), 1)
    m1 = jnp.where(lane < RW_HEAD, 1.0, 0.0)
    m2 = 1.0 - m1
    ri = lax.broadcasted_iota(jnp.int32, (LANES, LANES), 0)
    ci = lax.broadcasted_iota(jnp.int32, (LANES, LANES), 1)
    same_head = (ri // RW_HEAD) == (ci // RW_HEAD)
    gsum_b = jnp.where(same_head, 1.0, 0.0).astype(BF16)
    gavg_b = jnp.where(same_head, 1.0 / RW_HEAD, 0.0).astype(BF16)
    eye = jnp.where(ri == ci, 1.0, 0.0)
    tr_i, tc_i = ri % t_n, ci % t_n
    blk_base = (ri // INV_BASE) == (ci // INV_BASE)
    sizes = [INV_BASE * 2 ** i for i in range(1, 8) if INV_BASE * 2 ** i <= t_n]
    blk_sibling = [((ri // b) == (ci // b)) & ((ri // (b // 2)) != (ci // (b // 2))) for b in sizes]
    k_k, k_a = kk_ref[...], ka_ref[...]

    def stack(x):
        return jnp.concatenate([x * m1, x * m2], axis=0)

    def rows_of(c):
        return pl.ds(pl.multiple_of(c * t_n, t_n), t_n)

    def head_sum(x, w_b):
        return _mdot(_parts(x, 2), [w_b], order=2)

    tri3_b = [_cumsum_matrix(t_n, rev) for rev in (False, True)]
    strict = [(tc_i > tr_i) if rev else (tc_i < tr_i) for rev in (False, True)]
    incl = [(tc_i >= tr_i) if rev else (tc_i <= tr_i) for rev in (False, True)]

    def stage_prep(d, c):
        rows = rows_of(c)
        k, r, v = k_ref[rows, :], r_ref[rows, :], v_ref[rows, :]
        kkr = k * k_k
        lw = -RW_DECAY_SCALE * _sigmoid(w0_ref[d:d + 1, :] + lw_ref[d, rows, :])
        a = _sigmoid(a0_ref[d:d + 1, :] + la_ref[d, rows, :])
        return dict(d=d, c=c, k=k, r=r, v=v, lw=lw, a=a, kkr=kkr, ss=head_sum(kkr * kkr, gsum_b),
                    cum=_cumsum(tri3_b[d], lw))

    def stage_amat(s):
        d, cum, lw, a = s["d"], s["cum"], s["lw"], s["a"]
        kk = s["kkr"] * lax.rsqrt(s["ss"] + 1e-12)
        kd = s["k"] * (1 + (a - 1) * k_a)
        bv = kk * a
        p_end = cum[0:1, :] if d == 1 else cum[t_n - 1:t_n, :]
        e_m = jnp.exp(-cum)
        e_h = jnp.exp(p_end - cum)
        ktp = _parts(stack(kk * jnp.exp(cum - lw)), 2)
        rt = stack(s["r"] * jnp.exp(cum))
        k2p = _cat_parts([_parts(stack(bv * e_m), 1), _parts(stack(kd * e_m), 1)], 0)
        return dict(d=d, c=s["c"], ktp=ktp, rt=rt, vp=_parts(stack(s["v"]), 1), p_end=p_end,
                    bh=stack(bv * e_h), kh=stack(kd * e_h),
                    amat=_mdot(_cat_parts([ktp, _parts(rt, 2)], 0), k2p, NT))

    def stage_square(s):
        d, amat = s["d"], s["amat"]
        lt = jnp.where(strict[d], amat[:h2, :h2], 0.0).T
        ldt = jnp.where(blk_base, lt, 0.0)
        ltp = _parts(ldt, 2)
        msk = jnp.concatenate([jnp.where(strict[d], amat[:h2, h2:], 0.0),
                               jnp.where(incl[d], amat[h2:, h2:], 0.0)], axis=0)
        s = dict(s, pt=eye - ldt, xt=_mdot(ltp, ltp[:1]),
                 ct=[jnp.where(m, lt, 0.0).astype(BF16) for m in blk_sibling],
                 av=_mdot(_parts(msk, 2), s["vp"][:1]),
                 arbp=_parts(jnp.where(incl[d], amat[h2:, :h2], 0.0), 2))
        del s["amat"]
        return s

    def stage_merge_a(s, level):
        return dict(s, t1=_mdot([s["ct"][level]], _parts(s["pt"], 2)).astype(BF16))

    def stage_merge_b(s):
        return dict(s, pt=s["pt"] - _mdot(_parts(s["pt"], 2), [s["t1"]]))

    def stage_double(s, final):
        xh = _parts(s["xt"], 1)
        ptp = _parts(s["pt"], 2)
        if final:
            return dict(s, pt=s["pt"] + _mdot(xh, ptp))
        rhs = [jnp.concatenate([ptp[0], xh[0]], axis=1), jnp.concatenate([ptp[1], jnp.zeros_like(xh[0])], axis=1)]
        both = _mdot(xh, rhs)
        return dict(s, pt=s["pt"] + both[:, :LANES], xt=both[:, LANES:])

    def stage_solve(s):
        rhs = jnp.concatenate([s["ktp"][0], (-s["av"][:h2]).astype(BF16)], axis=1)
        return dict(s, wub=_mdot(_parts(s["pt"].T, 2), [rhs]).astype(BF16))

    def stage_fold(s):
        d, c, wub = s["d"], s["c"], s["wub"]
        aw = _mdot(s["arbp"], [wub])
        zb = jnp.zeros((h2, LANES), BF16)
        lhs = _cat_parts([_parts(s["bh"].T, 2), _parts(s["kh"].T, 2)], 1)
        rhs = jnp.concatenate([wub, jnp.concatenate([zb, s["vp"][0]], axis=1)], axis=0)
        mn = _mdot(lhs, [rhs])
        dg = jnp.where(ri == ci, jnp.broadcast_to(jnp.exp(s["p_end"]), (LANES, LANES)), 0.0)
        rp = s["rt"] - aw[:, :LANES]
        mrp = _parts(jnp.concatenate([dg - mn[:, :LANES], rp[:t_n] + rp[t_n:]], axis=0), 2)
        for i in range(2):
            mr_s[d, c, i] = mrp[i]
        n_s[d, c] = mn[:, LANES:]
        y0 = s["av"][h2:] + aw[:, LANES:]
        return y0[:t_n] + y0[t_n:]

    def seq(q, hs):
        cs = (q, _chunk_of(q, ncc, nc, True))
        mh = [_mdot([mr_s[d, cs[d], 0], mr_s[d, cs[d], 1]], _parts(hs[d], 2)) for d in (0, 1)]
        for d in (0, 1):
            y_scr[rows_of(cs[d]), :] += mh[d][h2:]
        return tuple(mh[d][:h2] + n_s[d, cs[d]] for d in (0, 1))

    def local(i, hs, with_seq):
        steps = [i * group + g for g in range(group)]
        pending = [q - group for q in steps] if with_seq else []
        sts = [stage_prep(d, q if d == 0 else _chunk_of(q, ncc, nc, True)) for q in steps for d in (0, 1)]
        n_dbl = INV_BASE.bit_length() - 2
        stages = [stage_amat, stage_square]
        stages += [functools.partial(stage_double, final=i == n_dbl - 1) for i in range(n_dbl)]
        for level in range(len(blk_sibling)):
            stages += [functools.partial(stage_merge_a, level=level), stage_merge_b]
        stages += [stage_solve]
        for stage in stages:
            sts = [stage(s) for s in sts]
            if pending:
                hs = seq(pending.pop(0), hs)
        for s in sts:
            y0 = stage_fold(s)
            y_scr[rows_of(s["c"]), :] += y0
        while pending:
            hs = seq(pending.pop(0), hs)
        return hs

    y_scr[...] = jnp.zeros(y_scr.shape, F32)
    zero = jnp.zeros((LANES, LANES), F32)
    n_trip = nc // group
    hs = local(0, (zero, zero), False)
    hs = lax.fori_loop(1, n_trip, functools.partial(local, with_seq=True), hs)
    for q in range((n_trip - 1) * group, nc):
        hs = seq(q, hs)

    n_post = _pick(nc, (9, 6, 4, 3, 2, 1))

    def post(i, carry):
        rows = [rows_of(i * n_post + g) for g in range(n_post)]

        def bonus_sum(rw):
            k, r = k_ref[rw, :], r_ref[rw, :]
            kd_f = k * (1 + (_sigmoid(a0_ref[0:1, :] + la_ref[0, rw, :]) - 1) * k_a)
            kd_b = k * (1 + (_sigmoid(a0_ref[1:2, :] + la_ref[1, rw, :]) - 1) * k_a)
            return head_sum(r * (kd_f + kd_b) * rk_ref[...], gsum_b)
        bsum = [bonus_sum(rw) for rw in rows]
        ys = [y_scr[rw, :] for rw in rows]
        ycs = [y - m for y, m in zip(ys, [head_sum(y, gavg_b) for y in ys])]
        var = [head_sum(yc * yc, gavg_b) for yc in ycs]
        for rw, yc, vr, bs in zip(rows, ycs, var, bsum):
            yn = yc * lax.rsqrt(vr + RW_GN_EPS) * gng_ref[...] + gnb_ref[...]
            o_ref[rw, :] = ((yn + bs * v_ref[rw, :]) * g_ref[rw, :]).astype(BF16)
        return carry

    lax.fori_loop(0, nc // n_post, post, 0)


def _rwkv_scan(rkv, lw, la, g, w0, a0, k_k, k_a, r_k, gn_g, gn_b, *, n_ctx):
    _, b_n, l_n, d = rkv.shape
    nc = l_n // CHUNK
    col = lambda n: pl.BlockSpec((None, None, l_n, LANES), lambda b, p, n=n: (n, b, 0, p))
    two = pl.BlockSpec((2, None, l_n, LANES), lambda b, p: (0, b, 0, p))
    par = lambda rows: pl.BlockSpec((rows, LANES), lambda b, p: (0, p))
    return pl.pallas_call(
        functools.partial(_rwkv_kernel, n_ctx=n_ctx),
        grid=(b_n, d // LANES),
        in_specs=[col(0), col(1), col(2), two, two,
                  pl.BlockSpec((None, l_n, LANES), lambda b, p: (b, 0, p)),
                  par(2), par(2), par(1), par(1), par(1), par(1), par(1)],
        out_specs=pl.BlockSpec((None, l_n, LANES), lambda b, p: (b, 0, p)),
        out_shape=jax.ShapeDtypeStruct((b_n, l_n, d), BF16),
        scratch_shapes=[pltpu.VMEM((l_n, LANES), F32),
                        pltpu.VMEM((2, nc, 2, LANES + CHUNK, LANES), BF16),
                        pltpu.VMEM((2, nc, LANES, LANES), F32)],
        compiler_params=_cparams(("parallel", "parallel"), 56),
        name="rwkv_scan",
    )(rkv, rkv, rkv, lw, la, g, w0, a0, k_k.reshape(1, d), k_a.reshape(1, d), r_k.reshape(1, d),
      gn_g.reshape(1, d), gn_b.reshape(1, d))


def _pad_axis(w, axis, to):
    pad = [(0, 0)] * w.ndim
    pad[axis] = (0, to - w.shape[axis])
    return jnp.pad(w, pad)


def _rwkv7_layer(z, mod, n_ctx, mu, w_rkv, w0, w1, w2, a0, a1, a2, g1, g2, k_k, k_a, r_k, gn_g, gn_b):
    b_n, l_n, d = z.shape
    m_n = b_n * l_n
    xs = _rw_mix(z, mod, mu, n_ctx=n_ctx).reshape(6, m_n, d)
    rkv = _matmul(xs, w_rkv)
    r_w = -(-w1.shape[-1] // LANES) * LANES
    r_a = -(-a1.shape[-1] // LANES) * LANES
    lw = _lora(xs, 3, _pad_axis(w1, 2, r_w).astype(BF16), _pad_axis(w2, 1, r_w).astype(BF16), "tanh")
    la = _lora(xs, 4, _pad_axis(a1, 2, r_a).astype(BF16), _pad_axis(a2, 1, r_a).astype(BF16), None)
    gate = _lora(xs, 5, g1[None].astype(BF16), g2[None].astype(BF16), "sigmoid")
    o = _rwkv_scan(rkv.reshape(3, b_n, l_n, d), lw.reshape(2, b_n, l_n, d), la.reshape(2, b_n, l_n, d),
                   gate.reshape(b_n, l_n, d), w0, a0, k_k, k_a, r_k, gn_g, gn_b, n_ctx=n_ctx)
    return o


def _qkv_rope_kernel(a_ref, w_ref, cos_ref, sa_ref, sb_ref, o_ref, wb_ref, *, n_q, n_qk):
    j, i = pl.program_id(0), pl.program_id(1)

    @pl.when(i == 0)
    def _():
        wb_ref[...] = w_ref[...].astype(BF16)

    tn = o_ref.shape[1]
    sw = math.gcd(tn, 2 * DA_HEAD)

    def product(s):
        return jnp.dot(a_ref[...], wb_ref[:, s * sw:(s + 1) * sw], preferred_element_type=F32)

    @pl.when(j < n_qk)
    def _():
        q = DA_HEAD // 4
        q_scale = jnp.where(j < n_q, DA_HEAD ** -0.5 * math.log2(math.e), 1.0)
        cos, s_a, s_b = cos_ref[...] * q_scale, sa_ref[...] * q_scale, sb_ref[...] * q_scale

        def rotary(s, x):
            for c in range(0, sw, DA_HEAD):
                xs = x[:, c:c + DA_HEAD]
                rot = xs * cos + pltpu.roll(xs, DA_HEAD - q, 1) * s_a + pltpu.roll(xs, q, 1) * s_b
                o_ref[:, s * sw + c:s * sw + c + DA_HEAD] = rot.astype(BF16)
        x_prev = product(0)
        for s in range(1, tn // sw):
            x_next = product(s)
            rotary(s - 1, x_prev)
            x_prev = x_next
        rotary(tn // sw - 1, x_prev)

    @pl.when(j >= n_qk)
    def _():
        for s in range(tn // sw):
            o_ref[:, s * sw:(s + 1) * sw] = product(s).astype(BF16)


def _qkv_rope(h, w_qkv, cos, s_a, s_b):
    b_n, l_n, d = h.shape
    d3 = w_qkv.shape[1]
    tm = _pick(l_n, (768, 1024, 512, 256, 128, 64))
    tn = _pick(d, (1024, 512, 256, 128))
    n_t = l_n // tm
    tab = pl.BlockSpec((tm, DA_HEAD), lambda j, i: (i % n_t, 0))
    out = pl.pallas_call(
        functools.partial(_qkv_rope_kernel, n_q=d // tn, n_qk=2 * d // tn),
        grid=(d3 // tn, b_n * n_t),
        in_specs=[pl.BlockSpec((tm, d), lambda j, i: (i, 0)),
                  pl.BlockSpec((d, tn), lambda j, i: (0, j)), tab, tab, tab],
        out_specs=pl.BlockSpec((tm, tn), lambda j, i: (i, j)),
        out_shape=jax.ShapeDtypeStruct((b_n * l_n, d3), BF16),
        scratch_shapes=[pltpu.VMEM((d, tn), BF16)],
        compiler_params=_cparams(("parallel", "arbitrary"), 48),
        name="qkv_rope",
    )(h.reshape(b_n * l_n, d), w_qkv, cos, s_a, s_b)
    return out.reshape(b_n, l_n, d3)


def _attn_kernel(q_ref, k_ref, v_ref, lam_ref, sg_ref, o_ref, *, ncb, n_ctx, lam_init):
    qi = pl.program_id(2)
    lv = lam_ref[...]
    lam = (jnp.exp(jnp.sum(lv[0:1] * lv[1:2], axis=-1, keepdims=True))
           - jnp.exp(jnp.sum(lv[2:3] * lv[3:4], axis=-1, keepdims=True)) + lam_init)

    hw = 2 * DA_HEAD
    n_hh = o_ref.shape[1] // hw

    def attend(nk):
        def scores(hh, m):
            cols = slice(hh * hw + m * DA_HEAD, hh * hw + (m + 1) * DA_HEAD)
            return lax.dot_general(q_ref[:, cols], k_ref[0:nk, cols], NT, preferred_element_type=F32)
        s_all = [[scores(hh, m) for m in (0, 1)] for hh in range(n_hh)]
        for hh in range(n_hh):
            def probs(s):
                e = jnp.exp2(s - jnp.max(s, axis=-1, keepdims=True))
                return e, 1.0 / jnp.sum(e, axis=-1, keepdims=True)
            e0, i0 = probs(s_all[hh][0])
            e1, i1 = probs(s_all[hh][1])
            v = v_ref[0:nk, hh * hw:(hh + 1) * hw]
            o = (jnp.dot(e0.astype(BF16), v, preferred_element_type=F32) * i0
                 - jnp.dot(e1.astype(BF16), v, preferred_element_type=F32) * (lam * i1))
            o = o * lax.rsqrt(jnp.mean(o * o, axis=-1, keepdims=True) + 1e-5) * sg_ref[...] * (1 - lam_init)
            o_ref[:, hh * hw:(hh + 1) * hw] = o.astype(BF16)

    if ncb > 0:
        @pl.when(qi < ncb)
        def _():
            attend(n_ctx)

    @pl.when(qi >= ncb)
    def _():
        attend(k_ref.shape[0])


def _attention(qkv, lam_vec, sub_g, *, tq, n_ctx, lam_init):
    b_n, l_n, d3 = qkv.shape
    d = d3 // 3
    hw = 2 * DA_HEAD
    n_hh = 2 if (d // hw) % 2 == 0 else 1
    bw = n_hh * hw
    nh = d // bw
    return pl.pallas_call(
        functools.partial(_attn_kernel, ncb=n_ctx // tq, n_ctx=n_ctx, lam_init=lam_init),
        grid=(b_n, nh, l_n // tq),
        in_specs=[pl.BlockSpec((None, tq, bw), lambda b, h, t: (b, t, h)),
                  pl.BlockSpec((None, l_n, bw), lambda b, h, t: (b, 0, nh + h)),
                  pl.BlockSpec((None, l_n, bw), lambda b, h, t: (b, 0, 2 * nh + h)),
                  pl.BlockSpec((4, DA_HEAD), lambda b, h, t: (0, 0)),
                  pl.BlockSpec((1, hw), lambda b, h, t: (0, 0))],
        out_specs=pl.BlockSpec((None, tq, bw), lambda b, h, t: (b, t, h)),
        out_shape=jax.ShapeDtypeStruct((b_n, l_n, d), BF16),
        compiler_params=_cparams(("parallel", "parallel", "arbitrary"), 48),
        name="diff_attn",
    )(qkv, qkv, qkv, lam_vec, sub_g.reshape(1, hw))


def _rope_tables(n_ctx, n_lat):
    n_rows = n_lat // GRID_W
    row = jnp.repeat(jnp.arange(n_rows, dtype=F32), GRID_W)
    col = jnp.tile(jnp.arange(GRID_W, dtype=F32), n_rows)
    nf = DA_HEAD // 4
    inv_freq = ROPE_BASE ** (-jnp.arange(nf, dtype=F32) / nf)
    ang_r, ang_c = row[:, None] * inv_freq, col[:, None] * inv_freq
    ang = jnp.concatenate([ang_r, ang_r, ang_c, ang_c], axis=-1)
    ang = jnp.concatenate([jnp.zeros((n_ctx, DA_HEAD), F32), ang], axis=0)
    cos, sin = jnp.cos(ang), jnp.sin(ang)
    even_q = (jnp.arange(DA_HEAD) // nf) % 2 == 0
    return cos, jnp.where(even_q, -sin, 0.0), jnp.where(even_q, 0.0, sin)


def _diff_attention_layer(h, n_ctx, layer_idx, w_qkv, lam_vec, sub_g, *, tr):
    b_n, l_n, d = h.shape
    cos, s_a, s_b = _rope_tables(n_ctx, l_n - n_ctx)
    qkv = _qkv_rope(h, w_qkv, cos, s_a, s_b)
    lam_init = 0.8 - 0.6 * math.exp(-0.3 * layer_idx)
    o = _attention(qkv, lam_vec, sub_g, tq=tr, n_ctx=n_ctx, lam_init=lam_init)
    return o


def _hgrn_kernel(q_ref, i_ref, g_ref, ff_ref, fb_ref, low_ref, ng_ref, o_ref, o_scr, *, n_ctx, layer_idx):
    t_n = CHUNK
    l_n = q_ref.shape[0]
    nc, ncc = l_n // t_n, n_ctx // t_n
    r64 = lax.broadcasted_iota(jnp.int32, (t_n, t_n), 0)
    c64 = lax.broadcasted_iota(jnp.int32, (t_n, t_n), 1)

    def rows_of(c):
        return pl.ds(pl.multiple_of(c * t_n, t_n), t_n)

    f_refs = (ff_ref, fb_ref)
    lbs, incl = [], []
    tri3_b = [_cumsum_matrix(t_n, rev) for rev in (False, True)]
    for d in (0, 1):
        low = low_ref[d]
        e = jnp.exp(low - jnp.max(low, axis=0, keepdims=True))
        sm = e / jnp.sum(e, axis=0, keepdims=True)
        cs = sm[0:1]
        for rr in range(1, layer_idx + 1):
            cs = cs + sm[rr:rr + 1]
        lbs.append(cs - sm[0:1])
        incl.append((c64 >= r64) if d == 1 else (c64 <= r64))
    group = _pick(nc, (6, 4, 2, 1))

    def stage_cum(d, c):
        rows = rows_of(c)
        f = lbs[d] + (1.0 - lbs[d]) * _sigmoid(f_refs[d][rows, :])
        return dict(d=d, rows=rows, f=f, cum=_cumsum(tri3_b[d], jnp.log(f)))

    def stage_att(s):
        d, cum, rows = s["d"], s["cum"], s["rows"]
        b_end = cum[0:1, :] if d == 1 else cum[t_n - 1:t_n, :]
        qv = q_ref[rows, :]
        qd = (qv * _sigmoid(qv) * jnp.exp(cum)).astype(BF16)
        kk = 1.0 - s["f"]
        v = i_ref[rows, :]
        kd = (kk * jnp.exp(-cum)).astype(BF16)
        ke = (kk * jnp.exp(b_end - cum)).astype(BF16)
        return dict(d=d, rows=rows, qd=qd, vb=v.astype(BF16), dec=jnp.exp(b_end),
                    att=lax.dot_general(qd, kd, NT, preferred_element_type=F32),
                    upd=jnp.dot(v.T.astype(BF16), ke, preferred_element_type=F32))

    def stage_intra(s):
        att = jnp.where(incl[s["d"]], s["att"], 0.0).astype(BF16)
        return dict(s, o=jnp.dot(att, s["vb"], preferred_element_type=F32))

    def body(i, states):
        items = [(d, _chunk_of(i * group + g, ncc, nc, d == 1)) for g in range(group) for d in (0, 1)]
        sts = [stage_cum(d, c) for d, c in items]
        sts = [stage_att(s) for s in sts]
        sts = [stage_intra(s) for s in sts]
        states = list(states)
        for s in sts:
            d = s["d"]
            o = s["o"] + lax.dot_general(s["qd"], states[d].astype(BF16), NT, preferred_element_type=F32)
            o_scr[d, s["rows"], :] = o
            states[d] = states[d] * s["dec"] + s["upd"]
        return tuple(states)

    zero = jnp.zeros((LANES, LANES), F32)
    lax.fori_loop(0, nc // group, body, (zero, zero))

    p_n = t_n * _pick(nc, (4, 3, 2, 1))

    def post(c, carry):
        rows = pl.ds(pl.multiple_of(c * p_n, p_n), p_n)
        o = o_scr[0, rows, :] + o_scr[1, rows, :]
        o = o * lax.rsqrt(jnp.mean(o * o, axis=-1, keepdims=True) + 1e-5) * ng_ref[...]
        gv = g_ref[rows, :]
        o_ref[rows, :] = (o * (gv * _sigmoid(gv))).astype(BF16)
        return carry

    lax.fori_loop(0, l_n // p_n, post, 0)


def _hgrn2_layer(h, n_ctx, layer_idx, w_in, lower, norm_g):
    b_n, l_n, d = h.shape
    m_n = b_n * l_n
    nh = d // HG_EXPAND
    proj = _mm2(h.reshape(m_n, d), w_in).reshape(b_n, l_n, 5 * d)
    col = lambda n: pl.BlockSpec((None, l_n, LANES), lambda b, p, n=n: (b, 0, n * nh + p))
    o = pl.pallas_call(
        functools.partial(_hgrn_kernel, n_ctx=n_ctx, layer_idx=layer_idx),
        grid=(b_n, nh),
        in_specs=[col(0), col(1), col(2), col(3), col(4),
                  pl.BlockSpec((2, lower.shape[1], LANES), lambda b, p: (0, 0, p)),
                  pl.BlockSpec((1, LANES), lambda b, p: (0, 0))],
        out_specs=pl.BlockSpec((None, l_n, LANES), lambda b, p: (b, 0, p)),
        out_shape=jax.ShapeDtypeStruct((b_n, l_n, d), BF16),
        scratch_shapes=[pltpu.VMEM((2, l_n, LANES), F32)],
        compiler_params=_cparams(("parallel", "parallel"), 40),
        name="hgrn_scan",
    )(proj, proj, proj, proj, proj, lower, norm_g.reshape(1, LANES))
    return o


def _gelu_tanh(x):
    return 0.5 * x * (1.0 + jnp.tanh(math.sqrt(2.0 / math.pi) * (x + 0.044715 * (x * x * x))))


def _softplus(x):
    return jnp.maximum(x, 0.0) + jnp.log1p(jnp.exp(-jnp.abs(x)))


SEG_PAD = 8


def _lin_scan(a_ref, u_ref, hl_s, cp_s, h_s, base, row0, n, h_in):
    seg = n // 8
    stride = seg + SEG_PAD
    n_p = a_ref.shape[1]
    chains = [(d, j) for d in (0, 1) for j in range(n_p)]

    def step(i, carry):
        out = []
        for (d, j), (hl, cp) in zip(chains, carry):
            idx = pl.ds(base + (i if d == 0 else seg - 1 - i), 8, stride=stride)
            a = a_ref[d, j, idx, :]
            hl = a * hl + u_ref[d, j, idx, :]
            cp = a * cp
            hl_s[d, j, idx, :] = hl
            cp_s[d, j, idx, :] = cp
            out.append((hl, cp))
        return tuple(out)

    init = tuple((jnp.zeros((8, LANES), F32), jnp.ones((8, LANES), F32)) for _ in chains)
    ends = lax.fori_loop(0, seg, step, init)
    h_out = [[None] * n_p, [None] * n_p]
    for (d, j), (hl_e, cp_e) in zip(chains, ends):
        carry = h_in[d][j]
        for s in (range(8) if d == 0 else range(7, -1, -1)):
            r0, p0 = row0 + s * seg, base + s * stride
            blk = hl_s[d, j, p0:p0 + seg, :] + cp_s[d, j, p0:p0 + seg, :] * carry
            if d == 0:
                h_s[j, r0:r0 + seg, :] = blk
            else:
                h_s[j, r0:r0 + seg, :] += blk
            carry = hl_e[s:s + 1, :] + cp_e[s:s + 1, :] * carry
        h_out[d][j] = carry
    return h_out


def _rglru_kernel(gb_ref, xb_ref, cw_ref, cb_ref, wg_ref, bg_ref, lam_ref, o_ref, a_s, u_s, h_s, hl_s, cp_s, *,
                  n_ctx):
    l_n = xb_ref.shape[0]
    n_lat = l_n - n_ctx
    x = xb_ref[...]
    row = lax.broadcasted_iota(jnp.int32, x.shape, 0)
    k_w = cw_ref.shape[0]
    xc = cb_ref[...] + sum(_seg_shift(x, row, j - (k_w - 1) // 2, n_ctx) * cw_ref[j:j + 1, :]
                           for j in range(k_w))
    xcb = xc.astype(BF16)
    n_p = x.shape[1] // LANES
    for d in (0, 1):
        gate = lambda g: _sigmoid(jnp.dot(xcb, wg_ref[d, g].astype(BF16), preferred_element_type=F32)
                                  + bg_ref[d, g:g + 1, :])
        log_a = -LR_C * gate(0) * _softplus(-lam_ref[d:d + 1, :])
        a = jnp.exp(log_a)
        u = jnp.sqrt(jnp.tanh(-log_a) * (jnp.exp(2.0 * log_a) + 1.0)) * gate(1) * xc
        base = 0
        for row0, n in ((0, n_ctx), (n_ctx, n_lat)):
            seg = n // 8
            for j in range(n_p if n else 0):
                for s in range(8):
                    src = slice(row0 + s * seg, row0 + (s + 1) * seg)
                    dst = slice(base + s * (seg + SEG_PAD), base + s * (seg + SEG_PAD) + seg)
                    a_s[d, j, dst, :] = a[src, j * LANES:(j + 1) * LANES]
                    u_s[d, j, dst, :] = u[src, j * LANES:(j + 1) * LANES]
            base += 8 * (seg + SEG_PAD) if n else 0
    h = [[jnp.zeros((1, LANES), F32)] * n_p] * 2
    base = 0
    for row0, n in ((0, n_ctx), (n_ctx, n_lat)):
        if n:
            h = _lin_scan(a_s, u_s, hl_s, cp_s, h_s, base, row0, n, h)
            base += 8 * (n // 8 + SEG_PAD)
    for j in range(n_p):
        cols = slice(j * LANES, (j + 1) * LANES)
        o_ref[:, cols] = (h_s[j, n_ctx:l_n, :] * _gelu_tanh(gb_ref[n_ctx:, cols])).astype(BF16)


def _rglru_layer(h, n_ctx, w_in, conv_w, conv_b, w_gate, b_gate, lam):
    b_n, l_n, d = h.shape
    n_lat = l_n - n_ctx
    nb = d // LR_BS
    proj = _mm2(h.reshape(b_n * l_n, d), w_in).reshape(b_n, l_n, 2 * d)
    k_w = conv_w.shape[0]
    plane = (LR_BS // LANES, l_n + 16 * SEG_PAD, LANES)
    o = pl.pallas_call(
        functools.partial(_rglru_kernel, n_ctx=n_ctx),
        grid=(b_n, nb),
        in_specs=[pl.BlockSpec((None, l_n, LR_BS), lambda b, j: (b, 0, j)),
                  pl.BlockSpec((None, l_n, LR_BS), lambda b, j: (b, 0, nb + j)),
                  pl.BlockSpec((k_w, LR_BS), lambda b, j: (0, j)),
                  pl.BlockSpec((1, LR_BS), lambda b, j: (0, j)),
                  pl.BlockSpec((2, 2, None, LR_BS, LR_BS), lambda b, j: (0, 0, j, 0, 0)),
                  pl.BlockSpec((2, 2, LR_BS), lambda b, j: (0, 0, j)),
                  pl.BlockSpec((2, LR_BS), lambda b, j: (0, j))],
        out_specs=pl.BlockSpec((None, n_lat, LR_BS), lambda b, j: (b, 0, j)),
        out_shape=jax.ShapeDtypeStruct((b_n, n_lat, d), BF16),
        scratch_shapes=[pltpu.VMEM((2,) + plane, F32), pltpu.VMEM((2,) + plane, F32), pltpu.VMEM(plane, F32),
                        pltpu.VMEM((2,) + plane, F32), pltpu.VMEM((2,) + plane, F32)],
        compiler_params=_cparams(("parallel", "parallel"), 56),
        name="rglru",
    )(proj, proj, conv_w, conv_b.reshape(1, d), w_gate, b_gate, lam)
    return o


def _ffn_up_kernel(h_ref, wg_ref, wv_ref, cg_ref, cv_ref, bg_ref, bv_ref, o_ref, u_scr, *, n_ctx, tr):
    l_n, tf = o_ref.shape
    k_w = cg_ref.shape[0]
    half = (k_w - 1) // 2
    n_buf, pad = u_scr.shape[0], (u_scr.shape[1] - l_n) // 2
    rblk = tr * _pick(l_n // tr, (3, 4, 2, 1))
    row = lax.broadcasted_iota(jnp.int32, (tr, LANES), 0)
    for p in range(n_buf):
        u_scr[p, 0:pad, :] = jnp.zeros((pad, 2 * LANES), F32)
        u_scr[p, pad + l_n:, :] = jnp.zeros((pad, 2 * LANES), F32)

    def weights(s):
        cols = slice(s * LANES, (s + 1) * LANES)
        return jnp.concatenate([wg_ref[:, cols], wv_ref[:, cols]], axis=1).astype(BF16)

    def product(s, w, r0):
        u_scr[s % n_buf, pad + r0:pad + r0 + rblk, :] = jnp.dot(h_ref[r0:r0 + rblk, :], w,
                                                                preferred_element_type=F32)

    def finish(s, r0):
        p, cols = s % n_buf, slice(s * LANES, (s + 1) * LANES)

        def conv(lane0, w_ref, b_ref):
            acc = None
            for j in range(k_w):
                sh = j - half
                x = u_scr[p, pad + r0 + sh:pad + r0 + sh + tr, lane0:lane0 + LANES]
                if sh < 0 and r0 in (0, n_ctx):
                    x = jnp.where(row < -sh, 0.0, x)
                if sh > 0 and r0 + tr in (n_ctx, l_n):
                    x = jnp.where(row >= tr - sh, 0.0, x)
                t = x * w_ref[j:j + 1, cols]
                acc = t if acc is None else acc + t
            return b_ref[:, cols] + acc
        gate = conv(0, cg_ref, bg_ref)
        val = conv(LANES, cv_ref, bv_ref)
        o_ref[r0:r0 + tr, cols] = (gate * _sigmoid(gate) * val).astype(BF16)

    n_s = tf // LANES
    for s in range(n_s + 1):
        w = weights(s) if s < n_s else None
        for r0 in range(0, l_n, rblk):
            if s < n_s:
                product(s, w, r0)
            if s > 0:
                for r1 in range(r0, r0 + rblk, tr):
                    finish(s - 1, r1)


def _conv_ffn(h, n_ctx, w_up, layer, conv_w, conv_b, *, tr):
    b_n, l_n, d = h.shape
    f = w_up.shape[2] // 2
    tf = _pick(f, (512, 256, 128))
    nf = f // tf
    k_w = conv_w.shape[0]
    cb = conv_b.reshape(1, 2 * f)
    act = pl.pallas_call(
        functools.partial(_ffn_up_kernel, n_ctx=n_ctx, tr=tr),
        grid=(b_n, nf),
        in_specs=[pl.BlockSpec((None, l_n, d), lambda b, j: (b, 0, 0), pipeline_mode=pl.Buffered(1)),
                  pl.BlockSpec((None, d, tf), lambda b, j: (layer, 0, j)),
                  pl.BlockSpec((None, d, tf), lambda b, j: (layer, 0, nf + j)),
                  pl.BlockSpec((k_w, tf), lambda b, j: (0, j)),
                  pl.BlockSpec((k_w, tf), lambda b, j: (0, nf + j)),
                  pl.BlockSpec((1, tf), lambda b, j: (0, j)),
                  pl.BlockSpec((1, tf), lambda b, j: (0, nf + j))],
        out_specs=pl.BlockSpec((None, l_n, tf), lambda b, j: (b, 0, j)),
        out_shape=jax.ShapeDtypeStruct((b_n, l_n, f), BF16),
        scratch_shapes=[pltpu.VMEM((3, l_n + 16, 2 * LANES), F32)],
        compiler_params=_cparams(("parallel", "arbitrary"), 56),
        name="ffn_up",
    )(h, w_up, w_up, conv_w, conv_w, cb, cb)
    return act


def kernel(x, c, ctx, c_ctx, ada_w, ada_b, ln_g, ln_b, ffn_w_up, ffn_conv_w, ffn_conv_b, ffn_w_down, rw_mu, rw_w_rkv, rw_w0, rw_w1, rw_w2, rw_a0, rw_a1, rw_a2, rw_g1, rw_g2, rw_k_k, rw_k_a, rw_r_k, rw_gn_g, rw_gn_b, rw_w_o, da_w_qkv, da_lambda, da_sub_g, da_w_o, hg_w_in, hg_lower, hg_norm_g, hg_w_o, lr_w_in, lr_conv_w, lr_conv_b, lr_w_gate, lr_b_gate, lr_lambda, lr_w_o):
    b_n, n_lat, d = x.shape
    n_ctx = ctx.shape[1]
    depth = ada_w.shape[0]
    assert depth == 4 and rw_mu.shape[0] == 1, "one occurrence of each of the four mixers"
    assert b_n + 1 <= 8 and n_ctx % CHUNK == 0 and n_lat % CHUNK == 0
    tr = math.gcd(math.gcd(n_ctx, n_lat), 256)
    alpha = (2 * depth) ** 0.25

    c8 = jnp.concatenate([c, c_ctx[None], jnp.zeros((8 - b_n - 1, d), F32)], axis=0)
    m = _ada(c8, ada_w, ada_b)
    m_lat = m[:, :b_n].reshape(depth, b_n, 1, 6, d)
    m_ctx = jnp.broadcast_to(m[:, b_n].reshape(depth, 1, 1, 6, d), (depth, b_n, 1, 6, d))
    mod = jnp.concatenate([m_ctx, m_lat], axis=2)

    z = jnp.concatenate([ctx, x], axis=1)
    w_down_b = _to_bf16(ffn_w_down)
    h = None
    for i in range(depth):
        last = i == depth - 1
        if i == 0:
            o = _rwkv7_layer(z, mod[0], n_ctx, rw_mu[0], rw_w_rkv[0], rw_w0[0], rw_w1[0], rw_w2[0], rw_a0[0],
                             rw_a1[0], rw_a2[0], rw_g1[0], rw_g2[0], rw_k_k[0], rw_k_a[0], rw_r_k[0],
                             rw_gn_g[0], rw_gn_b[0])
            w_o = rw_w_o
        elif i == 1:
            o = _diff_attention_layer(h, n_ctx, i, da_w_qkv[0], da_lambda[0], da_sub_g[0], tr=tr)
            w_o = da_w_o
        elif i == 2:
            o = _hgrn2_layer(h, n_ctx, i, hg_w_in[0], hg_lower, hg_norm_g[0])
            w_o = hg_w_o
        else:
            o = _rglru_layer(h, n_ctx, lr_w_in[0], lr_conv_w[0], lr_conv_b[0], lr_w_gate[0], lr_b_gate[0],
                             lr_lambda[0])
            w_o = lr_w_o
        z, h = _proj_ln(o, w_o, 0, z, mod[i], ln_g[i, 0], ln_b[i, 0], mod[i], gate_j=2, mod_j=3, tr=tr,
                        n_ctx=n_ctx, alpha=alpha)
        if last:
            n_ctx = 0
        act = _conv_ffn(h, n_ctx, ffn_w_up, i, ffn_conv_w[i], ffn_conv_b[i], tr=tr)
        z, h = _proj_ln(act, w_down_b, i, z, mod[i], ln_g[i, 1], ln_b[i, 1], mod[min(i + 1, depth - 1)],
                        gate_j=5, mod_j=None if last else 0, tr=tr, n_ctx=n_ctx, alpha=alpha)
    return z
```

```python
import functools
import math

import jax
import jax.numpy as jnp
from jax import lax
from jax.experimental import pallas as pl
from jax.experimental.pallas import tpu as pltpu

F32, BF16 = jnp.float32, jnp.bfloat16

LANES = 128
CHUNK = 64
INV_BASE = 8
LN_EPS = 1e-5
GRID_W = 64
ROPE_BASE = 10000.0
RW_HEAD = 64
RW_DECAY_SCALE = 0.606531
RW_GN_EPS = 64e-5
DA_HEAD = 128
HG_EXPAND = 128
LR_BS = 256
LR_C = 8.0
MIB = 1024 * 1024


def _pick(n, cands):
    for c in cands:
        if n % c == 0:
            return c
    return n


V7X_VMEM_MIB = 64


def _cparams(sem, vmem_mib):
    assert vmem_mib < V7X_VMEM_MIB
    return pltpu.CompilerParams(dimension_semantics=sem, vmem_limit_bytes=vmem_mib * MIB)


def _sigmoid(x):
    return jax.nn.sigmoid(x)


NN = (((1,), (0,)), ((), ()))
NT = (((1,), (1,)), ((), ()))


def _parts(x, n):
    out = []
    for i in range(n):
        p = x.astype(BF16)
        out.append(p)
        if i + 1 < n:
            x = x - p.astype(F32)
    return out


def _mdot(ap, bp, dims=NN, order=2):
    pairs = [(a, b) for i, a in enumerate(ap) for j, b in enumerate(bp) if i + j < order]
    (ca,), (cb,) = dims[0]
    lhs = jnp.concatenate([a for a, _ in pairs], axis=ca) if len(pairs) > 1 else pairs[0][0]
    rhs = jnp.concatenate([b for _, b in pairs], axis=cb) if len(pairs) > 1 else pairs[0][1]
    return lax.dot_general(lhs, rhs, dims, preferred_element_type=F32)


def _cumsum_matrix(t_n, rev):
    r = lax.broadcasted_iota(jnp.int32, (t_n, 3 * t_n), 0)
    c = lax.broadcasted_iota(jnp.int32, (t_n, 3 * t_n), 1) % t_n
    return jnp.where((c >= r) if rev else (c <= r), 1.0, 0.0).astype(BF16)


def _cumsum(tri3_b, x):
    return jnp.dot(tri3_b, jnp.concatenate(_parts(x, 3), axis=0), preferred_element_type=F32)


def _cat_parts(xs, axis):
    return [jnp.concatenate(ps, axis=axis) for ps in zip(*xs)]


def _mm_wres_kernel(a_ref, w_ref, o_ref, wb_ref):
    i = pl.program_id(2)
    k_n = w_ref.shape[0]
    n_kc = 4 if k_n % (4 * LANES) == 0 else 1

    @pl.when(i == 0)
    def _():
        kc = k_n // n_kc
        acc = None
        for c in range(n_kc):
            ks = slice(c * kc, (c + 1) * kc)
            wb = w_ref[ks, :].astype(BF16)
            wb_ref[ks, :] = wb
            t = jnp.dot(a_ref[:, ks], wb, preferred_element_type=F32)
            acc = t if acc is None else acc + t
        o_ref[...] = acc.astype(o_ref.dtype)

    @pl.when(i > 0)
    def _():
        o_ref[...] = jnp.dot(a_ref[...], wb_ref[...], preferred_element_type=F32).astype(o_ref.dtype)


def _matmul(a, w, *, out_dtype=F32, a_off=0):
    g_n, k_n, n_n = w.shape
    m_n = a.shape[1]
    assert w.dtype == F32 and k_n <= 2048, "weight block = all of K for one column block"
    tm = _pick(m_n, (1024, 512, 256, 128, 64))
    tn = _pick(n_n, (1024, 512, 256, 128))
    return pl.pallas_call(
        _mm_wres_kernel,
        grid=(g_n, n_n // tn, m_n // tm),
        in_specs=[pl.BlockSpec((None, tm, k_n), lambda g, j, i: (g + a_off, i, 0)),
                  pl.BlockSpec((None, k_n, tn), lambda g, j, i: (g, 0, j))],
        out_specs=pl.BlockSpec((None, tm, tn), lambda g, j, i: (g, i, j)),
        out_shape=jax.ShapeDtypeStruct((g_n, m_n, n_n), out_dtype),
        scratch_shapes=[pltpu.VMEM((k_n, tn), BF16)],
        compiler_params=_cparams(("parallel", "parallel", "arbitrary"), 48),
        name="matmul_wres",
    )(a, w)


def _mm2(a, w, **kw):
    return _matmul(a[None], w[None], **kw)[0]


def _ada_kernel(c_ref, w_ref, b_ref, o_ref):
    c = c_ref[...]
    s = (c * _sigmoid(c)).astype(BF16)
    o_ref[...] = jnp.dot(s, w_ref[...].astype(BF16), preferred_element_type=F32) + b_ref[...]


def _ada(c8, ada_w, ada_b):
    depth, d, n = ada_w.shape
    tn = _pick(n, (1024, 512, 256, 128))
    return pl.pallas_call(
        _ada_kernel,
        grid=(depth, n // tn),
        in_specs=[pl.BlockSpec((8, d), lambda l, j: (0, 0)),
                  pl.BlockSpec((None, d, tn), lambda l, j: (l, 0, j)),
                  pl.BlockSpec((None, 1, tn), lambda l, j: (l, 0, j))],
        out_specs=pl.BlockSpec((None, 8, tn), lambda l, j: (l, 0, j)),
        out_shape=jax.ShapeDtypeStruct((depth, 8, n), F32),
        compiler_params=_cparams(("parallel", "parallel"), 40),
        name="ada",
    )(c8, ada_w, ada_b.reshape(depth, 1, n))


def _ln_mod_kernel(z_ref, y_ref, mod_ref, g_ref, b_ref, mod2_ref, *out_refs, gate_j, mod_j, alpha):
    m = mod_ref[...]
    zz = alpha * z_ref[...] + y_ref[...] * m[gate_j:gate_j + 1]
    mu = jnp.mean(zz, axis=-1, keepdims=True)
    zc = zz - mu
    var = jnp.mean(zc * zc, axis=-1, keepdims=True)
    zn = zc * lax.rsqrt(var + LN_EPS) * g_ref[...] + b_ref[...]
    out_refs[0][...] = zn
    if mod_j is not None:
        m2 = mod2_ref[...]
        out_refs[1][...] = (zn * (1 + m2[mod_j + 1:mod_j + 2]) + m2[mod_j:mod_j + 1]).astype(BF16)


def _ln_mod(z, y, mod, ln_g, ln_b, mod2, *, gate_j, mod_j, tr, n_ctx, alpha):
    b_n, l_z, d = z.shape
    l_y = y.shape[1]
    z_off = (l_z - l_y) // tr
    ncb = (n_ctx - (l_z - l_y)) // tr
    seg = lambda b, t: (b, jnp.where(t < ncb, 0, 1), 0, 0)
    row = pl.BlockSpec((None, tr, d), lambda b, t: (b, t, 0))
    out_shape = [jax.ShapeDtypeStruct((b_n, l_y, d), F32)]
    out_specs = [row]
    if mod_j is not None:
        out_shape.append(jax.ShapeDtypeStruct((b_n, l_y, d), BF16))
        out_specs.append(row)
    res = pl.pallas_call(
        functools.partial(_ln_mod_kernel, gate_j=gate_j, mod_j=mod_j, alpha=alpha),
        grid=(b_n, l_y // tr),
        in_specs=[pl.BlockSpec((None, tr, d), lambda b, t: (b, t + z_off, 0)),
                  row,
                  pl.BlockSpec((None, None, 6, d), seg),
                  pl.BlockSpec((1, d), lambda b, t: (0, 0)),
                  pl.BlockSpec((1, d), lambda b, t: (0, 0)),
                  pl.BlockSpec((None, None, 6, d), seg)],
        out_specs=out_specs,
        out_shape=out_shape,
        compiler_params=_cparams(("parallel", "parallel"), 40),
        name="ln_mod",
    )(z, y, mod, ln_g.reshape(1, d), ln_b.reshape(1, d), mod2)
    return res if mod_j is not None else (res[0], None)


def _cast_kernel(x_ref, o_ref):
    o_ref[...] = x_ref[...].astype(o_ref.dtype)


def _to_bf16(w):
    g_n, k_n, n_n = w.shape
    tk = _pick(k_n, (512, 256, 128))
    spec = pl.BlockSpec((None, tk, n_n), lambda g, i: (g, i, 0))
    return pl.pallas_call(
        _cast_kernel, grid=(g_n, k_n // tk), in_specs=[spec], out_specs=spec,
        out_shape=jax.ShapeDtypeStruct(w.shape, BF16),
        compiler_params=_cparams(("parallel", "parallel"), 32),
        name="to_bf16",
    )(w)


def _proj_ln_kernel(a_ref, w_ref, z_ref, mod_ref, g_ref, b_ref, mod2_ref, zo_ref, *ho_refs,
                    nk, gate_j, mod_j, alpha, tr, ncb):
    t, k = pl.program_id(1), pl.program_id(2)
    tm = zo_ref.shape[0]
    rb = tm // 2 if tm % 16 == 0 else tm

    def accumulate(first):
        for r0 in range(0, tm, rb):
            p = jnp.dot(a_ref[r0:r0 + rb, :], w_ref[...], preferred_element_type=F32)
            if first:
                zo_ref[r0:r0 + rb, :] = p
            else:
                zo_ref[r0:r0 + rb, :] += p

    rs = math.gcd(tr, 64)

    def finish():
        for sb in range(tm // rs):
            rows = slice(sb * rs, (sb + 1) * rs)
            is_ctx = t * (tm // tr) + (sb * rs) // tr < ncb
            m = jnp.where(is_ctx, mod_ref[0], mod_ref[1])
            zz = alpha * z_ref[rows, :] + zo_ref[rows, :] * m[gate_j:gate_j + 1]
            mu = jnp.mean(zz, axis=-1, keepdims=True)
            zc = zz - mu
            var = jnp.mean(zc * zc, axis=-1, keepdims=True)
            zn = zc * lax.rsqrt(var + LN_EPS) * g_ref[...] + b_ref[...]
            zo_ref[rows, :] = zn
            if mod_j is not None:
                m2 = jnp.where(is_ctx, mod2_ref[0], mod2_ref[1])
                ho_refs[0][rows, :] = (zn * (1 + m2[mod_j + 1:mod_j + 2]) + m2[mod_j:mod_j + 1]).astype(BF16)

    if nk == 1:
        accumulate(True)
        finish()
    else:
        @pl.when(k == 0)
        def _():
            accumulate(True)

        @pl.when((k > 0) & (k < nk - 1))
        def _():
            accumulate(False)

        @pl.when(k == nk - 1)
        def _():
            accumulate(False)
            finish()


def _proj_ln(a, w, g, z, mod, ln_g, ln_b, mod2, *, gate_j, mod_j, tr, n_ctx, alpha):
    b_n, l_a, k_n = a.shape
    d = w.shape[2]
    if z.shape[1] != l_a:
        y = _matmul(a.reshape(1, b_n * l_a, k_n), w[g:g + 1])[0].reshape(b_n, l_a, d)
        return _ln_mod(z, y, mod, ln_g, ln_b, mod2, gate_j=gate_j, mod_j=mod_j, tr=tr, n_ctx=n_ctx, alpha=alpha)
    if w.dtype != BF16:
        w = _to_bf16(w)
    tm = tr * _pick(l_a // tr, (3, 2, 1))
    tk = k_n if k_n <= 2048 else _pick(k_n, (1408, 1024, 512, 256, 128))
    nk = k_n // tk
    row = pl.BlockSpec((None, tm, d), lambda b, t, k: (b, t, 0))
    seg = pl.BlockSpec((None, 2, 6, d), lambda b, t, k: (b, 0, 0, 0))
    vec = pl.BlockSpec((1, d), lambda b, t, k: (0, 0))
    out_shape = [jax.ShapeDtypeStruct((b_n, l_a, d), F32)]
    out_specs = [row]
    if mod_j is not None:
        out_shape.append(jax.ShapeDtypeStruct((b_n, l_a, d), BF16))
        out_specs.append(row)
    res = pl.pallas_call(
        functools.partial(_proj_ln_kernel, nk=nk, gate_j=gate_j, mod_j=mod_j, alpha=alpha, tr=tr,
                          ncb=n_ctx // tr),
        grid=(b_n, l_a // tm, nk),
        in_specs=[pl.BlockSpec((None, tm, tk), lambda b, t, k: (b, t, k)),
                  pl.BlockSpec((None, tk, d), lambda b, t, k: (g, k, 0),
                               pipeline_mode=pl.Buffered(1) if nk == 1 else None),
                  row, seg, vec, vec, seg],
        out_specs=out_specs,
        out_shape=out_shape,
        compiler_params=_cparams(("parallel", "parallel", "arbitrary"), 56),
        name="proj_ln",
    )(a, w, z, mod, ln_g.reshape(1, d), ln_b.reshape(1, d), mod2)
    return res if mod_j is not None else (res[0], None)


def _seg_shift(x, row, shift, n_ctx):
    l_n = x.shape[0]
    rolled = pltpu.roll(x, (-shift) % l_n, 0)
    src = row + shift
    same_seg = (src >= 0) & (src < l_n) & ((src < n_ctx) == (row < n_ctx))
    return jnp.where(same_seg, rolled, 0.0)


def _rw_mix_kernel(z_ref, mod_ref, mu_ref, o_ref, *, n_ctx):
    z = z_ref[...]
    row = lax.broadcasted_iota(jnp.int32, z.shape, 0)
    is_ctx = row < n_ctx
    shift = jnp.where(is_ctx, mod_ref[0, 0:1, :], mod_ref[1, 0:1, :])
    scale = jnp.where(is_ctx, mod_ref[0, 1:2, :], mod_ref[1, 1:2, :])
    h = z * (1 + scale) + shift
    dx = 0.5 * (_seg_shift(h, row, -1, n_ctx) + _seg_shift(h, row, 1, n_ctx)) - h
    for n in range(6):
        o_ref[n] = (h + dx * mu_ref[n:n + 1, :]).astype(BF16)


def _rw_mix(z, mod, mu, *, n_ctx):
    b_n, l_n, d = z.shape
    tc = _pick(d, (256, 128))
    return pl.pallas_call(
        functools.partial(_rw_mix_kernel, n_ctx=n_ctx),
        grid=(b_n, d // tc),
        in_specs=[pl.BlockSpec((None, l_n, tc), lambda b, j: (b, 0, j)),
                  pl.BlockSpec((None, 2, 6, tc), lambda b, j: (b, 0, 0, j)),
                  pl.BlockSpec((6, tc), lambda b, j: (0, j))],
        out_specs=pl.BlockSpec((6, None, l_n, tc), lambda b, j: (0, b, 0, j)),
        out_shape=jax.ShapeDtypeStruct((6, b_n, l_n, d), BF16),
        compiler_params=_cparams(("parallel", "parallel"), 48),
        name="rw_mix",
    )(z, mod, mu)


def _lora_kernel(x_ref, a_ref, b_ref, o_ref, *, act):
    tm = x_ref.shape[0]
    rb = tm // 4 if tm % 64 == 0 else tm
    ts = [jnp.dot(x_ref[r0:r0 + rb, :], a_ref[...], preferred_element_type=F32) for r0 in range(0, tm, rb)]
    for i, t in enumerate(ts):
        if act == "tanh":
            t = jnp.tanh(t)
        elif act == "sigmoid":
            t = _sigmoid(t)
        o_ref[i * rb:(i + 1) * rb, :] = jnp.dot(t.astype(BF16), b_ref[...], preferred_element_type=F32)


def _lora(xs, x_idx, a, b, act):
    g_n, d, r = a.shape
    m_n = xs.shape[1]
    tm = _pick(m_n, (512, 256, 128, 64))
    return pl.pallas_call(
        functools.partial(_lora_kernel, act=act),
        grid=(g_n, m_n // tm),
        in_specs=[pl.BlockSpec((None, tm, d), lambda g, i: (x_idx, i, 0)),
                  pl.BlockSpec((None, d, r), lambda g, i: (g, 0, 0)),
                  pl.BlockSpec((None, r, d), lambda g, i: (g, 0, 0))],
        out_specs=pl.BlockSpec((None, tm, d), lambda g, i: (g, i, 0)),
        out_shape=jax.ShapeDtypeStruct((g_n, m_n, d), F32),
        compiler_params=_cparams(("parallel", "parallel"), 40),
        name="lora",
    )(xs, a, b)


def _chunk_of(q, ncc, nc, rev):
    if not rev:
        return q
    return jnp.where(q < ncc, ncc - 1 - q, nc - 1 - (q - ncc))


def _rwkv_kernel(r_ref, k_ref, v_ref, lw_ref, la_ref, g_ref, w0_ref, a0_ref, kk_ref, ka_ref, rk_ref,
                 gng_ref, gnb_ref, o_ref,
                 y_scr, mr_s, n_s, *, n_ctx):
    t_n = CHUNK
    h2 = 2 * t_n
    l_n = r_ref.shape[0]
    nc, ncc = l_n // t_n, n_ctx // t_n
    group = _pick(nc, (6, 4, 3, 2, 1))
    lane = lax.broadcasted_iota(jnp.int32, (1, LANES), 1)
    m1 = jnp.where(lane < RW_HEAD, 1.0, 0.0)
    m2 = 1.0 - m1
    ri = lax.broadcasted_iota(jnp.int32, (LANES, LANES), 0)
    ci = lax.broadcasted_iota(jnp.int32, (LANES, LANES), 1)
    same_head = (ri // RW_HEAD) == (ci // RW_HEAD)
    gsum_b = jnp.where(same_head, 1.0, 0.0).astype(BF16)
    gavg_b = jnp.where(same_head, 1.0 / RW_HEAD, 0.0).astype(BF16)
    eye = jnp.where(ri == ci, 1.0, 0.0)
    tr_i, tc_i = ri % t_n, ci % t_n
    blk_base = (ri // INV_BASE) == (ci // INV_BASE)
    sizes = [INV_BASE * 2 ** i for i in range(1, 8) if INV_BASE * 2 ** i <= t_n]
    blk_sibling = [((ri // b) == (ci // b)) & ((ri // (b // 2)) != (ci // (b // 2))) for b in sizes]
    k_k, k_a = kk_ref[...], ka_ref[...]

    def stack(x):
        return jnp.concatenate([x * m1, x * m2], axis=0)

    def rows_of(c):
        return pl.ds(pl.multiple_of(c * t_n, t_n), t_n)

    def head_sum(x, w_b):
        return _mdot(_parts(x, 2), [w_b], order=2)

    tri3_b = [_cumsum_matrix(t_n, rev) for rev in (False, True)]
    strict = [(tc_i > tr_i) if rev else (tc_i < tr_i) for rev in (False, True)]
    incl = [(tc_i >= tr_i) if rev else (tc_i <= tr_i) for rev in (False, True)]

    def stage_prep(d, c):
        rows = rows_of(c)
        k, r, v = k_ref[rows, :], r_ref[rows, :], v_ref[rows, :]
        kkr = k * k_k
        lw = -RW_DECAY_SCALE * _sigmoid(w0_ref[d:d + 1, :] + lw_ref[d, rows, :])
        a = _sigmoid(a0_ref[d:d + 1, :] + la_ref[d, rows, :])
        return dict(d=d, c=c, k=k, r=r, v=v, lw=lw, a=a, kkr=kkr, ss=head_sum(kkr * kkr, gsum_b),
                    cum=_cumsum(tri3_b[d], lw))

    def stage_amat(s):
        d, cum, lw, a = s["d"], s["cum"], s["lw"], s["a"]
        kk = s["kkr"] * lax.rsqrt(s["ss"] + 1e-12)
        kd = s["k"] * (1 + (a - 1) * k_a)
        bv = kk * a
        p_end = cum[0:1, :] if d == 1 else cum[t_n - 1:t_n, :]
        e_m = jnp.exp(-cum)
        e_h = jnp.exp(p_end - cum)
        ktp = _parts(stack(kk * jnp.exp(cum - lw)), 2)
        rt = stack(s["r"] * jnp.exp(cum))
        k2p = _cat_parts([_parts(stack(bv * e_m), 1), _parts(stack(kd * e_m), 1)], 0)
        return dict(d=d, c=s["c"], ktp=ktp, rt=rt, vp=_parts(stack(s["v"]), 1), p_end=p_end,
                    bh=stack(bv * e_h), kh=stack(kd * e_h),
                    amat=_mdot(_cat_parts([ktp, _parts(rt, 2)], 0), k2p, NT))

    def stage_square(s):
        d, amat = s["d"], s["amat"]
        lt = jnp.where(strict[d], amat[:h2, :h2], 0.0).T
        ldt = jnp.where(blk_base, lt, 0.0)
        ltp = _parts(ldt, 2)
        msk = jnp.concatenate([jnp.where(strict[d], amat[:h2, h2:], 0.0),
                               jnp.where(incl[d], amat[h2:, h2:], 0.0)], axis=0)
        s = dict(s, pt=eye - ldt, xt=_mdot(ltp, ltp[:1]),
                 ct=[jnp.where(m, lt, 0.0).astype(BF16) for m in blk_sibling],
                 av=_mdot(_parts(msk, 2), s["vp"][:1]),
                 arbp=_parts(jnp.where(incl[d], amat[h2:, :h2], 0.0), 2))
        del s["amat"]
        return s

    def stage_merge_a(s, level):
        return dict(s, t1=_mdot([s["ct"][level]], _parts(s["pt"], 2)).astype(BF16))

    def stage_merge_b(s):
        return dict(s, pt=s["pt"] - _mdot(_parts(s["pt"], 2), [s["t1"]]))

    def stage_double(s, final):
        xh = _parts(s["xt"], 1)
        ptp = _parts(s["pt"], 2)
        if final:
            return dict(s, pt=s["pt"] + _mdot(xh, ptp))
        rhs = [jnp.concatenate([ptp[0], xh[0]], axis=1), jnp.concatenate([ptp[1], jnp.zeros_like(xh[0])], axis=1)]
        both = _mdot(xh, rhs)
        return dict(s, pt=s["pt"] + both[:, :LANES], xt=both[:, LANES:])

    def stage_solve(s):
        rhs = jnp.concatenate([s["ktp"][0], (-s["av"][:h2]).astype(BF16)], axis=1)
        return dict(s, wub=_mdot(_parts(s["pt"].T, 2), [rhs]).astype(BF16))

    def stage_fold(s):
        d, c, wub = s["d"], s["c"], s["wub"]
        aw = _mdot(s["arbp"], [wub])
        zb = jnp.zeros((h2, LANES), BF16)
        lhs = _cat_parts([_parts(s["bh"].T, 2), _parts(s["kh"].T, 2)], 1)
        rhs = jnp.concatenate([wub, jnp.concatenate([zb, s["vp"][0]], axis=1)], axis=0)
        mn = _mdot(lhs, [rhs])
        dg = jnp.where(ri == ci, jnp.broadcast_to(jnp.exp(s["p_end"]), (LANES, LANES)), 0.0)
        rp = s["rt"] - aw[:, :LANES]
        mrp = _parts(jnp.concatenate([dg - mn[:, :LANES], rp[:t_n] + rp[t_n:]], axis=0), 2)
        for i in range(2):
            mr_s[d, c, i] = mrp[i]
        n_s[d, c] = mn[:, LANES:]
        y0 = s["av"][h2:] + aw[:, LANES:]
        return y0[:t_n] + y0[t_n:]

    def seq(q, hs):
        cs = (q, _chunk_of(q, ncc, nc, True))
        mh = [_mdot([mr_s[d, cs[d], 0], mr_s[d, cs[d], 1]], _parts(hs[d], 2)) for d in (0, 1)]
        for d in (0, 1):
            y_scr[rows_of(cs[d]), :] += mh[d][h2:]
        return tuple(mh[d][:h2] + n_s[d, cs[d]] for d in (0, 1))

    def local(i, hs, with_seq):
        steps = [i * group + g for g in range(group)]
        pending = [q - group for q in steps] if with_seq else []
        sts = [stage_prep(d, q if d == 0 else _chunk_of(q, ncc, nc, True)) for q in steps for d in (0, 1)]
        n_dbl = INV_BASE.bit_length() - 2
        stages = [stage_amat, stage_square]
        stages += [functools.partial(stage_double, final=i == n_dbl - 1) for i in range(n_dbl)]
        for level in range(len(blk_sibling)):
            stages += [functools.partial(stage_merge_a, level=level), stage_merge_b]
        stages += [stage_solve]
        for stage in stages:
            sts = [stage(s) for s in sts]
            if pending:
                hs = seq(pending.pop(0), hs)
        for s in sts:
            y0 = stage_fold(s)
            y_scr[rows_of(s["c"]), :] += y0
        while pending:
            hs = seq(pending.pop(0), hs)
        return hs

    y_scr[...] = jnp.zeros(y_scr.shape, F32)
    zero = jnp.zeros((LANES, LANES), F32)
    n_trip = nc // group
    hs = local(0, (zero, zero), False)
    hs = lax.fori_loop(1, n_trip, functools.partial(local, with_seq=True), hs)
    for q in range((n_trip - 1) * group, nc):
        hs = seq(q, hs)

    n_post = _pick(nc, (9, 6, 4, 3, 2, 1))

    def post(i, carry):
        rows = [rows_of(i * n_post + g) for g in range(n_post)]

        def bonus_sum(rw):
            k, r = k_ref[rw, :], r_ref[rw, :]
            kd_f = k * (1 + (_sigmoid(a0_ref[0:1, :] + la_ref[0, rw, :]) - 1) * k_a)
            kd_b = k * (1 + (_sigmoid(a0_ref[1:2, :] + la_ref[1, rw, :]) - 1) * k_a)
            return head_sum(r * (kd_f + kd_b) * rk_ref[...], gsum_b)
        bsum = [bonus_sum(rw) for rw in rows]
        ys = [y_scr[rw, :] for rw in rows]
        ycs = [y - m for y, m in zip(ys, [head_sum(y, gavg_b) for y in ys])]
        var = [head_sum(yc * yc, gavg_b) for yc in ycs]
        for rw, yc, vr, bs in zip(rows, ycs, var, bsum):
            yn = yc * lax.rsqrt(vr + RW_GN_EPS) * gng_ref[...] + gnb_ref[...]
            o_ref[rw, :] = ((yn + bs * v_ref[rw, :]) * g_ref[rw, :]).astype(BF16)
        return carry

    lax.fori_loop(0, nc // n_post, post, 0)


def _rwkv_scan(rkv, lw, la, g, w0, a0, k_k, k_a, r_k, gn_g, gn_b, *, n_ctx):
    _, b_n, l_n, d = rkv.shape
    nc = l_n // CHUNK
    col = lambda n: pl.BlockSpec((None, None, l_n, LANES), lambda b, p, n=n: (n, b, 0, p))
    two = pl.BlockSpec((2, None, l_n, LANES), lambda b, p: (0, b, 0, p))
    par = lambda rows: pl.BlockSpec((rows, LANES), lambda b, p: (0, p))
    return pl.pallas_call(
        functools.partial(_rwkv_kernel, n_ctx=n_ctx),
        grid=(b_n, d // LANES),
        in_specs=[col(0), col(1), col(2), two, two,
                  pl.BlockSpec((None, l_n, LANES), lambda b, p: (b, 0, p)),
                  par(2), par(2), par(1), par(1), par(1), par(1), par(1)],
        out_specs=pl.BlockSpec((None, l_n, LANES), lambda b, p: (b, 0, p)),
        out_shape=jax.ShapeDtypeStruct((b_n, l_n, d), BF16),
        scratch_shapes=[pltpu.VMEM((l_n, LANES), F32),
                        pltpu.VMEM((2, nc, 2, LANES + CHUNK, LANES), BF16),
                        pltpu.VMEM((2, nc, LANES, LANES), F32)],
        compiler_params=_cparams(("parallel", "parallel"), 56),
        name="rwkv_scan",
    )(rkv, rkv, rkv, lw, la, g, w0, a0, k_k.reshape(1, d), k_a.reshape(1, d), r_k.reshape(1, d),
      gn_g.reshape(1, d), gn_b.reshape(1, d))


def _pad_axis(w, axis, to):
    pad = [(0, 0)] * w.ndim
    pad[axis] = (0, to - w.shape[axis])
    return jnp.pad(w, pad)


def _rwkv7_layer(z, mod, n_ctx, mu, w_rkv, w0, w1, w2, a0, a1, a2, g1, g2, k_k, k_a, r_k, gn_g, gn_b):
    b_n, l_n, d = z.shape
    m_n = b_n * l_n
    xs = _rw_mix(z, mod, mu, n_ctx=n_ctx).reshape(6, m_n, d)
    rkv = _matmul(xs, w_rkv)
    r_w = -(-w1.shape[-1] // LANES) * LANES
    r_a = -(-a1.shape[-1] // LANES) * LANES
    lw = _lora(xs, 3, _pad_axis(w1, 2, r_w).astype(BF16), _pad_axis(w2, 1, r_w).astype(BF16), "tanh")
    la = _lora(xs, 4, _pad_axis(a1, 2, r_a).astype(BF16), _pad_axis(a2, 1, r_a).astype(BF16), None)
    gate = _lora(xs, 5, g1[None].astype(BF16), g2[None].astype(BF16), "sigmoid")
    o = _rwkv_scan(rkv.reshape(3, b_n, l_n, d), lw.reshape(2, b_n, l_n, d), la.reshape(2, b_n, l_n, d),
                   gate.reshape(b_n, l_n, d), w0, a0, k_k, k_a, r_k, gn_g, gn_b, n_ctx=n_ctx)
    return o


def _qkv_rope_kernel(a_ref, w_ref, cos_ref, sa_ref, sb_ref, o_ref, wb_ref, *, n_q, n_qk):
    j, i = pl.program_id(0), pl.program_id(1)

    @pl.when(i == 0)
    def _():
        wb_ref[...] = w_ref[...].astype(BF16)

    tn = o_ref.shape[1]
    sw = math.gcd(tn, 2 * DA_HEAD)

    def product(s):
        return jnp.dot(a_ref[...], wb_ref[:, s * sw:(s + 1) * sw], preferred_element_type=F32)

    @pl.when(j < n_qk)
    def _():
        q = DA_HEAD // 4
        q_scale = jnp.where(j < n_q, DA_HEAD ** -0.5 * math.log2(math.e), 1.0)
        cos, s_a, s_b = cos_ref[...] * q_scale, sa_ref[...] * q_scale, sb_ref[...] * q_scale

        def rotary(s, x):
            for c in range(0, sw, DA_HEAD):
                xs = x[:, c:c + DA_HEAD]
                rot = xs * cos + pltpu.roll(xs, DA_HEAD - q, 1) * s_a + pltpu.roll(xs, q, 1) * s_b
                o_ref[:, s * sw + c:s * sw + c + DA_HEAD] = rot.astype(BF16)
        x_prev = product(0)
        for s in range(1, tn // sw):
            x_next = product(s)
            rotary(s - 1, x_prev)
            x_prev = x_next
        rotary(tn // sw - 1, x_prev)

    @pl.when(j >= n_qk)
    def _():
        for s in range(tn // sw):
            o_ref[:, s * sw:(s + 1) * sw] = product(s).astype(BF16)


def _qkv_rope(h, w_qkv, cos, s_a, s_b):
    b_n, l_n, d = h.shape
    d3 = w_qkv.shape[1]
    tm = _pick(l_n, (768, 1024, 512, 256, 128, 64))
    tn = _pick(d, (1024, 512, 256, 128))
    n_t = l_n // tm
    tab = pl.BlockSpec((tm, DA_HEAD), lambda j, i: (i % n_t, 0))
    out = pl.pallas_call(
        functools.partial(_qkv_rope_kernel, n_q=d // tn, n_qk=2 * d // tn),
        grid=(d3 // tn, b_n * n_t),
        in_specs=[pl.BlockSpec((tm, d), lambda j, i: (i, 0)),
                  pl.BlockSpec((d, tn), lambda j, i: (0, j)), tab, tab, tab],
        out_specs=pl.BlockSpec((tm, tn), lambda j, i: (i, j)),
        out_shape=jax.ShapeDtypeStruct((b_n * l_n, d3), BF16),
        scratch_shapes=[pltpu.VMEM((d, tn), BF16)],
        compiler_params=_cparams(("parallel", "arbitrary"), 48),
        name="qkv_rope",
    )(h.reshape(b_n * l_n, d), w_qkv, cos, s_a, s_b)
    return out.reshape(b_n, l_n, d3)


def _attn_kernel(q_ref, k_ref, v_ref, lam_ref, sg_ref, o_ref, *, ncb, n_ctx, lam_init):
    qi = pl.program_id(2)
    lv = lam_ref[...]
    lam = (jnp.exp(jnp.sum(lv[0:1] * lv[1:2], axis=-1, keepdims=True))
           - jnp.exp(jnp.sum(lv[2:3] * lv[3:4], axis=-1, keepdims=True)) + lam_init)

    hw = 2 * DA_HEAD
    n_hh = o_ref.shape[1] // hw

    def attend(nk):
        def scores(hh, m):
            cols = slice(hh * hw + m * DA_HEAD, hh * hw + (m + 1) * DA_HEAD)
            return lax.dot_general(q_ref[:, cols], k_ref[0:nk, cols], NT, preferred_element_type=F32)
        s_all = [[scores(hh, m) for m in (0, 1)] for hh in range(n_hh)]
        for hh in range(n_hh):
            def probs(s):
                e = jnp.exp2(s - jnp.max(s, axis=-1, keepdims=True))
                return e, 1.0 / jnp.sum(e, axis=-1, keepdims=True)
            e0, i0 = probs(s_all[hh][0])
            e1, i1 = probs(s_all[hh][1])
            v = v_ref[0:nk, hh * hw:(hh + 1) * hw]
            o = (jnp.dot(e0.astype(BF16), v, preferred_element_type=F32) * i0
                 - jnp.dot(e1.astype(BF16), v, preferred_element_type=F32) * (lam * i1))
            o = o * lax.rsqrt(jnp.mean(o * o, axis=-1, keepdims=True) + 1e-5) * sg_ref[...] * (1 - lam_init)
            o_ref[:, hh * hw:(hh + 1) * hw] = o.astype(BF16)

    if ncb > 0:
        @pl.when(qi < ncb)
        def _():
            attend(n_ctx)

    @pl.when(qi >= ncb)
    def _():
        attend(k_ref.shape[0])


def _attention(qkv, lam_vec, sub_g, *, tq, n_ctx, lam_init):
    b_n, l_n, d3 = qkv.shape
    d = d3 // 3
    hw = 2 * DA_HEAD
    n_hh = 2 if (d // hw) % 2 == 0 else 1
    bw = n_hh * hw
    nh = d // bw
    return pl.pallas_call(
        functools.partial(_attn_kernel, ncb=n_ctx // tq, n_ctx=n_ctx, lam_init=lam_init),
        grid=(b_n, nh, l_n // tq),
        in_specs=[pl.BlockSpec((None, tq, bw), lambda b, h, t: (b, t, h)),
                  pl.BlockSpec((None, l_n, bw), lambda b, h, t: (b, 0, nh + h)),
                  pl.BlockSpec((None, l_n, bw), lambda b, h, t: (b, 0, 2 * nh + h)),
                  pl.BlockSpec((4, DA_HEAD), lambda b, h, t: (0, 0)),
                  pl.BlockSpec((1, hw), lambda b, h, t: (0, 0))],
        out_specs=pl.BlockSpec((None, tq, bw), lambda b, h, t: (b, t, h)),
        out_shape=jax.ShapeDtypeStruct((b_n, l_n, d), BF16),
        compiler_params=_cparams(("parallel", "parallel", "arbitrary"), 48),
        name="diff_attn",
    )(qkv, qkv, qkv, lam_vec, sub_g.reshape(1, hw))


def _rope_tables(n_ctx, n_lat):
    n_rows = n_lat // GRID_W
    row = jnp.repeat(jnp.arange(n_rows, dtype=F32), GRID_W)
    col = jnp.tile(jnp.arange(GRID_W, dtype=F32), n_rows)
    nf = DA_HEAD // 4
    inv_freq = ROPE_BASE ** (-jnp.arange(nf, dtype=F32) / nf)
    ang_r, ang_c = row[:, None] * inv_freq, col[:, None] * inv_freq
    ang = jnp.concatenate([ang_r, ang_r, ang_c, ang_c], axis=-1)
    ang = jnp.concatenate([jnp.zeros((n_ctx, DA_HEAD), F32), ang], axis=0)
    cos, sin = jnp.cos(ang), jnp.sin(ang)
    even_q = (jnp.arange(DA_HEAD) // nf) % 2 == 0
    return cos, jnp.where(even_q, -sin, 0.0), jnp.where(even_q, 0.0, sin)


def _diff_attention_layer(h, n_ctx, layer_idx, w_qkv, lam_vec, sub_g, *, tr):
    b_n, l_n, d = h.shape
    cos, s_a, s_b = _rope_tables(n_ctx, l_n - n_ctx)
    qkv = _qkv_rope(h, w_qkv, cos, s_a, s_b)
    lam_init = 0.8 - 0.6 * math.exp(-0.3 * layer_idx)
    o = _attention(qkv, lam_vec, sub_g, tq=tr, n_ctx=n_ctx, lam_init=lam_init)
    return o


def _hgrn_kernel(q_ref, i_ref, g_ref, ff_ref, fb_ref, low_ref, ng_ref, o_ref, o_scr, *, n_ctx, layer_idx):
    t_n = CHUNK
    l_n = q_ref.shape[0]
    nc, ncc = l_n // t_n, n_ctx // t_n
    r64 = lax.broadcasted_iota(jnp.int32, (t_n, t_n), 0)
    c64 = lax.broadcasted_iota(jnp.int32, (t_n, t_n), 1)

    def rows_of(c):
        return pl.ds(pl.multiple_of(c * t_n, t_n), t_n)

    f_refs = (ff_ref, fb_ref)
    lbs, incl = [], []
    tri3_b = [_cumsum_matrix(t_n, rev) for rev in (False, True)]
    for d in (0, 1):
        low = low_ref[d]
        e = jnp.exp(low - jnp.max(low, axis=0, keepdims=True))
        sm = e / jnp.sum(e, axis=0, keepdims=True)
        cs = sm[0:1]
        for rr in range(1, layer_idx + 1):
            cs = cs + sm[rr:rr + 1]
        lbs.append(cs - sm[0:1])
        incl.append((c64 >= r64) if d == 1 else (c64 <= r64))
    group = _pick(nc, (6, 4, 2, 1))

    def stage_cum(d, c):
        rows = rows_of(c)
        f = lbs[d] + (1.0 - lbs[d]) * _sigmoid(f_refs[d][rows, :])
        return dict(d=d, rows=rows, f=f, cum=_cumsum(tri3_b[d], jnp.log(f)))

    def stage_att(s):
        d, cum, rows = s["d"], s["cum"], s["rows"]
        b_end = cum[0:1, :] if d == 1 else cum[t_n - 1:t_n, :]
        qv = q_ref[rows, :]
        qd = (qv * _sigmoid(qv) * jnp.exp(cum)).astype(BF16)
        kk = 1.0 - s["f"]
        v = i_ref[rows, :]
        kd = (kk * jnp.exp(-cum)).astype(BF16)
        ke = (kk * jnp.exp(b_end - cum)).astype(BF16)
        return dict(d=d, rows=rows, qd=qd, vb=v.astype(BF16), dec=jnp.exp(b_end),
                    att=lax.dot_general(qd, kd, NT, preferred_element_type=F32),
                    upd=jnp.dot(v.T.astype(BF16), ke, preferred_element_type=F32))

    def stage_intra(s):
        att = jnp.where(incl[s["d"]], s["att"], 0.0).astype(BF16)
        return dict(s, o=jnp.dot(att, s["vb"], preferred_element_type=F32))

    def body(i, states):
        items = [(d, _chunk_of(i * group + g, ncc, nc, d == 1)) for g in range(group) for d in (0, 1)]
        sts = [stage_cum(d, c) for d, c in items]
        sts = [stage_att(s) for s in sts]
        sts = [stage_intra(s) for s in sts]
        states = list(states)
        for s in sts:
            d = s["d"]
            o = s["o"] + lax.dot_general(s["qd"], states[d].astype(BF16), NT, preferred_element_type=F32)
            o_scr[d, s["rows"], :] = o
            states[d] = states[d] * s["dec"] + s["upd"]
        return tuple(states)

    zero = jnp.zeros((LANES, LANES), F32)
    lax.fori_loop(0, nc // group, body, (zero, zero))

    p_n = t_n * _pick(nc, (4, 3, 2, 1))

    def post(c, carry):
        rows = pl.ds(pl.multiple_of(c * p_n, p_n), p_n)
        o = o_scr[0, rows, :] + o_scr[1, rows, :]
        o = o * lax.rsqrt(jnp.mean(o * o, axis=-1, keepdims=True) + 1e-5) * ng_ref[...]
        gv = g_ref[rows, :]
        o_ref[rows, :] = (o * (gv * _sigmoid(gv))).astype(BF16)
        return carry

    lax.fori_loop(0, l_n // p_n, post, 0)


def _hgrn2_layer(h, n_ctx, layer_idx, w_in, lower, norm_g):
    b_n, l_n, d = h.shape
    m_n = b_n * l_n
    nh = d // HG_EXPAND
    proj = _mm2(h.reshape(m_n, d), w_in).reshape(b_n, l_n, 5 * d)
    col = lambda n: pl.BlockSpec((None, l_n, LANES), lambda b, p, n=n: (b, 0, n * nh + p))
    o = pl.pallas_call(
        functools.partial(_hgrn_kernel, n_ctx=n_ctx, layer_idx=layer_idx),
        grid=(b_n, nh),
        in_specs=[col(0), col(1), col(2), col(3), col(4),
                  pl.BlockSpec((2, lower.shape[1], LANES), lambda b, p: (0, 0, p)),
                  pl.BlockSpec((1, LANES), lambda b, p: (0, 0))],
        out_specs=pl.BlockSpec((None, l_n, LANES), lambda b, p: (b, 0, p)),
        out_shape=jax.ShapeDtypeStruct((b_n, l_n, d), BF16),
        scratch_shapes=[pltpu.VMEM((2, l_n, LANES), F32)],
        compiler_params=_cparams(("parallel", "parallel"), 40),
        name="hgrn_scan",
    )(proj, proj, proj, proj, proj, lower, norm_g.reshape(1, LANES))
    return o


def _gelu_tanh(x):
    return 0.5 * x * (1.0 + jnp.tanh(math.sqrt(2.0 / math.pi) * (x + 0.044715 * (x * x * x))))


def _softplus(x):
    return jnp.maximum(x, 0.0) + jnp.log1p(jnp.exp(-jnp.abs(x)))


SEG_PAD = 8


def _lin_scan(a_ref, u_ref, hl_s, cp_s, h_s, base, row0, n, h_in):
    seg = n // 8
    stride = seg + SEG_PAD
    n_p = a_ref.shape[1]
    chains = [(d, j) for d in (0, 1) for j in range(n_p)]

    def step(i, carry):
        out = []
        for (d, j), (hl, cp) in zip(chains, carry):
            idx = pl.ds(base + (i if d == 0 else seg - 1 - i), 8, stride=stride)
            a = a_ref[d, j, idx, :]
            hl = a * hl + u_ref[d, j, idx, :]
            cp = a * cp
            hl_s[d, j, idx, :] = hl
            cp_s[d, j, idx, :] = cp
            out.append((hl, cp))
        return tuple(out)

    init = tuple((jnp.zeros((8, LANES), F32), jnp.ones((8, LANES), F32)) for _ in chains)
    ends = lax.fori_loop(0, seg, step, init)
    h_out = [[None] * n_p, [None] * n_p]
    for (d, j), (hl_e, cp_e) in zip(chains, ends):
        carry = h_in[d][j]
        for s in (range(8) if d == 0 else range(7, -1, -1)):
            r0, p0 = row0 + s * seg, base + s * stride
            blk = hl_s[d, j, p0:p0 + seg, :] + cp_s[d, j, p0:p0 + seg, :] * carry
            if d == 0:
                h_s[j, r0:r0 + seg, :] = blk
            else:
                h_s[j, r0:r0 + seg, :] += blk
            carry = hl_e[s:s + 1, :] + cp_e[s:s + 1, :] * carry
        h_out[d][j] = carry
    return h_out


def _rglru_kernel(gb_ref, xb_ref, cw_ref, cb_ref, wg_ref, bg_ref, lam_ref, o_ref, a_s, u_s, h_s, hl_s, cp_s, *,
                  n_ctx):
    l_n = xb_ref.shape[0]
    n_lat = l_n - n_ctx
    x = xb_ref[...]
    row = lax.broadcasted_iota(jnp.int32, x.shape, 0)
    k_w = cw_ref.shape[0]
    xc = cb_ref[...] + sum(_seg_shift(x, row, j - (k_w - 1) // 2, n_ctx) * cw_ref[j:j + 1, :]
                           for j in range(k_w))
    xcb = xc.astype(BF16)
    n_p = x.shape[1] // LANES
    for d in (0, 1):
        gate = lambda g: _sigmoid(jnp.dot(xcb, wg_ref[d, g].astype(BF16), preferred_element_type=F32)
                                  + bg_ref[d, g:g + 1, :])
        log_a = -LR_C * gate(0) * _softplus(-lam_ref[d:d + 1, :])
        a = jnp.exp(log_a)
        u = jnp.sqrt(jnp.tanh(-log_a) * (jnp.exp(2.0 * log_a) + 1.0)) * gate(1) * xc
        base = 0
        for row0, n in ((0, n_ctx), (n_ctx, n_lat)):
            seg = n // 8
            for j in range(n_p if n else 0):
                for s in range(8):
                    src = slice(row0 + s * seg, row0 + (s + 1) * seg)
                    dst = slice(base + s * (seg + SEG_PAD), base + s * (seg + SEG_PAD) + seg)
                    a_s[d, j, dst, :] = a[src, j * LANES:(j + 1) * LANES]
                    u_s[d, j, dst, :] = u[src, j * LANES:(j + 1) * LANES]
            base += 8 * (seg + SEG_PAD) if n else 0
    h = [[jnp.zeros((1, LANES), F32)] * n_p] * 2
    base = 0
    for row0, n in ((0, n_ctx), (n_ctx, n_lat)):
        if n:
            h = _lin_scan(a_s, u_s, hl_s, cp_s, h_s, base, row0, n, h)
            base += 8 * (n // 8 + SEG_PAD)
    for j in range(n_p):
        cols = slice(j * LANES, (j + 1) * LANES)
        o_ref[:, cols] = (h_s[j, n_ctx:l_n, :] * _gelu_tanh(gb_ref[n_ctx:, cols])).astype(BF16)


def _rglru_layer(h, n_ctx, w_in, conv_w, conv_b, w_gate, b_gate, lam):
    b_n, l_n, d = h.shape
    n_lat = l_n - n_ctx
    nb = d // LR_BS
    proj = _mm2(h.reshape(b_n * l_n, d), w_in).reshape(b_n, l_n, 2 * d)
    k_w = conv_w.shape[0]
    plane = (LR_BS // LANES, l_n + 16 * SEG_PAD, LANES)
    o = pl.pallas_call(
        functools.partial(_rglru_kernel, n_ctx=n_ctx),
        grid=(b_n, nb),
        in_specs=[pl.BlockSpec((None, l_n, LR_BS), lambda b, j: (b, 0, j)),
                  pl.BlockSpec((None, l_n, LR_BS), lambda b, j: (b, 0, nb + j)),
                  pl.BlockSpec((k_w, LR_BS), lambda b, j: (0, j)),
                  pl.BlockSpec((1, LR_BS), lambda b, j: (0, j)),
                  pl.BlockSpec((2, 2, None, LR_BS, LR_BS), lambda b, j: (0, 0, j, 0, 0)),
                  pl.BlockSpec((2, 2, LR_BS), lambda b, j: (0, 0, j)),
                  pl.BlockSpec((2, LR_BS), lambda b, j: (0, j))],
        out_specs=pl.BlockSpec((None, n_lat, LR_BS), lambda b, j: (b, 0, j)),
        out_shape=jax.ShapeDtypeStruct((b_n, n_lat, d), BF16),
        scratch_shapes=[pltpu.VMEM((2,) + plane, F32), pltpu.VMEM((2,) + plane, F32), pltpu.VMEM(plane, F32),
                        pltpu.VMEM((2,) + plane, F32), pltpu.VMEM((2,) + plane, F32)],
        compiler_params=_cparams(("parallel", "parallel"), 56),
        name="rglru",
    )(proj, proj, conv_w, conv_b.reshape(1, d), w_gate, b_gate, lam)
    return o


def _ffn_up_kernel(h_ref, wg_ref, wv_ref, cg_ref, cv_ref, bg_ref, bv_ref, o_ref, u_scr, *, n_ctx, tr):
    l_n, tf = o_ref.shape
    k_w = cg_ref.shape[0]
    half = (k_w - 1) // 2
    n_buf, pad = u_scr.shape[0], (u_scr.shape[1] - l_n) // 2
    rblk = tr * _pick(l_n // tr, (3, 4, 2, 1))
    row = lax.broadcasted_iota(jnp.int32, (tr, LANES), 0)
    for p in range(n_buf):
        u_scr[p, 0:pad, :] = jnp.zeros((pad, 2 * LANES), F32)
        u_scr[p, pad + l_n:, :] = jnp.zeros((pad, 2 * LANES), F32)

    def weights(s):
        cols = slice(s * LANES, (s + 1) * LANES)
        return jnp.concatenate([wg_ref[:, cols], wv_ref[:, cols]], axis=1).astype(BF16)

    def product(s, w, r0):
        u_scr[s % n_buf, pad + r0:pad + r0 + rblk, :] = jnp.dot(h_ref[r0:r0 + rblk, :], w,
                                                                preferred_element_type=F32)

    def finish(s, r0):
        p, cols = s % n_buf, slice(s * LANES, (s + 1) * LANES)

        def conv(lane0, w_ref, b_ref):
            acc = None
            for j in range(k_w):
                sh = j - half
                x = u_scr[p, pad + r0 + sh:pad + r0 + sh + tr, lane0:lane0 + LANES]
                if sh < 0 and r0 in (0, n_ctx):
                    x = jnp.where(row < -sh, 0.0, x)
                if sh > 0 and r0 + tr in (n_ctx, l_n):
                    x = jnp.where(row >= tr - sh, 0.0, x)
                t = x * w_ref[j:j + 1, cols]
                acc = t if acc is None else acc + t
            return b_ref[:, cols] + acc
        gate = conv(0, cg_ref, bg_ref)
        val = conv(LANES, cv_ref, bv_ref)
        o_ref[r0:r0 + tr, cols] = (gate * _sigmoid(gate) * val).astype(BF16)

    n_s = tf // LANES
    for s in range(n_s + 1):
        w = weights(s) if s < n_s else None
        for r0 in range(0, l_n, rblk):
            if s < n_s:
                product(s, w, r0)
            if s > 0:
                for r1 in range(r0, r0 + rblk, tr):
                    finish(s - 1, r1)


def _conv_ffn(h, n_ctx, w_up, layer, conv_w, conv_b, *, tr):
    b_n, l_n, d = h.shape
    f = w_up.shape[2] // 2
    tf = _pick(f, (512, 256, 128))
    nf = f // tf
    k_w = conv_w.shape[0]
    cb = conv_b.reshape(1, 2 * f)
    act = pl.pallas_call(
        functools.partial(_ffn_up_kernel, n_ctx=n_ctx, tr=tr),
        grid=(b_n, nf),
        in_specs=[pl.BlockSpec((None, l_n, d), lambda b, j: (b, 0, 0), pipeline_mode=pl.Buffered(1)),
                  pl.BlockSpec((None, d, tf), lambda b, j: (layer, 0, j)),
                  pl.BlockSpec((None, d, tf), lambda b, j: (layer, 0, nf + j)),
                  pl.BlockSpec((k_w, tf), lambda b, j: (0, j)),
                  pl.BlockSpec((k_w, tf), lambda b, j: (0, nf + j)),
                  pl.BlockSpec((1, tf), lambda b, j: (0, j)),
                  pl.BlockSpec((1, tf), lambda b, j: (0, nf + j))],
        out_specs=pl.BlockSpec((None, l_n, tf), lambda b, j: (b, 0, j)),
        out_shape=jax.ShapeDtypeStruct((b_n, l_n, f), BF16),
        scratch_shapes=[pltpu.VMEM((3, l_n + 16, 2 * LANES), F32)],
        compiler_params=_cparams(("parallel", "arbitrary"), 56),
        name="ffn_up",
    )(h, w_up, w_up, conv_w, conv_w, cb, cb)
    return act


def kernel(x, c, ctx, c_ctx, ada_w, ada_b, ln_g, ln_b, ffn_w_up, ffn_conv_w, ffn_conv_b, ffn_w_down, rw_mu, rw_w_rkv, rw_w0, rw_w1, rw_w2, rw_a0, rw_a1, rw_a2, rw_g1, rw_g2, rw_k_k, rw_k_a, rw_r_k, rw_gn_g, rw_gn_b, rw_w_o, da_w_qkv, da_lambda, da_sub_g, da_w_o, hg_w_in, hg_lower, hg_norm_g, hg_w_o, lr_w_in, lr_conv_w, lr_conv_b, lr_w_gate, lr_b_gate, lr_lambda, lr_w_o):
    b_n, n_lat, d = x.shape
    n_ctx = ctx.shape[1]
    depth = ada_w.shape[0]
    assert depth == 4 and rw_mu.shape[0] == 1, "one occurrence of each of the four mixers"
    assert b_n + 1 <= 8 and n_ctx % CHUNK == 0 and n_lat % CHUNK == 0
    tr = math.gcd(math.gcd(n_ctx, n_lat), 256)
    alpha = (2 * depth) ** 0.25

    c8 = jnp.concatenate([c, c_ctx[None], jnp.zeros((8 - b_n - 1, d), F32)], axis=0)
    m = _ada(c8, ada_w, ada_b)
    m_lat = m[:, :b_n].reshape(depth, b_n, 1, 6, d)
    m_ctx = jnp.broadcast_to(m[:, b_n].reshape(depth, 1, 1, 6, d), (depth, b_n, 1, 6, d))
    mod = jnp.concatenate([m_ctx, m_lat], axis=2)

    z = jnp.concatenate([ctx, x], axis=1)
    w_down_b = _to_bf16(ffn_w_down)
    h = None
    for i in range(depth):
        last = i == depth - 1
        if i == 0:
            o = _rwkv7_layer(z, mod[0], n_ctx, rw_mu[0], rw_w_rkv[0], rw_w0[0], rw_w1[0], rw_w2[0], rw_a0[0],
                             rw_a1[0], rw_a2[0], rw_g1[0], rw_g2[0], rw_k_k[0], rw_k_a[0], rw_r_k[0],
                             rw_gn_g[0], rw_gn_b[0])
            w_o = rw_w_o
        elif i == 1:
            o = _diff_attention_layer(h, n_ctx, i, da_w_qkv[0], da_lambda[0], da_sub_g[0], tr=tr)
            w_o = da_w_o
        elif i == 2:
            o = _hgrn2_layer(h, n_ctx, i, hg_w_in[0], hg_lower, hg_norm_g[0])
            w_o = hg_w_o
        else:
            o = _rglru_layer(h, n_ctx, lr_w_in[0], lr_conv_w[0], lr_conv_b[0], lr_w_gate[0], lr_b_gate[0],
                             lr_lambda[0])
            w_o = lr_w_o
        z, h = _proj_ln(o, w_o, 0, z, mod[i], ln_g[i, 0], ln_b[i, 0], mod[i], gate_j=2, mod_j=3, tr=tr,
                        n_ctx=n_ctx, alpha=alpha)
        if last:
            n_ctx = 0
        act = _conv_ffn(h, n_ctx, ffn_w_up, i, ffn_conv_w[i], ffn_conv_b[i], tr=tr)
        z, h = _proj_ln(act, w_down_b, i, z, mod[i], ln_g[i, 1], ln_b[i, 1], mod[min(i + 1, depth - 1)],
                        gate_j=5, mod_j=None if last else 0, tr=tr, n_ctx=n_ctx, alpha=alpha)
    return z
```

```python
import functools
import math

import jax
import jax.numpy as jnp
from jax import lax
from jax.experimental import pallas as pl
from jax.experimental.pallas import tpu as pltpu

F32, BF16 = jnp.float32, jnp.bfloat16

LANES = 128
CHUNK = 64
INV_BASE = 8
LN_EPS = 1e-5
GRID_W = 64
ROPE_BASE = 10000.0
RW_HEAD = 64
RW_DECAY_SCALE = 0.606531
RW_GN_EPS = 64e-5
DA_HEAD = 128
HG_EXPAND = 128
LR_BS = 256
LR_C = 8.0
MIB = 1024 * 1024


def _pick(n, cands):
    for c in cands:
        if n % c == 0:
            return c
    return n


V7X_VMEM_MIB = 64


def _cparams(sem, vmem_mib):
    assert vmem_mib < V7X_VMEM_MIB
    return pltpu.CompilerParams(dimension_semantics=sem, vmem_limit_bytes=vmem_mib * MIB)


def _sigmoid(x):
    return 0.5 * jnp.tanh(0.5 * x) + 0.5


NN = (((1,), (0,)), ((), ()))
NT = (((1,), (1,)), ((), ()))


def _parts(x, n):
    out = []
    for i in range(n):
        p = x.astype(BF16)
        out.append(p)
        if i + 1 < n:
            x = x - p.astype(F32)
    return out


def _mdot(ap, bp, dims=NN, order=2):
    pairs = [(a, b) for i, a in enumerate(ap) for j, b in enumerate(bp) if i + j < order]
    (ca,), (cb,) = dims[0]
    lhs = jnp.concatenate([a for a, _ in pairs], axis=ca) if len(pairs) > 1 else pairs[0][0]
    rhs = jnp.concatenate([b for _, b in pairs], axis=cb) if len(pairs) > 1 else pairs[0][1]
    return lax.dot_general(lhs, rhs, dims, preferred_element_type=F32)


def _cumsum_matrix(t_n, rev):
    r = lax.broadcasted_iota(jnp.int32, (t_n, 3 * t_n), 0)
    c = lax.broadcasted_iota(jnp.int32, (t_n, 3 * t_n), 1) % t_n
    return jnp.where((c >= r) if rev else (c <= r), 1.0, 0.0).astype(BF16)


def _cumsum(tri3_b, x):
    return jnp.dot(tri3_b, jnp.concatenate(_parts(x, 3), axis=0), preferred_element_type=F32)


def _cat_parts(xs, axis):
    return [jnp.concatenate(ps, axis=axis) for ps in zip(*xs)]


def _mm_wres_kernel(a_ref, w_ref, o_ref, wb_ref):
    i = pl.program_id(2)
    k_n = w_ref.shape[0]
    n_kc = 4 if k_n % (4 * LANES) == 0 else 1

    @pl.when(i == 0)
    def _():
        kc = k_n // n_kc
        acc = None
        for c in range(n_kc):
            ks = slice(c * kc, (c + 1) * kc)
            wb = w_ref[ks, :].astype(BF16)
            wb_ref[ks, :] = wb
            t = jnp.dot(a_ref[:, ks], wb, preferred_element_type=F32)
            acc = t if acc is None else acc + t
        o_ref[...] = acc.astype(o_ref.dtype)

    @pl.when(i > 0)
    def _():
        o_ref[...] = jnp.dot(a_ref[...], wb_ref[...], preferred_element_type=F32).astype(o_ref.dtype)


def _matmul(a, w, *, out_dtype=F32, a_off=0):
    g_n, k_n, n_n = w.shape
    m_n = a.shape[1]
    assert w.dtype == F32 and k_n <= 2048, "weight block = all of K for one column block"
    tm = _pick(m_n, (1024, 512, 256, 128, 64))
    tn = _pick(n_n, (1024, 512, 256, 128))
    return pl.pallas_call(
        _mm_wres_kernel,
        grid=(g_n, n_n // tn, m_n // tm),
        in_specs=[pl.BlockSpec((None, tm, k_n), lambda g, j, i: (g + a_off, i, 0)),
                  pl.BlockSpec((None, k_n, tn), lambda g, j, i: (g, 0, j))],
        out_specs=pl.BlockSpec((None, tm, tn), lambda g, j, i: (g, i, j)),
        out_shape=jax.ShapeDtypeStruct((g_n, m_n, n_n), out_dtype),
        scratch_shapes=[pltpu.VMEM((k_n, tn), BF16)],
        compiler_params=_cparams(("parallel", "parallel", "arbitrary"), 48),
        name="matmul_wres",
    )(a, w)


def _mm2(a, w, **kw):
    return _matmul(a[None], w[None], **kw)[0]


def _ada_kernel(c_ref, w_ref, b_ref, o_ref):
    c = c_ref[...]
    s = (c * _sigmoid(c)).astype(BF16)
    o_ref[...] = jnp.dot(s, w_ref[...].astype(BF16), preferred_element_type=F32) + b_ref[...]


def _ada(c8, ada_w, ada_b):
    depth, d, n = ada_w.shape
    tn = _pick(n, (1024, 512, 256, 128))
    return pl.pallas_call(
        _ada_kernel,
        grid=(depth, n // tn),
        in_specs=[pl.BlockSpec((8, d), lambda l, j: (0, 0)),
                  pl.BlockSpec((None, d, tn), lambda l, j: (l, 0, j)),
                  pl.BlockSpec((None, 1, tn), lambda l, j: (l, 0, j))],
        out_specs=pl.BlockSpec((None, 8, tn), lambda l, j: (l, 0, j)),
        out_shape=jax.ShapeDtypeStruct((depth, 8, n), F32),
        compiler_params=_cparams(("parallel", "parallel"), 40),
        name="ada",
    )(c8, ada_w, ada_b.reshape(depth, 1, n))


def _ln_mod_kernel(z_ref, y_ref, mod_ref, g_ref, b_ref, mod2_ref, *out_refs, gate_j, mod_j, alpha):
    m = mod_ref[...]
    zz = alpha * z_ref[...] + y_ref[...] * m[gate_j:gate_j + 1]
    mu = jnp.mean(zz, axis=-1, keepdims=True)
    zc = zz - mu
    var = jnp.mean(zc * zc, axis=-1, keepdims=True)
    zn = zc * lax.rsqrt(var + LN_EPS) * g_ref[...] + b_ref[...]
    out_refs[0][...] = zn
    if mod_j is not None:
        m2 = mod2_ref[...]
        out_refs[1][...] = (zn * (1 + m2[mod_j + 1:mod_j + 2]) + m2[mod_j:mod_j + 1]).astype(BF16)


def _ln_mod(z, y, mod, ln_g, ln_b, mod2, *, gate_j, mod_j, tr, n_ctx, alpha):
    b_n, l_z, d = z.shape
    l_y = y.shape[1]
    z_off = (l_z - l_y) // tr
    ncb = (n_ctx - (l_z - l_y)) // tr
    seg = lambda b, t: (b, jnp.where(t < ncb, 0, 1), 0, 0)
    row = pl.BlockSpec((None, tr, d), lambda b, t: (b, t, 0))
    out_shape = [jax.ShapeDtypeStruct((b_n, l_y, d), F32)]
    out_specs = [row]
    if mod_j is not None:
        out_shape.append(jax.ShapeDtypeStruct((b_n, l_y, d), BF16))
        out_specs.append(row)
    res = pl.pallas_call(
        functools.partial(_ln_mod_kernel, gate_j=gate_j, mod_j=mod_j, alpha=alpha),
        grid=(b_n, l_y // tr),
        in_specs=[pl.BlockSpec((None, tr, d), lambda b, t: (b, t + z_off, 0)),
                  row,
                  pl.BlockSpec((None, None, 6, d), seg),
                  pl.BlockSpec((1, d), lambda b, t: (0, 0)),
                  pl.BlockSpec((1, d), lambda b, t: (0, 0)),
                  pl.BlockSpec((None, None, 6, d), seg)],
        out_specs=out_specs,
        out_shape=out_shape,
        compiler_params=_cparams(("parallel", "parallel"), 40),
        name="ln_mod",
    )(z, y, mod, ln_g.reshape(1, d), ln_b.reshape(1, d), mod2)
    return res if mod_j is not None else (res[0], None)


def _cast_kernel(x_ref, o_ref):
    o_ref[...] = x_ref[...].astype(o_ref.dtype)


def _to_bf16(w):
    g_n, k_n, n_n = w.shape
    tk = _pick(k_n, (512, 256, 128))
    spec = pl.BlockSpec((None, tk, n_n), lambda g, i: (g, i, 0))
    return pl.pallas_call(
        _cast_kernel, grid=(g_n, k_n // tk), in_specs=[spec], out_specs=spec,
        out_shape=jax.ShapeDtypeStruct(w.shape, BF16),
        compiler_params=_cparams(("parallel", "parallel"), 32),
        name="to_bf16",
    )(w)


def _proj_ln_kernel(a_ref, w_ref, z_ref, mod_ref, g_ref, b_ref, mod2_ref, zo_ref, *ho_refs,
                    nk, gate_j, mod_j, alpha, tr, ncb):
    t, k = pl.program_id(1), pl.program_id(2)
    tm = zo_ref.shape[0]
    rb = tm // 2 if tm % 16 == 0 else tm

    def accumulate(first):
        for r0 in range(0, tm, rb):
            p = jnp.dot(a_ref[r0:r0 + rb, :], w_ref[...], preferred_element_type=F32)
            if first:
                zo_ref[r0:r0 + rb, :] = p
            else:
                zo_ref[r0:r0 + rb, :] += p

    rs = math.gcd(tr, 64)

    def finish():
        for sb in range(tm // rs):
            rows = slice(sb * rs, (sb + 1) * rs)
            is_ctx = t * (tm // tr) + (sb * rs) // tr < ncb
            m = jnp.where(is_ctx, mod_ref[0], mod_ref[1])
            zz = alpha * z_ref[rows, :] + zo_ref[rows, :] * m[gate_j:gate_j + 1]
            mu = jnp.mean(zz, axis=-1, keepdims=True)
            zc = zz - mu
            var = jnp.mean(zc * zc, axis=-1, keepdims=True)
            zn = zc * lax.rsqrt(var + LN_EPS) * g_ref[...] + b_ref[...]
            zo_ref[rows, :] = zn
            if mod_j is not None:
                m2 = jnp.where(is_ctx, mod2_ref[0], mod2_ref[1])
                ho_refs[0][rows, :] = (zn * (1 + m2[mod_j + 1:mod_j + 2]) + m2[mod_j:mod_j + 1]).astype(BF16)

    if nk == 1:
        accumulate(True)
        finish()
    else:
        @pl.when(k == 0)
        def _():
            accumulate(True)

        @pl.when((k > 0) & (k < nk - 1))
        def _():
            accumulate(False)

        @pl.when(k == nk - 1)
        def _():
            accumulate(False)
            finish()


def _proj_ln(a, w, g, z, mod, ln_g, ln_b, mod2, *, gate_j, mod_j, tr, n_ctx, alpha):
    b_n, l_a, k_n = a.shape
    d = w.shape[2]
    if z.shape[1] != l_a:
        y = _matmul(a.reshape(1, b_n * l_a, k_n), w[g:g + 1])[0].reshape(b_n, l_a, d)
        return _ln_mod(z, y, mod, ln_g, ln_b, mod2, gate_j=gate_j, mod_j=mod_j, tr=tr, n_ctx=n_ctx, alpha=alpha)
    if w.dtype != BF16:
        w = _to_bf16(w)
    tm = tr * _pick(l_a // tr, (3, 2, 1))
    tk = k_n if k_n <= 2048 else _pick(k_n, (1408, 1024, 512, 256, 128))
    nk = k_n // tk
    row = pl.BlockSpec((None, tm, d), lambda b, t, k: (b, t, 0))
    seg = pl.BlockSpec((None, 2, 6, d), lambda b, t, k: (b, 0, 0, 0))
    vec = pl.BlockSpec((1, d), lambda b, t, k: (0, 0))
    out_shape = [jax.ShapeDtypeStruct((b_n, l_a, d), F32)]
    out_specs = [row]
    if mod_j is not None:
        out_shape.append(jax.ShapeDtypeStruct((b_n, l_a, d), BF16))
        out_specs.append(row)
    res = pl.pallas_call(
        functools.partial(_proj_ln_kernel, nk=nk, gate_j=gate_j, mod_j=mod_j, alpha=alpha, tr=tr,
                          ncb=n_ctx // tr),
        grid=(b_n, l_a // tm, nk),
        in_specs=[pl.BlockSpec((None, tm, tk), lambda b, t, k: (b, t, k)),
                  pl.BlockSpec((None, tk, d), lambda b, t, k: (g, k, 0),
                               pipeline_mode=pl.Buffered(1) if nk == 1 else None),
                  row, seg, vec, vec, seg],
        out_specs=out_specs,
        out_shape=out_shape,
        compiler_params=_cparams(("parallel", "parallel", "arbitrary"), 56),
        name="proj_ln",
    )(a, w, z, mod, ln_g.reshape(1, d), ln_b.reshape(1, d), mod2)
    return res if mod_j is not None else (res[0], None)


def _seg_shift(x, row, shift, n_ctx):
    l_n = x.shape[0]
    rolled = pltpu.roll(x, (-shift) % l_n, 0)
    src = row + shift
    same_seg = (src >= 0) & (src < l_n) & ((src < n_ctx) == (row < n_ctx))
    return jnp.where(same_seg, rolled, 0.0)


def _rw_mix_kernel(z_ref, mod_ref, mu_ref, o_ref, *, n_ctx):
    z = z_ref[...]
    row = lax.broadcasted_iota(jnp.int32, z.shape, 0)
    is_ctx = row < n_ctx
    shift = jnp.where(is_ctx, mod_ref[0, 0:1, :], mod_ref[1, 0:1, :])
    scale = jnp.where(is_ctx, mod_ref[0, 1:2, :], mod_ref[1, 1:2, :])
    h = z * (1 + scale) + shift
    dx = 0.5 * (_seg_shift(h, row, -1, n_ctx) + _seg_shift(h, row, 1, n_ctx)) - h
    for n in range(6):
        o_ref[n] = (h + dx * mu_ref[n:n + 1, :]).astype(BF16)


def _rw_mix(z, mod, mu, *, n_ctx):
    b_n, l_n, d = z.shape
    tc = _pick(d, (256, 128))
    return pl.pallas_call(
        functools.partial(_rw_mix_kernel, n_ctx=n_ctx),
        grid=(b_n, d // tc),
        in_specs=[pl.BlockSpec((None, l_n, tc), lambda b, j: (b, 0, j)),
                  pl.BlockSpec((None, 2, 6, tc), lambda b, j: (b, 0, 0, j)),
                  pl.BlockSpec((6, tc), lambda b, j: (0, j))],
        out_specs=pl.BlockSpec((6, None, l_n, tc), lambda b, j: (0, b, 0, j)),
        out_shape=jax.ShapeDtypeStruct((6, b_n, l_n, d), BF16),
        compiler_params=_cparams(("parallel", "parallel"), 48),
        name="rw_mix",
    )(z, mod, mu)


def _lora_kernel(x_ref, a_ref, b_ref, o_ref, *, act):
    tm = x_ref.shape[0]
    rb = tm // 4 if tm % 64 == 0 else tm
    ts = [jnp.dot(x_ref[r0:r0 + rb, :], a_ref[...], preferred_element_type=F32) for r0 in range(0, tm, rb)]
    for i, t in enumerate(ts):
        if act == "tanh":
            t = jnp.tanh(t)
        elif act == "sigmoid":
            t = _sigmoid(t)
        o_ref[i * rb:(i + 1) * rb, :] = jnp.dot(t.astype(BF16), b_ref[...], preferred_element_type=F32)


def _lora(xs, x_idx, a, b, act):
    g_n, d, r = a.shape
    m_n = xs.shape[1]
    tm = _pick(m_n, (512, 256, 128, 64))
    return pl.pallas_call(
        functools.partial(_lora_kernel, act=act),
        grid=(g_n, m_n // tm),
        in_specs=[pl.BlockSpec((None, tm, d), lambda g, i: (x_idx, i, 0)),
                  pl.BlockSpec((None, d, r), lambda g, i: (g, 0, 0)),
                  pl.BlockSpec((None, r, d), lambda g, i: (g, 0, 0))],
        out_specs=pl.BlockSpec((None, tm, d), lambda g, i: (g, i, 0)),
        out_shape=jax.ShapeDtypeStruct((g_n, m_n, d), F32),
        compiler_params=_cparams(("parallel", "parallel"), 40),
        name="lora",
    )(xs, a, b)


def _chunk_of(q, ncc, nc, rev):
    if not rev:
        return q
    return jnp.where(q < ncc, ncc - 1 - q, nc - 1 - (q - ncc))


def _rwkv_kernel(r_ref, k_ref, v_ref, lw_ref, la_ref, g_ref, w0_ref, a0_ref, kk_ref, ka_ref, rk_ref,
                 gng_ref, gnb_ref, o_ref,
                 y_scr, mr_s, n_s, *, n_ctx):
    t_n = CHUNK
    h2 = 2 * t_n
    l_n = r_ref.shape[0]
    nc, ncc = l_n // t_n, n_ctx // t_n
    group = _pick(nc, (6, 4, 3, 2, 1))
    lane = lax.broadcasted_iota(jnp.int32, (1, LANES), 1)
    m1 = jnp.where(lane < RW_HEAD, 1.0, 0.0)
    m2 = 1.0 - m1
    ri = lax.broadcasted_iota(jnp.int32, (LANES, LANES), 0)
    ci = lax.broadcasted_iota(jnp.int32, (LANES, LANES), 1)
    same_head = (ri // RW_HEAD) == (ci // RW_HEAD)
    gsum_b = jnp.where(same_head, 1.0, 0.0).astype(BF16)
    gavg_b = jnp.where(same_head, 1.0 / RW_HEAD, 0.0).astype(BF16)
    eye = jnp.where(ri == ci, 1.0, 0.0)
    tr_i, tc_i = ri % t_n, ci % t_n
    blk_base = (ri // INV_BASE) == (ci // INV_BASE)
    sizes = [INV_BASE * 2 ** i for i in range(1, 8) if INV_BASE * 2 ** i <= t_n]
    blk_sibling = [((ri // b) == (ci // b)) & ((ri // (b // 2)) != (ci // (b // 2))) for b in sizes]
    k_k, k_a = kk_ref[...], ka_ref[...]

    def stack(x):
        return jnp.concatenate([x * m1, x * m2], axis=0)

    def rows_of(c):
        return pl.ds(pl.multiple_of(c * t_n, t_n), t_n)

    def head_sum(x, w_b):
        return _mdot(_parts(x, 2), [w_b], order=2)

    tri3_b = [_cumsum_matrix(t_n, rev) for rev in (False, True)]
    strict = [(tc_i > tr_i) if rev else (tc_i < tr_i) for rev in (False, True)]
    incl = [(tc_i >= tr_i) if rev else (tc_i <= tr_i) for rev in (False, True)]

    def stage_prep(d, c):
        rows = rows_of(c)
        k, r, v = k_ref[rows, :], r_ref[rows, :], v_ref[rows, :]
        kkr = k * k_k
        lw = -RW_DECAY_SCALE * _sigmoid(w0_ref[d:d + 1, :] + lw_ref[d, rows, :])
        a = _sigmoid(a0_ref[d:d + 1, :] + la_ref[d, rows, :])
        return dict(d=d, c=c, k=k, r=r, v=v, lw=lw, a=a, kkr=kkr, ss=head_sum(kkr * kkr, gsum_b),
                    cum=_cumsum(tri3_b[d], lw))

    def stage_amat(s):
        d, cum, lw, a = s["d"], s["cum"], s["lw"], s["a"]
        kk = s["kkr"] * lax.rsqrt(s["ss"] + 1e-12)
        kd = s["k"] * (1 + (a - 1) * k_a)
        bv = kk * a
        p_end = cum[0:1, :] if d == 1 else cum[t_n - 1:t_n, :]
        e_m = jnp.exp(-cum)
        e_h = jnp.exp(p_end - cum)
        ktp = _parts(stack(kk * jnp.exp(cum - lw)), 2)
        rt = stack(s["r"] * jnp.exp(cum))
        k2p = _cat_parts([_parts(stack(bv * e_m), 1), _parts(stack(kd * e_m), 1)], 0)
        return dict(d=d, c=s["c"], ktp=ktp, rt=rt, vp=_parts(stack(s["v"]), 1), p_end=p_end,
                    bh=stack(bv * e_h), kh=stack(kd * e_h),
                    amat=_mdot(_cat_parts([ktp, _parts(rt, 2)], 0), k2p, NT))

    def stage_square(s):
        d, amat = s["d"], s["amat"]
        lt = jnp.where(strict[d], amat[:h2, :h2], 0.0).T
        ldt = jnp.where(blk_base, lt, 0.0)
        ltp = _parts(ldt, 2)
        msk = jnp.concatenate([jnp.where(strict[d], amat[:h2, h2:], 0.0),
                               jnp.where(incl[d], amat[h2:, h2:], 0.0)], axis=0)
        s = dict(s, pt=eye - ldt, xt=_mdot(ltp, ltp[:1]),
                 ct=[jnp.where(m, lt, 0.0).astype(BF16) for m in blk_sibling],
                 av=_mdot(_parts(msk, 2), s["vp"][:1]),
                 arbp=_parts(jnp.where(incl[d], amat[h2:, :h2], 0.0), 2))
        del s["amat"]
        return s

    def stage_merge_a(s, level):
        return dict(s, t1=_mdot([s["ct"][level]], _parts(s["pt"], 2)).astype(BF16))

    def stage_merge_b(s):
        return dict(s, pt=s["pt"] - _mdot(_parts(s["pt"], 2), [s["t1"]]))

    def stage_double(s, final):
        xh = _parts(s["xt"], 1)
        ptp = _parts(s["pt"], 2)
        if final:
            return dict(s, pt=s["pt"] + _mdot(xh, ptp))
        rhs = [jnp.concatenate([ptp[0], xh[0]], axis=1), jnp.concatenate([ptp[1], jnp.zeros_like(xh[0])], axis=1)]
        both = _mdot(xh, rhs)
        return dict(s, pt=s["pt"] + both[:, :LANES], xt=both[:, LANES:])

    def stage_solve(s):
        rhs = jnp.concatenate([s["ktp"][0], (-s["av"][:h2]).astype(BF16)], axis=1)
        return dict(s, wub=_mdot(_parts(s["pt"].T, 2), [rhs]).astype(BF16))

    def stage_fold(s):
        d, c, wub = s["d"], s["c"], s["wub"]
        aw = _mdot(s["arbp"], [wub])
        zb = jnp.zeros((h2, LANES), BF16)
        lhs = _cat_parts([_parts(s["bh"].T, 2), _parts(s["kh"].T, 2)], 1)
        rhs = jnp.concatenate([wub, jnp.concatenate([zb, s["vp"][0]], axis=1)], axis=0)
        mn = _mdot(lhs, [rhs])
        dg = jnp.where(ri == ci, jnp.broadcast_to(jnp.exp(s["p_end"]), (LANES, LANES)), 0.0)
        rp = s["rt"] - aw[:, :LANES]
        mrp = _parts(jnp.concatenate([dg - mn[:, :LANES], rp[:t_n] + rp[t_n:]], axis=0), 2)
        for i in range(2):
            mr_s[d, c, i] = mrp[i]
        n_s[d, c] = mn[:, LANES:]
        y0 = s["av"][h2:] + aw[:, LANES:]
        return y0[:t_n] + y0[t_n:]

    def seq(q, hs):
        cs = (q, _chunk_of(q, ncc, nc, True))
        mh = [_mdot([mr_s[d, cs[d], 0], mr_s[d, cs[d], 1]], _parts(hs[d], 2)) for d in (0, 1)]
        for d in (0, 1):
            y_scr[rows_of(cs[d]), :] += mh[d][h2:]
        return tuple(mh[d][:h2] + n_s[d, cs[d]] for d in (0, 1))

    def local(i, hs, with_seq):
        steps = [i * group + g for g in range(group)]
        pending = [q - group for q in steps] if with_seq else []
        sts = [stage_prep(d, q if d == 0 else _chunk_of(q, ncc, nc, True)) for q in steps for d in (0, 1)]
        n_dbl = INV_BASE.bit_length() - 2
        stages = [stage_amat, stage_square]
        stages += [functools.partial(stage_double, final=i == n_dbl - 1) for i in range(n_dbl)]
        for level in range(len(blk_sibling)):
            stages += [functools.partial(stage_merge_a, level=level), stage_merge_b]
        stages += [stage_solve]
        for stage in stages:
            sts = [stage(s) for s in sts]
            if pending:
                hs = seq(pending.pop(0), hs)
        for s in sts:
            y0 = stage_fold(s)
            y_scr[rows_of(s["c"]), :] += y0
        while pending:
            hs = seq(pending.pop(0), hs)
        return hs

    y_scr[...] = jnp.zeros(y_scr.shape, F32)
    zero = jnp.zeros((LANES, LANES), F32)
    n_trip = nc // group
    hs = local(0, (zero, zero), False)
    hs = lax.fori_loop(1, n_trip, functools.partial(local, with_seq=True), hs)
    for q in range((n_trip - 1) * group, nc):
        hs = seq(q, hs)

    n_post = _pick(nc, (9, 6, 4, 3, 2, 1))

    def post(i, carry):
        rows = [rows_of(i * n_post + g) for g in range(n_post)]

        def bonus_sum(rw):
            k, r = k_ref[rw, :], r_ref[rw, :]
            kd_f = k * (1 + (_sigmoid(a0_ref[0:1, :] + la_ref[0, rw, :]) - 1) * k_a)
            kd_b = k * (1 + (_sigmoid(a0_ref[1:2, :] + la_ref[1, rw, :]) - 1) * k_a)
            return head_sum(r * (kd_f + kd_b) * rk_ref[...], gsum_b)
        bsum = [bonus_sum(rw) for rw in rows]
        ys = [y_scr[rw, :] for rw in rows]
        ycs = [y - m for y, m in zip(ys, [head_sum(y, gavg_b) for y in ys])]
        var = [head_sum(yc * yc, gavg_b) for yc in ycs]
        for rw, yc, vr, bs in zip(rows, ycs, var, bsum):
            yn = yc * lax.rsqrt(vr + RW_GN_EPS) * gng_ref[...] + gnb_ref[...]
            o_ref[rw, :] = ((yn + bs * v_ref[rw, :]) * g_ref[rw, :]).astype(BF16)
        return carry

    lax.fori_loop(0, nc // n_post, post, 0)


def _rwkv_scan(rkv, lw, la, g, w0, a0, k_k, k_a, r_k, gn_g, gn_b, *, n_ctx):
    _, b_n, l_n, d = rkv.shape
    nc = l_n // CHUNK
    col = lambda n: pl.BlockSpec((None, None, l_n, LANES), lambda b, p, n=n: (n, b, 0, p))
    two = pl.BlockSpec((2, None, l_n, LANES), lambda b, p: (0, b, 0, p))
    par = lambda rows: pl.BlockSpec((rows, LANES), lambda b, p: (0, p))
    return pl.pallas_call(
        functools.partial(_rwkv_kernel, n_ctx=n_ctx),
        grid=(b_n, d // LANES),
        in_specs=[col(0), col(1), col(2), two, two,
                  pl.BlockSpec((None, l_n, LANES), lambda b, p: (b, 0, p)),
                  par(2), par(2), par(1), par(1), par(1), par(1), par(1)],
        out_specs=pl.BlockSpec((None, l_n, LANES), lambda b, p: (b, 0, p)),
        out_shape=jax.ShapeDtypeStruct((b_n, l_n, d), BF16),
        scratch_shapes=[pltpu.VMEM((l_n, LANES), F32),
                        pltpu.VMEM((2, nc, 2, LANES + CHUNK, LANES), BF16),
                        pltpu.VMEM((2, nc, LANES, LANES), F32)],
        compiler_params=_cparams(("parallel", "parallel"), 56),
        name="rwkv_scan",
    )(rkv, rkv, rkv, lw, la, g, w0, a0, k_k.reshape(1, d), k_a.reshape(1, d), r_k.reshape(1, d),
      gn_g.reshape(1, d), gn_b.reshape(1, d))


def _pad_axis(w, axis, to):
    pad = [(0, 0)] * w.ndim
    pad[axis] = (0, to - w.shape[axis])
    return jnp.pad(w, pad)


def _rwkv7_layer(z, mod, n_ctx, mu, w_rkv, w0, w1, w2, a0, a1, a2, g1, g2, k_k, k_a, r_k, gn_g, gn_b):
    b_n, l_n, d = z.shape
    m_n = b_n * l_n
    xs = _rw_mix(z, mod, mu, n_ctx=n_ctx).reshape(6, m_n, d)
    rkv = _matmul(xs, w_rkv)
    r_w = -(-w1.shape[-1] // LANES) * LANES
    r_a = -(-a1.shape[-1] // LANES) * LANES
    lw = _lora(xs, 3, _pad_axis(w1, 2, r_w).astype(BF16), _pad_axis(w2, 1, r_w).astype(BF16), "tanh")
    la = _lora(xs, 4, _pad_axis(a1, 2, r_a).astype(BF16), _pad_axis(a2, 1, r_a).astype(BF16), None)
    gate = _lora(xs, 5, g1[None].astype(BF16), g2[None].astype(BF16), "sigmoid")
    o = _rwkv_scan(rkv.reshape(3, b_n, l_n, d), lw.reshape(2, b_n, l_n, d), la.reshape(2, b_n, l_n, d),
                   gate.reshape(b_n, l_n, d), w0, a0, k_k, k_a, r_k, gn_g, gn_b, n_ctx=n_ctx)
    return o


def _qkv_rope_kernel(a_ref, w_ref, cos_ref, sa_ref, sb_ref, o_ref, wb_ref, *, n_q, n_qk):
    j, i = pl.program_id(0), pl.program_id(1)

    @pl.when(i == 0)
    def _():
        wb_ref[...] = w_ref[...].astype(BF16)

    tn = o_ref.shape[1]
    sw = math.gcd(tn, 2 * DA_HEAD)

    def product(s):
        return jnp.dot(a_ref[...], wb_ref[:, s * sw:(s + 1) * sw], preferred_element_type=F32)

    @pl.when(j < n_qk)
    def _():
        q = DA_HEAD // 4
        q_scale = jnp.where(j < n_q, DA_HEAD ** -0.5 * math.log2(math.e), 1.0)
        cos, s_a, s_b = cos_ref[...] * q_scale, sa_ref[...] * q_scale, sb_ref[...] * q_scale

        def rotary(s, x):
            for c in range(0, sw, DA_HEAD):
                xs = x[:, c:c + DA_HEAD]
                rot = xs * cos + pltpu.roll(xs, DA_HEAD - q, 1) * s_a + pltpu.roll(xs, q, 1) * s_b
                o_ref[:, s * sw + c:s * sw + c + DA_HEAD] = rot.astype(BF16)
        x_prev = product(0)
        for s in range(1, tn // sw):
            x_next = product(s)
            rotary(s - 1, x_prev)
            x_prev = x_next
        rotary(tn // sw - 1, x_prev)

    @pl.when(j >= n_qk)
    def _():
        for s in range(tn // sw):
            o_ref[:, s * sw:(s + 1) * sw] = product(s).astype(BF16)


def _qkv_rope(h, w_qkv, cos, s_a, s_b):
    b_n, l_n, d = h.shape
    d3 = w_qkv.shape[1]
    tm = _pick(l_n, (768, 1024, 512, 256, 128, 64))
    tn = _pick(d, (1024, 512, 256, 128))
    n_t = l_n // tm
    tab = pl.BlockSpec((tm, DA_HEAD), lambda j, i: (i % n_t, 0))
    out = pl.pallas_call(
        functools.partial(_qkv_rope_kernel, n_q=d // tn, n_qk=2 * d // tn),
        grid=(d3 // tn, b_n * n_t),
        in_specs=[pl.BlockSpec((tm, d), lambda j, i: (i, 0)),
                  pl.BlockSpec((d, tn), lambda j, i: (0, j)), tab, tab, tab],
        out_specs=pl.BlockSpec((tm, tn), lambda j, i: (i, j)),
        out_shape=jax.ShapeDtypeStruct((b_n * l_n, d3), BF16),
        scratch_shapes=[pltpu.VMEM((d, tn), BF16)],
        compiler_params=_cparams(("parallel", "arbitrary"), 48),
        name="qkv_rope",
    )(h.reshape(b_n * l_n, d), w_qkv, cos, s_a, s_b)
    return out.reshape(b_n, l_n, d3)


def _attn_kernel(q_ref, k_ref, v_ref, lam_ref, sg_ref, o_ref, *, ncb, n_ctx, lam_init):
    qi = pl.program_id(2)
    lv = lam_ref[...]
    lam = (jnp.exp(jnp.sum(lv[0:1] * lv[1:2], axis=-1, keepdims=True))
           - jnp.exp(jnp.sum(lv[2:3] * lv[3:4], axis=-1, keepdims=True)) + lam_init)

    hw = 2 * DA_HEAD
    n_hh = o_ref.shape[1] // hw

    def attend(nk):
        def scores(hh, m):
            cols = slice(hh * hw + m * DA_HEAD, hh * hw + (m + 1) * DA_HEAD)
            return lax.dot_general(q_ref[:, cols], k_ref[0:nk, cols], NT, preferred_element_type=F32)
        s_all = [[scores(hh, m) for m in (0, 1)] for hh in range(n_hh)]
        for hh in range(n_hh):
            def probs(s):
                e = jnp.exp2(s - jnp.max(s, axis=-1, keepdims=True))
                return e, 1.0 / jnp.sum(e, axis=-1, keepdims=True)
            e0, i0 = probs(s_all[hh][0])
            e1, i1 = probs(s_all[hh][1])
            v = v_ref[0:nk, hh * hw:(hh + 1) * hw]
            o = (jnp.dot(e0.astype(BF16), v, preferred_element_type=F32) * i0
                 - jnp.dot(e1.astype(BF16), v, preferred_element_type=F32) * (lam * i1))
            o = o * lax.rsqrt(jnp.mean(o * o, axis=-1, keepdims=True) + 1e-5) * sg_ref[...] * (1 - lam_init)
            o_ref[:, hh * hw:(hh + 1) * hw] = o.astype(BF16)

    if ncb > 0:
        @pl.when(qi < ncb)
        def _():
            attend(n_ctx)

    @pl.when(qi >= ncb)
    def _():
        attend(k_ref.shape[0])


def _attention(qkv, lam_vec, sub_g, *, tq, n_ctx, lam_init):
    b_n, l_n, d3 = qkv.shape
    d = d3 // 3
    hw = 2 * DA_HEAD
    n_hh = 2 if (d // hw) % 2 == 0 else 1
    bw = n_hh * hw
    nh = d // bw
    return pl.pallas_call(
        functools.partial(_attn_kernel, ncb=n_ctx // tq, n_ctx=n_ctx, lam_init=lam_init),
        grid=(b_n, nh, l_n // tq),
        in_specs=[pl.BlockSpec((None, tq, bw), lambda b, h, t: (b, t, h)),
                  pl.BlockSpec((None, l_n, bw), lambda b, h, t: (b, 0, nh + h)),
                  pl.BlockSpec((None, l_n, bw), lambda b, h, t: (b, 0, 2 * nh + h)),
                  pl.BlockSpec((4, DA_HEAD), lambda b, h, t: (0, 0)),
                  pl.BlockSpec((1, hw), lambda b, h, t: (0, 0))],
        out_specs=pl.BlockSpec((None, tq, bw), lambda b, h, t: (b, t, h)),
        out_shape=jax.ShapeDtypeStruct((b_n, l_n, d), BF16),
        compiler_params=_cparams(("parallel", "parallel", "arbitrary"), 48),
        name="diff_attn",
    )(qkv, qkv, qkv, lam_vec, sub_g.reshape(1, hw))


def _rope_tables(n_ctx, n_lat):
    n_rows = n_lat // GRID_W
    row = jnp.repeat(jnp.arange(n_rows, dtype=F32), GRID_W)
    col = jnp.tile(jnp.arange(GRID_W, dtype=F32), n_rows)
    nf = DA_HEAD // 4
    inv_freq = ROPE_BASE ** (-jnp.arange(nf, dtype=F32) / nf)
    ang_r, ang_c = row[:, None] * inv_freq, col[:, None] * inv_freq
    ang = jnp.concatenate([ang_r, ang_r, ang_c, ang_c], axis=-1)
    ang = jnp.concatenate([jnp.zeros((n_ctx, DA_HEAD), F32), ang], axis=0)
    cos, sin = jnp.cos(ang), jnp.sin(ang)
    even_q = (jnp.arange(DA_HEAD) // nf) % 2 == 0
    return cos, jnp.where(even_q, -sin, 0.0), jnp.where(even_q, 0.0, sin)


def _diff_attention_layer(h, n_ctx, layer_idx, w_qkv, lam_vec, sub_g, *, tr):
    b_n, l_n, d = h.shape
    cos, s_a, s_b = _rope_tables(n_ctx, l_n - n_ctx)
    qkv = _qkv_rope(h, w_qkv, cos, s_a, s_b)
    lam_init = 0.8 - 0.6 * math.exp(-0.3 * layer_idx)
    o = _attention(qkv, lam_vec, sub_g, tq=tr, n_ctx=n_ctx, lam_init=lam_init)
    return o


def _hgrn_kernel(q_ref, i_ref, g_ref, ff_ref, fb_ref, low_ref, ng_ref, o_ref, o_scr, *, n_ctx, layer_idx):
    t_n = CHUNK
    l_n = q_ref.shape[0]
    nc, ncc = l_n // t_n, n_ctx // t_n
    r64 = lax.broadcasted_iota(jnp.int32, (t_n, t_n), 0)
    c64 = lax.broadcasted_iota(jnp.int32, (t_n, t_n), 1)

    def rows_of(c):
        return pl.ds(pl.multiple_of(c * t_n, t_n), t_n)

    f_refs = (ff_ref, fb_ref)
    lbs, incl = [], []
    tri3_b = [_cumsum_matrix(t_n, rev) for rev in (False, True)]
    for d in (0, 1):
        low = low_ref[d]
        e = jnp.exp(low - jnp.max(low, axis=0, keepdims=True))
        sm = e / jnp.sum(e, axis=0, keepdims=True)
        cs = sm[0:1]
        for rr in range(1, layer_idx + 1):
            cs = cs + sm[rr:rr + 1]
        lbs.append(cs - sm[0:1])
        incl.append((c64 >= r64) if d == 1 else (c64 <= r64))
    group = _pick(nc, (6, 4, 2, 1))

    def stage_cum(d, c):
        rows = rows_of(c)
        f = lbs[d] + (1.0 - lbs[d]) * _sigmoid(f_refs[d][rows, :])
        return dict(d=d, rows=rows, f=f, cum=_cumsum(tri3_b[d], jnp.log(f)))

    def stage_att(s):
        d, cum, rows = s["d"], s["cum"], s["rows"]
        b_end = cum[0:1, :] if d == 1 else cum[t_n - 1:t_n, :]
        qv = q_ref[rows, :]
        qd = (qv * _sigmoid(qv) * jnp.exp(cum)).astype(BF16)
        kk = 1.0 - s["f"]
        v = i_ref[rows, :]
        kd = (kk * jnp.exp(-cum)).astype(BF16)
        ke = (kk * jnp.exp(b_end - cum)).astype(BF16)
        return dict(d=d, rows=rows, qd=qd, vb=v.astype(BF16), dec=jnp.exp(b_end),
                    att=lax.dot_general(qd, kd, NT, preferred_element_type=F32),
                    upd=jnp.dot(v.T.astype(BF16), ke, preferred_element_type=F32))

    def stage_intra(s):
        att = jnp.where(incl[s["d"]], s["att"], 0.0).astype(BF16)
        return dict(s, o=jnp.dot(att, s["vb"], preferred_element_type=F32))

    def body(i, states):
        items = [(d, _chunk_of(i * group + g, ncc, nc, d == 1)) for g in range(group) for d in (0, 1)]
        sts = [stage_cum(d, c) for d, c in items]
        sts = [stage_att(s) for s in sts]
        sts = [stage_intra(s) for s in sts]
        states = list(states)
        for s in sts:
            d = s["d"]
            o = s["o"] + lax.dot_general(s["qd"], states[d].astype(BF16), NT, preferred_element_type=F32)
            o_scr[d, s["rows"], :] = o
            states[d] = states[d] * s["dec"] + s["upd"]
        return tuple(states)

    zero = jnp.zeros((LANES, LANES), F32)
    lax.fori_loop(0, nc // group, body, (zero, zero))

    p_n = t_n * _pick(nc, (4, 3, 2, 1))

    def post(c, carry):
        rows = pl.ds(pl.multiple_of(c * p_n, p_n), p_n)
        o = o_scr[0, rows, :] + o_scr[1, rows, :]
        o = o * lax.rsqrt(jnp.mean(o * o, axis=-1, keepdims=True) + 1e-5) * ng_ref[...]
        gv = g_ref[rows, :]
        o_ref[rows, :] = (o * (gv * _sigmoid(gv))).astype(BF16)
        return carry

    lax.fori_loop(0, l_n // p_n, post, 0)


def _hgrn2_layer(h, n_ctx, layer_idx, w_in, lower, norm_g):
    b_n, l_n, d = h.shape
    m_n = b_n * l_n
    nh = d // HG_EXPAND
    proj = _mm2(h.reshape(m_n, d), w_in).reshape(b_n, l_n, 5 * d)
    col = lambda n: pl.BlockSpec((None, l_n, LANES), lambda b, p, n=n: (b, 0, n * nh + p))
    o = pl.pallas_call(
        functools.partial(_hgrn_kernel, n_ctx=n_ctx, layer_idx=layer_idx),
        grid=(b_n, nh),
        in_specs=[col(0), col(1), col(2), col(3), col(4),
                  pl.BlockSpec((2, lower.shape[1], LANES), lambda b, p: (0, 0, p)),
                  pl.BlockSpec((1, LANES), lambda b, p: (0, 0))],
        out_specs=pl.BlockSpec((None, l_n, LANES), lambda b, p: (b, 0, p)),
        out_shape=jax.ShapeDtypeStruct((b_n, l_n, d), BF16),
        scratch_shapes=[pltpu.VMEM((2, l_n, LANES), F32)],
        compiler_params=_cparams(("parallel", "parallel"), 40),
        name="hgrn_scan",
    )(proj, proj, proj, proj, proj, lower, norm_g.reshape(1, LANES))
    return o


def _gelu_tanh(x):
    return 0.5 * x * (1.0 + jnp.tanh(math.sqrt(2.0 / math.pi) * (x + 0.044715 * (x * x * x))))


def _softplus(x):
    return jnp.maximum(x, 0.0) + jnp.log1p(jnp.exp(-jnp.abs(x)))


SEG_PAD = 8


def _lin_scan(a_ref, u_ref, hl_s, cp_s, h_s, base, row0, n, h_in):
    seg = n // 8
    stride = seg + SEG_PAD
    n_p = a_ref.shape[1]
    chains = [(d, j) for d in (0, 1) for j in range(n_p)]

    def step(i, carry):
        out = []
        for (d, j), (hl, cp) in zip(chains, carry):
            idx = pl.ds(base + (i if d == 0 else seg - 1 - i), 8, stride=stride)
            a = a_ref[d, j, idx, :]
            hl = a * hl + u_ref[d, j, idx, :]
            cp = a * cp
            hl_s[d, j, idx, :] = hl
            cp_s[d, j, idx, :] = cp
            out.append((hl, cp))
        return tuple(out)

    init = tuple((jnp.zeros((8, LANES), F32), jnp.ones((8, LANES), F32)) for _ in chains)
    ends = lax.fori_loop(0, seg, step, init)
    h_out = [[None] * n_p, [None] * n_p]
    for (d, j), (hl_e, cp_e) in zip(chains, ends):
        carry = h_in[d][j]
        for s in (range(8) if d == 0 else range(7, -1, -1)):
            r0, p0 = row0 + s * seg, base + s * stride
            blk = hl_s[d, j, p0:p0 + seg, :] + cp_s[d, j, p0:p0 + seg, :] * carry
            if d == 0:
                h_s[j, r0:r0 + seg, :] = blk
            else:
                h_s[j, r0:r0 + seg, :] += blk
            carry = hl_e[s:s + 1, :] + cp_e[s:s + 1, :] * carry
        h_out[d][j] = carry
    return h_out


def _rglru_kernel(gb_ref, xb_ref, cw_ref, cb_ref, wg_ref, bg_ref, lam_ref, o_ref, a_s, u_s, h_s, hl_s, cp_s, *,
                  n_ctx):
    l_n, w_n = xb_ref.shape
    n_lat = l_n - n_ctx
    k_w = cw_ref.shape[0]
    half = (k_w - 1) // 2
    n_p = w_n // LANES
    tb = math.gcd(math.gcd(n_ctx, n_lat) if n_ctx else n_lat, 256)
    rowb = lax.broadcasted_iota(jnp.int32, (tb, w_n), 0)
    wgb = [[wg_ref[d, g].astype(BF16) for g in (0, 1)] for d in (0, 1)]
    sp = [_softplus(-lam_ref[d:d + 1, :]) for d in (0, 1)]
    base_lat = 8 * (n_ctx // 8 + SEG_PAD) if n_ctx else 0
    for r0 in range(0, l_n, tb):
        lo, hi = (0, n_ctx) if r0 < n_ctx else (n_ctx, l_n)
        acc = None
        for j in range(k_w):
            sh = j - half
            if r0 + sh < 0 or r0 + sh + tb > l_n:
                xj = pltpu.roll(xb_ref[r0:r0 + tb, :], (-sh) % tb, 0)
            else:
                xj = xb_ref[r0 + sh:r0 + sh + tb, :]
            if r0 + sh < lo:
                xj = jnp.where(rowb < lo - r0 - sh, 0.0, xj)
            if r0 + sh + tb > hi:
                xj = jnp.where(rowb >= hi - r0 - sh, 0.0, xj)
            t = xj * cw_ref[j:j + 1, :]
            acc = t if acc is None else acc + t
        xc = cb_ref[...] + acc
        xcb = xc.astype(BF16)
        row0, n, base = (0, n_ctx, 0) if r0 < n_ctx else (n_ctx, n_lat, base_lat)
        seg = n // 8
        pc = math.gcd(seg, tb)
        for d in (0, 1):
            gate = lambda g: _sigmoid(jnp.dot(xcb, wgb[d][g], preferred_element_type=F32) + bg_ref[d, g:g + 1, :])
            log_a = -LR_C * gate(0) * sp[d]
            a = jnp.exp(log_a)
            u = jnp.sqrt(jnp.tanh(-log_a) * (a * a + 1.0)) * gate(1) * xc
            for p0 in range(0, tb, pc):
                s, off = divmod(r0 + p0 - row0, seg)
                dst = slice(base + s * (seg + SEG_PAD) + off, base + s * (seg + SEG_PAD) + off + pc)
                for j in range(n_p):
                    a_s[d, j, dst, :] = a[p0:p0 + pc, j * LANES:(j + 1) * LANES]
                    u_s[d, j, dst, :] = u[p0:p0 + pc, j * LANES:(j + 1) * LANES]
    h = [[jnp.zeros((1, LANES), F32)] * n_p] * 2
    base = 0
    for row0, n in ((0, n_ctx), (n_ctx, n_lat)):
        if n:
            h = _lin_scan(a_s, u_s, hl_s, cp_s, h_s, base, row0, n, h)
            base += 8 * (n // 8 + SEG_PAD)
    for r0 in range(n_ctx, l_n, tb):
        for j in range(n_p):
            cols = slice(j * LANES, (j + 1) * LANES)
            o_ref[r0 - n_ctx:r0 - n_ctx + tb, cols] = (h_s[j, r0:r0 + tb, :]
                                                       * _gelu_tanh(gb_ref[r0:r0 + tb, cols])).astype(BF16)


def _rglru_layer(h, n_ctx, w_in, conv_w, conv_b, w_gate, b_gate, lam):
    b_n, l_n, d = h.shape
    n_lat = l_n - n_ctx
    nb = d // LR_BS
    proj = _mm2(h.reshape(b_n * l_n, d), w_in).reshape(b_n, l_n, 2 * d)
    k_w = conv_w.shape[0]
    plane = (LR_BS // LANES, l_n + 16 * SEG_PAD, LANES)
    o = pl.pallas_call(
        functools.partial(_rglru_kernel, n_ctx=n_ctx),
        grid=(b_n, nb),
        in_specs=[pl.BlockSpec((None, l_n, LR_BS), lambda b, j: (b, 0, j)),
                  pl.BlockSpec((None, l_n, LR_BS), lambda b, j: (b, 0, nb + j)),
                  pl.BlockSpec((k_w, LR_BS), lambda b, j: (0, j)),
                  pl.BlockSpec((1, LR_BS), lambda b, j: (0, j)),
                  pl.BlockSpec((2, 2, None, LR_BS, LR_BS), lambda b, j: (0, 0, j, 0, 0)),
                  pl.BlockSpec((2, 2, LR_BS), lambda b, j: (0, 0, j)),
                  pl.BlockSpec((2, LR_BS), lambda b, j: (0, j))],
        out_specs=pl.BlockSpec((None, n_lat, LR_BS), lambda b, j: (b, 0, j)),
        out_shape=jax.ShapeDtypeStruct((b_n, n_lat, d), BF16),
        scratch_shapes=[pltpu.VMEM((2,) + plane, F32), pltpu.VMEM((2,) + plane, F32), pltpu.VMEM(plane, F32),
                        pltpu.VMEM((2,) + plane, F32), pltpu.VMEM((2,) + plane, F32)],
        compiler_params=_cparams(("parallel", "parallel"), 56),
        name="rglru",
    )(proj, proj, conv_w, conv_b.reshape(1, d), w_gate, b_gate, lam)
    return o


def _ffn_up_kernel(h_ref, wg_ref, wv_ref, cg_ref, cv_ref, bg_ref, bv_ref, o_ref, u_scr, *, n_ctx, tr):
    l_n, tf = o_ref.shape
    k_w = cg_ref.shape[0]
    half = (k_w - 1) // 2
    n_buf, pad = u_scr.shape[0], (u_scr.shape[1] - l_n) // 2
    rblk = tr * _pick(l_n // tr, (3, 4, 2, 1))
    row = lax.broadcasted_iota(jnp.int32, (tr, LANES), 0)
    for p in range(n_buf):
        u_scr[p, 0:pad, :] = jnp.zeros((pad, 2 * LANES), F32)
        u_scr[p, pad + l_n:, :] = jnp.zeros((pad, 2 * LANES), F32)

    def weights(s):
        cols = slice(s * LANES, (s + 1) * LANES)
        return jnp.concatenate([wg_ref[:, cols], wv_ref[:, cols]], axis=1).astype(BF16)

    def product(s, w, r0):
        u_scr[s % n_buf, pad + r0:pad + r0 + rblk, :] = jnp.dot(h_ref[r0:r0 + rblk, :], w,
                                                                preferred_element_type=F32)

    def finish(s, r0):
        p, cols = s % n_buf, slice(s * LANES, (s + 1) * LANES)

        def conv(lane0, w_ref, b_ref):
            acc = None
            for j in range(k_w):
                sh = j - half
                x = u_scr[p, pad + r0 + sh:pad + r0 + sh + tr, lane0:lane0 + LANES]
                if sh < 0 and r0 in (0, n_ctx):
                    x = jnp.where(row < -sh, 0.0, x)
                if sh > 0 and r0 + tr in (n_ctx, l_n):
                    x = jnp.where(row >= tr - sh, 0.0, x)
                t = x * w_ref[j:j + 1, cols]
                acc = t if acc is None else acc + t
            return b_ref[:, cols] + acc
        gate = conv(0, cg_ref, bg_ref)
        val = conv(LANES, cv_ref, bv_ref)
        o_ref[r0:r0 + tr, cols] = (gate * _sigmoid(gate) * val).astype(BF16)

    n_s = tf // LANES
    for s in range(n_s + 1):
        w = weights(s) if s < n_s else None
        for r0 in range(0, l_n, rblk):
            if s < n_s:
                product(s, w, r0)
            if s > 0:
                for r1 in range(r0, r0 + rblk, tr):
                    finish(s - 1, r1)


def _conv_ffn(h, n_ctx, w_up, layer, conv_w, conv_b, *, tr):
    b_n, l_n, d = h.shape
    f = w_up.shape[2] // 2
    tf = _pick(f, (512, 256, 128))
    nf = f // tf
    k_w = conv_w.shape[0]
    cb = conv_b.reshape(1, 2 * f)
    act = pl.pallas_call(
        functools.partial(_ffn_up_kernel, n_ctx=n_ctx, tr=tr),
        grid=(b_n, nf),
        in_specs=[pl.BlockSpec((None, l_n, d), lambda b, j: (b, 0, 0), pipeline_mode=pl.Buffered(1)),
                  pl.BlockSpec((None, d, tf), lambda b, j: (layer, 0, j)),
                  pl.BlockSpec((None, d, tf), lambda b, j: (layer, 0, nf + j)),
                  pl.BlockSpec((k_w, tf), lambda b, j: (0, j)),
                  pl.BlockSpec((k_w, tf), lambda b, j: (0, nf + j)),
                  pl.BlockSpec((1, tf), lambda b, j: (0, j)),
                  pl.BlockSpec((1, tf), lambda b, j: (0, nf + j))],
        out_specs=pl.BlockSpec((None, l_n, tf), lambda b, j: (b, 0, j)),
        out_shape=jax.ShapeDtypeStruct((b_n, l_n, f), BF16),
        scratch_shapes=[pltpu.VMEM((3, l_n + 16, 2 * LANES), F32)],
        compiler_params=_cparams(("parallel", "arbitrary"), 56),
        name="ffn_up",
    )(h, w_up, w_up, conv_w, conv_w, cb, cb)
    return act


def kernel(x, c, ctx, c_ctx, ada_w, ada_b, ln_g, ln_b, ffn_w_up, ffn_conv_w, ffn_conv_b, ffn_w_down, rw_mu, rw_w_rkv, rw_w0, rw_w1, rw_w2, rw_a0, rw_a1, rw_a2, rw_g1, rw_g2, rw_k_k, rw_k_a, rw_r_k, rw_gn_g, rw_gn_b, rw_w_o, da_w_qkv, da_lambda, da_sub_g, da_w_o, hg_w_in, hg_lower, hg_norm_g, hg_w_o, lr_w_in, lr_conv_w, lr_conv_b, lr_w_gate, lr_b_gate, lr_lambda, lr_w_o):
    b_n, n_lat, d = x.shape
    n_ctx = ctx.shape[1]
    depth = ada_w.shape[0]
    assert depth == 4 and rw_mu.shape[0] == 1, "one occurrence of each of the four mixers"
    assert b_n + 1 <= 8 and n_ctx % CHUNK == 0 and n_lat % CHUNK == 0
    tr = math.gcd(math.gcd(n_ctx, n_lat), 256)
    alpha = (2 * depth) ** 0.25

    c8 = jnp.concatenate([c, c_ctx[None], jnp.zeros((8 - b_n - 1, d), F32)], axis=0)
    m = _ada(c8, ada_w, ada_b)
    m_lat = m[:, :b_n].reshape(depth, b_n, 1, 6, d)
    m_ctx = jnp.broadcast_to(m[:, b_n].reshape(depth, 1, 1, 6, d), (depth, b_n, 1, 6, d))
    mod = jnp.concatenate([m_ctx, m_lat], axis=2)

    z = jnp.concatenate([ctx, x], axis=1)
    w_down_b = _to_bf16(ffn_w_down)
    h = None
    for i in range(depth):
        last = i == depth - 1
        if i == 0:
            o = _rwkv7_layer(z, mod[0], n_ctx, rw_mu[0], rw_w_rkv[0], rw_w0[0], rw_w1[0], rw_w2[0], rw_a0[0],
                             rw_a1[0], rw_a2[0], rw_g1[0], rw_g2[0], rw_k_k[0], rw_k_a[0], rw_r_k[0],
                             rw_gn_g[0], rw_gn_b[0])
            w_o = rw_w_o
        elif i == 1:
            o = _diff_attention_layer(h, n_ctx, i, da_w_qkv[0], da_lambda[0], da_sub_g[0], tr=tr)
            w_o = da_w_o
        elif i == 2:
            o = _hgrn2_layer(h, n_ctx, i, hg_w_in[0], hg_lower, hg_norm_g[0])
            w_o = hg_w_o
        else:
            o = _rglru_layer(h, n_ctx, lr_w_in[0], lr_conv_w[0], lr_conv_b[0], lr_w_gate[0], lr_b_gate[0],
                             lr_lambda[0])
            w_o = lr_w_o
        z, h = _proj_ln(o, w_o, 0, z, mod[i], ln_g[i, 0], ln_b[i, 0], mod[i], gate_j=2, mod_j=3, tr=tr,
                        n_ctx=n_ctx, alpha=alpha)
        if last:
            n_ctx = 0
        act = _conv_ffn(h, n_ctx, ffn_w_up, i, ffn_conv_w[i], ffn_conv_b[i], tr=tr)
        z, h = _proj_ln(act, w_down_b, i, z, mod[i], ln_g[i, 1], ln_b[i, 1], mod[min(i + 1, depth - 1)],
                        gate_j=5, mod_j=None if last else 0, tr=tr, n_ctx=n_ctx, alpha=alpha)
    return z
```
